```python
import jax, jax.numpy as jnp
from jax import lax
import numpy as np

D_MODEL = 1024
BATCH = 8
SEQ = 8192
DEPTH = 2

N_MEM = 256
POOL_WINDOWS = (2, 4, 8, 16)
POOL_GROUPS = 4
POOL_W = D_MODEL // 2
POOL_GW = POOL_W // POOL_GROUPS
LRU_W = D_MODEL
LRU_HEADS = 8
LRU_HD = LRU_W // LRU_HEADS
CONV_W = 4
LRU_C = 8.0
FOX_HEADS = 8
FOX_HD = 64
FOX_W = FOX_HEADS * FOX_HD
Q_BLOCK = 128
X_HEADS = 4
X_HD = D_MODEL // X_HEADS
D_FF = ((8 * D_MODEL // 3 + 127) // 128) * 128
N_BRANCH = 3
EPS = 1e-6
IN_SIZES = (POOL_W, LRU_W, LRU_W, FOX_W, FOX_W, FOX_W, FOX_HEADS, N_BRANCH * D_MODEL)
IN_W = sum(IN_SIZES)

kernel_name = "hybrid_pool_rglru_fox_macaron_block"


def rmsnorm(x, g):
    xf = x.astype(jnp.float32)
    y = xf * lax.rsqrt(jnp.mean(xf * xf, axis=-1, keepdims=True) + EPS)
    return (y * g.astype(jnp.float32)).astype(x.dtype)


def swiglu(h, w_in, w_out):
    a, b = jnp.split(h @ w_in, 2, axis=-1)
    return (jax.nn.silu(a) * b) @ w_out


def pool_mixer(xa, w_grp, scale):
    B, S, _ = xa.shape
    xf = xa.astype(jnp.float32)
    cs = jnp.pad(jnp.cumsum(xf, axis=1), ((0, 0), (1, 0), (0, 0)))
    pos = jnp.arange(1, S + 1, dtype=jnp.float32)
    outs = []
    for g, w in enumerate(POOL_WINDOWS):
        c = cs[:, :, g * POOL_GW:(g + 1) * POOL_GW]
        lo = jnp.pad(c[:, :S + 1 - w], ((0, 0), (w - 1, 0), (0, 0)))
        cnt = jnp.minimum(pos, float(w))[None, :, None]
        outs.append((c[:, 1:] - lo) / cnt)
    mean = jnp.concatenate(outs, axis=-1)
    d = (mean - xf).astype(xa.dtype).reshape(B, S, POOL_GROUPS, POOL_GW)
    y = jnp.einsum('bsgc,gcd->bsgd', d, w_grp).reshape(B, S, POOL_W)
    return y * scale


def causal_depthwise_conv(x, w, b):
    S = x.shape[1]
    xp = jnp.pad(x, ((0, 0), (CONV_W - 1, 0), (0, 0)))
    y = b
    for k in range(CONV_W):
        y = y + xp[:, k:k + S] * w[k]
    return y


def rglru(xb, w_a, b_a, w_x, b_x, lam):
    B, S, _ = xb.shape
    xh = xb.reshape(B, S, LRU_HEADS, LRU_HD)
    r = jax.nn.sigmoid(jnp.einsum('bshc,hcd->bshd', xh, w_a).reshape(B, S, LRU_W) + b_a)
    i = jax.nn.sigmoid(jnp.einsum('bshc,hcd->bshd', xh, w_x).reshape(B, S, LRU_W) + b_x)
    log_a = -LRU_C * r.astype(jnp.float32) * jax.nn.softplus(-lam.astype(jnp.float32))
    a = jnp.exp(log_a)
    mult = jnp.sqrt(-jnp.expm1(2.0 * log_a))
    u = mult * (i * xb).astype(jnp.float32)

    def combine(l, rr):
        a1, b1 = l
        a2, b2 = rr
        return a1 * a2, a2 * b1 + b2

    _, h = lax.associative_scan(combine, (a, u), axis=1)
    return h.astype(xb.dtype)


def forgetting_attention(q, k, v, logf):
    B, S, _ = q.shape
    nb = S // Q_BLOCK
    q = q.reshape(B, S, FOX_HEADS, FOX_HD).transpose(0, 2, 1, 3)
    k = k.reshape(B, S, FOX_HEADS, FOX_HD).transpose(0, 2, 1, 3)
    v = v.reshape(B, S, FOX_HEADS, FOX_HD).transpose(0, 2, 1, 3)
    c = jnp.cumsum(logf.astype(jnp.float32), axis=1).transpose(0, 2, 1)
    qb = q.reshape(B, FOX_HEADS, nb, Q_BLOCK, FOX_HD).transpose(2, 0, 1, 3, 4)
    cb = c.reshape(B, FOX_HEADS, nb, Q_BLOCK).transpose(2, 0, 1, 3)
    starts = jnp.arange(nb, dtype=jnp.int32) * Q_BLOCK
    kpos = jnp.arange(S, dtype=jnp.int32)
    scale = FOX_HD ** -0.5

    def block(args):
        qi, ci, start = args
        s = jnp.einsum('bhqd,bhkd->bhqk', qi, k).astype(jnp.float32) * scale
        s = s + ci[..., None] - c[:, :, None, :]
        qpos = start + jnp.arange(Q_BLOCK, dtype=jnp.int32)
        s = jnp.where(kpos[None, :] <= qpos[:, None], s, -jnp.inf)
        p = jax.nn.softmax(s, axis=-1).astype(v.dtype)
        return jnp.einsum('bhqk,bhkd->bhqd', p, v)

    o = lax.map(block, (qb, cb, starts))
    return o.transpose(1, 0, 3, 2, 4).reshape(B, S, FOX_W)


def memory_cross_attention(h, m, w_q, w_kv, w_o):
    B, S, _ = h.shape
    q = (h @ w_q).reshape(B, S, X_HEADS, X_HD)
    k, v = jnp.split(m @ w_kv, 2, axis=-1)
    k = k.reshape(B, -1, X_HEADS, X_HD)
    v = v.reshape(B, -1, X_HEADS, X_HD)
    s = jnp.einsum('bshd,bmhd->bhsm', q, k).astype(jnp.float32) * (X_HD ** -0.5)
    p = jax.nn.softmax(s, axis=-1).astype(v.dtype)
    o = jnp.einsum('bhsm,bmhd->bshd', p, v).reshape(B, S, D_MODEL)
    return o @ w_o


def _fwd_setup_inputs(seed: int = 0) -> dict:
    key = jax.random.key(seed)
    ks = iter(jax.random.split(key, 64))
    f32 = jnp.float32

    def dense(shape, fan_in):
        return jax.random.normal(next(ks), shape, f32) * (fan_in ** -0.5)

    def gain(shape):
        return 1.0 + 0.02 * jax.random.normal(next(ks), shape, f32)

    def bias(shape, s=0.01):
        return s * jax.random.normal(next(ks), shape, f32)

    L, D = DEPTH, D_MODEL
    a0 = jax.random.uniform(next(ks), (L, LRU_W), f32, 0.9, 0.999)
    s0 = a0 ** (1.0 / LRU_C)
    lru_lambda = jnp.log(s0) - jnp.log1p(-s0)
    return {
        "x": jax.random.normal(next(ks), (BATCH, SEQ, D), f32),
        "mem": jax.random.normal(next(ks), (BATCH, N_MEM, D), f32),
        "g_ffn1": gain((L, D)),
        "w_ffn1_in": dense((L, D, 2 * D_FF), D),
        "w_ffn1_out": dense((L, D_FF, D), D_FF),
        "g_mix": gain((L, D)),
        "w_in": dense((L, D, IN_W), D),
        "b_f": 2.0 + 0.5 * jax.random.normal(next(ks), (L, FOX_HEADS), f32),
        "b_gate": bias((L, N_BRANCH * D)),
        "w_pool": dense((L, POOL_GROUPS, POOL_GW, POOL_GW), POOL_GW),
        "pool_scale": gain((L, POOL_W)),
        "w_up_a": dense((L, POOL_W, D), POOL_W),
        "conv_w": dense((L, CONV_W, LRU_W), CONV_W),
        "conv_b": bias((L, LRU_W)),
        "w_rg_a": dense((L, LRU_HEADS, LRU_HD, LRU_HD), LRU_HD),
        "b_rg_a": bias((L, LRU_W)),
        "w_rg_x": dense((L, LRU_HEADS, LRU_HD, LRU_HD), LRU_HD),
        "b_rg_x": bias((L, LRU_W)),
        "lru_lambda": lru_lambda,
        "w_up_b": dense((L, LRU_W, D), LRU_W),
        "w_up_c": dense((L, FOX_W, D), FOX_W),
        "w_o": dense((L, D, D), D),
        "g_cross": gain((L, D)),
        "g_mem": gain((L, D)),
        "w_xq": dense((L, D, D), D),
        "w_xkv": dense((L, D, 2 * D), D),
        "w_xo": dense((L, D, D), D),
        "g_ffn2": gain((L, D)),
        "w_ffn2_in": dense((L, D, 2 * D_FF), D),
        "w_ffn2_out": dense((L, D_FF, D), D_FF),
        "g_final": gain((D,)),
    }


def _fwd_reference(x, mem, g_ffn1, w_ffn1_in, w_ffn1_out, g_mix, w_in, b_f, b_gate, w_pool, pool_scale,
              w_up_a, conv_w, conv_b, w_rg_a, b_rg_a, w_rg_x, b_rg_x, lru_lambda, w_up_b, w_up_c, w_o,
              g_cross, g_mem, w_xq, w_xkv, w_xo, g_ffn2, w_ffn2_in, w_ffn2_out, g_final):
    B, S, D = x.shape
    offs = []
    acc = 0
    for n in IN_SIZES[:-1]:
        acc += n
        offs.append(acc)
    for l in range(DEPTH):
        x = x + 0.5 * swiglu(rmsnorm(x, g_ffn1[l]), w_ffn1_in[l], w_ffn1_out[l])
        u = rmsnorm(x, g_mix[l])
        xa, xb, gb, q, k, v, fl, gl = jnp.split(u @ w_in[l], offs, axis=-1)
        y_a = pool_mixer(xa, w_pool[l], pool_scale[l]) @ w_up_a[l]
        xb = causal_depthwise_conv(xb, conv_w[l], conv_b[l])
        h_b = rglru(xb, w_rg_a[l], b_rg_a[l], w_rg_x[l], b_rg_x[l], lru_lambda[l])
        y_b = (h_b * jax.nn.gelu(gb)) @ w_up_b[l]
        logf = jax.nn.log_sigmoid((fl + b_f[l]).astype(jnp.float32))
        y_c = forgetting_attention(q, k, v, logf) @ w_up_c[l]
        g = jax.nn.sigmoid(gl + b_gate[l]).reshape(B, S, N_BRANCH, D)
        merged = g[:, :, 0] * y_a + g[:, :, 1] * y_b + g[:, :, 2] * y_c
        x = x + merged @ w_o[l]
        x = x + memory_cross_attention(rmsnorm(x, g_cross[l]), rmsnorm(mem, g_mem[l]),
                                       w_xq[l], w_xkv[l], w_xo[l])
        x = x + 0.5 * swiglu(rmsnorm(x, g_ffn2[l]), w_ffn2_in[l], w_ffn2_out[l])
    return rmsnorm(x, g_final)


import jax as _jax
import jax.numpy as _jnp

TWIN_FORMAT = 'train_step'
FWD_PARAMS = ['x', 'mem', 'g_ffn1', 'w_ffn1_in', 'w_ffn1_out', 'g_mix', 'w_in', 'b_f', 'b_gate', 'w_pool', 'pool_scale', 'w_up_a', 'conv_w', 'conv_b', 'w_rg_a', 'b_rg_a', 'w_rg_x', 'b_rg_x', 'lru_lambda', 'w_up_b', 'w_up_c', 'w_o', 'g_cross', 'g_mem', 'w_xq', 'w_xkv', 'w_xo', 'g_ffn2', 'w_ffn2_in', 'w_ffn2_out', 'g_final']
TWIN_WEIGHTS = ['g_ffn1', 'w_ffn1_in', 'w_ffn1_out', 'g_mix', 'w_in', 'b_f', 'b_gate', 'w_pool', 'pool_scale', 'w_up_a', 'conv_w', 'conv_b', 'w_rg_a', 'b_rg_a', 'w_rg_x', 'b_rg_x', 'lru_lambda', 'w_up_b', 'w_up_c', 'w_o', 'g_cross', 'g_mem', 'w_xq', 'w_xkv', 'w_xo', 'g_ffn2', 'w_ffn2_in', 'w_ffn2_out', 'g_final']
TWIN_DIFF_INPUT = 'x'
TWIN_INPUTS = ['x', 'mem', 'g_ffn1', 'w_ffn1_in', 'w_ffn1_out', 'g_mix', 'w_in', 'b_f', 'b_gate', 'w_pool', 'pool_scale', 'w_up_a', 'conv_w', 'conv_b', 'w_rg_a', 'b_rg_a', 'w_rg_x', 'b_rg_x', 'lru_lambda', 'w_up_b', 'w_up_c', 'w_o', 'g_cross', 'g_mem', 'w_xq', 'w_xkv', 'w_xo', 'g_ffn2', 'w_ffn2_in', 'w_ffn2_out', 'g_final', 'loss_target', 'm_g_ffn1', 'm_w_ffn1_in', 'm_w_ffn1_out', 'm_g_mix', 'm_w_in', 'm_b_f', 'm_b_gate', 'm_w_pool', 'm_pool_scale', 'm_w_up_a', 'm_conv_w', 'm_conv_b', 'm_w_rg_a', 'm_b_rg_a', 'm_w_rg_x', 'm_b_rg_x', 'm_lru_lambda', 'm_w_up_b', 'm_w_up_c', 'm_w_o', 'm_g_cross', 'm_g_mem', 'm_w_xq', 'm_w_xkv', 'm_w_xo', 'm_g_ffn2', 'm_w_ffn2_in', 'm_w_ffn2_out', 'm_g_final', 'v_g_ffn1', 'v_w_ffn1_in', 'v_w_ffn1_out', 'v_g_mix', 'v_w_in', 'v_b_f', 'v_b_gate', 'v_w_pool', 'v_pool_scale', 'v_w_up_a', 'v_conv_w', 'v_conv_b', 'v_w_rg_a', 'v_b_rg_a', 'v_w_rg_x', 'v_b_rg_x', 'v_lru_lambda', 'v_w_up_b', 'v_w_up_c', 'v_w_o', 'v_g_cross', 'v_g_mem', 'v_w_xq', 'v_w_xkv', 'v_w_xo', 'v_g_ffn2', 'v_w_ffn2_in', 'v_w_ffn2_out', 'v_g_final']
TWIN_OUTPUTS = ['loss', 'grad_x', 'grad_g_ffn1', 'grad_w_ffn1_in', 'grad_w_ffn1_out', 'grad_g_mix', 'grad_w_in', 'grad_b_f', 'grad_b_gate', 'grad_w_pool', 'grad_pool_scale', 'grad_w_up_a', 'grad_conv_w', 'grad_conv_b', 'grad_w_rg_a', 'grad_b_rg_a', 'grad_w_rg_x', 'grad_b_rg_x', 'grad_lru_lambda', 'grad_w_up_b', 'grad_w_up_c', 'grad_w_o', 'grad_g_cross', 'grad_g_mem', 'grad_w_xq', 'grad_w_xkv', 'grad_w_xo', 'grad_g_ffn2', 'grad_w_ffn2_in', 'grad_w_ffn2_out', 'grad_g_final', 'delta_g_ffn1', 'delta_w_ffn1_in', 'delta_w_ffn1_out', 'delta_g_mix', 'delta_w_in', 'delta_b_f', 'delta_b_gate', 'delta_w_pool', 'delta_pool_scale', 'delta_w_up_a', 'delta_conv_w', 'delta_conv_b', 'delta_w_rg_a', 'delta_b_rg_a', 'delta_w_rg_x', 'delta_b_rg_x', 'delta_lru_lambda', 'delta_w_up_b', 'delta_w_up_c', 'delta_w_o', 'delta_g_cross', 'delta_g_mem', 'delta_w_xq', 'delta_w_xkv', 'delta_w_xo', 'delta_g_ffn2', 'delta_w_ffn2_in', 'delta_w_ffn2_out', 'delta_g_final', 'new_m_g_ffn1', 'new_m_w_ffn1_in', 'new_m_w_ffn1_out', 'new_m_g_mix', 'new_m_w_in', 'new_m_b_f', 'new_m_b_gate', 'new_m_w_pool', 'new_m_pool_scale', 'new_m_w_up_a', 'new_m_conv_w', 'new_m_conv_b', 'new_m_w_rg_a', 'new_m_b_rg_a', 'new_m_w_rg_x', 'new_m_b_rg_x', 'new_m_lru_lambda', 'new_m_w_up_b', 'new_m_w_up_c', 'new_m_w_o', 'new_m_g_cross', 'new_m_g_mem', 'new_m_w_xq', 'new_m_w_xkv', 'new_m_w_xo', 'new_m_g_ffn2', 'new_m_w_ffn2_in', 'new_m_w_ffn2_out', 'new_m_g_final', 'new_v_g_ffn1', 'new_v_w_ffn1_in', 'new_v_w_ffn1_out', 'new_v_g_mix', 'new_v_w_in', 'new_v_b_f', 'new_v_b_gate', 'new_v_w_pool', 'new_v_pool_scale', 'new_v_w_up_a', 'new_v_conv_w', 'new_v_conv_b', 'new_v_w_rg_a', 'new_v_b_rg_a', 'new_v_w_rg_x', 'new_v_b_rg_x', 'new_v_lru_lambda', 'new_v_w_up_b', 'new_v_w_up_c', 'new_v_w_o', 'new_v_g_cross', 'new_v_g_mem', 'new_v_w_xq', 'new_v_w_xkv', 'new_v_w_xo', 'new_v_g_ffn2', 'new_v_w_ffn2_in', 'new_v_w_ffn2_out', 'new_v_g_final']
TWIN_LEAF_KINDS = {'loss': 'loss', 'grad_x': 'grad_x', 'grad_g_ffn1': 'grad_w', 'grad_w_ffn1_in': 'grad_w', 'grad_w_ffn1_out': 'grad_w', 'grad_g_mix': 'grad_w', 'grad_w_in': 'grad_w', 'grad_b_f': 'grad_w', 'grad_b_gate': 'grad_w', 'grad_w_pool': 'grad_w', 'grad_pool_scale': 'grad_w', 'grad_w_up_a': 'grad_w', 'grad_conv_w': 'grad_w', 'grad_conv_b': 'grad_w', 'grad_w_rg_a': 'grad_w', 'grad_b_rg_a': 'grad_w', 'grad_w_rg_x': 'grad_w', 'grad_b_rg_x': 'grad_w', 'grad_lru_lambda': 'grad_w', 'grad_w_up_b': 'grad_w', 'grad_w_up_c': 'grad_w', 'grad_w_o': 'grad_w', 'grad_g_cross': 'grad_w', 'grad_g_mem': 'grad_w', 'grad_w_xq': 'grad_w', 'grad_w_xkv': 'grad_w', 'grad_w_xo': 'grad_w', 'grad_g_ffn2': 'grad_w', 'grad_w_ffn2_in': 'grad_w', 'grad_w_ffn2_out': 'grad_w', 'grad_g_final': 'grad_w', 'delta_g_ffn1': 'delta_w', 'delta_w_ffn1_in': 'delta_w', 'delta_w_ffn1_out': 'delta_w', 'delta_g_mix': 'delta_w', 'delta_w_in': 'delta_w', 'delta_b_f': 'delta_w', 'delta_b_gate': 'delta_w', 'delta_w_pool': 'delta_w', 'delta_pool_scale': 'delta_w', 'delta_w_up_a': 'delta_w', 'delta_conv_w': 'delta_w', 'delta_conv_b': 'delta_w', 'delta_w_rg_a': 'delta_w', 'delta_b_rg_a': 'delta_w', 'delta_w_rg_x': 'delta_w', 'delta_b_rg_x': 'delta_w', 'delta_lru_lambda': 'delta_w', 'delta_w_up_b': 'delta_w', 'delta_w_up_c': 'delta_w', 'delta_w_o': 'delta_w', 'delta_g_cross': 'delta_w', 'delta_g_mem': 'delta_w', 'delta_w_xq': 'delta_w', 'delta_w_xkv': 'delta_w', 'delta_w_xo': 'delta_w', 'delta_g_ffn2': 'delta_w', 'delta_w_ffn2_in': 'delta_w', 'delta_w_ffn2_out': 'delta_w', 'delta_g_final': 'delta_w', 'new_m_g_ffn1': 'new_m', 'new_m_w_ffn1_in': 'new_m', 'new_m_w_ffn1_out': 'new_m', 'new_m_g_mix': 'new_m', 'new_m_w_in': 'new_m', 'new_m_b_f': 'new_m', 'new_m_b_gate': 'new_m', 'new_m_w_pool': 'new_m', 'new_m_pool_scale': 'new_m', 'new_m_w_up_a': 'new_m', 'new_m_conv_w': 'new_m', 'new_m_conv_b': 'new_m', 'new_m_w_rg_a': 'new_m', 'new_m_b_rg_a': 'new_m', 'new_m_w_rg_x': 'new_m', 'new_m_b_rg_x': 'new_m', 'new_m_lru_lambda': 'new_m', 'new_m_w_up_b': 'new_m', 'new_m_w_up_c': 'new_m', 'new_m_w_o': 'new_m', 'new_m_g_cross': 'new_m', 'new_m_g_mem': 'new_m', 'new_m_w_xq': 'new_m', 'new_m_w_xkv': 'new_m', 'new_m_w_xo': 'new_m', 'new_m_g_ffn2': 'new_m', 'new_m_w_ffn2_in': 'new_m', 'new_m_w_ffn2_out': 'new_m', 'new_m_g_final': 'new_m', 'new_v_g_ffn1': 'new_v', 'new_v_w_ffn1_in': 'new_v', 'new_v_w_ffn1_out': 'new_v', 'new_v_g_mix': 'new_v', 'new_v_w_in': 'new_v', 'new_v_b_f': 'new_v', 'new_v_b_gate': 'new_v', 'new_v_w_pool': 'new_v', 'new_v_pool_scale': 'new_v', 'new_v_w_up_a': 'new_v', 'new_v_conv_w': 'new_v', 'new_v_conv_b': 'new_v', 'new_v_w_rg_a': 'new_v', 'new_v_b_rg_a': 'new_v', 'new_v_w_rg_x': 'new_v', 'new_v_b_rg_x': 'new_v', 'new_v_lru_lambda': 'new_v', 'new_v_w_up_b': 'new_v', 'new_v_w_up_c': 'new_v', 'new_v_w_o': 'new_v', 'new_v_g_cross': 'new_v', 'new_v_g_mem': 'new_v', 'new_v_w_xq': 'new_v', 'new_v_w_xkv': 'new_v', 'new_v_w_xo': 'new_v', 'new_v_g_ffn2': 'new_v', 'new_v_w_ffn2_in': 'new_v', 'new_v_w_ffn2_out': 'new_v', 'new_v_g_final': 'new_v'}


def _forward(args):
    return _fwd_reference(*[args[k] for k in FWD_PARAMS])


def _output_shape():
    def fwd():
        inp = _fwd_setup_inputs(0)
        return _fwd_reference(*[inp[k] for k in FWD_PARAMS])
    out = _jax.eval_shape(fwd)
    return out.shape, out.dtype

N_MICROBATCH = 1
ADAM_LR = 0.001
ADAM_B1 = 0.9
ADAM_B2 = 0.999
ADAM_EPS = 1e-08
ADAM_WD = 0.01
ADAM_STEP = 10
PER_EXAMPLE_BATCH_AXIS = {'x': 0, 'mem': 0, 'loss_target': 0}
SHARED_INPUTS = []
_WEIGHT_DTYPES = {'g_ffn1': _jnp.float32, 'w_ffn1_in': _jnp.float32, 'w_ffn1_out': _jnp.float32, 'g_mix': _jnp.float32, 'w_in': _jnp.float32, 'b_f': _jnp.float32, 'b_gate': _jnp.float32, 'w_pool': _jnp.float32, 'pool_scale': _jnp.float32, 'w_up_a': _jnp.float32, 'conv_w': _jnp.float32, 'conv_b': _jnp.float32, 'w_rg_a': _jnp.float32, 'b_rg_a': _jnp.float32, 'w_rg_x': _jnp.float32, 'b_rg_x': _jnp.float32, 'lru_lambda': _jnp.float32, 'w_up_b': _jnp.float32, 'w_up_c': _jnp.float32, 'w_o': _jnp.float32, 'g_cross': _jnp.float32, 'g_mem': _jnp.float32, 'w_xq': _jnp.float32, 'w_xkv': _jnp.float32, 'w_xo': _jnp.float32, 'g_ffn2': _jnp.float32, 'w_ffn2_in': _jnp.float32, 'w_ffn2_out': _jnp.float32, 'g_final': _jnp.float32}
MOMENT_SCALE = {'g_ffn1': 1.090913e-01, 'w_ffn1_in': 4.605367e-02, 'w_ffn1_out': 7.528697e-02, 'g_mix': 1.510474e-01, 'w_in': 5.857344e-02, 'b_f': 7.086798e-01, 'b_gate': 2.902434e-02, 'w_pool': 1.422307e-01, 'pool_scale': 1.466842e-01, 'w_up_a': 1.016022e-01, 'conv_w': 5.739870e-02, 'conv_b': 6.443114e-01, 'w_rg_a': 1.883887e-02, 'b_rg_a': 1.697215e-02, 'w_rg_x': 3.397722e-02, 'b_rg_x': 2.100826e-02, 'lru_lambda': 3.252302e-02, 'w_up_b': 5.869041e-02, 'w_up_c': 5.225110e-02, 'w_o': 1.272279e-01, 'g_cross': 2.217172e-02, 'g_mem': 3.638749e-02, 'w_xq': 2.209551e-02, 'w_xkv': 2.507598e-02, 'w_xo': 2.755845e-02, 'g_ffn2': 8.698852e-02, 'w_ffn2_in': 3.671610e-02, 'w_ffn2_out': 5.987823e-02, 'g_final': 6.395359e+01}


def _to_microbatches(a, axis):
    t = _jnp.moveaxis(a, axis, 0)
    t = t.reshape((N_MICROBATCH, t.shape[0] // N_MICROBATCH) + t.shape[1:])
    return _jnp.moveaxis(t, 1, axis + 1)


def setup_inputs(seed: int = 0) -> dict:
    inp = _fwd_setup_inputs(seed)
    key = _jax.random.fold_in(_jax.random.key(seed), 7919)
    shape, _ = _output_shape()
    out = dict(inp)
    out["loss_target"] = _jax.random.normal(_jax.random.fold_in(key, 0), shape, _jnp.float32)
    for i, name in enumerate(TWIN_WEIGHTS):
        w = inp[name].astype(_jnp.float32)
        if MOMENT_SCALE is None:
            s = _jnp.sqrt(_jnp.mean(_jnp.square(w)) + 1e-30)
        else:
            s = MOMENT_SCALE[name]
        km, kv = _jax.random.split(_jax.random.fold_in(key, i + 1))
        out[name] = w
        out["m_" + name] = s * _jax.random.normal(km, w.shape, _jnp.float32)
        out["v_" + name] = (s * s) * _jax.random.uniform(kv, w.shape, _jnp.float32, 0.5, 1.5)
    if N_MICROBATCH > 1:
        for name, axis in PER_EXAMPLE_BATCH_AXIS.items():
            out[name] = _to_microbatches(out[name], axis)
    return {'x': out['x'], 'mem': out['mem'], 'g_ffn1': out['g_ffn1'], 'w_ffn1_in': out['w_ffn1_in'], 'w_ffn1_out': out['w_ffn1_out'], 'g_mix': out['g_mix'], 'w_in': out['w_in'], 'b_f': out['b_f'], 'b_gate': out['b_gate'], 'w_pool': out['w_pool'], 'pool_scale': out['pool_scale'], 'w_up_a': out['w_up_a'], 'conv_w': out['conv_w'], 'conv_b': out['conv_b'], 'w_rg_a': out['w_rg_a'], 'b_rg_a': out['b_rg_a'], 'w_rg_x': out['w_rg_x'], 'b_rg_x': out['b_rg_x'], 'lru_lambda': out['lru_lambda'], 'w_up_b': out['w_up_b'], 'w_up_c': out['w_up_c'], 'w_o': out['w_o'], 'g_cross': out['g_cross'], 'g_mem': out['g_mem'], 'w_xq': out['w_xq'], 'w_xkv': out['w_xkv'], 'w_xo': out['w_xo'], 'g_ffn2': out['g_ffn2'], 'w_ffn2_in': out['w_ffn2_in'], 'w_ffn2_out': out['w_ffn2_out'], 'g_final': out['g_final'], 'loss_target': out['loss_target'], 'm_g_ffn1': out['m_g_ffn1'], 'm_w_ffn1_in': out['m_w_ffn1_in'], 'm_w_ffn1_out': out['m_w_ffn1_out'], 'm_g_mix': out['m_g_mix'], 'm_w_in': out['m_w_in'], 'm_b_f': out['m_b_f'], 'm_b_gate': out['m_b_gate'], 'm_w_pool': out['m_w_pool'], 'm_pool_scale': out['m_pool_scale'], 'm_w_up_a': out['m_w_up_a'], 'm_conv_w': out['m_conv_w'], 'm_conv_b': out['m_conv_b'], 'm_w_rg_a': out['m_w_rg_a'], 'm_b_rg_a': out['m_b_rg_a'], 'm_w_rg_x': out['m_w_rg_x'], 'm_b_rg_x': out['m_b_rg_x'], 'm_lru_lambda': out['m_lru_lambda'], 'm_w_up_b': out['m_w_up_b'], 'm_w_up_c': out['m_w_up_c'], 'm_w_o': out['m_w_o'], 'm_g_cross': out['m_g_cross'], 'm_g_mem': out['m_g_mem'], 'm_w_xq': out['m_w_xq'], 'm_w_xkv': out['m_w_xkv'], 'm_w_xo': out['m_w_xo'], 'm_g_ffn2': out['m_g_ffn2'], 'm_w_ffn2_in': out['m_w_ffn2_in'], 'm_w_ffn2_out': out['m_w_ffn2_out'], 'm_g_final': out['m_g_final'], 'v_g_ffn1': out['v_g_ffn1'], 'v_w_ffn1_in': out['v_w_ffn1_in'], 'v_w_ffn1_out': out['v_w_ffn1_out'], 'v_g_mix': out['v_g_mix'], 'v_w_in': out['v_w_in'], 'v_b_f': out['v_b_f'], 'v_b_gate': out['v_b_gate'], 'v_w_pool': out['v_w_pool'], 'v_pool_scale': out['v_pool_scale'], 'v_w_up_a': out['v_w_up_a'], 'v_conv_w': out['v_conv_w'], 'v_conv_b': out['v_conv_b'], 'v_w_rg_a': out['v_w_rg_a'], 'v_b_rg_a': out['v_b_rg_a'], 'v_w_rg_x': out['v_w_rg_x'], 'v_b_rg_x': out['v_b_rg_x'], 'v_lru_lambda': out['v_lru_lambda'], 'v_w_up_b': out['v_w_up_b'], 'v_w_up_c': out['v_w_up_c'], 'v_w_o': out['v_w_o'], 'v_g_cross': out['v_g_cross'], 'v_g_mem': out['v_g_mem'], 'v_w_xq': out['v_w_xq'], 'v_w_xkv': out['v_w_xkv'], 'v_w_xo': out['v_w_xo'], 'v_g_ffn2': out['v_g_ffn2'], 'v_w_ffn2_in': out['v_w_ffn2_in'], 'v_w_ffn2_out': out['v_w_ffn2_out'], 'v_g_final': out['v_g_final']}


def _loss(weights, diff, rest, loss_target):
    with _jax.named_scope("forward"):
        args = {**rest, TWIN_DIFF_INPUT: diff, **{k: w.astype(_WEIGHT_DTYPES[k]) for k, w in weights.items()}}
        y = _forward(args)
    with _jax.named_scope("loss_head"):
        err = _jnp.square(y.astype(_jnp.float32) - loss_target)
        return 0.5 * _jnp.sum(_jnp.mean(err, axis=-1)) if err.ndim else 0.5 * err


def _adamw(w, g, m, v):
    m = ADAM_B1 * m + (1.0 - ADAM_B1) * g
    v = ADAM_B2 * v + (1.0 - ADAM_B2) * _jnp.square(g)
    m_hat = m / (1.0 - ADAM_B1 ** ADAM_STEP)
    v_hat = v / (1.0 - ADAM_B2 ** ADAM_STEP)
    delta = -ADAM_LR * (m_hat / (_jnp.sqrt(v_hat) + ADAM_EPS) + ADAM_WD * w)
    return delta, m, v


def reference(x, mem, g_ffn1, w_ffn1_in, w_ffn1_out, g_mix, w_in, b_f, b_gate, w_pool, pool_scale, w_up_a, conv_w, conv_b, w_rg_a, b_rg_a, w_rg_x, b_rg_x, lru_lambda, w_up_b, w_up_c, w_o, g_cross, g_mem, w_xq, w_xkv, w_xo, g_ffn2, w_ffn2_in, w_ffn2_out, g_final, loss_target, m_g_ffn1, m_w_ffn1_in, m_w_ffn1_out, m_g_mix, m_w_in, m_b_f, m_b_gate, m_w_pool, m_pool_scale, m_w_up_a, m_conv_w, m_conv_b, m_w_rg_a, m_b_rg_a, m_w_rg_x, m_b_rg_x, m_lru_lambda, m_w_up_b, m_w_up_c, m_w_o, m_g_cross, m_g_mem, m_w_xq, m_w_xkv, m_w_xo, m_g_ffn2, m_w_ffn2_in, m_w_ffn2_out, m_g_final, v_g_ffn1, v_w_ffn1_in, v_w_ffn1_out, v_g_mix, v_w_in, v_b_f, v_b_gate, v_w_pool, v_pool_scale, v_w_up_a, v_conv_w, v_conv_b, v_w_rg_a, v_b_rg_a, v_w_rg_x, v_b_rg_x, v_lru_lambda, v_w_up_b, v_w_up_c, v_w_o, v_g_cross, v_g_mem, v_w_xq, v_w_xkv, v_w_xo, v_g_ffn2, v_w_ffn2_in, v_w_ffn2_out, v_g_final):
    given = dict(x=x, mem=mem, g_ffn1=g_ffn1, w_ffn1_in=w_ffn1_in, w_ffn1_out=w_ffn1_out, g_mix=g_mix, w_in=w_in, b_f=b_f, b_gate=b_gate, w_pool=w_pool, pool_scale=pool_scale, w_up_a=w_up_a, conv_w=conv_w, conv_b=conv_b, w_rg_a=w_rg_a, b_rg_a=b_rg_a, w_rg_x=w_rg_x, b_rg_x=b_rg_x, lru_lambda=lru_lambda, w_up_b=w_up_b, w_up_c=w_up_c, w_o=w_o, g_cross=g_cross, g_mem=g_mem, w_xq=w_xq, w_xkv=w_xkv, w_xo=w_xo, g_ffn2=g_ffn2, w_ffn2_in=w_ffn2_in, w_ffn2_out=w_ffn2_out, g_final=g_final, loss_target=loss_target, m_g_ffn1=m_g_ffn1, m_w_ffn1_in=m_w_ffn1_in, m_w_ffn1_out=m_w_ffn1_out, m_g_mix=m_g_mix, m_w_in=m_w_in, m_b_f=m_b_f, m_b_gate=m_b_gate, m_w_pool=m_w_pool, m_pool_scale=m_pool_scale, m_w_up_a=m_w_up_a, m_conv_w=m_conv_w, m_conv_b=m_conv_b, m_w_rg_a=m_w_rg_a, m_b_rg_a=m_b_rg_a, m_w_rg_x=m_w_rg_x, m_b_rg_x=m_b_rg_x, m_lru_lambda=m_lru_lambda, m_w_up_b=m_w_up_b, m_w_up_c=m_w_up_c, m_w_o=m_w_o, m_g_cross=m_g_cross, m_g_mem=m_g_mem, m_w_xq=m_w_xq, m_w_xkv=m_w_xkv, m_w_xo=m_w_xo, m_g_ffn2=m_g_ffn2, m_w_ffn2_in=m_w_ffn2_in, m_w_ffn2_out=m_w_ffn2_out, m_g_final=m_g_final, v_g_ffn1=v_g_ffn1, v_w_ffn1_in=v_w_ffn1_in, v_w_ffn1_out=v_w_ffn1_out, v_g_mix=v_g_mix, v_w_in=v_w_in, v_b_f=v_b_f, v_b_gate=v_b_gate, v_w_pool=v_w_pool, v_pool_scale=v_pool_scale, v_w_up_a=v_w_up_a, v_conv_w=v_conv_w, v_conv_b=v_conv_b, v_w_rg_a=v_w_rg_a, v_b_rg_a=v_b_rg_a, v_w_rg_x=v_w_rg_x, v_b_rg_x=v_b_rg_x, v_lru_lambda=v_lru_lambda, v_w_up_b=v_w_up_b, v_w_up_c=v_w_up_c, v_w_o=v_w_o, v_g_cross=v_g_cross, v_g_mem=v_g_mem, v_w_xq=v_w_xq, v_w_xkv=v_w_xkv, v_w_xo=v_w_xo, v_g_ffn2=v_g_ffn2, v_w_ffn2_in=v_w_ffn2_in, v_w_ffn2_out=v_w_ffn2_out, v_g_final=v_g_final)
    weights = {n: given[n] for n in TWIN_WEIGHTS}
    shared = {n: given[n] for n in SHARED_INPUTS}
    per_example = {n: given[n] for n in ['x', 'mem']}
    grad_fn = _jax.value_and_grad(_loss, argnums=(0, 1))

    def one_microbatch(ex, loss_target):
        ex = dict(ex)
        diff = ex.pop(TWIN_DIFF_INPUT)
        return grad_fn(weights, diff, {**shared, **ex}, loss_target)

    if N_MICROBATCH == 1:
        loss, (grad_w, grad_x) = one_microbatch(per_example, given["loss_target"])
    else:
        def body(carry, xs):
            loss_sum, grad_sum = carry
            l_k, (gw_k, gx_k) = one_microbatch(xs[0], xs[1])
            with _jax.named_scope("update"):
                return (loss_sum + l_k, _jax.tree.map(_jnp.add, grad_sum, gw_k)), gx_k

        init = (_jnp.zeros((), _jnp.float32), _jax.tree.map(_jnp.zeros_like, weights))
        (loss, grad_w), grad_x = _jax.lax.scan(body, init, (per_example, given["loss_target"]))
    with _jax.named_scope("update"):
        delta_w, new_m, new_v = {}, {}, {}
        for n in TWIN_WEIGHTS:
            delta_w[n], new_m[n], new_v[n] = _adamw(weights[n], grad_w[n], given["m_" + n], given["v_" + n])
    return (loss, grad_x, *[grad_w[n] for n in TWIN_WEIGHTS], *[delta_w[n] for n in TWIN_WEIGHTS],
            *[new_m[n] for n in TWIN_WEIGHTS], *[new_v[n] for n in TWIN_WEIGHTS])
```

```python
import functools
import math

import jax
import jax.numpy as jnp
from jax import lax
from jax.experimental import pallas as pl
from jax.experimental.pallas import tpu as pltpu

F32 = jnp.float32
BF = jnp.bfloat16

D = 1024
DFF = 2816
DEPTH = 2
POOL_W = 512
POOL_WINDOWS = (2, 4, 8, 16)
LRU_HEADS = 8
LRU_C = 8.0
FOX_HEADS = 8
FOX_HD = 64
FOX_W = 512
X_HEADS = 4
X_HD = 256
EPS = 1e-6
LANES = 128
N_CHIPS = 4

ADAM_LR, ADAM_B1, ADAM_B2, ADAM_EPS, ADAM_WD, ADAM_STEP = 0.001, 0.9, 0.999, 1e-08, 0.01, 10

VMEM_LIMIT_BYTES = 56 * 2 ** 20

PK_XB, PK_GB, PK_GL, PK_XA, PK_Q, PK_K, PK_V = 0, 1024, 2048, 5120, 5632, 6144, 6656
PK_W = 7168

MESH = pl.DeviceIdType.MESH
ANY = pl.BlockSpec(memory_space=pl.ANY)


def _cparams(ngrid):
    return pltpu.CompilerParams(dimension_semantics=("arbitrary",) * ngrid, vmem_limit_bytes=VMEM_LIMIT_BYTES)


def _pick(n, cands):
    for c in cands:
        if n % c == 0:
            return c
    return n


def _iota(shape, dim):
    return lax.broadcasted_iota(jnp.int32, shape, dim)


def _sigmoid(x):
    return 1.0 / (1.0 + jnp.exp(-x))


def _softplus(z):
    return jnp.maximum(z, 0.0) + jnp.log1p(jnp.exp(-jnp.abs(z)))


def _expm1(x):
    small = jnp.abs(x) < 0.25
    xs = jnp.where(small, x, 0.0)
    poly = xs * (1.0 + xs * (1 / 2 + xs * (1 / 6 + xs * (1 / 24 + xs * (1 / 120 + xs * (1 / 720 + xs * (1 / 5040 + xs * (1 / 40320))))))))
    return jnp.where(small, poly, jnp.exp(x) - 1.0)


_GELU_K = math.sqrt(2.0 / math.pi)


def _gelu_and_grad(x):
    inner = _GELU_K * (x + 0.044715 * x * x * x)
    t = jnp.tanh(inner)
    g = 0.5 * x * (1.0 + t)
    dg = 0.5 * (1.0 + t) + 0.5 * x * (1.0 - t * t) * _GELU_K * (1.0 + 3 * 0.044715 * x * x)
    return g, dg


def _dot(a, b, dims):
    return lax.dot_general(a.astype(BF), b.astype(BF), (dims, ((), ())), preferred_element_type=F32)


def _dot_nn(a, b):
    return _dot(a, b, ((1,), (0,)))


def _dot_nt(a, b):
    return _dot(a, b, ((1,), (1,)))


def _dot_tn(a, b):
    return _dot(a, b, ((0,), (0,)))


def _colsum(x):
    return jnp.sum(x, axis=0, keepdims=True)


_TM = (1024, 1408, 512, 256, 128)
_TN = (1408, 1024, 512, 384, 256, 128)
_TK = (2432, 1408, 1024, 512, 256, 128)


def matmul(a, b, mode, *, out_dtype=None, alpha=1.0, res=None, name):
    out_dtype = BF if out_dtype is None else out_dtype
    if mode == "nn":
        (M, K), N = a.shape, b.shape[1]
    elif mode == "nt":
        (M, K), N = a.shape, b.shape[0]
    else:
        (K, M), N = a.shape, b.shape[1]
    tm, tn, tk = _pick(M, _TM), _pick(N, _TN), _pick(K, _TK)
    nk = K // tk
    dims = {"nn": ((1,), (0,)), "nt": ((1,), (1,)), "tn": ((0,), (0,))}[mode]
    a_spec = pl.BlockSpec((tk, tm), lambda i, j, k: (k, i)) if mode == "tn" else pl.BlockSpec((tm, tk), lambda i, j, k: (i, k))
    b_spec = pl.BlockSpec((tn, tk), lambda i, j, k: (j, k)) if mode == "nt" else pl.BlockSpec((tk, tn), lambda i, j, k: (k, j))
    o_spec = pl.BlockSpec((tm, tn), lambda i, j, k: (i, j))
    has_res = res is not None

    def body(*refs):
        a_ref, b_ref = refs[0], refs[1]
        r_ref = refs[2] if has_res else None
        o_ref, acc = refs[2 + has_res], refs[3 + has_res]
        k = pl.program_id(2)

        @pl.when(k == 0)
        def _():
            acc[...] = jnp.zeros_like(acc)

        acc[...] += _dot(a_ref[...], b_ref[...], dims)

        @pl.when(k == nk - 1)
        def _():
            r = acc[...]
            if alpha != 1.0:
                r = r * alpha
            if has_res:
                r = r + r_ref[...].astype(F32)
            o_ref[...] = r.astype(o_ref.dtype)

    ins = [a, b] + ([res] if has_res else [])
    specs = [a_spec, b_spec] + ([o_spec] if has_res else [])
    return pl.pallas_call(
        body, name=name, grid=(M // tm, N // tn, nk), in_specs=specs, out_specs=o_spec,
        out_shape=jax.ShapeDtypeStruct((M, N), out_dtype), scratch_shapes=[pltpu.VMEM((tm, tn), F32)],
        compiler_params=_cparams(3))(*ins)


class Rows:
    def __init__(self, arr, w=None, cb=0, halo=None, h=8):
        self.arr, self.w, self.cb, self.halo, self.h = arr, (arr.shape[1] if w is None else w), cb, halo, h


def rowwise(fn, rows, consts=(), outs=(), accs=(), scratch=(), *, ts, name, reverse=False):
    rows = [r if isinstance(r, Rows) else Rows(r) for r in rows]
    S = rows[0].arr.shape[0]
    n = S // ts
    tile = (lambda s: n - 1 - s) if reverse else (lambda s: s)

    def row_spec(r):
        if r.halo is None:
            return pl.BlockSpec((ts, r.w), lambda s: (tile(s), r.cb))
        per, last = ts // r.h, S // r.h - 1
        if r.halo == "prev":
            return pl.BlockSpec((r.h, r.w), lambda s: (jnp.maximum(tile(s) * per - 1, 0), r.cb))
        return pl.BlockSpec((r.h, r.w), lambda s: (jnp.minimum((tile(s) + 1) * per, last), r.cb))

    def whole(shape):
        nd = len(shape)
        return pl.BlockSpec(tuple(shape), lambda s: (0,) * nd)

    nr, nc, no, na = len(rows), len(consts), len(outs), len(accs)

    def body(*refs):
        in_refs, out_refs = refs[:nr + nc], refs[nr + nc:nr + nc + no]
        acc_refs, scr = refs[nr + nc + no:nr + nc + no + na], refs[nr + nc + no + na:]
        step = pl.program_id(0)
        o, inc = fn(step, tile(step), n, *[r[...] for r in in_refs], *scr)
        for ref, val in zip(out_refs, o):
            ref[...] = val.astype(ref.dtype)
        if na:
            @pl.when(step == 0)
            def _():
                for ref in acc_refs:
                    ref[...] = jnp.zeros_like(ref)
            for ref, val in zip(acc_refs, inc):
                ref[...] += val.astype(ref.dtype)

    out_shape = [jax.ShapeDtypeStruct((S, w), dt) for w, dt in outs] + [jax.ShapeDtypeStruct(tuple(s), dt) for s, dt in accs]
    out_specs = [pl.BlockSpec((ts, w), lambda s: (tile(s), 0)) for w, _ in outs] + [whole(s) for s, _ in accs]
    res = pl.pallas_call(
        body, name=name, grid=(n,),
        in_specs=[row_spec(r) for r in rows] + [whole(c.shape) for c in consts],
        out_specs=out_specs, out_shape=out_shape, scratch_shapes=list(scratch),
        compiler_params=_cparams(1))(*[r.arr for r in rows], *consts)
    return res[:no], res[no:]


def _rstd(x):
    return lax.rsqrt(jnp.mean(x * x, axis=-1, keepdims=True) + EPS)


def rms_fwd(x, g, *, name, ts=512):
    def fn(step, i, n, x, g):
        return ((x * _rstd(x)) * g,), ()
    return rowwise(fn, [x], [g], [(D, BF)], ts=min(ts, x.shape[0]), name=name)[0][0]


def rms_bwd(x, g, dh, dres, *, name, ts=512):
    has_res = dres is not None

    def fn(step, i, n, x, dh, *rest):
        g = rest[-1]
        r = _rstd(x)
        xhat = x * r
        dh = dh.astype(F32)
        gd = dh * g
        dx = r * (gd - xhat * jnp.mean(xhat * gd, axis=-1, keepdims=True))
        if has_res:
            dx = dx + rest[0]
        return (dx,), (_colsum(dh * xhat),)

    rows = [x, dh] + ([dres] if has_res else [])
    o, a = rowwise(fn, rows, [g], [(D, F32)], [((1, D), F32)], ts=min(ts, x.shape[0]), name=name)
    return o[0], a[0]


def ffn_fwd(x, g, w_in, w_out, tag):
    h = rms_fwd(x, g, name=tag + "_rms")
    ab = matmul(h, w_in, "nn", name=tag + "_in")

    def act_fn(step, i, n, ab):
        a, b = ab[:, :DFF].astype(F32), ab[:, DFF:].astype(F32)
        return ((a * _sigmoid(a)) * b,), ()

    act = rowwise(act_fn, [ab], outs=[(DFF, BF)], ts=256, name=tag + "_act")[0][0]
    y = matmul(act, w_out, "nn", out_dtype=F32, alpha=0.5, res=x, name=tag + "_out")
    return y, (x, h, ab, act)


def ffn_bwd(dy, saved, g, w_in, w_out, tag):
    x, h, ab, act = saved
    dact = matmul(dy, w_out, "nt", alpha=0.5, name=tag + "_dact")
    dw_out = matmul(act, dy, "tn", alpha=0.5, name=tag + "_dwout")

    def dab_fn(step, i, n, ab, dact):
        a, b, dact = ab[:, :DFF].astype(F32), ab[:, DFF:].astype(F32), dact.astype(F32)
        s = _sigmoid(a)
        da = dact * b * (s * (1.0 + a * (1.0 - s)))
        db = dact * (a * s)
        return (jnp.concatenate([da, db], axis=1),), ()

    dab = rowwise(dab_fn, [ab, dact], outs=[(2 * DFF, BF)], ts=256, name=tag + "_dab")[0][0]
    dw_in = matmul(h, dab, "tn", name=tag + "_dwin")
    dh = matmul(dab, w_in, "nt", name=tag + "_dh")
    dx, dg = rms_bwd(x, g, dh, dy, name=tag + "_drms")
    return dx, dg, dw_in, dw_out


POOL_TS = 256
POOL_HALO = 16


def pool_fwd(P, w_pool, pool_scale, tag):
    ts = POOL_TS

    def fn(step, i, n, xa, halo, w, scale):
        xa = xa.astype(F32)
        halo = halo.astype(F32) * jnp.where(i > 0, 1.0, 0.0)
        ext = jnp.concatenate([halo, xa], axis=0)
        pos = (i * ts + 1 + _iota((ts, LANES), 0)).astype(F32)
        ds, ys = [], []
        for gi, win in enumerate(POOL_WINDOWS):
            e = ext[:, gi * LANES:(gi + 1) * LANES]
            sh = 1
            while sh < win:
                e = e + pltpu.roll(e, sh, 0)
                sh *= 2
            mean = e[POOL_HALO:] / jnp.minimum(pos, float(win))
            d = mean - xa[:, gi * LANES:(gi + 1) * LANES]
            ds.append(d)
            ys.append(_dot_nn(d, w[gi]))
        d = jnp.concatenate(ds, axis=1)
        return (d, jnp.concatenate(ys, axis=1) * scale), ()

    xa = Rows(P, POOL_W, PK_XA // POOL_W)
    xa_prev = Rows(P, POOL_W, PK_XA // POOL_W, "prev", POOL_HALO)
    (d, ya0), _ = rowwise(fn, [xa, xa_prev], [w_pool, pool_scale], [(POOL_W, BF), (POOL_W, BF)], ts=ts, name=tag + "_pool")
    return d, ya0


def pool_bwd(dya0, d, w_pool, pool_scale, tag):
    ts = POOL_TS
    L = ts + POOL_HALO

    def fn(step, i, n, dya0, dya0_next, d, w, scale):
        d = d.astype(F32)
        dz = jnp.concatenate([dya0.astype(F32), dya0_next.astype(F32) * jnp.where(i < n - 1, 1.0, 0.0)], axis=0) * scale
        pos = (i * ts + 1 + _iota((L, LANES), 0)).astype(F32)
        dxa, dws, dsc = [], [], []
        for gi, win in enumerate(POOL_WINDOWS):
            sl = slice(gi * LANES, (gi + 1) * LANES)
            dzg = dz[:, sl]
            dd = _dot_nt(dzg, w[gi])
            e = dd / jnp.minimum(pos, float(win))
            sh = 1
            while sh < win:
                e = e + pltpu.roll(e, L - sh, 0)
                sh *= 2
            dxa.append(e[:ts] - dd[:ts])
            dws.append(_dot_tn(d[:, sl], dzg[:ts])[None])
            z = _dot_nn(d[:, sl], w[gi])
            dsc.append(_colsum(dya0[:, sl].astype(F32) * z))
        return (jnp.concatenate(dxa, axis=1),), (jnp.concatenate(dws, axis=0), jnp.concatenate(dsc, axis=1))

    (dxa,), (dw_pool, dscale) = rowwise(
        fn, [dya0, Rows(dya0, halo="next", h=POOL_HALO), d], [w_pool, pool_scale],
        [(POOL_W, BF)], [((4, LANES, LANES), F32), ((1, POOL_W), F32)], ts=ts, name=tag + "_dpool")
    return dxa, dw_pool, dscale


LRU_R = 256


def _scan_fwd(A, U):
    R = A.shape[0]
    row = _iota(A.shape, 0)
    d = 1
    while d < R:
        m = row >= d
        A_sh = jnp.where(m, pltpu.roll(A, d, 0), 1.0)
        U_sh = jnp.where(m, pltpu.roll(U, d, 0), 0.0)
        U = A * U_sh + U
        A = A * A_sh
        d *= 2
    return A, U


def _scan_bwd(B, X):
    R = B.shape[0]
    row = _iota(B.shape, 0)
    d = 1
    while d < R:
        m = row < R - d
        B_sh = jnp.where(m, pltpu.roll(B, R - d, 0), 1.0)
        X_sh = jnp.where(m, pltpu.roll(X, R - d, 0), 0.0)
        X = X + B * X_sh
        B = B * B_sh
        d *= 2
    return X


def _lru_gates(xc, wa, ba, wx, bx, lam):
    r = _sigmoid(_dot_nn(xc, wa) + ba)
    ig = _sigmoid(_dot_nn(xc, wx) + bx)
    sp = _softplus(-lam)
    log_a = -LRU_C * r * sp
    a = jnp.exp(log_a)
    mult = jnp.sqrt(-_expm1(2.0 * log_a))
    return r, ig, sp, a, mult


def _lru_specs(S, R, reverse):
    nch = S // R
    ch = (lambda j: nch - 1 - j) if reverse else (lambda j: j)
    per = R // 8

    def col(off):
        return pl.BlockSpec((R, LANES), lambda h, j: (ch(j), off + h))

    def prev(off):
        return pl.BlockSpec((8, LANES), lambda h, j: (jnp.maximum(ch(j) * per - 1, 0), off + h))

    vec = pl.BlockSpec((1, LANES), lambda h, j: (0, h))
    cw = pl.BlockSpec((4, LANES), lambda h, j: (0, h))
    wsq = pl.BlockSpec((None, LANES, LANES), lambda h, j: (h, 0, 0))
    return nch, ch, col, prev, vec, cw, wsq


def lru_fwd(P, conv_w, conv_b, w_a, b_a, w_x, b_x, lam, tag):
    S = P.shape[0]
    R = min(LRU_R, S)
    nch, ch, col, prev, vec, cw_spec, wsq = _lru_specs(S, R, False)

    def body(xb_ref, halo_ref, gb_ref, cw_ref, cb_ref, wa_ref, ba_ref, wx_ref, bx_ref, lam_ref, xc_ref, h_ref, yb_ref, carry):
        j = pl.program_id(1)
        xb = xb_ref[...].astype(F32)
        halo = halo_ref[...].astype(F32) * jnp.where(j > 0, 1.0, 0.0)
        ext = jnp.concatenate([halo, xb], axis=0)
        cw = cw_ref[...]
        xc = cb_ref[...]
        for k in range(4):
            e = ext if k == 3 else pltpu.roll(ext, 3 - k, 0)
            xc = xc + e[8:] * cw[k:k + 1]
        r, ig, sp, a, mult = _lru_gates(xc, wa_ref[...], ba_ref[...], wx_ref[...], bx_ref[...], lam_ref[...])
        u = mult * (ig * xc)
        cum_a, hloc = _scan_fwd(a, u)

        @pl.when(j == 0)
        def _():
            carry[...] = jnp.zeros_like(carry)

        hfull = hloc + cum_a * carry[0:1]
        carry[...] = jnp.broadcast_to(hfull[R - 1:R], carry.shape)
        gel, _ = _gelu_and_grad(gb_ref[...].astype(F32))
        xc_ref[...] = xc
        h_ref[...] = hfull
        yb_ref[...] = (hfull * gel).astype(yb_ref.dtype)

    out = pl.BlockSpec((R, LANES), lambda h, j: (j, h))
    return pl.pallas_call(
        body, name=tag + "_lru", grid=(LRU_HEADS, nch),
        in_specs=[col(PK_XB // LANES), prev(PK_XB // LANES), col(PK_GB // LANES), cw_spec, vec, wsq, vec, wsq, vec, vec],
        out_specs=[out, out, out],
        out_shape=[jax.ShapeDtypeStruct((S, D), F32), jax.ShapeDtypeStruct((S, D), F32), jax.ShapeDtypeStruct((S, D), BF)],
        scratch_shapes=[pltpu.VMEM((8, LANES), F32)], compiler_params=_cparams(2),
    )(P, P, P, conv_w, conv_b, w_a, b_a, w_x, b_x, lam)


def lru_bwd(dyb0, P, hh, xc, conv_w, w_a, b_a, w_x, b_x, lam, tag):
    S = P.shape[0]
    R = min(LRU_R, S)
    nch, ch, col, prev, vec, cw_spec, wsq = _lru_specs(S, R, True)

    def body(dyb_ref, gb_ref, h_ref, hprev_ref, xc_ref, xb_ref, xbprev_ref, cw_ref, wa_ref, ba_ref, wx_ref, bx_ref, lam_ref,
             dgb_ref, dxb_ref, dwa_ref, dba_ref, dwx_ref, dbx_ref, dlam_ref, dcw_ref, dcb_ref, gcarry, dxc_head):
        j = pl.program_id(1)
        jj = nch - 1 - j
        has_prev = jnp.where(jj > 0, 1.0, 0.0)
        row = _iota((R, LANES), 0)
        xc, hv, lam = xc_ref[...], h_ref[...], lam_ref[...]
        wa, wx = wa_ref[...], wx_ref[...]
        r, ig, sp, a, mult = _lru_gates(xc, wa, ba_ref[...], wx, bx_ref[...], lam)
        dyb = dyb_ref[...].astype(F32)
        gel, dgel = _gelu_and_grad(gb_ref[...].astype(F32))
        dgb_ref[...] = (dyb * hv * dgel).astype(dgb_ref.dtype)

        @pl.when(j == 0)
        def _():
            gcarry[...] = jnp.zeros_like(gcarry)
            dxc_head[...] = jnp.zeros_like(dxc_head)
            for ref in (dwa_ref, dba_ref, dwx_ref, dbx_ref, dlam_ref, dcw_ref, dcb_ref):
                ref[...] = jnp.zeros_like(ref)

        B = jnp.where(row < R - 1, pltpu.roll(a, R - 1, 0), 0.0)
        X = dyb * gel + jnp.where(row == R - 1, gcarry[0:1], 0.0)
        G = _scan_bwd(B, X)
        gcarry[...] = jnp.broadcast_to(a[0:1] * G[0:1], gcarry.shape)
        hprev = jnp.where(row == 0, hprev_ref[...][7:8] * has_prev, pltpu.roll(hv, 1, 0))
        da = G * hprev
        dmult = G * ig * xc
        dig = G * mult * xc
        dxc = G * mult * ig
        dlog_a = da * a - dmult * (a * a) / mult
        dzr = dlog_a * (-LRU_C * sp) * (r * (1.0 - r))
        dzi = dig * (ig * (1.0 - ig))
        dxc = dxc + _dot_nt(dzr, wa) + _dot_nt(dzi, wx)
        dwa_ref[...] += _dot_tn(xc, dzr)
        dwx_ref[...] += _dot_tn(xc, dzi)
        dba_ref[...] += _colsum(dzr)
        dbx_ref[...] += _colsum(dzi)
        dlam_ref[...] += _colsum(dlog_a * (-LRU_C * r)) * (-_sigmoid(-lam))
        dcb_ref[...] += _colsum(dxc)
        cw = cw_ref[...]
        ext = jnp.concatenate([dxc, dxc_head[...]], axis=0)
        dxb = dxc * cw[3:4]
        for k in range(3):
            dxb = dxb + pltpu.roll(ext, R + 8 - (3 - k), 0)[:R] * cw[k:k + 1]
        dxb_ref[...] = dxb.astype(dxb_ref.dtype)
        dxc_head[...] = dxc[0:8]
        extx = jnp.concatenate([xbprev_ref[...].astype(F32) * has_prev, xb_ref[...].astype(F32)], axis=0)
        incs = []
        for k in range(4):
            e = extx if k == 3 else pltpu.roll(extx, 3 - k, 0)
            incs.append(_colsum(dxc * e[8:]))
        dcw_ref[...] += jnp.concatenate(incs, axis=0)

    plain = pl.BlockSpec((R, LANES), lambda h, j: (ch(j), h))
    plain_prev = pl.BlockSpec((8, LANES), lambda h, j: (jnp.maximum(ch(j) * (R // 8) - 1, 0), h))
    return pl.pallas_call(
        body, name=tag + "_dlru", grid=(LRU_HEADS, nch),
        in_specs=[plain, col(PK_GB // LANES), plain, plain_prev, plain, col(PK_XB // LANES), prev(PK_XB // LANES),
                  cw_spec, wsq, vec, wsq, vec, vec],
        out_specs=[plain, plain, wsq, vec, wsq, vec, vec, cw_spec, vec],
        out_shape=[jax.ShapeDtypeStruct((S, D), BF), jax.ShapeDtypeStruct((S, D), BF),
                   jax.ShapeDtypeStruct((LRU_HEADS, LANES, LANES), F32), jax.ShapeDtypeStruct((1, D), F32),
                   jax.ShapeDtypeStruct((LRU_HEADS, LANES, LANES), F32), jax.ShapeDtypeStruct((1, D), F32),
                   jax.ShapeDtypeStruct((1, D), F32), jax.ShapeDtypeStruct((4, D), F32), jax.ShapeDtypeStruct((1, D), F32)],
        scratch_shapes=[pltpu.VMEM((8, LANES), F32), pltpu.VMEM((8, LANES), F32)], compiler_params=_cparams(2),
    )(dyb0, P, hh, hh, xc, P, P, conv_w, w_a, b_a, w_x, b_x, lam)


CUM_TS = 512
FLASH_T = 512
NEG = -1e30


def cumlogf_fwd(fl, b_f, tag):
    ts = min(CUM_TS, fl.shape[0])

    def fn(step, i, n, fl, bf, carry):
        x = -_softplus(-(fl + bf))
        row = _iota(x.shape, 0)
        d = 1
        while d < ts:
            x = x + jnp.where(row >= d, pltpu.roll(x, d, 0), 0.0)
            d *= 2

        @pl.when(step == 0)
        def _():
            carry[...] = jnp.zeros_like(carry)

        c = x + carry[0:1]
        carry[...] = jnp.broadcast_to(c[ts - 1:ts], carry.shape)
        return (c,), ()

    return rowwise(fn, [fl], [b_f], [(LANES, F32)], scratch=[pltpu.VMEM((8, LANES), F32)], ts=ts, name=tag + "_cum")[0][0]


def cumlogf_bwd(dc, fl, b_f, tag):
    ts = min(CUM_TS, fl.shape[0])

    def fn(step, i, n, dc, fl, bf, carry):
        row = _iota(dc.shape, 0)
        x = dc
        d = 1
        while d < ts:
            x = x + jnp.where(row < ts - d, pltpu.roll(x, ts - d, 0), 0.0)
            d *= 2

        @pl.when(step == 0)
        def _():
            carry[...] = jnp.zeros_like(carry)

        g = x + carry[0:1]
        carry[...] = jnp.broadcast_to(g[0:1], carry.shape)
        dfl = g * _sigmoid(-(fl + bf))
        return (dfl,), (_colsum(dfl),)

    (dfl,), (db,) = rowwise(fn, [dc, fl], [b_f], [(LANES, F32)], [((1, LANES), F32)], scratch=[pltpu.VMEM((8, LANES), F32)],
                            ts=ts, name=tag + "_dcum", reverse=True)
    return dfl, db


def _fox_scores(q, k, ccol, crow, qi, ki, T):
    s = _dot_nt(q, k) * (FOX_HD ** -0.5) + ccol - crow
    rowp = qi * T + _iota((T, T), 0)
    colp = ki * T + _iota((T, T), 1)
    return jnp.where(colp <= rowp, s, NEG)


def flash_fwd(q, k, v, ccol, crow, tag):
    H, S, hd = q.shape
    T = min(FLASH_T, S)
    nb = S // T

    def body(q_ref, k_ref, v_ref, cc_ref, cr_ref, o_ref, lse_ref, m_s, l_s, acc):
        qi, ki = pl.program_id(1), pl.program_id(2)

        @pl.when(ki == 0)
        def _():
            m_s[...] = jnp.full_like(m_s, NEG)
            l_s[...] = jnp.zeros_like(l_s)
            acc[...] = jnp.zeros_like(acc)

        @pl.when(ki <= qi)
        def _():
            s = _fox_scores(q_ref[...], k_ref[...], cc_ref[...], cr_ref[...], qi, ki, T)
            m_new = jnp.maximum(m_s[...], jnp.max(s, axis=1, keepdims=True))
            alpha = jnp.exp(m_s[...] - m_new)
            p = jnp.exp(s - m_new)
            l_s[...] = alpha * l_s[...] + jnp.sum(p, axis=1, keepdims=True)
            p_hi = p.astype(BF)
            p_lo = p - p_hi.astype(F32)
            acc[...] = alpha * acc[...] + (_dot_nn(p_hi, v_ref[...]) + _dot_nn(p_lo, v_ref[...]))
            m_s[...] = m_new

        @pl.when(ki == qi)
        def _():
            o_ref[...] = (acc[...] / l_s[...]).astype(o_ref.dtype)
            lse_ref[...] = m_s[...] + jnp.log(l_s[...])

    qs = pl.BlockSpec((None, T, hd), lambda h, qi, ki: (h, qi, 0))
    ks = pl.BlockSpec((None, T, hd), lambda h, qi, ki: (h, jnp.minimum(ki, qi), 0))
    cc = pl.BlockSpec((None, T, 1), lambda h, qi, ki: (h, qi, 0))
    cr = pl.BlockSpec((None, 1, T), lambda h, qi, ki: (h, 0, jnp.minimum(ki, qi)))
    return pl.pallas_call(
        body, name=tag + "_flash", grid=(H, nb, nb), in_specs=[qs, ks, ks, cc, cr], out_specs=[qs, cc],
        out_shape=[jax.ShapeDtypeStruct((H, S, hd), F32), jax.ShapeDtypeStruct((H, S, 1), F32)],
        scratch_shapes=[pltpu.VMEM((T, 1), F32), pltpu.VMEM((T, 1), F32), pltpu.VMEM((T, hd), F32)],
        compiler_params=_cparams(3))(q, k, v, ccol, crow)


def flash_bwd(q, k, v, o, do, lse, ccol, crow, tag):
    H, S, hd = q.shape
    T = min(FLASH_T, S)
    nb = S // T

    def body(q_ref, k_ref, v_ref, o_ref, do_ref, lse_ref, cc_ref, cr_ref, dq_ref, dk_ref, dv_ref, dcr_ref, dk_acc, dv_acc, dc_acc):
        ki, qi = pl.program_id(1), pl.program_id(2)

        @pl.when(jnp.logical_and(ki == 0, qi == 0))
        def _():
            dq_ref[...] = jnp.zeros_like(dq_ref)

        @pl.when(qi == 0)
        def _():
            dk_acc[...] = jnp.zeros_like(dk_acc)
            dv_acc[...] = jnp.zeros_like(dv_acc)
            dc_acc[...] = jnp.zeros_like(dc_acc)

        @pl.when(qi >= ki)
        def _():
            qv, kv, do = q_ref[...], k_ref[...], do_ref[...]
            s = _fox_scores(qv, kv, cc_ref[...], cr_ref[...], qi, ki, T)
            p = jnp.exp(s - lse_ref[...])
            dv_acc[...] += _dot_tn(p, do)
            dp = _dot_nt(do, v_ref[...])
            delta = jnp.sum(do.astype(F32) * o_ref[...].astype(F32), axis=1, keepdims=True)
            ds = p * (dp - delta)
            dc_acc[...] -= _colsum(ds)
            dsb = (ds * (FOX_HD ** -0.5)).astype(BF)
            dk_acc[...] += _dot_tn(dsb, qv)
            rows = pl.ds(pl.multiple_of(qi * T, T), T)
            dq_ref[rows, :] += _dot_nn(dsb, kv)

        @pl.when(qi == nb - 1)
        def _():
            dk_ref[...] = dk_acc[...]
            dv_ref[...] = dv_acc[...]
            dcr_ref[...] = dc_acc[...]

    qs = pl.BlockSpec((None, T, hd), lambda h, ki, qi: (h, jnp.maximum(qi, ki), 0))
    ks = pl.BlockSpec((None, T, hd), lambda h, ki, qi: (h, ki, 0))
    cc = pl.BlockSpec((None, T, 1), lambda h, ki, qi: (h, jnp.maximum(qi, ki), 0))
    cr = pl.BlockSpec((None, 1, T), lambda h, ki, qi: (h, 0, ki))
    dqs = pl.BlockSpec((None, S, hd), lambda h, ki, qi: (h, 0, 0))
    return pl.pallas_call(
        body, name=tag + "_dflash", grid=(H, nb, nb), in_specs=[qs, ks, ks, qs, qs, cc, cc, cr], out_specs=[dqs, ks, ks, cr],
        out_shape=[jax.ShapeDtypeStruct((H, S, hd), F32), jax.ShapeDtypeStruct((H, S, hd), F32),
                   jax.ShapeDtypeStruct((H, S, hd), F32), jax.ShapeDtypeStruct((H, 1, S), F32)],
        scratch_shapes=[pltpu.VMEM((T, hd), F32), pltpu.VMEM((T, hd), F32), pltpu.VMEM((1, T), F32)],
        compiler_params=_cparams(3))(q, k, v, o, do, lse, ccol, crow)


def _to_heads(x2d):
    S = x2d.shape[0]
    return x2d.reshape(S, FOX_HEADS, FOX_HD).transpose(1, 0, 2)


def _from_heads(x3d):
    S = x3d.shape[1]
    return x3d.transpose(1, 0, 2).reshape(S, FOX_W)


def _gl_rows(P):
    return [Rows(P, D, PK_GL // D + k) for k in range(3)]


def merge_fwd(P, ya, yb, yc, b_gate, tag):
    def fn(step, i, n, g0, g1, g2, ya, yb, yc, b):
        out = 0.0
        for k, (gl, y) in enumerate(((g0, ya), (g1, yb), (g2, yc))):
            out = out + _sigmoid(gl.astype(F32) + b[:, k * D:(k + 1) * D]) * y.astype(F32)
        return (out,), ()
    return rowwise(fn, _gl_rows(P) + [ya, yb, yc], [b_gate], [(D, BF)], ts=512, name=tag + "_merge")[0][0]


def merge_bwd(dm, P, ya, yb, yc, b_gate, tag):
    def fn(step, i, n, dm, g0, g1, g2, ya, yb, yc, b):
        dm = dm.astype(F32)
        dys, dgls = [], []
        for k, (gl, y) in enumerate(((g0, ya), (g1, yb), (g2, yc))):
            g = _sigmoid(gl.astype(F32) + b[:, k * D:(k + 1) * D])
            dys.append(dm * g)
            dgls.append(dm * y.astype(F32) * (g * (1.0 - g)))
        dgl = jnp.concatenate(dgls, axis=1)
        return (dys[0], dys[1], dys[2], dgl), (_colsum(dgl),)
    (dya, dyb, dyc, dgl), (db,) = rowwise(
        fn, [dm] + _gl_rows(P) + [ya, yb, yc], [b_gate], [(D, BF), (D, BF), (D, BF), (3 * D, BF)], [((1, 3 * D), F32)],
        ts=256, name=tag + "_dmerge")
    return dya, dyb, dyc, dgl, db


def _xattn_probs(qh, kh):
    s = _dot_nt(qh, kh) * (X_HD ** -0.5)
    e = jnp.exp(s - jnp.max(s, axis=1, keepdims=True))
    return e / jnp.sum(e, axis=1, keepdims=True)


def xattn_fwd(q, kv, tag):
    def fn(step, i, n, q, k, v):
        os = []
        for h in range(X_HEADS):
            sl = slice(h * X_HD, (h + 1) * X_HD)
            os.append(_dot_nn(_xattn_probs(q[:, sl], k[:, sl]), v[:, sl]))
        return (jnp.concatenate(os, axis=1),), ()

    return _xattn_call(fn, [q], kv, [(D, BF)], [], tag + "_xattn")[0][0]


def _xattn_call(fn, rows, kv, outs, accs, name):
    S, ts, M = rows[0].shape[0], 512, kv.shape[0]
    ts = min(ts, S)
    nr, no, na = len(rows), len(outs), len(accs)

    def body(*refs):
        in_refs, out_refs, acc_refs = refs[:nr + 2], refs[nr + 2:nr + 2 + no], refs[nr + 2 + no:]
        step = pl.program_id(0)
        o, inc = fn(step, step, S // ts, *[r[...] for r in in_refs])
        for ref, val in zip(out_refs, o):
            ref[...] = val.astype(ref.dtype)
        if na:
            @pl.when(step == 0)
            def _():
                for ref in acc_refs:
                    ref[...] = jnp.zeros_like(ref)
            for ref, val in zip(acc_refs, inc):
                ref[...] += val

    row = pl.BlockSpec((ts, D), lambda s: (s, 0))
    res = pl.pallas_call(
        body, name=name, grid=(S // ts,),
        in_specs=[row] * nr + [pl.BlockSpec((M, D), lambda s: (0, 0)), pl.BlockSpec((M, D), lambda s: (0, 1))],
        out_specs=[row] * no + [pl.BlockSpec(tuple(s), lambda s_: (0, 0)) for s, _ in accs],
        out_shape=[jax.ShapeDtypeStruct((S, w), dt) for w, dt in outs] + [jax.ShapeDtypeStruct(tuple(s), dt) for s, dt in accs],
        compiler_params=_cparams(1))(*rows, kv, kv)
    return res[:no], res[no:]


def xattn_bwd(q, kv, do, tag):
    def fn(step, i, n, q, do, k, v):
        dqs, dks, dvs = [], [], []
        for h in range(X_HEADS):
            sl = slice(h * X_HD, (h + 1) * X_HD)
            p = _xattn_probs(q[:, sl], k[:, sl])
            dp = _dot_nt(do[:, sl], v[:, sl])
            ds = (p * (dp - jnp.sum(p * dp, axis=1, keepdims=True)) * (X_HD ** -0.5)).astype(BF)
            dqs.append(_dot_nn(ds, k[:, sl]))
            dks.append(_dot_tn(ds, q[:, sl]))
            dvs.append(_dot_tn(p, do[:, sl]))
        return (jnp.concatenate(dqs, axis=1),), (jnp.concatenate(dks + dvs, axis=1),)

    (dq,), (dkv,) = _xattn_call(fn, [q, do], kv, [(D, BF)], [((kv.shape[0], 2 * D), F32)], tag + "_dxattn")
    return dq, dkv


def layer_fwd(x, mem, w, tag):
    x1, s_ffn1 = ffn_fwd(x, w["g_ffn1"], w["ffn1_in"], w["ffn1_out"], tag + "_ffn1")
    u = rms_fwd(x1, w["g_mix"], name=tag + "_mixrms")
    P = matmul(u, w["wp"], "nn", name=tag + "_proj")
    fl = matmul(u, w["wfl"], "nn", out_dtype=F32, name=tag + "_projf")
    d, ya0 = pool_fwd(P, w["w_pool"], w["pool_scale"], tag)
    ya = matmul(ya0, w["up_a"], "nn", name=tag + "_upa")
    xc, hh, yb0 = lru_fwd(P, w["conv_w"], w["conv_b"], w["w_rg_a"], w["b_rg_a"], w["w_rg_x"], w["b_rg_x"], w["lam"], tag)
    yb = matmul(yb0, w["up_b"], "nn", name=tag + "_upb")
    c = cumlogf_fwd(fl, w["b_f"], tag)
    ch = c[:, :FOX_HEADS].T
    ccol, crow = ch[:, :, None], ch[:, None, :]
    q, k, v = (_to_heads(P[:, o:o + FOX_W]) for o in (PK_Q, PK_K, PK_V))
    o, lse = flash_fwd(q, k, v, ccol, crow, tag)
    o2 = _from_heads(o)
    yc = matmul(o2, w["up_c"], "nn", name=tag + "_upc")
    merged = merge_fwd(P, ya, yb, yc, w["b_gate"], tag)
    x2 = matmul(merged, w["w_o"], "nn", out_dtype=F32, res=x1, name=tag + "_wo")
    hq = rms_fwd(x2, w["g_cross"], name=tag + "_xrms")
    qx = matmul(hq, w["xq"], "nn", name=tag + "_xq")
    mn = rms_fwd(mem, w["g_mem"], name=tag + "_mrms")
    kv = matmul(mn, w["xkv"], "nn", name=tag + "_xkv")
    ox = xattn_fwd(qx, kv, tag)
    x3 = matmul(ox, w["xo"], "nn", out_dtype=F32, res=x2, name=tag + "_xo")
    x4, s_ffn2 = ffn_fwd(x3, w["g_ffn2"], w["ffn2_in"], w["ffn2_out"], tag + "_ffn2")
    saved = dict(ffn1=s_ffn1, ffn2=s_ffn2, x1=x1, u=u, P=P, fl=fl, d=d, ya0=ya0, ya=ya, xc=xc, hh=hh, yb0=yb0, yb=yb,
                 ccol=ccol, crow=crow, q=q, k=k, v=v, o=o, lse=lse, o2=o2, yc=yc, merged=merged, x2=x2, hq=hq, qx=qx,
                 mn=mn, kv=kv, ox=ox)
    return x4, saved


def layer_bwd(dx4, mem, w, s, tag):
    g = {}
    dx3, g["g_ffn2"], g["w_ffn2_in"], g["w_ffn2_out"] = ffn_bwd(dx4, s["ffn2"], w["g_ffn2"], w["ffn2_in"], w["ffn2_out"], tag + "_ffn2")
    dox = matmul(dx3, w["xo"], "nt", name=tag + "_dox")
    g["w_xo"] = matmul(s["ox"], dx3, "tn", name=tag + "_dwxo")
    dqx, dkv = xattn_bwd(s["qx"], s["kv"], dox, tag)
    g["w_xq"] = matmul(s["hq"], dqx, "tn", name=tag + "_dwxq")
    dhq = matmul(dqx, w["xq"], "nt", name=tag + "_dhq")
    dx2, g["g_cross"] = rms_bwd(s["x2"], w["g_cross"], dhq, dx3, name=tag + "_dxrms")
    g["w_xkv"] = matmul(s["mn"], dkv, "tn", name=tag + "_dwxkv")
    dmn = matmul(dkv, w["xkv"], "nt", name=tag + "_dmn")
    _, g["g_mem"] = rms_bwd(mem, w["g_mem"], dmn, None, name=tag + "_dmrms")
    P = s["P"]
    dmerged = matmul(dx2, w["w_o"], "nt", name=tag + "_dmerged")
    g["w_o"] = matmul(s["merged"], dx2, "tn", name=tag + "_dwo")
    dya, dyb, dyc, dgl, g["b_gate"] = merge_bwd(dmerged, P, s["ya"], s["yb"], s["yc"], w["b_gate"], tag)
    dya0 = matmul(dya, w["up_a"], "nt", name=tag + "_dya0")
    g["w_up_a"] = matmul(s["ya0"], dya, "tn", name=tag + "_dwupa")
    dxa, g["w_pool"], g["pool_scale"] = pool_bwd(dya0, s["d"], w["w_pool"], w["pool_scale"], tag)
    dyb0 = matmul(dyb, w["up_b"], "nt", name=tag + "_dyb0")
    g["w_up_b"] = matmul(s["yb0"], dyb, "tn", name=tag + "_dwupb")
    (dgb, dxb, g["w_rg_a"], g["b_rg_a"], g["w_rg_x"], g["b_rg_x"], g["lru_lambda"], g["conv_w"], g["conv_b"]) = lru_bwd(
        dyb0, P, s["hh"], s["xc"], w["conv_w"], w["w_rg_a"], w["b_rg_a"], w["w_rg_x"], w["b_rg_x"], w["lam"], tag)
    do2 = matmul(dyc, w["up_c"], "nt", name=tag + "_do2")
    g["w_up_c"] = matmul(s["o2"], dyc, "tn", name=tag + "_dwupc")
    dq, dk, dv, dcrow = flash_bwd(s["q"], s["k"], s["v"], s["o"], _to_heads(do2), s["lse"], s["ccol"], s["crow"], tag)
    dc = jnp.pad(dcrow[:, 0, :].T, ((0, 0), (0, LANES - FOX_HEADS)))
    dfl, g["b_f"] = cumlogf_bwd(dc, s["fl"], w["b_f"], tag)
    dP = jnp.concatenate([dxb, dgb, dgl, dxa] + [_from_heads(t).astype(BF) for t in (dq, dk, dv)], axis=1)
    du = matmul(dP, w["wp"], "nt", out_dtype=F32, name=tag + "_du")
    du = matmul(dfl, w["wfl"], "nt", out_dtype=F32, res=du, name=tag + "_duf")
    g["wp"] = matmul(s["u"], dP, "tn", name=tag + "_dwp")
    g["wfl"] = matmul(s["u"], dfl, "tn", out_dtype=F32, name=tag + "_dwfl")
    dx1, g["g_mix"] = rms_bwd(s["x1"], w["g_mix"], du, dx2, name=tag + "_dmixrms")
    dx0, g["g_ffn1"], g["w_ffn1_in"], g["w_ffn1_out"] = ffn_bwd(dx1, s["ffn1"], w["g_ffn1"], w["ffn1_in"], w["ffn1_out"], tag + "_ffn1")
    return dx0, g


def loss_head(x, target, g_final):
    def fn(step, i, n, x, t, g):
        r = _rstd(x)
        xhat = x * r
        e = xhat * g - t
        dy = e * (1.0 / D)
        gd = dy * g
        dx = r * (gd - xhat * jnp.mean(xhat * gd, axis=-1, keepdims=True))
        loss = 0.5 * jnp.sum(jnp.mean(e * e, axis=-1, keepdims=True), axis=0, keepdims=True)
        return (dx,), (jnp.broadcast_to(loss, (1, LANES)), _colsum(dy * xhat))

    (dx,), (loss, dg) = rowwise(fn, [x, target], [g_final], [(D, F32)], [((1, LANES), F32), ((1, D), F32)], ts=512, name="loss_head")
    return loss[0, 0], dx, dg


BIG = (("w_ffn1_in", (D, 2 * DFF), 1), ("w_ffn1_out", (DFF, D), 0), ("w_in", (D, 7176), 1), ("w_up_a", (POOL_W, D), 1),
       ("conv_w", (4, D), 1), ("w_up_b", (D, D), 0), ("w_up_c", (FOX_W, D), 1), ("w_o", (D, D), 0), ("w_xq", (D, D), 0),
       ("w_xkv", (D, 2 * D), 1), ("w_xo", (D, D), 0), ("w_ffn2_in", (D, 2 * DFF), 1), ("w_ffn2_out", (DFF, D), 0))
SMALL = (("g_ffn1", (D,)), ("g_mix", (D,)), ("b_f", (FOX_HEADS,)), ("b_gate", (3 * D,)), ("w_pool", (4, LANES, LANES)),
         ("pool_scale", (POOL_W,)), ("conv_b", (D,)), ("w_rg_a", (LRU_HEADS, LANES, LANES)), ("b_rg_a", (D,)),
         ("w_rg_x", (LRU_HEADS, LANES, LANES)), ("b_rg_x", (D,)), ("lru_lambda", (D,)), ("g_cross", (D,)), ("g_mem", (D,)),
         ("g_ffn2", (D,)))
ORDER = ("g_ffn1", "w_ffn1_in", "w_ffn1_out", "g_mix", "w_in", "b_f", "b_gate", "w_pool", "pool_scale", "w_up_a", "conv_w",
         "conv_b", "w_rg_a", "b_rg_a", "w_rg_x", "b_rg_x", "lru_lambda", "w_up_b", "w_up_c", "w_o", "g_cross", "g_mem", "w_xq",
         "w_xkv", "w_xo", "g_ffn2", "w_ffn2_in", "w_ffn2_out", "g_final")

IN_SPLIT = (("xa", 0, 512), ("xb", 512, 1024), ("gb", 1536, 1024), ("q", 2560, 512), ("k", 3072, 512), ("v", 3584, 512),
            ("fl", 4096, 8), ("gl", 4104, 3072))
PACK_ORDER = ("xb", "gb", "gl", "xa", "q", "k", "v")


def _shard_shape(shape, axis):
    s = list(shape)
    s[axis] //= N_CHIPS
    return (DEPTH, *s)


def _round_up(n, m):
    return -(-n // m) * m


def _pack_rows(parts, dtype, row_multiple):
    flat = jnp.concatenate([p.astype(dtype).reshape(-1) for p in parts])
    rows = _round_up(-(-flat.shape[0] // D), row_multiple)
    return jnp.pad(flat, (0, rows * D - flat.shape[0])).reshape(rows, D)


def _f32_as_bf16(x):
    return lax.bitcast_convert_type(x.astype(F32), BF).reshape(-1)


def _bf16_as_f32(x):
    return lax.bitcast_convert_type(x.reshape(-1, 2), F32)


def _unshard(seg, shape, axis):
    sh = _shard_shape(shape, axis)
    t = seg.reshape(N_CHIPS, *sh)
    t = jnp.moveaxis(t, 0, axis + 1)
    return t.reshape(DEPTH, *shape)


def _shard_split(full, shape, axis):
    sh = list(shape)
    sh[axis:axis + 1] = [N_CHIPS, shape[axis] // N_CHIPS]
    t = full.reshape(DEPTH, *sh)
    t = jnp.moveaxis(t, axis + 1, 0)
    return t.reshape(N_CHIPS, -1)


def _gathered_weights(gath):
    flat = gath.reshape(N_CHIPS, -1)
    out, off = {}, 0
    for name, shape, axis in BIG:
        n = math.prod(_shard_shape(shape, axis))
        if name == "conv_w":
            seg = lax.bitcast_convert_type(flat[:, off:off + 2 * n].reshape(N_CHIPS, n, 2), F32)
            off += 2 * n
        else:
            seg = flat[:, off:off + n]
            off += n
        out[name] = _unshard(seg, shape, axis)
    return out


def _layer_weights(full, small, l):
    w_in = full["w_in"][l]
    cols = {n: w_in[:, o:o + s] for n, o, s in IN_SPLIT}
    w = dict(ffn1_in=full["w_ffn1_in"][l], ffn1_out=full["w_ffn1_out"][l], ffn2_in=full["w_ffn2_in"][l], ffn2_out=full["w_ffn2_out"][l],
             wp=jnp.concatenate([cols[n] for n in PACK_ORDER], axis=1),
             wfl=jnp.pad(cols["fl"], ((0, 0), (0, LANES - FOX_HEADS))),
             up_a=full["w_up_a"][l], up_b=full["w_up_b"][l], up_c=full["w_up_c"][l], w_o=full["w_o"][l], xq=full["w_xq"][l],
             xkv=full["w_xkv"][l], xo=full["w_xo"][l], conv_w=full["conv_w"][l])
    for n in ("g_ffn1", "g_mix", "b_gate", "pool_scale", "conv_b", "b_rg_a", "b_rg_x", "g_cross", "g_mem", "g_ffn2"):
        w[n] = small[n][l].reshape(1, -1)
    w["lam"] = small["lru_lambda"][l].reshape(1, -1)
    w["b_f"] = jnp.pad(small["b_f"][l], (0, LANES - FOX_HEADS)).reshape(1, LANES)
    for n in ("w_pool", "w_rg_a", "w_rg_x"):
        w[n] = small[n][l]
    return w


def _unpack_w_in_grad(gwp, gwfl):
    pk = {}
    off = 0
    sizes = {n: s for n, _, s in IN_SPLIT}
    for n in PACK_ORDER:
        pk[n] = gwp[:, off:off + sizes[n]]
        off += sizes[n]
    pk["fl"] = gwfl[:, :FOX_HEADS].astype(gwp.dtype)
    return jnp.concatenate([pk[n] for n, _, _ in IN_SPLIT], axis=1)


def _place():
    x, y, c = lax.axis_index("x"), lax.axis_index("y"), lax.axis_index("c")
    others = [(1 - x, y), (x, 1 - y), (1 - x, 1 - y)]
    return x, y, c, others


def _rcopy(src, dst, send_sems, recv_sems, k, to):
    return pltpu.make_async_remote_copy(src_ref=src, dst_ref=dst, send_sem=send_sems.at[k], recv_sem=recv_sems.at[k],
                                        device_id=to, device_id_type=MESH)


def _comm_call(body, name, ins, out_shape, n_sems):
    return pl.pallas_call(
        body, name=name, in_specs=[ANY] * len(ins), out_specs=[ANY] * len(out_shape), out_shape=out_shape,
        scratch_shapes=[pltpu.SemaphoreType.DMA((n_sems,)), pltpu.SemaphoreType.DMA((n_sems,)), pltpu.SemaphoreType.DMA((n_sems,))],
    )(*ins)


def allgather_chips(mine):
    rows = mine.shape[0]
    half = rows // 2

    def body(x_ref, out_ref, send_sems, recv_sems, local_sems):
        x, y, c, others = _place()
        me = 2 * x + y

        def part(chip, hc):
            return out_ref.at[chip, pl.ds(pl.multiple_of(hc * half, 16), half), :]

        local = pltpu.make_async_copy(x_ref, out_ref.at[me], local_sems.at[0])
        local.start()
        src = x_ref.at[pl.ds(pl.multiple_of(c * half, 16), half), :]
        first = [_rcopy(src, part(me, c), send_sems, recv_sems, j, (ox, oy, c)) for j, (ox, oy) in enumerate(others)]
        for cp in first:
            cp.start()
        passed = []
        for j, (ox, oy) in enumerate(others):
            theirs = part(2 * ox + oy, c)
            _rcopy(theirs, theirs, send_sems, recv_sems, j, (ox, oy, c)).wait_recv()
            cp = _rcopy(theirs, theirs, send_sems, recv_sems, 3 + j, (x, y, 1 - c))
            cp.start()
            passed.append(cp)
        for j, (ox, oy) in enumerate(others):
            theirs = part(2 * ox + oy, 1 - c)
            _rcopy(theirs, theirs, send_sems, recv_sems, 3 + j, (x, y, 1 - c)).wait_recv()
        for cp in first + passed:
            cp.wait_send()
        local.wait()

    return _comm_call(body, "allgather_chips", [mine], [jax.ShapeDtypeStruct((N_CHIPS, rows, D), mine.dtype)], 6)[0]


def allgather_same_core(bufs):
    nb = len(bufs)

    def body(*refs):
        ins, outs = refs[:nb], refs[nb:2 * nb]
        send_sems, recv_sems, local_sems = refs[2 * nb:]
        x, y, c, others = _place()
        me = 2 * x + y
        copies = []
        for b in range(nb):
            local = pltpu.make_async_copy(ins[b], outs[b].at[me], local_sems.at[b])
            local.start()
            copies.append(local)
        sends = []
        for b in range(nb):
            for j, (ox, oy) in enumerate(others):
                cp = _rcopy(ins[b], outs[b].at[me], send_sems, recv_sems, 3 * b + j, (ox, oy, c))
                cp.start()
                sends.append(cp)
        for b in range(nb):
            for j, (ox, oy) in enumerate(others):
                theirs = outs[b].at[2 * ox + oy]
                _rcopy(theirs, theirs, send_sems, recv_sems, 3 * b + j, (ox, oy, c)).wait_recv()
        for cp in sends:
            cp.wait_send()
        for cp in copies:
            cp.wait()

    shapes = [jax.ShapeDtypeStruct((N_CHIPS, *b.shape), b.dtype) for b in bufs]
    return _comm_call(body, "allgather_same_core", list(bufs), shapes, 3 * nb)


def sibling_halves(bufs):
    nb = len(bufs)

    def body(*refs):
        ins, outs = refs[:nb], refs[nb:3 * nb]
        send_sems, recv_sems, local_sems = refs[3 * nb:]
        x, y, c, _ = _place()
        work = []
        for b in range(nb):
            keep = pltpu.make_async_copy(ins[b].at[:, c], outs[2 * b], local_sems.at[b])
            keep.start()
            cp = _rcopy(ins[b].at[:, 1 - c], outs[2 * b + 1], send_sems, recv_sems, b, (x, y, 1 - c))
            cp.start()
            work.append((keep, cp))
        for keep, cp in work:
            cp.wait()
            keep.wait()

    shapes = []
    for b in bufs:
        s = jax.ShapeDtypeStruct((N_CHIPS, *b.shape[2:]), b.dtype)
        shapes += [s, s]
    return _comm_call(body, "sibling_halves", list(bufs), shapes, nb)


def chip_exchange(bufs):
    nb = len(bufs)

    def body(*refs):
        ins, outs = refs[:nb], refs[nb:3 * nb]
        send_sems, recv_sems, local_sems = refs[3 * nb:]
        x, y, c, others = _place()
        me = 2 * x + y
        keeps, sends = [], []
        for b in range(nb):
            keep = pltpu.make_async_copy(ins[b].at[me], outs[2 * b], local_sems.at[b])
            keep.start()
            keeps.append(keep)
            for j, (ox, oy) in enumerate(others):
                cp = _rcopy(ins[b].at[2 * ox + oy], outs[2 * b + 1].at[j], send_sems, recv_sems, 3 * b + j, (ox, oy, c))
                cp.start()
                sends.append(cp)
        for cp in sends:
            cp.wait()
        for keep in keeps:
            keep.wait()

    shapes = []
    for b in bufs:
        shapes += [jax.ShapeDtypeStruct(b.shape[1:], b.dtype), jax.ShapeDtypeStruct((3, *b.shape[1:]), b.dtype)]
    return _comm_call(body, "chip_exchange", list(bufs), shapes, 3 * nb)


def sibling_allgather(bufs):
    nb = len(bufs)

    def body(*refs):
        ins, outs = refs[:nb], refs[nb:2 * nb]
        send_sems, recv_sems, local_sems = refs[2 * nb:]
        x, y, c, _ = _place()
        work = []
        for b in range(nb):
            keep = pltpu.make_async_copy(ins[b], outs[b].at[c], local_sems.at[b])
            keep.start()
            cp = pltpu.make_async_remote_copy(src_ref=ins[b], dst_ref=outs[b].at[c], send_sem=send_sems.at[b],
                                              recv_sem=recv_sems.at[b], device_id=(x, y, 1 - c), device_id_type=MESH)
            cp.start()
            work.append((keep, cp))
        for b, (keep, cp) in enumerate(work):
            cp.wait_send()
            theirs = outs[b].at[1 - c]
            _rcopy(theirs, theirs, send_sems, recv_sems, b, (x, y, 1 - c)).wait_recv()
            keep.wait()

    shapes = [jax.ShapeDtypeStruct((2, *b.shape), b.dtype) for b in bufs]
    return _comm_call(body, "sibling_allgather", list(bufs), shapes, nb)


def add_rows(parts, out_dtype, name):
    def fn(step, i, n, *vals):
        acc = vals[0].astype(F32)
        for v in vals[1:]:
            acc = acc + v.astype(F32)
        return (acc,), ()
    rows = parts[0].shape[0]
    return rowwise(fn, list(parts), outs=[(D, out_dtype)], ts=_pick(rows, (1024, 512, 256, 128, 64, 32, 16, 8)), name=name)[0][0]


def reduce_scatter(big, small):
    rb, rs = big.shape[1], small.shape[1]
    halves = sibling_halves([big.reshape(N_CHIPS, 2, rb // 2, D), small.reshape(N_CHIPS, 2, rs // 2, D)])
    pb = add_rows([h.reshape(-1, D) for h in halves[0:2]], BF, "rs_add_sibling_big").reshape(N_CHIPS, rb // 2, D)
    ps = add_rows([h.reshape(-1, D) for h in halves[2:4]], F32, "rs_add_sibling_small").reshape(N_CHIPS, rs // 2, D)
    own_b, recv_b, own_s, recv_s = chip_exchange([pb, ps])
    qb = add_rows([own_b, recv_b[0], recv_b[1], recv_b[2]], F32, "rs_add_chips_big")
    qs = add_rows([own_s, recv_s[0], recv_s[1], recv_s[2]], F32, "rs_add_chips_small")
    fb, fs = sibling_allgather([qb, qs])
    return fb.reshape(rb, D), fs.reshape(rs, D)


_BC1 = 1.0 - ADAM_B1 ** ADAM_STEP
_BC2 = 1.0 - ADAM_B2 ** ADAM_STEP


def adamw(w, g, m, v, name):
    rows, cols = w.shape

    def fn(step, i, n, w, g, m, v):
        m2 = ADAM_B1 * m + (1.0 - ADAM_B1) * g
        v2 = ADAM_B2 * v + (1.0 - ADAM_B2) * (g * g)
        delta = -ADAM_LR * ((m2 / _BC1) / (jnp.sqrt(v2 / _BC2) + ADAM_EPS) + ADAM_WD * w)
        return (delta, m2, v2), ()

    ts = _pick(rows, (256, 128, 64, 32, 16, 8))
    return rowwise(fn, [w, g, m, v], outs=[(cols, F32)] * 3, ts=ts, name=name)[0]


SMALL_ROW_MULTIPLE = 16
BIG_ROW_MULTIPLE = 512


def local_step(x, mem, target, full, small):
    ws = [_layer_weights(full, small, l) for l in range(DEPTH)]
    saved = []
    h = x
    for l in range(DEPTH):
        h, s = layer_fwd(h, mem, ws[l], "l")
        saved.append(s)
    loss, dh, dg_final = loss_head(h, target, small["g_final"].reshape(1, D))
    grads = [None] * DEPTH
    for l in reversed(range(DEPTH)):
        dh, grads[l] = layer_bwd(dh, mem, ws[l], saved[l], "l")
    out = {}
    for l in range(DEPTH):
        g = grads[l]
        g["w_in"] = _unpack_w_in_grad(g.pop("wp"), g.pop("wfl"))
        g["b_f"] = g["b_f"][:, :FOX_HEADS]
    for name, shape, _ in BIG:
        out[name] = jnp.stack([grads[l][name].reshape(shape) for l in range(DEPTH)])
    for name, shape in SMALL:
        out[name] = jnp.stack([grads[l][name].reshape(shape) for l in range(DEPTH)])
    out["g_final"] = dg_final.reshape(D)
    return loss, dh, out


def kernel(x, mem, g_ffn1, w_ffn1_in, w_ffn1_out, g_mix, w_in, b_f, b_gate, w_pool, pool_scale, w_up_a, conv_w, conv_b, w_rg_a, b_rg_a, w_rg_x, b_rg_x, lru_lambda, w_up_b, w_up_c, w_o, g_cross, g_mem, w_xq, w_xkv, w_xo, g_ffn2, w_ffn2_in, w_ffn2_out, g_final, loss_target, m_g_ffn1, m_w_ffn1_in, m_w_ffn1_out, m_g_mix, m_w_in, m_b_f, m_b_gate, m_w_pool, m_pool_scale, m_w_up_a, m_conv_w, m_conv_b, m_w_rg_a, m_b_rg_a, m_w_rg_x, m_b_rg_x, m_lru_lambda, m_w_up_b, m_w_up_c, m_w_o, m_g_cross, m_g_mem, m_w_xq, m_w_xkv, m_w_xo, m_g_ffn2, m_w_ffn2_in, m_w_ffn2_out, m_g_final, v_g_ffn1, v_w_ffn1_in, v_w_ffn1_out, v_g_mix, v_w_in, v_b_f, v_b_gate, v_w_pool, v_pool_scale, v_w_up_a, v_conv_w, v_conv_b, v_w_rg_a, v_b_rg_a, v_w_rg_x, v_b_rg_x, v_lru_lambda, v_w_up_b, v_w_up_c, v_w_o, v_g_cross, v_g_mem, v_w_xq, v_w_xkv, v_w_xo, v_g_ffn2, v_w_ffn2_in, v_w_ffn2_out, v_g_final):
    args = dict(locals())
    weights = {n: args[n] for n in ORDER}
    m_in = {n: args["m_" + n] for n in ORDER}
    v_in = {n: args["v_" + n] for n in ORDER}
    big_names = [n for n, _, _ in BIG]
    small_names = [n for n, _ in SMALL] + ["g_final"]

    parts = [_f32_as_bf16(weights[n]) if n == "conv_w" else weights[n].astype(BF).reshape(-1) for n in big_names]
    mine = _pack_rows(parts, BF, BIG_ROW_MULTIPLE)
    full = _gathered_weights(allgather_chips(mine))

    small = {n: weights[n] for n in small_names}
    loss, grad_x, contrib = local_step(x[0], mem[0], loss_target[0], full, small)
    loss = lax.psum(loss, ("x", "y", "c"))

    big_split = [_shard_split(contrib[n], shape, axis) for n, shape, axis in BIG]
    big_buf = jnp.concatenate([b.astype(BF) for b in big_split], axis=1)
    rows_b = _round_up(-(-big_buf.shape[1] // D), BIG_ROW_MULTIPLE)
    big_buf = jnp.pad(big_buf, ((0, 0), (0, rows_b * D - big_buf.shape[1]))).reshape(N_CHIPS, rows_b, D)
    small_flat = jnp.concatenate([contrib[n].reshape(-1) for n in small_names])
    rows_s = _round_up(-(-small_flat.shape[0] // (N_CHIPS * D)), SMALL_ROW_MULTIPLE)
    small_buf = jnp.pad(small_flat, (0, N_CHIPS * rows_s * D - small_flat.shape[0])).reshape(N_CHIPS, rows_s, D)
    red_big, red_small_mine = reduce_scatter(big_buf, small_buf)
    red_small = allgather_same_core([red_small_mine])[0].reshape(-1)

    grads = {}
    flat = red_big.reshape(-1)
    off = 0
    for name, shape, axis in BIG:
        sh = _shard_shape(shape, axis)
        grads[name] = flat[off:off + math.prod(sh)].reshape(sh)
        off += math.prod(sh)
    off = 0
    for name in small_names:
        sh = weights[name].shape
        grads[name] = red_small[off:off + math.prod(sh)].reshape(sh)
        off += math.prod(sh)

    delta, new_m, new_v = {}, {}, {}
    for name in big_names:
        sh = weights[name].shape
        two_d = (-1, sh[-1])
        d, m2, v2 = adamw(weights[name].reshape(two_d), grads[name].reshape(two_d), m_in[name].reshape(two_d),
                          v_in[name].reshape(two_d), "adamw_" + name)
        delta[name], new_m[name], new_v[name] = d.reshape(sh), m2.reshape(sh), v2.reshape(sh)
    packed = [_pack_rows([t[n] for n in small_names], F32, 8) for t in (weights, grads, m_in)]
    v_flat = jnp.concatenate([v_in[n].reshape(-1) for n in small_names])
    v_pack = jnp.pad(v_flat, (0, packed[0].size - v_flat.shape[0]), constant_values=1.0).reshape(packed[0].shape)
    outs = adamw(packed[0], packed[1], packed[2], v_pack, "adamw_small")
    for t, res in zip((delta, new_m, new_v), outs):
        flat = res.reshape(-1)
        off = 0
        for name in small_names:
            sh = weights[name].shape
            t[name] = flat[off:off + math.prod(sh)].reshape(sh)
            off += math.prod(sh)

    return (loss, grad_x[None], *[grads[n] for n in ORDER], *[delta[n] for n in ORDER],
            *[new_m[n] for n in ORDER], *[new_v[n] for n in ORDER])
```

```python
import functools
import math

import jax
import jax.numpy as jnp
from jax import lax
from jax.experimental import pallas as pl
from jax.experimental.pallas import tpu as pltpu

F32 = jnp.float32
BF = jnp.bfloat16

D = 1024
DFF = 2816
DEPTH = 2
POOL_W = 512
POOL_WINDOWS = (2, 4, 8, 16)
LRU_HEADS = 8
LRU_C = 8.0
FOX_HEADS = 8
FOX_HD = 64
FOX_W = 512
X_HEADS = 4
X_HD = 256
EPS = 1e-6
LANES = 128
N_CHIPS = 4

ADAM_LR, ADAM_B1, ADAM_B2, ADAM_EPS, ADAM_WD, ADAM_STEP = 0.001, 0.9, 0.999, 1e-08, 0.01, 10

VMEM_LIMIT_BYTES = 56 * 2 ** 20

PK_XB, PK_GB, PK_GL, PK_XA, PK_Q, PK_K, PK_V = 0, 1024, 2048, 5120, 5632, 6144, 6656
PK_W = 7168

MESH = pl.DeviceIdType.MESH
ANY = pl.BlockSpec(memory_space=pl.ANY)


def _cparams(ngrid):
    return pltpu.CompilerParams(dimension_semantics=("arbitrary",) * ngrid, vmem_limit_bytes=VMEM_LIMIT_BYTES)


def _pick(n, cands):
    for c in cands:
        if n % c == 0:
            return c
    return n


def _iota(shape, dim):
    return lax.broadcasted_iota(jnp.int32, shape, dim)


def _sigmoid(x):
    return 1.0 / (1.0 + jnp.exp(-x))


def _softplus(z):
    return jnp.maximum(z, 0.0) + jnp.log1p(jnp.exp(-jnp.abs(z)))


def _expm1(x):
    small = jnp.abs(x) < 0.25
    xs = jnp.where(small, x, 0.0)
    poly = xs * (1.0 + xs * (1 / 2 + xs * (1 / 6 + xs * (1 / 24 + xs * (1 / 120 + xs * (1 / 720 + xs * (1 / 5040 + xs * (1 / 40320))))))))
    return jnp.where(small, poly, jnp.exp(x) - 1.0)


_GELU_K = math.sqrt(2.0 / math.pi)


def _gelu_and_grad(x):
    inner = _GELU_K * (x + 0.044715 * x * x * x)
    t = jnp.tanh(inner)
    g = 0.5 * x * (1.0 + t)
    dg = 0.5 * (1.0 + t) + 0.5 * x * (1.0 - t * t) * _GELU_K * (1.0 + 3 * 0.044715 * x * x)
    return g, dg


def _dot(a, b, dims):
    return lax.dot_general(a.astype(BF), b.astype(BF), (dims, ((), ())), preferred_element_type=F32)


def _dot_nn(a, b):
    return _dot(a, b, ((1,), (0,)))


def _dot_nt(a, b):
    return _dot(a, b, ((1,), (1,)))


def _dot_tn(a, b):
    return _dot(a, b, ((0,), (0,)))


def _colsum(x):
    return jnp.sum(x, axis=0, keepdims=True)


_TM = (1024, 1408, 512, 256, 128)
_TN = (1408, 1024, 512, 384, 256, 128)
_TK = (2432, 1408, 1024, 512, 256, 128)


def matmul(a, b, mode, *, out_dtype=None, alpha=1.0, res=None, name):
    out_dtype = BF if out_dtype is None else out_dtype
    if mode == "nn":
        (M, K), N = a.shape, b.shape[1]
    elif mode == "nt":
        (M, K), N = a.shape, b.shape[0]
    else:
        (K, M), N = a.shape, b.shape[1]
    tm, tn, tk = _pick(M, _TM), _pick(N, _TN), _pick(K, _TK)
    nk = K // tk
    dims = {"nn": ((1,), (0,)), "nt": ((1,), (1,)), "tn": ((0,), (0,))}[mode]
    a_spec = pl.BlockSpec((tk, tm), lambda i, j, k: (k, i)) if mode == "tn" else pl.BlockSpec((tm, tk), lambda i, j, k: (i, k))
    b_spec = pl.BlockSpec((tn, tk), lambda i, j, k: (j, k)) if mode == "nt" else pl.BlockSpec((tk, tn), lambda i, j, k: (k, j))
    o_spec = pl.BlockSpec((tm, tn), lambda i, j, k: (i, j))
    has_res = res is not None

    def body(*refs):
        a_ref, b_ref = refs[0], refs[1]
        r_ref = refs[2] if has_res else None
        o_ref, acc = refs[2 + has_res], refs[3 + has_res]
        k = pl.program_id(2)

        @pl.when(k == 0)
        def _():
            acc[...] = jnp.zeros_like(acc)

        acc[...] += _dot(a_ref[...], b_ref[...], dims)

        @pl.when(k == nk - 1)
        def _():
            r = acc[...]
            if alpha != 1.0:
                r = r * alpha
            if has_res:
                r = r + r_ref[...].astype(F32)
            o_ref[...] = r.astype(o_ref.dtype)

    ins = [a, b] + ([res] if has_res else [])
    specs = [a_spec, b_spec] + ([o_spec] if has_res else [])
    return pl.pallas_call(
        body, name=name, grid=(M // tm, N // tn, nk), in_specs=specs, out_specs=o_spec,
        out_shape=jax.ShapeDtypeStruct((M, N), out_dtype), scratch_shapes=[pltpu.VMEM((tm, tn), F32)],
        compiler_params=_cparams(3))(*ins)


class Rows:
    def __init__(self, arr, w=None, cb=0, halo=None, h=8):
        self.arr, self.w, self.cb, self.halo, self.h = arr, (arr.shape[1] if w is None else w), cb, halo, h


def rowwise(fn, rows, consts=(), outs=(), accs=(), scratch=(), *, ts, name, reverse=False):
    rows = [r if isinstance(r, Rows) else Rows(r) for r in rows]
    S = rows[0].arr.shape[0]
    n = S // ts
    tile = (lambda s: n - 1 - s) if reverse else (lambda s: s)

    def row_spec(r):
        if r.halo is None:
            return pl.BlockSpec((ts, r.w), lambda s: (tile(s), r.cb))
        per, last = ts // r.h, S // r.h - 1
        if r.halo == "prev":
            return pl.BlockSpec((r.h, r.w), lambda s: (jnp.maximum(tile(s) * per - 1, 0), r.cb))
        return pl.BlockSpec((r.h, r.w), lambda s: (jnp.minimum((tile(s) + 1) * per, last), r.cb))

    def whole(shape):
        nd = len(shape)
        return pl.BlockSpec(tuple(shape), lambda s: (0,) * nd)

    nr, nc, no, na = len(rows), len(consts), len(outs), len(accs)

    def body(*refs):
        in_refs, out_refs = refs[:nr + nc], refs[nr + nc:nr + nc + no]
        acc_refs, scr = refs[nr + nc + no:nr + nc + no + na], refs[nr + nc + no + na:]
        step = pl.program_id(0)
        o, inc = fn(step, tile(step), n, *[r[...] for r in in_refs], *scr)
        for ref, val in zip(out_refs, o):
            ref[...] = val.astype(ref.dtype)
        if na:
            @pl.when(step == 0)
            def _():
                for ref in acc_refs:
                    ref[...] = jnp.zeros_like(ref)
            for ref, val in zip(acc_refs, inc):
                ref[...] += val.astype(ref.dtype)

    out_shape = [jax.ShapeDtypeStruct((S, w), dt) for w, dt in outs] + [jax.ShapeDtypeStruct(tuple(s), dt) for s, dt in accs]
    out_specs = [pl.BlockSpec((ts, w), lambda s: (tile(s), 0)) for w, _ in outs] + [whole(s) for s, _ in accs]
    res = pl.pallas_call(
        body, name=name, grid=(n,),
        in_specs=[row_spec(r) for r in rows] + [whole(c.shape) for c in consts],
        out_specs=out_specs, out_shape=out_shape, scratch_shapes=list(scratch),
        compiler_params=_cparams(1))(*[r.arr for r in rows], *consts)
    return res[:no], res[no:]


def _rstd(x):
    return lax.rsqrt(jnp.mean(x * x, axis=-1, keepdims=True) + EPS)


def rms_fwd(x, g, *, name, ts=512):
    def fn(step, i, n, x, g):
        return ((x * _rstd(x)) * g,), ()
    return rowwise(fn, [x], [g], [(D, BF)], ts=min(ts, x.shape[0]), name=name)[0][0]


def rms_bwd(x, g, dh, dres, *, name, ts=512):
    has_res = dres is not None

    def fn(step, i, n, x, dh, *rest):
        g = rest[-1]
        r = _rstd(x)
        xhat = x * r
        dh = dh.astype(F32)
        gd = dh * g
        dx = r * (gd - xhat * jnp.mean(xhat * gd, axis=-1, keepdims=True))
        if has_res:
            dx = dx + rest[0]
        return (dx,), (_colsum(dh * xhat),)

    rows = [x, dh] + ([dres] if has_res else [])
    o, a = rowwise(fn, rows, [g], [(D, F32)], [((1, D), F32)], ts=min(ts, x.shape[0]), name=name)
    return o[0], a[0]


def ffn_fwd(x, g, w_in, w_out, tag):
    h = rms_fwd(x, g, name=tag + "_rms")
    ab = matmul(h, w_in, "nn", name=tag + "_in")

    def act_fn(step, i, n, ab):
        a, b = ab[:, :DFF].astype(F32), ab[:, DFF:].astype(F32)
        return ((a * _sigmoid(a)) * b,), ()

    act = rowwise(act_fn, [ab], outs=[(DFF, BF)], ts=256, name=tag + "_act")[0][0]
    y = matmul(act, w_out, "nn", out_dtype=F32, alpha=0.5, res=x, name=tag + "_out")
    return y, (x, h, ab, act)


def ffn_bwd(dy, saved, g, w_in, w_out, tag):
    x, h, ab, act = saved
    dact = matmul(dy, w_out, "nt", alpha=0.5, name=tag + "_dact")
    dw_out = matmul(act, dy, "tn", alpha=0.5, name=tag + "_dwout")

    def dab_fn(step, i, n, ab, dact):
        a, b, dact = ab[:, :DFF].astype(F32), ab[:, DFF:].astype(F32), dact.astype(F32)
        s = _sigmoid(a)
        da = dact * b * (s * (1.0 + a * (1.0 - s)))
        db = dact * (a * s)
        return (jnp.concatenate([da, db], axis=1),), ()

    dab = rowwise(dab_fn, [ab, dact], outs=[(2 * DFF, BF)], ts=256, name=tag + "_dab")[0][0]
    dw_in = matmul(h, dab, "tn", name=tag + "_dwin")
    dh = matmul(dab, w_in, "nt", name=tag + "_dh")
    dx, dg = rms_bwd(x, g, dh, dy, name=tag + "_drms")
    return dx, dg, dw_in, dw_out


POOL_TS = 256
POOL_HALO = 16


def pool_fwd(P, w_pool, pool_scale, tag):
    ts = POOL_TS

    def fn(step, i, n, xa, halo, w, scale):
        xa = xa.astype(F32)
        halo = halo.astype(F32) * jnp.where(i > 0, 1.0, 0.0)
        ext = jnp.concatenate([halo, xa], axis=0)
        pos = (i * ts + 1 + _iota((ts, LANES), 0)).astype(F32)
        ds, ys = [], []
        for gi, win in enumerate(POOL_WINDOWS):
            e = ext[:, gi * LANES:(gi + 1) * LANES]
            sh = 1
            while sh < win:
                e = e + pltpu.roll(e, sh, 0)
                sh *= 2
            mean = e[POOL_HALO:] / jnp.minimum(pos, float(win))
            d = mean - xa[:, gi * LANES:(gi + 1) * LANES]
            ds.append(d)
            ys.append(_dot_nn(d, w[gi]))
        d = jnp.concatenate(ds, axis=1)
        return (d, jnp.concatenate(ys, axis=1) * scale), ()

    xa = Rows(P, POOL_W, PK_XA // POOL_W)
    xa_prev = Rows(P, POOL_W, PK_XA // POOL_W, "prev", POOL_HALO)
    (d, ya0), _ = rowwise(fn, [xa, xa_prev], [w_pool, pool_scale], [(POOL_W, BF), (POOL_W, BF)], ts=ts, name=tag + "_pool")
    return d, ya0


def pool_bwd(dya0, d, w_pool, pool_scale, tag):
    ts = POOL_TS
    L = ts + POOL_HALO

    def fn(step, i, n, dya0, dya0_next, d, w, scale):
        d = d.astype(F32)
        dz = jnp.concatenate([dya0.astype(F32), dya0_next.astype(F32) * jnp.where(i < n - 1, 1.0, 0.0)], axis=0) * scale
        pos = (i * ts + 1 + _iota((L, LANES), 0)).astype(F32)
        dxa, dws, dsc = [], [], []
        for gi, win in enumerate(POOL_WINDOWS):
            sl = slice(gi * LANES, (gi + 1) * LANES)
            dzg = dz[:, sl]
            dd = _dot_nt(dzg, w[gi])
            e = dd / jnp.minimum(pos, float(win))
            sh = 1
            while sh < win:
                e = e + pltpu.roll(e, L - sh, 0)
                sh *= 2
            dxa.append(e[:ts] - dd[:ts])
            dws.append(_dot_tn(d[:, sl], dzg[:ts])[None])
            z = _dot_nn(d[:, sl], w[gi])
            dsc.append(_colsum(dya0[:, sl].astype(F32) * z))
        return (jnp.concatenate(dxa, axis=1),), (jnp.concatenate(dws, axis=0), jnp.concatenate(dsc, axis=1))

    (dxa,), (dw_pool, dscale) = rowwise(
        fn, [dya0, Rows(dya0, halo="next", h=POOL_HALO), d], [w_pool, pool_scale],
        [(POOL_W, BF)], [((4, LANES, LANES), F32), ((1, POOL_W), F32)], ts=ts, name=tag + "_dpool")
    return dxa, dw_pool, dscale


LRU_R = 256


def _scan_fwd(A, U):
    R = A.shape[0]
    row = _iota(A.shape, 0)
    d = 1
    while d < R:
        m = row >= d
        A_sh = jnp.where(m, pltpu.roll(A, d, 0), 1.0)
        U_sh = jnp.where(m, pltpu.roll(U, d, 0), 0.0)
        U = A * U_sh + U
        A = A * A_sh
        d *= 2
    return A, U


def _scan_bwd(B, X):
    R = B.shape[0]
    row = _iota(B.shape, 0)
    d = 1
    while d < R:
        m = row < R - d
        B_sh = jnp.where(m, pltpu.roll(B, R - d, 0), 1.0)
        X_sh = jnp.where(m, pltpu.roll(X, R - d, 0), 0.0)
        X = X + B * X_sh
        B = B * B_sh
        d *= 2
    return X


def _lru_gates(xc, wa, ba, wx, bx, lam):
    r = _sigmoid(_dot_nn(xc, wa) + ba)
    ig = _sigmoid(_dot_nn(xc, wx) + bx)
    sp = _softplus(-lam)
    log_a = -LRU_C * r * sp
    a = jnp.exp(log_a)
    mult = jnp.sqrt(-_expm1(2.0 * log_a))
    return r, ig, sp, a, mult


def _lru_specs(S, R, reverse):
    nch = S // R
    ch = (lambda j: nch - 1 - j) if reverse else (lambda j: j)
    per = R // 8

    def col(off):
        return pl.BlockSpec((R, LANES), lambda h, j: (ch(j), off + h))

    def prev(off):
        return pl.BlockSpec((8, LANES), lambda h, j: (jnp.maximum(ch(j) * per - 1, 0), off + h))

    vec = pl.BlockSpec((1, LANES), lambda h, j: (0, h))
    cw = pl.BlockSpec((4, LANES), lambda h, j: (0, h))
    wsq = pl.BlockSpec((None, LANES, LANES), lambda h, j: (h, 0, 0))
    return nch, ch, col, prev, vec, cw, wsq


def lru_fwd(P, conv_w, conv_b, w_a, b_a, w_x, b_x, lam, tag):
    S = P.shape[0]
    R = min(LRU_R, S)
    nch, ch, col, prev, vec, cw_spec, wsq = _lru_specs(S, R, False)

    def body(xb_ref, halo_ref, gb_ref, cw_ref, cb_ref, wa_ref, ba_ref, wx_ref, bx_ref, lam_ref, xc_ref, h_ref, yb_ref, carry):
        j = pl.program_id(1)
        xb = xb_ref[...].astype(F32)
        halo = halo_ref[...].astype(F32) * jnp.where(j > 0, 1.0, 0.0)
        ext = jnp.concatenate([halo, xb], axis=0)
        cw = cw_ref[...]
        xc = cb_ref[...]
        for k in range(4):
            e = ext if k == 3 else pltpu.roll(ext, 3 - k, 0)
            xc = xc + e[8:] * cw[k:k + 1]
        r, ig, sp, a, mult = _lru_gates(xc, wa_ref[...], ba_ref[...], wx_ref[...], bx_ref[...], lam_ref[...])
        u = mult * (ig * xc)
        cum_a, hloc = _scan_fwd(a, u)

        @pl.when(j == 0)
        def _():
            carry[...] = jnp.zeros_like(carry)

        hfull = hloc + cum_a * carry[0:1]
        carry[...] = jnp.broadcast_to(hfull[R - 1:R], carry.shape)
        gel, _ = _gelu_and_grad(gb_ref[...].astype(F32))
        xc_ref[...] = xc
        h_ref[...] = hfull
        yb_ref[...] = (hfull * gel).astype(yb_ref.dtype)

    out = pl.BlockSpec((R, LANES), lambda h, j: (j, h))
    return pl.pallas_call(
        body, name=tag + "_lru", grid=(LRU_HEADS, nch),
        in_specs=[col(PK_XB // LANES), prev(PK_XB // LANES), col(PK_GB // LANES), cw_spec, vec, wsq, vec, wsq, vec, vec],
        out_specs=[out, out, out],
        out_shape=[jax.ShapeDtypeStruct((S, D), F32), jax.ShapeDtypeStruct((S, D), F32), jax.ShapeDtypeStruct((S, D), BF)],
        scratch_shapes=[pltpu.VMEM((8, LANES), F32)], compiler_params=_cparams(2),
    )(P, P, P, conv_w, conv_b, w_a, b_a, w_x, b_x, lam)


def lru_bwd(dyb0, P, hh, xc, conv_w, w_a, b_a, w_x, b_x, lam, tag):
    S = P.shape[0]
    R = min(LRU_R, S)
    nch, ch, col, prev, vec, cw_spec, wsq = _lru_specs(S, R, True)

    def body(dyb_ref, gb_ref, h_ref, hprev_ref, xc_ref, xb_ref, xbprev_ref, cw_ref, wa_ref, ba_ref, wx_ref, bx_ref, lam_ref,
             dgb_ref, dxb_ref, dwa_ref, dba_ref, dwx_ref, dbx_ref, dlam_ref, dcw_ref, dcb_ref, gcarry, dxc_head):
        j = pl.program_id(1)
        jj = nch - 1 - j
        has_prev = jnp.where(jj > 0, 1.0, 0.0)
        row = _iota((R, LANES), 0)
        xc, hv, lam = xc_ref[...], h_ref[...], lam_ref[...]
        wa, wx = wa_ref[...], wx_ref[...]
        r, ig, sp, a, mult = _lru_gates(xc, wa, ba_ref[...], wx, bx_ref[...], lam)
        dyb = dyb_ref[...].astype(F32)
        gel, dgel = _gelu_and_grad(gb_ref[...].astype(F32))
        dgb_ref[...] = (dyb * hv * dgel).astype(dgb_ref.dtype)

        @pl.when(j == 0)
        def _():
            gcarry[...] = jnp.zeros_like(gcarry)
            dxc_head[...] = jnp.zeros_like(dxc_head)
            for ref in (dwa_ref, dba_ref, dwx_ref, dbx_ref, dlam_ref, dcw_ref, dcb_ref):
                ref[...] = jnp.zeros_like(ref)

        B = jnp.where(row < R - 1, pltpu.roll(a, R - 1, 0), 0.0)
        X = dyb * gel + jnp.where(row == R - 1, gcarry[0:1], 0.0)
        G = _scan_bwd(B, X)
        gcarry[...] = jnp.broadcast_to(a[0:1] * G[0:1], gcarry.shape)
        hprev = jnp.where(row == 0, hprev_ref[...][7:8] * has_prev, pltpu.roll(hv, 1, 0))
        da = G * hprev
        dmult = G * ig * xc
        dig = G * mult * xc
        dxc = G * mult * ig
        dlog_a = da * a - dmult * (a * a) / mult
        dzr = dlog_a * (-LRU_C * sp) * (r * (1.0 - r))
        dzi = dig * (ig * (1.0 - ig))
        dxc = dxc + _dot_nt(dzr, wa) + _dot_nt(dzi, wx)
        dwa_ref[...] += _dot_tn(xc, dzr)
        dwx_ref[...] += _dot_tn(xc, dzi)
        dba_ref[...] += _colsum(dzr)
        dbx_ref[...] += _colsum(dzi)
        dlam_ref[...] += _colsum(dlog_a * (-LRU_C * r)) * (-_sigmoid(-lam))
        dcb_ref[...] += _colsum(dxc)
        cw = cw_ref[...]
        ext = jnp.concatenate([dxc, dxc_head[...]], axis=0)
        dxb = dxc * cw[3:4]
        for k in range(3):
            dxb = dxb + pltpu.roll(ext, R + 8 - (3 - k), 0)[:R] * cw[k:k + 1]
        dxb_ref[...] = dxb.astype(dxb_ref.dtype)
        dxc_head[...] = dxc[0:8]
        extx = jnp.concatenate([xbprev_ref[...].astype(F32) * has_prev, xb_ref[...].astype(F32)], axis=0)
        incs = []
        for k in range(4):
            e = extx if k == 3 else pltpu.roll(extx, 3 - k, 0)
            incs.append(_colsum(dxc * e[8:]))
        dcw_ref[...] += jnp.concatenate(incs, axis=0)

    plain = pl.BlockSpec((R, LANES), lambda h, j: (ch(j), h))
    plain_prev = pl.BlockSpec((8, LANES), lambda h, j: (jnp.maximum(ch(j) * (R // 8) - 1, 0), h))
    return pl.pallas_call(
        body, name=tag + "_dlru", grid=(LRU_HEADS, nch),
        in_specs=[plain, col(PK_GB // LANES), plain, plain_prev, plain, col(PK_XB // LANES), prev(PK_XB // LANES),
                  cw_spec, wsq, vec, wsq, vec, vec],
        out_specs=[plain, plain, wsq, vec, wsq, vec, vec, cw_spec, vec],
        out_shape=[jax.ShapeDtypeStruct((S, D), BF), jax.ShapeDtypeStruct((S, D), BF),
                   jax.ShapeDtypeStruct((LRU_HEADS, LANES, LANES), F32), jax.ShapeDtypeStruct((1, D), F32),
                   jax.ShapeDtypeStruct((LRU_HEADS, LANES, LANES), F32), jax.ShapeDtypeStruct((1, D), F32),
                   jax.ShapeDtypeStruct((1, D), F32), jax.ShapeDtypeStruct((4, D), F32), jax.ShapeDtypeStruct((1, D), F32)],
        scratch_shapes=[pltpu.VMEM((8, LANES), F32), pltpu.VMEM((8, LANES), F32)], compiler_params=_cparams(2),
    )(dyb0, P, hh, hh, xc, P, P, conv_w, w_a, b_a, w_x, b_x, lam)


CUM_TS = 512
FLASH_T = 512
NEG = -1e30


def cumlogf_fwd(fl, b_f, tag):
    ts = min(CUM_TS, fl.shape[0])

    def fn(step, i, n, fl, bf, carry):
        x = -_softplus(-(fl + bf))
        row = _iota(x.shape, 0)
        d = 1
        while d < ts:
            x = x + jnp.where(row >= d, pltpu.roll(x, d, 0), 0.0)
            d *= 2

        @pl.when(step == 0)
        def _():
            carry[...] = jnp.zeros_like(carry)

        c = x + carry[0:1]
        carry[...] = jnp.broadcast_to(c[ts - 1:ts], carry.shape)
        return (c,), ()

    return rowwise(fn, [fl], [b_f], [(LANES, F32)], scratch=[pltpu.VMEM((8, LANES), F32)], ts=ts, name=tag + "_cum")[0][0]


def cumlogf_bwd(dc, fl, b_f, tag):
    ts = min(CUM_TS, fl.shape[0])

    def fn(step, i, n, dc, fl, bf, carry):
        row = _iota(dc.shape, 0)
        x = dc
        d = 1
        while d < ts:
            x = x + jnp.where(row < ts - d, pltpu.roll(x, ts - d, 0), 0.0)
            d *= 2

        @pl.when(step == 0)
        def _():
            carry[...] = jnp.zeros_like(carry)

        g = x + carry[0:1]
        carry[...] = jnp.broadcast_to(g[0:1], carry.shape)
        dfl = g * _sigmoid(-(fl + bf))
        return (dfl,), (_colsum(dfl),)

    (dfl,), (db,) = rowwise(fn, [dc, fl], [b_f], [(LANES, F32)], [((1, LANES), F32)], scratch=[pltpu.VMEM((8, LANES), F32)],
                            ts=ts, name=tag + "_dcum", reverse=True)
    return dfl, db


FOX_KA = 80


def _split3(c):
    c = c.astype(F32)
    hi = lax.reduce_precision(c, 8, 7)
    r = c - hi
    mid = lax.reduce_precision(r, 8, 7)
    return hi.astype(BF), mid.astype(BF), (r - mid).astype(BF)


def fox_operands(q, k, ch):
    H, S, hd = q.shape
    qs = q * (FOX_HD ** -0.5)
    pieces = [p.astype(F32) for p in _split3(ch)]
    pad = FOX_KA - hd

    def unit(i):
        return (jnp.arange(FOX_KA) == hd + i).astype(F32)

    qa = jnp.pad(qs.transpose(0, 2, 1).astype(F32), ((0, 0), (0, pad), (0, 0)))
    ka = jnp.pad(k.astype(F32), ((0, 0), (0, 0), (0, pad)))
    for i, p in enumerate(pieces):
        qa = qa + p[:, None, :] * unit(i)[None, :, None] + unit(3 + i)[None, :, None]
        ka = ka + unit(i)[None, None, :] - p[:, :, None] * unit(3 + i)[None, None, :]
    return qa.astype(BF), ka.astype(BF), qs


def _causal(s):
    return jnp.where(_iota(s.shape, 0) <= _iota(s.shape, 1), s, NEG)


def flash_fwd(qa, ka, vt, tag):
    H, hd, S = vt.shape
    T = min(FLASH_T, S)
    nb = S // T

    def body(ka_ref, qa_ref, vt_ref, o_ref, lse_ref, m_s, l_s, acc):
        qi, ki = pl.program_id(1), pl.program_id(2)

        @pl.when(ki == 0)
        def _():
            m_s[...] = jnp.full_like(m_s, NEG)
            l_s[...] = jnp.zeros_like(l_s)
            acc[...] = jnp.zeros_like(acc)

        def step(diagonal):
            s = _dot_nn(ka_ref[...], qa_ref[...])
            if diagonal:
                s = _causal(s)
            m_new = jnp.maximum(m_s[...], jnp.max(s, axis=0, keepdims=True))
            alpha = jnp.exp(m_s[...] - m_new)
            p = jnp.exp(s - m_new)
            l_s[...] = alpha * l_s[...] + jnp.sum(p, axis=0, keepdims=True)
            p_hi = p.astype(BF)
            p_lo = p - p_hi.astype(F32)
            v = vt_ref[...]
            acc[...] = alpha * acc[...] + (_dot_nn(v, p_hi) + _dot_nn(v, p_lo))
            m_s[...] = m_new

        @pl.when(ki < qi)
        def _():
            step(False)

        @pl.when(ki == qi)
        def _():
            step(True)
            o_ref[...] = acc[...] / l_s[...]
            lse_ref[...] = m_s[...] + jnp.log(l_s[...])

    kblk = lambda h, qi, ki: (h, jnp.minimum(ki, qi), 0)
    return pl.pallas_call(
        body, name=tag + "_flash", grid=(H, nb, nb),
        in_specs=[pl.BlockSpec((None, T, FOX_KA), kblk),
                  pl.BlockSpec((None, FOX_KA, T), lambda h, qi, ki: (h, 0, qi)),
                  pl.BlockSpec((None, hd, T), lambda h, qi, ki: (h, 0, jnp.minimum(ki, qi)))],
        out_specs=[pl.BlockSpec((None, hd, T), lambda h, qi, ki: (h, 0, qi)),
                   pl.BlockSpec((None, 1, T), lambda h, qi, ki: (h, 0, qi))],
        out_shape=[jax.ShapeDtypeStruct((H, hd, S), F32), jax.ShapeDtypeStruct((H, 1, S), F32)],
        scratch_shapes=[pltpu.VMEM((1, T), F32), pltpu.VMEM((1, T), F32), pltpu.VMEM((hd, T), F32)],
        compiler_params=_cparams(3))(ka, qa, vt)


def flash_bwd(qa, ka, qs, kt, v, ot, do, dot_, lse, tag):
    H, S, hd = v.shape
    T = min(FLASH_T, S)
    nb = S // T

    def body(ka_ref, qa_ref, qs_ref, kt_ref, v_ref, ot_ref, do_ref, dot_ref, lse_ref, dqt_ref, dk_ref, dv_ref, dc_ref,
             dk_acc, dv_acc, dc_acc):
        ki, qi = pl.program_id(1), pl.program_id(2)

        @pl.when(jnp.logical_and(ki == 0, qi == 0))
        def _():
            dqt_ref[...] = jnp.zeros_like(dqt_ref)

        @pl.when(qi == 0)
        def _():
            dk_acc[...] = jnp.zeros_like(dk_acc)
            dv_acc[...] = jnp.zeros_like(dv_acc)
            dc_acc[...] = jnp.zeros_like(dc_acc)

        def step(diagonal):
            s = _dot_nn(ka_ref[...], qa_ref[...])
            if diagonal:
                s = _causal(s)
            p = jnp.exp(s - lse_ref[...])
            dot_v = dot_ref[...]
            dp = _dot_nn(v_ref[...], dot_v)
            delta = jnp.sum(dot_v.astype(F32) * ot_ref[...], axis=0, keepdims=True)
            ds = p * (dp - delta)
            part = ds[:, 0:LANES]
            for g in range(1, T // LANES):
                part = part + ds[:, g * LANES:(g + 1) * LANES]
            dc_acc[...] += part
            dsb = ds.astype(BF)
            dv_acc[...] += _dot_nn(p, do_ref[...])
            dk_acc[...] += _dot_nn(dsb, qs_ref[...])
            dqt_ref[qi] += _dot_nn(kt_ref[...], dsb)

        @pl.when(qi > ki)
        def _():
            step(False)

        @pl.when(qi == ki)
        def _():
            step(True)

        @pl.when(qi == nb - 1)
        def _():
            dk_ref[...] = dk_acc[...]
            dv_ref[...] = dv_acc[...]
            dc_ref[...] = -jnp.sum(dc_acc[...], axis=1, keepdims=True)

    qblk = lambda h, ki, qi: (h, jnp.maximum(qi, ki), 0)
    qblk_t = lambda h, ki, qi: (h, 0, jnp.maximum(qi, ki))
    kblk = lambda h, ki, qi: (h, ki, 0)
    kblk_t = lambda h, ki, qi: (h, 0, ki)
    return pl.pallas_call(
        body, name=tag + "_dflash", grid=(H, nb, nb),
        in_specs=[pl.BlockSpec((None, T, FOX_KA), kblk), pl.BlockSpec((None, FOX_KA, T), qblk_t),
                  pl.BlockSpec((None, T, hd), qblk), pl.BlockSpec((None, hd, T), kblk_t), pl.BlockSpec((None, T, hd), kblk),
                  pl.BlockSpec((None, hd, T), qblk_t), pl.BlockSpec((None, T, hd), qblk), pl.BlockSpec((None, hd, T), qblk_t),
                  pl.BlockSpec((None, 1, T), qblk_t)],
        out_specs=[pl.BlockSpec((None, nb, hd, T), lambda h, ki, qi: (h, 0, 0, 0)), pl.BlockSpec((None, T, hd), kblk),
                   pl.BlockSpec((None, T, hd), kblk), pl.BlockSpec((None, T, 1), kblk)],
        out_shape=[jax.ShapeDtypeStruct((H, nb, hd, T), F32), jax.ShapeDtypeStruct((H, S, hd), F32),
                   jax.ShapeDtypeStruct((H, S, hd), F32), jax.ShapeDtypeStruct((H, S, 1), F32)],
        scratch_shapes=[pltpu.VMEM((T, hd), F32), pltpu.VMEM((T, hd), F32), pltpu.VMEM((T, LANES), F32)],
        compiler_params=_cparams(3))(ka, qa, qs, kt, v, ot, do, dot_, lse)


def _to_heads(x2d):
    S = x2d.shape[0]
    return x2d.reshape(S, FOX_HEADS, FOX_HD).transpose(1, 0, 2)


def _from_heads(x3d):
    S = x3d.shape[1]
    return x3d.transpose(1, 0, 2).reshape(S, FOX_W)


def _gl_rows(P):
    return [Rows(P, D, PK_GL // D + k) for k in range(3)]


def merge_fwd(P, ya, yb, yc, b_gate, tag):
    def fn(step, i, n, g0, g1, g2, ya, yb, yc, b):
        out = 0.0
        for k, (gl, y) in enumerate(((g0, ya), (g1, yb), (g2, yc))):
            out = out + _sigmoid(gl.astype(F32) + b[:, k * D:(k + 1) * D]) * y.astype(F32)
        return (out,), ()
    return rowwise(fn, _gl_rows(P) + [ya, yb, yc], [b_gate], [(D, BF)], ts=512, name=tag + "_merge")[0][0]


def merge_bwd(dm, P, ya, yb, yc, b_gate, tag):
    def fn(step, i, n, dm, g0, g1, g2, ya, yb, yc, b):
        dm = dm.astype(F32)
        dys, dgls = [], []
        for k, (gl, y) in enumerate(((g0, ya), (g1, yb), (g2, yc))):
            g = _sigmoid(gl.astype(F32) + b[:, k * D:(k + 1) * D])
            dys.append(dm * g)
            dgls.append(dm * y.astype(F32) * (g * (1.0 - g)))
        dgl = jnp.concatenate(dgls, axis=1)
        return (dys[0], dys[1], dys[2], dgl), (_colsum(dgl),)
    (dya, dyb, dyc, dgl), (db,) = rowwise(
        fn, [dm] + _gl_rows(P) + [ya, yb, yc], [b_gate], [(D, BF), (D, BF), (D, BF), (3 * D, BF)], [((1, 3 * D), F32)],
        ts=256, name=tag + "_dmerge")
    return dya, dyb, dyc, dgl, db


def _xattn_probs(qh, kh):
    s = _dot_nt(qh, kh) * (X_HD ** -0.5)
    e = jnp.exp(s - jnp.max(s, axis=1, keepdims=True))
    return e / jnp.sum(e, axis=1, keepdims=True)


def xattn_fwd(q, kv, tag):
    def fn(step, i, n, q, k, v):
        os = []
        for h in range(X_HEADS):
            sl = slice(h * X_HD, (h + 1) * X_HD)
            os.append(_dot_nn(_xattn_probs(q[:, sl], k[:, sl]), v[:, sl]))
        return (jnp.concatenate(os, axis=1),), ()

    return _xattn_call(fn, [q], kv, [(D, BF)], [], tag + "_xattn")[0][0]


def _xattn_call(fn, rows, kv, outs, accs, name):
    S, ts, M = rows[0].shape[0], 512, kv.shape[0]
    ts = min(ts, S)
    nr, no, na = len(rows), len(outs), len(accs)

    def body(*refs):
        in_refs, out_refs, acc_refs = refs[:nr + 2], refs[nr + 2:nr + 2 + no], refs[nr + 2 + no:]
        step = pl.program_id(0)
        o, inc = fn(step, step, S // ts, *[r[...] for r in in_refs])
        for ref, val in zip(out_refs, o):
            ref[...] = val.astype(ref.dtype)
        if na:
            @pl.when(step == 0)
            def _():
                for ref in acc_refs:
                    ref[...] = jnp.zeros_like(ref)
            for ref, val in zip(acc_refs, inc):
                ref[...] += val

    row = pl.BlockSpec((ts, D), lambda s: (s, 0))
    res = pl.pallas_call(
        body, name=name, grid=(S // ts,),
        in_specs=[row] * nr + [pl.BlockSpec((M, D), lambda s: (0, 0)), pl.BlockSpec((M, D), lambda s: (0, 1))],
        out_specs=[row] * no + [pl.BlockSpec(tuple(s), lambda s_: (0, 0)) for s, _ in accs],
        out_shape=[jax.ShapeDtypeStruct((S, w), dt) for w, dt in outs] + [jax.ShapeDtypeStruct(tuple(s), dt) for s, dt in accs],
        compiler_params=_cparams(1))(*rows, kv, kv)
    return res[:no], res[no:]


def xattn_bwd(q, kv, do, tag):
    def fn(step, i, n, q, do, k, v):
        dqs, dks, dvs = [], [], []
        for h in range(X_HEADS):
            sl = slice(h * X_HD, (h + 1) * X_HD)
            p = _xattn_probs(q[:, sl], k[:, sl])
            dp = _dot_nt(do[:, sl], v[:, sl])
            ds = (p * (dp - jnp.sum(p * dp, axis=1, keepdims=True)) * (X_HD ** -0.5)).astype(BF)
            dqs.append(_dot_nn(ds, k[:, sl]))
            dks.append(_dot_tn(ds, q[:, sl]))
            dvs.append(_dot_tn(p, do[:, sl]))
        return (jnp.concatenate(dqs, axis=1),), (jnp.concatenate(dks + dvs, axis=1),)

    (dq,), (dkv,) = _xattn_call(fn, [q, do], kv, [(D, BF)], [((kv.shape[0], 2 * D), F32)], tag + "_dxattn")
    return dq, dkv


def layer_fwd(x, mem, w, tag):
    x1, s_ffn1 = ffn_fwd(x, w["g_ffn1"], w["ffn1_in"], w["ffn1_out"], tag + "_ffn1")
    u = rms_fwd(x1, w["g_mix"], name=tag + "_mixrms")
    P = matmul(u, w["wp"], "nn", name=tag + "_proj")
    fl = matmul(u, w["wfl"], "nn", out_dtype=F32, name=tag + "_projf")
    d, ya0 = pool_fwd(P, w["w_pool"], w["pool_scale"], tag)
    ya = matmul(ya0, w["up_a"], "nn", name=tag + "_upa")
    xc, hh, yb0 = lru_fwd(P, w["conv_w"], w["conv_b"], w["w_rg_a"], w["b_rg_a"], w["w_rg_x"], w["b_rg_x"], w["lam"], tag)
    yb = matmul(yb0, w["up_b"], "nn", name=tag + "_upb")
    c = cumlogf_fwd(fl, w["b_f"], tag)
    q, k, v = (_to_heads(P[:, o:o + FOX_W]) for o in (PK_Q, PK_K, PK_V))
    qa, ka, qs = fox_operands(q, k, c[:, :FOX_HEADS].T)
    ot, lse = flash_fwd(qa, ka, v.transpose(0, 2, 1), tag)
    o2 = ot.transpose(2, 0, 1).reshape(-1, FOX_W)
    yc = matmul(o2, w["up_c"], "nn", name=tag + "_upc")
    merged = merge_fwd(P, ya, yb, yc, w["b_gate"], tag)
    x2 = matmul(merged, w["w_o"], "nn", out_dtype=F32, res=x1, name=tag + "_wo")
    hq = rms_fwd(x2, w["g_cross"], name=tag + "_xrms")
    qx = matmul(hq, w["xq"], "nn", name=tag + "_xq")
    mn = rms_fwd(mem, w["g_mem"], name=tag + "_mrms")
    kv = matmul(mn, w["xkv"], "nn", name=tag + "_xkv")
    ox = xattn_fwd(qx, kv, tag)
    x3 = matmul(ox, w["xo"], "nn", out_dtype=F32, res=x2, name=tag + "_xo")
    x4, s_ffn2 = ffn_fwd(x3, w["g_ffn2"], w["ffn2_in"], w["ffn2_out"], tag + "_ffn2")
    saved = dict(ffn1=s_ffn1, ffn2=s_ffn2, x1=x1, u=u, P=P, fl=fl, d=d, ya0=ya0, ya=ya, xc=xc, hh=hh, yb0=yb0, yb=yb,
                 qa=qa, ka=ka, qs=qs, k=k, v=v, ot=ot, lse=lse, o2=o2, yc=yc, merged=merged, x2=x2, hq=hq, qx=qx,
                 mn=mn, kv=kv, ox=ox)
    return x4, saved


def layer_bwd(dx4, mem, w, s, tag):
    g = {}
    dx3, g["g_ffn2"], g["w_ffn2_in"], g["w_ffn2_out"] = ffn_bwd(dx4, s["ffn2"], w["g_ffn2"], w["ffn2_in"], w["ffn2_out"], tag + "_ffn2")
    dox = matmul(dx3, w["xo"], "nt", name=tag + "_dox")
    g["w_xo"] = matmul(s["ox"], dx3, "tn", name=tag + "_dwxo")
    dqx, dkv = xattn_bwd(s["qx"], s["kv"], dox, tag)
    g["w_xq"] = matmul(s["hq"], dqx, "tn", name=tag + "_dwxq")
    dhq = matmul(dqx, w["xq"], "nt", name=tag + "_dhq")
    dx2, g["g_cross"] = rms_bwd(s["x2"], w["g_cross"], dhq, dx3, name=tag + "_dxrms")
    g["w_xkv"] = matmul(s["mn"], dkv, "tn", name=tag + "_dwxkv")
    dmn = matmul(dkv, w["xkv"], "nt", name=tag + "_dmn")
    _, g["g_mem"] = rms_bwd(mem, w["g_mem"], dmn, None, name=tag + "_dmrms")
    P = s["P"]
    dmerged = matmul(dx2, w["w_o"], "nt", name=tag + "_dmerged")
    g["w_o"] = matmul(s["merged"], dx2, "tn", name=tag + "_dwo")
    dya, dyb, dyc, dgl, g["b_gate"] = merge_bwd(dmerged, P, s["ya"], s["yb"], s["yc"], w["b_gate"], tag)
    dya0 = matmul(dya, w["up_a"], "nt", name=tag + "_dya0")
    g["w_up_a"] = matmul(s["ya0"], dya, "tn", name=tag + "_dwupa")
    dxa, g["w_pool"], g["pool_scale"] = pool_bwd(dya0, s["d"], w["w_pool"], w["pool_scale"], tag)
    dyb0 = matmul(dyb, w["up_b"], "nt", name=tag + "_dyb0")
    g["w_up_b"] = matmul(s["yb0"], dyb, "tn", name=tag + "_dwupb")
    (dgb, dxb, g["w_rg_a"], g["b_rg_a"], g["w_rg_x"], g["b_rg_x"], g["lru_lambda"], g["conv_w"], g["conv_b"]) = lru_bwd(
        dyb0, P, s["hh"], s["xc"], w["conv_w"], w["w_rg_a"], w["b_rg_a"], w["w_rg_x"], w["b_rg_x"], w["lam"], tag)
    do2 = matmul(dyc, w["up_c"], "nt", name=tag + "_do2")
    g["w_up_c"] = matmul(s["o2"], dyc, "tn", name=tag + "_dwupc")
    do = _to_heads(do2)
    dqt, dk, dv, dck = flash_bwd(s["qa"], s["ka"], s["qs"], s["k"].transpose(0, 2, 1), s["v"], s["ot"], do,
                                 do.transpose(0, 2, 1), s["lse"], tag)
    dq = (dqt * (FOX_HD ** -0.5)).transpose(0, 1, 3, 2).reshape(dk.shape)
    dc = jnp.pad(dck[:, :, 0].T, ((0, 0), (0, LANES - FOX_HEADS)))
    dfl, g["b_f"] = cumlogf_bwd(dc, s["fl"], w["b_f"], tag)
    dP = jnp.concatenate([dxb, dgb, dgl, dxa] + [_from_heads(t).astype(BF) for t in (dq, dk, dv)], axis=1)
    du = matmul(dP, w["wp"], "nt", out_dtype=F32, name=tag + "_du")
    du = matmul(dfl, w["wfl"], "nt", out_dtype=F32, res=du, name=tag + "_duf")
    g["wp"] = matmul(s["u"], dP, "tn", name=tag + "_dwp")
    g["wfl"] = matmul(s["u"], dfl, "tn", out_dtype=F32, name=tag + "_dwfl")
    dx1, g["g_mix"] = rms_bwd(s["x1"], w["g_mix"], du, dx2, name=tag + "_dmixrms")
    dx0, g["g_ffn1"], g["w_ffn1_in"], g["w_ffn1_out"] = ffn_bwd(dx1, s["ffn1"], w["g_ffn1"], w["ffn1_in"], w["ffn1_out"], tag + "_ffn1")
    return dx0, g


def loss_head(x, target, g_final):
    def fn(step, i, n, x, t, g):
        r = _rstd(x)
        xhat = x * r
        e = xhat * g - t
        dy = e * (1.0 / D)
        gd = dy * g
        dx = r * (gd - xhat * jnp.mean(xhat * gd, axis=-1, keepdims=True))
        loss = 0.5 * jnp.sum(jnp.mean(e * e, axis=-1, keepdims=True), axis=0, keepdims=True)
        return (dx,), (jnp.broadcast_to(loss, (1, LANES)), _colsum(dy * xhat))

    (dx,), (loss, dg) = rowwise(fn, [x, target], [g_final], [(D, F32)], [((1, LANES), F32), ((1, D), F32)], ts=512, name="loss_head")
    return loss[0, 0], dx, dg


BIG = (("w_ffn1_in", (D, 2 * DFF), 1), ("w_ffn1_out", (DFF, D), 0), ("w_in", (D, 7176), 1), ("w_up_a", (POOL_W, D), 1),
       ("conv_w", (4, D), 1), ("w_up_b", (D, D), 0), ("w_up_c", (FOX_W, D), 1), ("w_o", (D, D), 0), ("w_xq", (D, D), 0),
       ("w_xkv", (D, 2 * D), 1), ("w_xo", (D, D), 0), ("w_ffn2_in", (D, 2 * DFF), 1), ("w_ffn2_out", (DFF, D), 0))
SMALL = (("g_ffn1", (D,)), ("g_mix", (D,)), ("b_f", (FOX_HEADS,)), ("b_gate", (3 * D,)), ("w_pool", (4, LANES, LANES)),
         ("pool_scale", (POOL_W,)), ("conv_b", (D,)), ("w_rg_a", (LRU_HEADS, LANES, LANES)), ("b_rg_a", (D,)),
         ("w_rg_x", (LRU_HEADS, LANES, LANES)), ("b_rg_x", (D,)), ("lru_lambda", (D,)), ("g_cross", (D,)), ("g_mem", (D,)),
         ("g_ffn2", (D,)))
ORDER = ("g_ffn1", "w_ffn1_in", "w_ffn1_out", "g_mix", "w_in", "b_f", "b_gate", "w_pool", "pool_scale", "w_up_a", "conv_w",
         "conv_b", "w_rg_a", "b_rg_a", "w_rg_x", "b_rg_x", "lru_lambda", "w_up_b", "w_up_c", "w_o", "g_cross", "g_mem", "w_xq",
         "w_xkv", "w_xo", "g_ffn2", "w_ffn2_in", "w_ffn2_out", "g_final")

IN_SPLIT = (("xa", 0, 512), ("xb", 512, 1024), ("gb", 1536, 1024), ("q", 2560, 512), ("k", 3072, 512), ("v", 3584, 512),
            ("fl", 4096, 8), ("gl", 4104, 3072))
PACK_ORDER = ("xb", "gb", "gl", "xa", "q", "k", "v")


def _shard_shape(shape, axis):
    s = list(shape)
    s[axis] //= N_CHIPS
    return (DEPTH, *s)


def _round_up(n, m):
    return -(-n // m) * m


def _pack_rows(parts, dtype, row_multiple):
    flat = jnp.concatenate([p.astype(dtype).reshape(-1) for p in parts])
    rows = _round_up(-(-flat.shape[0] // D), row_multiple)
    return jnp.pad(flat, (0, rows * D - flat.shape[0])).reshape(rows, D)


def _n_rows(shape, axis):
    return math.prod(_shard_shape(shape, axis)) // D


def _pad_tile_rows(x, axis):
    widths = [(0, 0)] * x.ndim
    widths[axis] = (0, _round_up(x.shape[axis], DMA_TILE_ROWS) - x.shape[axis])
    return jnp.pad(x, widths)


def _unshard(seg, shape, axis):
    t = jnp.moveaxis(seg, 0, axis + 1)
    return t.reshape(DEPTH, *shape)


def _shard_split(full, shape, axis):
    sh = list(shape)
    sh[axis:axis + 1] = [N_CHIPS, shape[axis] // N_CHIPS]
    t = full.reshape(DEPTH, *sh)
    t = jnp.moveaxis(t, axis + 1, 0)
    return t.reshape(N_CHIPS, -1, D)


def _gathered_weights(gath):
    out, off = {}, 0
    for name, shape, axis in BIG:
        sh = _shard_shape(shape, axis)
        nr = _n_rows(shape, axis)
        seg = gath[:, off:off + nr].reshape(N_CHIPS, *sh)
        off += _round_up(nr, DMA_TILE_ROWS)
        if name == "conv_w":
            for _ in range(2):
                seg = seg.astype(F32) + gath[:, off:off + nr].reshape(N_CHIPS, *sh).astype(F32)
                off += _round_up(nr, DMA_TILE_ROWS)
        out[name] = _unshard(seg, shape, axis)
    return out


def _layer_weights(full, small, l):
    w_in = full["w_in"][l]
    cols = {n: w_in[:, o:o + s] for n, o, s in IN_SPLIT}
    w = dict(ffn1_in=full["w_ffn1_in"][l], ffn1_out=full["w_ffn1_out"][l], ffn2_in=full["w_ffn2_in"][l], ffn2_out=full["w_ffn2_out"][l],
             wp=jnp.concatenate([cols[n] for n in PACK_ORDER], axis=1),
             wfl=jnp.pad(cols["fl"], ((0, 0), (0, LANES - FOX_HEADS))),
             up_a=full["w_up_a"][l], up_b=full["w_up_b"][l], up_c=full["w_up_c"][l], w_o=full["w_o"][l], xq=full["w_xq"][l],
             xkv=full["w_xkv"][l], xo=full["w_xo"][l], conv_w=full["conv_w"][l])
    for n in ("g_ffn1", "g_mix", "b_gate", "pool_scale", "conv_b", "b_rg_a", "b_rg_x", "g_cross", "g_mem", "g_ffn2"):
        w[n] = small[n][l].reshape(1, -1)
    w["lam"] = small["lru_lambda"][l].reshape(1, -1)
    w["b_f"] = jnp.pad(small["b_f"][l], (0, LANES - FOX_HEADS)).reshape(1, LANES)
    for n in ("w_pool", "w_rg_a", "w_rg_x"):
        w[n] = small[n][l]
    return w


def _unpack_w_in_grad(gwp, gwfl):
    pk = {}
    off = 0
    sizes = {n: s for n, _, s in IN_SPLIT}
    for n in PACK_ORDER:
        pk[n] = gwp[:, off:off + sizes[n]]
        off += sizes[n]
    pk["fl"] = gwfl[:, :FOX_HEADS].astype(gwp.dtype)
    return jnp.concatenate([pk[n] for n, _, _ in IN_SPLIT], axis=1)


def _place():
    x, y, c = lax.axis_index("x"), lax.axis_index("y"), lax.axis_index("c")
    others = [(1 - x, y), (x, 1 - y), (1 - x, 1 - y)]
    return x, y, c, others


def _rcopy(src, dst, send_sems, recv_sems, k, to):
    return pltpu.make_async_remote_copy(src_ref=src, dst_ref=dst, send_sem=send_sems.at[k], recv_sem=recv_sems.at[k],
                                        device_id=to, device_id_type=MESH)


DMA_TILE_ROWS = 16


def _cut_rows(ref, n):
    sz = ref.shape[0] // n
    return [ref.at[pl.ds(i * sz, sz)] for i in range(n)]


def _n_cuts(rows, most=8):
    n = most
    while n > 1 and rows % (n * DMA_TILE_ROWS):
        n //= 2
    return n


class _Striped:
    def __init__(self, pairs, whole, sems, to=None):
        if to is None:
            make = lambda s, d: pltpu.make_async_copy(s, d, sems[0])
        else:
            make = lambda s, d: pltpu.make_async_remote_copy(src_ref=s, dst_ref=d, send_sem=sems[0], recv_sem=sems[1],
                                                             device_id=to, device_id_type=MESH)
        self.parts = [make(s, d) for s, d in pairs]
        self.whole = make(*whole)

    def start(self):
        for p in self.parts:
            p.start()

    def wait(self):
        self.whole.wait()

    def wait_send(self):
        self.whole.wait_send()

    def wait_recv(self):
        self.whole.wait_recv()


def _striped_rows(src, dst, sems, to=None, most=8):
    n = _n_cuts(src.shape[0], most)
    return _Striped(list(zip(_cut_rows(src, n), _cut_rows(dst, n))), (src, dst), sems, to)


def _comm_call(body, name, ins, out_shape, n_sems):
    return pl.pallas_call(
        body, name=name, in_specs=[ANY] * len(ins), out_specs=[ANY] * len(out_shape), out_shape=out_shape,
        scratch_shapes=[pltpu.SemaphoreType.DMA((n_sems,)), pltpu.SemaphoreType.DMA((n_sems,)), pltpu.SemaphoreType.DMA((n_sems,))],
    )(*ins)


def allgather_chips(mine):
    rows = mine.shape[0]
    half = rows // 2
    nc = _n_cuts(half)
    piece = half // nc

    def body(x_ref, out_ref, send_sems, recv_sems, local_sems):
        x, y, c, others = _place()
        me = 2 * x + y

        def part(chip, hc, p):
            return out_ref.at[chip, pl.ds(pl.multiple_of(hc * half + p * piece, DMA_TILE_ROWS), piece), :]

        local = _striped_rows(x_ref, out_ref.at[me], (local_sems.at[0],))
        local.start()
        first, passed = [], []
        for p in range(nc):
            src = x_ref.at[pl.ds(pl.multiple_of(c * half + p * piece, DMA_TILE_ROWS), piece), :]
            for j, (ox, oy) in enumerate(others):
                cp = _rcopy(src, part(me, c, p), send_sems, recv_sems, j * nc + p, (ox, oy, c))
                cp.start()
                first.append(cp)
        for p in range(nc):
            for j, (ox, oy) in enumerate(others):
                theirs = part(2 * ox + oy, c, p)
                _rcopy(theirs, theirs, send_sems, recv_sems, j * nc + p, (ox, oy, c)).wait_recv()
                cp = _rcopy(theirs, theirs, send_sems, recv_sems, (3 + j) * nc + p, (x, y, 1 - c))
                cp.start()
                passed.append(cp)
        for p in range(nc):
            for j, (ox, oy) in enumerate(others):
                theirs = part(2 * ox + oy, 1 - c, p)
                _rcopy(theirs, theirs, send_sems, recv_sems, (3 + j) * nc + p, (x, y, 1 - c)).wait_recv()
        for cp in first + passed:
            cp.wait_send()
        local.wait()

    return _comm_call(body, "allgather_chips", [mine], [jax.ShapeDtypeStruct((N_CHIPS, rows, D), mine.dtype)], 6 * nc)[0]


def allgather_same_core(bufs):
    nb = len(bufs)

    def body(*refs):
        ins, outs = refs[:nb], refs[nb:2 * nb]
        send_sems, recv_sems, local_sems = refs[2 * nb:]
        x, y, c, others = _place()
        me = 2 * x + y
        copies = []
        for b in range(nb):
            local = pltpu.make_async_copy(ins[b], outs[b].at[me], local_sems.at[b])
            local.start()
            copies.append(local)
        sends = []
        for b in range(nb):
            for j, (ox, oy) in enumerate(others):
                cp = _rcopy(ins[b], outs[b].at[me], send_sems, recv_sems, 3 * b + j, (ox, oy, c))
                cp.start()
                sends.append(cp)
        for b in range(nb):
            for j, (ox, oy) in enumerate(others):
                theirs = outs[b].at[2 * ox + oy]
                _rcopy(theirs, theirs, send_sems, recv_sems, 3 * b + j, (ox, oy, c)).wait_recv()
        for cp in sends:
            cp.wait_send()
        for cp in copies:
            cp.wait()

    shapes = [jax.ShapeDtypeStruct((N_CHIPS, *b.shape), b.dtype) for b in bufs]
    return _comm_call(body, "allgather_same_core", list(bufs), shapes, 3 * nb)


def sibling_halves(bufs):
    nb = len(bufs)

    def body(*refs):
        ins, outs = refs[:nb], refs[nb:3 * nb]
        send_sems, recv_sems, local_sems = refs[3 * nb:]
        x, y, c, _ = _place()
        work = []
        for b in range(nb):
            n = _n_cuts(ins[b].shape[2], 4)

            def pieces(hc, out):
                return [pair for chip in range(N_CHIPS)
                        for pair in zip(_cut_rows(ins[b].at[chip, hc], n), _cut_rows(out.at[chip], n))]

            keep = _Striped(pieces(c, outs[2 * b]), (ins[b].at[:, c], outs[2 * b]), (local_sems.at[b],))
            keep.start()
            cp = _Striped(pieces(1 - c, outs[2 * b + 1]), (ins[b].at[:, 1 - c], outs[2 * b + 1]),
                          (send_sems.at[b], recv_sems.at[b]), to=(x, y, 1 - c))
            cp.start()
            work.append((keep, cp))
        for keep, cp in work:
            cp.wait()
            keep.wait()

    shapes = []
    for b in bufs:
        s = jax.ShapeDtypeStruct((N_CHIPS, *b.shape[2:]), b.dtype)
        shapes += [s, s]
    return _comm_call(body, "sibling_halves", list(bufs), shapes, nb)


def chip_exchange(bufs):
    nb = len(bufs)

    def body(*refs):
        ins, outs = refs[:nb], refs[nb:3 * nb]
        send_sems, recv_sems, local_sems = refs[3 * nb:]
        x, y, c, others = _place()
        me = 2 * x + y
        keeps, sends = [], []
        for b in range(nb):
            keep = _striped_rows(ins[b].at[me], outs[2 * b], (local_sems.at[b],))
            keep.start()
            keeps.append(keep)
            for j, (ox, oy) in enumerate(others):
                k = 3 * b + j
                cp = _striped_rows(ins[b].at[2 * ox + oy], outs[2 * b + 1].at[j], (send_sems.at[k], recv_sems.at[k]),
                                   to=(ox, oy, c), most=4)
                cp.start()
                sends.append(cp)
        for cp in sends:
            cp.wait()
        for keep in keeps:
            keep.wait()

    shapes = []
    for b in bufs:
        shapes += [jax.ShapeDtypeStruct(b.shape[1:], b.dtype), jax.ShapeDtypeStruct((3, *b.shape[1:]), b.dtype)]
    return _comm_call(body, "chip_exchange", list(bufs), shapes, 3 * nb)


def sibling_allgather(bufs):
    nb = len(bufs)

    def body(*refs):
        ins, outs = refs[:nb], refs[nb:2 * nb]
        send_sems, recv_sems, local_sems = refs[2 * nb:]
        x, y, c, _ = _place()
        work = []
        for b in range(nb):
            keep = _striped_rows(ins[b], outs[b].at[c], (local_sems.at[b],))
            keep.start()
            cp = _striped_rows(ins[b], outs[b].at[c], (send_sems.at[b], recv_sems.at[b]), to=(x, y, 1 - c), most=16)
            cp.start()
            work.append((keep, cp))
        for b, (keep, cp) in enumerate(work):
            cp.wait_send()
            theirs = outs[b].at[1 - c]
            _rcopy(theirs, theirs, send_sems, recv_sems, b, (x, y, 1 - c)).wait_recv()
            keep.wait()

    shapes = [jax.ShapeDtypeStruct((2, *b.shape), b.dtype) for b in bufs]
    return _comm_call(body, "sibling_allgather", list(bufs), shapes, nb)


def add_rows(parts, out_dtype, name):
    def fn(step, i, n, *vals):
        acc = vals[0].astype(F32)
        for v in vals[1:]:
            acc = acc + v.astype(F32)
        return (acc,), ()
    rows = parts[0].shape[0]
    return rowwise(fn, list(parts), outs=[(D, out_dtype)], ts=_pick(rows, (1024, 512, 256, 128, 64, 32, 16, 8)), name=name)[0][0]


def reduce_scatter(big, small):
    rb, rs = big.shape[1], small.shape[1]
    halves = sibling_halves([big.reshape(N_CHIPS, 2, rb // 2, D), small.reshape(N_CHIPS, 2, rs // 2, D)])
    pb = add_rows([h.reshape(-1, D) for h in halves[0:2]], BF, "rs_add_sibling_big").reshape(N_CHIPS, rb // 2, D)
    ps = add_rows([h.reshape(-1, D) for h in halves[2:4]], F32, "rs_add_sibling_small").reshape(N_CHIPS, rs // 2, D)
    own_b, recv_b, own_s, recv_s = chip_exchange([pb, ps])
    qb = add_rows([own_b, recv_b[0], recv_b[1], recv_b[2]], F32, "rs_add_chips_big")
    qs = add_rows([own_s, recv_s[0], recv_s[1], recv_s[2]], F32, "rs_add_chips_small")
    fb, fs = sibling_allgather([qb, qs])
    return fb.reshape(rb, D), fs.reshape(rs, D)


_BC1 = 1.0 - ADAM_B1 ** ADAM_STEP
_BC2 = 1.0 - ADAM_B2 ** ADAM_STEP


def adamw(w, g, m, v, name):
    rows, cols = w.shape

    def fn(step, i, n, w, g, m, v):
        m2 = ADAM_B1 * m + (1.0 - ADAM_B1) * g
        v2 = ADAM_B2 * v + (1.0 - ADAM_B2) * (g * g)
        delta = -ADAM_LR * ((m2 / _BC1) / (jnp.sqrt(v2 / _BC2) + ADAM_EPS) + ADAM_WD * w)
        return (delta, m2, v2), ()

    ts = _pick(rows, (256, 128, 64, 32, 16, 8))
    return rowwise(fn, [w, g, m, v], outs=[(cols, F32)] * 3, ts=ts, name=name)[0]


SMALL_ROW_MULTIPLE = 16
BIG_ROW_MULTIPLE = 512


def local_step(x, mem, target, full, small):
    ws = [_layer_weights(full, small, l) for l in range(DEPTH)]
    saved = []
    h = x
    for l in range(DEPTH):
        h, s = layer_fwd(h, mem, ws[l], "l")
        saved.append(s)
    loss, dh, dg_final = loss_head(h, target, small["g_final"].reshape(1, D))
    grads = [None] * DEPTH
    for l in reversed(range(DEPTH)):
        dh, grads[l] = layer_bwd(dh, mem, ws[l], saved[l], "l")
    out = {}
    for l in range(DEPTH):
        g = grads[l]
        g["w_in"] = _unpack_w_in_grad(g.pop("wp"), g.pop("wfl"))
        g["b_f"] = g["b_f"][:, :FOX_HEADS]
    for name, shape, _ in BIG:
        out[name] = jnp.stack([grads[l][name].reshape(shape) for l in range(DEPTH)])
    for name, shape in SMALL:
        out[name] = jnp.stack([grads[l][name].reshape(shape) for l in range(DEPTH)])
    out["g_final"] = dg_final.reshape(D)
    return loss, dh, out


def kernel(x, mem, g_ffn1, w_ffn1_in, w_ffn1_out, g_mix, w_in, b_f, b_gate, w_pool, pool_scale, w_up_a, conv_w, conv_b, w_rg_a, b_rg_a, w_rg_x, b_rg_x, lru_lambda, w_up_b, w_up_c, w_o, g_cross, g_mem, w_xq, w_xkv, w_xo, g_ffn2, w_ffn2_in, w_ffn2_out, g_final, loss_target, m_g_ffn1, m_w_ffn1_in, m_w_ffn1_out, m_g_mix, m_w_in, m_b_f, m_b_gate, m_w_pool, m_pool_scale, m_w_up_a, m_conv_w, m_conv_b, m_w_rg_a, m_b_rg_a, m_w_rg_x, m_b_rg_x, m_lru_lambda, m_w_up_b, m_w_up_c, m_w_o, m_g_cross, m_g_mem, m_w_xq, m_w_xkv, m_w_xo, m_g_ffn2, m_w_ffn2_in, m_w_ffn2_out, m_g_final, v_g_ffn1, v_w_ffn1_in, v_w_ffn1_out, v_g_mix, v_w_in, v_b_f, v_b_gate, v_w_pool, v_pool_scale, v_w_up_a, v_conv_w, v_conv_b, v_w_rg_a, v_b_rg_a, v_w_rg_x, v_b_rg_x, v_lru_lambda, v_w_up_b, v_w_up_c, v_w_o, v_g_cross, v_g_mem, v_w_xq, v_w_xkv, v_w_xo, v_g_ffn2, v_w_ffn2_in, v_w_ffn2_out, v_g_final):
    args = dict(locals())
    weights = {n: args[n] for n in ORDER}
    m_in = {n: args["m_" + n] for n in ORDER}
    v_in = {n: args["v_" + n] for n in ORDER}
    big_names = [n for n, _, _ in BIG]
    small_names = [n for n, _ in SMALL] + ["g_final"]

    parts = []
    for n in big_names:
        parts += [p.reshape(-1, D) for p in (_split3(weights[n]) if n == "conv_w" else (weights[n].astype(BF),))]
    mine = jnp.concatenate([_pad_tile_rows(p, 0) for p in parts], axis=0)
    mine = jnp.pad(mine, ((0, _round_up(mine.shape[0], BIG_ROW_MULTIPLE) - mine.shape[0]), (0, 0)))
    full = _gathered_weights(allgather_chips(mine))

    small = {n: weights[n] for n in small_names}
    loss, grad_x, contrib = local_step(x[0], mem[0], loss_target[0], full, small)
    loss = lax.psum(loss, ("x", "y", "c"))

    big_buf = jnp.concatenate([_pad_tile_rows(_shard_split(contrib[n].astype(BF), shape, axis), 1) for n, shape, axis in BIG], axis=1)
    rows_b = _round_up(big_buf.shape[1], BIG_ROW_MULTIPLE)
    big_buf = jnp.pad(big_buf, ((0, 0), (0, rows_b - big_buf.shape[1]), (0, 0)))
    small_flat = jnp.concatenate([contrib[n].reshape(-1) for n in small_names])
    rows_s = _round_up(-(-small_flat.shape[0] // (N_CHIPS * D)), SMALL_ROW_MULTIPLE)
    small_buf = jnp.pad(small_flat, (0, N_CHIPS * rows_s * D - small_flat.shape[0])).reshape(N_CHIPS, rows_s, D)
    red_big, red_small_mine = reduce_scatter(big_buf, small_buf)
    red_small = allgather_same_core([red_small_mine])[0].reshape(-1)

    grads = {}
    off = 0
    for name, shape, axis in BIG:
        nr = _n_rows(shape, axis)
        grads[name] = red_big[off:off + nr].reshape(_shard_shape(shape, axis))
        off += _round_up(nr, DMA_TILE_ROWS)
    off = 0
    for name in small_names:
        sh = weights[name].shape
        grads[name] = red_small[off:off + math.prod(sh)].reshape(sh)
        off += math.prod(sh)

    delta, new_m, new_v = {}, {}, {}
    for name in big_names:
        sh = weights[name].shape
        two_d = (-1, sh[-1])
        d, m2, v2 = adamw(weights[name].reshape(two_d), grads[name].reshape(two_d), m_in[name].reshape(two_d),
                          v_in[name].reshape(two_d), "adamw_" + name)
        delta[name], new_m[name], new_v[name] = d.reshape(sh), m2.reshape(sh), v2.reshape(sh)
    packed = [_pack_rows([t[n] for n in small_names], F32, 8) for t in (weights, grads, m_in)]
    v_flat = jnp.concatenate([v_in[n].reshape(-1) for n in small_names])
    v_pack = jnp.pad(v_flat, (0, packed[0].size - v_flat.shape[0]), constant_values=1.0).reshape(packed[0].shape)
    outs = adamw(packed[0], packed[1], packed[2], v_pack, "adamw_small")
    for t, res in zip((delta, new_m, new_v), outs):
        flat = res.reshape(-1)
        off = 0
        for name in small_names:
            sh = weights[name].shape
            t[name] = flat[off:off + math.prod(sh)].reshape(sh)
            off += math.prod(sh)

    return (loss, grad_x[None], *[grads[n] for n in ORDER], *[delta[n] for n in ORDER],
            *[new_m[n] for n in ORDER], *[new_v[n] for n in ORDER])
```

```python
import functools
import math

import jax
import jax.numpy as jnp
from jax import lax
from jax.experimental import pallas as pl
from jax.experimental.pallas import tpu as pltpu

F32 = jnp.float32
BF = jnp.bfloat16

D = 1024
DFF = 2816
DEPTH = 2
POOL_W = 512
POOL_WINDOWS = (2, 4, 8, 16)
LRU_HEADS = 8
LRU_C = 8.0
FOX_HEADS = 8
FOX_HD = 64
FOX_W = 512
X_HEADS = 4
X_HD = 256
EPS = 1e-6
LANES = 128
N_CHIPS = 4

ADAM_LR, ADAM_B1, ADAM_B2, ADAM_EPS, ADAM_WD, ADAM_STEP = 0.001, 0.9, 0.999, 1e-08, 0.01, 10

VMEM_LIMIT_BYTES = 56 * 2 ** 20

PK_XB, PK_GB, PK_GL, PK_XA, PK_Q, PK_K, PK_V = 0, 1024, 2048, 5120, 5632, 6144, 6656
PK_W = 7168

MESH = pl.DeviceIdType.MESH
ANY = pl.BlockSpec(memory_space=pl.ANY)


def _cparams(ngrid):
    return pltpu.CompilerParams(dimension_semantics=("arbitrary",) * ngrid, vmem_limit_bytes=VMEM_LIMIT_BYTES)


def _pick(n, cands):
    for c in cands:
        if n % c == 0:
            return c
    return n


def _iota(shape, dim):
    return lax.broadcasted_iota(jnp.int32, shape, dim)


def _sigmoid(x):
    return 1.0 / (1.0 + jnp.exp(-x))


def _softplus(z):
    return jnp.maximum(z, 0.0) + jnp.log1p(jnp.exp(-jnp.abs(z)))


def _expm1(x):
    small = jnp.abs(x) < 0.25
    xs = jnp.where(small, x, 0.0)
    poly = xs * (1.0 + xs * (1 / 2 + xs * (1 / 6 + xs * (1 / 24 + xs * (1 / 120 + xs * (1 / 720 + xs * (1 / 5040 + xs * (1 / 40320))))))))
    return jnp.where(small, poly, jnp.exp(x) - 1.0)


_GELU_K = math.sqrt(2.0 / math.pi)


def _gelu_and_grad(x):
    inner = _GELU_K * (x + 0.044715 * x * x * x)
    t = jnp.tanh(inner)
    g = 0.5 * x * (1.0 + t)
    dg = 0.5 * (1.0 + t) + 0.5 * x * (1.0 - t * t) * _GELU_K * (1.0 + 3 * 0.044715 * x * x)
    return g, dg


def _dot(a, b, dims):
    return lax.dot_general(a.astype(BF), b.astype(BF), (dims, ((), ())), preferred_element_type=F32)


def _dot_nn(a, b):
    return _dot(a, b, ((1,), (0,)))


def _dot_nt(a, b):
    return _dot(a, b, ((1,), (1,)))


def _dot_tn(a, b):
    return _dot(a, b, ((0,), (0,)))


def _colsum(x):
    return jnp.sum(x, axis=0, keepdims=True)


_TM = (1024, 1408, 512, 256, 128)
_TN = (1408, 1024, 512, 384, 256, 128)
_TK = (2432, 1408, 1024, 512, 256, 128)


def matmul(a, b, mode, *, out_dtype=None, alpha=1.0, res=None, name):
    out_dtype = BF if out_dtype is None else out_dtype
    if mode == "nn":
        (M, K), N = a.shape, b.shape[1]
    elif mode == "nt":
        (M, K), N = a.shape, b.shape[0]
    else:
        (K, M), N = a.shape, b.shape[1]
    tm, tn, tk = _pick(M, _TM), _pick(N, _TN), _pick(K, _TK)
    nk = K // tk
    dims = {"nn": ((1,), (0,)), "nt": ((1,), (1,)), "tn": ((0,), (0,))}[mode]
    a_spec = pl.BlockSpec((tk, tm), lambda i, j, k: (k, i)) if mode == "tn" else pl.BlockSpec((tm, tk), lambda i, j, k: (i, k))
    b_spec = pl.BlockSpec((tn, tk), lambda i, j, k: (j, k)) if mode == "nt" else pl.BlockSpec((tk, tn), lambda i, j, k: (k, j))
    o_spec = pl.BlockSpec((tm, tn), lambda i, j, k: (i, j))
    has_res = res is not None

    def body(*refs):
        a_ref, b_ref = refs[0], refs[1]
        r_ref = refs[2] if has_res else None
        o_ref, acc = refs[2 + has_res], refs[3 + has_res]
        k = pl.program_id(2)

        @pl.when(k == 0)
        def _():
            acc[...] = jnp.zeros_like(acc)

        acc[...] += _dot(a_ref[...], b_ref[...], dims)

        @pl.when(k == nk - 1)
        def _():
            r = acc[...]
            if alpha != 1.0:
                r = r * alpha
            if has_res:
                r = r + r_ref[...].astype(F32)
            o_ref[...] = r.astype(o_ref.dtype)

    ins = [a, b] + ([res] if has_res else [])
    specs = [a_spec, b_spec] + ([o_spec] if has_res else [])
    return pl.pallas_call(
        body, name=name, grid=(M // tm, N // tn, nk), in_specs=specs, out_specs=o_spec,
        out_shape=jax.ShapeDtypeStruct((M, N), out_dtype), scratch_shapes=[pltpu.VMEM((tm, tn), F32)],
        compiler_params=_cparams(3))(*ins)


class Rows:
    def __init__(self, arr, w=None, cb=0, halo=None, h=8):
        self.arr, self.w, self.cb, self.halo, self.h = arr, (arr.shape[1] if w is None else w), cb, halo, h


def rowwise(fn, rows, consts=(), outs=(), accs=(), scratch=(), *, ts, name, reverse=False):
    rows = [r if isinstance(r, Rows) else Rows(r) for r in rows]
    S = rows[0].arr.shape[0]
    n = S // ts
    tile = (lambda s: n - 1 - s) if reverse else (lambda s: s)

    def row_spec(r):
        if r.halo is None:
            return pl.BlockSpec((ts, r.w), lambda s: (tile(s), r.cb))
        per, last = ts // r.h, S // r.h - 1
        if r.halo == "prev":
            return pl.BlockSpec((r.h, r.w), lambda s: (jnp.maximum(tile(s) * per - 1, 0), r.cb))
        return pl.BlockSpec((r.h, r.w), lambda s: (jnp.minimum((tile(s) + 1) * per, last), r.cb))

    def whole(shape):
        nd = len(shape)
        return pl.BlockSpec(tuple(shape), lambda s: (0,) * nd)

    nr, nc, no, na = len(rows), len(consts), len(outs), len(accs)

    def body(*refs):
        in_refs, out_refs = refs[:nr + nc], refs[nr + nc:nr + nc + no]
        acc_refs, scr = refs[nr + nc + no:nr + nc + no + na], refs[nr + nc + no + na:]
        step = pl.program_id(0)
        o, inc = fn(step, tile(step), n, *[r[...] for r in in_refs], *scr)
        for ref, val in zip(out_refs, o):
            ref[...] = val.astype(ref.dtype)
        if na:
            @pl.when(step == 0)
            def _():
                for ref in acc_refs:
                    ref[...] = jnp.zeros_like(ref)
            for ref, val in zip(acc_refs, inc):
                ref[...] += val.astype(ref.dtype)

    out_shape = [jax.ShapeDtypeStruct((S, w), dt) for w, dt in outs] + [jax.ShapeDtypeStruct(tuple(s), dt) for s, dt in accs]
    out_specs = [pl.BlockSpec((ts, w), lambda s: (tile(s), 0)) for w, _ in outs] + [whole(s) for s, _ in accs]
    res = pl.pallas_call(
        body, name=name, grid=(n,),
        in_specs=[row_spec(r) for r in rows] + [whole(c.shape) for c in consts],
        out_specs=out_specs, out_shape=out_shape, scratch_shapes=list(scratch),
        compiler_params=_cparams(1))(*[r.arr for r in rows], *consts)
    return res[:no], res[no:]


def _rstd(x):
    return lax.rsqrt(jnp.mean(x * x, axis=-1, keepdims=True) + EPS)


def rms_fwd(x, g, *, name, ts=512):
    def fn(step, i, n, x, g):
        return ((x * _rstd(x)) * g,), ()
    return rowwise(fn, [x], [g], [(D, BF)], ts=min(ts, x.shape[0]), name=name)[0][0]


def rms_bwd(x, g, dh, dres, *, name, ts=512):
    has_res = dres is not None

    def fn(step, i, n, x, dh, *rest):
        g = rest[-1]
        r = _rstd(x)
        xhat = x * r
        dh = dh.astype(F32)
        gd = dh * g
        dx = r * (gd - xhat * jnp.mean(xhat * gd, axis=-1, keepdims=True))
        if has_res:
            dx = dx + rest[0]
        return (dx,), (_colsum(dh * xhat),)

    rows = [x, dh] + ([dres] if has_res else [])
    o, a = rowwise(fn, rows, [g], [(D, F32)], [((1, D), F32)], ts=min(ts, x.shape[0]), name=name)
    return o[0], a[0]


def ffn_fwd(x, g, w_in, w_out, tag):
    h = rms_fwd(x, g, name=tag + "_rms")
    ab = matmul(h, w_in, "nn", name=tag + "_in")

    def act_fn(step, i, n, ab):
        a, b = ab[:, :DFF].astype(F32), ab[:, DFF:].astype(F32)
        return ((a * _sigmoid(a)) * b,), ()

    act = rowwise(act_fn, [ab], outs=[(DFF, BF)], ts=256, name=tag + "_act")[0][0]
    y = matmul(act, w_out, "nn", out_dtype=F32, alpha=0.5, res=x, name=tag + "_out")
    return y, (x, h, ab, act)


def ffn_bwd(dy, saved, g, w_in, w_out, tag):
    x, h, ab, act = saved
    dact = matmul(dy, w_out, "nt", alpha=0.5, name=tag + "_dact")
    dw_out = matmul(act, dy, "tn", alpha=0.5, name=tag + "_dwout")

    def dab_fn(step, i, n, ab, dact):
        a, b, dact = ab[:, :DFF].astype(F32), ab[:, DFF:].astype(F32), dact.astype(F32)
        s = _sigmoid(a)
        da = dact * b * (s * (1.0 + a * (1.0 - s)))
        db = dact * (a * s)
        return (jnp.concatenate([da, db], axis=1),), ()

    dab = rowwise(dab_fn, [ab, dact], outs=[(2 * DFF, BF)], ts=256, name=tag + "_dab")[0][0]
    dw_in = matmul(h, dab, "tn", name=tag + "_dwin")
    dh = matmul(dab, w_in, "nt", name=tag + "_dh")
    dx, dg = rms_bwd(x, g, dh, dy, name=tag + "_drms")
    return dx, dg, dw_in, dw_out


POOL_TS = 256
POOL_HALO = 16


def pool_fwd(P, w_pool, pool_scale, tag):
    ts = POOL_TS

    def fn(step, i, n, xa, halo, w, scale):
        xa = xa.astype(F32)
        halo = halo.astype(F32) * jnp.where(i > 0, 1.0, 0.0)
        ext = jnp.concatenate([halo, xa], axis=0)
        pos = (i * ts + 1 + _iota((ts, LANES), 0)).astype(F32)
        ds, ys = [], []
        for gi, win in enumerate(POOL_WINDOWS):
            e = ext[:, gi * LANES:(gi + 1) * LANES]
            sh = 1
            while sh < win:
                e = e + pltpu.roll(e, sh, 0)
                sh *= 2
            mean = e[POOL_HALO:] / jnp.minimum(pos, float(win))
            d = mean - xa[:, gi * LANES:(gi + 1) * LANES]
            ds.append(d)
            ys.append(_dot_nn(d, w[gi]))
        d = jnp.concatenate(ds, axis=1)
        return (d, jnp.concatenate(ys, axis=1) * scale), ()

    xa = Rows(P, POOL_W, PK_XA // POOL_W)
    xa_prev = Rows(P, POOL_W, PK_XA // POOL_W, "prev", POOL_HALO)
    (d, ya0), _ = rowwise(fn, [xa, xa_prev], [w_pool, pool_scale], [(POOL_W, BF), (POOL_W, BF)], ts=ts, name=tag + "_pool")
    return d, ya0


def pool_bwd(dya0, d, w_pool, pool_scale, tag):
    ts = POOL_TS
    L = ts + POOL_HALO

    def fn(step, i, n, dya0, dya0_next, d, w, scale):
        d = d.astype(F32)
        dz = jnp.concatenate([dya0.astype(F32), dya0_next.astype(F32) * jnp.where(i < n - 1, 1.0, 0.0)], axis=0) * scale
        pos = (i * ts + 1 + _iota((L, LANES), 0)).astype(F32)
        dxa, dws, dsc = [], [], []
        for gi, win in enumerate(POOL_WINDOWS):
            sl = slice(gi * LANES, (gi + 1) * LANES)
            dzg = dz[:, sl]
            dd = _dot_nt(dzg, w[gi])
            e = dd / jnp.minimum(pos, float(win))
            sh = 1
            while sh < win:
                e = e + pltpu.roll(e, L - sh, 0)
                sh *= 2
            dxa.append(e[:ts] - dd[:ts])
            dws.append(_dot_tn(d[:, sl], dzg[:ts])[None])
            z = _dot_nn(d[:, sl], w[gi])
            dsc.append(_colsum(dya0[:, sl].astype(F32) * z))
        return (jnp.concatenate(dxa, axis=1),), (jnp.concatenate(dws, axis=0), jnp.concatenate(dsc, axis=1))

    (dxa,), (dw_pool, dscale) = rowwise(
        fn, [dya0, Rows(dya0, halo="next", h=POOL_HALO), d], [w_pool, pool_scale],
        [(POOL_W, BF)], [((4, LANES, LANES), F32), ((1, POOL_W), F32)], ts=ts, name=tag + "_dpool")
    return dxa, dw_pool, dscale


LRU_R = 256


def _scan_fwd(A, U):
    R = A.shape[0]
    row = _iota(A.shape, 0)
    d = 1
    while d < R:
        m = row >= d
        A_sh = jnp.where(m, pltpu.roll(A, d, 0), 1.0)
        U_sh = jnp.where(m, pltpu.roll(U, d, 0), 0.0)
        U = A * U_sh + U
        A = A * A_sh
        d *= 2
    return A, U


def _scan_bwd(B, X):
    R = B.shape[0]
    row = _iota(B.shape, 0)
    d = 1
    while d < R:
        m = row < R - d
        B_sh = jnp.where(m, pltpu.roll(B, R - d, 0), 1.0)
        X_sh = jnp.where(m, pltpu.roll(X, R - d, 0), 0.0)
        X = X + B * X_sh
        B = B * B_sh
        d *= 2
    return X


def _lru_gates(xc, wa, ba, wx, bx, lam):
    r = _sigmoid(_dot_nn(xc, wa) + ba)
    ig = _sigmoid(_dot_nn(xc, wx) + bx)
    sp = _softplus(-lam)
    log_a = -LRU_C * r * sp
    a = jnp.exp(log_a)
    mult = jnp.sqrt(-_expm1(2.0 * log_a))
    return r, ig, sp, a, mult


def _lru_specs(S, R, reverse):
    nch = S // R
    ch = (lambda j: nch - 1 - j) if reverse else (lambda j: j)
    per = R // 8

    def col(off):
        return pl.BlockSpec((R, LANES), lambda h, j: (ch(j), off + h))

    def prev(off):
        return pl.BlockSpec((8, LANES), lambda h, j: (jnp.maximum(ch(j) * per - 1, 0), off + h))

    vec = pl.BlockSpec((1, LANES), lambda h, j: (0, h))
    cw = pl.BlockSpec((4, LANES), lambda h, j: (0, h))
    wsq = pl.BlockSpec((None, LANES, LANES), lambda h, j: (h, 0, 0))
    return nch, ch, col, prev, vec, cw, wsq


def lru_fwd(P, conv_w, conv_b, w_a, b_a, w_x, b_x, lam, tag):
    S = P.shape[0]
    R = min(LRU_R, S)
    nch, ch, col, prev, vec, cw_spec, wsq = _lru_specs(S, R, False)

    def body(xb_ref, halo_ref, gb_ref, cw_ref, cb_ref, wa_ref, ba_ref, wx_ref, bx_ref, lam_ref, xc_ref, h_ref, yb_ref, carry):
        j = pl.program_id(1)
        xb = xb_ref[...].astype(F32)
        halo = halo_ref[...].astype(F32) * jnp.where(j > 0, 1.0, 0.0)
        ext = jnp.concatenate([halo, xb], axis=0)
        cw = cw_ref[...]
        xc = cb_ref[...]
        for k in range(4):
            e = ext if k == 3 else pltpu.roll(ext, 3 - k, 0)
            xc = xc + e[8:] * cw[k:k + 1]
        r, ig, sp, a, mult = _lru_gates(xc, wa_ref[...], ba_ref[...], wx_ref[...], bx_ref[...], lam_ref[...])
        u = mult * (ig * xc)
        cum_a, hloc = _scan_fwd(a, u)

        @pl.when(j == 0)
        def _():
            carry[...] = jnp.zeros_like(carry)

        hfull = hloc + cum_a * carry[0:1]
        carry[...] = jnp.broadcast_to(hfull[R - 1:R], carry.shape)
        gel, _ = _gelu_and_grad(gb_ref[...].astype(F32))
        xc_ref[...] = xc
        h_ref[...] = hfull
        yb_ref[...] = (hfull * gel).astype(yb_ref.dtype)

    out = pl.BlockSpec((R, LANES), lambda h, j: (j, h))
    return pl.pallas_call(
        body, name=tag + "_lru", grid=(LRU_HEADS, nch),
        in_specs=[col(PK_XB // LANES), prev(PK_XB // LANES), col(PK_GB // LANES), cw_spec, vec, wsq, vec, wsq, vec, vec],
        out_specs=[out, out, out],
        out_shape=[jax.ShapeDtypeStruct((S, D), F32), jax.ShapeDtypeStruct((S, D), F32), jax.ShapeDtypeStruct((S, D), BF)],
        scratch_shapes=[pltpu.VMEM((8, LANES), F32)], compiler_params=_cparams(2),
    )(P, P, P, conv_w, conv_b, w_a, b_a, w_x, b_x, lam)


def lru_bwd(dyb0, P, hh, xc, conv_w, w_a, b_a, w_x, b_x, lam, tag):
    S = P.shape[0]
    R = min(LRU_R, S)
    nch, ch, col, prev, vec, cw_spec, wsq = _lru_specs(S, R, True)

    def body(dyb_ref, gb_ref, h_ref, hprev_ref, xc_ref, xb_ref, xbprev_ref, cw_ref, wa_ref, ba_ref, wx_ref, bx_ref, lam_ref,
             dgb_ref, dxb_ref, dwa_ref, dba_ref, dwx_ref, dbx_ref, dlam_ref, dcw_ref, dcb_ref, gcarry, dxc_head):
        j = pl.program_id(1)
        jj = nch - 1 - j
        has_prev = jnp.where(jj > 0, 1.0, 0.0)
        row = _iota((R, LANES), 0)
        xc, hv, lam = xc_ref[...], h_ref[...], lam_ref[...]
        wa, wx = wa_ref[...], wx_ref[...]
        r, ig, sp, a, mult = _lru_gates(xc, wa, ba_ref[...], wx, bx_ref[...], lam)
        dyb = dyb_ref[...].astype(F32)
        gel, dgel = _gelu_and_grad(gb_ref[...].astype(F32))
        dgb_ref[...] = (dyb * hv * dgel).astype(dgb_ref.dtype)

        @pl.when(j == 0)
        def _():
            gcarry[...] = jnp.zeros_like(gcarry)
            dxc_head[...] = jnp.zeros_like(dxc_head)
            for ref in (dwa_ref, dba_ref, dwx_ref, dbx_ref, dlam_ref, dcw_ref, dcb_ref):
                ref[...] = jnp.zeros_like(ref)

        B = jnp.where(row < R - 1, pltpu.roll(a, R - 1, 0), 0.0)
        X = dyb * gel + jnp.where(row == R - 1, gcarry[0:1], 0.0)
        G = _scan_bwd(B, X)
        gcarry[...] = jnp.broadcast_to(a[0:1] * G[0:1], gcarry.shape)
        hprev = jnp.where(row == 0, hprev_ref[...][7:8] * has_prev, pltpu.roll(hv, 1, 0))
        da = G * hprev
        dmult = G * ig * xc
        dig = G * mult * xc
        dxc = G * mult * ig
        dlog_a = da * a - dmult * (a * a) / mult
        dzr = dlog_a * (-LRU_C * sp) * (r * (1.0 - r))
        dzi = dig * (ig * (1.0 - ig))
        dxc = dxc + _dot_nt(dzr, wa) + _dot_nt(dzi, wx)
        dwa_ref[...] += _dot_tn(xc, dzr)
        dwx_ref[...] += _dot_tn(xc, dzi)
        dba_ref[...] += _colsum(dzr)
        dbx_ref[...] += _colsum(dzi)
        dlam_ref[...] += _colsum(dlog_a * (-LRU_C * r)) * (-_sigmoid(-lam))
        dcb_ref[...] += _colsum(dxc)
        cw = cw_ref[...]
        ext = jnp.concatenate([dxc, dxc_head[...]], axis=0)
        dxb = dxc * cw[3:4]
        for k in range(3):
            dxb = dxb + pltpu.roll(ext, R + 8 - (3 - k), 0)[:R] * cw[k:k + 1]
        dxb_ref[...] = dxb.astype(dxb_ref.dtype)
        dxc_head[...] = dxc[0:8]
        extx = jnp.concatenate([xbprev_ref[...].astype(F32) * has_prev, xb_ref[...].astype(F32)], axis=0)
        incs = []
        for k in range(4):
            e = extx if k == 3 else pltpu.roll(extx, 3 - k, 0)
            incs.append(_colsum(dxc * e[8:]))
        dcw_ref[...] += jnp.concatenate(incs, axis=0)

    plain = pl.BlockSpec((R, LANES), lambda h, j: (ch(j), h))
    plain_prev = pl.BlockSpec((8, LANES), lambda h, j: (jnp.maximum(ch(j) * (R // 8) - 1, 0), h))
    return pl.pallas_call(
        body, name=tag + "_dlru", grid=(LRU_HEADS, nch),
        in_specs=[plain, col(PK_GB // LANES), plain, plain_prev, plain, col(PK_XB // LANES), prev(PK_XB // LANES),
                  cw_spec, wsq, vec, wsq, vec, vec],
        out_specs=[plain, plain, wsq, vec, wsq, vec, vec, cw_spec, vec],
        out_shape=[jax.ShapeDtypeStruct((S, D), BF), jax.ShapeDtypeStruct((S, D), BF),
                   jax.ShapeDtypeStruct((LRU_HEADS, LANES, LANES), F32), jax.ShapeDtypeStruct((1, D), F32),
                   jax.ShapeDtypeStruct((LRU_HEADS, LANES, LANES), F32), jax.ShapeDtypeStruct((1, D), F32),
                   jax.ShapeDtypeStruct((1, D), F32), jax.ShapeDtypeStruct((4, D), F32), jax.ShapeDtypeStruct((1, D), F32)],
        scratch_shapes=[pltpu.VMEM((8, LANES), F32), pltpu.VMEM((8, LANES), F32)], compiler_params=_cparams(2),
    )(dyb0, P, hh, hh, xc, P, P, conv_w, w_a, b_a, w_x, b_x, lam)


CUM_TS = 512
FLASH_T = 512
NEG = -1e30


def cumlogf_fwd(fl, b_f, tag):
    ts = min(CUM_TS, fl.shape[0])

    def fn(step, i, n, fl, bf, carry):
        x = -_softplus(-(fl + bf))
        row = _iota(x.shape, 0)
        d = 1
        while d < ts:
            x = x + jnp.where(row >= d, pltpu.roll(x, d, 0), 0.0)
            d *= 2

        @pl.when(step == 0)
        def _():
            carry[...] = jnp.zeros_like(carry)

        c = x + carry[0:1]
        carry[...] = jnp.broadcast_to(c[ts - 1:ts], carry.shape)
        return (c,), ()

    return rowwise(fn, [fl], [b_f], [(LANES, F32)], scratch=[pltpu.VMEM((8, LANES), F32)], ts=ts, name=tag + "_cum")[0][0]


def cumlogf_bwd(dc, fl, b_f, tag):
    ts = min(CUM_TS, fl.shape[0])

    def fn(step, i, n, dc, fl, bf, carry):
        row = _iota(dc.shape, 0)
        x = dc
        d = 1
        while d < ts:
            x = x + jnp.where(row < ts - d, pltpu.roll(x, ts - d, 0), 0.0)
            d *= 2

        @pl.when(step == 0)
        def _():
            carry[...] = jnp.zeros_like(carry)

        g = x + carry[0:1]
        carry[...] = jnp.broadcast_to(g[0:1], carry.shape)
        dfl = g * _sigmoid(-(fl + bf))
        return (dfl,), (_colsum(dfl),)

    (dfl,), (db,) = rowwise(fn, [dc, fl], [b_f], [(LANES, F32)], [((1, LANES), F32)], scratch=[pltpu.VMEM((8, LANES), F32)],
                            ts=ts, name=tag + "_dcum", reverse=True)
    return dfl, db


FOX_KA = 80


def _split3(c):
    c = c.astype(F32)
    hi = lax.reduce_precision(c, 8, 7)
    r = c - hi
    mid = lax.reduce_precision(r, 8, 7)
    return hi.astype(BF), mid.astype(BF), (r - mid).astype(BF)


def fox_operands(q, k, ch):
    H, S, hd = q.shape
    qs = q * (FOX_HD ** -0.5)
    pieces = [p.astype(F32) for p in _split3(ch)]
    pad = FOX_KA - hd

    def unit(i):
        return (jnp.arange(FOX_KA) == hd + i).astype(F32)

    qa = jnp.pad(qs.transpose(0, 2, 1).astype(F32), ((0, 0), (0, pad), (0, 0)))
    ka = jnp.pad(k.astype(F32), ((0, 0), (0, 0), (0, pad)))
    for i, p in enumerate(pieces):
        qa = qa + p[:, None, :] * unit(i)[None, :, None] + unit(3 + i)[None, :, None]
        ka = ka + unit(i)[None, None, :] - p[:, :, None] * unit(3 + i)[None, None, :]
    return qa.astype(BF), ka.astype(BF), qs


def _causal(s):
    return jnp.where(_iota(s.shape, 0) <= _iota(s.shape, 1), s, NEG)


def flash_fwd(qa, ka, vt, tag):
    H, hd, S = vt.shape
    T = min(FLASH_T, S)
    nb = S // T

    pairs = [(qi, ki) for qi in range(nb) for ki in range(qi + 1)]
    q_tab = jnp.asarray([p[0] for p in pairs], jnp.int32)
    k_tab = jnp.asarray([p[1] for p in pairs], jnp.int32)

    def body(q_tab_ref, k_tab_ref, ka_ref, qa_ref, vt_ref, o_ref, lse_ref, m_s, l_s, acc):
        qi, ki = q_tab_ref[pl.program_id(1)], k_tab_ref[pl.program_id(1)]

        @pl.when(ki == 0)
        def _():
            m_s[...] = jnp.full_like(m_s, NEG)
            l_s[...] = jnp.zeros_like(l_s)
            acc[...] = jnp.zeros_like(acc)

        def step(diagonal):
            s = _dot_nn(ka_ref[...], qa_ref[...])
            if diagonal:
                s = _causal(s)
            m_new = jnp.maximum(m_s[...], jnp.max(s, axis=0, keepdims=True))
            alpha = jnp.exp(m_s[...] - m_new)
            p = jnp.exp(s - m_new)
            l_s[...] = alpha * l_s[...] + jnp.sum(p, axis=0, keepdims=True)
            p_hi = p.astype(BF)
            p_lo = p - p_hi.astype(F32)
            v = vt_ref[...]
            acc[...] = alpha * acc[...] + (_dot_nn(v, p_hi) + _dot_nn(v, p_lo))
            m_s[...] = m_new

        @pl.when(ki < qi)
        def _():
            step(False)

        @pl.when(ki == qi)
        def _():
            step(True)
            o_ref[...] = acc[...] / l_s[...]
            lse_ref[...] = m_s[...] + jnp.log(l_s[...])

    grid_spec = pltpu.PrefetchScalarGridSpec(
        num_scalar_prefetch=2, grid=(H, len(pairs)),
        in_specs=[pl.BlockSpec((None, T, FOX_KA), lambda h, p, qt, kt: (h, kt[p], 0)),
                  pl.BlockSpec((None, FOX_KA, T), lambda h, p, qt, kt: (h, 0, qt[p])),
                  pl.BlockSpec((None, hd, T), lambda h, p, qt, kt: (h, 0, kt[p]))],
        out_specs=[pl.BlockSpec((None, hd, T), lambda h, p, qt, kt: (h, 0, qt[p])),
                   pl.BlockSpec((None, 1, T), lambda h, p, qt, kt: (h, 0, qt[p]))],
        scratch_shapes=[pltpu.VMEM((1, T), F32), pltpu.VMEM((1, T), F32), pltpu.VMEM((hd, T), F32)])
    return pl.pallas_call(
        body, name=tag + "_flash", grid_spec=grid_spec,
        out_shape=[jax.ShapeDtypeStruct((H, hd, S), F32), jax.ShapeDtypeStruct((H, 1, S), F32)],
        compiler_params=_cparams(2))(q_tab, k_tab, ka, qa, vt)


def flash_bwd(qa, ka, qs, kt, v, ot, do, dot_, lse, tag):
    H, S, hd = v.shape
    T = min(FLASH_T, S)
    nb = S // T

    pairs = [(qi, ki) for ki in range(nb) for qi in range(ki, nb)]
    q_tab = jnp.asarray([p[0] for p in pairs], jnp.int32)
    k_tab = jnp.asarray([p[1] for p in pairs], jnp.int32)

    def body(q_tab_ref, k_tab_ref, ka_ref, qa_ref, qs_ref, kt_ref, v_ref, ot_ref, do_ref, dot_ref, lse_ref,
             dqt_ref, dk_ref, dv_ref, dc_ref, dk_acc, dv_acc, dc_acc):
        qi, ki = q_tab_ref[pl.program_id(1)], k_tab_ref[pl.program_id(1)]

        @pl.when(pl.program_id(1) == 0)
        def _():
            dqt_ref[...] = jnp.zeros_like(dqt_ref)

        @pl.when(qi == ki)
        def _():
            dk_acc[...] = jnp.zeros_like(dk_acc)
            dv_acc[...] = jnp.zeros_like(dv_acc)
            dc_acc[...] = jnp.zeros_like(dc_acc)

        def step(diagonal):
            s = _dot_nn(ka_ref[...], qa_ref[...])
            if diagonal:
                s = _causal(s)
            p = jnp.exp(s - lse_ref[...])
            dot_v = dot_ref[...]
            dp = _dot_nn(v_ref[...], dot_v)
            delta = jnp.sum(dot_v.astype(F32) * ot_ref[...], axis=0, keepdims=True)
            ds = p * (dp - delta)
            part = ds[:, 0:LANES]
            for g in range(1, T // LANES):
                part = part + ds[:, g * LANES:(g + 1) * LANES]
            dc_acc[...] += part
            dsb = ds.astype(BF)
            dv_acc[...] += _dot_nn(p, do_ref[...])
            dk_acc[...] += _dot_nn(dsb, qs_ref[...])
            dqt_ref[qi] += _dot_nn(kt_ref[...], dsb)

        @pl.when(qi > ki)
        def _():
            step(False)

        @pl.when(qi == ki)
        def _():
            step(True)

        @pl.when(qi == nb - 1)
        def _():
            dk_ref[...] = dk_acc[...]
            dv_ref[...] = dv_acc[...]
            dc_ref[...] = -jnp.sum(dc_acc[...], axis=1, keepdims=True)

    qblk = lambda h, p, qt, kt_: (h, qt[p], 0)
    qblk_t = lambda h, p, qt, kt_: (h, 0, qt[p])
    kblk = lambda h, p, qt, kt_: (h, kt_[p], 0)
    kblk_t = lambda h, p, qt, kt_: (h, 0, kt_[p])
    grid_spec = pltpu.PrefetchScalarGridSpec(
        num_scalar_prefetch=2, grid=(H, len(pairs)),
        in_specs=[pl.BlockSpec((None, T, FOX_KA), kblk), pl.BlockSpec((None, FOX_KA, T), qblk_t),
                  pl.BlockSpec((None, T, hd), qblk), pl.BlockSpec((None, hd, T), kblk_t), pl.BlockSpec((None, T, hd), kblk),
                  pl.BlockSpec((None, hd, T), qblk_t), pl.BlockSpec((None, T, hd), qblk), pl.BlockSpec((None, hd, T), qblk_t),
                  pl.BlockSpec((None, 1, T), qblk_t)],
        out_specs=[pl.BlockSpec((None, nb, hd, T), lambda h, p, qt, kt_: (h, 0, 0, 0)), pl.BlockSpec((None, T, hd), kblk),
                   pl.BlockSpec((None, T, hd), kblk), pl.BlockSpec((None, T, 1), kblk)],
        scratch_shapes=[pltpu.VMEM((T, hd), F32), pltpu.VMEM((T, hd), F32), pltpu.VMEM((T, LANES), F32)])
    return pl.pallas_call(
        body, name=tag + "_dflash", grid_spec=grid_spec,
        out_shape=[jax.ShapeDtypeStruct((H, nb, hd, T), F32), jax.ShapeDtypeStruct((H, S, hd), F32),
                   jax.ShapeDtypeStruct((H, S, hd), F32), jax.ShapeDtypeStruct((H, S, 1), F32)],
        compiler_params=_cparams(2))(q_tab, k_tab, ka, qa, qs, kt, v, ot, do, dot_, lse)


def _to_heads(x2d):
    S = x2d.shape[0]
    return x2d.reshape(S, FOX_HEADS, FOX_HD).transpose(1, 0, 2)


def _from_heads(x3d):
    S = x3d.shape[1]
    return x3d.transpose(1, 0, 2).reshape(S, FOX_W)


def _gl_rows(P):
    return [Rows(P, D, PK_GL // D + k) for k in range(3)]


def merge_fwd(P, ya, yb, yc, b_gate, tag):
    def fn(step, i, n, g0, g1, g2, ya, yb, yc, b):
        out = 0.0
        for k, (gl, y) in enumerate(((g0, ya), (g1, yb), (g2, yc))):
            out = out + _sigmoid(gl.astype(F32) + b[:, k * D:(k + 1) * D]) * y.astype(F32)
        return (out,), ()
    return rowwise(fn, _gl_rows(P) + [ya, yb, yc], [b_gate], [(D, BF)], ts=512, name=tag + "_merge")[0][0]


def merge_bwd(dm, P, ya, yb, yc, b_gate, tag):
    def fn(step, i, n, dm, g0, g1, g2, ya, yb, yc, b):
        dm = dm.astype(F32)
        dys, dgls = [], []
        for k, (gl, y) in enumerate(((g0, ya), (g1, yb), (g2, yc))):
            g = _sigmoid(gl.astype(F32) + b[:, k * D:(k + 1) * D])
            dys.append(dm * g)
            dgls.append(dm * y.astype(F32) * (g * (1.0 - g)))
        dgl = jnp.concatenate(dgls, axis=1)
        return (dys[0], dys[1], dys[2], dgl), (_colsum(dgl),)
    (dya, dyb, dyc, dgl), (db,) = rowwise(
        fn, [dm] + _gl_rows(P) + [ya, yb, yc], [b_gate], [(D, BF), (D, BF), (D, BF), (3 * D, BF)], [((1, 3 * D), F32)],
        ts=256, name=tag + "_dmerge")
    return dya, dyb, dyc, dgl, db


def _xattn_probs(qh, kh):
    s = _dot_nt(qh, kh) * (X_HD ** -0.5)
    e = jnp.exp(s - jnp.max(s, axis=1, keepdims=True))
    return e / jnp.sum(e, axis=1, keepdims=True)


def xattn_fwd(q, kv, tag):
    def fn(step, i, n, q, k, v):
        os = []
        for h in range(X_HEADS):
            sl = slice(h * X_HD, (h + 1) * X_HD)
            os.append(_dot_nn(_xattn_probs(q[:, sl], k[:, sl]), v[:, sl]))
        return (jnp.concatenate(os, axis=1),), ()

    return _xattn_call(fn, [q], kv, [(D, BF)], [], tag + "_xattn")[0][0]


def _xattn_call(fn, rows, kv, outs, accs, name):
    S, ts, M = rows[0].shape[0], 512, kv.shape[0]
    ts = min(ts, S)
    nr, no, na = len(rows), len(outs), len(accs)

    def body(*refs):
        in_refs, out_refs, acc_refs = refs[:nr + 2], refs[nr + 2:nr + 2 + no], refs[nr + 2 + no:]
        step = pl.program_id(0)
        o, inc = fn(step, step, S // ts, *[r[...] for r in in_refs])
        for ref, val in zip(out_refs, o):
            ref[...] = val.astype(ref.dtype)
        if na:
            @pl.when(step == 0)
            def _():
                for ref in acc_refs:
                    ref[...] = jnp.zeros_like(ref)
            for ref, val in zip(acc_refs, inc):
                ref[...] += val

    row = pl.BlockSpec((ts, D), lambda s: (s, 0))
    res = pl.pallas_call(
        body, name=name, grid=(S // ts,),
        in_specs=[row] * nr + [pl.BlockSpec((M, D), lambda s: (0, 0)), pl.BlockSpec((M, D), lambda s: (0, 1))],
        out_specs=[row] * no + [pl.BlockSpec(tuple(s), lambda s_: (0, 0)) for s, _ in accs],
        out_shape=[jax.ShapeDtypeStruct((S, w), dt) for w, dt in outs] + [jax.ShapeDtypeStruct(tuple(s), dt) for s, dt in accs],
        compiler_params=_cparams(1))(*rows, kv, kv)
    return res[:no], res[no:]


def xattn_bwd(q, kv, do, tag):
    def fn(step, i, n, q, do, k, v):
        dqs, dks, dvs = [], [], []
        for h in range(X_HEADS):
            sl = slice(h * X_HD, (h + 1) * X_HD)
            p = _xattn_probs(q[:, sl], k[:, sl])
            dp = _dot_nt(do[:, sl], v[:, sl])
            ds = (p * (dp - jnp.sum(p * dp, axis=1, keepdims=True)) * (X_HD ** -0.5)).astype(BF)
            dqs.append(_dot_nn(ds, k[:, sl]))
            dks.append(_dot_tn(ds, q[:, sl]))
            dvs.append(_dot_tn(p, do[:, sl]))
        return (jnp.concatenate(dqs, axis=1),), (jnp.concatenate(dks + dvs, axis=1),)

    (dq,), (dkv,) = _xattn_call(fn, [q, do], kv, [(D, BF)], [((kv.shape[0], 2 * D), F32)], tag + "_dxattn")
    return dq, dkv


def layer_fwd(x, mem, w, tag):
    x1, s_ffn1 = ffn_fwd(x, w["g_ffn1"], w["ffn1_in"], w["ffn1_out"], tag + "_ffn1")
    u = rms_fwd(x1, w["g_mix"], name=tag + "_mixrms")
    P = matmul(u, w["wp"], "nn", name=tag + "_proj")
    fl = matmul(u, w["wfl"], "nn", out_dtype=F32, name=tag + "_projf")
    d, ya0 = pool_fwd(P, w["w_pool"], w["pool_scale"], tag)
    ya = matmul(ya0, w["up_a"], "nn", name=tag + "_upa")
    xc, hh, yb0 = lru_fwd(P, w["conv_w"], w["conv_b"], w["w_rg_a"], w["b_rg_a"], w["w_rg_x"], w["b_rg_x"], w["lam"], tag)
    yb = matmul(yb0, w["up_b"], "nn", name=tag + "_upb")
    c = cumlogf_fwd(fl, w["b_f"], tag)
    q, k, v = (_to_heads(P[:, o:o + FOX_W]) for o in (PK_Q, PK_K, PK_V))
    qa, ka, qs = fox_operands(q, k, c[:, :FOX_HEADS].T)
    ot, lse = flash_fwd(qa, ka, v.transpose(0, 2, 1), tag)
    o2 = ot.transpose(2, 0, 1).reshape(-1, FOX_W)
    yc = matmul(o2, w["up_c"], "nn", name=tag + "_upc")
    merged = merge_fwd(P, ya, yb, yc, w["b_gate"], tag)
    x2 = matmul(merged, w["w_o"], "nn", out_dtype=F32, res=x1, name=tag + "_wo")
    hq = rms_fwd(x2, w["g_cross"], name=tag + "_xrms")
    qx = matmul(hq, w["xq"], "nn", name=tag + "_xq")
    mn = rms_fwd(mem, w["g_mem"], name=tag + "_mrms")
    kv = matmul(mn, w["xkv"], "nn", name=tag + "_xkv")
    ox = xattn_fwd(qx, kv, tag)
    x3 = matmul(ox, w["xo"], "nn", out_dtype=F32, res=x2, name=tag + "_xo")
    x4, s_ffn2 = ffn_fwd(x3, w["g_ffn2"], w["ffn2_in"], w["ffn2_out"], tag + "_ffn2")
    saved = dict(ffn1=s_ffn1, ffn2=s_ffn2, x1=x1, u=u, P=P, fl=fl, d=d, ya0=ya0, ya=ya, xc=xc, hh=hh, yb0=yb0, yb=yb,
                 qa=qa, ka=ka, qs=qs, k=k, v=v, ot=ot, lse=lse, o2=o2, yc=yc, merged=merged, x2=x2, hq=hq, qx=qx,
                 mn=mn, kv=kv, ox=ox)
    return x4, saved


def layer_bwd(dx4, mem, w, s, tag):
    g = {}
    dx3, g["g_ffn2"], g["w_ffn2_in"], g["w_ffn2_out"] = ffn_bwd(dx4, s["ffn2"], w["g_ffn2"], w["ffn2_in"], w["ffn2_out"], tag + "_ffn2")
    dox = matmul(dx3, w["xo"], "nt", name=tag + "_dox")
    g["w_xo"] = matmul(s["ox"], dx3, "tn", name=tag + "_dwxo")
    dqx, dkv = xattn_bwd(s["qx"], s["kv"], dox, tag)
    g["w_xq"] = matmul(s["hq"], dqx, "tn", name=tag + "_dwxq")
    dhq = matmul(dqx, w["xq"], "nt", name=tag + "_dhq")
    dx2, g["g_cross"] = rms_bwd(s["x2"], w["g_cross"], dhq, dx3, name=tag + "_dxrms")
    g["w_xkv"] = matmul(s["mn"], dkv, "tn", name=tag + "_dwxkv")
    dmn = matmul(dkv, w["xkv"], "nt", name=tag + "_dmn")
    _, g["g_mem"] = rms_bwd(mem, w["g_mem"], dmn, None, name=tag + "_dmrms")
    P = s["P"]
    dmerged = matmul(dx2, w["w_o"], "nt", name=tag + "_dmerged")
    g["w_o"] = matmul(s["merged"], dx2, "tn", name=tag + "_dwo")
    dya, dyb, dyc, dgl, g["b_gate"] = merge_bwd(dmerged, P, s["ya"], s["yb"], s["yc"], w["b_gate"], tag)
    dya0 = matmul(dya, w["up_a"], "nt", name=tag + "_dya0")
    g["w_up_a"] = matmul(s["ya0"], dya, "tn", name=tag + "_dwupa")
    dxa, g["w_pool"], g["pool_scale"] = pool_bwd(dya0, s["d"], w["w_pool"], w["pool_scale"], tag)
    dyb0 = matmul(dyb, w["up_b"], "nt", name=tag + "_dyb0")
    g["w_up_b"] = matmul(s["yb0"], dyb, "tn", name=tag + "_dwupb")
    (dgb, dxb, g["w_rg_a"], g["b_rg_a"], g["w_rg_x"], g["b_rg_x"], g["lru_lambda"], g["conv_w"], g["conv_b"]) = lru_bwd(
        dyb0, P, s["hh"], s["xc"], w["conv_w"], w["w_rg_a"], w["b_rg_a"], w["w_rg_x"], w["b_rg_x"], w["lam"], tag)
    do2 = matmul(dyc, w["up_c"], "nt", name=tag + "_do2")
    g["w_up_c"] = matmul(s["o2"], dyc, "tn", name=tag + "_dwupc")
    do = _to_heads(do2)
    dqt, dk, dv, dck = flash_bwd(s["qa"], s["ka"], s["qs"], s["k"].transpose(0, 2, 1), s["v"], s["ot"], do,
                                 do.transpose(0, 2, 1), s["lse"], tag)
    dq = (dqt * (FOX_HD ** -0.5)).transpose(0, 1, 3, 2).reshape(dk.shape)
    dc = jnp.pad(dck[:, :, 0].T, ((0, 0), (0, LANES - FOX_HEADS)))
    dfl, g["b_f"] = cumlogf_bwd(dc, s["fl"], w["b_f"], tag)
    dP = jnp.concatenate([dxb, dgb, dgl, dxa] + [_from_heads(t).astype(BF) for t in (dq, dk, dv)], axis=1)
    du = matmul(dP, w["wp"], "nt", out_dtype=F32, name=tag + "_du")
    du = matmul(dfl, w["wfl"], "nt", out_dtype=F32, res=du, name=tag + "_duf")
    g["wp"] = matmul(s["u"], dP, "tn", name=tag + "_dwp")
    g["wfl"] = matmul(s["u"], dfl, "tn", out_dtype=F32, name=tag + "_dwfl")
    dx1, g["g_mix"] = rms_bwd(s["x1"], w["g_mix"], du, dx2, name=tag + "_dmixrms")
    dx0, g["g_ffn1"], g["w_ffn1_in"], g["w_ffn1_out"] = ffn_bwd(dx1, s["ffn1"], w["g_ffn1"], w["ffn1_in"], w["ffn1_out"], tag + "_ffn1")
    return dx0, g


def loss_head(x, target, g_final):
    def fn(step, i, n, x, t, g):
        r = _rstd(x)
        xhat = x * r
        e = xhat * g - t
        dy = e * (1.0 / D)
        gd = dy * g
        dx = r * (gd - xhat * jnp.mean(xhat * gd, axis=-1, keepdims=True))
        loss = 0.5 * jnp.sum(jnp.mean(e * e, axis=-1, keepdims=True), axis=0, keepdims=True)
        return (dx,), (jnp.broadcast_to(loss, (1, LANES)), _colsum(dy * xhat))

    (dx,), (loss, dg) = rowwise(fn, [x, target], [g_final], [(D, F32)], [((1, LANES), F32), ((1, D), F32)], ts=512, name="loss_head")
    return loss[0, 0], dx, dg


BIG = (("w_ffn1_in", (D, 2 * DFF), 1), ("w_ffn1_out", (DFF, D), 0), ("w_in", (D, 7176), 1), ("w_up_a", (POOL_W, D), 1),
       ("conv_w", (4, D), 1), ("w_up_b", (D, D), 0), ("w_up_c", (FOX_W, D), 1), ("w_o", (D, D), 0), ("w_xq", (D, D), 0),
       ("w_xkv", (D, 2 * D), 1), ("w_xo", (D, D), 0), ("w_ffn2_in", (D, 2 * DFF), 1), ("w_ffn2_out", (DFF, D), 0))
SMALL = (("g_ffn1", (D,)), ("g_mix", (D,)), ("b_f", (FOX_HEADS,)), ("b_gate", (3 * D,)), ("w_pool", (4, LANES, LANES)),
         ("pool_scale", (POOL_W,)), ("conv_b", (D,)), ("w_rg_a", (LRU_HEADS, LANES, LANES)), ("b_rg_a", (D,)),
         ("w_rg_x", (LRU_HEADS, LANES, LANES)), ("b_rg_x", (D,)), ("lru_lambda", (D,)), ("g_cross", (D,)), ("g_mem", (D,)),
         ("g_ffn2", (D,)))
ORDER = ("g_ffn1", "w_ffn1_in", "w_ffn1_out", "g_mix", "w_in", "b_f", "b_gate", "w_pool", "pool_scale", "w_up_a", "conv_w",
         "conv_b", "w_rg_a", "b_rg_a", "w_rg_x", "b_rg_x", "lru_lambda", "w_up_b", "w_up_c", "w_o", "g_cross", "g_mem", "w_xq",
         "w_xkv", "w_xo", "g_ffn2", "w_ffn2_in", "w_ffn2_out", "g_final")

IN_SPLIT = (("xa", 0, 512), ("xb", 512, 1024), ("gb", 1536, 1024), ("q", 2560, 512), ("k", 3072, 512), ("v", 3584, 512),
            ("fl", 4096, 8), ("gl", 4104, 3072))
PACK_ORDER = ("xb", "gb", "gl", "xa", "q", "k", "v")


def _shard_shape(shape, axis):
    s = list(shape)
    s[axis] //= N_CHIPS
    return (DEPTH, *s)


def _round_up(n, m):
    return -(-n // m) * m


def _pack_rows(parts, dtype, row_multiple):
    flat = jnp.concatenate([p.astype(dtype).reshape(-1) for p in parts])
    rows = _round_up(-(-flat.shape[0] // D), row_multiple)
    return jnp.pad(flat, (0, rows * D - flat.shape[0])).reshape(rows, D)


def _n_rows(shape, axis):
    return math.prod(_shard_shape(shape, axis)) // D


def _pad_tile_rows(x, axis):
    widths = [(0, 0)] * x.ndim
    widths[axis] = (0, _round_up(x.shape[axis], DMA_TILE_ROWS) - x.shape[axis])
    return jnp.pad(x, widths)


def _unshard(seg, shape, axis):
    t = jnp.moveaxis(seg, 0, axis + 1)
    return t.reshape(DEPTH, *shape)


def _shard_split(full, shape, axis):
    sh = list(shape)
    sh[axis:axis + 1] = [N_CHIPS, shape[axis] // N_CHIPS]
    t = full.reshape(DEPTH, *sh)
    t = jnp.moveaxis(t, axis + 1, 0)
    return t.reshape(N_CHIPS, -1, D)


def _gathered_weights(gath):
    out, off = {}, 0
    for name, shape, axis in BIG:
        sh = _shard_shape(shape, axis)
        nr = _n_rows(shape, axis)
        seg = gath[:, off:off + nr].reshape(N_CHIPS, *sh)
        off += _round_up(nr, DMA_TILE_ROWS)
        if name == "conv_w":
            for _ in range(2):
                seg = seg.astype(F32) + gath[:, off:off + nr].reshape(N_CHIPS, *sh).astype(F32)
                off += _round_up(nr, DMA_TILE_ROWS)
        out[name] = _unshard(seg, shape, axis)
    return out


def _layer_weights(full, small, l):
    w_in = full["w_in"][l]
    cols = {n: w_in[:, o:o + s] for n, o, s in IN_SPLIT}
    w = dict(ffn1_in=full["w_ffn1_in"][l], ffn1_out=full["w_ffn1_out"][l], ffn2_in=full["w_ffn2_in"][l], ffn2_out=full["w_ffn2_out"][l],
             wp=jnp.concatenate([cols[n] for n in PACK_ORDER], axis=1),
             wfl=jnp.pad(cols["fl"], ((0, 0), (0, LANES - FOX_HEADS))),
             up_a=full["w_up_a"][l], up_b=full["w_up_b"][l], up_c=full["w_up_c"][l], w_o=full["w_o"][l], xq=full["w_xq"][l],
             xkv=full["w_xkv"][l], xo=full["w_xo"][l], conv_w=full["conv_w"][l])
    for n in ("g_ffn1", "g_mix", "b_gate", "pool_scale", "conv_b", "b_rg_a", "b_rg_x", "g_cross", "g_mem", "g_ffn2"):
        w[n] = small[n][l].reshape(1, -1)
    w["lam"] = small["lru_lambda"][l].reshape(1, -1)
    w["b_f"] = jnp.pad(small["b_f"][l], (0, LANES - FOX_HEADS)).reshape(1, LANES)
    for n in ("w_pool", "w_rg_a", "w_rg_x"):
        w[n] = small[n][l]
    return w


def _unpack_w_in_grad(gwp, gwfl):
    pk = {}
    off = 0
    sizes = {n: s for n, _, s in IN_SPLIT}
    for n in PACK_ORDER:
        pk[n] = gwp[:, off:off + sizes[n]]
        off += sizes[n]
    pk["fl"] = gwfl[:, :FOX_HEADS].astype(gwp.dtype)
    return jnp.concatenate([pk[n] for n, _, _ in IN_SPLIT], axis=1)


def _place():
    x, y, c = lax.axis_index("x"), lax.axis_index("y"), lax.axis_index("c")
    others = [(1 - x, y), (x, 1 - y), (1 - x, 1 - y)]
    return x, y, c, others


def _rcopy(src, dst, send_sems, recv_sems, k, to):
    return pltpu.make_async_remote_copy(src_ref=src, dst_ref=dst, send_sem=send_sems.at[k], recv_sem=recv_sems.at[k],
                                        device_id=to, device_id_type=MESH)


DMA_TILE_ROWS = 16


def _cut_rows(ref, n):
    sz = ref.shape[0] // n
    return [ref.at[pl.ds(i * sz, sz)] for i in range(n)]


def _n_cuts(rows, most=8):
    n = most
    while n > 1 and rows % (n * DMA_TILE_ROWS):
        n //= 2
    return n


class _Striped:
    def __init__(self, pairs, whole, sems, to=None):
        if to is None:
            make = lambda s, d: pltpu.make_async_copy(s, d, sems[0])
        else:
            make = lambda s, d: pltpu.make_async_remote_copy(src_ref=s, dst_ref=d, send_sem=sems[0], recv_sem=sems[1],
                                                             device_id=to, device_id_type=MESH)
        self.parts = [make(s, d) for s, d in pairs]
        self.whole = make(*whole)

    def start(self):
        for p in self.parts:
            p.start()

    def wait(self):
        self.whole.wait()

    def wait_send(self):
        self.whole.wait_send()

    def wait_recv(self):
        self.whole.wait_recv()


def _striped_rows(src, dst, sems, to=None, most=8):
    n = _n_cuts(src.shape[0], most)
    return _Striped(list(zip(_cut_rows(src, n), _cut_rows(dst, n))), (src, dst), sems, to)


def _comm_call(body, name, ins, out_shape, n_sems):
    return pl.pallas_call(
        body, name=name, in_specs=[ANY] * len(ins), out_specs=[ANY] * len(out_shape), out_shape=out_shape,
        scratch_shapes=[pltpu.SemaphoreType.DMA((n_sems,)), pltpu.SemaphoreType.DMA((n_sems,)), pltpu.SemaphoreType.DMA((n_sems,))],
    )(*ins)


def allgather_chips(mine):
    rows = mine.shape[0]
    half = rows // 2
    nc = _n_cuts(half)
    piece = half // nc

    def body(x_ref, out_ref, send_sems, recv_sems, local_sems):
        x, y, c, others = _place()
        me = 2 * x + y

        def part(chip, hc, p):
            return out_ref.at[chip, pl.ds(pl.multiple_of(hc * half + p * piece, DMA_TILE_ROWS), piece), :]

        first, passed = [], []
        for p in range(nc):
            src = x_ref.at[pl.ds(pl.multiple_of(c * half + p * piece, DMA_TILE_ROWS), piece), :]
            for j, (ox, oy) in enumerate(others):
                cp = _rcopy(src, part(me, c, p), send_sems, recv_sems, j * nc + p, (ox, oy, c))
                cp.start()
                first.append(cp)
        for p in range(nc):
            for j, (ox, oy) in enumerate(others):
                theirs = part(2 * ox + oy, c, p)
                _rcopy(theirs, theirs, send_sems, recv_sems, j * nc + p, (ox, oy, c)).wait_recv()
                cp = _rcopy(theirs, theirs, send_sems, recv_sems, (3 + j) * nc + p, (x, y, 1 - c))
                cp.start()
                passed.append(cp)
        for p in range(nc):
            for j, (ox, oy) in enumerate(others):
                theirs = part(2 * ox + oy, 1 - c, p)
                _rcopy(theirs, theirs, send_sems, recv_sems, (3 + j) * nc + p, (x, y, 1 - c)).wait_recv()
        for cp in first + passed:
            cp.wait_send()

    out = _comm_call(body, "allgather_chips", [mine], [jax.ShapeDtypeStruct((N_CHIPS, rows, D), mine.dtype)], 6 * nc)[0]
    return _with_own_block(out, mine)


def allgather_same_core(bufs):
    nb = len(bufs)

    def body(*refs):
        ins, outs = refs[:nb], refs[nb:2 * nb]
        send_sems, recv_sems, local_sems = refs[2 * nb:]
        x, y, c, others = _place()
        me = 2 * x + y
        sends = []
        for b in range(nb):
            for j, (ox, oy) in enumerate(others):
                cp = _rcopy(ins[b], outs[b].at[me], send_sems, recv_sems, 3 * b + j, (ox, oy, c))
                cp.start()
                sends.append(cp)
        for b in range(nb):
            for j, (ox, oy) in enumerate(others):
                theirs = outs[b].at[2 * ox + oy]
                _rcopy(theirs, theirs, send_sems, recv_sems, 3 * b + j, (ox, oy, c)).wait_recv()
        for cp in sends:
            cp.wait_send()

    shapes = [jax.ShapeDtypeStruct((N_CHIPS, *b.shape), b.dtype) for b in bufs]
    outs = _comm_call(body, "allgather_same_core", list(bufs), shapes, 3 * nb)
    return [_with_own_block(o, b) for o, b in zip(outs, bufs)]


def sibling_halves(bufs):
    nb = len(bufs)

    def body(*refs):
        ins, outs = refs[:nb], refs[nb:2 * nb]
        send_sems, recv_sems, local_sems = refs[2 * nb:]
        x, y, c, _ = _place()
        work = []
        for b in range(nb):
            n = _n_cuts(ins[b].shape[2], 4)
            pieces = [pair for chip in range(N_CHIPS)
                      for pair in zip(_cut_rows(ins[b].at[chip, 1 - c], n), _cut_rows(outs[b].at[chip], n))]
            cp = _Striped(pieces, (ins[b].at[:, 1 - c], outs[b]), (send_sems.at[b], recv_sems.at[b]), to=(x, y, 1 - c))
            cp.start()
            work.append(cp)
        for cp in work:
            cp.wait()

    shapes = [jax.ShapeDtypeStruct((N_CHIPS, *b.shape[2:]), b.dtype) for b in bufs]
    return _comm_call(body, "sibling_halves", list(bufs), shapes, nb)


def chip_exchange(bufs):
    nb = len(bufs)

    def body(*refs):
        ins, outs = refs[:nb], refs[nb:2 * nb]
        send_sems, recv_sems, local_sems = refs[2 * nb:]
        x, y, c, others = _place()
        sends = []
        for b in range(nb):
            for j, (ox, oy) in enumerate(others):
                k = 3 * b + j
                cp = _striped_rows(ins[b].at[2 * ox + oy], outs[b].at[j], (send_sems.at[k], recv_sems.at[k]),
                                   to=(ox, oy, c), most=4)
                cp.start()
                sends.append(cp)
        for cp in sends:
            cp.wait()

    shapes = [jax.ShapeDtypeStruct((3, *b.shape[1:]), b.dtype) for b in bufs]
    return _comm_call(body, "chip_exchange", list(bufs), shapes, 3 * nb)


def sibling_swap(bufs):
    nb = len(bufs)

    def body(*refs):
        ins, outs = refs[:nb], refs[nb:2 * nb]
        send_sems, recv_sems, local_sems = refs[2 * nb:]
        x, y, c, _ = _place()
        work = []
        for b in range(nb):
            cp = _striped_rows(ins[b], outs[b], (send_sems.at[b], recv_sems.at[b]), to=(x, y, 1 - c), most=16)
            cp.start()
            work.append(cp)
        for cp in work:
            cp.wait()

    shapes = [jax.ShapeDtypeStruct(b.shape, b.dtype) for b in bufs]
    return _comm_call(body, "sibling_swap", list(bufs), shapes, nb)


def _with_own_block(gathered, mine):
    me = 2 * lax.axis_index("x") + lax.axis_index("y")
    is_me = (jnp.arange(N_CHIPS) == me).reshape((N_CHIPS,) + (1,) * mine.ndim)
    return jnp.where(is_me, mine[None], gathered)


def _by_core(mine, theirs):
    c0 = lax.axis_index("c") == 0
    return jnp.stack([jnp.where(c0, mine, theirs), jnp.where(c0, theirs, mine)])


def add_rows(parts, out_dtype, name):
    def fn(step, i, n, *vals):
        acc = vals[0].astype(F32)
        for v in vals[1:]:
            acc = acc + v.astype(F32)
        return (acc,), ()
    rows = parts[0].shape[0]
    return rowwise(fn, list(parts), outs=[(D, out_dtype)], ts=_pick(rows, (1024, 512, 256, 128, 64, 32, 16, 8)), name=name)[0][0]


def reduce_scatter(big, small):
    rb, rs = big.shape[1], small.shape[1]
    c, me = lax.axis_index("c"), 2 * lax.axis_index("x") + lax.axis_index("y")
    halves = [big.reshape(N_CHIPS, 2, rb // 2, D), small.reshape(N_CHIPS, 2, rs // 2, D)]
    theirs = sibling_halves(halves)
    mine = [lax.dynamic_index_in_dim(h, c, axis=1, keepdims=False) for h in halves]
    pb = add_rows([mine[0].reshape(-1, D), theirs[0].reshape(-1, D)], BF, "rs_add_sibling_big").reshape(N_CHIPS, rb // 2, D)
    ps = add_rows([mine[1].reshape(-1, D), theirs[1].reshape(-1, D)], F32, "rs_add_sibling_small").reshape(N_CHIPS, rs // 2, D)
    recv_b, recv_s = chip_exchange([pb, ps])
    own_b, own_s = (lax.dynamic_index_in_dim(p, me, axis=0, keepdims=False) for p in (pb, ps))
    qb = add_rows([own_b, recv_b[0], recv_b[1], recv_b[2]], F32, "rs_add_chips_big")
    qs = add_rows([own_s, recv_s[0], recv_s[1], recv_s[2]], F32, "rs_add_chips_small")
    tb, ts = sibling_swap([qb, qs])
    return _by_core(qb, tb).reshape(rb, D), _by_core(qs, ts).reshape(rs, D)


_BC1 = 1.0 - ADAM_B1 ** ADAM_STEP
_BC2 = 1.0 - ADAM_B2 ** ADAM_STEP


def adamw(w, g, m, v, name):
    rows, cols = w.shape

    def fn(step, i, n, w, g, m, v):
        m2 = ADAM_B1 * m + (1.0 - ADAM_B1) * g
        v2 = ADAM_B2 * v + (1.0 - ADAM_B2) * (g * g)
        delta = -ADAM_LR * ((m2 / _BC1) / (jnp.sqrt(v2 / _BC2) + ADAM_EPS) + ADAM_WD * w)
        return (delta, m2, v2), ()

    ts = _pick(rows, (256, 128, 64, 32, 16, 8))
    return rowwise(fn, [w, g, m, v], outs=[(cols, F32)] * 3, ts=ts, name=name)[0]


SMALL_ROW_MULTIPLE = 16
BIG_ROW_MULTIPLE = 512


def local_step(x, mem, target, full, small):
    ws = [_layer_weights(full, small, l) for l in range(DEPTH)]
    saved = []
    h = x
    for l in range(DEPTH):
        h, s = layer_fwd(h, mem, ws[l], "l")
        saved.append(s)
    loss, dh, dg_final = loss_head(h, target, small["g_final"].reshape(1, D))
    grads = [None] * DEPTH
    for l in reversed(range(DEPTH)):
        dh, grads[l] = layer_bwd(dh, mem, ws[l], saved[l], "l")
    out = {}
    for l in range(DEPTH):
        g = grads[l]
        g["w_in"] = _unpack_w_in_grad(g.pop("wp"), g.pop("wfl"))
        g["b_f"] = g["b_f"][:, :FOX_HEADS]
    for name, shape, _ in BIG:
        out[name] = jnp.stack([grads[l][name].reshape(shape) for l in range(DEPTH)])
    for name, shape in SMALL:
        out[name] = jnp.stack([grads[l][name].reshape(shape) for l in range(DEPTH)])
    out["g_final"] = dg_final.reshape(D)
    return loss, dh, out


def kernel(x, mem, g_ffn1, w_ffn1_in, w_ffn1_out, g_mix, w_in, b_f, b_gate, w_pool, pool_scale, w_up_a, conv_w, conv_b, w_rg_a, b_rg_a, w_rg_x, b_rg_x, lru_lambda, w_up_b, w_up_c, w_o, g_cross, g_mem, w_xq, w_xkv, w_xo, g_ffn2, w_ffn2_in, w_ffn2_out, g_final, loss_target, m_g_ffn1, m_w_ffn1_in, m_w_ffn1_out, m_g_mix, m_w_in, m_b_f, m_b_gate, m_w_pool, m_pool_scale, m_w_up_a, m_conv_w, m_conv_b, m_w_rg_a, m_b_rg_a, m_w_rg_x, m_b_rg_x, m_lru_lambda, m_w_up_b, m_w_up_c, m_w_o, m_g_cross, m_g_mem, m_w_xq, m_w_xkv, m_w_xo, m_g_ffn2, m_w_ffn2_in, m_w_ffn2_out, m_g_final, v_g_ffn1, v_w_ffn1_in, v_w_ffn1_out, v_g_mix, v_w_in, v_b_f, v_b_gate, v_w_pool, v_pool_scale, v_w_up_a, v_conv_w, v_conv_b, v_w_rg_a, v_b_rg_a, v_w_rg_x, v_b_rg_x, v_lru_lambda, v_w_up_b, v_w_up_c, v_w_o, v_g_cross, v_g_mem, v_w_xq, v_w_xkv, v_w_xo, v_g_ffn2, v_w_ffn2_in, v_w_ffn2_out, v_g_final):
    args = dict(locals())
    weights = {n: args[n] for n in ORDER}
    m_in = {n: args["m_" + n] for n in ORDER}
    v_in = {n: args["v_" + n] for n in ORDER}
    big_names = [n for n, _, _ in BIG]
    small_names = [n for n, _ in SMALL] + ["g_final"]

    parts = []
    for n in big_names:
        parts += [p.reshape(-1, D) for p in (_split3(weights[n]) if n == "conv_w" else (weights[n].astype(BF),))]
    mine = jnp.concatenate([_pad_tile_rows(p, 0) for p in parts], axis=0)
    mine = jnp.pad(mine, ((0, _round_up(mine.shape[0], BIG_ROW_MULTIPLE) - mine.shape[0]), (0, 0)))
    full = _gathered_weights(allgather_chips(mine))

    small = {n: weights[n] for n in small_names}
    loss, grad_x, contrib = local_step(x[0], mem[0], loss_target[0], full, small)
    loss = lax.psum(loss, ("x", "y", "c"))

    big_buf = jnp.concatenate([_pad_tile_rows(_shard_split(contrib[n].astype(BF), shape, axis), 1) for n, shape, axis in BIG], axis=1)
    rows_b = _round_up(big_buf.shape[1], BIG_ROW_MULTIPLE)
    big_buf = jnp.pad(big_buf, ((0, 0), (0, rows_b - big_buf.shape[1]), (0, 0)))
    small_flat = jnp.concatenate([contrib[n].reshape(-1) for n in small_names])
    rows_s = _round_up(-(-small_flat.shape[0] // (N_CHIPS * D)), SMALL_ROW_MULTIPLE)
    small_buf = jnp.pad(small_flat, (0, N_CHIPS * rows_s * D - small_flat.shape[0])).reshape(N_CHIPS, rows_s, D)
    red_big, red_small_mine = reduce_scatter(big_buf, small_buf)
    red_small = allgather_same_core([red_small_mine])[0].reshape(-1)

    grads = {}
    off = 0
    for name, shape, axis in BIG:
        nr = _n_rows(shape, axis)
        grads[name] = red_big[off:off + nr].reshape(_shard_shape(shape, axis))
        off += _round_up(nr, DMA_TILE_ROWS)
    off = 0
    for name in small_names:
        sh = weights[name].shape
        grads[name] = red_small[off:off + math.prod(sh)].reshape(sh)
        off += math.prod(sh)

    delta, new_m, new_v = {}, {}, {}
    for name in big_names:
        sh = weights[name].shape
        two_d = (-1, sh[-1])
        d, m2, v2 = adamw(weights[name].reshape(two_d), grads[name].reshape(two_d), m_in[name].reshape(two_d),
                          v_in[name].reshape(two_d), "adamw_" + name)
        delta[name], new_m[name], new_v[name] = d.reshape(sh), m2.reshape(sh), v2.reshape(sh)
    packed = [_pack_rows([t[n] for n in small_names], F32, 8) for t in (weights, grads, m_in)]
    v_flat = jnp.concatenate([v_in[n].reshape(-1) for n in small_names])
    v_pack = jnp.pad(v_flat, (0, packed[0].size - v_flat.shape[0]), constant_values=1.0).reshape(packed[0].shape)
    outs = adamw(packed[0], packed[1], packed[2], v_pack, "adamw_small")
    for t, res in zip((delta, new_m, new_v), outs):
        flat = res.reshape(-1)
        off = 0
        for name in small_names:
            sh = weights[name].shape
            t[name] = flat[off:off + math.prod(sh)].reshape(sh)
            off += math.prod(sh)

    return (loss, grad_x[None], *[grads[n] for n in ORDER], *[delta[n] for n in ORDER],
            *[new_m[n] for n in ORDER], *[new_v[n] for n in ORDER])
```

```python
import functools
import math

import jax
import jax.numpy as jnp
from jax import lax
from jax.experimental import pallas as pl
from jax.experimental.pallas import tpu as pltpu

F32 = jnp.float32
BF = jnp.bfloat16

D = 1024
DFF = 2816
DEPTH = 2
POOL_W = 512
POOL_WINDOWS = (2, 4, 8, 16)
LRU_HEADS = 8
LRU_C = 8.0
FOX_HEADS = 8
FOX_HD = 64
FOX_W = 512
X_HEADS = 4
X_HD = 256
EPS = 1e-6
LANES = 128
N_CHIPS = 4

ADAM_LR, ADAM_B1, ADAM_B2, ADAM_EPS, ADAM_WD, ADAM_STEP = 0.001, 0.9, 0.999, 1e-08, 0.01, 10

VMEM_LIMIT_BYTES = 56 * 2 ** 20

PK_XB, PK_GB, PK_GL, PK_XA, PK_Q, PK_K, PK_V = 0, 1024, 2048, 5120, 5632, 6144, 6656
PK_W = 7168

MESH = pl.DeviceIdType.MESH
ANY = pl.BlockSpec(memory_space=pl.ANY)


def _cparams(ngrid):
    return pltpu.CompilerParams(dimension_semantics=("arbitrary",) * ngrid, vmem_limit_bytes=VMEM_LIMIT_BYTES)


def _pick(n, cands):
    for c in cands:
        if n % c == 0:
            return c
    return n


def _iota(shape, dim):
    return lax.broadcasted_iota(jnp.int32, shape, dim)


def _sigmoid(x):
    return 1.0 / (1.0 + jnp.exp(-x))


def _softplus(z):
    return jnp.maximum(z, 0.0) + jnp.log1p(jnp.exp(-jnp.abs(z)))


def _expm1(x):
    small = jnp.abs(x) < 0.25
    xs = jnp.where(small, x, 0.0)
    poly = xs * (1.0 + xs * (1 / 2 + xs * (1 / 6 + xs * (1 / 24 + xs * (1 / 120 + xs * (1 / 720 + xs * (1 / 5040 + xs * (1 / 40320))))))))
    return jnp.where(small, poly, jnp.exp(x) - 1.0)


_GELU_K = math.sqrt(2.0 / math.pi)


def _gelu_and_grad(x):
    inner = _GELU_K * (x + 0.044715 * x * x * x)
    t = jnp.tanh(inner)
    g = 0.5 * x * (1.0 + t)
    dg = 0.5 * (1.0 + t) + 0.5 * x * (1.0 - t * t) * _GELU_K * (1.0 + 3 * 0.044715 * x * x)
    return g, dg


def _dot(a, b, dims):
    return lax.dot_general(a.astype(BF), b.astype(BF), (dims, ((), ())), preferred_element_type=F32)


def _dot_nn(a, b):
    return _dot(a, b, ((1,), (0,)))


def _dot_nt(a, b):
    return _dot(a, b, ((1,), (1,)))


def _dot_tn(a, b):
    return _dot(a, b, ((0,), (0,)))


def _colsum(x):
    return jnp.sum(x, axis=0, keepdims=True)


_TM = (1024, 1408, 512, 256, 128)
_TN = (1408, 1024, 512, 384, 256, 128)
_TK = (2432, 1408, 1024, 512, 256, 128)


def matmul(a, b, mode, *, out_dtype=None, alpha=1.0, res=None, name):
    out_dtype = BF if out_dtype is None else out_dtype
    if mode == "nn":
        (M, K), N = a.shape, b.shape[1]
    elif mode == "nt":
        (M, K), N = a.shape, b.shape[0]
    else:
        (K, M), N = a.shape, b.shape[1]
    tm, tn, tk = _pick(M, _TM), _pick(N, _TN), _pick(K, _TK)
    nk = K // tk
    dims = {"nn": ((1,), (0,)), "nt": ((1,), (1,)), "tn": ((0,), (0,))}[mode]
    a_spec = pl.BlockSpec((tk, tm), lambda i, j, k: (k, i)) if mode == "tn" else pl.BlockSpec((tm, tk), lambda i, j, k: (i, k))
    b_spec = pl.BlockSpec((tn, tk), lambda i, j, k: (j, k)) if mode == "nt" else pl.BlockSpec((tk, tn), lambda i, j, k: (k, j))
    o_spec = pl.BlockSpec((tm, tn), lambda i, j, k: (i, j))
    has_res = res is not None

    def body(*refs):
        a_ref, b_ref = refs[0], refs[1]
        r_ref = refs[2] if has_res else None
        o_ref, acc = refs[2 + has_res], refs[3 + has_res]
        k = pl.program_id(2)

        @pl.when(k == 0)
        def _():
            acc[...] = jnp.zeros_like(acc)

        acc[...] += _dot(a_ref[...], b_ref[...], dims)

        @pl.when(k == nk - 1)
        def _():
            r = acc[...]
            if alpha != 1.0:
                r = r * alpha
            if has_res:
                r = r + r_ref[...].astype(F32)
            o_ref[...] = r.astype(o_ref.dtype)

    ins = [a, b] + ([res] if has_res else [])
    specs = [a_spec, b_spec] + ([o_spec] if has_res else [])
    return pl.pallas_call(
        body, name=name, grid=(M // tm, N // tn, nk), in_specs=specs, out_specs=o_spec,
        out_shape=jax.ShapeDtypeStruct((M, N), out_dtype), scratch_shapes=[pltpu.VMEM((tm, tn), F32)],
        compiler_params=_cparams(3))(*ins)


class Rows:
    def __init__(self, arr, w=None, cb=0, halo=None, h=8):
        self.arr, self.w, self.cb, self.halo, self.h = arr, (arr.shape[1] if w is None else w), cb, halo, h


def rowwise(fn, rows, consts=(), outs=(), accs=(), scratch=(), *, ts, name, reverse=False):
    rows = [r if isinstance(r, Rows) else Rows(r) for r in rows]
    S = rows[0].arr.shape[0]
    n = S // ts
    tile = (lambda s: n - 1 - s) if reverse else (lambda s: s)

    def row_spec(r):
        if r.halo is None:
            return pl.BlockSpec((ts, r.w), lambda s: (tile(s), r.cb))
        per, last = ts // r.h, S // r.h - 1
        if r.halo == "prev":
            return pl.BlockSpec((r.h, r.w), lambda s: (jnp.maximum(tile(s) * per - 1, 0), r.cb))
        return pl.BlockSpec((r.h, r.w), lambda s: (jnp.minimum((tile(s) + 1) * per, last), r.cb))

    def whole(shape):
        nd = len(shape)
        return pl.BlockSpec(tuple(shape), lambda s: (0,) * nd)

    nr, nc, no, na = len(rows), len(consts), len(outs), len(accs)

    def body(*refs):
        in_refs, out_refs = refs[:nr + nc], refs[nr + nc:nr + nc + no]
        acc_refs, scr = refs[nr + nc + no:nr + nc + no + na], refs[nr + nc + no + na:]
        step = pl.program_id(0)
        o, inc = fn(step, tile(step), n, *[r[...] for r in in_refs], *scr)
        for ref, val in zip(out_refs, o):
            ref[...] = val.astype(ref.dtype)
        if na:
            @pl.when(step == 0)
            def _():
                for ref in acc_refs:
                    ref[...] = jnp.zeros_like(ref)
            for ref, val in zip(acc_refs, inc):
                ref[...] += val.astype(ref.dtype)

    out_shape = [jax.ShapeDtypeStruct((S, w), dt) for w, dt in outs] + [jax.ShapeDtypeStruct(tuple(s), dt) for s, dt in accs]
    out_specs = [pl.BlockSpec((ts, w), lambda s: (tile(s), 0)) for w, _ in outs] + [whole(s) for s, _ in accs]
    res = pl.pallas_call(
        body, name=name, grid=(n,),
        in_specs=[row_spec(r) for r in rows] + [whole(c.shape) for c in consts],
        out_specs=out_specs, out_shape=out_shape, scratch_shapes=list(scratch),
        compiler_params=_cparams(1))(*[r.arr for r in rows], *consts)
    return res[:no], res[no:]


def _rstd(x):
    return lax.rsqrt(jnp.mean(x * x, axis=-1, keepdims=True) + EPS)


def rms_fwd(x, g, *, name, ts=512):
    def fn(step, i, n, x, g):
        return ((x * _rstd(x)) * g,), ()
    return rowwise(fn, [x], [g], [(D, BF)], ts=min(ts, x.shape[0]), name=name)[0][0]


def rms_bwd(x, g, dh, dres, *, name, ts=512):
    has_res = dres is not None

    def fn(step, i, n, x, dh, *rest):
        g = rest[-1]
        r = _rstd(x)
        xhat = x * r
        dh = dh.astype(F32)
        gd = dh * g
        dx = r * (gd - xhat * jnp.mean(xhat * gd, axis=-1, keepdims=True))
        if has_res:
            dx = dx + rest[0]
        return (dx,), (_colsum(dh * xhat),)

    rows = [x, dh] + ([dres] if has_res else [])
    o, a = rowwise(fn, rows, [g], [(D, F32)], [((1, D), F32)], ts=min(ts, x.shape[0]), name=name)
    return o[0], a[0]


def ffn_fwd(x, g, w_in, w_out, tag):
    h = rms_fwd(x, g, name=tag + "_rms")
    ab = matmul(h, w_in, "nn", name=tag + "_in")

    def act_fn(step, i, n, ab):
        a, b = ab[:, :DFF].astype(F32), ab[:, DFF:].astype(F32)
        return ((a * _sigmoid(a)) * b,), ()

    act = rowwise(act_fn, [ab], outs=[(DFF, BF)], ts=256, name=tag + "_act")[0][0]
    y = matmul(act, w_out, "nn", out_dtype=F32, alpha=0.5, res=x, name=tag + "_out")
    return y, (x, h, ab, act)


def ffn_bwd(dy, saved, g, w_in, w_out, tag):
    x, h, ab, act = saved
    dact = matmul(dy, w_out, "nt", alpha=0.5, name=tag + "_dact")
    dw_out = matmul(act, dy, "tn", alpha=0.5, name=tag + "_dwout")

    def dab_fn(step, i, n, ab, dact):
        a, b, dact = ab[:, :DFF].astype(F32), ab[:, DFF:].astype(F32), dact.astype(F32)
        s = _sigmoid(a)
        da = dact * b * (s * (1.0 + a * (1.0 - s)))
        db = dact * (a * s)
        return (jnp.concatenate([da, db], axis=1),), ()

    dab = rowwise(dab_fn, [ab, dact], outs=[(2 * DFF, BF)], ts=256, name=tag + "_dab")[0][0]
    dw_in = matmul(h, dab, "tn", name=tag + "_dwin")
    dh = matmul(dab, w_in, "nt", name=tag + "_dh")
    dx, dg = rms_bwd(x, g, dh, dy, name=tag + "_drms")
    return dx, dg, dw_in, dw_out


POOL_TS = 256
POOL_HALO = 16


def pool_fwd(P, w_pool, pool_scale, tag):
    ts = POOL_TS

    def fn(step, i, n, xa, halo, w, scale):
        xa = xa.astype(F32)
        halo = halo.astype(F32) * jnp.where(i > 0, 1.0, 0.0)
        ext = jnp.concatenate([halo, xa], axis=0)
        pos = (i * ts + 1 + _iota((ts, LANES), 0)).astype(F32)
        ds, ys = [], []
        for gi, win in enumerate(POOL_WINDOWS):
            e = ext[:, gi * LANES:(gi + 1) * LANES]
            sh = 1
            while sh < win:
                e = e + pltpu.roll(e, sh, 0)
                sh *= 2
            mean = e[POOL_HALO:] / jnp.minimum(pos, float(win))
            d = mean - xa[:, gi * LANES:(gi + 1) * LANES]
            ds.append(d)
            ys.append(_dot_nn(d, w[gi]))
        d = jnp.concatenate(ds, axis=1)
        return (d, jnp.concatenate(ys, axis=1) * scale), ()

    xa = Rows(P, POOL_W, PK_XA // POOL_W)
    xa_prev = Rows(P, POOL_W, PK_XA // POOL_W, "prev", POOL_HALO)
    (d, ya0), _ = rowwise(fn, [xa, xa_prev], [w_pool, pool_scale], [(POOL_W, BF), (POOL_W, BF)], ts=ts, name=tag + "_pool")
    return d, ya0


def pool_bwd(dya0, d, w_pool, pool_scale, tag):
    ts = POOL_TS
    L = ts + POOL_HALO

    def fn(step, i, n, dya0, dya0_next, d, w, scale):
        d = d.astype(F32)
        dz = jnp.concatenate([dya0.astype(F32), dya0_next.astype(F32) * jnp.where(i < n - 1, 1.0, 0.0)], axis=0) * scale
        pos = (i * ts + 1 + _iota((L, LANES), 0)).astype(F32)
        dxa, dws, dsc = [], [], []
        for gi, win in enumerate(POOL_WINDOWS):
            sl = slice(gi * LANES, (gi + 1) * LANES)
            dzg = dz[:, sl]
            dd = _dot_nt(dzg, w[gi])
            e = dd / jnp.minimum(pos, float(win))
            sh = 1
            while sh < win:
                e = e + pltpu.roll(e, L - sh, 0)
                sh *= 2
            dxa.append(e[:ts] - dd[:ts])
            dws.append(_dot_tn(d[:, sl], dzg[:ts])[None])
            z = _dot_nn(d[:, sl], w[gi])
            dsc.append(_colsum(dya0[:, sl].astype(F32) * z))
        return (jnp.concatenate(dxa, axis=1),), (jnp.concatenate(dws, axis=0), jnp.concatenate(dsc, axis=1))

    (dxa,), (dw_pool, dscale) = rowwise(
        fn, [dya0, Rows(dya0, halo="next", h=POOL_HALO), d], [w_pool, pool_scale],
        [(POOL_W, BF)], [((4, LANES, LANES), F32), ((1, POOL_W), F32)], ts=ts, name=tag + "_dpool")
    return dxa, dw_pool, dscale


LRU_R = 256


def _scan_fwd(A, U):
    R = A.shape[0]
    row = _iota(A.shape, 0)
    d = 1
    while d < R:
        m = row >= d
        A_sh = jnp.where(m, pltpu.roll(A, d, 0), 1.0)
        U_sh = jnp.where(m, pltpu.roll(U, d, 0), 0.0)
        U = A * U_sh + U
        A = A * A_sh
        d *= 2
    return A, U


def _scan_bwd(B, X):
    R = B.shape[0]
    row = _iota(B.shape, 0)
    d = 1
    while d < R:
        m = row < R - d
        B_sh = jnp.where(m, pltpu.roll(B, R - d, 0), 1.0)
        X_sh = jnp.where(m, pltpu.roll(X, R - d, 0), 0.0)
        X = X + B * X_sh
        B = B * B_sh
        d *= 2
    return X


def _lru_gates(xc, wa, ba, wx, bx, lam):
    r = _sigmoid(_dot_nn(xc, wa) + ba)
    ig = _sigmoid(_dot_nn(xc, wx) + bx)
    sp = _softplus(-lam)
    log_a = -LRU_C * r * sp
    a = jnp.exp(log_a)
    mult = jnp.sqrt(-_expm1(2.0 * log_a))
    return r, ig, sp, a, mult


def _lru_specs(S, R, reverse):
    nch = S // R
    ch = (lambda j: nch - 1 - j) if reverse else (lambda j: j)
    per = R // 8

    def col(off):
        return pl.BlockSpec((R, LANES), lambda h, j: (ch(j), off + h))

    def prev(off):
        return pl.BlockSpec((8, LANES), lambda h, j: (jnp.maximum(ch(j) * per - 1, 0), off + h))

    vec = pl.BlockSpec((1, LANES), lambda h, j: (0, h))
    cw = pl.BlockSpec((4, LANES), lambda h, j: (0, h))
    wsq = pl.BlockSpec((None, LANES, LANES), lambda h, j: (h, 0, 0))
    return nch, ch, col, prev, vec, cw, wsq


def lru_fwd(P, conv_w, conv_b, w_a, b_a, w_x, b_x, lam, tag):
    S = P.shape[0]
    R = min(LRU_R, S)
    nch, ch, col, prev, vec, cw_spec, wsq = _lru_specs(S, R, False)

    def body(xb_ref, halo_ref, gb_ref, cw_ref, cb_ref, wa_ref, ba_ref, wx_ref, bx_ref, lam_ref, xc_ref, h_ref, yb_ref, carry):
        j = pl.program_id(1)
        xb = xb_ref[...].astype(F32)
        halo = halo_ref[...].astype(F32) * jnp.where(j > 0, 1.0, 0.0)
        ext = jnp.concatenate([halo, xb], axis=0)
        cw = cw_ref[...]
        xc = cb_ref[...]
        for k in range(4):
            e = ext if k == 3 else pltpu.roll(ext, 3 - k, 0)
            xc = xc + e[8:] * cw[k:k + 1]
        r, ig, sp, a, mult = _lru_gates(xc, wa_ref[...], ba_ref[...], wx_ref[...], bx_ref[...], lam_ref[...])
        u = mult * (ig * xc)
        cum_a, hloc = _scan_fwd(a, u)

        @pl.when(j == 0)
        def _():
            carry[...] = jnp.zeros_like(carry)

        hfull = hloc + cum_a * carry[0:1]
        carry[...] = jnp.broadcast_to(hfull[R - 1:R], carry.shape)
        gel, _ = _gelu_and_grad(gb_ref[...].astype(F32))
        xc_ref[...] = xc
        h_ref[...] = hfull
        yb_ref[...] = (hfull * gel).astype(yb_ref.dtype)

    out = pl.BlockSpec((R, LANES), lambda h, j: (j, h))
    return pl.pallas_call(
        body, name=tag + "_lru", grid=(LRU_HEADS, nch),
        in_specs=[col(PK_XB // LANES), prev(PK_XB // LANES), col(PK_GB // LANES), cw_spec, vec, wsq, vec, wsq, vec, vec],
        out_specs=[out, out, out],
        out_shape=[jax.ShapeDtypeStruct((S, D), F32), jax.ShapeDtypeStruct((S, D), F32), jax.ShapeDtypeStruct((S, D), BF)],
        scratch_shapes=[pltpu.VMEM((8, LANES), F32)], compiler_params=_cparams(2),
    )(P, P, P, conv_w, conv_b, w_a, b_a, w_x, b_x, lam)


def lru_bwd(dyb0, P, hh, xc, conv_w, w_a, b_a, w_x, b_x, lam, tag):
    S = P.shape[0]
    R = min(LRU_R, S)
    nch, ch, col, prev, vec, cw_spec, wsq = _lru_specs(S, R, True)

    def body(dyb_ref, gb_ref, h_ref, hprev_ref, xc_ref, xb_ref, xbprev_ref, cw_ref, wa_ref, ba_ref, wx_ref, bx_ref, lam_ref,
             dgb_ref, dxb_ref, dwa_ref, dba_ref, dwx_ref, dbx_ref, dlam_ref, dcw_ref, dcb_ref, gcarry, dxc_head):
        j = pl.program_id(1)
        jj = nch - 1 - j
        has_prev = jnp.where(jj > 0, 1.0, 0.0)
        row = _iota((R, LANES), 0)
        xc, hv, lam = xc_ref[...], h_ref[...], lam_ref[...]
        wa, wx = wa_ref[...], wx_ref[...]
        r, ig, sp, a, mult = _lru_gates(xc, wa, ba_ref[...], wx, bx_ref[...], lam)
        dyb = dyb_ref[...].astype(F32)
        gel, dgel = _gelu_and_grad(gb_ref[...].astype(F32))
        dgb_ref[...] = (dyb * hv * dgel).astype(dgb_ref.dtype)

        @pl.when(j == 0)
        def _():
            gcarry[...] = jnp.zeros_like(gcarry)
            dxc_head[...] = jnp.zeros_like(dxc_head)
            for ref in (dwa_ref, dba_ref, dwx_ref, dbx_ref, dlam_ref, dcw_ref, dcb_ref):
                ref[...] = jnp.zeros_like(ref)

        B = jnp.where(row < R - 1, pltpu.roll(a, R - 1, 0), 0.0)
        X = dyb * gel + jnp.where(row == R - 1, gcarry[0:1], 0.0)
        G = _scan_bwd(B, X)
        gcarry[...] = jnp.broadcast_to(a[0:1] * G[0:1], gcarry.shape)
        hprev = jnp.where(row == 0, hprev_ref[...][7:8] * has_prev, pltpu.roll(hv, 1, 0))
        da = G * hprev
        dmult = G * ig * xc
        dig = G * mult * xc
        dxc = G * mult * ig
        dlog_a = da * a - dmult * (a * a) / mult
        dzr = dlog_a * (-LRU_C * sp) * (r * (1.0 - r))
        dzi = dig * (ig * (1.0 - ig))
        dxc = dxc + _dot_nt(dzr, wa) + _dot_nt(dzi, wx)
        dwa_ref[...] += _dot_tn(xc, dzr)
        dwx_ref[...] += _dot_tn(xc, dzi)
        dba_ref[...] += _colsum(dzr)
        dbx_ref[...] += _colsum(dzi)
        dlam_ref[...] += _colsum(dlog_a * (-LRU_C * r)) * (-_sigmoid(-lam))
        dcb_ref[...] += _colsum(dxc)
        cw = cw_ref[...]
        ext = jnp.concatenate([dxc, dxc_head[...]], axis=0)
        dxb = dxc * cw[3:4]
        for k in range(3):
            dxb = dxb + pltpu.roll(ext, R + 8 - (3 - k), 0)[:R] * cw[k:k + 1]
        dxb_ref[...] = dxb.astype(dxb_ref.dtype)
        dxc_head[...] = dxc[0:8]
        extx = jnp.concatenate([xbprev_ref[...].astype(F32) * has_prev, xb_ref[...].astype(F32)], axis=0)
        incs = []
        for k in range(4):
            e = extx if k == 3 else pltpu.roll(extx, 3 - k, 0)
            incs.append(_colsum(dxc * e[8:]))
        dcw_ref[...] += jnp.concatenate(incs, axis=0)

    plain = pl.BlockSpec((R, LANES), lambda h, j: (ch(j), h))
    plain_prev = pl.BlockSpec((8, LANES), lambda h, j: (jnp.maximum(ch(j) * (R // 8) - 1, 0), h))
    return pl.pallas_call(
        body, name=tag + "_dlru", grid=(LRU_HEADS, nch),
        in_specs=[plain, col(PK_GB // LANES), plain, plain_prev, plain, col(PK_XB // LANES), prev(PK_XB // LANES),
                  cw_spec, wsq, vec, wsq, vec, vec],
        out_specs=[plain, plain, wsq, vec, wsq, vec, vec, cw_spec, vec],
        out_shape=[jax.ShapeDtypeStruct((S, D), BF), jax.ShapeDtypeStruct((S, D), BF),
                   jax.ShapeDtypeStruct((LRU_HEADS, LANES, LANES), F32), jax.ShapeDtypeStruct((1, D), F32),
                   jax.ShapeDtypeStruct((LRU_HEADS, LANES, LANES), F32), jax.ShapeDtypeStruct((1, D), F32),
                   jax.ShapeDtypeStruct((1, D), F32), jax.ShapeDtypeStruct((4, D), F32), jax.ShapeDtypeStruct((1, D), F32)],
        scratch_shapes=[pltpu.VMEM((8, LANES), F32), pltpu.VMEM((8, LANES), F32)], compiler_params=_cparams(2),
    )(dyb0, P, hh, hh, xc, P, P, conv_w, w_a, b_a, w_x, b_x, lam)


CUM_TS = 512
FLASH_T = 512
FLASH_G = 2
NEG = -1e30


def cumlogf_fwd(fl, b_f, tag):
    ts = min(CUM_TS, fl.shape[0])

    def fn(step, i, n, fl, bf, carry):
        x = -_softplus(-(fl + bf))
        row = _iota(x.shape, 0)
        d = 1
        while d < ts:
            x = x + jnp.where(row >= d, pltpu.roll(x, d, 0), 0.0)
            d *= 2

        @pl.when(step == 0)
        def _():
            carry[...] = jnp.zeros_like(carry)

        c = x + carry[0:1]
        carry[...] = jnp.broadcast_to(c[ts - 1:ts], carry.shape)
        return (c,), ()

    return rowwise(fn, [fl], [b_f], [(LANES, F32)], scratch=[pltpu.VMEM((8, LANES), F32)], ts=ts, name=tag + "_cum")[0][0]


def cumlogf_bwd(dc, fl, b_f, tag):
    ts = min(CUM_TS, fl.shape[0])

    def fn(step, i, n, dc, fl, bf, carry):
        row = _iota(dc.shape, 0)
        x = dc
        d = 1
        while d < ts:
            x = x + jnp.where(row < ts - d, pltpu.roll(x, ts - d, 0), 0.0)
            d *= 2

        @pl.when(step == 0)
        def _():
            carry[...] = jnp.zeros_like(carry)

        g = x + carry[0:1]
        carry[...] = jnp.broadcast_to(g[0:1], carry.shape)
        dfl = g * _sigmoid(-(fl + bf))
        return (dfl,), (_colsum(dfl),)

    (dfl,), (db,) = rowwise(fn, [dc, fl], [b_f], [(LANES, F32)], [((1, LANES), F32)], scratch=[pltpu.VMEM((8, LANES), F32)],
                            ts=ts, name=tag + "_dcum", reverse=True)
    return dfl, db


FOX_KA = 80


def _split3(c):
    c = c.astype(F32)
    hi = lax.reduce_precision(c, 8, 7)
    r = c - hi
    mid = lax.reduce_precision(r, 8, 7)
    return hi.astype(BF), mid.astype(BF), (r - mid).astype(BF)


def fox_operands(q, k, ch):
    H, S, hd = q.shape
    qs = q * (FOX_HD ** -0.5)
    pieces = [p.astype(F32) for p in _split3(ch)]
    pad = FOX_KA - hd

    def unit(i):
        return (jnp.arange(FOX_KA) == hd + i).astype(F32)

    qa = jnp.pad(qs.transpose(0, 2, 1).astype(F32), ((0, 0), (0, pad), (0, 0)))
    ka = jnp.pad(k.astype(F32), ((0, 0), (0, 0), (0, pad)))
    for i, p in enumerate(pieces):
        qa = qa + p[:, None, :] * unit(i)[None, :, None] + unit(3 + i)[None, :, None]
        ka = ka + unit(i)[None, None, :] - p[:, :, None] * unit(3 + i)[None, None, :]
    return qa.astype(BF), ka.astype(BF), qs


def _causal(s):
    return jnp.where(_iota(s.shape, 0) <= _iota(s.shape, 1), s, NEG)


def flash_fwd(qa, ka, vt, tag):
    H, hd, S = vt.shape
    T = min(FLASH_T, S)
    nb = S // T

    pairs = [(qi, ki) for qi in range(nb) for ki in range(qi + 1)]
    q_tab = jnp.asarray([p[0] for p in pairs], jnp.int32)
    k_tab = jnp.asarray([p[1] for p in pairs], jnp.int32)

    def body(q_tab_ref, k_tab_ref, ka_ref, qa_ref, vt_ref, o_ref, lse_ref, m_s, l_s, acc):
        qi, ki = q_tab_ref[pl.program_id(1)], k_tab_ref[pl.program_id(1)]

        @pl.when(ki == 0)
        def _():
            m_s[...] = jnp.full_like(m_s, NEG)
            l_s[...] = jnp.zeros_like(l_s)
            acc[...] = jnp.zeros_like(acc)

        def step(diagonal):
            for g in range(FLASH_G):
                s = _dot_nn(ka_ref[g], qa_ref[g])
                if diagonal:
                    s = _causal(s)
                m_new = jnp.maximum(m_s[g], jnp.max(s, axis=0, keepdims=True))
                alpha = jnp.exp(m_s[g] - m_new)
                p = jnp.exp(s - m_new)
                l_s[g] = alpha * l_s[g] + jnp.sum(p, axis=0, keepdims=True)
                p_hi = p.astype(BF)
                p_lo = p - p_hi.astype(F32)
                v = vt_ref[g]
                acc[g] = alpha * acc[g] + (_dot_nn(v, p_hi) + _dot_nn(v, p_lo))
                m_s[g] = m_new

        @pl.when(ki < qi)
        def _():
            step(False)

        @pl.when(ki == qi)
        def _():
            step(True)
            o_ref[...] = acc[...] / l_s[...]
            lse_ref[...] = m_s[...] + jnp.log(l_s[...])

    G = FLASH_G
    grid_spec = pltpu.PrefetchScalarGridSpec(
        num_scalar_prefetch=2, grid=(H // G, len(pairs)),
        in_specs=[pl.BlockSpec((G, T, FOX_KA), lambda h, p, qt, kt: (h, kt[p], 0)),
                  pl.BlockSpec((G, FOX_KA, T), lambda h, p, qt, kt: (h, 0, qt[p])),
                  pl.BlockSpec((G, hd, T), lambda h, p, qt, kt: (h, 0, kt[p]))],
        out_specs=[pl.BlockSpec((G, hd, T), lambda h, p, qt, kt: (h, 0, qt[p])),
                   pl.BlockSpec((G, 1, T), lambda h, p, qt, kt: (h, 0, qt[p]))],
        scratch_shapes=[pltpu.VMEM((G, 1, T), F32), pltpu.VMEM((G, 1, T), F32), pltpu.VMEM((G, hd, T), F32)])
    return pl.pallas_call(
        body, name=tag + "_flash", grid_spec=grid_spec,
        out_shape=[jax.ShapeDtypeStruct((H, hd, S), F32), jax.ShapeDtypeStruct((H, 1, S), F32)],
        compiler_params=_cparams(2))(q_tab, k_tab, ka, qa, vt)


def flash_bwd(qa, ka, qs, kt, v, ot, do, dot_, lse, tag):
    H, S, hd = v.shape
    T = min(FLASH_T, S)
    nb = S // T

    pairs = [(qi, ki) for ki in range(nb) for qi in range(ki, nb)]
    q_tab = jnp.asarray([p[0] for p in pairs], jnp.int32)
    k_tab = jnp.asarray([p[1] for p in pairs], jnp.int32)

    def body(q_tab_ref, k_tab_ref, ka_ref, qa_ref, qs_ref, kt_ref, v_ref, ot_ref, do_ref, dot_ref, lse_ref,
             dqt_ref, dk_ref, dv_ref, dc_ref, dk_acc, dv_acc, dc_acc):
        qi, ki = q_tab_ref[pl.program_id(1)], k_tab_ref[pl.program_id(1)]

        @pl.when(pl.program_id(1) == 0)
        def _():
            dqt_ref[...] = jnp.zeros_like(dqt_ref)

        @pl.when(qi == ki)
        def _():
            dk_acc[...] = jnp.zeros_like(dk_acc)
            dv_acc[...] = jnp.zeros_like(dv_acc)
            dc_acc[...] = jnp.zeros_like(dc_acc)

        def step(diagonal):
            for g in range(FLASH_G):
                s = _dot_nn(ka_ref[g], qa_ref[g])
                if diagonal:
                    s = _causal(s)
                p = jnp.exp(s - lse_ref[g])
                dot_v = dot_ref[g]
                dp = _dot_nn(v_ref[g], dot_v)
                delta = jnp.sum(dot_v.astype(F32) * ot_ref[g], axis=0, keepdims=True)
                ds = p * (dp - delta)
                part = ds[:, 0:LANES]
                for j in range(1, T // LANES):
                    part = part + ds[:, j * LANES:(j + 1) * LANES]
                dc_acc[g] += part
                dsb = ds.astype(BF)
                dv_acc[g] += _dot_nn(p, do_ref[g])
                dk_acc[g] += _dot_nn(dsb, qs_ref[g])
                dqt_ref[g, qi] += _dot_nn(kt_ref[g], dsb)

        @pl.when(qi > ki)
        def _():
            step(False)

        @pl.when(qi == ki)
        def _():
            step(True)

        @pl.when(qi == nb - 1)
        def _():
            dk_ref[...] = dk_acc[...]
            dv_ref[...] = dv_acc[...]
            for g in range(FLASH_G):
                dc_ref[g] = -jnp.sum(dc_acc[g].T, axis=0, keepdims=True)

    qblk = lambda h, p, qt, kt_: (h, qt[p], 0)
    qblk_t = lambda h, p, qt, kt_: (h, 0, qt[p])
    kblk = lambda h, p, qt, kt_: (h, kt_[p], 0)
    kblk_t = lambda h, p, qt, kt_: (h, 0, kt_[p])
    G = FLASH_G
    grid_spec = pltpu.PrefetchScalarGridSpec(
        num_scalar_prefetch=2, grid=(H // G, len(pairs)),
        in_specs=[pl.BlockSpec((G, T, FOX_KA), kblk), pl.BlockSpec((G, FOX_KA, T), qblk_t),
                  pl.BlockSpec((G, T, hd), qblk), pl.BlockSpec((G, hd, T), kblk_t), pl.BlockSpec((G, T, hd), kblk),
                  pl.BlockSpec((G, hd, T), qblk_t), pl.BlockSpec((G, T, hd), qblk), pl.BlockSpec((G, hd, T), qblk_t),
                  pl.BlockSpec((G, 1, T), qblk_t)],
        out_specs=[pl.BlockSpec((G, nb, hd, T), lambda h, p, qt, kt_: (h, 0, 0, 0)), pl.BlockSpec((G, T, hd), kblk),
                   pl.BlockSpec((G, T, hd), kblk), pl.BlockSpec((G, 1, T), kblk_t)],
        scratch_shapes=[pltpu.VMEM((G, T, hd), F32), pltpu.VMEM((G, T, hd), F32), pltpu.VMEM((G, T, LANES), F32)])
    return pl.pallas_call(
        body, name=tag + "_dflash", grid_spec=grid_spec,
        out_shape=[jax.ShapeDtypeStruct((H, nb, hd, T), F32), jax.ShapeDtypeStruct((H, S, hd), F32),
                   jax.ShapeDtypeStruct((H, S, hd), F32), jax.ShapeDtypeStruct((H, 1, S), F32)],
        compiler_params=_cparams(2))(q_tab, k_tab, ka, qa, qs, kt, v, ot, do, dot_, lse)


def _to_heads(x2d):
    S = x2d.shape[0]
    return x2d.reshape(S, FOX_HEADS, FOX_HD).transpose(1, 0, 2)


def _from_heads(x3d):
    S = x3d.shape[1]
    return x3d.transpose(1, 0, 2).reshape(S, FOX_W)


def _gl_rows(P):
    return [Rows(P, D, PK_GL // D + k) for k in range(3)]


def merge_fwd(P, ya, yb, yc, b_gate, tag):
    def fn(step, i, n, g0, g1, g2, ya, yb, yc, b):
        out = 0.0
        for k, (gl, y) in enumerate(((g0, ya), (g1, yb), (g2, yc))):
            out = out + _sigmoid(gl.astype(F32) + b[:, k * D:(k + 1) * D]) * y.astype(F32)
        return (out,), ()
    return rowwise(fn, _gl_rows(P) + [ya, yb, yc], [b_gate], [(D, BF)], ts=512, name=tag + "_merge")[0][0]


def merge_bwd(dm, P, ya, yb, yc, b_gate, tag):
    def fn(step, i, n, dm, g0, g1, g2, ya, yb, yc, b):
        dm = dm.astype(F32)
        dys, dgls = [], []
        for k, (gl, y) in enumerate(((g0, ya), (g1, yb), (g2, yc))):
            g = _sigmoid(gl.astype(F32) + b[:, k * D:(k + 1) * D])
            dys.append(dm * g)
            dgls.append(dm * y.astype(F32) * (g * (1.0 - g)))
        dgl = jnp.concatenate(dgls, axis=1)
        return (dys[0], dys[1], dys[2], dgl), (_colsum(dgl),)
    (dya, dyb, dyc, dgl), (db,) = rowwise(
        fn, [dm] + _gl_rows(P) + [ya, yb, yc], [b_gate], [(D, BF), (D, BF), (D, BF), (3 * D, BF)], [((1, 3 * D), F32)],
        ts=256, name=tag + "_dmerge")
    return dya, dyb, dyc, dgl, db


def _xattn_probs(qh, kh):
    s = _dot_nt(qh, kh) * (X_HD ** -0.5)
    e = jnp.exp(s - jnp.max(s, axis=1, keepdims=True))
    return e / jnp.sum(e, axis=1, keepdims=True)


def xattn_fwd(q, kv, tag):
    def fn(step, i, n, q, k, v):
        os = []
        for h in range(X_HEADS):
            sl = slice(h * X_HD, (h + 1) * X_HD)
            os.append(_dot_nn(_xattn_probs(q[:, sl], k[:, sl]), v[:, sl]))
        return (jnp.concatenate(os, axis=1),), ()

    return _xattn_call(fn, [q], kv, [(D, BF)], [], tag + "_xattn")[0][0]


def _xattn_call(fn, rows, kv, outs, accs, name):
    S, ts, M = rows[0].shape[0], 512, kv.shape[0]
    ts = min(ts, S)
    nr, no, na = len(rows), len(outs), len(accs)

    def body(*refs):
        in_refs, out_refs, acc_refs = refs[:nr + 2], refs[nr + 2:nr + 2 + no], refs[nr + 2 + no:]
        step = pl.program_id(0)
        o, inc = fn(step, step, S // ts, *[r[...] for r in in_refs])
        for ref, val in zip(out_refs, o):
            ref[...] = val.astype(ref.dtype)
        if na:
            @pl.when(step == 0)
            def _():
                for ref in acc_refs:
                    ref[...] = jnp.zeros_like(ref)
            for ref, val in zip(acc_refs, inc):
                ref[...] += val

    row = pl.BlockSpec((ts, D), lambda s: (s, 0))
    res = pl.pallas_call(
        body, name=name, grid=(S // ts,),
        in_specs=[row] * nr + [pl.BlockSpec((M, D), lambda s: (0, 0)), pl.BlockSpec((M, D), lambda s: (0, 1))],
        out_specs=[row] * no + [pl.BlockSpec(tuple(s), lambda s_: (0, 0)) for s, _ in accs],
        out_shape=[jax.ShapeDtypeStruct((S, w), dt) for w, dt in outs] + [jax.ShapeDtypeStruct(tuple(s), dt) for s, dt in accs],
        compiler_params=_cparams(1))(*rows, kv, kv)
    return res[:no], res[no:]


def xattn_bwd(q, kv, do, tag):
    def fn(step, i, n, q, do, k, v):
        dqs, dks, dvs = [], [], []
        for h in range(X_HEADS):
            sl = slice(h * X_HD, (h + 1) * X_HD)
            p = _xattn_probs(q[:, sl], k[:, sl])
            dp = _dot_nt(do[:, sl], v[:, sl])
            ds = (p * (dp - jnp.sum(p * dp, axis=1, keepdims=True)) * (X_HD ** -0.5)).astype(BF)
            dqs.append(_dot_nn(ds, k[:, sl]))
            dks.append(_dot_tn(ds, q[:, sl]))
            dvs.append(_dot_tn(p, do[:, sl]))
        return (jnp.concatenate(dqs, axis=1),), (jnp.concatenate(dks + dvs, axis=1),)

    (dq,), (dkv,) = _xattn_call(fn, [q, do], kv, [(D, BF)], [((kv.shape[0], 2 * D), F32)], tag + "_dxattn")
    return dq, dkv


def layer_fwd(x, mem, w, tag):
    x1, s_ffn1 = ffn_fwd(x, w["g_ffn1"], w["ffn1_in"], w["ffn1_out"], tag + "_ffn1")
    u = rms_fwd(x1, w["g_mix"], name=tag + "_mixrms")
    P = matmul(u, w["wp"], "nn", name=tag + "_proj")
    fl = matmul(u, w["wfl"], "nn", out_dtype=F32, name=tag + "_projf")
    d, ya0 = pool_fwd(P, w["w_pool"], w["pool_scale"], tag)
    ya = matmul(ya0, w["up_a"], "nn", name=tag + "_upa")
    xc, hh, yb0 = lru_fwd(P, w["conv_w"], w["conv_b"], w["w_rg_a"], w["b_rg_a"], w["w_rg_x"], w["b_rg_x"], w["lam"], tag)
    yb = matmul(yb0, w["up_b"], "nn", name=tag + "_upb")
    c = cumlogf_fwd(fl, w["b_f"], tag)
    q, k, v = (_to_heads(P[:, o:o + FOX_W]) for o in (PK_Q, PK_K, PK_V))
    qa, ka, qs = fox_operands(q, k, c[:, :FOX_HEADS].T)
    ot, lse = flash_fwd(qa, ka, v.transpose(0, 2, 1), tag)
    o2 = ot.transpose(2, 0, 1).reshape(-1, FOX_W)
    yc = matmul(o2, w["up_c"], "nn", name=tag + "_upc")
    merged = merge_fwd(P, ya, yb, yc, w["b_gate"], tag)
    x2 = matmul(merged, w["w_o"], "nn", out_dtype=F32, res=x1, name=tag + "_wo")
    hq = rms_fwd(x2, w["g_cross"], name=tag + "_xrms")
    qx = matmul(hq, w["xq"], "nn", name=tag + "_xq")
    mn = rms_fwd(mem, w["g_mem"], name=tag + "_mrms")
    kv = matmul(mn, w["xkv"], "nn", name=tag + "_xkv")
    ox = xattn_fwd(qx, kv, tag)
    x3 = matmul(ox, w["xo"], "nn", out_dtype=F32, res=x2, name=tag + "_xo")
    x4, s_ffn2 = ffn_fwd(x3, w["g_ffn2"], w["ffn2_in"], w["ffn2_out"], tag + "_ffn2")
    saved = dict(ffn1=s_ffn1, ffn2=s_ffn2, x1=x1, u=u, P=P, fl=fl, d=d, ya0=ya0, ya=ya, xc=xc, hh=hh, yb0=yb0, yb=yb,
                 qa=qa, ka=ka, qs=qs, k=k, v=v, ot=ot, lse=lse, o2=o2, yc=yc, merged=merged, x2=x2, hq=hq, qx=qx,
                 mn=mn, kv=kv, ox=ox)
    return x4, saved


def layer_bwd(dx4, mem, w, s, tag):
    g = {}
    dx3, g["g_ffn2"], g["w_ffn2_in"], g["w_ffn2_out"] = ffn_bwd(dx4, s["ffn2"], w["g_ffn2"], w["ffn2_in"], w["ffn2_out"], tag + "_ffn2")
    dox = matmul(dx3, w["xo"], "nt", name=tag + "_dox")
    g["w_xo"] = matmul(s["ox"], dx3, "tn", name=tag + "_dwxo")
    dqx, dkv = xattn_bwd(s["qx"], s["kv"], dox, tag)
    g["w_xq"] = matmul(s["hq"], dqx, "tn", name=tag + "_dwxq")
    dhq = matmul(dqx, w["xq"], "nt", name=tag + "_dhq")
    dx2, g["g_cross"] = rms_bwd(s["x2"], w["g_cross"], dhq, dx3, name=tag + "_dxrms")
    g["w_xkv"] = matmul(s["mn"], dkv, "tn", name=tag + "_dwxkv")
    dmn = matmul(dkv, w["xkv"], "nt", name=tag + "_dmn")
    _, g["g_mem"] = rms_bwd(mem, w["g_mem"], dmn, None, name=tag + "_dmrms")
    P = s["P"]
    dmerged = matmul(dx2, w["w_o"], "nt", name=tag + "_dmerged")
    g["w_o"] = matmul(s["merged"], dx2, "tn", name=tag + "_dwo")
    dya, dyb, dyc, dgl, g["b_gate"] = merge_bwd(dmerged, P, s["ya"], s["yb"], s["yc"], w["b_gate"], tag)
    dya0 = matmul(dya, w["up_a"], "nt", name=tag + "_dya0")
    g["w_up_a"] = matmul(s["ya0"], dya, "tn", name=tag + "_dwupa")
    dxa, g["w_pool"], g["pool_scale"] = pool_bwd(dya0, s["d"], w["w_pool"], w["pool_scale"], tag)
    dyb0 = matmul(dyb, w["up_b"], "nt", name=tag + "_dyb0")
    g["w_up_b"] = matmul(s["yb0"], dyb, "tn", name=tag + "_dwupb")
    (dgb, dxb, g["w_rg_a"], g["b_rg_a"], g["w_rg_x"], g["b_rg_x"], g["lru_lambda"], g["conv_w"], g["conv_b"]) = lru_bwd(
        dyb0, P, s["hh"], s["xc"], w["conv_w"], w["w_rg_a"], w["b_rg_a"], w["w_rg_x"], w["b_rg_x"], w["lam"], tag)
    do2 = matmul(dyc, w["up_c"], "nt", name=tag + "_do2")
    g["w_up_c"] = matmul(s["o2"], dyc, "tn", name=tag + "_dwupc")
    do = _to_heads(do2)
    dqt, dk, dv, dck = flash_bwd(s["qa"], s["ka"], s["qs"], s["k"].transpose(0, 2, 1), s["v"], s["ot"], do,
                                 do.transpose(0, 2, 1), s["lse"], tag)
    dq = (dqt * (FOX_HD ** -0.5)).transpose(0, 1, 3, 2).reshape(dk.shape)
    dc = jnp.pad(dck[:, 0, :].T, ((0, 0), (0, LANES - FOX_HEADS)))
    dfl, g["b_f"] = cumlogf_bwd(dc, s["fl"], w["b_f"], tag)
    dP = jnp.concatenate([dxb, dgb, dgl, dxa] + [_from_heads(t).astype(BF) for t in (dq, dk, dv)], axis=1)
    du = matmul(dP, w["wp"], "nt", out_dtype=F32, name=tag + "_du")
    du = matmul(dfl, w["wfl"], "nt", out_dtype=F32, res=du, name=tag + "_duf")
    g["wp"] = matmul(s["u"], dP, "tn", name=tag + "_dwp")
    g["wfl"] = matmul(s["u"], dfl, "tn", out_dtype=F32, name=tag + "_dwfl")
    dx1, g["g_mix"] = rms_bwd(s["x1"], w["g_mix"], du, dx2, name=tag + "_dmixrms")
    dx0, g["g_ffn1"], g["w_ffn1_in"], g["w_ffn1_out"] = ffn_bwd(dx1, s["ffn1"], w["g_ffn1"], w["ffn1_in"], w["ffn1_out"], tag + "_ffn1")
    return dx0, g


def loss_head(x, target, g_final):
    def fn(step, i, n, x, t, g):
        r = _rstd(x)
        xhat = x * r
        e = xhat * g - t
        dy = e * (1.0 / D)
        gd = dy * g
        dx = r * (gd - xhat * jnp.mean(xhat * gd, axis=-1, keepdims=True))
        loss = 0.5 * jnp.sum(jnp.mean(e * e, axis=-1, keepdims=True), axis=0, keepdims=True)
        return (dx,), (jnp.broadcast_to(loss, (1, LANES)), _colsum(dy * xhat))

    (dx,), (loss, dg) = rowwise(fn, [x, target], [g_final], [(D, F32)], [((1, LANES), F32), ((1, D), F32)], ts=512, name="loss_head")
    return loss[0, 0], dx, dg


BIG = (("w_ffn1_in", (D, 2 * DFF), 1), ("w_ffn1_out", (DFF, D), 0), ("w_in", (D, 7176), 1), ("w_up_a", (POOL_W, D), 1),
       ("conv_w", (4, D), 1), ("w_up_b", (D, D), 0), ("w_up_c", (FOX_W, D), 1), ("w_o", (D, D), 0), ("w_xq", (D, D), 0),
       ("w_xkv", (D, 2 * D), 1), ("w_xo", (D, D), 0), ("w_ffn2_in", (D, 2 * DFF), 1), ("w_ffn2_out", (DFF, D), 0))
SMALL = (("g_ffn1", (D,)), ("g_mix", (D,)), ("b_f", (FOX_HEADS,)), ("b_gate", (3 * D,)), ("w_pool", (4, LANES, LANES)),
         ("pool_scale", (POOL_W,)), ("conv_b", (D,)), ("w_rg_a", (LRU_HEADS, LANES, LANES)), ("b_rg_a", (D,)),
         ("w_rg_x", (LRU_HEADS, LANES, LANES)), ("b_rg_x", (D,)), ("lru_lambda", (D,)), ("g_cross", (D,)), ("g_mem", (D,)),
         ("g_ffn2", (D,)))
ORDER = ("g_ffn1", "w_ffn1_in", "w_ffn1_out", "g_mix", "w_in", "b_f", "b_gate", "w_pool", "pool_scale", "w_up_a", "conv_w",
         "conv_b", "w_rg_a", "b_rg_a", "w_rg_x", "b_rg_x", "lru_lambda", "w_up_b", "w_up_c", "w_o", "g_cross", "g_mem", "w_xq",
         "w_xkv", "w_xo", "g_ffn2", "w_ffn2_in", "w_ffn2_out", "g_final")

IN_SPLIT = (("xa", 0, 512), ("xb", 512, 1024), ("gb", 1536, 1024), ("q", 2560, 512), ("k", 3072, 512), ("v", 3584, 512),
            ("fl", 4096, 8), ("gl", 4104, 3072))
PACK_ORDER = ("xb", "gb", "gl", "xa", "q", "k", "v")


def _shard_shape(shape, axis):
    s = list(shape)
    s[axis] //= N_CHIPS
    return (DEPTH, *s)


def _round_up(n, m):
    return -(-n // m) * m


SMALL_SEG_ROWS = 8


def _small_rows(n_elements):
    return _round_up(-(-n_elements // D), SMALL_SEG_ROWS)


def _pack_small(tree, names, row_multiple=SMALL_SEG_ROWS, fill=0.0):
    parts = []
    for n in names:
        flat = tree[n].astype(F32).reshape(-1)
        rows = _small_rows(flat.shape[0])
        parts.append(jnp.pad(flat, (0, rows * D - flat.shape[0]), constant_values=fill).reshape(rows, D))
    total = sum(p.shape[0] for p in parts)
    if total % row_multiple:
        parts.append(jnp.full((_round_up(total, row_multiple) - total, D), fill, F32))
    return jnp.concatenate(parts, axis=0)


def _unpack_small(buf, like, names):
    out, off = {}, 0
    for n in names:
        sh = like[n].shape
        rows = _small_rows(math.prod(sh))
        out[n] = buf[off:off + rows].reshape(-1)[:math.prod(sh)].reshape(sh)
        off += rows
    return out


def _concat_rows(parts, axis, row_multiple):
    total = sum(p.shape[axis] for p in parts)
    if total % row_multiple:
        shape = list(parts[0].shape)
        shape[axis] = _round_up(total, row_multiple) - total
        parts = parts + [jnp.zeros(shape, parts[0].dtype)]
    return jnp.concatenate(parts, axis=axis)


def _n_rows(shape, axis):
    return math.prod(_shard_shape(shape, axis)) // D


def _pad_tile_rows(x, axis):
    widths = [(0, 0)] * x.ndim
    widths[axis] = (0, _round_up(x.shape[axis], DMA_TILE_ROWS) - x.shape[axis])
    return jnp.pad(x, widths)


def _unshard(seg, shape, axis):
    t = jnp.moveaxis(seg, 0, axis + 1)
    return t.reshape(DEPTH, *shape)


def _shard_split(full, shape, axis):
    sh = list(shape)
    sh[axis:axis + 1] = [N_CHIPS, shape[axis] // N_CHIPS]
    t = full.reshape(DEPTH, *sh)
    t = jnp.moveaxis(t, axis + 1, 0)
    return t.reshape(N_CHIPS, -1, D)


def _gathered_weights(gath):
    out, off = {}, 0
    for name, shape, axis in BIG:
        sh = _shard_shape(shape, axis)
        nr = _n_rows(shape, axis)
        seg = gath[:, off:off + nr].reshape(N_CHIPS, *sh)
        off += _round_up(nr, DMA_TILE_ROWS)
        if name == "conv_w":
            for _ in range(2):
                seg = seg.astype(F32) + gath[:, off:off + nr].reshape(N_CHIPS, *sh).astype(F32)
                off += _round_up(nr, DMA_TILE_ROWS)
        out[name] = _unshard(seg, shape, axis)
    return out


def _layer_weights(full, small, l):
    w_in = full["w_in"][l]
    cols = {n: w_in[:, o:o + s] for n, o, s in IN_SPLIT}
    w = dict(ffn1_in=full["w_ffn1_in"][l], ffn1_out=full["w_ffn1_out"][l], ffn2_in=full["w_ffn2_in"][l], ffn2_out=full["w_ffn2_out"][l],
             wp=jnp.concatenate([cols[n] for n in PACK_ORDER], axis=1),
             wfl=jnp.pad(cols["fl"], ((0, 0), (0, LANES - FOX_HEADS))),
             up_a=full["w_up_a"][l], up_b=full["w_up_b"][l], up_c=full["w_up_c"][l], w_o=full["w_o"][l], xq=full["w_xq"][l],
             xkv=full["w_xkv"][l], xo=full["w_xo"][l], conv_w=full["conv_w"][l])
    for n in ("g_ffn1", "g_mix", "b_gate", "pool_scale", "conv_b", "b_rg_a", "b_rg_x", "g_cross", "g_mem", "g_ffn2"):
        w[n] = small[n][l].reshape(1, -1)
    w["lam"] = small["lru_lambda"][l].reshape(1, -1)
    w["b_f"] = jnp.pad(small["b_f"][l], (0, LANES - FOX_HEADS)).reshape(1, LANES)
    for n in ("w_pool", "w_rg_a", "w_rg_x"):
        w[n] = small[n][l]
    return w


def _unpack_w_in_grad(gwp, gwfl):
    pk = {}
    off = 0
    sizes = {n: s for n, _, s in IN_SPLIT}
    for n in PACK_ORDER:
        pk[n] = gwp[:, off:off + sizes[n]]
        off += sizes[n]
    pk["fl"] = gwfl[:, :FOX_HEADS].astype(gwp.dtype)
    return jnp.concatenate([pk[n] for n, _, _ in IN_SPLIT], axis=1)


def _place():
    x, y, c = lax.axis_index("x"), lax.axis_index("y"), lax.axis_index("c")
    others = [(1 - x, y), (x, 1 - y), (1 - x, 1 - y)]
    return x, y, c, others


def _rcopy(src, dst, send_sems, recv_sems, k, to):
    return pltpu.make_async_remote_copy(src_ref=src, dst_ref=dst, send_sem=send_sems.at[k], recv_sem=recv_sems.at[k],
                                        device_id=to, device_id_type=MESH)


DMA_TILE_ROWS = 16


def _cut_rows(ref, n):
    sz = ref.shape[0] // n
    return [ref.at[pl.ds(i * sz, sz)] for i in range(n)]


def _n_cuts(rows, most=8):
    n = most
    while n > 1 and rows % (n * DMA_TILE_ROWS):
        n //= 2
    return n


class _Striped:
    def __init__(self, pairs, whole, sems, to=None):
        if to is None:
            make = lambda s, d: pltpu.make_async_copy(s, d, sems[0])
        else:
            make = lambda s, d: pltpu.make_async_remote_copy(src_ref=s, dst_ref=d, send_sem=sems[0], recv_sem=sems[1],
                                                             device_id=to, device_id_type=MESH)
        self.parts = [make(s, d) for s, d in pairs]
        self.whole = make(*whole)

    def start(self):
        for p in self.parts:
            p.start()

    def wait(self):
        self.whole.wait()

    def wait_send(self):
        self.whole.wait_send()

    def wait_recv(self):
        self.whole.wait_recv()


def _striped_rows(src, dst, sems, to=None, most=8):
    n = _n_cuts(src.shape[0], most)
    return _Striped(list(zip(_cut_rows(src, n), _cut_rows(dst, n))), (src, dst), sems, to)


def _comm_call(body, name, ins, out_shape, n_sems):
    return pl.pallas_call(
        body, name=name, in_specs=[ANY] * len(ins), out_specs=[ANY] * len(out_shape), out_shape=out_shape,
        scratch_shapes=[pltpu.SemaphoreType.DMA((n_sems,)), pltpu.SemaphoreType.DMA((n_sems,)), pltpu.SemaphoreType.DMA((n_sems,))],
    )(*ins)


def allgather_chips(mine):
    rows = mine.shape[0]
    half = rows // 2
    nc = _n_cuts(half)
    piece = half // nc

    def body(x_ref, out_ref, send_sems, recv_sems, local_sems):
        x, y, c, others = _place()
        me = 2 * x + y

        def part(chip, hc, p):
            return out_ref.at[chip, pl.ds(pl.multiple_of(hc * half + p * piece, DMA_TILE_ROWS), piece), :]

        first, passed = [], []
        for p in range(nc):
            src = x_ref.at[pl.ds(pl.multiple_of(c * half + p * piece, DMA_TILE_ROWS), piece), :]
            for j, (ox, oy) in enumerate(others):
                cp = _rcopy(src, part(me, c, p), send_sems, recv_sems, j * nc + p, (ox, oy, c))
                cp.start()
                first.append(cp)
        for p in range(nc):
            for j, (ox, oy) in enumerate(others):
                theirs = part(2 * ox + oy, c, p)
                _rcopy(theirs, theirs, send_sems, recv_sems, j * nc + p, (ox, oy, c)).wait_recv()
                cp = _rcopy(theirs, theirs, send_sems, recv_sems, (3 + j) * nc + p, (x, y, 1 - c))
                cp.start()
                passed.append(cp)
        for p in range(nc):
            for j, (ox, oy) in enumerate(others):
                theirs = part(2 * ox + oy, 1 - c, p)
                _rcopy(theirs, theirs, send_sems, recv_sems, (3 + j) * nc + p, (x, y, 1 - c)).wait_recv()
        for cp in first + passed:
            cp.wait_send()

    out = _comm_call(body, "allgather_chips", [mine], [jax.ShapeDtypeStruct((N_CHIPS, rows, D), mine.dtype)], 6 * nc)[0]
    return _with_own_block(out, mine)


def allgather_same_core(bufs):
    nb = len(bufs)

    def body(*refs):
        ins, outs = refs[:nb], refs[nb:2 * nb]
        send_sems, recv_sems, local_sems = refs[2 * nb:]
        x, y, c, others = _place()
        me = 2 * x + y
        sends = []
        for b in range(nb):
            for j, (ox, oy) in enumerate(others):
                cp = _rcopy(ins[b], outs[b].at[me], send_sems, recv_sems, 3 * b + j, (ox, oy, c))
                cp.start()
                sends.append(cp)
        for b in range(nb):
            for j, (ox, oy) in enumerate(others):
                theirs = outs[b].at[2 * ox + oy]
                _rcopy(theirs, theirs, send_sems, recv_sems, 3 * b + j, (ox, oy, c)).wait_recv()
        for cp in sends:
            cp.wait_send()

    shapes = [jax.ShapeDtypeStruct((N_CHIPS, *b.shape), b.dtype) for b in bufs]
    outs = _comm_call(body, "allgather_same_core", list(bufs), shapes, 3 * nb)
    return [_with_own_block(o, b) for o, b in zip(outs, bufs)]


def sibling_halves(bufs):
    nb = len(bufs)

    def body(*refs):
        ins, outs = refs[:nb], refs[nb:2 * nb]
        send_sems, recv_sems, local_sems = refs[2 * nb:]
        x, y, c, _ = _place()
        work = []
        for b in range(nb):
            n = _n_cuts(ins[b].shape[2], 4)
            pieces = [pair for chip in range(N_CHIPS)
                      for pair in zip(_cut_rows(ins[b].at[chip, 1 - c], n), _cut_rows(outs[b].at[chip], n))]
            cp = _Striped(pieces, (ins[b].at[:, 1 - c], outs[b]), (send_sems.at[b], recv_sems.at[b]), to=(x, y, 1 - c))
            cp.start()
            work.append(cp)
        for cp in work:
            cp.wait()

    shapes = [jax.ShapeDtypeStruct((N_CHIPS, *b.shape[2:]), b.dtype) for b in bufs]
    return _comm_call(body, "sibling_halves", list(bufs), shapes, nb)


def chip_exchange(bufs):
    nb = len(bufs)

    def body(*refs):
        ins, outs = refs[:nb], refs[nb:2 * nb]
        send_sems, recv_sems, local_sems = refs[2 * nb:]
        x, y, c, others = _place()
        sends = []
        for b in range(nb):
            for j, (ox, oy) in enumerate(others):
                k = 3 * b + j
                cp = _striped_rows(ins[b].at[2 * ox + oy], outs[b].at[j], (send_sems.at[k], recv_sems.at[k]),
                                   to=(ox, oy, c), most=4)
                cp.start()
                sends.append(cp)
        for cp in sends:
            cp.wait()

    shapes = [jax.ShapeDtypeStruct((3, *b.shape[1:]), b.dtype) for b in bufs]
    return _comm_call(body, "chip_exchange", list(bufs), shapes, 3 * nb)


def sibling_swap(bufs):
    nb = len(bufs)

    def body(*refs):
        ins, outs = refs[:nb], refs[nb:2 * nb]
        send_sems, recv_sems, local_sems = refs[2 * nb:]
        x, y, c, _ = _place()
        work = []
        for b in range(nb):
            cp = _striped_rows(ins[b], outs[b].at[c], (send_sems.at[b], recv_sems.at[b]), to=(x, y, 1 - c), most=16)
            cp.start()
            work.append(cp)
        for b, cp in enumerate(work):
            cp.wait_send()
            theirs = outs[b].at[1 - c]
            _rcopy(theirs, theirs, send_sems, recv_sems, b, (x, y, 1 - c)).wait_recv()

    shapes = [jax.ShapeDtypeStruct((2, *b.shape), b.dtype) for b in bufs]
    return _comm_call(body, "sibling_swap", list(bufs), shapes, nb)


def _with_own_block(gathered, mine):
    me = 2 * lax.axis_index("x") + lax.axis_index("y")
    return lax.dynamic_update_index_in_dim(gathered, mine, me, axis=0)


def _by_core(mine, theirs):
    return lax.dynamic_update_index_in_dim(theirs, mine, lax.axis_index("c"), axis=0)


def add_rows(parts, out_dtype, name):
    def fn(step, i, n, *vals):
        acc = vals[0].astype(F32)
        for v in vals[1:]:
            acc = acc + v.astype(F32)
        return (acc,), ()
    rows = parts[0].shape[0]
    return rowwise(fn, list(parts), outs=[(D, out_dtype)], ts=_pick(rows, (1024, 512, 256, 128, 64, 32, 16, 8)), name=name)[0][0]


def reduce_scatter(big, small):
    rb, rs = big.shape[1], small.shape[1]
    c, me = lax.axis_index("c"), 2 * lax.axis_index("x") + lax.axis_index("y")
    halves = [big.reshape(N_CHIPS, 2, rb // 2, D), small.reshape(N_CHIPS, 2, rs // 2, D)]
    theirs = sibling_halves(halves)
    mine = [lax.dynamic_index_in_dim(h, c, axis=1, keepdims=False) for h in halves]
    pb = add_rows([mine[0].reshape(-1, D), theirs[0].reshape(-1, D)], BF, "rs_add_sibling_big").reshape(N_CHIPS, rb // 2, D)
    ps = add_rows([mine[1].reshape(-1, D), theirs[1].reshape(-1, D)], F32, "rs_add_sibling_small").reshape(N_CHIPS, rs // 2, D)
    recv_b, recv_s = chip_exchange([pb, ps])
    own_b, own_s = (lax.dynamic_index_in_dim(p, me, axis=0, keepdims=False) for p in (pb, ps))
    qb = add_rows([own_b, recv_b[0], recv_b[1], recv_b[2]], F32, "rs_add_chips_big")
    qs = add_rows([own_s, recv_s[0], recv_s[1], recv_s[2]], F32, "rs_add_chips_small")
    tb, ts = sibling_swap([qb, qs])
    return _by_core(qb, tb).reshape(rb, D), _by_core(qs, ts).reshape(rs, D)


_BC1 = 1.0 - ADAM_B1 ** ADAM_STEP
_BC2 = 1.0 - ADAM_B2 ** ADAM_STEP


def adamw(w, g, m, v, name):
    rows, cols = w.shape

    def fn(step, i, n, w, g, m, v):
        m2 = ADAM_B1 * m + (1.0 - ADAM_B1) * g
        v2 = ADAM_B2 * v + (1.0 - ADAM_B2) * (g * g)
        delta = -ADAM_LR * ((m2 / _BC1) / (jnp.sqrt(v2 / _BC2) + ADAM_EPS) + ADAM_WD * w)
        return (delta, m2, v2), ()

    ts = _pick(rows, (256, 128, 64, 32, 16, 8))
    return rowwise(fn, [w, g, m, v], outs=[(cols, F32)] * 3, ts=ts, name=name)[0]


SMALL_ROW_MULTIPLE = 16
BIG_ROW_MULTIPLE = 512


def local_step(x, mem, target, full, small):
    ws = [_layer_weights(full, small, l) for l in range(DEPTH)]
    saved = []
    h = x
    for l in range(DEPTH):
        h, s = layer_fwd(h, mem, ws[l], "l")
        saved.append(s)
    loss, dh, dg_final = loss_head(h, target, small["g_final"].reshape(1, D))
    grads = [None] * DEPTH
    for l in reversed(range(DEPTH)):
        dh, grads[l] = layer_bwd(dh, mem, ws[l], saved[l], "l")
    out = {}
    for l in range(DEPTH):
        g = grads[l]
        g["w_in"] = _unpack_w_in_grad(g.pop("wp"), g.pop("wfl"))
        g["b_f"] = g["b_f"][:, :FOX_HEADS]
    for name, shape, _ in BIG:
        out[name] = jnp.stack([grads[l][name].reshape(shape) for l in range(DEPTH)])
    for name, shape in SMALL:
        out[name] = jnp.stack([grads[l][name].reshape(shape) for l in range(DEPTH)])
    out["g_final"] = dg_final.reshape(D)
    return loss, dh, out


def kernel(x, mem, g_ffn1, w_ffn1_in, w_ffn1_out, g_mix, w_in, b_f, b_gate, w_pool, pool_scale, w_up_a, conv_w, conv_b, w_rg_a, b_rg_a, w_rg_x, b_rg_x, lru_lambda, w_up_b, w_up_c, w_o, g_cross, g_mem, w_xq, w_xkv, w_xo, g_ffn2, w_ffn2_in, w_ffn2_out, g_final, loss_target, m_g_ffn1, m_w_ffn1_in, m_w_ffn1_out, m_g_mix, m_w_in, m_b_f, m_b_gate, m_w_pool, m_pool_scale, m_w_up_a, m_conv_w, m_conv_b, m_w_rg_a, m_b_rg_a, m_w_rg_x, m_b_rg_x, m_lru_lambda, m_w_up_b, m_w_up_c, m_w_o, m_g_cross, m_g_mem, m_w_xq, m_w_xkv, m_w_xo, m_g_ffn2, m_w_ffn2_in, m_w_ffn2_out, m_g_final, v_g_ffn1, v_w_ffn1_in, v_w_ffn1_out, v_g_mix, v_w_in, v_b_f, v_b_gate, v_w_pool, v_pool_scale, v_w_up_a, v_conv_w, v_conv_b, v_w_rg_a, v_b_rg_a, v_w_rg_x, v_b_rg_x, v_lru_lambda, v_w_up_b, v_w_up_c, v_w_o, v_g_cross, v_g_mem, v_w_xq, v_w_xkv, v_w_xo, v_g_ffn2, v_w_ffn2_in, v_w_ffn2_out, v_g_final):
    args = dict(locals())
    weights = {n: args[n] for n in ORDER}
    m_in = {n: args["m_" + n] for n in ORDER}
    v_in = {n: args["v_" + n] for n in ORDER}
    big_names = [n for n, _, _ in BIG]
    small_names = [n for n, _ in SMALL] + ["g_final"]

    parts = []
    for n in big_names:
        parts += [p.reshape(-1, D) for p in (_split3(weights[n]) if n == "conv_w" else (weights[n].astype(BF),))]
    mine = _concat_rows([_pad_tile_rows(p, 0) for p in parts], 0, BIG_ROW_MULTIPLE)
    full = _gathered_weights(allgather_chips(mine))

    small = {n: weights[n] for n in small_names}
    loss, grad_x, contrib = local_step(x[0], mem[0], loss_target[0], full, small)
    loss = lax.psum(loss, ("x", "y", "c"))

    big_buf = _concat_rows([_pad_tile_rows(_shard_split(contrib[n].astype(BF), shape, axis), 1) for n, shape, axis in BIG],
                           1, BIG_ROW_MULTIPLE)
    small_buf = _pack_small(contrib, small_names, N_CHIPS * SMALL_ROW_MULTIPLE)
    red_big, red_small_mine = reduce_scatter(big_buf, small_buf.reshape(N_CHIPS, -1, D))
    red_small = allgather_same_core([red_small_mine])[0].reshape(-1, D)

    grads = {}
    off = 0
    for name, shape, axis in BIG:
        nr = _n_rows(shape, axis)
        grads[name] = red_big[off:off + nr].reshape(_shard_shape(shape, axis))
        off += _round_up(nr, DMA_TILE_ROWS)
    grads.update(_unpack_small(red_small, weights, small_names))

    delta, new_m, new_v = {}, {}, {}
    for name in big_names:
        sh = weights[name].shape
        two_d = (-1, sh[-1])
        d, m2, v2 = adamw(weights[name].reshape(two_d), grads[name].reshape(two_d), m_in[name].reshape(two_d),
                          v_in[name].reshape(two_d), "adamw_" + name)
        delta[name], new_m[name], new_v[name] = d.reshape(sh), m2.reshape(sh), v2.reshape(sh)
    outs = adamw(_pack_small(weights, small_names), _pack_small(grads, small_names), _pack_small(m_in, small_names),
                 _pack_small(v_in, small_names, fill=1.0), "adamw_small")
    for t, res in zip((delta, new_m, new_v), outs):
        t.update(_unpack_small(res, weights, small_names))

    return (loss, grad_x[None], *[grads[n] for n in ORDER], *[delta[n] for n in ORDER],
            *[new_m[n] for n in ORDER], *[new_v[n] for n in ORDER])
```

```python
import math

import jax
import jax.numpy as jnp
from jax import lax
from jax.experimental import pallas as pl
from jax.experimental.pallas import tpu as pltpu

F32 = jnp.float32
BF = jnp.bfloat16

D = 1024
DFF = 2816
DEPTH = 2
POOL_W = 512
POOL_WINDOWS = (2, 4, 8, 16)
LRU_HEADS = 8
LRU_C = 8.0
FOX_HEADS = 8
FOX_HD = 64
FOX_W = 512
X_HEADS = 4
X_HD = 256
EPS = 1e-6
LANES = 128
N_CHIPS = 4

ADAM_LR, ADAM_B1, ADAM_B2, ADAM_EPS, ADAM_WD, ADAM_STEP = 0.001, 0.9, 0.999, 1e-08, 0.01, 10

VMEM_LIMIT_BYTES = 56 * 2 ** 20

PK_XB, PK_GB, PK_GL, PK_XA, PK_Q, PK_K, PK_V = 0, 1024, 2048, 5120, 5632, 6144, 6656
PK_W = 7168

MESH = pl.DeviceIdType.MESH
ANY = pl.BlockSpec(memory_space=pl.ANY)


def _cparams(ngrid):
    return pltpu.CompilerParams(dimension_semantics=("arbitrary",) * ngrid, vmem_limit_bytes=VMEM_LIMIT_BYTES)


def _pick(n, cands):
    for c in cands:
        if n % c == 0:
            return c
    return n


def _iota(shape, dim):
    return lax.broadcasted_iota(jnp.int32, shape, dim)


def _sigmoid(x):
    return 1.0 / (1.0 + jnp.exp(-x))


def _softplus(z):
    return jnp.maximum(z, 0.0) + jnp.log1p(jnp.exp(-jnp.abs(z)))


def _expm1(x):
    small = jnp.abs(x) < 0.25
    xs = jnp.where(small, x, 0.0)
    poly = xs * (1.0 + xs * (1 / 2 + xs * (1 / 6 + xs * (1 / 24 + xs * (1 / 120 + xs * (1 / 720 + xs * (1 / 5040 + xs * (1 / 40320))))))))
    return jnp.where(small, poly, jnp.exp(x) - 1.0)


_GELU_K = math.sqrt(2.0 / math.pi)


def _gelu_and_grad(x):
    inner = _GELU_K * (x + 0.044715 * x * x * x)
    t = jnp.tanh(inner)
    g = 0.5 * x * (1.0 + t)
    dg = 0.5 * (1.0 + t) + 0.5 * x * (1.0 - t * t) * _GELU_K * (1.0 + 3 * 0.044715 * x * x)
    return g, dg


def _dot(a, b, dims):
    return lax.dot_general(a.astype(BF), b.astype(BF), (dims, ((), ())), preferred_element_type=F32)


def _dot_nn(a, b):
    return _dot(a, b, ((1,), (0,)))


def _dot_nt(a, b):
    return _dot(a, b, ((1,), (1,)))


def _dot_tn(a, b):
    return _dot(a, b, ((0,), (0,)))


def _colsum(x):
    return jnp.sum(x, axis=0, keepdims=True)


_TM = (1024, 1408, 512, 256, 128)
_TN = (1408, 1024, 512, 384, 256, 128)
_TK = (2432, 1408, 1024, 512, 256, 128)


def matmul(a, b, mode, *, out_dtype=None, alpha=1.0, res=None, name):
    out_dtype = BF if out_dtype is None else out_dtype
    if mode == "nn":
        (M, K), N = a.shape, b.shape[1]
    elif mode == "nt":
        (M, K), N = a.shape, b.shape[0]
    else:
        (K, M), N = a.shape, b.shape[1]
    tm, tn, tk = _pick(M, _TM), _pick(N, _TN), _pick(K, _TK)
    nk = K // tk
    dims = {"nn": ((1,), (0,)), "nt": ((1,), (1,)), "tn": ((0,), (0,))}[mode]
    a_spec = pl.BlockSpec((tk, tm), lambda i, j, k: (k, i)) if mode == "tn" else pl.BlockSpec((tm, tk), lambda i, j, k: (i, k))
    b_spec = pl.BlockSpec((tn, tk), lambda i, j, k: (j, k)) if mode == "nt" else pl.BlockSpec((tk, tn), lambda i, j, k: (k, j))
    o_spec = pl.BlockSpec((tm, tn), lambda i, j, k: (i, j))
    has_res = res is not None

    def body(*refs):
        a_ref, b_ref = refs[0], refs[1]
        r_ref = refs[2] if has_res else None
        o_ref, acc = refs[2 + has_res], refs[3 + has_res]
        k = pl.program_id(2)

        @pl.when(k == 0)
        def _():
            acc[...] = jnp.zeros_like(acc)

        acc[...] += _dot(a_ref[...], b_ref[...], dims)

        @pl.when(k == nk - 1)
        def _():
            r = acc[...]
            if alpha != 1.0:
                r = r * alpha
            if has_res:
                r = r + r_ref[...].astype(F32)
            o_ref[...] = r.astype(o_ref.dtype)

    ins = [a, b] + ([res] if has_res else [])
    specs = [a_spec, b_spec] + ([o_spec] if has_res else [])
    return pl.pallas_call(
        body, name=name, grid=(M // tm, N // tn, nk), in_specs=specs, out_specs=o_spec,
        out_shape=jax.ShapeDtypeStruct((M, N), out_dtype), scratch_shapes=[pltpu.VMEM((tm, tn), F32)],
        compiler_params=_cparams(3))(*ins)


class Rows:
    def __init__(self, arr, w=None, cb=0, halo=None, h=8):
        self.arr, self.w, self.cb, self.halo, self.h = arr, (arr.shape[1] if w is None else w), cb, halo, h


def rowwise(fn, rows, consts=(), outs=(), accs=(), scratch=(), *, ts, name, reverse=False):
    rows = [r if isinstance(r, Rows) else Rows(r) for r in rows]
    S = rows[0].arr.shape[0]
    n = S // ts
    tile = (lambda s: n - 1 - s) if reverse else (lambda s: s)

    def row_spec(r):
        if r.halo is None:
            return pl.BlockSpec((ts, r.w), lambda s: (tile(s), r.cb))
        per, last = ts // r.h, S // r.h - 1
        if r.halo == "prev":
            return pl.BlockSpec((r.h, r.w), lambda s: (jnp.maximum(tile(s) * per - 1, 0), r.cb))
        return pl.BlockSpec((r.h, r.w), lambda s: (jnp.minimum((tile(s) + 1) * per, last), r.cb))

    def whole(shape):
        nd = len(shape)
        return pl.BlockSpec(tuple(shape), lambda s: (0,) * nd)

    nr, nc, no, na = len(rows), len(consts), len(outs), len(accs)

    def body(*refs):
        in_refs, out_refs = refs[:nr + nc], refs[nr + nc:nr + nc + no]
        acc_refs, scr = refs[nr + nc + no:nr + nc + no + na], refs[nr + nc + no + na:]
        step = pl.program_id(0)
        o, inc = fn(step, tile(step), n, *[r[...] for r in in_refs], *scr)
        for ref, val in zip(out_refs, o):
            ref[...] = val.astype(ref.dtype)
        if na:
            @pl.when(step == 0)
            def _():
                for ref in acc_refs:
                    ref[...] = jnp.zeros_like(ref)
            for ref, val in zip(acc_refs, inc):
                ref[...] += val.astype(ref.dtype)

    out_shape = [jax.ShapeDtypeStruct((S, w), dt) for w, dt in outs] + [jax.ShapeDtypeStruct(tuple(s), dt) for s, dt in accs]
    out_specs = [pl.BlockSpec((ts, w), lambda s: (tile(s), 0)) for w, _ in outs] + [whole(s) for s, _ in accs]
    res = pl.pallas_call(
        body, name=name, grid=(n,),
        in_specs=[row_spec(r) for r in rows] + [whole(c.shape) for c in consts],
        out_specs=out_specs, out_shape=out_shape, scratch_shapes=list(scratch),
        compiler_params=_cparams(1))(*[r.arr for r in rows], *consts)
    return res[:no], res[no:]


def _rstd(x):
    return lax.rsqrt(jnp.mean(x * x, axis=-1, keepdims=True) + EPS)


def rms_fwd(x, g, *, name, ts=512):
    def fn(step, i, n, x, g):
        return ((x * _rstd(x)) * g,), ()
    return rowwise(fn, [x], [g], [(D, BF)], ts=min(ts, x.shape[0]), name=name)[0][0]


def rms_bwd(x, g, dh, dres, *, name, ts=512):
    has_res = dres is not None

    def fn(step, i, n, x, dh, *rest):
        g = rest[-1]
        r = _rstd(x)
        xhat = x * r
        dh = dh.astype(F32)
        gd = dh * g
        dx = r * (gd - xhat * jnp.mean(xhat * gd, axis=-1, keepdims=True))
        if has_res:
            dx = dx + rest[0]
        return (dx,), (_colsum(dh * xhat),)

    rows = [x, dh] + ([dres] if has_res else [])
    o, a = rowwise(fn, rows, [g], [(D, F32)], [((1, D), F32)], ts=min(ts, x.shape[0]), name=name)
    return o[0], a[0]


def ffn_fwd(x, g, w_in, w_out, tag):
    h = rms_fwd(x, g, name=tag + "_rms")
    ab = matmul(h, w_in, "nn", name=tag + "_in")

    def act_fn(step, i, n, ab):
        a, b = ab[:, :DFF].astype(F32), ab[:, DFF:].astype(F32)
        return ((a * _sigmoid(a)) * b,), ()

    act = rowwise(act_fn, [ab], outs=[(DFF, BF)], ts=256, name=tag + "_act")[0][0]
    y = matmul(act, w_out, "nn", out_dtype=F32, alpha=0.5, res=x, name=tag + "_out")
    return y, (x, h, ab, act)


def ffn_bwd(dy, saved, g, w_in, w_out, tag):
    x, h, ab, act = saved
    dact = matmul(dy, w_out, "nt", alpha=0.5, name=tag + "_dact")
    dw_out = matmul(act, dy, "tn", alpha=0.5, name=tag + "_dwout")

    def dab_fn(step, i, n, ab, dact):
        a, b, dact = ab[:, :DFF].astype(F32), ab[:, DFF:].astype(F32), dact.astype(F32)
        s = _sigmoid(a)
        da = dact * b * (s * (1.0 + a * (1.0 - s)))
        db = dact * (a * s)
        return (jnp.concatenate([da, db], axis=1),), ()

    dab = rowwise(dab_fn, [ab, dact], outs=[(2 * DFF, BF)], ts=256, name=tag + "_dab")[0][0]
    dw_in = matmul(h, dab, "tn", name=tag + "_dwin")
    dh = matmul(dab, w_in, "nt", name=tag + "_dh")
    dx, dg = rms_bwd(x, g, dh, dy, name=tag + "_drms")
    return dx, dg, dw_in, dw_out


POOL_TS = 256
POOL_HALO = 16


def pool_fwd(P, w_pool, pool_scale, tag):
    ts = POOL_TS

    def fn(step, i, n, xa, halo, w, scale):
        xa = xa.astype(F32)
        halo = halo.astype(F32) * jnp.where(i > 0, 1.0, 0.0)
        ext = jnp.concatenate([halo, xa], axis=0)
        pos = (i * ts + 1 + _iota((ts, LANES), 0)).astype(F32)
        ds, ys = [], []
        for gi, win in enumerate(POOL_WINDOWS):
            e = ext[:, gi * LANES:(gi + 1) * LANES]
            sh = 1
            while sh < win:
                e = e + pltpu.roll(e, sh, 0)
                sh *= 2
            mean = e[POOL_HALO:] / jnp.minimum(pos, float(win))
            d = mean - xa[:, gi * LANES:(gi + 1) * LANES]
            ds.append(d)
            ys.append(_dot_nn(d, w[gi]))
        d = jnp.concatenate(ds, axis=1)
        return (d, jnp.concatenate(ys, axis=1) * scale), ()

    xa = Rows(P, POOL_W, PK_XA // POOL_W)
    xa_prev = Rows(P, POOL_W, PK_XA // POOL_W, "prev", POOL_HALO)
    (d, ya0), _ = rowwise(fn, [xa, xa_prev], [w_pool, pool_scale], [(POOL_W, BF), (POOL_W, BF)], ts=ts, name=tag + "_pool")
    return d, ya0


def pool_bwd(dya0, d, w_pool, pool_scale, tag):
    ts = POOL_TS
    L = ts + POOL_HALO

    def fn(step, i, n, dya0, dya0_next, d, w, scale):
        d = d.astype(F32)
        dz = jnp.concatenate([dya0.astype(F32), dya0_next.astype(F32) * jnp.where(i < n - 1, 1.0, 0.0)], axis=0) * scale
        pos = (i * ts + 1 + _iota((L, LANES), 0)).astype(F32)
        dxa, dws, dsc = [], [], []
        for gi, win in enumerate(POOL_WINDOWS):
            sl = slice(gi * LANES, (gi + 1) * LANES)
            dzg = dz[:, sl]
            dd = _dot_nt(dzg, w[gi])
            e = dd / jnp.minimum(pos, float(win))
            sh = 1
            while sh < win:
                e = e + pltpu.roll(e, L - sh, 0)
                sh *= 2
            dxa.append(e[:ts] - dd[:ts])
            dws.append(_dot_tn(d[:, sl], dzg[:ts])[None])
            z = _dot_nn(d[:, sl], w[gi])
            dsc.append(_colsum(dya0[:, sl].astype(F32) * z))
        return (jnp.concatenate(dxa, axis=1),), (jnp.concatenate(dws, axis=0), jnp.concatenate(dsc, axis=1))

    (dxa,), (dw_pool, dscale) = rowwise(
        fn, [dya0, Rows(dya0, halo="next", h=POOL_HALO), d], [w_pool, pool_scale],
        [(POOL_W, BF)], [((4, LANES, LANES), F32), ((1, POOL_W), F32)], ts=ts, name=tag + "_dpool")
    return dxa, dw_pool, dscale


LRU_R = 256


def _scan_fwd(A, U):
    R = A.shape[0]
    row = _iota(A.shape, 0)
    d = 1
    while d < R:
        m = row >= d
        A_sh = jnp.where(m, pltpu.roll(A, d, 0), 1.0)
        U_sh = jnp.where(m, pltpu.roll(U, d, 0), 0.0)
        U = A * U_sh + U
        A = A * A_sh
        d *= 2
    return A, U


def _scan_bwd(B, X):
    R = B.shape[0]
    row = _iota(B.shape, 0)
    d = 1
    while d < R:
        m = row < R - d
        B_sh = jnp.where(m, pltpu.roll(B, R - d, 0), 1.0)
        X_sh = jnp.where(m, pltpu.roll(X, R - d, 0), 0.0)
        X = X + B * X_sh
        B = B * B_sh
        d *= 2
    return X


def _lru_gates(xc, wa, ba, wx, bx, lam):
    r = _sigmoid(_dot_nn(xc, wa) + ba)
    ig = _sigmoid(_dot_nn(xc, wx) + bx)
    sp = _softplus(-lam)
    log_a = -LRU_C * r * sp
    a = jnp.exp(log_a)
    mult = jnp.sqrt(-_expm1(2.0 * log_a))
    return r, ig, sp, a, mult


def _lru_specs(S, R, reverse):
    nch = S // R
    ch = (lambda j: nch - 1 - j) if reverse else (lambda j: j)
    per = R // 8

    def col(off):
        return pl.BlockSpec((R, LANES), lambda h, j: (ch(j), off + h))

    def prev(off):
        return pl.BlockSpec((8, LANES), lambda h, j: (jnp.maximum(ch(j) * per - 1, 0), off + h))

    vec = pl.BlockSpec((1, LANES), lambda h, j: (0, h))
    cw = pl.BlockSpec((4, LANES), lambda h, j: (0, h))
    wsq = pl.BlockSpec((None, LANES, LANES), lambda h, j: (h, 0, 0))
    return nch, ch, col, prev, vec, cw, wsq


def lru_fwd(P, conv_w, conv_b, w_a, b_a, w_x, b_x, lam, tag):
    S = P.shape[0]
    R = min(LRU_R, S)
    nch, ch, col, prev, vec, cw_spec, wsq = _lru_specs(S, R, False)

    def body(xb_ref, halo_ref, gb_ref, cw_ref, cb_ref, wa_ref, ba_ref, wx_ref, bx_ref, lam_ref, xc_ref, h_ref, yb_ref, carry):
        j = pl.program_id(1)
        xb = xb_ref[...].astype(F32)
        halo = halo_ref[...].astype(F32) * jnp.where(j > 0, 1.0, 0.0)
        ext = jnp.concatenate([halo, xb], axis=0)
        cw = cw_ref[...]
        xc = cb_ref[...]
        for k in range(4):
            e = ext if k == 3 else pltpu.roll(ext, 3 - k, 0)
            xc = xc + e[8:] * cw[k:k + 1]
        r, ig, sp, a, mult = _lru_gates(xc, wa_ref[...], ba_ref[...], wx_ref[...], bx_ref[...], lam_ref[...])
        u = mult * (ig * xc)
        cum_a, hloc = _scan_fwd(a, u)

        @pl.when(j == 0)
        def _():
            carry[...] = jnp.zeros_like(carry)

        hfull = hloc + cum_a * carry[0:1]
        carry[...] = jnp.broadcast_to(hfull[R - 1:R], carry.shape)
        gel, _ = _gelu_and_grad(gb_ref[...].astype(F32))
        xc_ref[...] = xc
        h_ref[...] = hfull
        yb_ref[...] = (hfull * gel).astype(yb_ref.dtype)

    out = pl.BlockSpec((R, LANES), lambda h, j: (j, h))
    return pl.pallas_call(
        body, name=tag + "_lru", grid=(LRU_HEADS, nch),
        in_specs=[col(PK_XB // LANES), prev(PK_XB // LANES), col(PK_GB // LANES), cw_spec, vec, wsq, vec, wsq, vec, vec],
        out_specs=[out, out, out],
        out_shape=[jax.ShapeDtypeStruct((S, D), F32), jax.ShapeDtypeStruct((S, D), F32), jax.ShapeDtypeStruct((S, D), BF)],
        scratch_shapes=[pltpu.VMEM((8, LANES), F32)], compiler_params=_cparams(2),
    )(P, P, P, conv_w, conv_b, w_a, b_a, w_x, b_x, lam)


def lru_bwd(dyb0, P, hh, xc, conv_w, w_a, b_a, w_x, b_x, lam, tag):
    S = P.shape[0]
    R = min(LRU_R, S)
    nch, ch, col, prev, vec, cw_spec, wsq = _lru_specs(S, R, True)

    def body(dyb_ref, gb_ref, h_ref, hprev_ref, xc_ref, xb_ref, xbprev_ref, cw_ref, wa_ref, ba_ref, wx_ref, bx_ref, lam_ref,
             dgb_ref, dxb_ref, dwa_ref, dba_ref, dwx_ref, dbx_ref, dlam_ref, dcw_ref, dcb_ref, gcarry, dxc_head):
        j = pl.program_id(1)
        jj = nch - 1 - j
        has_prev = jnp.where(jj > 0, 1.0, 0.0)
        row = _iota((R, LANES), 0)
        xc, hv, lam = xc_ref[...], h_ref[...], lam_ref[...]
        wa, wx = wa_ref[...], wx_ref[...]
        r, ig, sp, a, mult = _lru_gates(xc, wa, ba_ref[...], wx, bx_ref[...], lam)
        dyb = dyb_ref[...].astype(F32)
        gel, dgel = _gelu_and_grad(gb_ref[...].astype(F32))
        dgb_ref[...] = (dyb * hv * dgel).astype(dgb_ref.dtype)

        @pl.when(j == 0)
        def _():
            gcarry[...] = jnp.zeros_like(gcarry)
            dxc_head[...] = jnp.zeros_like(dxc_head)
            for ref in (dwa_ref, dba_ref, dwx_ref, dbx_ref, dlam_ref, dcw_ref, dcb_ref):
                ref[...] = jnp.zeros_like(ref)

        B = jnp.where(row < R - 1, pltpu.roll(a, R - 1, 0), 0.0)
        X = dyb * gel + jnp.where(row == R - 1, gcarry[0:1], 0.0)
        G = _scan_bwd(B, X)
        gcarry[...] = jnp.broadcast_to(a[0:1] * G[0:1], gcarry.shape)
        hprev = jnp.where(row == 0, hprev_ref[...][7:8] * has_prev, pltpu.roll(hv, 1, 0))
        da = G * hprev
        dmult = G * ig * xc
        dig = G * mult * xc
        dxc = G * mult * ig
        dlog_a = da * a - dmult * (a * a) / mult
        dzr = dlog_a * (-LRU_C * sp) * (r * (1.0 - r))
        dzi = dig * (ig * (1.0 - ig))
        dxc = dxc + _dot_nt(dzr, wa) + _dot_nt(dzi, wx)
        dwa_ref[...] += _dot_tn(xc, dzr)
        dwx_ref[...] += _dot_tn(xc, dzi)
        dba_ref[...] += _colsum(dzr)
        dbx_ref[...] += _colsum(dzi)
        dlam_ref[...] += _colsum(dlog_a * (-LRU_C * r)) * (-_sigmoid(-lam))
        dcb_ref[...] += _colsum(dxc)
        cw = cw_ref[...]
        ext = jnp.concatenate([dxc, dxc_head[...]], axis=0)
        dxb = dxc * cw[3:4]
        for k in range(3):
            dxb = dxb + pltpu.roll(ext, R + 8 - (3 - k), 0)[:R] * cw[k:k + 1]
        dxb_ref[...] = dxb.astype(dxb_ref.dtype)
        dxc_head[...] = dxc[0:8]
        extx = jnp.concatenate([xbprev_ref[...].astype(F32) * has_prev, xb_ref[...].astype(F32)], axis=0)
        incs = []
        for k in range(4):
            e = extx if k == 3 else pltpu.roll(extx, 3 - k, 0)
            incs.append(_colsum(dxc * e[8:]))
        dcw_ref[...] += jnp.concatenate(incs, axis=0)

    plain = pl.BlockSpec((R, LANES), lambda h, j: (ch(j), h))
    plain_prev = pl.BlockSpec((8, LANES), lambda h, j: (jnp.maximum(ch(j) * (R // 8) - 1, 0), h))
    return pl.pallas_call(
        body, name=tag + "_dlru", grid=(LRU_HEADS, nch),
        in_specs=[plain, col(PK_GB // LANES), plain, plain_prev, plain, col(PK_XB // LANES), prev(PK_XB // LANES),
                  cw_spec, wsq, vec, wsq, vec, vec],
        out_specs=[plain, plain, wsq, vec, wsq, vec, vec, cw_spec, vec],
        out_shape=[jax.ShapeDtypeStruct((S, D), BF), jax.ShapeDtypeStruct((S, D), BF),
                   jax.ShapeDtypeStruct((LRU_HEADS, LANES, LANES), F32), jax.ShapeDtypeStruct((1, D), F32),
                   jax.ShapeDtypeStruct((LRU_HEADS, LANES, LANES), F32), jax.ShapeDtypeStruct((1, D), F32),
                   jax.ShapeDtypeStruct((1, D), F32), jax.ShapeDtypeStruct((4, D), F32), jax.ShapeDtypeStruct((1, D), F32)],
        scratch_shapes=[pltpu.VMEM((8, LANES), F32), pltpu.VMEM((8, LANES), F32)], compiler_params=_cparams(2),
    )(dyb0, P, hh, hh, xc, P, P, conv_w, w_a, b_a, w_x, b_x, lam)


CUM_TS = 512
FLASH_T = 512
FLASH_G = 2
NEG = -1e30


def cumlogf_fwd(fl, b_f, tag):
    ts = min(CUM_TS, fl.shape[0])

    def fn(step, i, n, fl, bf, carry):
        x = -_softplus(-(fl + bf))
        row = _iota(x.shape, 0)
        d = 1
        while d < ts:
            x = x + jnp.where(row >= d, pltpu.roll(x, d, 0), 0.0)
            d *= 2

        @pl.when(step == 0)
        def _():
            carry[...] = jnp.zeros_like(carry)

        c = x + carry[0:1]
        carry[...] = jnp.broadcast_to(c[ts - 1:ts], carry.shape)
        return (c,), ()

    return rowwise(fn, [fl], [b_f], [(LANES, F32)], scratch=[pltpu.VMEM((8, LANES), F32)], ts=ts, name=tag + "_cum")[0][0]


def cumlogf_bwd(dc, fl, b_f, tag):
    ts = min(CUM_TS, fl.shape[0])

    def fn(step, i, n, dc, fl, bf, carry):
        row = _iota(dc.shape, 0)
        x = dc
        d = 1
        while d < ts:
            x = x + jnp.where(row < ts - d, pltpu.roll(x, ts - d, 0), 0.0)
            d *= 2

        @pl.when(step == 0)
        def _():
            carry[...] = jnp.zeros_like(carry)

        g = x + carry[0:1]
        carry[...] = jnp.broadcast_to(g[0:1], carry.shape)
        dfl = g * _sigmoid(-(fl + bf))
        return (dfl,), (_colsum(dfl),)

    (dfl,), (db,) = rowwise(fn, [dc, fl], [b_f], [(LANES, F32)], [((1, LANES), F32)], scratch=[pltpu.VMEM((8, LANES), F32)],
                            ts=ts, name=tag + "_dcum", reverse=True)
    return dfl, db


FOX_KA = 80


def _split3(c):
    c = c.astype(F32)
    hi = lax.reduce_precision(c, 8, 7)
    r = c - hi
    mid = lax.reduce_precision(r, 8, 7)
    return hi.astype(BF), mid.astype(BF), (r - mid).astype(BF)


def fox_operands(q, k, ch):
    H, S, hd = q.shape
    qs = q * (FOX_HD ** -0.5)
    pieces = [p.astype(F32) for p in _split3(ch)]
    pad = FOX_KA - hd

    def unit(i):
        return (jnp.arange(FOX_KA) == hd + i).astype(F32)

    qa = jnp.pad(qs.transpose(0, 2, 1).astype(F32), ((0, 0), (0, pad), (0, 0)))
    ka = jnp.pad(k.astype(F32), ((0, 0), (0, 0), (0, pad)))
    for i, p in enumerate(pieces):
        qa = qa + p[:, None, :] * unit(i)[None, :, None] + unit(3 + i)[None, :, None]
        ka = ka + unit(i)[None, None, :] - p[:, :, None] * unit(3 + i)[None, None, :]
    return qa.astype(BF), ka.astype(BF), qs


def _causal(s):
    return jnp.where(_iota(s.shape, 0) <= _iota(s.shape, 1), s, NEG)


def flash_fwd(qa, ka, vt, tag):
    H, hd, S = vt.shape
    T = min(FLASH_T, S)
    nb = S // T

    pairs = [(qi, ki) for qi in range(nb) for ki in range(qi + 1)]
    q_tab = jnp.asarray([p[0] for p in pairs], jnp.int32)
    k_tab = jnp.asarray([p[1] for p in pairs], jnp.int32)

    def body(q_tab_ref, k_tab_ref, ka_ref, qa_ref, vt_ref, o_ref, lse_ref, m_s, l_s, acc):
        qi, ki = q_tab_ref[pl.program_id(1)], k_tab_ref[pl.program_id(1)]

        @pl.when(ki == 0)
        def _():
            m_s[...] = jnp.full_like(m_s, NEG)
            l_s[...] = jnp.zeros_like(l_s)
            acc[...] = jnp.zeros_like(acc)

        def step(diagonal):
            for g in range(FLASH_G):
                s = _dot_nn(ka_ref[g], qa_ref[g])
                if diagonal:
                    s = _causal(s)
                m_new = jnp.maximum(m_s[g], jnp.max(s, axis=0, keepdims=True))
                alpha = jnp.exp(m_s[g] - m_new)
                p = jnp.exp(s - m_new)
                l_s[g] = alpha * l_s[g] + jnp.sum(p, axis=0, keepdims=True)
                p_hi = p.astype(BF)
                p_lo = p - p_hi.astype(F32)
                v = vt_ref[g]
                acc[g] = alpha * acc[g] + (_dot_nn(v, p_hi) + _dot_nn(v, p_lo))
                m_s[g] = m_new

        @pl.when(ki < qi)
        def _():
            step(False)

        @pl.when(ki == qi)
        def _():
            step(True)
            o_ref[...] = acc[...] / l_s[...]
            lse_ref[...] = m_s[...] + jnp.log(l_s[...])

    G = FLASH_G
    grid_spec = pltpu.PrefetchScalarGridSpec(
        num_scalar_prefetch=2, grid=(H // G, len(pairs)),
        in_specs=[pl.BlockSpec((G, T, FOX_KA), lambda h, p, qt, kt: (h, kt[p], 0)),
                  pl.BlockSpec((G, FOX_KA, T), lambda h, p, qt, kt: (h, 0, qt[p])),
                  pl.BlockSpec((G, hd, T), lambda h, p, qt, kt: (h, 0, kt[p]))],
        out_specs=[pl.BlockSpec((G, hd, T), lambda h, p, qt, kt: (h, 0, qt[p])),
                   pl.BlockSpec((G, 1, T), lambda h, p, qt, kt: (h, 0, qt[p]))],
        scratch_shapes=[pltpu.VMEM((G, 1, T), F32), pltpu.VMEM((G, 1, T), F32), pltpu.VMEM((G, hd, T), F32)])
    return pl.pallas_call(
        body, name=tag + "_flash", grid_spec=grid_spec,
        out_shape=[jax.ShapeDtypeStruct((H, hd, S), F32), jax.ShapeDtypeStruct((H, 1, S), F32)],
        compiler_params=_cparams(2))(q_tab, k_tab, ka, qa, vt)


def flash_bwd(qa, ka, qs, kt, v, ot, do, dot_, lse, tag):
    H, S, hd = v.shape
    T = min(FLASH_T, S)
    nb = S // T

    pairs = [(qi, ki) for ki in range(nb) for qi in range(ki, nb)]
    q_tab = jnp.asarray([p[0] for p in pairs], jnp.int32)
    k_tab = jnp.asarray([p[1] for p in pairs], jnp.int32)

    def body(q_tab_ref, k_tab_ref, ka_ref, qa_ref, qs_ref, kt_ref, v_ref, ot_ref, do_ref, dot_ref, lse_ref,
             dqt_ref, dk_ref, dv_ref, dc_ref, dk_acc, dv_acc, dc_acc):
        qi, ki = q_tab_ref[pl.program_id(1)], k_tab_ref[pl.program_id(1)]

        @pl.when(pl.program_id(1) == 0)
        def _():
            dqt_ref[...] = jnp.zeros_like(dqt_ref)

        @pl.when(qi == ki)
        def _():
            dk_acc[...] = jnp.zeros_like(dk_acc)
            dv_acc[...] = jnp.zeros_like(dv_acc)
            dc_acc[...] = jnp.zeros_like(dc_acc)

        def step(diagonal):
            for g in range(FLASH_G):
                s = _dot_nn(ka_ref[g], qa_ref[g])
                if diagonal:
                    s = _causal(s)
                p = jnp.exp(s - lse_ref[g])
                dot_v = dot_ref[g]
                dp = _dot_nn(v_ref[g], dot_v)
                delta = jnp.sum(dot_v.astype(F32) * ot_ref[g], axis=0, keepdims=True)
                ds = p * (dp - delta)
                part = ds[:, 0:LANES]
                for j in range(1, T // LANES):
                    part = part + ds[:, j * LANES:(j + 1) * LANES]
                dc_acc[g] += part
                dsb = ds.astype(BF)
                dv_acc[g] += _dot_nn(p, do_ref[g])
                dk_acc[g] += _dot_nn(dsb, qs_ref[g])
                dqt_ref[g, qi] += _dot_nn(kt_ref[g], dsb)

        @pl.when(qi > ki)
        def _():
            step(False)

        @pl.when(qi == ki)
        def _():
            step(True)

        @pl.when(qi == nb - 1)
        def _():
            dk_ref[...] = dk_acc[...]
            dv_ref[...] = dv_acc[...]
            for g in range(FLASH_G):
                dc_ref[g] = -jnp.sum(dc_acc[g].T, axis=0, keepdims=True)

    qblk = lambda h, p, qt, kt_: (h, qt[p], 0)
    qblk_t = lambda h, p, qt, kt_: (h, 0, qt[p])
    kblk = lambda h, p, qt, kt_: (h, kt_[p], 0)
    kblk_t = lambda h, p, qt, kt_: (h, 0, kt_[p])
    G = FLASH_G
    grid_spec = pltpu.PrefetchScalarGridSpec(
        num_scalar_prefetch=2, grid=(H // G, len(pairs)),
        in_specs=[pl.BlockSpec((G, T, FOX_KA), kblk), pl.BlockSpec((G, FOX_KA, T), qblk_t),
                  pl.BlockSpec((G, T, hd), qblk), pl.BlockSpec((G, hd, T), kblk_t), pl.BlockSpec((G, T, hd), kblk),
                  pl.BlockSpec((G, hd, T), qblk_t), pl.BlockSpec((G, T, hd), qblk), pl.BlockSpec((G, hd, T), qblk_t),
                  pl.BlockSpec((G, 1, T), qblk_t)],
        out_specs=[pl.BlockSpec((G, nb, hd, T), lambda h, p, qt, kt_: (h, 0, 0, 0)), pl.BlockSpec((G, T, hd), kblk),
                   pl.BlockSpec((G, T, hd), kblk), pl.BlockSpec((G, 1, T), kblk_t)],
        scratch_shapes=[pltpu.VMEM((G, T, hd), F32), pltpu.VMEM((G, T, hd), F32), pltpu.VMEM((G, T, LANES), F32)])
    return pl.pallas_call(
        body, name=tag + "_dflash", grid_spec=grid_spec,
        out_shape=[jax.ShapeDtypeStruct((H, nb, hd, T), F32), jax.ShapeDtypeStruct((H, S, hd), F32),
                   jax.ShapeDtypeStruct((H, S, hd), F32), jax.ShapeDtypeStruct((H, 1, S), F32)],
        compiler_params=_cparams(2))(q_tab, k_tab, ka, qa, qs, kt, v, ot, do, dot_, lse)


def _to_heads(x2d):
    S = x2d.shape[0]
    return x2d.reshape(S, FOX_HEADS, FOX_HD).transpose(1, 0, 2)


def _from_heads(x3d):
    S = x3d.shape[1]
    return x3d.transpose(1, 0, 2).reshape(S, FOX_W)


def _gl_rows(P):
    return [Rows(P, D, PK_GL // D + k) for k in range(3)]


def merge_fwd(P, ya, yb, yc, b_gate, tag):
    def fn(step, i, n, g0, g1, g2, ya, yb, yc, b):
        out = 0.0
        for k, (gl, y) in enumerate(((g0, ya), (g1, yb), (g2, yc))):
            out = out + _sigmoid(gl.astype(F32) + b[:, k * D:(k + 1) * D]) * y.astype(F32)
        return (out,), ()
    return rowwise(fn, _gl_rows(P) + [ya, yb, yc], [b_gate], [(D, BF)], ts=512, name=tag + "_merge")[0][0]


def merge_bwd(dm, P, ya, yb, yc, b_gate, tag):
    def fn(step, i, n, dm, g0, g1, g2, ya, yb, yc, b):
        dm = dm.astype(F32)
        dys, dgls = [], []
        for k, (gl, y) in enumerate(((g0, ya), (g1, yb), (g2, yc))):
            g = _sigmoid(gl.astype(F32) + b[:, k * D:(k + 1) * D])
            dys.append(dm * g)
            dgls.append(dm * y.astype(F32) * (g * (1.0 - g)))
        dgl = jnp.concatenate(dgls, axis=1)
        return (dys[0], dys[1], dys[2], dgl), (_colsum(dgl),)
    (dya, dyb, dyc, dgl), (db,) = rowwise(
        fn, [dm] + _gl_rows(P) + [ya, yb, yc], [b_gate], [(D, BF), (D, BF), (D, BF), (3 * D, BF)], [((1, 3 * D), F32)],
        ts=256, name=tag + "_dmerge")
    return dya, dyb, dyc, dgl, db


def _xattn_probs(qh, kh):
    s = _dot_nt(qh, kh) * (X_HD ** -0.5)
    e = jnp.exp(s - jnp.max(s, axis=1, keepdims=True))
    return e / jnp.sum(e, axis=1, keepdims=True)


def xattn_fwd(q, kv, tag):
    def fn(step, i, n, q, k, v):
        os = []
        for h in range(X_HEADS):
            sl = slice(h * X_HD, (h + 1) * X_HD)
            os.append(_dot_nn(_xattn_probs(q[:, sl], k[:, sl]), v[:, sl]))
        return (jnp.concatenate(os, axis=1),), ()

    return _xattn_call(fn, [q], kv, [(D, BF)], [], tag + "_xattn")[0][0]


def _xattn_call(fn, rows, kv, outs, accs, name):
    S, ts, M = rows[0].shape[0], 512, kv.shape[0]
    ts = min(ts, S)
    nr, no, na = len(rows), len(outs), len(accs)

    def body(*refs):
        in_refs, out_refs, acc_refs = refs[:nr + 2], refs[nr + 2:nr + 2 + no], refs[nr + 2 + no:]
        step = pl.program_id(0)
        o, inc = fn(step, step, S // ts, *[r[...] for r in in_refs])
        for ref, val in zip(out_refs, o):
            ref[...] = val.astype(ref.dtype)
        if na:
            @pl.when(step == 0)
            def _():
                for ref in acc_refs:
                    ref[...] = jnp.zeros_like(ref)
            for ref, val in zip(acc_refs, inc):
                ref[...] += val

    row = pl.BlockSpec((ts, D), lambda s: (s, 0))
    res = pl.pallas_call(
        body, name=name, grid=(S // ts,),
        in_specs=[row] * nr + [pl.BlockSpec((M, D), lambda s: (0, 0)), pl.BlockSpec((M, D), lambda s: (0, 1))],
        out_specs=[row] * no + [pl.BlockSpec(tuple(s), lambda s_: (0, 0)) for s, _ in accs],
        out_shape=[jax.ShapeDtypeStruct((S, w), dt) for w, dt in outs] + [jax.ShapeDtypeStruct(tuple(s), dt) for s, dt in accs],
        compiler_params=_cparams(1))(*rows, kv, kv)
    return res[:no], res[no:]


def xattn_bwd(q, kv, do, tag):
    def fn(step, i, n, q, do, k, v):
        dqs, dks, dvs = [], [], []
        for h in range(X_HEADS):
            sl = slice(h * X_HD, (h + 1) * X_HD)
            p = _xattn_probs(q[:, sl], k[:, sl])
            dp = _dot_nt(do[:, sl], v[:, sl])
            ds = (p * (dp - jnp.sum(p * dp, axis=1, keepdims=True)) * (X_HD ** -0.5)).astype(BF)
            dqs.append(_dot_nn(ds, k[:, sl]))
            dks.append(_dot_tn(ds, q[:, sl]))
            dvs.append(_dot_tn(p, do[:, sl]))
        return (jnp.concatenate(dqs, axis=1),), (jnp.concatenate(dks + dvs, axis=1),)

    (dq,), (dkv,) = _xattn_call(fn, [q, do], kv, [(D, BF)], [((kv.shape[0], 2 * D), F32)], tag + "_dxattn")
    return dq, dkv


def layer_fwd(x, mem, w, tag):
    x1, s_ffn1 = ffn_fwd(x, w["g_ffn1"], w["ffn1_in"], w["ffn1_out"], tag + "_ffn1")
    u = rms_fwd(x1, w["g_mix"], name=tag + "_mixrms")
    P = matmul(u, w["wp"], "nn", name=tag + "_proj")
    fl = matmul(u, w["wfl"], "nn", out_dtype=F32, name=tag + "_projf")
    d, ya0 = pool_fwd(P, w["w_pool"], w["pool_scale"], tag)
    ya = matmul(ya0, w["up_a"], "nn", name=tag + "_upa")
    xc, hh, yb0 = lru_fwd(P, w["conv_w"], w["conv_b"], w["w_rg_a"], w["b_rg_a"], w["w_rg_x"], w["b_rg_x"], w["lam"], tag)
    yb = matmul(yb0, w["up_b"], "nn", name=tag + "_upb")
    c = cumlogf_fwd(fl, w["b_f"], tag)
    q, k, v = (_to_heads(P[:, o:o + FOX_W]) for o in (PK_Q, PK_K, PK_V))
    qa, ka, qs = fox_operands(q, k, c[:, :FOX_HEADS].T)
    ot, lse = flash_fwd(qa, ka, v.transpose(0, 2, 1), tag)
    o2 = ot.transpose(2, 0, 1).reshape(-1, FOX_W)
    yc = matmul(o2, w["up_c"], "nn", name=tag + "_upc")
    merged = merge_fwd(P, ya, yb, yc, w["b_gate"], tag)
    x2 = matmul(merged, w["w_o"], "nn", out_dtype=F32, res=x1, name=tag + "_wo")
    hq = rms_fwd(x2, w["g_cross"], name=tag + "_xrms")
    qx = matmul(hq, w["xq"], "nn", name=tag + "_xq")
    mn = rms_fwd(mem, w["g_mem"], name=tag + "_mrms")
    kv = matmul(mn, w["xkv"], "nn", name=tag + "_xkv")
    ox = xattn_fwd(qx, kv, tag)
    x3 = matmul(ox, w["xo"], "nn", out_dtype=F32, res=x2, name=tag + "_xo")
    x4, s_ffn2 = ffn_fwd(x3, w["g_ffn2"], w["ffn2_in"], w["ffn2_out"], tag + "_ffn2")
    saved = dict(ffn1=s_ffn1, ffn2=s_ffn2, x1=x1, u=u, P=P, fl=fl, d=d, ya0=ya0, ya=ya, xc=xc, hh=hh, yb0=yb0, yb=yb,
                 qa=qa, ka=ka, qs=qs, k=k, v=v, ot=ot, lse=lse, o2=o2, yc=yc, merged=merged, x2=x2, hq=hq, qx=qx,
                 mn=mn, kv=kv, ox=ox)
    return x4, saved


def layer_bwd(dx4, mem, w, s, tag):
    g = {}
    dx3, g["g_ffn2"], g["w_ffn2_in"], g["w_ffn2_out"] = ffn_bwd(dx4, s["ffn2"], w["g_ffn2"], w["ffn2_in"], w["ffn2_out"], tag + "_ffn2")
    dox = matmul(dx3, w["xo"], "nt", name=tag + "_dox")
    g["w_xo"] = matmul(s["ox"], dx3, "tn", name=tag + "_dwxo")
    dqx, dkv = xattn_bwd(s["qx"], s["kv"], dox, tag)
    g["w_xq"] = matmul(s["hq"], dqx, "tn", name=tag + "_dwxq")
    dhq = matmul(dqx, w["xq"], "nt", name=tag + "_dhq")
    dx2, g["g_cross"] = rms_bwd(s["x2"], w["g_cross"], dhq, dx3, name=tag + "_dxrms")
    g["w_xkv"] = matmul(s["mn"], dkv, "tn", name=tag + "_dwxkv")
    dmn = matmul(dkv, w["xkv"], "nt", name=tag + "_dmn")
    _, g["g_mem"] = rms_bwd(mem, w["g_mem"], dmn, None, name=tag + "_dmrms")
    P = s["P"]
    dmerged = matmul(dx2, w["w_o"], "nt", name=tag + "_dmerged")
    g["w_o"] = matmul(s["merged"], dx2, "tn", name=tag + "_dwo")
    dya, dyb, dyc, dgl, g["b_gate"] = merge_bwd(dmerged, P, s["ya"], s["yb"], s["yc"], w["b_gate"], tag)
    dya0 = matmul(dya, w["up_a"], "nt", name=tag + "_dya0")
    g["w_up_a"] = matmul(s["ya0"], dya, "tn", name=tag + "_dwupa")
    dxa, g["w_pool"], g["pool_scale"] = pool_bwd(dya0, s["d"], w["w_pool"], w["pool_scale"], tag)
    dyb0 = matmul(dyb, w["up_b"], "nt", name=tag + "_dyb0")
    g["w_up_b"] = matmul(s["yb0"], dyb, "tn", name=tag + "_dwupb")
    (dgb, dxb, g["w_rg_a"], g["b_rg_a"], g["w_rg_x"], g["b_rg_x"], g["lru_lambda"], g["conv_w"], g["conv_b"]) = lru_bwd(
        dyb0, P, s["hh"], s["xc"], w["conv_w"], w["w_rg_a"], w["b_rg_a"], w["w_rg_x"], w["b_rg_x"], w["lam"], tag)
    do2 = matmul(dyc, w["up_c"], "nt", name=tag + "_do2")
    g["w_up_c"] = matmul(s["o2"], dyc, "tn", name=tag + "_dwupc")
    do = _to_heads(do2)
    dqt, dk, dv, dck = flash_bwd(s["qa"], s["ka"], s["qs"], s["k"].transpose(0, 2, 1), s["v"], s["ot"], do,
                                 do.transpose(0, 2, 1), s["lse"], tag)
    dq = (dqt * (FOX_HD ** -0.5)).transpose(0, 1, 3, 2).reshape(dk.shape)
    dc = jnp.pad(dck[:, 0, :].T, ((0, 0), (0, LANES - FOX_HEADS)))
    dfl, g["b_f"] = cumlogf_bwd(dc, s["fl"], w["b_f"], tag)
    dP = jnp.concatenate([dxb, dgb, dgl, dxa] + [_from_heads(t).astype(BF) for t in (dq, dk, dv)], axis=1)
    du = matmul(dP, w["wp"], "nt", out_dtype=F32, name=tag + "_du")
    du = matmul(dfl, w["wfl"], "nt", out_dtype=F32, res=du, name=tag + "_duf")
    g["wp"] = matmul(s["u"], dP, "tn", name=tag + "_dwp")
    g["wfl"] = matmul(s["u"], dfl, "tn", out_dtype=F32, name=tag + "_dwfl")
    dx1, g["g_mix"] = rms_bwd(s["x1"], w["g_mix"], du, dx2, name=tag + "_dmixrms")
    dx0, g["g_ffn1"], g["w_ffn1_in"], g["w_ffn1_out"] = ffn_bwd(dx1, s["ffn1"], w["g_ffn1"], w["ffn1_in"], w["ffn1_out"], tag + "_ffn1")
    return dx0, g


def loss_head(x, target, g_final):
    def fn(step, i, n, x, t, g):
        r = _rstd(x)
        xhat = x * r
        e = xhat * g - t
        dy = e * (1.0 / D)
        gd = dy * g
        dx = r * (gd - xhat * jnp.mean(xhat * gd, axis=-1, keepdims=True))
        loss = 0.5 * jnp.sum(jnp.mean(e * e, axis=-1, keepdims=True), axis=0, keepdims=True)
        return (dx,), (jnp.broadcast_to(loss, (1, LANES)), _colsum(dy * xhat))

    (dx,), (loss, dg) = rowwise(fn, [x, target], [g_final], [(D, F32)], [((1, LANES), F32), ((1, D), F32)], ts=512, name="loss_head")
    return loss[0, 0], dx, dg


BIG = (("w_ffn1_in", (D, 2 * DFF), 1), ("w_ffn1_out", (DFF, D), 0), ("w_in", (D, 7176), 1), ("w_up_a", (POOL_W, D), 1),
       ("conv_w", (4, D), 1), ("w_up_b", (D, D), 0), ("w_up_c", (FOX_W, D), 1), ("w_o", (D, D), 0), ("w_xq", (D, D), 0),
       ("w_xkv", (D, 2 * D), 1), ("w_xo", (D, D), 0), ("w_ffn2_in", (D, 2 * DFF), 1), ("w_ffn2_out", (DFF, D), 0))
SMALL = (("g_ffn1", (D,)), ("g_mix", (D,)), ("b_f", (FOX_HEADS,)), ("b_gate", (3 * D,)), ("w_pool", (4, LANES, LANES)),
         ("pool_scale", (POOL_W,)), ("conv_b", (D,)), ("w_rg_a", (LRU_HEADS, LANES, LANES)), ("b_rg_a", (D,)),
         ("w_rg_x", (LRU_HEADS, LANES, LANES)), ("b_rg_x", (D,)), ("lru_lambda", (D,)), ("g_cross", (D,)), ("g_mem", (D,)),
         ("g_ffn2", (D,)))
ORDER = ("g_ffn1", "w_ffn1_in", "w_ffn1_out", "g_mix", "w_in", "b_f", "b_gate", "w_pool", "pool_scale", "w_up_a", "conv_w",
         "conv_b", "w_rg_a", "b_rg_a", "w_rg_x", "b_rg_x", "lru_lambda", "w_up_b", "w_up_c", "w_o", "g_cross", "g_mem", "w_xq",
         "w_xkv", "w_xo", "g_ffn2", "w_ffn2_in", "w_ffn2_out", "g_final")

IN_SPLIT = (("xa", 0, 512), ("xb", 512, 1024), ("gb", 1536, 1024), ("q", 2560, 512), ("k", 3072, 512), ("v", 3584, 512),
            ("fl", 4096, 8), ("gl", 4104, 3072))
PACK_ORDER = ("xb", "gb", "gl", "xa", "q", "k", "v")


def _shard_shape(shape, axis):
    s = list(shape)
    s[axis] //= N_CHIPS
    return (DEPTH, *s)


def _round_up(n, m):
    return -(-n // m) * m


SMALL_SEG_ROWS = 8


def _small_rows(n_elements):
    return _round_up(-(-n_elements // D), SMALL_SEG_ROWS)


def _pack_small(tree, names, row_multiple=SMALL_SEG_ROWS, fill=0.0):
    parts = []
    for n in names:
        flat = tree[n].astype(F32).reshape(-1)
        rows = _small_rows(flat.shape[0])
        parts.append(jnp.pad(flat, (0, rows * D - flat.shape[0]), constant_values=fill).reshape(rows, D))
    total = sum(p.shape[0] for p in parts)
    if total % row_multiple:
        parts.append(jnp.full((_round_up(total, row_multiple) - total, D), fill, F32))
    return jnp.concatenate(parts, axis=0)


def _unpack_small(buf, like, names):
    out, off = {}, 0
    for n in names:
        sh = like[n].shape
        rows = _small_rows(math.prod(sh))
        out[n] = buf[off:off + rows].reshape(-1)[:math.prod(sh)].reshape(sh)
        off += rows
    return out


def _from_chip_major(g, shape, axis):
    return jnp.moveaxis(g, 1, axis + 1).reshape(DEPTH, *shape)


def _to_chip_major(full, shape, axis):
    sh = list(shape)
    sh[axis:axis + 1] = [N_CHIPS, shape[axis] // N_CHIPS]
    return jnp.moveaxis(full.reshape(DEPTH, *sh), axis + 1, 1)


def _layer_weights(full, small, l):
    w_in = full["w_in"][l]
    cols = {n: w_in[:, o:o + s] for n, o, s in IN_SPLIT}
    w = dict(ffn1_in=full["w_ffn1_in"][l], ffn1_out=full["w_ffn1_out"][l], ffn2_in=full["w_ffn2_in"][l], ffn2_out=full["w_ffn2_out"][l],
             wp=jnp.concatenate([cols[n] for n in PACK_ORDER], axis=1),
             wfl=jnp.pad(cols["fl"], ((0, 0), (0, LANES - FOX_HEADS))),
             up_a=full["w_up_a"][l], up_b=full["w_up_b"][l], up_c=full["w_up_c"][l], w_o=full["w_o"][l], xq=full["w_xq"][l],
             xkv=full["w_xkv"][l], xo=full["w_xo"][l], conv_w=full["conv_w"][l])
    for n in ("g_ffn1", "g_mix", "b_gate", "pool_scale", "conv_b", "b_rg_a", "b_rg_x", "g_cross", "g_mem", "g_ffn2"):
        w[n] = small[n][l].reshape(1, -1)
    w["lam"] = small["lru_lambda"][l].reshape(1, -1)
    w["b_f"] = jnp.pad(small["b_f"][l], (0, LANES - FOX_HEADS)).reshape(1, LANES)
    for n in ("w_pool", "w_rg_a", "w_rg_x"):
        w[n] = small[n][l]
    return w


def _unpack_w_in_grad(gwp, gwfl):
    pk = {}
    off = 0
    sizes = {n: s for n, _, s in IN_SPLIT}
    for n in PACK_ORDER:
        pk[n] = gwp[:, off:off + sizes[n]]
        off += sizes[n]
    pk["fl"] = gwfl[:, :FOX_HEADS].astype(gwp.dtype)
    return jnp.concatenate([pk[n] for n, _, _ in IN_SPLIT], axis=1)


def _place():
    x, y, c = lax.axis_index("x"), lax.axis_index("y"), lax.axis_index("c")
    others = [(1 - x, y), (x, 1 - y), (1 - x, 1 - y)]
    return x, y, c, others


def _rcopy(src, dst, send_sems, recv_sems, k, to):
    return pltpu.make_async_remote_copy(src_ref=src, dst_ref=dst, send_sem=send_sems.at[k], recv_sem=recv_sems.at[k],
                                        device_id=to, device_id_type=MESH)


def _comm_call(body, name, ins, out_shape, n_sems):
    return pl.pallas_call(
        body, name=name, in_specs=[ANY] * len(ins), out_specs=[ANY] * len(out_shape), out_shape=out_shape,
        scratch_shapes=[pltpu.SemaphoreType.DMA((n_sems,)), pltpu.SemaphoreType.DMA((n_sems,)), pltpu.SemaphoreType.DMA((n_sems,))],
    )(*ins)


def allgather_same_core(bufs):
    nb = len(bufs)

    def body(*refs):
        ins, outs = refs[:nb], refs[nb:2 * nb]
        send_sems, recv_sems, local_sems = refs[2 * nb:]
        x, y, c, others = _place()
        me = 2 * x + y
        sends = []
        for b in range(nb):
            for j, (ox, oy) in enumerate(others):
                cp = _rcopy(ins[b], outs[b].at[me], send_sems, recv_sems, 3 * b + j, (ox, oy, c))
                cp.start()
                sends.append(cp)
        for b in range(nb):
            for j, (ox, oy) in enumerate(others):
                theirs = outs[b].at[2 * ox + oy]
                _rcopy(theirs, theirs, send_sems, recv_sems, 3 * b + j, (ox, oy, c)).wait_recv()
        for cp in sends:
            cp.wait_send()

    shapes = [jax.ShapeDtypeStruct((N_CHIPS, *b.shape), b.dtype) for b in bufs]
    outs = _comm_call(body, "allgather_same_core", list(bufs), shapes, 3 * nb)
    return [_with_own_block(o, b) for o, b in zip(outs, bufs)]


def _with_own_block(gathered, mine):
    me = 2 * lax.axis_index("x") + lax.axis_index("y")
    return lax.dynamic_update_index_in_dim(gathered, mine, me, axis=0)


def gather_weights(shards):
    nw = len(shards)

    def body(*refs):
        ins, outs = refs[:nw], refs[nw:2 * nw]
        send_sems, recv_sems, _ = refs[2 * nw:]
        x, y, c, others = _place()
        me = 2 * x + y
        first, passed = [], []
        for w in range(nw):
            for j, (ox, oy) in enumerate(others):
                cp = _rcopy(ins[w].at[c], outs[w].at[c, me], send_sems, recv_sems, 3 * w + j, (ox, oy, c))
                cp.start()
                first.append(cp)
        for w in range(nw):
            for j, (ox, oy) in enumerate(others):
                theirs = outs[w].at[c, 2 * ox + oy]
                _rcopy(theirs, theirs, send_sems, recv_sems, 3 * w + j, (ox, oy, c)).wait_recv()
                cp = _rcopy(theirs, theirs, send_sems, recv_sems, 3 * (nw + w) + j, (x, y, 1 - c))
                cp.start()
                passed.append(cp)
        for w in range(nw):
            for j, (ox, oy) in enumerate(others):
                theirs = outs[w].at[1 - c, 2 * ox + oy]
                _rcopy(theirs, theirs, send_sems, recv_sems, 3 * (nw + w) + j, (x, y, 1 - c)).wait_recv()
        for cp in first + passed:
            cp.wait_send()

    shapes = [jax.ShapeDtypeStruct((DEPTH, N_CHIPS, *s.shape[1:]), s.dtype) for s in shards]
    outs = _comm_call(body, "gather_weights", list(shards), shapes, 6 * nw)
    me = 2 * lax.axis_index("x") + lax.axis_index("y")
    return [lax.dynamic_update_index_in_dim(o, s, me, axis=1) for o, s in zip(outs, shards)]


def sibling_layers(bufs):
    nb = len(bufs)

    def body(*refs):
        ins, outs = refs[:nb], refs[nb:2 * nb]
        send_sems, recv_sems, _ = refs[2 * nb:]
        x, y, c, _ = _place()
        work = []
        for b in range(nb):
            cp = _rcopy(ins[b].at[1 - c], outs[b], send_sems, recv_sems, b, (x, y, 1 - c))
            cp.start()
            work.append(cp)
        for cp in work:
            cp.wait()

    shapes = [jax.ShapeDtypeStruct(b.shape[1:], b.dtype) for b in bufs]
    return _comm_call(body, "sibling_layers", list(bufs), shapes, nb)


def chip_blocks(bufs):
    nb = len(bufs)

    def body(*refs):
        ins, outs = refs[:nb], refs[nb:2 * nb]
        send_sems, recv_sems, _ = refs[2 * nb:]
        x, y, c, others = _place()
        sends = []
        for b in range(nb):
            for j, (ox, oy) in enumerate(others):
                cp = _rcopy(ins[b].at[2 * ox + oy], outs[b].at[j], send_sems, recv_sems, 3 * b + j, (ox, oy, c))
                cp.start()
                sends.append(cp)
        for cp in sends:
            cp.wait()

    shapes = [jax.ShapeDtypeStruct((3, *b.shape[1:]), b.dtype) for b in bufs]
    return _comm_call(body, "chip_blocks", list(bufs), shapes, 3 * nb)


def sibling_other_layer(bufs):
    nb = len(bufs)

    def body(*refs):
        ins, outs = refs[:nb], refs[nb:2 * nb]
        send_sems, recv_sems, _ = refs[2 * nb:]
        x, y, c, _ = _place()
        work = []
        for b in range(nb):
            cp = _rcopy(ins[b], outs[b].at[c], send_sems, recv_sems, b, (x, y, 1 - c))
            cp.start()
            work.append(cp)
        for b, cp in enumerate(work):
            cp.wait_send()
            theirs = outs[b].at[1 - c]
            _rcopy(theirs, theirs, send_sems, recv_sems, b, (x, y, 1 - c)).wait_recv()

    shapes = [jax.ShapeDtypeStruct((DEPTH, *b.shape), b.dtype) for b in bufs]
    outs = _comm_call(body, "sibling_other_layer", list(bufs), shapes, nb)
    c = lax.axis_index("c")
    return [lax.dynamic_update_index_in_dim(o, b, c, axis=0) for o, b in zip(outs, bufs)]


ADD_BLOCK_BYTES = 2 * 2 ** 20


def add_arrays(parts, out_dtype, name):
    shape = parts[0].shape
    cols = shape[-1]
    rows = math.prod(shape[:-1])

    def fn(step, i, n, *vals):
        acc = vals[0].astype(F32)
        for v in vals[1:]:
            acc = acc + v.astype(F32)
        return (acc,), ()

    ts = rows
    for cand in (2048, 1024, 512, 256, 128, 64, 32, 16):
        if rows % cand == 0 and cand * cols * 4 <= ADD_BLOCK_BYTES:
            ts = cand
            break
    out = rowwise(fn, [p.reshape(rows, cols) for p in parts], outs=[(cols, out_dtype)], ts=ts, name=name)[0][0]
    return out.reshape(shape)


def reduce_gradients(bufs, names):
    c, me = lax.axis_index("c"), 2 * lax.axis_index("x") + lax.axis_index("y")
    theirs = sibling_layers(bufs)
    mine = [lax.dynamic_index_in_dim(b, c, axis=0, keepdims=False) for b in bufs]
    part = [add_arrays([m, t], m.dtype, "rs_add_sibling_" + n) for m, t, n in zip(mine, theirs, names)]
    recv = chip_blocks(part)
    own = [lax.dynamic_index_in_dim(p, me, axis=0, keepdims=False) for p in part]
    red = [add_arrays([o, r[0], r[1], r[2]], F32, "rs_add_chips_" + n) for o, r, n in zip(own, recv, names)]
    return sibling_other_layer(red)


_BC1 = 1.0 - ADAM_B1 ** ADAM_STEP
_BC2 = 1.0 - ADAM_B2 ** ADAM_STEP


def adamw(w, g, m, v, name):
    rows, cols = w.shape

    def fn(step, i, n, w, g, m, v):
        m2 = ADAM_B1 * m + (1.0 - ADAM_B1) * g
        v2 = ADAM_B2 * v + (1.0 - ADAM_B2) * (g * g)
        delta = -ADAM_LR * ((m2 / _BC1) / (jnp.sqrt(v2 / _BC2) + ADAM_EPS) + ADAM_WD * w)
        return (delta, m2, v2), ()

    ts = _pick(rows, (256, 128, 64, 32, 16, 8))
    return rowwise(fn, [w, g, m, v], outs=[(cols, F32)] * 3, ts=ts, name=name)[0]


def local_step(x, mem, target, full, small):
    ws = [_layer_weights(full, small, l) for l in range(DEPTH)]
    saved = []
    h = x
    for l in range(DEPTH):
        h, s = layer_fwd(h, mem, ws[l], "l")
        saved.append(s)
    loss, dh, dg_final = loss_head(h, target, small["g_final"].reshape(1, D))
    grads = [None] * DEPTH
    for l in reversed(range(DEPTH)):
        dh, grads[l] = layer_bwd(dh, mem, ws[l], saved[l], "l")
    out = {}
    for l in range(DEPTH):
        g = grads[l]
        g["w_in"] = _unpack_w_in_grad(g.pop("wp"), g.pop("wfl"))
        g["b_f"] = g["b_f"][:, :FOX_HEADS]
    for name, shape, _ in BIG:
        out[name] = jnp.stack([grads[l][name].reshape(shape) for l in range(DEPTH)])
    for name, shape in SMALL:
        out[name] = jnp.stack([grads[l][name].reshape(shape) for l in range(DEPTH)])
    out["g_final"] = dg_final.reshape(D)
    return loss, dh, out


def kernel(x, mem, g_ffn1, w_ffn1_in, w_ffn1_out, g_mix, w_in, b_f, b_gate, w_pool, pool_scale, w_up_a, conv_w, conv_b, w_rg_a, b_rg_a, w_rg_x, b_rg_x, lru_lambda, w_up_b, w_up_c, w_o, g_cross, g_mem, w_xq, w_xkv, w_xo, g_ffn2, w_ffn2_in, w_ffn2_out, g_final, loss_target, m_g_ffn1, m_w_ffn1_in, m_w_ffn1_out, m_g_mix, m_w_in, m_b_f, m_b_gate, m_w_pool, m_pool_scale, m_w_up_a, m_conv_w, m_conv_b, m_w_rg_a, m_b_rg_a, m_w_rg_x, m_b_rg_x, m_lru_lambda, m_w_up_b, m_w_up_c, m_w_o, m_g_cross, m_g_mem, m_w_xq, m_w_xkv, m_w_xo, m_g_ffn2, m_w_ffn2_in, m_w_ffn2_out, m_g_final, v_g_ffn1, v_w_ffn1_in, v_w_ffn1_out, v_g_mix, v_w_in, v_b_f, v_b_gate, v_w_pool, v_pool_scale, v_w_up_a, v_conv_w, v_conv_b, v_w_rg_a, v_b_rg_a, v_w_rg_x, v_b_rg_x, v_lru_lambda, v_w_up_b, v_w_up_c, v_w_o, v_g_cross, v_g_mem, v_w_xq, v_w_xkv, v_w_xo, v_g_ffn2, v_w_ffn2_in, v_w_ffn2_out, v_g_final):
    args = dict(locals())
    weights = {n: args[n] for n in ORDER}
    m_in = {n: args["m_" + n] for n in ORDER}
    v_in = {n: args["v_" + n] for n in ORDER}
    big_names = [n for n, _, _ in BIG]
    small_names = [n for n, _ in SMALL] + ["g_final"]

    shards = []
    for n in big_names:
        shards += list(_split3(weights[n])) if n == "conv_w" else [weights[n].astype(BF)]
    gathered = iter(gather_weights(shards))
    full = {}
    for n, shape, axis in BIG:
        g = next(gathered)
        if n == "conv_w":
            g = g.astype(F32) + next(gathered).astype(F32) + next(gathered).astype(F32)
        full[n] = _from_chip_major(g, shape, axis)

    small = {n: weights[n] for n in small_names}
    loss, grad_x, contrib = local_step(x[0], mem[0], loss_target[0], full, small)
    loss = lax.psum(loss, ("x", "y", "c"))

    bufs = [_to_chip_major(contrib[n].astype(BF), shape, axis) for n, shape, axis in BIG]
    small_buf = _pack_small(contrib, small_names, DEPTH * N_CHIPS * SMALL_SEG_ROWS).reshape(DEPTH, N_CHIPS, -1, D)
    red = reduce_gradients(bufs + [small_buf], big_names + ["small"])

    grads = dict(zip(big_names, red[:-1]))
    red_small = allgather_same_core([red[-1]])[0].transpose(1, 0, 2, 3).reshape(-1, D)
    grads.update(_unpack_small(red_small, weights, small_names))

    delta, new_m, new_v = {}, {}, {}
    for name in big_names:
        sh = weights[name].shape
        two_d = (-1, sh[-1])
        d, m2, v2 = adamw(weights[name].reshape(two_d), grads[name].reshape(two_d), m_in[name].reshape(two_d),
                          v_in[name].reshape(two_d), "adamw_" + name)
        delta[name], new_m[name], new_v[name] = d.reshape(sh), m2.reshape(sh), v2.reshape(sh)
    outs = adamw(_pack_small(weights, small_names), _pack_small(grads, small_names), _pack_small(m_in, small_names),
                 _pack_small(v_in, small_names, fill=1.0), "adamw_small")
    for t, res in zip((delta, new_m, new_v), outs):
        t.update(_unpack_small(res, weights, small_names))

    return (loss, grad_x[None], *[grads[n] for n in ORDER], *[delta[n] for n in ORDER],
            *[new_m[n] for n in ORDER], *[new_v[n] for n in ORDER])
```

```python
import math

import jax
import jax.numpy as jnp
from jax import lax
from jax.experimental import pallas as pl
from jax.experimental.pallas import tpu as pltpu

F32 = jnp.float32
BF = jnp.bfloat16

D = 1024
DFF = 2816
DEPTH = 2
POOL_W = 512
POOL_WINDOWS = (2, 4, 8, 16)
LRU_HEADS = 8
LRU_C = 8.0
FOX_HEADS = 8
FOX_HD = 64
FOX_W = 512
X_HEADS = 4
X_HD = 256
EPS = 1e-6
LANES = 128
N_CHIPS = 4

ADAM_LR, ADAM_B1, ADAM_B2, ADAM_EPS, ADAM_WD, ADAM_STEP = 0.001, 0.9, 0.999, 1e-08, 0.01, 10

VMEM_LIMIT_BYTES = 56 * 2 ** 20

PK_XB, PK_GB, PK_GL, PK_XA, PK_Q, PK_K, PK_V = 0, 1024, 2048, 5120, 5632, 6144, 6656
PK_W = 7168

MESH = pl.DeviceIdType.MESH
ANY = pl.BlockSpec(memory_space=pl.ANY)


def _cparams(ngrid):
    return pltpu.CompilerParams(dimension_semantics=("arbitrary",) * ngrid, vmem_limit_bytes=VMEM_LIMIT_BYTES)


def _pick(n, cands):
    for c in cands:
        if n % c == 0:
            return c
    return n


def _iota(shape, dim):
    return lax.broadcasted_iota(jnp.int32, shape, dim)


def _sigmoid(x):
    return 1.0 / (1.0 + jnp.exp(-x))


def _softplus(z):
    return jnp.maximum(z, 0.0) + jnp.log1p(jnp.exp(-jnp.abs(z)))


def _expm1(x):
    small = jnp.abs(x) < 0.25
    xs = jnp.where(small, x, 0.0)
    poly = xs * (1.0 + xs * (1 / 2 + xs * (1 / 6 + xs * (1 / 24 + xs * (1 / 120 + xs * (1 / 720 + xs * (1 / 5040 + xs * (1 / 40320))))))))
    return jnp.where(small, poly, jnp.exp(x) - 1.0)


_GELU_K = math.sqrt(2.0 / math.pi)


def _gelu_and_grad(x):
    inner = _GELU_K * (x + 0.044715 * x * x * x)
    t = jnp.tanh(inner)
    g = 0.5 * x * (1.0 + t)
    dg = 0.5 * (1.0 + t) + 0.5 * x * (1.0 - t * t) * _GELU_K * (1.0 + 3 * 0.044715 * x * x)
    return g, dg


def _dot(a, b, dims):
    return lax.dot_general(a.astype(BF), b.astype(BF), (dims, ((), ())), preferred_element_type=F32)


def _dot_nn(a, b):
    return _dot(a, b, ((1,), (0,)))


def _dot_nt(a, b):
    return _dot(a, b, ((1,), (1,)))


def _dot_tn(a, b):
    return _dot(a, b, ((0,), (0,)))


def _colsum(x):
    return jnp.sum(x, axis=0, keepdims=True)


_TM = (1024, 1408, 512, 256, 128)
_TN = (1408, 1024, 512, 384, 256, 128)
_TK = (2432, 1408, 1024, 512, 256, 128)


def matmul(a, b, mode, *, out_dtype=None, alpha=1.0, res=None, name):
    out_dtype = BF if out_dtype is None else out_dtype
    if mode == "nn":
        (M, K), N = a.shape, b.shape[1]
    elif mode == "nt":
        (M, K), N = a.shape, b.shape[0]
    else:
        (K, M), N = a.shape, b.shape[1]
    tm, tn, tk = _pick(M, _TM), _pick(N, _TN), _pick(K, _TK)
    nk = K // tk
    dims = {"nn": ((1,), (0,)), "nt": ((1,), (1,)), "tn": ((0,), (0,))}[mode]
    a_spec = pl.BlockSpec((tk, tm), lambda i, j, k: (k, i)) if mode == "tn" else pl.BlockSpec((tm, tk), lambda i, j, k: (i, k))
    b_spec = pl.BlockSpec((tn, tk), lambda i, j, k: (j, k)) if mode == "nt" else pl.BlockSpec((tk, tn), lambda i, j, k: (k, j))
    o_spec = pl.BlockSpec((tm, tn), lambda i, j, k: (i, j))
    has_res = res is not None

    def body(*refs):
        a_ref, b_ref = refs[0], refs[1]
        r_ref = refs[2] if has_res else None
        o_ref, acc = refs[2 + has_res], refs[3 + has_res]
        k = pl.program_id(2)

        @pl.when(k == 0)
        def _():
            acc[...] = jnp.zeros_like(acc)

        acc[...] += _dot(a_ref[...], b_ref[...], dims)

        @pl.when(k == nk - 1)
        def _():
            r = acc[...]
            if alpha != 1.0:
                r = r * alpha
            if has_res:
                r = r + r_ref[...].astype(F32)
            o_ref[...] = r.astype(o_ref.dtype)

    ins = [a, b] + ([res] if has_res else [])
    specs = [a_spec, b_spec] + ([o_spec] if has_res else [])
    return pl.pallas_call(
        body, name=name, grid=(M // tm, N // tn, nk), in_specs=specs, out_specs=o_spec,
        out_shape=jax.ShapeDtypeStruct((M, N), out_dtype), scratch_shapes=[pltpu.VMEM((tm, tn), F32)],
        compiler_params=_cparams(3))(*ins)


class Rows:
    def __init__(self, arr, w=None, cb=0, halo=None, h=8):
        self.arr, self.w, self.cb, self.halo, self.h = arr, (arr.shape[1] if w is None else w), cb, halo, h


def rowwise(fn, rows, consts=(), outs=(), accs=(), scratch=(), *, ts, name, reverse=False):
    rows = [r if isinstance(r, Rows) else Rows(r) for r in rows]
    S = rows[0].arr.shape[0]
    n = S // ts
    tile = (lambda s: n - 1 - s) if reverse else (lambda s: s)

    def row_spec(r):
        if r.halo is None:
            return pl.BlockSpec((ts, r.w), lambda s: (tile(s), r.cb))
        per, last = ts // r.h, S // r.h - 1
        if r.halo == "prev":
            return pl.BlockSpec((r.h, r.w), lambda s: (jnp.maximum(tile(s) * per - 1, 0), r.cb))
        return pl.BlockSpec((r.h, r.w), lambda s: (jnp.minimum((tile(s) + 1) * per, last), r.cb))

    def whole(shape):
        nd = len(shape)
        return pl.BlockSpec(tuple(shape), lambda s: (0,) * nd)

    nr, nc, no, na = len(rows), len(consts), len(outs), len(accs)

    def body(*refs):
        in_refs, out_refs = refs[:nr + nc], refs[nr + nc:nr + nc + no]
        acc_refs, scr = refs[nr + nc + no:nr + nc + no + na], refs[nr + nc + no + na:]
        step = pl.program_id(0)
        o, inc = fn(step, tile(step), n, *[r[...] for r in in_refs], *scr)
        for ref, val in zip(out_refs, o):
            ref[...] = val.astype(ref.dtype)
        if na:
            @pl.when(step == 0)
            def _():
                for ref in acc_refs:
                    ref[...] = jnp.zeros_like(ref)
            for ref, val in zip(acc_refs, inc):
                ref[...] += val.astype(ref.dtype)

    out_shape = [jax.ShapeDtypeStruct((S, w), dt) for w, dt in outs] + [jax.ShapeDtypeStruct(tuple(s), dt) for s, dt in accs]
    out_specs = [pl.BlockSpec((ts, w), lambda s: (tile(s), 0)) for w, _ in outs] + [whole(s) for s, _ in accs]
    res = pl.pallas_call(
        body, name=name, grid=(n,),
        in_specs=[row_spec(r) for r in rows] + [whole(c.shape) for c in consts],
        out_specs=out_specs, out_shape=out_shape, scratch_shapes=list(scratch),
        compiler_params=_cparams(1))(*[r.arr for r in rows], *consts)
    return res[:no], res[no:]


def _rstd(x):
    return lax.rsqrt(jnp.mean(x * x, axis=-1, keepdims=True) + EPS)


def rms_fwd(x, g, *, name, ts=512):
    def fn(step, i, n, x, g):
        return ((x * _rstd(x)) * g,), ()
    return rowwise(fn, [x], [g], [(D, BF)], ts=min(ts, x.shape[0]), name=name)[0][0]


def rms_bwd(x, g, dh, dres, *, name, ts=512):
    has_res = dres is not None

    def fn(step, i, n, x, dh, *rest):
        g = rest[-1]
        r = _rstd(x)
        xhat = x * r
        dh = dh.astype(F32)
        gd = dh * g
        dx = r * (gd - xhat * jnp.mean(xhat * gd, axis=-1, keepdims=True))
        if has_res:
            dx = dx + rest[0]
        return (dx,), (_colsum(dh * xhat),)

    rows = [x, dh] + ([dres] if has_res else [])
    o, a = rowwise(fn, rows, [g], [(D, F32)], [((1, D), F32)], ts=min(ts, x.shape[0]), name=name)
    return o[0], a[0]


def ffn_fwd(x, g, w_in, w_out, tag):
    h = rms_fwd(x, g, name=tag + "_rms")
    ab = matmul(h, w_in, "nn", name=tag + "_in")

    def act_fn(step, i, n, ab):
        a, b = ab[:, :DFF].astype(F32), ab[:, DFF:].astype(F32)
        return ((a * _sigmoid(a)) * b,), ()

    act = rowwise(act_fn, [ab], outs=[(DFF, BF)], ts=256, name=tag + "_act")[0][0]
    y = matmul(act, w_out, "nn", out_dtype=F32, alpha=0.5, res=x, name=tag + "_out")
    return y, (x, h, ab, act)


def ffn_bwd(dy, saved, g, w_in, w_out, tag):
    x, h, ab, act = saved
    dact = matmul(dy, w_out, "nt", alpha=0.5, name=tag + "_dact")
    dw_out = matmul(act, dy, "tn", alpha=0.5, name=tag + "_dwout")

    def dab_fn(step, i, n, ab, dact):
        a, b, dact = ab[:, :DFF].astype(F32), ab[:, DFF:].astype(F32), dact.astype(F32)
        s = _sigmoid(a)
        da = dact * b * (s * (1.0 + a * (1.0 - s)))
        db = dact * (a * s)
        return (jnp.concatenate([da, db], axis=1),), ()

    dab = rowwise(dab_fn, [ab, dact], outs=[(2 * DFF, BF)], ts=256, name=tag + "_dab")[0][0]
    dw_in = matmul(h, dab, "tn", name=tag + "_dwin")
    dh = matmul(dab, w_in, "nt", name=tag + "_dh")
    dx, dg = rms_bwd(x, g, dh, dy, name=tag + "_drms")
    return dx, dg, dw_in, dw_out


POOL_TS = 256
POOL_HALO = 16


def pool_fwd(P, w_pool, pool_scale, tag):
    ts = POOL_TS

    def fn(step, i, n, xa, halo, w, scale):
        xa = xa.astype(F32)
        halo = halo.astype(F32) * jnp.where(i > 0, 1.0, 0.0)
        ext = jnp.concatenate([halo, xa], axis=0)
        pos = (i * ts + 1 + _iota((ts, LANES), 0)).astype(F32)
        ds, ys = [], []
        for gi, win in enumerate(POOL_WINDOWS):
            e = ext[:, gi * LANES:(gi + 1) * LANES]
            sh = 1
            while sh < win:
                e = e + pltpu.roll(e, sh, 0)
                sh *= 2
            mean = e[POOL_HALO:] / jnp.minimum(pos, float(win))
            d = mean - xa[:, gi * LANES:(gi + 1) * LANES]
            ds.append(d)
            ys.append(_dot_nn(d, w[gi]))
        d = jnp.concatenate(ds, axis=1)
        return (d, jnp.concatenate(ys, axis=1) * scale), ()

    xa = Rows(P, POOL_W, PK_XA // POOL_W)
    xa_prev = Rows(P, POOL_W, PK_XA // POOL_W, "prev", POOL_HALO)
    (d, ya0), _ = rowwise(fn, [xa, xa_prev], [w_pool, pool_scale], [(POOL_W, BF), (POOL_W, BF)], ts=ts, name=tag + "_pool")
    return d, ya0


def pool_bwd(dya0, d, w_pool, pool_scale, tag):
    ts = POOL_TS
    L = ts + POOL_HALO

    def fn(step, i, n, dya0, dya0_next, d, w, scale):
        d = d.astype(F32)
        dz = jnp.concatenate([dya0.astype(F32), dya0_next.astype(F32) * jnp.where(i < n - 1, 1.0, 0.0)], axis=0) * scale
        pos = (i * ts + 1 + _iota((L, LANES), 0)).astype(F32)
        dxa, dws, dsc = [], [], []
        for gi, win in enumerate(POOL_WINDOWS):
            sl = slice(gi * LANES, (gi + 1) * LANES)
            dzg = dz[:, sl]
            dd = _dot_nt(dzg, w[gi])
            e = dd / jnp.minimum(pos, float(win))
            sh = 1
            while sh < win:
                e = e + pltpu.roll(e, L - sh, 0)
                sh *= 2
            dxa.append(e[:ts] - dd[:ts])
            dws.append(_dot_tn(d[:, sl], dzg[:ts])[None])
            z = _dot_nn(d[:, sl], w[gi])
            dsc.append(_colsum(dya0[:, sl].astype(F32) * z))
        return (jnp.concatenate(dxa, axis=1),), (jnp.concatenate(dws, axis=0), jnp.concatenate(dsc, axis=1))

    (dxa,), (dw_pool, dscale) = rowwise(
        fn, [dya0, Rows(dya0, halo="next", h=POOL_HALO), d], [w_pool, pool_scale],
        [(POOL_W, BF)], [((4, LANES, LANES), F32), ((1, POOL_W), F32)], ts=ts, name=tag + "_dpool")
    return dxa, dw_pool, dscale


LRU_R = 512


def _scan_fwd(A, U):
    R = A.shape[0]
    row = _iota(A.shape, 0)
    d = 1
    while d < R:
        m = row >= d
        A_sh = jnp.where(m, pltpu.roll(A, d, 0), 1.0)
        U_sh = jnp.where(m, pltpu.roll(U, d, 0), 0.0)
        U = A * U_sh + U
        A = A * A_sh
        d *= 2
    return A, U


def _scan_bwd(B, X):
    R = B.shape[0]
    row = _iota(B.shape, 0)
    d = 1
    while d < R:
        m = row < R - d
        B_sh = jnp.where(m, pltpu.roll(B, R - d, 0), 1.0)
        X_sh = jnp.where(m, pltpu.roll(X, R - d, 0), 0.0)
        X = X + B * X_sh
        B = B * B_sh
        d *= 2
    return X


def _lru_gates(xc, wa, ba, wx, bx, lam):
    r = _sigmoid(_dot_nn(xc, wa) + ba)
    ig = _sigmoid(_dot_nn(xc, wx) + bx)
    sp = _softplus(-lam)
    log_a = -LRU_C * r * sp
    a = jnp.exp(log_a)
    mult = jnp.sqrt(-_expm1(2.0 * log_a))
    return r, ig, sp, a, mult


def _lru_specs(S, R, reverse):
    nch = S // R
    ch = (lambda j: nch - 1 - j) if reverse else (lambda j: j)
    per = R // 8

    def col(off):
        return pl.BlockSpec((R, LANES), lambda h, j: (ch(j), off + h))

    def prev(off):
        return pl.BlockSpec((8, LANES), lambda h, j: (jnp.maximum(ch(j) * per - 1, 0), off + h))

    vec = pl.BlockSpec((1, LANES), lambda h, j: (0, h))
    cw = pl.BlockSpec((4, LANES), lambda h, j: (0, h))
    wsq = pl.BlockSpec((None, LANES, LANES), lambda h, j: (h, 0, 0))
    return nch, ch, col, prev, vec, cw, wsq


def lru_fwd(P, conv_w, conv_b, w_a, b_a, w_x, b_x, lam, tag):
    S = P.shape[0]
    R = min(LRU_R, S)
    nch, ch, col, prev, vec, cw_spec, wsq = _lru_specs(S, R, False)

    def body(xb_ref, halo_ref, gb_ref, cw_ref, cb_ref, wa_ref, ba_ref, wx_ref, bx_ref, lam_ref, xc_ref, h_ref, yb_ref, carry):
        j = pl.program_id(1)
        xb = xb_ref[...].astype(F32)
        halo = halo_ref[...].astype(F32) * jnp.where(j > 0, 1.0, 0.0)
        ext = jnp.concatenate([halo, xb], axis=0)
        cw = cw_ref[...]
        xc = cb_ref[...]
        for k in range(4):
            e = ext if k == 3 else pltpu.roll(ext, 3 - k, 0)
            xc = xc + e[8:] * cw[k:k + 1]
        r, ig, sp, a, mult = _lru_gates(xc, wa_ref[...], ba_ref[...], wx_ref[...], bx_ref[...], lam_ref[...])
        u = mult * (ig * xc)
        cum_a, hloc = _scan_fwd(a, u)

        @pl.when(j == 0)
        def _():
            carry[...] = jnp.zeros_like(carry)

        hfull = hloc + cum_a * carry[0:1]
        carry[...] = jnp.broadcast_to(hfull[R - 1:R], carry.shape)
        gel, _ = _gelu_and_grad(gb_ref[...].astype(F32))
        xc_ref[...] = xc
        h_ref[...] = hfull
        yb_ref[...] = (hfull * gel).astype(yb_ref.dtype)

    out = pl.BlockSpec((R, LANES), lambda h, j: (j, h))
    return pl.pallas_call(
        body, name=tag + "_lru", grid=(LRU_HEADS, nch),
        in_specs=[col(PK_XB // LANES), prev(PK_XB // LANES), col(PK_GB // LANES), cw_spec, vec, wsq, vec, wsq, vec, vec],
        out_specs=[out, out, out],
        out_shape=[jax.ShapeDtypeStruct((S, D), F32), jax.ShapeDtypeStruct((S, D), F32), jax.ShapeDtypeStruct((S, D), BF)],
        scratch_shapes=[pltpu.VMEM((8, LANES), F32)], compiler_params=_cparams(2),
    )(P, P, P, conv_w, conv_b, w_a, b_a, w_x, b_x, lam)


def lru_bwd(dyb0, P, hh, xc, conv_w, w_a, b_a, w_x, b_x, lam, tag):
    S = P.shape[0]
    R = min(LRU_R, S)
    nch, ch, col, prev, vec, cw_spec, wsq = _lru_specs(S, R, True)

    def body(dyb_ref, gb_ref, h_ref, hprev_ref, xc_ref, xb_ref, xbprev_ref, cw_ref, wa_ref, ba_ref, wx_ref, bx_ref, lam_ref,
             dgb_ref, dxb_ref, dwa_ref, dba_ref, dwx_ref, dbx_ref, dlam_ref, dcw_ref, dcb_ref, gcarry, dxc_head):
        j = pl.program_id(1)
        jj = nch - 1 - j
        has_prev = jnp.where(jj > 0, 1.0, 0.0)
        row = _iota((R, LANES), 0)
        xc, hv, lam = xc_ref[...], h_ref[...], lam_ref[...]
        wa, wx = wa_ref[...], wx_ref[...]
        r, ig, sp, a, mult = _lru_gates(xc, wa, ba_ref[...], wx, bx_ref[...], lam)
        dyb = dyb_ref[...].astype(F32)
        gel, dgel = _gelu_and_grad(gb_ref[...].astype(F32))
        dgb_ref[...] = (dyb * hv * dgel).astype(dgb_ref.dtype)

        @pl.when(j == 0)
        def _():
            gcarry[...] = jnp.zeros_like(gcarry)
            dxc_head[...] = jnp.zeros_like(dxc_head)
            for ref in (dwa_ref, dba_ref, dwx_ref, dbx_ref, dlam_ref, dcw_ref, dcb_ref):
                ref[...] = jnp.zeros_like(ref)

        B = jnp.where(row < R - 1, pltpu.roll(a, R - 1, 0), 0.0)
        X = dyb * gel + jnp.where(row == R - 1, gcarry[0:1], 0.0)
        G = _scan_bwd(B, X)
        gcarry[...] = jnp.broadcast_to(a[0:1] * G[0:1], gcarry.shape)
        hprev = jnp.where(row == 0, hprev_ref[...][7:8] * has_prev, pltpu.roll(hv, 1, 0))
        da = G * hprev
        dmult = G * ig * xc
        dig = G * mult * xc
        dxc = G * mult * ig
        dlog_a = da * a - dmult * (a * a) / mult
        dzr = dlog_a * (-LRU_C * sp) * (r * (1.0 - r))
        dzi = dig * (ig * (1.0 - ig))
        dxc = dxc + _dot_nt(dzr, wa) + _dot_nt(dzi, wx)
        dwa_ref[...] += _dot_tn(xc, dzr)
        dwx_ref[...] += _dot_tn(xc, dzi)
        dba_ref[...] += _colsum(dzr)
        dbx_ref[...] += _colsum(dzi)
        dlam_ref[...] += _colsum(dlog_a * (-LRU_C * r)) * (-_sigmoid(-lam))
        dcb_ref[...] += _colsum(dxc)
        cw = cw_ref[...]
        ext = jnp.concatenate([dxc, dxc_head[...]], axis=0)
        dxb = dxc * cw[3:4]
        for k in range(3):
            dxb = dxb + pltpu.roll(ext, R + 8 - (3 - k), 0)[:R] * cw[k:k + 1]
        dxb_ref[...] = dxb.astype(dxb_ref.dtype)
        dxc_head[...] = dxc[0:8]
        extx = jnp.concatenate([xbprev_ref[...].astype(F32) * has_prev, xb_ref[...].astype(F32)], axis=0)
        incs = []
        for k in range(4):
            e = extx if k == 3 else pltpu.roll(extx, 3 - k, 0)
            incs.append(_colsum(dxc * e[8:]))
        dcw_ref[...] += jnp.concatenate(incs, axis=0)

    plain = pl.BlockSpec((R, LANES), lambda h, j: (ch(j), h))
    plain_prev = pl.BlockSpec((8, LANES), lambda h, j: (jnp.maximum(ch(j) * (R // 8) - 1, 0), h))
    return pl.pallas_call(
        body, name=tag + "_dlru", grid=(LRU_HEADS, nch),
        in_specs=[plain, col(PK_GB // LANES), plain, plain_prev, plain, col(PK_XB // LANES), prev(PK_XB // LANES),
                  cw_spec, wsq, vec, wsq, vec, vec],
        out_specs=[plain, plain, wsq, vec, wsq, vec, vec, cw_spec, vec],
        out_shape=[jax.ShapeDtypeStruct((S, D), BF), jax.ShapeDtypeStruct((S, D), BF),
                   jax.ShapeDtypeStruct((LRU_HEADS, LANES, LANES), F32), jax.ShapeDtypeStruct((1, D), F32),
                   jax.ShapeDtypeStruct((LRU_HEADS, LANES, LANES), F32), jax.ShapeDtypeStruct((1, D), F32),
                   jax.ShapeDtypeStruct((1, D), F32), jax.ShapeDtypeStruct((4, D), F32), jax.ShapeDtypeStruct((1, D), F32)],
        scratch_shapes=[pltpu.VMEM((8, LANES), F32), pltpu.VMEM((8, LANES), F32)], compiler_params=_cparams(2),
    )(dyb0, P, hh, hh, xc, P, P, conv_w, w_a, b_a, w_x, b_x, lam)


CUM_TS = 512
FLASH_T = 512
FLASH_G = 4
NEG = -1e30


def cumlogf_fwd(fl, b_f, tag):
    ts = min(CUM_TS, fl.shape[0])

    def fn(step, i, n, fl, bf, carry):
        x = -_softplus(-(fl + bf))
        row = _iota(x.shape, 0)
        d = 1
        while d < ts:
            x = x + jnp.where(row >= d, pltpu.roll(x, d, 0), 0.0)
            d *= 2

        @pl.when(step == 0)
        def _():
            carry[...] = jnp.zeros_like(carry)

        c = x + carry[0:1]
        carry[...] = jnp.broadcast_to(c[ts - 1:ts], carry.shape)
        return (c,), ()

    return rowwise(fn, [fl], [b_f], [(LANES, F32)], scratch=[pltpu.VMEM((8, LANES), F32)], ts=ts, name=tag + "_cum")[0][0]


def cumlogf_bwd(dc, fl, b_f, tag):
    ts = min(CUM_TS, fl.shape[0])

    def fn(step, i, n, dc, fl, bf, carry):
        row = _iota(dc.shape, 0)
        x = dc
        d = 1
        while d < ts:
            x = x + jnp.where(row < ts - d, pltpu.roll(x, ts - d, 0), 0.0)
            d *= 2

        @pl.when(step == 0)
        def _():
            carry[...] = jnp.zeros_like(carry)

        g = x + carry[0:1]
        carry[...] = jnp.broadcast_to(g[0:1], carry.shape)
        dfl = g * _sigmoid(-(fl + bf))
        return (dfl,), (_colsum(dfl),)

    (dfl,), (db,) = rowwise(fn, [dc, fl], [b_f], [(LANES, F32)], [((1, LANES), F32)], scratch=[pltpu.VMEM((8, LANES), F32)],
                            ts=ts, name=tag + "_dcum", reverse=True)
    return dfl, db


FOX_KA = 80


def _split3(c):
    c = c.astype(F32)
    hi = lax.reduce_precision(c, 8, 7)
    r = c - hi
    mid = lax.reduce_precision(r, 8, 7)
    return hi.astype(BF), mid.astype(BF), (r - mid).astype(BF)


def fox_operands(q, k, ch):
    H, S, hd = q.shape
    qs = q * (FOX_HD ** -0.5)
    pieces = [p.astype(F32) for p in _split3(ch)]
    pad = FOX_KA - hd

    def unit(i):
        return (jnp.arange(FOX_KA) == hd + i).astype(F32)

    qa = jnp.pad(qs.transpose(0, 2, 1).astype(F32), ((0, 0), (0, pad), (0, 0)))
    ka = jnp.pad(k.astype(F32), ((0, 0), (0, 0), (0, pad)))
    for i, p in enumerate(pieces):
        qa = qa + p[:, None, :] * unit(i)[None, :, None] + unit(3 + i)[None, :, None]
        ka = ka + unit(i)[None, None, :] - p[:, :, None] * unit(3 + i)[None, None, :]
    return qa.astype(BF), ka.astype(BF), qs


def _causal(s):
    return jnp.where(_iota(s.shape, 0) <= _iota(s.shape, 1), s, NEG)


def flash_fwd(qa, ka, vt, tag):
    H, hd, S = vt.shape
    T = min(FLASH_T, S)
    nb = S // T

    pairs = [(qi, ki) for qi in range(nb) for ki in range(qi + 1)]
    q_tab = jnp.asarray([p[0] for p in pairs], jnp.int32)
    k_tab = jnp.asarray([p[1] for p in pairs], jnp.int32)

    def body(q_tab_ref, k_tab_ref, ka_ref, qa_ref, vt_ref, o_ref, lse_ref, m_s, l_s, acc):
        qi, ki = q_tab_ref[pl.program_id(1)], k_tab_ref[pl.program_id(1)]

        @pl.when(ki == 0)
        def _():
            m_s[...] = jnp.full_like(m_s, NEG)
            l_s[...] = jnp.zeros_like(l_s)
            acc[...] = jnp.zeros_like(acc)

        def step(diagonal):
            for g in range(FLASH_G):
                s = _dot_nn(ka_ref[g], qa_ref[g])
                if diagonal:
                    s = _causal(s)
                m_new = jnp.maximum(m_s[g], jnp.max(s, axis=0, keepdims=True))
                alpha = jnp.exp(m_s[g] - m_new)
                p = jnp.exp(s - m_new)
                l_s[g] = alpha * l_s[g] + jnp.sum(p, axis=0, keepdims=True)
                p_hi = p.astype(BF)
                p_lo = p - p_hi.astype(F32)
                v = vt_ref[g]
                acc[g] = alpha * acc[g] + (_dot_nn(v, p_hi) + _dot_nn(v, p_lo))
                m_s[g] = m_new

        @pl.when(ki < qi)
        def _():
            step(False)

        @pl.when(ki == qi)
        def _():
            step(True)
            o_ref[...] = acc[...] / l_s[...]
            lse_ref[...] = m_s[...] + jnp.log(l_s[...])

    G = FLASH_G
    grid_spec = pltpu.PrefetchScalarGridSpec(
        num_scalar_prefetch=2, grid=(H // G, len(pairs)),
        in_specs=[pl.BlockSpec((G, T, FOX_KA), lambda h, p, qt, kt: (h, kt[p], 0)),
                  pl.BlockSpec((G, FOX_KA, T), lambda h, p, qt, kt: (h, 0, qt[p])),
                  pl.BlockSpec((G, hd, T), lambda h, p, qt, kt: (h, 0, kt[p]))],
        out_specs=[pl.BlockSpec((G, hd, T), lambda h, p, qt, kt: (h, 0, qt[p])),
                   pl.BlockSpec((G, 1, T), lambda h, p, qt, kt: (h, 0, qt[p]))],
        scratch_shapes=[pltpu.VMEM((G, 1, T), F32), pltpu.VMEM((G, 1, T), F32), pltpu.VMEM((G, hd, T), F32)])
    return pl.pallas_call(
        body, name=tag + "_flash", grid_spec=grid_spec,
        out_shape=[jax.ShapeDtypeStruct((H, hd, S), F32), jax.ShapeDtypeStruct((H, 1, S), F32)],
        compiler_params=_cparams(2))(q_tab, k_tab, ka, qa, vt)


def flash_bwd(qa, ka, qs, kt, v, ot, do, dot_, lse, tag):
    H, S, hd = v.shape
    T = min(FLASH_T, S)
    nb = S // T

    pairs = [(qi, ki) for ki in range(nb) for qi in range(ki, nb)]
    q_tab = jnp.asarray([p[0] for p in pairs], jnp.int32)
    k_tab = jnp.asarray([p[1] for p in pairs], jnp.int32)

    def body(q_tab_ref, k_tab_ref, ka_ref, qa_ref, qs_ref, kt_ref, v_ref, ot_ref, do_ref, dot_ref, lse_ref,
             dqt_ref, dk_ref, dv_ref, dc_ref, dk_acc, dv_acc, dc_acc):
        qi, ki = q_tab_ref[pl.program_id(1)], k_tab_ref[pl.program_id(1)]

        @pl.when(pl.program_id(1) == 0)
        def _():
            dqt_ref[...] = jnp.zeros_like(dqt_ref)

        @pl.when(qi == ki)
        def _():
            dk_acc[...] = jnp.zeros_like(dk_acc)
            dv_acc[...] = jnp.zeros_like(dv_acc)
            dc_acc[...] = jnp.zeros_like(dc_acc)

        def step(diagonal):
            for g in range(FLASH_G):
                s = _dot_nn(ka_ref[g], qa_ref[g])
                if diagonal:
                    s = _causal(s)
                p = jnp.exp(s - lse_ref[g])
                dot_v = dot_ref[g]
                dp = _dot_nn(v_ref[g], dot_v)
                delta = jnp.sum(dot_v.astype(F32) * ot_ref[g], axis=0, keepdims=True)
                ds = p * (dp - delta)
                part = ds[:, 0:LANES]
                for j in range(1, T // LANES):
                    part = part + ds[:, j * LANES:(j + 1) * LANES]
                dc_acc[g] += part
                dsb = ds.astype(BF)
                dv_acc[g] += _dot_nn(p, do_ref[g])
                dk_acc[g] += _dot_nn(dsb, qs_ref[g])
                dqt_ref[g, qi] += _dot_nn(kt_ref[g], dsb)

        @pl.when(qi > ki)
        def _():
            step(False)

        @pl.when(qi == ki)
        def _():
            step(True)

        @pl.when(qi == nb - 1)
        def _():
            dk_ref[...] = dk_acc[...]
            dv_ref[...] = dv_acc[...]
            for g in range(FLASH_G):
                dc_ref[g] = -jnp.sum(dc_acc[g].T, axis=0, keepdims=True)

    qblk = lambda h, p, qt, kt_: (h, qt[p], 0)
    qblk_t = lambda h, p, qt, kt_: (h, 0, qt[p])
    kblk = lambda h, p, qt, kt_: (h, kt_[p], 0)
    kblk_t = lambda h, p, qt, kt_: (h, 0, kt_[p])
    G = FLASH_G
    grid_spec = pltpu.PrefetchScalarGridSpec(
        num_scalar_prefetch=2, grid=(H // G, len(pairs)),
        in_specs=[pl.BlockSpec((G, T, FOX_KA), kblk), pl.BlockSpec((G, FOX_KA, T), qblk_t),
                  pl.BlockSpec((G, T, hd), qblk), pl.BlockSpec((G, hd, T), kblk_t), pl.BlockSpec((G, T, hd), kblk),
                  pl.BlockSpec((G, hd, T), qblk_t), pl.BlockSpec((G, T, hd), qblk), pl.BlockSpec((G, hd, T), qblk_t),
                  pl.BlockSpec((G, 1, T), qblk_t)],
        out_specs=[pl.BlockSpec((G, nb, hd, T), lambda h, p, qt, kt_: (h, 0, 0, 0)), pl.BlockSpec((G, T, hd), kblk),
                   pl.BlockSpec((G, T, hd), kblk), pl.BlockSpec((G, 1, T), kblk_t)],
        scratch_shapes=[pltpu.VMEM((G, T, hd), F32), pltpu.VMEM((G, T, hd), F32), pltpu.VMEM((G, T, LANES), F32)])
    return pl.pallas_call(
        body, name=tag + "_dflash", grid_spec=grid_spec,
        out_shape=[jax.ShapeDtypeStruct((H, nb, hd, T), F32), jax.ShapeDtypeStruct((H, S, hd), F32),
                   jax.ShapeDtypeStruct((H, S, hd), F32), jax.ShapeDtypeStruct((H, 1, S), F32)],
        compiler_params=_cparams(2))(q_tab, k_tab, ka, qa, qs, kt, v, ot, do, dot_, lse)


def _to_heads(x2d):
    S = x2d.shape[0]
    return x2d.reshape(S, FOX_HEADS, FOX_HD).transpose(1, 0, 2)


def _from_heads(x3d):
    S = x3d.shape[1]
    return x3d.transpose(1, 0, 2).reshape(S, FOX_W)


def _gl_rows(P):
    return [Rows(P, D, PK_GL // D + k) for k in range(3)]


def merge_fwd(P, ya, yb, yc, b_gate, tag):
    def fn(step, i, n, g0, g1, g2, ya, yb, yc, b):
        out = 0.0
        for k, (gl, y) in enumerate(((g0, ya), (g1, yb), (g2, yc))):
            out = out + _sigmoid(gl.astype(F32) + b[:, k * D:(k + 1) * D]) * y.astype(F32)
        return (out,), ()
    return rowwise(fn, _gl_rows(P) + [ya, yb, yc], [b_gate], [(D, BF)], ts=512, name=tag + "_merge")[0][0]


def merge_bwd(dm, P, ya, yb, yc, b_gate, tag):
    def fn(step, i, n, dm, g0, g1, g2, ya, yb, yc, b):
        dm = dm.astype(F32)
        dys, dgls = [], []
        for k, (gl, y) in enumerate(((g0, ya), (g1, yb), (g2, yc))):
            g = _sigmoid(gl.astype(F32) + b[:, k * D:(k + 1) * D])
            dys.append(dm * g)
            dgls.append(dm * y.astype(F32) * (g * (1.0 - g)))
        dgl = jnp.concatenate(dgls, axis=1)
        return (dys[0], dys[1], dys[2], dgl), (_colsum(dgl),)
    (dya, dyb, dyc, dgl), (db,) = rowwise(
        fn, [dm] + _gl_rows(P) + [ya, yb, yc], [b_gate], [(D, BF), (D, BF), (D, BF), (3 * D, BF)], [((1, 3 * D), F32)],
        ts=256, name=tag + "_dmerge")
    return dya, dyb, dyc, dgl, db


def _xattn_probs(qh, kh):
    s = _dot_nt(qh, kh) * (X_HD ** -0.5)
    e = jnp.exp(s - jnp.max(s, axis=1, keepdims=True))
    return e / jnp.sum(e, axis=1, keepdims=True)


def xattn_fwd(q, kv, tag):
    def fn(step, i, n, q, k, v):
        os = []
        for h in range(X_HEADS):
            sl = slice(h * X_HD, (h + 1) * X_HD)
            os.append(_dot_nn(_xattn_probs(q[:, sl], k[:, sl]), v[:, sl]))
        return (jnp.concatenate(os, axis=1),), ()

    return _xattn_call(fn, [q], kv, [(D, BF)], [], tag + "_xattn")[0][0]


def _xattn_call(fn, rows, kv, outs, accs, name):
    S, ts, M = rows[0].shape[0], 512, kv.shape[0]
    ts = min(ts, S)
    nr, no, na = len(rows), len(outs), len(accs)

    def body(*refs):
        in_refs, out_refs, acc_refs = refs[:nr + 2], refs[nr + 2:nr + 2 + no], refs[nr + 2 + no:]
        step = pl.program_id(0)
        o, inc = fn(step, step, S // ts, *[r[...] for r in in_refs])
        for ref, val in zip(out_refs, o):
            ref[...] = val.astype(ref.dtype)
        if na:
            @pl.when(step == 0)
            def _():
                for ref in acc_refs:
                    ref[...] = jnp.zeros_like(ref)
            for ref, val in zip(acc_refs, inc):
                ref[...] += val

    row = pl.BlockSpec((ts, D), lambda s: (s, 0))
    res = pl.pallas_call(
        body, name=name, grid=(S // ts,),
        in_specs=[row] * nr + [pl.BlockSpec((M, D), lambda s: (0, 0)), pl.BlockSpec((M, D), lambda s: (0, 1))],
        out_specs=[row] * no + [pl.BlockSpec(tuple(s), lambda s_: (0, 0)) for s, _ in accs],
        out_shape=[jax.ShapeDtypeStruct((S, w), dt) for w, dt in outs] + [jax.ShapeDtypeStruct(tuple(s), dt) for s, dt in accs],
        compiler_params=_cparams(1))(*rows, kv, kv)
    return res[:no], res[no:]


def xattn_bwd(q, kv, do, tag):
    def fn(step, i, n, q, do, k, v):
        dqs, dks, dvs = [], [], []
        for h in range(X_HEADS):
            sl = slice(h * X_HD, (h + 1) * X_HD)
            p = _xattn_probs(q[:, sl], k[:, sl])
            dp = _dot_nt(do[:, sl], v[:, sl])
            ds = (p * (dp - jnp.sum(p * dp, axis=1, keepdims=True)) * (X_HD ** -0.5)).astype(BF)
            dqs.append(_dot_nn(ds, k[:, sl]))
            dks.append(_dot_tn(ds, q[:, sl]))
            dvs.append(_dot_tn(p, do[:, sl]))
        return (jnp.concatenate(dqs, axis=1),), (jnp.concatenate(dks + dvs, axis=1),)

    (dq,), (dkv,) = _xattn_call(fn, [q, do], kv, [(D, BF)], [((kv.shape[0], 2 * D), F32)], tag + "_dxattn")
    return dq, dkv


def layer_fwd(x, mem, w, tag):
    x1, s_ffn1 = ffn_fwd(x, w["g_ffn1"], w["ffn1_in"], w["ffn1_out"], tag + "_ffn1")
    u = rms_fwd(x1, w["g_mix"], name=tag + "_mixrms")
    P = matmul(u, w["wp"], "nn", name=tag + "_proj")
    fl = matmul(u, w["wfl"], "nn", out_dtype=F32, name=tag + "_projf")
    d, ya0 = pool_fwd(P, w["w_pool"], w["pool_scale"], tag)
    ya = matmul(ya0, w["up_a"], "nn", name=tag + "_upa")
    xc, hh, yb0 = lru_fwd(P, w["conv_w"], w["conv_b"], w["w_rg_a"], w["b_rg_a"], w["w_rg_x"], w["b_rg_x"], w["lam"], tag)
    yb = matmul(yb0, w["up_b"], "nn", name=tag + "_upb")
    c = cumlogf_fwd(fl, w["b_f"], tag)
    q, k, v = (_to_heads(P[:, o:o + FOX_W]) for o in (PK_Q, PK_K, PK_V))
    qa, ka, qs = fox_operands(q, k, c[:, :FOX_HEADS].T)
    ot, lse = flash_fwd(qa, ka, v.transpose(0, 2, 1), tag)
    o2 = ot.transpose(2, 0, 1).reshape(-1, FOX_W)
    yc = matmul(o2, w["up_c"], "nn", name=tag + "_upc")
    merged = merge_fwd(P, ya, yb, yc, w["b_gate"], tag)
    x2 = matmul(merged, w["w_o"], "nn", out_dtype=F32, res=x1, name=tag + "_wo")
    hq = rms_fwd(x2, w["g_cross"], name=tag + "_xrms")
    qx = matmul(hq, w["xq"], "nn", name=tag + "_xq")
    mn = rms_fwd(mem, w["g_mem"], name=tag + "_mrms")
    kv = matmul(mn, w["xkv"], "nn", name=tag + "_xkv")
    ox = xattn_fwd(qx, kv, tag)
    x3 = matmul(ox, w["xo"], "nn", out_dtype=F32, res=x2, name=tag + "_xo")
    x4, s_ffn2 = ffn_fwd(x3, w["g_ffn2"], w["ffn2_in"], w["ffn2_out"], tag + "_ffn2")
    saved = dict(ffn1=s_ffn1, ffn2=s_ffn2, x1=x1, u=u, P=P, fl=fl, d=d, ya0=ya0, ya=ya, xc=xc, hh=hh, yb0=yb0, yb=yb,
                 qa=qa, ka=ka, qs=qs, k=k, v=v, ot=ot, lse=lse, o2=o2, yc=yc, merged=merged, x2=x2, hq=hq, qx=qx,
                 mn=mn, kv=kv, ox=ox)
    return x4, saved


def layer_bwd(dx4, mem, w, s, tag):
    g = {}
    dx3, g["g_ffn2"], g["w_ffn2_in"], g["w_ffn2_out"] = ffn_bwd(dx4, s["ffn2"], w["g_ffn2"], w["ffn2_in"], w["ffn2_out"], tag + "_ffn2")
    dox = matmul(dx3, w["xo"], "nt", name=tag + "_dox")
    g["w_xo"] = matmul(s["ox"], dx3, "tn", name=tag + "_dwxo")
    dqx, dkv = xattn_bwd(s["qx"], s["kv"], dox, tag)
    g["w_xq"] = matmul(s["hq"], dqx, "tn", name=tag + "_dwxq")
    dhq = matmul(dqx, w["xq"], "nt", name=tag + "_dhq")
    dx2, g["g_cross"] = rms_bwd(s["x2"], w["g_cross"], dhq, dx3, name=tag + "_dxrms")
    g["w_xkv"] = matmul(s["mn"], dkv, "tn", name=tag + "_dwxkv")
    dmn = matmul(dkv, w["xkv"], "nt", name=tag + "_dmn")
    _, g["g_mem"] = rms_bwd(mem, w["g_mem"], dmn, None, name=tag + "_dmrms")
    P = s["P"]
    dmerged = matmul(dx2, w["w_o"], "nt", name=tag + "_dmerged")
    g["w_o"] = matmul(s["merged"], dx2, "tn", name=tag + "_dwo")
    dya, dyb, dyc, dgl, g["b_gate"] = merge_bwd(dmerged, P, s["ya"], s["yb"], s["yc"], w["b_gate"], tag)
    dya0 = matmul(dya, w["up_a"], "nt", name=tag + "_dya0")
    g["w_up_a"] = matmul(s["ya0"], dya, "tn", name=tag + "_dwupa")
    dxa, g["w_pool"], g["pool_scale"] = pool_bwd(dya0, s["d"], w["w_pool"], w["pool_scale"], tag)
    dyb0 = matmul(dyb, w["up_b"], "nt", name=tag + "_dyb0")
    g["w_up_b"] = matmul(s["yb0"], dyb, "tn", name=tag + "_dwupb")
    (dgb, dxb, g["w_rg_a"], g["b_rg_a"], g["w_rg_x"], g["b_rg_x"], g["lru_lambda"], g["conv_w"], g["conv_b"]) = lru_bwd(
        dyb0, P, s["hh"], s["xc"], w["conv_w"], w["w_rg_a"], w["b_rg_a"], w["w_rg_x"], w["b_rg_x"], w["lam"], tag)
    do2 = matmul(dyc, w["up_c"], "nt", name=tag + "_do2")
    g["w_up_c"] = matmul(s["o2"], dyc, "tn", name=tag + "_dwupc")
    do = _to_heads(do2)
    dqt, dk, dv, dck = flash_bwd(s["qa"], s["ka"], s["qs"], s["k"].transpose(0, 2, 1), s["v"], s["ot"], do,
                                 do.transpose(0, 2, 1), s["lse"], tag)
    dq = (dqt * (FOX_HD ** -0.5)).transpose(0, 1, 3, 2).reshape(dk.shape)
    dc = jnp.pad(dck[:, 0, :].T, ((0, 0), (0, LANES - FOX_HEADS)))
    dfl, g["b_f"] = cumlogf_bwd(dc, s["fl"], w["b_f"], tag)
    dP = jnp.concatenate([dxb, dgb, dgl, dxa] + [_from_heads(t).astype(BF) for t in (dq, dk, dv)], axis=1)
    du = matmul(dP, w["wp"], "nt", out_dtype=F32, name=tag + "_du")
    du = matmul(dfl, w["wfl"], "nt", out_dtype=F32, res=du, name=tag + "_duf")
    g["wp"] = matmul(s["u"], dP, "tn", name=tag + "_dwp")
    g["wfl"] = matmul(s["u"], dfl, "tn", out_dtype=F32, name=tag + "_dwfl")
    dx1, g["g_mix"] = rms_bwd(s["x1"], w["g_mix"], du, dx2, name=tag + "_dmixrms")
    dx0, g["g_ffn1"], g["w_ffn1_in"], g["w_ffn1_out"] = ffn_bwd(dx1, s["ffn1"], w["g_ffn1"], w["ffn1_in"], w["ffn1_out"], tag + "_ffn1")
    return dx0, g


def loss_head(x, target, g_final):
    def fn(step, i, n, x, t, g):
        r = _rstd(x)
        xhat = x * r
        e = xhat * g - t
        dy = e * (1.0 / D)
        gd = dy * g
        dx = r * (gd - xhat * jnp.mean(xhat * gd, axis=-1, keepdims=True))
        loss = 0.5 * jnp.sum(jnp.mean(e * e, axis=-1, keepdims=True), axis=0, keepdims=True)
        return (dx,), (jnp.broadcast_to(loss, (1, LANES)), _colsum(dy * xhat))

    (dx,), (loss, dg) = rowwise(fn, [x, target], [g_final], [(D, F32)], [((1, LANES), F32), ((1, D), F32)], ts=512, name="loss_head")
    return loss[0, 0], dx, dg


BIG = (("w_ffn1_in", (D, 2 * DFF), 1), ("w_ffn1_out", (DFF, D), 0), ("w_in", (D, 7176), 1), ("w_up_a", (POOL_W, D), 1),
       ("conv_w", (4, D), 1), ("w_up_b", (D, D), 0), ("w_up_c", (FOX_W, D), 1), ("w_o", (D, D), 0), ("w_xq", (D, D), 0),
       ("w_xkv", (D, 2 * D), 1), ("w_xo", (D, D), 0), ("w_ffn2_in", (D, 2 * DFF), 1), ("w_ffn2_out", (DFF, D), 0))
SMALL = (("g_ffn1", (D,)), ("g_mix", (D,)), ("b_f", (FOX_HEADS,)), ("b_gate", (3 * D,)), ("w_pool", (4, LANES, LANES)),
         ("pool_scale", (POOL_W,)), ("conv_b", (D,)), ("w_rg_a", (LRU_HEADS, LANES, LANES)), ("b_rg_a", (D,)),
         ("w_rg_x", (LRU_HEADS, LANES, LANES)), ("b_rg_x", (D,)), ("lru_lambda", (D,)), ("g_cross", (D,)), ("g_mem", (D,)),
         ("g_ffn2", (D,)))
ORDER = ("g_ffn1", "w_ffn1_in", "w_ffn1_out", "g_mix", "w_in", "b_f", "b_gate", "w_pool", "pool_scale", "w_up_a", "conv_w",
         "conv_b", "w_rg_a", "b_rg_a", "w_rg_x", "b_rg_x", "lru_lambda", "w_up_b", "w_up_c", "w_o", "g_cross", "g_mem", "w_xq",
         "w_xkv", "w_xo", "g_ffn2", "w_ffn2_in", "w_ffn2_out", "g_final")

IN_SPLIT = (("xa", 0, 512), ("xb", 512, 1024), ("gb", 1536, 1024), ("q", 2560, 512), ("k", 3072, 512), ("v", 3584, 512),
            ("fl", 4096, 8), ("gl", 4104, 3072))
PACK_ORDER = ("xb", "gb", "gl", "xa", "q", "k", "v")


def _shard_shape(shape, axis):
    s = list(shape)
    s[axis] //= N_CHIPS
    return (DEPTH, *s)


def _round_up(n, m):
    return -(-n // m) * m


SMALL_SEG_ROWS = 8


def _small_rows(n_elements):
    return _round_up(-(-n_elements // D), SMALL_SEG_ROWS)


def _pack_small(tree, names, row_multiple=SMALL_SEG_ROWS, fill=0.0):
    parts = []
    for n in names:
        flat = tree[n].astype(F32).reshape(-1)
        rows = _small_rows(flat.shape[0])
        parts.append(jnp.pad(flat, (0, rows * D - flat.shape[0]), constant_values=fill).reshape(rows, D))
    total = sum(p.shape[0] for p in parts)
    if total % row_multiple:
        parts.append(jnp.full((_round_up(total, row_multiple) - total, D), fill, F32))
    return jnp.concatenate(parts, axis=0)


def _unpack_small(buf, like, names):
    out, off = {}, 0
    for n in names:
        sh = like[n].shape
        rows = _small_rows(math.prod(sh))
        out[n] = buf[off:off + rows].reshape(-1)[:math.prod(sh)].reshape(sh)
        off += rows
    return out


def _from_chip_major(g, shape, axis):
    return jnp.moveaxis(g, 1, axis + 1).reshape(DEPTH, *shape)


def _to_chip_major(full, shape, axis):
    sh = list(shape)
    sh[axis:axis + 1] = [N_CHIPS, shape[axis] // N_CHIPS]
    return jnp.moveaxis(full.reshape(DEPTH, *sh), axis + 1, 1)


def _layer_weights(full, small, l):
    w_in = full["w_in"][l]
    cols = {n: w_in[:, o:o + s] for n, o, s in IN_SPLIT}
    w = dict(ffn1_in=full["w_ffn1_in"][l], ffn1_out=full["w_ffn1_out"][l], ffn2_in=full["w_ffn2_in"][l], ffn2_out=full["w_ffn2_out"][l],
             wp=jnp.concatenate([cols[n] for n in PACK_ORDER], axis=1),
             wfl=jnp.pad(cols["fl"], ((0, 0), (0, LANES - FOX_HEADS))),
             up_a=full["w_up_a"][l], up_b=full["w_up_b"][l], up_c=full["w_up_c"][l], w_o=full["w_o"][l], xq=full["w_xq"][l],
             xkv=full["w_xkv"][l], xo=full["w_xo"][l], conv_w=full["conv_w"][l])
    for n in ("g_ffn1", "g_mix", "b_gate", "pool_scale", "conv_b", "b_rg_a", "b_rg_x", "g_cross", "g_mem", "g_ffn2"):
        w[n] = small[n][l].reshape(1, -1)
    w["lam"] = small["lru_lambda"][l].reshape(1, -1)
    w["b_f"] = jnp.pad(small["b_f"][l], (0, LANES - FOX_HEADS)).reshape(1, LANES)
    for n in ("w_pool", "w_rg_a", "w_rg_x"):
        w[n] = small[n][l]
    return w


def _unpack_w_in_grad(gwp, gwfl):
    pk = {}
    off = 0
    sizes = {n: s for n, _, s in IN_SPLIT}
    for n in PACK_ORDER:
        pk[n] = gwp[:, off:off + sizes[n]]
        off += sizes[n]
    pk["fl"] = gwfl[:, :FOX_HEADS].astype(gwp.dtype)
    return jnp.concatenate([pk[n] for n, _, _ in IN_SPLIT], axis=1)


def _place():
    x, y, c = lax.axis_index("x"), lax.axis_index("y"), lax.axis_index("c")
    others = [(1 - x, y), (x, 1 - y), (1 - x, 1 - y)]
    return x, y, c, others


def _rcopy(src, dst, send_sems, recv_sems, k, to):
    return pltpu.make_async_remote_copy(src_ref=src, dst_ref=dst, send_sem=send_sems.at[k], recv_sem=recv_sems.at[k],
                                        device_id=to, device_id_type=MESH)


def _comm_call(body, name, ins, out_shape, n_sems):
    return pl.pallas_call(
        body, name=name, in_specs=[ANY] * len(ins), out_specs=[ANY] * len(out_shape), out_shape=out_shape,
        scratch_shapes=[pltpu.SemaphoreType.DMA((n_sems,)), pltpu.SemaphoreType.DMA((n_sems,)), pltpu.SemaphoreType.DMA((n_sems,))],
    )(*ins)


def allgather_same_core(bufs):
    nb = len(bufs)

    def body(*refs):
        ins, outs = refs[:nb], refs[nb:2 * nb]
        send_sems, recv_sems, local_sems = refs[2 * nb:]
        x, y, c, others = _place()
        me = 2 * x + y
        sends = []
        for b in range(nb):
            for j, (ox, oy) in enumerate(others):
                cp = _rcopy(ins[b], outs[b].at[me], send_sems, recv_sems, 3 * b + j, (ox, oy, c))
                cp.start()
                sends.append(cp)
        for b in range(nb):
            for j, (ox, oy) in enumerate(others):
                theirs = outs[b].at[2 * ox + oy]
                _rcopy(theirs, theirs, send_sems, recv_sems, 3 * b + j, (ox, oy, c)).wait_recv()
        for cp in sends:
            cp.wait_send()

    shapes = [jax.ShapeDtypeStruct((N_CHIPS, *b.shape), b.dtype) for b in bufs]
    outs = _comm_call(body, "allgather_same_core", list(bufs), shapes, 3 * nb)
    return [_with_own_block(o, b) for o, b in zip(outs, bufs)]


def _with_own_block(gathered, mine):
    me = 2 * lax.axis_index("x") + lax.axis_index("y")
    return lax.dynamic_update_index_in_dim(gathered, mine, me, axis=0)


def gather_weights(shards):
    nw = len(shards)

    def body(*refs):
        ins, outs = refs[:nw], refs[nw:2 * nw]
        send_sems, recv_sems, _ = refs[2 * nw:]
        x, y, c, others = _place()
        me = 2 * x + y
        first, passed = [], []
        for w in range(nw):
            for j, (ox, oy) in enumerate(others):
                cp = _rcopy(ins[w].at[c], outs[w].at[c, me], send_sems, recv_sems, 3 * w + j, (ox, oy, c))
                cp.start()
                first.append(cp)
        for w in range(nw):
            for j, (ox, oy) in enumerate(others):
                theirs = outs[w].at[c, 2 * ox + oy]
                _rcopy(theirs, theirs, send_sems, recv_sems, 3 * w + j, (ox, oy, c)).wait_recv()
                cp = _rcopy(theirs, theirs, send_sems, recv_sems, 3 * (nw + w) + j, (x, y, 1 - c))
                cp.start()
                passed.append(cp)
        for w in range(nw):
            for j, (ox, oy) in enumerate(others):
                theirs = outs[w].at[1 - c, 2 * ox + oy]
                _rcopy(theirs, theirs, send_sems, recv_sems, 3 * (nw + w) + j, (x, y, 1 - c)).wait_recv()
        for cp in first + passed:
            cp.wait_send()

    shapes = [jax.ShapeDtypeStruct((DEPTH, N_CHIPS, *s.shape[1:]), s.dtype) for s in shards]
    outs = _comm_call(body, "gather_weights", list(shards), shapes, 6 * nw)
    me = 2 * lax.axis_index("x") + lax.axis_index("y")
    return [lax.dynamic_update_index_in_dim(o, s, me, axis=1) for o, s in zip(outs, shards)]


def sibling_layers(bufs):
    nb = len(bufs)

    def body(*refs):
        ins, outs = refs[:nb], refs[nb:2 * nb]
        send_sems, recv_sems, _ = refs[2 * nb:]
        x, y, c, _ = _place()
        work = []
        for b in range(nb):
            cp = _rcopy(ins[b].at[1 - c], outs[b], send_sems, recv_sems, b, (x, y, 1 - c))
            cp.start()
            work.append(cp)
        for cp in work:
            cp.wait()

    shapes = [jax.ShapeDtypeStruct(b.shape[1:], b.dtype) for b in bufs]
    return _comm_call(body, "sibling_layers", list(bufs), shapes, nb)


def chip_blocks(bufs):
    nb = len(bufs)

    def body(*refs):
        ins, outs = refs[:nb], refs[nb:2 * nb]
        send_sems, recv_sems, _ = refs[2 * nb:]
        x, y, c, others = _place()
        sends = []
        for b in range(nb):
            for j, (ox, oy) in enumerate(others):
                cp = _rcopy(ins[b].at[2 * ox + oy], outs[b].at[j], send_sems, recv_sems, 3 * b + j, (ox, oy, c))
                cp.start()
                sends.append(cp)
        for cp in sends:
            cp.wait()

    shapes = [jax.ShapeDtypeStruct((3, *b.shape[1:]), b.dtype) for b in bufs]
    return _comm_call(body, "chip_blocks", list(bufs), shapes, 3 * nb)


def sibling_other_layer(bufs):
    nb = len(bufs)

    def body(*refs):
        ins, outs = refs[:nb], refs[nb:2 * nb]
        send_sems, recv_sems, _ = refs[2 * nb:]
        x, y, c, _ = _place()
        work = []
        for b in range(nb):
            cp = _rcopy(ins[b], outs[b].at[c], send_sems, recv_sems, b, (x, y, 1 - c))
            cp.start()
            work.append(cp)
        for b, cp in enumerate(work):
            cp.wait_send()
            theirs = outs[b].at[1 - c]
            _rcopy(theirs, theirs, send_sems, recv_sems, b, (x, y, 1 - c)).wait_recv()

    shapes = [jax.ShapeDtypeStruct((DEPTH, *b.shape), b.dtype) for b in bufs]
    outs = _comm_call(body, "sibling_other_layer", list(bufs), shapes, nb)
    c = lax.axis_index("c")
    return [lax.dynamic_update_index_in_dim(o, b, c, axis=0) for o, b in zip(outs, bufs)]


ADD_BLOCK_BYTES = 2 * 2 ** 20


def add_arrays(parts, out_dtype, name):
    shape = parts[0].shape
    cols = shape[-1]
    rows = math.prod(shape[:-1])

    def fn(step, i, n, *vals):
        acc = vals[0].astype(F32)
        for v in vals[1:]:
            acc = acc + v.astype(F32)
        return (acc,), ()

    ts = rows
    for cand in (2048, 1024, 512, 256, 128, 64, 32, 16):
        if rows % cand == 0 and cand * cols * 4 <= ADD_BLOCK_BYTES:
            ts = cand
            break
    out = rowwise(fn, [p.reshape(rows, cols) for p in parts], outs=[(cols, out_dtype)], ts=ts, name=name)[0][0]
    return out.reshape(shape)


def reduce_gradients(bufs, names):
    c, me = lax.axis_index("c"), 2 * lax.axis_index("x") + lax.axis_index("y")
    theirs = sibling_layers(bufs)
    mine = [lax.dynamic_index_in_dim(b, c, axis=0, keepdims=False) for b in bufs]
    part = [add_arrays([m, t], m.dtype, "rs_add_sibling_" + n) for m, t, n in zip(mine, theirs, names)]
    recv = chip_blocks(part)
    own = [lax.dynamic_index_in_dim(p, me, axis=0, keepdims=False) for p in part]
    red = [add_arrays([o, r[0], r[1], r[2]], F32, "rs_add_chips_" + n) for o, r, n in zip(own, recv, names)]
    return sibling_other_layer(red)


_BC1 = 1.0 - ADAM_B1 ** ADAM_STEP
_BC2 = 1.0 - ADAM_B2 ** ADAM_STEP


def adamw(w, g, m, v, name):
    rows, cols = w.shape

    def fn(step, i, n, w, g, m, v):
        m2 = ADAM_B1 * m + (1.0 - ADAM_B1) * g
        v2 = ADAM_B2 * v + (1.0 - ADAM_B2) * (g * g)
        delta = -ADAM_LR * ((m2 / _BC1) / (jnp.sqrt(v2 / _BC2) + ADAM_EPS) + ADAM_WD * w)
        return (delta, m2, v2), ()

    ts = _pick(rows, (256, 128, 64, 32, 16, 8))
    return rowwise(fn, [w, g, m, v], outs=[(cols, F32)] * 3, ts=ts, name=name)[0]


def local_step(x, mem, target, full, small):
    ws = [_layer_weights(full, small, l) for l in range(DEPTH)]
    saved = []
    h = x
    for l in range(DEPTH):
        h, s = layer_fwd(h, mem, ws[l], "l")
        saved.append(s)
    loss, dh, dg_final = loss_head(h, target, small["g_final"].reshape(1, D))
    grads = [None] * DEPTH
    for l in reversed(range(DEPTH)):
        dh, grads[l] = layer_bwd(dh, mem, ws[l], saved[l], "l")
    out = {}
    for l in range(DEPTH):
        g = grads[l]
        g["w_in"] = _unpack_w_in_grad(g.pop("wp"), g.pop("wfl"))
        g["b_f"] = g["b_f"][:, :FOX_HEADS]
    for name, shape, _ in BIG:
        out[name] = jnp.stack([grads[l][name].reshape(shape) for l in range(DEPTH)])
    for name, shape in SMALL:
        out[name] = jnp.stack([grads[l][name].reshape(shape) for l in range(DEPTH)])
    out["g_final"] = dg_final.reshape(D)
    return loss, dh, out


def kernel(x, mem, g_ffn1, w_ffn1_in, w_ffn1_out, g_mix, w_in, b_f, b_gate, w_pool, pool_scale, w_up_a, conv_w, conv_b, w_rg_a, b_rg_a, w_rg_x, b_rg_x, lru_lambda, w_up_b, w_up_c, w_o, g_cross, g_mem, w_xq, w_xkv, w_xo, g_ffn2, w_ffn2_in, w_ffn2_out, g_final, loss_target, m_g_ffn1, m_w_ffn1_in, m_w_ffn1_out, m_g_mix, m_w_in, m_b_f, m_b_gate, m_w_pool, m_pool_scale, m_w_up_a, m_conv_w, m_conv_b, m_w_rg_a, m_b_rg_a, m_w_rg_x, m_b_rg_x, m_lru_lambda, m_w_up_b, m_w_up_c, m_w_o, m_g_cross, m_g_mem, m_w_xq, m_w_xkv, m_w_xo, m_g_ffn2, m_w_ffn2_in, m_w_ffn2_out, m_g_final, v_g_ffn1, v_w_ffn1_in, v_w_ffn1_out, v_g_mix, v_w_in, v_b_f, v_b_gate, v_w_pool, v_pool_scale, v_w_up_a, v_conv_w, v_conv_b, v_w_rg_a, v_b_rg_a, v_w_rg_x, v_b_rg_x, v_lru_lambda, v_w_up_b, v_w_up_c, v_w_o, v_g_cross, v_g_mem, v_w_xq, v_w_xkv, v_w_xo, v_g_ffn2, v_w_ffn2_in, v_w_ffn2_out, v_g_final):
    args = dict(locals())
    weights = {n: args[n] for n in ORDER}
    m_in = {n: args["m_" + n] for n in ORDER}
    v_in = {n: args["v_" + n] for n in ORDER}
    big_names = [n for n, _, _ in BIG]
    small_names = [n for n, _ in SMALL] + ["g_final"]

    shards = []
    for n in big_names:
        shards += list(_split3(weights[n])) if n == "conv_w" else [weights[n].astype(BF)]
    gathered = iter(gather_weights(shards))
    full = {}
    for n, shape, axis in BIG:
        g = next(gathered)
        if n == "conv_w":
            g = g.astype(F32) + next(gathered).astype(F32) + next(gathered).astype(F32)
        full[n] = _from_chip_major(g, shape, axis)

    small = {n: weights[n] for n in small_names}
    loss, grad_x, contrib = local_step(x[0], mem[0], loss_target[0], full, small)
    loss = lax.psum(loss, ("x", "y", "c"))

    bufs = [_to_chip_major(contrib[n].astype(BF), shape, axis) for n, shape, axis in BIG]
    small_buf = _pack_small(contrib, small_names, DEPTH * N_CHIPS * SMALL_SEG_ROWS).reshape(DEPTH, N_CHIPS, -1, D)
    red = reduce_gradients(bufs + [small_buf], big_names + ["small"])

    grads = dict(zip(big_names, red[:-1]))
    red_small = allgather_same_core([red[-1]])[0].transpose(1, 0, 2, 3).reshape(-1, D)
    grads.update(_unpack_small(red_small, weights, small_names))

    delta, new_m, new_v = {}, {}, {}
    for name in big_names:
        sh = weights[name].shape
        two_d = (-1, sh[-1])
        d, m2, v2 = adamw(weights[name].reshape(two_d), grads[name].reshape(two_d), m_in[name].reshape(two_d),
                          v_in[name].reshape(two_d), "adamw_" + name)
        delta[name], new_m[name], new_v[name] = d.reshape(sh), m2.reshape(sh), v2.reshape(sh)
    outs = adamw(_pack_small(weights, small_names), _pack_small(grads, small_names), _pack_small(m_in, small_names),
                 _pack_small(v_in, small_names, fill=1.0), "adamw_small")
    for t, res in zip((delta, new_m, new_v), outs):
        t.update(_unpack_small(res, weights, small_names))

    return (loss, grad_x[None], *[grads[n] for n in ORDER], *[delta[n] for n in ORDER],
            *[new_m[n] for n in ORDER], *[new_v[n] for n in ORDER])
```

```python
import math

import jax
import jax.numpy as jnp
from jax import lax
from jax.experimental import pallas as pl
from jax.experimental.pallas import tpu as pltpu

F32 = jnp.float32
BF = jnp.bfloat16

D = 1024
DFF = 2816
DEPTH = 2
POOL_W = 512
POOL_WINDOWS = (2, 4, 8, 16)
LRU_HEADS = 8
LRU_C = 8.0
FOX_HEADS = 8
FOX_HD = 64
FOX_W = 512
X_HEADS = 4
X_HD = 256
EPS = 1e-6
LANES = 128
N_CHIPS = 4

ADAM_LR, ADAM_B1, ADAM_B2, ADAM_EPS, ADAM_WD, ADAM_STEP = 0.001, 0.9, 0.999, 1e-08, 0.01, 10

VMEM_LIMIT_BYTES = 56 * 2 ** 20

PK_XB, PK_GB, PK_GL, PK_XA, PK_Q, PK_K, PK_V = 0, 1024, 2048, 5120, 5632, 6144, 6656
PK_W = 7168

MESH = pl.DeviceIdType.MESH
ANY = pl.BlockSpec(memory_space=pl.ANY)


def _cparams(ngrid):
    return pltpu.CompilerParams(dimension_semantics=("arbitrary",) * ngrid, vmem_limit_bytes=VMEM_LIMIT_BYTES)


def _pick(n, cands):
    for c in cands:
        if n % c == 0:
            return c
    return n


def _iota(shape, dim):
    return lax.broadcasted_iota(jnp.int32, shape, dim)


def _sigmoid(x):
    return 1.0 / (1.0 + jnp.exp(-x))


def _softplus(z):
    return jnp.maximum(z, 0.0) + jnp.log1p(jnp.exp(-jnp.abs(z)))


def _expm1(x):
    small = jnp.abs(x) < 0.25
    xs = jnp.where(small, x, 0.0)
    poly = xs * (1.0 + xs * (1 / 2 + xs * (1 / 6 + xs * (1 / 24 + xs * (1 / 120 + xs * (1 / 720 + xs * (1 / 5040 + xs * (1 / 40320))))))))
    return jnp.where(small, poly, jnp.exp(x) - 1.0)


_GELU_K = math.sqrt(2.0 / math.pi)


def _gelu_and_grad(x):
    inner = _GELU_K * (x + 0.044715 * x * x * x)
    t = jnp.tanh(inner)
    g = 0.5 * x * (1.0 + t)
    dg = 0.5 * (1.0 + t) + 0.5 * x * (1.0 - t * t) * _GELU_K * (1.0 + 3 * 0.044715 * x * x)
    return g, dg


def _dot(a, b, dims):
    return lax.dot_general(a.astype(BF), b.astype(BF), (dims, ((), ())), preferred_element_type=F32)


def _dot_nn(a, b):
    return _dot(a, b, ((1,), (0,)))


def _dot_nt(a, b):
    return _dot(a, b, ((1,), (1,)))


def _dot_tn(a, b):
    return _dot(a, b, ((0,), (0,)))


def _colsum(x):
    return jnp.sum(x, axis=0, keepdims=True)


_TM = (1024, 1408, 512, 256, 128)
_TN = (1408, 1024, 512, 384, 256, 128)
_TK = (2432, 1408, 1024, 512, 256, 128)


def matmul(a, b, mode, *, out_dtype=None, alpha=1.0, res=None, name):
    out_dtype = BF if out_dtype is None else out_dtype
    if mode == "nn":
        (M, K), N = a.shape, b.shape[1]
    elif mode == "nt":
        (M, K), N = a.shape, b.shape[0]
    else:
        (K, M), N = a.shape, b.shape[1]
    tm, tn, tk = _pick(M, _TM), _pick(N, _TN), _pick(K, _TK)
    nk = K // tk
    dims = {"nn": ((1,), (0,)), "nt": ((1,), (1,)), "tn": ((0,), (0,))}[mode]
    a_spec = pl.BlockSpec((tk, tm), lambda i, j, k: (k, i)) if mode == "tn" else pl.BlockSpec((tm, tk), lambda i, j, k: (i, k))
    b_spec = pl.BlockSpec((tn, tk), lambda i, j, k: (j, k)) if mode == "nt" else pl.BlockSpec((tk, tn), lambda i, j, k: (k, j))
    o_spec = pl.BlockSpec((tm, tn), lambda i, j, k: (i, j))
    has_res = res is not None

    def body(*refs):
        a_ref, b_ref = refs[0], refs[1]
        r_ref = refs[2] if has_res else None
        o_ref, acc = refs[2 + has_res], refs[3 + has_res]
        k = pl.program_id(2)

        @pl.when(k == 0)
        def _():
            acc[...] = jnp.zeros_like(acc)

        acc[...] += _dot(a_ref[...], b_ref[...], dims)

        @pl.when(k == nk - 1)
        def _():
            r = acc[...]
            if alpha != 1.0:
                r = r * alpha
            if has_res:
                r = r + r_ref[...].astype(F32)
            o_ref[...] = r.astype(o_ref.dtype)

    ins = [a, b] + ([res] if has_res else [])
    specs = [a_spec, b_spec] + ([o_spec] if has_res else [])
    return pl.pallas_call(
        body, name=name, grid=(M // tm, N // tn, nk), in_specs=specs, out_specs=o_spec,
        out_shape=jax.ShapeDtypeStruct((M, N), out_dtype), scratch_shapes=[pltpu.VMEM((tm, tn), F32)],
        compiler_params=_cparams(3))(*ins)


class Rows:
    def __init__(self, arr, w=None, cb=0, halo=None, h=8):
        self.arr, self.w, self.cb, self.halo, self.h = arr, (arr.shape[1] if w is None else w), cb, halo, h


def rowwise(fn, rows, consts=(), outs=(), accs=(), scratch=(), *, ts, name, reverse=False):
    rows = [r if isinstance(r, Rows) else Rows(r) for r in rows]
    S = rows[0].arr.shape[0]
    n = S // ts
    tile = (lambda s: n - 1 - s) if reverse else (lambda s: s)

    def row_spec(r):
        if r.halo is None:
            return pl.BlockSpec((ts, r.w), lambda s: (tile(s), r.cb))
        per, last = ts // r.h, S // r.h - 1
        if r.halo == "prev":
            return pl.BlockSpec((r.h, r.w), lambda s: (jnp.maximum(tile(s) * per - 1, 0), r.cb))
        return pl.BlockSpec((r.h, r.w), lambda s: (jnp.minimum((tile(s) + 1) * per, last), r.cb))

    def whole(shape):
        nd = len(shape)
        return pl.BlockSpec(tuple(shape), lambda s: (0,) * nd)

    nr, nc, no, na = len(rows), len(consts), len(outs), len(accs)

    def body(*refs):
        in_refs, out_refs = refs[:nr + nc], refs[nr + nc:nr + nc + no]
        acc_refs, scr = refs[nr + nc + no:nr + nc + no + na], refs[nr + nc + no + na:]
        step = pl.program_id(0)
        o, inc = fn(step, tile(step), n, *[r[...] for r in in_refs], *scr)
        for ref, val in zip(out_refs, o):
            ref[...] = val.astype(ref.dtype)
        if na:
            @pl.when(step == 0)
            def _():
                for ref in acc_refs:
                    ref[...] = jnp.zeros_like(ref)
            for ref, val in zip(acc_refs, inc):
                ref[...] += val.astype(ref.dtype)

    out_shape = [jax.ShapeDtypeStruct((S, w), dt) for w, dt in outs] + [jax.ShapeDtypeStruct(tuple(s), dt) for s, dt in accs]
    out_specs = [pl.BlockSpec((ts, w), lambda s: (tile(s), 0)) for w, _ in outs] + [whole(s) for s, _ in accs]
    res = pl.pallas_call(
        body, name=name, grid=(n,),
        in_specs=[row_spec(r) for r in rows] + [whole(c.shape) for c in consts],
        out_specs=out_specs, out_shape=out_shape, scratch_shapes=list(scratch),
        compiler_params=_cparams(1))(*[r.arr for r in rows], *consts)
    return res[:no], res[no:]


def _rstd(x):
    return lax.rsqrt(jnp.mean(x * x, axis=-1, keepdims=True) + EPS)


def rms_fwd(x, g, *, name, ts=512):
    def fn(step, i, n, x, g):
        return ((x * _rstd(x)) * g,), ()
    return rowwise(fn, [x], [g], [(D, BF)], ts=min(ts, x.shape[0]), name=name)[0][0]


def rms_bwd(x, g, dh, dres, *, name, ts=512):
    has_res = dres is not None

    def fn(step, i, n, x, dh, *rest):
        g = rest[-1]
        r = _rstd(x)
        xhat = x * r
        dh = dh.astype(F32)
        gd = dh * g
        dx = r * (gd - xhat * jnp.mean(xhat * gd, axis=-1, keepdims=True))
        if has_res:
            dx = dx + rest[0]
        return (dx,), (_colsum(dh * xhat),)

    rows = [x, dh] + ([dres] if has_res else [])
    o, a = rowwise(fn, rows, [g], [(D, F32)], [((1, D), F32)], ts=min(ts, x.shape[0]), name=name)
    return o[0], a[0]


def ffn_fwd(x, g, w_in, w_out, tag):
    h = rms_fwd(x, g, name=tag + "_rms")
    ab = matmul(h, w_in, "nn", name=tag + "_in")

    def act_fn(step, i, n, ab):
        a, b = ab[:, :DFF].astype(F32), ab[:, DFF:].astype(F32)
        return ((a * _sigmoid(a)) * b,), ()

    act = rowwise(act_fn, [ab], outs=[(DFF, BF)], ts=256, name=tag + "_act")[0][0]
    y = matmul(act, w_out, "nn", out_dtype=F32, alpha=0.5, res=x, name=tag + "_out")
    return y, (x, h, ab, act)


def ffn_bwd(dy, saved, g, w_in, w_out, tag):
    x, h, ab, act = saved
    dact = matmul(dy, w_out, "nt", alpha=0.5, name=tag + "_dact")
    dw_out = matmul(act, dy, "tn", alpha=0.5, name=tag + "_dwout")

    def dab_fn(step, i, n, ab, dact):
        a, b, dact = ab[:, :DFF].astype(F32), ab[:, DFF:].astype(F32), dact.astype(F32)
        s = _sigmoid(a)
        da = dact * b * (s * (1.0 + a * (1.0 - s)))
        db = dact * (a * s)
        return (jnp.concatenate([da, db], axis=1),), ()

    dab = rowwise(dab_fn, [ab, dact], outs=[(2 * DFF, BF)], ts=256, name=tag + "_dab")[0][0]
    dw_in = matmul(h, dab, "tn", name=tag + "_dwin")
    dh = matmul(dab, w_in, "nt", name=tag + "_dh")
    dx, dg = rms_bwd(x, g, dh, dy, name=tag + "_drms")
    return dx, dg, dw_in, dw_out


POOL_TS = 256
POOL_HALO = 16


def pool_fwd(P, w_pool, pool_scale, tag):
    ts = POOL_TS

    def fn(step, i, n, xa, halo, w, scale):
        xa = xa.astype(F32)
        halo = halo.astype(F32) * jnp.where(i > 0, 1.0, 0.0)
        ext = jnp.concatenate([halo, xa], axis=0)
        pos = (i * ts + 1 + _iota((ts, LANES), 0)).astype(F32)
        ds, ys = [], []
        for gi, win in enumerate(POOL_WINDOWS):
            e = ext[:, gi * LANES:(gi + 1) * LANES]
            sh = 1
            while sh < win:
                e = e + pltpu.roll(e, sh, 0)
                sh *= 2
            mean = e[POOL_HALO:] / jnp.minimum(pos, float(win))
            d = mean - xa[:, gi * LANES:(gi + 1) * LANES]
            ds.append(d)
            ys.append(_dot_nn(d, w[gi]))
        d = jnp.concatenate(ds, axis=1)
        return (d, jnp.concatenate(ys, axis=1) * scale), ()

    xa = Rows(P, POOL_W, PK_XA // POOL_W)
    xa_prev = Rows(P, POOL_W, PK_XA // POOL_W, "prev", POOL_HALO)
    (d, ya0), _ = rowwise(fn, [xa, xa_prev], [w_pool, pool_scale], [(POOL_W, BF), (POOL_W, BF)], ts=ts, name=tag + "_pool")
    return d, ya0


def pool_bwd(dya0, d, w_pool, pool_scale, tag):
    ts = POOL_TS
    L = ts + POOL_HALO

    def fn(step, i, n, dya0, dya0_next, d, w, scale):
        d = d.astype(F32)
        dz = jnp.concatenate([dya0.astype(F32), dya0_next.astype(F32) * jnp.where(i < n - 1, 1.0, 0.0)], axis=0) * scale
        pos = (i * ts + 1 + _iota((L, LANES), 0)).astype(F32)
        dxa, dws, dsc = [], [], []
        for gi, win in enumerate(POOL_WINDOWS):
            sl = slice(gi * LANES, (gi + 1) * LANES)
            dzg = dz[:, sl]
            dd = _dot_nt(dzg, w[gi])
            e = dd / jnp.minimum(pos, float(win))
            sh = 1
            while sh < win:
                e = e + pltpu.roll(e, L - sh, 0)
                sh *= 2
            dxa.append(e[:ts] - dd[:ts])
            dws.append(_dot_tn(d[:, sl], dzg[:ts])[None])
            z = _dot_nn(d[:, sl], w[gi])
            dsc.append(_colsum(dya0[:, sl].astype(F32) * z))
        return (jnp.concatenate(dxa, axis=1),), (jnp.concatenate(dws, axis=0), jnp.concatenate(dsc, axis=1))

    (dxa,), (dw_pool, dscale) = rowwise(
        fn, [dya0, Rows(dya0, halo="next", h=POOL_HALO), d], [w_pool, pool_scale],
        [(POOL_W, BF)], [((4, LANES, LANES), F32), ((1, POOL_W), F32)], ts=ts, name=tag + "_dpool")
    return dxa, dw_pool, dscale


LRU_R = 512


def _scan_fwd(A, U):
    R = A.shape[0]
    row = _iota(A.shape, 0)
    d = 1
    while d < R:
        m = row >= d
        A_sh = jnp.where(m, pltpu.roll(A, d, 0), 1.0)
        U_sh = jnp.where(m, pltpu.roll(U, d, 0), 0.0)
        U = A * U_sh + U
        A = A * A_sh
        d *= 2
    return A, U


def _scan_bwd(B, X):
    R = B.shape[0]
    row = _iota(B.shape, 0)
    d = 1
    while d < R:
        m = row < R - d
        B_sh = jnp.where(m, pltpu.roll(B, R - d, 0), 1.0)
        X_sh = jnp.where(m, pltpu.roll(X, R - d, 0), 0.0)
        X = X + B * X_sh
        B = B * B_sh
        d *= 2
    return X


def _lru_gates(xc, wa, ba, wx, bx, lam):
    r = _sigmoid(_dot_nn(xc, wa) + ba)
    ig = _sigmoid(_dot_nn(xc, wx) + bx)
    sp = _softplus(-lam)
    log_a = -LRU_C * r * sp
    a = jnp.exp(log_a)
    mult = jnp.sqrt(-_expm1(2.0 * log_a))
    return r, ig, sp, a, mult


def _lru_specs(S, R, reverse):
    nch = S // R
    ch = (lambda j: nch - 1 - j) if reverse else (lambda j: j)
    per = R // 8

    def col(off):
        return pl.BlockSpec((R, LANES), lambda h, j: (ch(j), off + h))

    def prev(off):
        return pl.BlockSpec((8, LANES), lambda h, j: (jnp.maximum(ch(j) * per - 1, 0), off + h))

    vec = pl.BlockSpec((1, LANES), lambda h, j: (0, h))
    cw = pl.BlockSpec((4, LANES), lambda h, j: (0, h))
    wsq = pl.BlockSpec((None, LANES, LANES), lambda h, j: (h, 0, 0))
    return nch, ch, col, prev, vec, cw, wsq


def lru_fwd(P, conv_w, conv_b, w_a, b_a, w_x, b_x, lam, tag):
    S = P.shape[0]
    R = min(LRU_R, S)
    nch, ch, col, prev, vec, cw_spec, wsq = _lru_specs(S, R, False)

    def body(xb_ref, halo_ref, gb_ref, cw_ref, cb_ref, wa_ref, ba_ref, wx_ref, bx_ref, lam_ref, xc_ref, h_ref, yb_ref, carry):
        j = pl.program_id(1)
        xb = xb_ref[...].astype(F32)
        halo = halo_ref[...].astype(F32) * jnp.where(j > 0, 1.0, 0.0)
        ext = jnp.concatenate([halo, xb], axis=0)
        cw = cw_ref[...]
        xc = cb_ref[...]
        for k in range(4):
            e = ext if k == 3 else pltpu.roll(ext, 3 - k, 0)
            xc = xc + e[8:] * cw[k:k + 1]
        r, ig, sp, a, mult = _lru_gates(xc, wa_ref[...], ba_ref[...], wx_ref[...], bx_ref[...], lam_ref[...])
        u = mult * (ig * xc)
        cum_a, hloc = _scan_fwd(a, u)

        @pl.when(j == 0)
        def _():
            carry[...] = jnp.zeros_like(carry)

        hfull = hloc + cum_a * carry[0:1]
        carry[...] = jnp.broadcast_to(hfull[R - 1:R], carry.shape)
        gel, _ = _gelu_and_grad(gb_ref[...].astype(F32))
        xc_ref[...] = xc
        h_ref[...] = hfull
        yb_ref[...] = (hfull * gel).astype(yb_ref.dtype)

    out = pl.BlockSpec((R, LANES), lambda h, j: (j, h))
    return pl.pallas_call(
        body, name=tag + "_lru", grid=(LRU_HEADS, nch),
        in_specs=[col(PK_XB // LANES), prev(PK_XB // LANES), col(PK_GB // LANES), cw_spec, vec, wsq, vec, wsq, vec, vec],
        out_specs=[out, out, out],
        out_shape=[jax.ShapeDtypeStruct((S, D), F32), jax.ShapeDtypeStruct((S, D), F32), jax.ShapeDtypeStruct((S, D), BF)],
        scratch_shapes=[pltpu.VMEM((8, LANES), F32)], compiler_params=_cparams(2),
    )(P, P, P, conv_w, conv_b, w_a, b_a, w_x, b_x, lam)


def lru_bwd(dyb0, P, hh, xc, conv_w, w_a, b_a, w_x, b_x, lam, tag):
    S = P.shape[0]
    R = min(LRU_R, S)
    nch, ch, col, prev, vec, cw_spec, wsq = _lru_specs(S, R, True)

    def body(dyb_ref, gb_ref, h_ref, hprev_ref, xc_ref, xb_ref, xbprev_ref, cw_ref, wa_ref, ba_ref, wx_ref, bx_ref, lam_ref,
             dgb_ref, dxb_ref, dwa_ref, dba_ref, dwx_ref, dbx_ref, dlam_ref, dcw_ref, dcb_ref, gcarry, dxc_head):
        j = pl.program_id(1)
        jj = nch - 1 - j
        has_prev = jnp.where(jj > 0, 1.0, 0.0)
        row = _iota((R, LANES), 0)
        xc, hv, lam = xc_ref[...], h_ref[...], lam_ref[...]
        wa, wx = wa_ref[...], wx_ref[...]
        r, ig, sp, a, mult = _lru_gates(xc, wa, ba_ref[...], wx, bx_ref[...], lam)
        dyb = dyb_ref[...].astype(F32)
        gel, dgel = _gelu_and_grad(gb_ref[...].astype(F32))
        dgb_ref[...] = (dyb * hv * dgel).astype(dgb_ref.dtype)

        @pl.when(j == 0)
        def _():
            gcarry[...] = jnp.zeros_like(gcarry)
            dxc_head[...] = jnp.zeros_like(dxc_head)
            for ref in (dwa_ref, dba_ref, dwx_ref, dbx_ref, dlam_ref, dcw_ref, dcb_ref):
                ref[...] = jnp.zeros_like(ref)

        B = jnp.where(row < R - 1, pltpu.roll(a, R - 1, 0), 0.0)
        X = dyb * gel + jnp.where(row == R - 1, gcarry[0:1], 0.0)
        G = _scan_bwd(B, X)
        gcarry[...] = jnp.broadcast_to(a[0:1] * G[0:1], gcarry.shape)
        hprev = jnp.where(row == 0, hprev_ref[...][7:8] * has_prev, pltpu.roll(hv, 1, 0))
        da = G * hprev
        dmult = G * ig * xc
        dig = G * mult * xc
        dxc = G * mult * ig
        dlog_a = da * a - dmult * (a * a) / mult
        dzr = dlog_a * (-LRU_C * sp) * (r * (1.0 - r))
        dzi = dig * (ig * (1.0 - ig))
        dxc = dxc + _dot_nt(dzr, wa) + _dot_nt(dzi, wx)
        dwa_ref[...] += _dot_tn(xc, dzr)
        dwx_ref[...] += _dot_tn(xc, dzi)
        dba_ref[...] += _colsum(dzr)
        dbx_ref[...] += _colsum(dzi)
        dlam_ref[...] += _colsum(dlog_a * (-LRU_C * r)) * (-_sigmoid(-lam))
        dcb_ref[...] += _colsum(dxc)
        cw = cw_ref[...]
        ext = jnp.concatenate([dxc, dxc_head[...]], axis=0)
        dxb = dxc * cw[3:4]
        for k in range(3):
            dxb = dxb + pltpu.roll(ext, R + 8 - (3 - k), 0)[:R] * cw[k:k + 1]
        dxb_ref[...] = dxb.astype(dxb_ref.dtype)
        dxc_head[...] = dxc[0:8]
        extx = jnp.concatenate([xbprev_ref[...].astype(F32) * has_prev, xb_ref[...].astype(F32)], axis=0)
        incs = []
        for k in range(4):
            e = extx if k == 3 else pltpu.roll(extx, 3 - k, 0)
            incs.append(_colsum(dxc * e[8:]))
        dcw_ref[...] += jnp.concatenate(incs, axis=0)

    plain = pl.BlockSpec((R, LANES), lambda h, j: (ch(j), h))
    plain_prev = pl.BlockSpec((8, LANES), lambda h, j: (jnp.maximum(ch(j) * (R // 8) - 1, 0), h))
    return pl.pallas_call(
        body, name=tag + "_dlru", grid=(LRU_HEADS, nch),
        in_specs=[plain, col(PK_GB // LANES), plain, plain_prev, plain, col(PK_XB // LANES), prev(PK_XB // LANES),
                  cw_spec, wsq, vec, wsq, vec, vec],
        out_specs=[plain, plain, wsq, vec, wsq, vec, vec, cw_spec, vec],
        out_shape=[jax.ShapeDtypeStruct((S, D), BF), jax.ShapeDtypeStruct((S, D), BF),
                   jax.ShapeDtypeStruct((LRU_HEADS, LANES, LANES), F32), jax.ShapeDtypeStruct((1, D), F32),
                   jax.ShapeDtypeStruct((LRU_HEADS, LANES, LANES), F32), jax.ShapeDtypeStruct((1, D), F32),
                   jax.ShapeDtypeStruct((1, D), F32), jax.ShapeDtypeStruct((4, D), F32), jax.ShapeDtypeStruct((1, D), F32)],
        scratch_shapes=[pltpu.VMEM((8, LANES), F32), pltpu.VMEM((8, LANES), F32)], compiler_params=_cparams(2),
    )(dyb0, P, hh, hh, xc, P, P, conv_w, w_a, b_a, w_x, b_x, lam)


CUM_TS = 512
FLASH_T = 512
FLASH_G = 8
NEG = -1e30


def cumlogf_fwd(fl, b_f, tag):
    ts = min(CUM_TS, fl.shape[0])

    def fn(step, i, n, fl, bf, carry):
        x = -_softplus(-(fl + bf))
        row = _iota(x.shape, 0)
        d = 1
        while d < ts:
            x = x + jnp.where(row >= d, pltpu.roll(x, d, 0), 0.0)
            d *= 2

        @pl.when(step == 0)
        def _():
            carry[...] = jnp.zeros_like(carry)

        c = x + carry[0:1]
        carry[...] = jnp.broadcast_to(c[ts - 1:ts], carry.shape)
        return (c,), ()

    return rowwise(fn, [fl], [b_f], [(LANES, F32)], scratch=[pltpu.VMEM((8, LANES), F32)], ts=ts, name=tag + "_cum")[0][0]


def cumlogf_bwd(dc, fl, b_f, tag):
    ts = min(CUM_TS, fl.shape[0])

    def fn(step, i, n, dc, fl, bf, carry):
        row = _iota(dc.shape, 0)
        x = dc
        d = 1
        while d < ts:
            x = x + jnp.where(row < ts - d, pltpu.roll(x, ts - d, 0), 0.0)
            d *= 2

        @pl.when(step == 0)
        def _():
            carry[...] = jnp.zeros_like(carry)

        g = x + carry[0:1]
        carry[...] = jnp.broadcast_to(g[0:1], carry.shape)
        dfl = g * _sigmoid(-(fl + bf))
        return (dfl,), (_colsum(dfl),)

    (dfl,), (db,) = rowwise(fn, [dc, fl], [b_f], [(LANES, F32)], [((1, LANES), F32)], scratch=[pltpu.VMEM((8, LANES), F32)],
                            ts=ts, name=tag + "_dcum", reverse=True)
    return dfl, db


FOX_KA = 80


def _split3(c):
    c = c.astype(F32)
    hi = lax.reduce_precision(c, 8, 7)
    r = c - hi
    mid = lax.reduce_precision(r, 8, 7)
    return hi.astype(BF), mid.astype(BF), (r - mid).astype(BF)


def fox_operands(q, k, ch):
    H, S, hd = q.shape
    qs = q * (FOX_HD ** -0.5)
    pieces = [p.astype(F32) for p in _split3(ch)]
    pad = FOX_KA - hd

    def unit(i):
        return (jnp.arange(FOX_KA) == hd + i).astype(F32)

    qa = jnp.pad(qs.transpose(0, 2, 1).astype(F32), ((0, 0), (0, pad), (0, 0)))
    ka = jnp.pad(k.astype(F32), ((0, 0), (0, 0), (0, pad)))
    for i, p in enumerate(pieces):
        qa = qa + p[:, None, :] * unit(i)[None, :, None] + unit(3 + i)[None, :, None]
        ka = ka + unit(i)[None, None, :] - p[:, :, None] * unit(3 + i)[None, None, :]
    return qa.astype(BF), ka.astype(BF), qs


def _causal(s):
    return jnp.where(_iota(s.shape, 0) <= _iota(s.shape, 1), s, NEG)


def flash_fwd(qa, ka, vt, tag):
    H, hd, S = vt.shape
    T = min(FLASH_T, S)
    nb = S // T

    pairs = [(qi, ki) for qi in range(nb) for ki in range(qi + 1)]
    q_tab = jnp.asarray([p[0] for p in pairs], jnp.int32)
    k_tab = jnp.asarray([p[1] for p in pairs], jnp.int32)

    def body(q_tab_ref, k_tab_ref, ka_ref, qa_ref, vt_ref, o_ref, lse_ref, m_s, l_s, acc):
        qi, ki = q_tab_ref[pl.program_id(1)], k_tab_ref[pl.program_id(1)]

        @pl.when(ki == 0)
        def _():
            m_s[...] = jnp.full_like(m_s, NEG)
            l_s[...] = jnp.zeros_like(l_s)
            acc[...] = jnp.zeros_like(acc)

        def step(diagonal):
            for g in range(FLASH_G):
                s = _dot_nn(ka_ref[g], qa_ref[g])
                if diagonal:
                    s = _causal(s)
                m_new = jnp.maximum(m_s[g], jnp.max(s, axis=0, keepdims=True))
                alpha = jnp.exp(m_s[g] - m_new)
                p = jnp.exp(s - m_new)
                l_s[g] = alpha * l_s[g] + jnp.sum(p, axis=0, keepdims=True)
                p_hi = p.astype(BF)
                p_lo = p - p_hi.astype(F32)
                v = vt_ref[g]
                acc[g] = alpha * acc[g] + (_dot_nn(v, p_hi) + _dot_nn(v, p_lo))
                m_s[g] = m_new

        @pl.when(ki < qi)
        def _():
            step(False)

        @pl.when(ki == qi)
        def _():
            step(True)
            o_ref[...] = acc[...] / l_s[...]
            lse_ref[...] = m_s[...] + jnp.log(l_s[...])

    G = FLASH_G
    grid_spec = pltpu.PrefetchScalarGridSpec(
        num_scalar_prefetch=2, grid=(H // G, len(pairs)),
        in_specs=[pl.BlockSpec((G, T, FOX_KA), lambda h, p, qt, kt: (h, kt[p], 0)),
                  pl.BlockSpec((G, FOX_KA, T), lambda h, p, qt, kt: (h, 0, qt[p])),
                  pl.BlockSpec((G, hd, T), lambda h, p, qt, kt: (h, 0, kt[p]))],
        out_specs=[pl.BlockSpec((G, hd, T), lambda h, p, qt, kt: (h, 0, qt[p])),
                   pl.BlockSpec((G, 1, T), lambda h, p, qt, kt: (h, 0, qt[p]))],
        scratch_shapes=[pltpu.VMEM((G, 1, T), F32), pltpu.VMEM((G, 1, T), F32), pltpu.VMEM((G, hd, T), F32)])
    return pl.pallas_call(
        body, name=tag + "_flash", grid_spec=grid_spec,
        out_shape=[jax.ShapeDtypeStruct((H, hd, S), F32), jax.ShapeDtypeStruct((H, 1, S), F32)],
        compiler_params=_cparams(2))(q_tab, k_tab, ka, qa, vt)


def flash_bwd(qa, ka, qs, kt, v, ot, do, dot_, lse, tag):
    H, S, hd = v.shape
    T = min(FLASH_T, S)
    nb = S // T

    pairs = [(qi, ki) for ki in range(nb) for qi in range(ki, nb)]
    q_tab = jnp.asarray([p[0] for p in pairs], jnp.int32)
    k_tab = jnp.asarray([p[1] for p in pairs], jnp.int32)

    def body(q_tab_ref, k_tab_ref, ka_ref, qa_ref, qs_ref, kt_ref, v_ref, ot_ref, do_ref, dot_ref, lse_ref,
             dqt_ref, dk_ref, dv_ref, dc_ref, dk_acc, dv_acc, dc_acc):
        qi, ki = q_tab_ref[pl.program_id(1)], k_tab_ref[pl.program_id(1)]

        @pl.when(pl.program_id(1) == 0)
        def _():
            dqt_ref[...] = jnp.zeros_like(dqt_ref)

        @pl.when(qi == ki)
        def _():
            dk_acc[...] = jnp.zeros_like(dk_acc)
            dv_acc[...] = jnp.zeros_like(dv_acc)
            dc_acc[...] = jnp.zeros_like(dc_acc)

        def step(diagonal):
            for g in range(FLASH_G):
                s = _dot_nn(ka_ref[g], qa_ref[g])
                if diagonal:
                    s = _causal(s)
                p = jnp.exp(s - lse_ref[g])
                dot_v = dot_ref[g]
                dp = _dot_nn(v_ref[g], dot_v)
                delta = jnp.sum(dot_v.astype(F32) * ot_ref[g], axis=0, keepdims=True)
                ds = p * (dp - delta)
                part = ds[:, 0:LANES]
                for j in range(1, T // LANES):
                    part = part + ds[:, j * LANES:(j + 1) * LANES]
                dc_acc[g] += part
                dsb = ds.astype(BF)
                dv_acc[g] += _dot_nn(p, do_ref[g])
                dk_acc[g] += _dot_nn(dsb, qs_ref[g])
                dqt_ref[g, qi] += _dot_nn(kt_ref[g], dsb)

        @pl.when(qi > ki)
        def _():
            step(False)

        @pl.when(qi == ki)
        def _():
            step(True)

        @pl.when(qi == nb - 1)
        def _():
            dk_ref[...] = dk_acc[...]
            dv_ref[...] = dv_acc[...]
            for g in range(FLASH_G):
                dc_ref[g] = -jnp.sum(dc_acc[g].T, axis=0, keepdims=True)

    qblk = lambda h, p, qt, kt_: (h, qt[p], 0)
    qblk_t = lambda h, p, qt, kt_: (h, 0, qt[p])
    kblk = lambda h, p, qt, kt_: (h, kt_[p], 0)
    kblk_t = lambda h, p, qt, kt_: (h, 0, kt_[p])
    G = FLASH_G
    grid_spec = pltpu.PrefetchScalarGridSpec(
        num_scalar_prefetch=2, grid=(H // G, len(pairs)),
        in_specs=[pl.BlockSpec((G, T, FOX_KA), kblk), pl.BlockSpec((G, FOX_KA, T), qblk_t),
                  pl.BlockSpec((G, T, hd), qblk), pl.BlockSpec((G, hd, T), kblk_t), pl.BlockSpec((G, T, hd), kblk),
                  pl.BlockSpec((G, hd, T), qblk_t), pl.BlockSpec((G, T, hd), qblk), pl.BlockSpec((G, hd, T), qblk_t),
                  pl.BlockSpec((G, 1, T), qblk_t)],
        out_specs=[pl.BlockSpec((G, nb, hd, T), lambda h, p, qt, kt_: (h, 0, 0, 0)), pl.BlockSpec((G, T, hd), kblk),
                   pl.BlockSpec((G, T, hd), kblk), pl.BlockSpec((G, 1, T), kblk_t)],
        scratch_shapes=[pltpu.VMEM((G, T, hd), F32), pltpu.VMEM((G, T, hd), F32), pltpu.VMEM((G, T, LANES), F32)])
    return pl.pallas_call(
        body, name=tag + "_dflash", grid_spec=grid_spec,
        out_shape=[jax.ShapeDtypeStruct((H, nb, hd, T), F32), jax.ShapeDtypeStruct((H, S, hd), F32),
                   jax.ShapeDtypeStruct((H, S, hd), F32), jax.ShapeDtypeStruct((H, 1, S), F32)],
        compiler_params=_cparams(2))(q_tab, k_tab, ka, qa, qs, kt, v, ot, do, dot_, lse)


def _to_heads(x2d):
    S = x2d.shape[0]
    return x2d.reshape(S, FOX_HEADS, FOX_HD).transpose(1, 0, 2)


def _from_heads(x3d):
    S = x3d.shape[1]
    return x3d.transpose(1, 0, 2).reshape(S, FOX_W)


def _gl_rows(P):
    return [Rows(P, D, PK_GL // D + k) for k in range(3)]


def merge_fwd(P, ya, yb, yc, b_gate, tag):
    def fn(step, i, n, g0, g1, g2, ya, yb, yc, b):
        out = 0.0
        for k, (gl, y) in enumerate(((g0, ya), (g1, yb), (g2, yc))):
            out = out + _sigmoid(gl.astype(F32) + b[:, k * D:(k + 1) * D]) * y.astype(F32)
        return (out,), ()
    return rowwise(fn, _gl_rows(P) + [ya, yb, yc], [b_gate], [(D, BF)], ts=512, name=tag + "_merge")[0][0]


def merge_bwd(dm, P, ya, yb, yc, b_gate, tag):
    def fn(step, i, n, dm, g0, g1, g2, ya, yb, yc, b):
        dm = dm.astype(F32)
        dys, dgls = [], []
        for k, (gl, y) in enumerate(((g0, ya), (g1, yb), (g2, yc))):
            g = _sigmoid(gl.astype(F32) + b[:, k * D:(k + 1) * D])
            dys.append(dm * g)
            dgls.append(dm * y.astype(F32) * (g * (1.0 - g)))
        dgl = jnp.concatenate(dgls, axis=1)
        return (dys[0], dys[1], dys[2], dgl), (_colsum(dgl),)
    (dya, dyb, dyc, dgl), (db,) = rowwise(
        fn, [dm] + _gl_rows(P) + [ya, yb, yc], [b_gate], [(D, BF), (D, BF), (D, BF), (3 * D, BF)], [((1, 3 * D), F32)],
        ts=256, name=tag + "_dmerge")
    return dya, dyb, dyc, dgl, db


def _xattn_probs(qh, kh):
    s = _dot_nt(qh, kh) * (X_HD ** -0.5)
    e = jnp.exp(s - jnp.max(s, axis=1, keepdims=True))
    return e / jnp.sum(e, axis=1, keepdims=True)


def xattn_fwd(q, kv, tag):
    def fn(step, i, n, q, k, v):
        os = []
        for h in range(X_HEADS):
            sl = slice(h * X_HD, (h + 1) * X_HD)
            os.append(_dot_nn(_xattn_probs(q[:, sl], k[:, sl]), v[:, sl]))
        return (jnp.concatenate(os, axis=1),), ()

    return _xattn_call(fn, [q], kv, [(D, BF)], [], tag + "_xattn")[0][0]


def _xattn_call(fn, rows, kv, outs, accs, name):
    S, ts, M = rows[0].shape[0], 512, kv.shape[0]
    ts = min(ts, S)
    nr, no, na = len(rows), len(outs), len(accs)

    def body(*refs):
        in_refs, out_refs, acc_refs = refs[:nr + 2], refs[nr + 2:nr + 2 + no], refs[nr + 2 + no:]
        step = pl.program_id(0)
        o, inc = fn(step, step, S // ts, *[r[...] for r in in_refs])
        for ref, val in zip(out_refs, o):
            ref[...] = val.astype(ref.dtype)
        if na:
            @pl.when(step == 0)
            def _():
                for ref in acc_refs:
                    ref[...] = jnp.zeros_like(ref)
            for ref, val in zip(acc_refs, inc):
                ref[...] += val

    row = pl.BlockSpec((ts, D), lambda s: (s, 0))
    res = pl.pallas_call(
        body, name=name, grid=(S // ts,),
        in_specs=[row] * nr + [pl.BlockSpec((M, D), lambda s: (0, 0)), pl.BlockSpec((M, D), lambda s: (0, 1))],
        out_specs=[row] * no + [pl.BlockSpec(tuple(s), lambda s_: (0, 0)) for s, _ in accs],
        out_shape=[jax.ShapeDtypeStruct((S, w), dt) for w, dt in outs] + [jax.ShapeDtypeStruct(tuple(s), dt) for s, dt in accs],
        compiler_params=_cparams(1))(*rows, kv, kv)
    return res[:no], res[no:]


def xattn_bwd(q, kv, do, tag):
    def fn(step, i, n, q, do, k, v):
        dqs, dks, dvs = [], [], []
        for h in range(X_HEADS):
            sl = slice(h * X_HD, (h + 1) * X_HD)
            p = _xattn_probs(q[:, sl], k[:, sl])
            dp = _dot_nt(do[:, sl], v[:, sl])
            ds = (p * (dp - jnp.sum(p * dp, axis=1, keepdims=True)) * (X_HD ** -0.5)).astype(BF)
            dqs.append(_dot_nn(ds, k[:, sl]))
            dks.append(_dot_tn(ds, q[:, sl]))
            dvs.append(_dot_tn(p, do[:, sl]))
        return (jnp.concatenate(dqs, axis=1),), (jnp.concatenate(dks + dvs, axis=1),)

    (dq,), (dkv,) = _xattn_call(fn, [q, do], kv, [(D, BF)], [((kv.shape[0], 2 * D), F32)], tag + "_dxattn")
    return dq, dkv


def layer_fwd(x, mem, w, tag):
    x1, s_ffn1 = ffn_fwd(x, w["g_ffn1"], w["ffn1_in"], w["ffn1_out"], tag + "_ffn1")
    u = rms_fwd(x1, w["g_mix"], name=tag + "_mixrms")
    P = matmul(u, w["wp"], "nn", name=tag + "_proj")
    fl = matmul(u, w["wfl"], "nn", out_dtype=F32, name=tag + "_projf")
    d, ya0 = pool_fwd(P, w["w_pool"], w["pool_scale"], tag)
    ya = matmul(ya0, w["up_a"], "nn", name=tag + "_upa")
    xc, hh, yb0 = lru_fwd(P, w["conv_w"], w["conv_b"], w["w_rg_a"], w["b_rg_a"], w["w_rg_x"], w["b_rg_x"], w["lam"], tag)
    yb = matmul(yb0, w["up_b"], "nn", name=tag + "_upb")
    c = cumlogf_fwd(fl, w["b_f"], tag)
    q, k, v = (_to_heads(P[:, o:o + FOX_W]) for o in (PK_Q, PK_K, PK_V))
    qa, ka, qs = fox_operands(q, k, c[:, :FOX_HEADS].T)
    ot, lse = flash_fwd(qa, ka, v.transpose(0, 2, 1), tag)
    o2 = ot.transpose(2, 0, 1).reshape(-1, FOX_W)
    yc = matmul(o2, w["up_c"], "nn", name=tag + "_upc")
    merged = merge_fwd(P, ya, yb, yc, w["b_gate"], tag)
    x2 = matmul(merged, w["w_o"], "nn", out_dtype=F32, res=x1, name=tag + "_wo")
    hq = rms_fwd(x2, w["g_cross"], name=tag + "_xrms")
    qx = matmul(hq, w["xq"], "nn", name=tag + "_xq")
    mn = rms_fwd(mem, w["g_mem"], name=tag + "_mrms")
    kv = matmul(mn, w["xkv"], "nn", name=tag + "_xkv")
    ox = xattn_fwd(qx, kv, tag)
    x3 = matmul(ox, w["xo"], "nn", out_dtype=F32, res=x2, name=tag + "_xo")
    x4, s_ffn2 = ffn_fwd(x3, w["g_ffn2"], w["ffn2_in"], w["ffn2_out"], tag + "_ffn2")
    saved = dict(ffn1=s_ffn1, ffn2=s_ffn2, x1=x1, u=u, P=P, fl=fl, d=d, ya0=ya0, ya=ya, xc=xc, hh=hh, yb0=yb0, yb=yb,
                 qa=qa, ka=ka, qs=qs, k=k, v=v, ot=ot, lse=lse, o2=o2, yc=yc, merged=merged, x2=x2, hq=hq, qx=qx,
                 mn=mn, kv=kv, ox=ox)
    return x4, saved


def layer_bwd(dx4, mem, w, s, tag):
    g = {}
    dx3, g["g_ffn2"], g["w_ffn2_in"], g["w_ffn2_out"] = ffn_bwd(dx4, s["ffn2"], w["g_ffn2"], w["ffn2_in"], w["ffn2_out"], tag + "_ffn2")
    dox = matmul(dx3, w["xo"], "nt", name=tag + "_dox")
    g["w_xo"] = matmul(s["ox"], dx3, "tn", name=tag + "_dwxo")
    dqx, dkv = xattn_bwd(s["qx"], s["kv"], dox, tag)
    g["w_xq"] = matmul(s["hq"], dqx, "tn", name=tag + "_dwxq")
    dhq = matmul(dqx, w["xq"], "nt", name=tag + "_dhq")
    dx2, g["g_cross"] = rms_bwd(s["x2"], w["g_cross"], dhq, dx3, name=tag + "_dxrms")
    g["w_xkv"] = matmul(s["mn"], dkv, "tn", name=tag + "_dwxkv")
    dmn = matmul(dkv, w["xkv"], "nt", name=tag + "_dmn")
    _, g["g_mem"] = rms_bwd(mem, w["g_mem"], dmn, None, name=tag + "_dmrms")
    P = s["P"]
    dmerged = matmul(dx2, w["w_o"], "nt", name=tag + "_dmerged")
    g["w_o"] = matmul(s["merged"], dx2, "tn", name=tag + "_dwo")
    dya, dyb, dyc, dgl, g["b_gate"] = merge_bwd(dmerged, P, s["ya"], s["yb"], s["yc"], w["b_gate"], tag)
    dya0 = matmul(dya, w["up_a"], "nt", name=tag + "_dya0")
    g["w_up_a"] = matmul(s["ya0"], dya, "tn", name=tag + "_dwupa")
    dxa, g["w_pool"], g["pool_scale"] = pool_bwd(dya0, s["d"], w["w_pool"], w["pool_scale"], tag)
    dyb0 = matmul(dyb, w["up_b"], "nt", name=tag + "_dyb0")
    g["w_up_b"] = matmul(s["yb0"], dyb, "tn", name=tag + "_dwupb")
    (dgb, dxb, g["w_rg_a"], g["b_rg_a"], g["w_rg_x"], g["b_rg_x"], g["lru_lambda"], g["conv_w"], g["conv_b"]) = lru_bwd(
        dyb0, P, s["hh"], s["xc"], w["conv_w"], w["w_rg_a"], w["b_rg_a"], w["w_rg_x"], w["b_rg_x"], w["lam"], tag)
    do2 = matmul(dyc, w["up_c"], "nt", name=tag + "_do2")
    g["w_up_c"] = matmul(s["o2"], dyc, "tn", name=tag + "_dwupc")
    do = _to_heads(do2)
    dqt, dk, dv, dck = flash_bwd(s["qa"], s["ka"], s["qs"], s["k"].transpose(0, 2, 1), s["v"], s["ot"], do,
                                 do.transpose(0, 2, 1), s["lse"], tag)
    dq = (dqt * (FOX_HD ** -0.5)).transpose(0, 1, 3, 2).reshape(dk.shape)
    dc = jnp.pad(dck[:, 0, :].T, ((0, 0), (0, LANES - FOX_HEADS)))
    dfl, g["b_f"] = cumlogf_bwd(dc, s["fl"], w["b_f"], tag)
    dP = jnp.concatenate([dxb, dgb, dgl, dxa] + [_from_heads(t).astype(BF) for t in (dq, dk, dv)], axis=1)
    du = matmul(dP, w["wp"], "nt", out_dtype=F32, name=tag + "_du")
    du = matmul(dfl, w["wfl"], "nt", out_dtype=F32, res=du, name=tag + "_duf")
    g["wp"] = matmul(s["u"], dP, "tn", name=tag + "_dwp")
    g["wfl"] = matmul(s["u"], dfl, "tn", out_dtype=F32, name=tag + "_dwfl")
    dx1, g["g_mix"] = rms_bwd(s["x1"], w["g_mix"], du, dx2, name=tag + "_dmixrms")
    dx0, g["g_ffn1"], g["w_ffn1_in"], g["w_ffn1_out"] = ffn_bwd(dx1, s["ffn1"], w["g_ffn1"], w["ffn1_in"], w["ffn1_out"], tag + "_ffn1")
    return dx0, g


def loss_head(x, target, g_final):
    def fn(step, i, n, x, t, g):
        r = _rstd(x)
        xhat = x * r
        e = xhat * g - t
        dy = e * (1.0 / D)
        gd = dy * g
        dx = r * (gd - xhat * jnp.mean(xhat * gd, axis=-1, keepdims=True))
        loss = 0.5 * jnp.sum(jnp.mean(e * e, axis=-1, keepdims=True), axis=0, keepdims=True)
        return (dx,), (jnp.broadcast_to(loss, (1, LANES)), _colsum(dy * xhat))

    (dx,), (loss, dg) = rowwise(fn, [x, target], [g_final], [(D, F32)], [((1, LANES), F32), ((1, D), F32)], ts=512, name="loss_head")
    return loss[0, 0], dx, dg


BIG = (("w_ffn1_in", (D, 2 * DFF), 1), ("w_ffn1_out", (DFF, D), 0), ("w_in", (D, 7176), 1), ("w_up_a", (POOL_W, D), 1),
       ("conv_w", (4, D), 1), ("w_up_b", (D, D), 0), ("w_up_c", (FOX_W, D), 1), ("w_o", (D, D), 0), ("w_xq", (D, D), 0),
       ("w_xkv", (D, 2 * D), 1), ("w_xo", (D, D), 0), ("w_ffn2_in", (D, 2 * DFF), 1), ("w_ffn2_out", (DFF, D), 0))
SMALL = (("g_ffn1", (D,)), ("g_mix", (D,)), ("b_f", (FOX_HEADS,)), ("b_gate", (3 * D,)), ("w_pool", (4, LANES, LANES)),
         ("pool_scale", (POOL_W,)), ("conv_b", (D,)), ("w_rg_a", (LRU_HEADS, LANES, LANES)), ("b_rg_a", (D,)),
         ("w_rg_x", (LRU_HEADS, LANES, LANES)), ("b_rg_x", (D,)), ("lru_lambda", (D,)), ("g_cross", (D,)), ("g_mem", (D,)),
         ("g_ffn2", (D,)))
ORDER = ("g_ffn1", "w_ffn1_in", "w_ffn1_out", "g_mix", "w_in", "b_f", "b_gate", "w_pool", "pool_scale", "w_up_a", "conv_w",
         "conv_b", "w_rg_a", "b_rg_a", "w_rg_x", "b_rg_x", "lru_lambda", "w_up_b", "w_up_c", "w_o", "g_cross", "g_mem", "w_xq",
         "w_xkv", "w_xo", "g_ffn2", "w_ffn2_in", "w_ffn2_out", "g_final")

IN_SPLIT = (("xa", 0, 512), ("xb", 512, 1024), ("gb", 1536, 1024), ("q", 2560, 512), ("k", 3072, 512), ("v", 3584, 512),
            ("fl", 4096, 8), ("gl", 4104, 3072))
PACK_ORDER = ("xb", "gb", "gl", "xa", "q", "k", "v")


def _shard_shape(shape, axis):
    s = list(shape)
    s[axis] //= N_CHIPS
    return (DEPTH, *s)


def _round_up(n, m):
    return -(-n // m) * m


SMALL_SEG_ROWS = 8


def _small_rows(n_elements):
    return _round_up(-(-n_elements // D), SMALL_SEG_ROWS)


def _pack_small(tree, names, row_multiple=SMALL_SEG_ROWS, fill=0.0):
    parts = []
    for n in names:
        flat = tree[n].astype(F32).reshape(-1)
        rows = _small_rows(flat.shape[0])
        parts.append(jnp.pad(flat, (0, rows * D - flat.shape[0]), constant_values=fill).reshape(rows, D))
    total = sum(p.shape[0] for p in parts)
    if total % row_multiple:
        parts.append(jnp.full((_round_up(total, row_multiple) - total, D), fill, F32))
    return jnp.concatenate(parts, axis=0)


def _unpack_small(buf, like, names):
    out, off = {}, 0
    for n in names:
        sh = like[n].shape
        rows = _small_rows(math.prod(sh))
        out[n] = buf[off:off + rows].reshape(-1)[:math.prod(sh)].reshape(sh)
        off += rows
    return out


def _from_chip_major(g, shape, axis):
    return jnp.moveaxis(g, 1, axis + 1).reshape(DEPTH, *shape)


def _to_chip_major(full, shape, axis):
    sh = list(shape)
    sh[axis:axis + 1] = [N_CHIPS, shape[axis] // N_CHIPS]
    return jnp.moveaxis(full.reshape(DEPTH, *sh), axis + 1, 1)


def _layer_weights(full, small, l):
    w_in = full["w_in"][l]
    cols = {n: w_in[:, o:o + s] for n, o, s in IN_SPLIT}
    w = dict(ffn1_in=full["w_ffn1_in"][l], ffn1_out=full["w_ffn1_out"][l], ffn2_in=full["w_ffn2_in"][l], ffn2_out=full["w_ffn2_out"][l],
             wp=jnp.concatenate([cols[n] for n in PACK_ORDER], axis=1),
             wfl=jnp.pad(cols["fl"], ((0, 0), (0, LANES - FOX_HEADS))),
             up_a=full["w_up_a"][l], up_b=full["w_up_b"][l], up_c=full["w_up_c"][l], w_o=full["w_o"][l], xq=full["w_xq"][l],
             xkv=full["w_xkv"][l], xo=full["w_xo"][l], conv_w=full["conv_w"][l])
    for n in ("g_ffn1", "g_mix", "b_gate", "pool_scale", "conv_b", "b_rg_a", "b_rg_x", "g_cross", "g_mem", "g_ffn2"):
        w[n] = small[n][l].reshape(1, -1)
    w["lam"] = small["lru_lambda"][l].reshape(1, -1)
    w["b_f"] = jnp.pad(small["b_f"][l], (0, LANES - FOX_HEADS)).reshape(1, LANES)
    for n in ("w_pool", "w_rg_a", "w_rg_x"):
        w[n] = small[n][l]
    return w


def _unpack_w_in_grad(gwp, gwfl):
    pk = {}
    off = 0
    sizes = {n: s for n, _, s in IN_SPLIT}
    for n in PACK_ORDER:
        pk[n] = gwp[:, off:off + sizes[n]]
        off += sizes[n]
    pk["fl"] = gwfl[:, :FOX_HEADS].astype(gwp.dtype)
    return jnp.concatenate([pk[n] for n, _, _ in IN_SPLIT], axis=1)


def _place():
    x, y, c = lax.axis_index("x"), lax.axis_index("y"), lax.axis_index("c")
    others = [(1 - x, y), (x, 1 - y), (1 - x, 1 - y)]
    return x, y, c, others


def _rcopy(src, dst, send_sems, recv_sems, k, to):
    return pltpu.make_async_remote_copy(src_ref=src, dst_ref=dst, send_sem=send_sems.at[k], recv_sem=recv_sems.at[k],
                                        device_id=to, device_id_type=MESH)


def _comm_call(body, name, ins, out_shape, n_sems):
    return pl.pallas_call(
        body, name=name, in_specs=[ANY] * len(ins), out_specs=[ANY] * len(out_shape), out_shape=out_shape,
        scratch_shapes=[pltpu.SemaphoreType.DMA((n_sems,)), pltpu.SemaphoreType.DMA((n_sems,)), pltpu.SemaphoreType.DMA((n_sems,))],
    )(*ins)


def allgather_same_core(bufs):
    nb = len(bufs)

    def body(*refs):
        ins, outs = refs[:nb], refs[nb:2 * nb]
        send_sems, recv_sems, local_sems = refs[2 * nb:]
        x, y, c, others = _place()
        me = 2 * x + y
        sends = []
        for b in range(nb):
            for j, (ox, oy) in enumerate(others):
                cp = _rcopy(ins[b], outs[b].at[me], send_sems, recv_sems, 3 * b + j, (ox, oy, c))
                cp.start()
                sends.append(cp)
        for b in range(nb):
            for j, (ox, oy) in enumerate(others):
                theirs = outs[b].at[2 * ox + oy]
                _rcopy(theirs, theirs, send_sems, recv_sems, 3 * b + j, (ox, oy, c)).wait_recv()
        for cp in sends:
            cp.wait_send()

    shapes = [jax.ShapeDtypeStruct((N_CHIPS, *b.shape), b.dtype) for b in bufs]
    outs = _comm_call(body, "allgather_same_core", list(bufs), shapes, 3 * nb)
    return [_with_own_block(o, b) for o, b in zip(outs, bufs)]


def _with_own_block(gathered, mine):
    me = 2 * lax.axis_index("x") + lax.axis_index("y")
    return lax.dynamic_update_index_in_dim(gathered, mine, me, axis=0)


def gather_weights(shards):
    nw = len(shards)

    def body(*refs):
        ins, outs = refs[:nw], refs[nw:2 * nw]
        send_sems, recv_sems, _ = refs[2 * nw:]
        x, y, c, _ = _place()
        fx, fy = 1 - c, c
        first = (x + fx - 2 * x * fx, y + fy - 2 * y * fy)
        second = (x + fy - 2 * x * fy, y + fx - 2 * y * fx)
        me, c1, c2, cd = 2 * x + y, 2 * first[0] + first[1], 2 * second[0] + second[1], 2 * (1 - x) + (1 - y)
        sibling = (x, y, 1 - c)
        started = []

        def send(w, k, src, dst, to):
            cp = _rcopy(src, dst, send_sems, recv_sems, 6 * w + k, to)
            cp.start()
            started.append(cp)

        def landed(w, k, block):
            _rcopy(block, block, send_sems, recv_sems, 6 * w + k, sibling).wait_recv()

        for w in range(nw):
            send(w, 0, ins[w].at[c], outs[w].at[c, me], (*first, c))
            send(w, 1, ins[w].at[c], outs[w].at[c, me], (*second, c))
        for w in range(nw):
            block = outs[w].at[c, c1]
            landed(w, 0, block)
            send(w, 2, block, block, (*second, c))
            send(w, 3, block, block, sibling)
        for w in range(nw):
            block = outs[w].at[c, c2]
            landed(w, 1, block)
            send(w, 4, block, block, sibling)
        for w in range(nw):
            block = outs[w].at[c, cd]
            landed(w, 2, block)
            send(w, 5, block, block, sibling)
        for w in range(nw):
            landed(w, 3, outs[w].at[1 - c, c2])
            landed(w, 4, outs[w].at[1 - c, c1])
            landed(w, 5, outs[w].at[1 - c, cd])
        for cp in started:
            cp.wait_send()

    shapes = [jax.ShapeDtypeStruct((DEPTH, N_CHIPS, *s.shape[1:]), s.dtype) for s in shards]
    outs = _comm_call(body, "gather_weights", list(shards), shapes, 6 * nw)
    me = 2 * lax.axis_index("x") + lax.axis_index("y")
    return [lax.dynamic_update_index_in_dim(o, s, me, axis=1) for o, s in zip(outs, shards)]


def sibling_layers(bufs):
    nb = len(bufs)

    def body(*refs):
        ins, outs = refs[:nb], refs[nb:2 * nb]
        send_sems, recv_sems, _ = refs[2 * nb:]
        x, y, c, _ = _place()
        work = []
        for b in range(nb):
            cp = _rcopy(ins[b].at[1 - c], outs[b], send_sems, recv_sems, b, (x, y, 1 - c))
            cp.start()
            work.append(cp)
        for cp in work:
            cp.wait()

    shapes = [jax.ShapeDtypeStruct(b.shape[1:], b.dtype) for b in bufs]
    return _comm_call(body, "sibling_layers", list(bufs), shapes, nb)


def chip_blocks(bufs):
    nb = len(bufs)

    def body(*refs):
        ins, outs = refs[:nb], refs[nb:2 * nb]
        send_sems, recv_sems, _ = refs[2 * nb:]
        x, y, c, others = _place()
        sends = []
        for b in range(nb):
            for j, (ox, oy) in enumerate(others):
                cp = _rcopy(ins[b].at[2 * ox + oy], outs[b].at[j], send_sems, recv_sems, 3 * b + j, (ox, oy, c))
                cp.start()
                sends.append(cp)
        for cp in sends:
            cp.wait()

    shapes = [jax.ShapeDtypeStruct((3, *b.shape[1:]), b.dtype) for b in bufs]
    return _comm_call(body, "chip_blocks", list(bufs), shapes, 3 * nb)


def sibling_other_layer(bufs):
    nb = len(bufs)

    def body(*refs):
        ins, outs = refs[:nb], refs[nb:2 * nb]
        send_sems, recv_sems, _ = refs[2 * nb:]
        x, y, c, _ = _place()
        work = []
        for b in range(nb):
            cp = _rcopy(ins[b], outs[b].at[c], send_sems, recv_sems, b, (x, y, 1 - c))
            cp.start()
            work.append(cp)
        for b, cp in enumerate(work):
            cp.wait_send()
            theirs = outs[b].at[1 - c]
            _rcopy(theirs, theirs, send_sems, recv_sems, b, (x, y, 1 - c)).wait_recv()

    shapes = [jax.ShapeDtypeStruct((DEPTH, *b.shape), b.dtype) for b in bufs]
    outs = _comm_call(body, "sibling_other_layer", list(bufs), shapes, nb)
    c = lax.axis_index("c")
    return [lax.dynamic_update_index_in_dim(o, b, c, axis=0) for o, b in zip(outs, bufs)]


ADD_BLOCK_BYTES = 2 * 2 ** 20


def add_arrays(parts, out_dtype, name):
    shape = parts[0].shape
    cols = shape[-1]
    rows = math.prod(shape[:-1])

    def fn(step, i, n, *vals):
        acc = vals[0].astype(F32)
        for v in vals[1:]:
            acc = acc + v.astype(F32)
        return (acc,), ()

    ts = rows
    for cand in (2048, 1024, 512, 256, 128, 64, 32, 16):
        if rows % cand == 0 and cand * cols * 4 <= ADD_BLOCK_BYTES:
            ts = cand
            break
    out = rowwise(fn, [p.reshape(rows, cols) for p in parts], outs=[(cols, out_dtype)], ts=ts, name=name)[0][0]
    return out.reshape(shape)


def reduce_gradients(bufs, names):
    c, me = lax.axis_index("c"), 2 * lax.axis_index("x") + lax.axis_index("y")
    theirs = sibling_layers(bufs)
    mine = [lax.dynamic_index_in_dim(b, c, axis=0, keepdims=False) for b in bufs]
    part = [add_arrays([m, t], m.dtype, "rs_add_sibling_" + n) for m, t, n in zip(mine, theirs, names)]
    recv = chip_blocks(part)
    own = [lax.dynamic_index_in_dim(p, me, axis=0, keepdims=False) for p in part]
    red = [add_arrays([o, r[0], r[1], r[2]], F32, "rs_add_chips_" + n) for o, r, n in zip(own, recv, names)]
    return sibling_other_layer(red)


_BC1 = 1.0 - ADAM_B1 ** ADAM_STEP
_BC2 = 1.0 - ADAM_B2 ** ADAM_STEP


def adamw(w, g, m, v, name):
    rows, cols = w.shape

    def fn(step, i, n, w, g, m, v):
        m2 = ADAM_B1 * m + (1.0 - ADAM_B1) * g
        v2 = ADAM_B2 * v + (1.0 - ADAM_B2) * (g * g)
        delta = -ADAM_LR * ((m2 / _BC1) / (jnp.sqrt(v2 / _BC2) + ADAM_EPS) + ADAM_WD * w)
        return (delta, m2, v2), ()

    ts = _pick(rows, (256, 128, 64, 32, 16, 8))
    return rowwise(fn, [w, g, m, v], outs=[(cols, F32)] * 3, ts=ts, name=name)[0]


def local_step(x, mem, target, full, small):
    ws = [_layer_weights(full, small, l) for l in range(DEPTH)]
    saved = []
    h = x
    for l in range(DEPTH):
        h, s = layer_fwd(h, mem, ws[l], "l")
        saved.append(s)
    loss, dh, dg_final = loss_head(h, target, small["g_final"].reshape(1, D))
    grads = [None] * DEPTH
    for l in reversed(range(DEPTH)):
        dh, grads[l] = layer_bwd(dh, mem, ws[l], saved[l], "l")
    out = {}
    for l in range(DEPTH):
        g = grads[l]
        g["w_in"] = _unpack_w_in_grad(g.pop("wp"), g.pop("wfl"))
        g["b_f"] = g["b_f"][:, :FOX_HEADS]
    for name, shape, _ in BIG:
        out[name] = jnp.stack([grads[l][name].reshape(shape) for l in range(DEPTH)])
    for name, shape in SMALL:
        out[name] = jnp.stack([grads[l][name].reshape(shape) for l in range(DEPTH)])
    out["g_final"] = dg_final.reshape(D)
    return loss, dh, out


def kernel(x, mem, g_ffn1, w_ffn1_in, w_ffn1_out, g_mix, w_in, b_f, b_gate, w_pool, pool_scale, w_up_a, conv_w, conv_b, w_rg_a, b_rg_a, w_rg_x, b_rg_x, lru_lambda, w_up_b, w_up_c, w_o, g_cross, g_mem, w_xq, w_xkv, w_xo, g_ffn2, w_ffn2_in, w_ffn2_out, g_final, loss_target, m_g_ffn1, m_w_ffn1_in, m_w_ffn1_out, m_g_mix, m_w_in, m_b_f, m_b_gate, m_w_pool, m_pool_scale, m_w_up_a, m_conv_w, m_conv_b, m_w_rg_a, m_b_rg_a, m_w_rg_x, m_b_rg_x, m_lru_lambda, m_w_up_b, m_w_up_c, m_w_o, m_g_cross, m_g_mem, m_w_xq, m_w_xkv, m_w_xo, m_g_ffn2, m_w_ffn2_in, m_w_ffn2_out, m_g_final, v_g_ffn1, v_w_ffn1_in, v_w_ffn1_out, v_g_mix, v_w_in, v_b_f, v_b_gate, v_w_pool, v_pool_scale, v_w_up_a, v_conv_w, v_conv_b, v_w_rg_a, v_b_rg_a, v_w_rg_x, v_b_rg_x, v_lru_lambda, v_w_up_b, v_w_up_c, v_w_o, v_g_cross, v_g_mem, v_w_xq, v_w_xkv, v_w_xo, v_g_ffn2, v_w_ffn2_in, v_w_ffn2_out, v_g_final):
    args = dict(locals())
    weights = {n: args[n] for n in ORDER}
    m_in = {n: args["m_" + n] for n in ORDER}
    v_in = {n: args["v_" + n] for n in ORDER}
    big_names = [n for n, _, _ in BIG]
    small_names = [n for n, _ in SMALL] + ["g_final"]

    shards = []
    for n in big_names:
        shards += list(_split3(weights[n])) if n == "conv_w" else [weights[n].astype(BF)]
    gathered = iter(gather_weights(shards))
    full = {}
    for n, shape, axis in BIG:
        g = next(gathered)
        if n == "conv_w":
            g = g.astype(F32) + next(gathered).astype(F32) + next(gathered).astype(F32)
        full[n] = _from_chip_major(g, shape, axis)

    small = {n: weights[n] for n in small_names}
    loss, grad_x, contrib = local_step(x[0], mem[0], loss_target[0], full, small)
    loss = lax.psum(loss, ("x", "y", "c"))

    bufs = [_to_chip_major(contrib[n].astype(BF), shape, axis) for n, shape, axis in BIG]
    small_buf = _pack_small(contrib, small_names, DEPTH * N_CHIPS * SMALL_SEG_ROWS).reshape(DEPTH, N_CHIPS, -1, D)
    red = reduce_gradients(bufs + [small_buf], big_names + ["small"])

    grads = dict(zip(big_names, red[:-1]))
    red_small = allgather_same_core([red[-1]])[0].transpose(1, 0, 2, 3).reshape(-1, D)
    grads.update(_unpack_small(red_small, weights, small_names))

    delta, new_m, new_v = {}, {}, {}
    for name in big_names:
        sh = weights[name].shape
        two_d = (-1, sh[-1])
        d, m2, v2 = adamw(weights[name].reshape(two_d), grads[name].reshape(two_d), m_in[name].reshape(two_d),
                          v_in[name].reshape(two_d), "adamw_" + name)
        delta[name], new_m[name], new_v[name] = d.reshape(sh), m2.reshape(sh), v2.reshape(sh)
    outs = adamw(_pack_small(weights, small_names), _pack_small(grads, small_names), _pack_small(m_in, small_names),
                 _pack_small(v_in, small_names, fill=1.0), "adamw_small")
    for t, res in zip((delta, new_m, new_v), outs):
        t.update(_unpack_small(res, weights, small_names))

    return (loss, grad_x[None], *[grads[n] for n in ORDER], *[delta[n] for n in ORDER],
            *[new_m[n] for n in ORDER], *[new_v[n] for n in ORDER])
```

```python
import math

import jax
import jax.numpy as jnp
from jax import lax
from jax.experimental import pallas as pl
from jax.experimental.pallas import tpu as pltpu

F32 = jnp.float32
BF = jnp.bfloat16

D = 1024
DFF = 2816
DEPTH = 2
POOL_W = 512
POOL_WINDOWS = (2, 4, 8, 16)
LRU_HEADS = 8
LRU_C = 8.0
FOX_HEADS = 8
FOX_HD = 64
FOX_W = 512
X_HEADS = 4
X_HD = 256
EPS = 1e-6
LANES = 128
N_CHIPS = 4

ADAM_LR, ADAM_B1, ADAM_B2, ADAM_EPS, ADAM_WD, ADAM_STEP = 0.001, 0.9, 0.999, 1e-08, 0.01, 10

VMEM_LIMIT_BYTES = 56 * 2 ** 20

PK_XB, PK_GB, PK_GL, PK_XA, PK_Q, PK_K, PK_V = 0, 1024, 2048, 5120, 5632, 6144, 6656
PK_W = 7168

MESH = pl.DeviceIdType.MESH
ANY = pl.BlockSpec(memory_space=pl.ANY)


def _cparams(ngrid):
    return pltpu.CompilerParams(dimension_semantics=("arbitrary",) * ngrid, vmem_limit_bytes=VMEM_LIMIT_BYTES)


def _pick(n, cands):
    for c in cands:
        if n % c == 0:
            return c
    return n


def _iota(shape, dim):
    return lax.broadcasted_iota(jnp.int32, shape, dim)


def _sigmoid(x):
    return 1.0 / (1.0 + jnp.exp(-x))


def _softplus(z):
    return jnp.maximum(z, 0.0) + jnp.log1p(jnp.exp(-jnp.abs(z)))


def _expm1(x):
    small = jnp.abs(x) < 0.25
    xs = jnp.where(small, x, 0.0)
    poly = xs * (1.0 + xs * (1 / 2 + xs * (1 / 6 + xs * (1 / 24 + xs * (1 / 120 + xs * (1 / 720 + xs * (1 / 5040 + xs * (1 / 40320))))))))
    return jnp.where(small, poly, jnp.exp(x) - 1.0)


_GELU_K = math.sqrt(2.0 / math.pi)


def _gelu_and_grad(x):
    inner = _GELU_K * (x + 0.044715 * x * x * x)
    t = jnp.tanh(inner)
    g = 0.5 * x * (1.0 + t)
    dg = 0.5 * (1.0 + t) + 0.5 * x * (1.0 - t * t) * _GELU_K * (1.0 + 3 * 0.044715 * x * x)
    return g, dg


def _dot(a, b, dims):
    return lax.dot_general(a.astype(BF), b.astype(BF), (dims, ((), ())), preferred_element_type=F32)


def _dot_nn(a, b):
    return _dot(a, b, ((1,), (0,)))


def _dot_nt(a, b):
    return _dot(a, b, ((1,), (1,)))


def _dot_tn(a, b):
    return _dot(a, b, ((0,), (0,)))


def _colsum(x):
    return jnp.sum(x, axis=0, keepdims=True)


_TM = (1024, 1408, 512, 256, 128)
_TN = (1408, 1024, 512, 384, 256, 128)
_TK = (2432, 1408, 1024, 512, 256, 128)


def matmul(a, b, mode, *, out_dtype=None, alpha=1.0, res=None, name):
    out_dtype = BF if out_dtype is None else out_dtype
    if mode == "nn":
        (M, K), N = a.shape, b.shape[1]
    elif mode == "nt":
        (M, K), N = a.shape, b.shape[0]
    else:
        (K, M), N = a.shape, b.shape[1]
    tm, tn, tk = _pick(M, _TM), _pick(N, _TN), _pick(K, _TK)
    nk = K // tk
    dims = {"nn": ((1,), (0,)), "nt": ((1,), (1,)), "tn": ((0,), (0,))}[mode]
    a_spec = pl.BlockSpec((tk, tm), lambda i, j, k: (k, i)) if mode == "tn" else pl.BlockSpec((tm, tk), lambda i, j, k: (i, k))
    b_spec = pl.BlockSpec((tn, tk), lambda i, j, k: (j, k)) if mode == "nt" else pl.BlockSpec((tk, tn), lambda i, j, k: (k, j))
    o_spec = pl.BlockSpec((tm, tn), lambda i, j, k: (i, j))
    has_res = res is not None

    def body(*refs):
        a_ref, b_ref = refs[0], refs[1]
        r_ref = refs[2] if has_res else None
        o_ref, acc = refs[2 + has_res], refs[3 + has_res]
        k = pl.program_id(2)

        @pl.when(k == 0)
        def _():
            acc[...] = jnp.zeros_like(acc)

        acc[...] += _dot(a_ref[...], b_ref[...], dims)

        @pl.when(k == nk - 1)
        def _():
            r = acc[...]
            if alpha != 1.0:
                r = r * alpha
            if has_res:
                r = r + r_ref[...].astype(F32)
            o_ref[...] = r.astype(o_ref.dtype)

    ins = [a, b] + ([res] if has_res else [])
    specs = [a_spec, b_spec] + ([o_spec] if has_res else [])
    return pl.pallas_call(
        body, name=name, grid=(M // tm, N // tn, nk), in_specs=specs, out_specs=o_spec,
        out_shape=jax.ShapeDtypeStruct((M, N), out_dtype), scratch_shapes=[pltpu.VMEM((tm, tn), F32)],
        compiler_params=_cparams(3))(*ins)


class Rows:
    def __init__(self, arr, w=None, cb=0, halo=None, h=8):
        self.arr, self.w, self.cb, self.halo, self.h = arr, (arr.shape[1] if w is None else w), cb, halo, h


def rowwise(fn, rows, consts=(), outs=(), accs=(), scratch=(), *, ts, name, reverse=False):
    rows = [r if isinstance(r, Rows) else Rows(r) for r in rows]
    S = rows[0].arr.shape[0]
    n = S // ts
    tile = (lambda s: n - 1 - s) if reverse else (lambda s: s)

    def row_spec(r):
        if r.halo is None:
            return pl.BlockSpec((ts, r.w), lambda s: (tile(s), r.cb))
        per, last = ts // r.h, S // r.h - 1
        if r.halo == "prev":
            return pl.BlockSpec((r.h, r.w), lambda s: (jnp.maximum(tile(s) * per - 1, 0), r.cb))
        return pl.BlockSpec((r.h, r.w), lambda s: (jnp.minimum((tile(s) + 1) * per, last), r.cb))

    def whole(shape):
        nd = len(shape)
        return pl.BlockSpec(tuple(shape), lambda s: (0,) * nd)

    nr, nc, no, na = len(rows), len(consts), len(outs), len(accs)

    def body(*refs):
        in_refs, out_refs = refs[:nr + nc], refs[nr + nc:nr + nc + no]
        acc_refs, scr = refs[nr + nc + no:nr + nc + no + na], refs[nr + nc + no + na:]
        step = pl.program_id(0)
        o, inc = fn(step, tile(step), n, *[r[...] for r in in_refs], *scr)
        for ref, val in zip(out_refs, o):
            ref[...] = val.astype(ref.dtype)
        if na:
            @pl.when(step == 0)
            def _():
                for ref in acc_refs:
                    ref[...] = jnp.zeros_like(ref)
            for ref, val in zip(acc_refs, inc):
                ref[...] += val.astype(ref.dtype)

    out_shape = [jax.ShapeDtypeStruct((S, w), dt) for w, dt in outs] + [jax.ShapeDtypeStruct(tuple(s), dt) for s, dt in accs]
    out_specs = [pl.BlockSpec((ts, w), lambda s: (tile(s), 0)) for w, _ in outs] + [whole(s) for s, _ in accs]
    res = pl.pallas_call(
        body, name=name, grid=(n,),
        in_specs=[row_spec(r) for r in rows] + [whole(c.shape) for c in consts],
        out_specs=out_specs, out_shape=out_shape, scratch_shapes=list(scratch),
        compiler_params=_cparams(1))(*[r.arr for r in rows], *consts)
    return res[:no], res[no:]


def _rstd(x):
    return lax.rsqrt(jnp.mean(x * x, axis=-1, keepdims=True) + EPS)


def rms_fwd(x, g, *, name, ts=512):
    def fn(step, i, n, x, g):
        return ((x * _rstd(x)) * g,), ()
    return rowwise(fn, [x], [g], [(D, BF)], ts=min(ts, x.shape[0]), name=name)[0][0]


def rms_bwd(x, g, dh, dres, *, name, ts=512):
    has_res = dres is not None

    def fn(step, i, n, x, dh, *rest):
        g = rest[-1]
        r = _rstd(x)
        xhat = x * r
        dh = dh.astype(F32)
        gd = dh * g
        dx = r * (gd - xhat * jnp.mean(xhat * gd, axis=-1, keepdims=True))
        if has_res:
            dx = dx + rest[0]
        return (dx,), (_colsum(dh * xhat),)

    rows = [x, dh] + ([dres] if has_res else [])
    o, a = rowwise(fn, rows, [g], [(D, F32)], [((1, D), F32)], ts=min(ts, x.shape[0]), name=name)
    return o[0], a[0]


def ffn_fwd(x, g, w_in, w_out, tag):
    h = rms_fwd(x, g, name=tag + "_rms")
    ab = matmul(h, w_in, "nn", name=tag + "_in")

    def act_fn(step, i, n, ab):
        a, b = ab[:, :DFF].astype(F32), ab[:, DFF:].astype(F32)
        return ((a * _sigmoid(a)) * b,), ()

    act = rowwise(act_fn, [ab], outs=[(DFF, BF)], ts=256, name=tag + "_act")[0][0]
    y = matmul(act, w_out, "nn", out_dtype=F32, alpha=0.5, res=x, name=tag + "_out")
    return y, (x, h, ab, act)


def ffn_bwd(dy, saved, g, w_in, w_out, tag):
    x, h, ab, act = saved
    dact = matmul(dy, w_out, "nt", alpha=0.5, name=tag + "_dact")
    dw_out = matmul(act, dy, "tn", alpha=0.5, name=tag + "_dwout")

    def dab_fn(step, i, n, ab, dact):
        a, b, dact = ab[:, :DFF].astype(F32), ab[:, DFF:].astype(F32), dact.astype(F32)
        s = _sigmoid(a)
        da = dact * b * (s * (1.0 + a * (1.0 - s)))
        db = dact * (a * s)
        return (jnp.concatenate([da, db], axis=1),), ()

    dab = rowwise(dab_fn, [ab, dact], outs=[(2 * DFF, BF)], ts=256, name=tag + "_dab")[0][0]
    dw_in = matmul(h, dab, "tn", name=tag + "_dwin")
    dh = matmul(dab, w_in, "nt", name=tag + "_dh")
    dx, dg = rms_bwd(x, g, dh, dy, name=tag + "_drms")
    return dx, dg, dw_in, dw_out


POOL_TS = 256
POOL_HALO = 16


def pool_fwd(P, w_pool, pool_scale, tag):
    ts = POOL_TS

    def fn(step, i, n, xa, halo, w, scale):
        xa = xa.astype(F32)
        halo = halo.astype(F32) * jnp.where(i > 0, 1.0, 0.0)
        ext = jnp.concatenate([halo, xa], axis=0)
        pos = (i * ts + 1 + _iota((ts, LANES), 0)).astype(F32)
        ds, ys = [], []
        for gi, win in enumerate(POOL_WINDOWS):
            e = ext[:, gi * LANES:(gi + 1) * LANES]
            sh = 1
            while sh < win:
                e = e + pltpu.roll(e, sh, 0)
                sh *= 2
            mean = e[POOL_HALO:] / jnp.minimum(pos, float(win))
            d = mean - xa[:, gi * LANES:(gi + 1) * LANES]
            ds.append(d)
            ys.append(_dot_nn(d, w[gi]))
        d = jnp.concatenate(ds, axis=1)
        return (d, jnp.concatenate(ys, axis=1) * scale), ()

    xa = Rows(P, POOL_W, PK_XA // POOL_W)
    xa_prev = Rows(P, POOL_W, PK_XA // POOL_W, "prev", POOL_HALO)
    (d, ya0), _ = rowwise(fn, [xa, xa_prev], [w_pool, pool_scale], [(POOL_W, BF), (POOL_W, BF)], ts=ts, name=tag + "_pool")
    return d, ya0


def pool_bwd(dya0, d, w_pool, pool_scale, tag):
    ts = POOL_TS
    L = ts + POOL_HALO

    def fn(step, i, n, dya0, dya0_next, d, w, scale):
        d = d.astype(F32)
        dz = jnp.concatenate([dya0.astype(F32), dya0_next.astype(F32) * jnp.where(i < n - 1, 1.0, 0.0)], axis=0) * scale
        pos = (i * ts + 1 + _iota((L, LANES), 0)).astype(F32)
        dxa, dws, dsc = [], [], []
        for gi, win in enumerate(POOL_WINDOWS):
            sl = slice(gi * LANES, (gi + 1) * LANES)
            dzg = dz[:, sl]
            dd = _dot_nt(dzg, w[gi])
            e = dd / jnp.minimum(pos, float(win))
            sh = 1
            while sh < win:
                e = e + pltpu.roll(e, L - sh, 0)
                sh *= 2
            dxa.append(e[:ts] - dd[:ts])
            dws.append(_dot_tn(d[:, sl], dzg[:ts])[None])
            z = _dot_nn(d[:, sl], w[gi])
            dsc.append(_colsum(dya0[:, sl].astype(F32) * z))
        return (jnp.concatenate(dxa, axis=1),), (jnp.concatenate(dws, axis=0), jnp.concatenate(dsc, axis=1))

    (dxa,), (dw_pool, dscale) = rowwise(
        fn, [dya0, Rows(dya0, halo="next", h=POOL_HALO), d], [w_pool, pool_scale],
        [(POOL_W, BF)], [((4, LANES, LANES), F32), ((1, POOL_W), F32)], ts=ts, name=tag + "_dpool")
    return dxa, dw_pool, dscale


LRU_R = 512


def _scan_fwd(A, U):
    R = A.shape[0]
    row = _iota(A.shape, 0)
    d = 1
    while d < R:
        m = row >= d
        A_sh = jnp.where(m, pltpu.roll(A, d, 0), 1.0)
        U_sh = jnp.where(m, pltpu.roll(U, d, 0), 0.0)
        U = A * U_sh + U
        A = A * A_sh
        d *= 2
    return A, U


def _scan_bwd(B, X):
    R = B.shape[0]
    row = _iota(B.shape, 0)
    d = 1
    while d < R:
        m = row < R - d
        B_sh = jnp.where(m, pltpu.roll(B, R - d, 0), 1.0)
        X_sh = jnp.where(m, pltpu.roll(X, R - d, 0), 0.0)
        X = X + B * X_sh
        B = B * B_sh
        d *= 2
    return X


def _lru_gates(xc, wa, ba, wx, bx, lam):
    r = _sigmoid(_dot_nn(xc, wa) + ba)
    ig = _sigmoid(_dot_nn(xc, wx) + bx)
    sp = _softplus(-lam)
    log_a = -LRU_C * r * sp
    a = jnp.exp(log_a)
    mult = jnp.sqrt(-_expm1(2.0 * log_a))
    return r, ig, sp, a, mult


def _lru_specs(S, R, reverse):
    nch = S // R
    ch = (lambda j: nch - 1 - j) if reverse else (lambda j: j)
    per = R // 8

    def col(off):
        return pl.BlockSpec((R, LANES), lambda h, j: (ch(j), off + h))

    def prev(off):
        return pl.BlockSpec((8, LANES), lambda h, j: (jnp.maximum(ch(j) * per - 1, 0), off + h))

    vec = pl.BlockSpec((1, LANES), lambda h, j: (0, h))
    cw = pl.BlockSpec((4, LANES), lambda h, j: (0, h))
    wsq = pl.BlockSpec((None, LANES, LANES), lambda h, j: (h, 0, 0))
    return nch, ch, col, prev, vec, cw, wsq


def lru_fwd(P, conv_w, conv_b, w_a, b_a, w_x, b_x, lam, tag):
    S = P.shape[0]
    R = min(LRU_R, S)
    nch, ch, col, prev, vec, cw_spec, wsq = _lru_specs(S, R, False)

    def body(xb_ref, halo_ref, gb_ref, cw_ref, cb_ref, wa_ref, ba_ref, wx_ref, bx_ref, lam_ref, xc_ref, h_ref, yb_ref, carry):
        j = pl.program_id(1)
        xb = xb_ref[...].astype(F32)
        halo = halo_ref[...].astype(F32) * jnp.where(j > 0, 1.0, 0.0)
        ext = jnp.concatenate([halo, xb], axis=0)
        cw = cw_ref[...]
        xc = cb_ref[...]
        for k in range(4):
            e = ext if k == 3 else pltpu.roll(ext, 3 - k, 0)
            xc = xc + e[8:] * cw[k:k + 1]
        r, ig, sp, a, mult = _lru_gates(xc, wa_ref[...], ba_ref[...], wx_ref[...], bx_ref[...], lam_ref[...])
        u = mult * (ig * xc)
        cum_a, hloc = _scan_fwd(a, u)

        @pl.when(j == 0)
        def _():
            carry[...] = jnp.zeros_like(carry)

        hfull = hloc + cum_a * carry[0:1]
        carry[...] = jnp.broadcast_to(hfull[R - 1:R], carry.shape)
        gel, _ = _gelu_and_grad(gb_ref[...].astype(F32))
        xc_ref[...] = xc
        h_ref[...] = hfull
        yb_ref[...] = (hfull * gel).astype(yb_ref.dtype)

    out = pl.BlockSpec((R, LANES), lambda h, j: (j, h))
    return pl.pallas_call(
        body, name=tag + "_lru", grid=(LRU_HEADS, nch),
        in_specs=[col(PK_XB // LANES), prev(PK_XB // LANES), col(PK_GB // LANES), cw_spec, vec, wsq, vec, wsq, vec, vec],
        out_specs=[out, out, out],
        out_shape=[jax.ShapeDtypeStruct((S, D), F32), jax.ShapeDtypeStruct((S, D), F32), jax.ShapeDtypeStruct((S, D), BF)],
        scratch_shapes=[pltpu.VMEM((8, LANES), F32)], compiler_params=_cparams(2),
    )(P, P, P, conv_w, conv_b, w_a, b_a, w_x, b_x, lam)


def lru_bwd(dyb0, P, hh, xc, conv_w, w_a, b_a, w_x, b_x, lam, tag):
    S = P.shape[0]
    R = min(LRU_R, S)
    nch, ch, col, prev, vec, cw_spec, wsq = _lru_specs(S, R, True)

    def body(dyb_ref, gb_ref, h_ref, hprev_ref, xc_ref, xb_ref, xbprev_ref, cw_ref, wa_ref, ba_ref, wx_ref, bx_ref, lam_ref,
             dgb_ref, dxb_ref, dwa_ref, dba_ref, dwx_ref, dbx_ref, dlam_ref, dcw_ref, dcb_ref, gcarry, dxc_head):
        j = pl.program_id(1)
        jj = nch - 1 - j
        has_prev = jnp.where(jj > 0, 1.0, 0.0)
        row = _iota((R, LANES), 0)
        xc, hv, lam = xc_ref[...], h_ref[...], lam_ref[...]
        wa, wx = wa_ref[...], wx_ref[...]
        r, ig, sp, a, mult = _lru_gates(xc, wa, ba_ref[...], wx, bx_ref[...], lam)
        dyb = dyb_ref[...].astype(F32)
        gel, dgel = _gelu_and_grad(gb_ref[...].astype(F32))
        dgb_ref[...] = (dyb * hv * dgel).astype(dgb_ref.dtype)

        @pl.when(j == 0)
        def _():
            gcarry[...] = jnp.zeros_like(gcarry)
            dxc_head[...] = jnp.zeros_like(dxc_head)
            for ref in (dwa_ref, dba_ref, dwx_ref, dbx_ref, dlam_ref, dcw_ref, dcb_ref):
                ref[...] = jnp.zeros_like(ref)

        B = jnp.where(row < R - 1, pltpu.roll(a, R - 1, 0), 0.0)
        X = dyb * gel + jnp.where(row == R - 1, gcarry[0:1], 0.0)
        G = _scan_bwd(B, X)
        gcarry[...] = jnp.broadcast_to(a[0:1] * G[0:1], gcarry.shape)
        hprev = jnp.where(row == 0, hprev_ref[...][7:8] * has_prev, pltpu.roll(hv, 1, 0))
        da = G * hprev
        dmult = G * ig * xc
        dig = G * mult * xc
        dxc = G * mult * ig
        dlog_a = da * a - dmult * (a * a) / mult
        dzr = dlog_a * (-LRU_C * sp) * (r * (1.0 - r))
        dzi = dig * (ig * (1.0 - ig))
        dxc = dxc + _dot_nt(dzr, wa) + _dot_nt(dzi, wx)
        dwa_ref[...] += _dot_tn(xc, dzr)
        dwx_ref[...] += _dot_tn(xc, dzi)
        dba_ref[...] += _colsum(dzr)
        dbx_ref[...] += _colsum(dzi)
        dlam_ref[...] += _colsum(dlog_a * (-LRU_C * r)) * (-_sigmoid(-lam))
        dcb_ref[...] += _colsum(dxc)
        cw = cw_ref[...]
        ext = jnp.concatenate([dxc, dxc_head[...]], axis=0)
        dxb = dxc * cw[3:4]
        for k in range(3):
            dxb = dxb + pltpu.roll(ext, R + 8 - (3 - k), 0)[:R] * cw[k:k + 1]
        dxb_ref[...] = dxb.astype(dxb_ref.dtype)
        dxc_head[...] = dxc[0:8]
        extx = jnp.concatenate([xbprev_ref[...].astype(F32) * has_prev, xb_ref[...].astype(F32)], axis=0)
        incs = []
        for k in range(4):
            e = extx if k == 3 else pltpu.roll(extx, 3 - k, 0)
            incs.append(_colsum(dxc * e[8:]))
        dcw_ref[...] += jnp.concatenate(incs, axis=0)

    plain = pl.BlockSpec((R, LANES), lambda h, j: (ch(j), h))
    plain_prev = pl.BlockSpec((8, LANES), lambda h, j: (jnp.maximum(ch(j) * (R // 8) - 1, 0), h))
    return pl.pallas_call(
        body, name=tag + "_dlru", grid=(LRU_HEADS, nch),
        in_specs=[plain, col(PK_GB // LANES), plain, plain_prev, plain, col(PK_XB // LANES), prev(PK_XB // LANES),
                  cw_spec, wsq, vec, wsq, vec, vec],
        out_specs=[plain, plain, wsq, vec, wsq, vec, vec, cw_spec, vec],
        out_shape=[jax.ShapeDtypeStruct((S, D), BF), jax.ShapeDtypeStruct((S, D), BF),
                   jax.ShapeDtypeStruct((LRU_HEADS, LANES, LANES), F32), jax.ShapeDtypeStruct((1, D), F32),
                   jax.ShapeDtypeStruct((LRU_HEADS, LANES, LANES), F32), jax.ShapeDtypeStruct((1, D), F32),
                   jax.ShapeDtypeStruct((1, D), F32), jax.ShapeDtypeStruct((4, D), F32), jax.ShapeDtypeStruct((1, D), F32)],
        scratch_shapes=[pltpu.VMEM((8, LANES), F32), pltpu.VMEM((8, LANES), F32)], compiler_params=_cparams(2),
    )(dyb0, P, hh, hh, xc, P, P, conv_w, w_a, b_a, w_x, b_x, lam)


CUM_TS = 512
FLASH_T = 512
FLASH_G = 8
NEG = -1e30


def cumlogf_fwd(fl, b_f, tag):
    ts = min(CUM_TS, fl.shape[0])

    def fn(step, i, n, fl, bf, carry):
        x = -_softplus(-(fl + bf))
        row = _iota(x.shape, 0)
        d = 1
        while d < ts:
            x = x + jnp.where(row >= d, pltpu.roll(x, d, 0), 0.0)
            d *= 2

        @pl.when(step == 0)
        def _():
            carry[...] = jnp.zeros_like(carry)

        c = x + carry[0:1]
        carry[...] = jnp.broadcast_to(c[ts - 1:ts], carry.shape)
        return (c,), ()

    return rowwise(fn, [fl], [b_f], [(LANES, F32)], scratch=[pltpu.VMEM((8, LANES), F32)], ts=ts, name=tag + "_cum")[0][0]


def cumlogf_bwd(dc, fl, b_f, tag):
    ts = min(CUM_TS, fl.shape[0])

    def fn(step, i, n, dc, fl, bf, carry):
        row = _iota(dc.shape, 0)
        x = dc
        d = 1
        while d < ts:
            x = x + jnp.where(row < ts - d, pltpu.roll(x, ts - d, 0), 0.0)
            d *= 2

        @pl.when(step == 0)
        def _():
            carry[...] = jnp.zeros_like(carry)

        g = x + carry[0:1]
        carry[...] = jnp.broadcast_to(g[0:1], carry.shape)
        dfl = g * _sigmoid(-(fl + bf))
        return (dfl,), (_colsum(dfl),)

    (dfl,), (db,) = rowwise(fn, [dc, fl], [b_f], [(LANES, F32)], [((1, LANES), F32)], scratch=[pltpu.VMEM((8, LANES), F32)],
                            ts=ts, name=tag + "_dcum", reverse=True)
    return dfl, db


FOX_KA = 80


def _split3(c):
    c = c.astype(F32)
    hi = lax.reduce_precision(c, 8, 7)
    r = c - hi
    mid = lax.reduce_precision(r, 8, 7)
    return hi.astype(BF), mid.astype(BF), (r - mid).astype(BF)


def fox_operands(q, k, ch):
    H, S, hd = q.shape
    qs = q * (FOX_HD ** -0.5)
    pieces = [p.astype(F32) for p in _split3(ch)]
    pad = FOX_KA - hd

    def unit(i):
        return (jnp.arange(FOX_KA) == hd + i).astype(F32)

    qa = jnp.pad(qs.transpose(0, 2, 1).astype(F32), ((0, 0), (0, pad), (0, 0)))
    ka = jnp.pad(k.astype(F32), ((0, 0), (0, 0), (0, pad)))
    for i, p in enumerate(pieces):
        qa = qa + p[:, None, :] * unit(i)[None, :, None] + unit(3 + i)[None, :, None]
        ka = ka + unit(i)[None, None, :] - p[:, :, None] * unit(3 + i)[None, None, :]
    return qa.astype(BF), ka.astype(BF), qs


def _causal(s):
    return jnp.where(_iota(s.shape, 0) <= _iota(s.shape, 1), s, NEG)


def flash_fwd(qa, ka, vt, tag):
    H, hd, S = vt.shape
    T = min(FLASH_T, S)
    nb = S // T

    pairs = [(qi, ki) for qi in range(nb) for ki in range(qi + 1)]
    q_tab = jnp.asarray([p[0] for p in pairs], jnp.int32)
    k_tab = jnp.asarray([p[1] for p in pairs], jnp.int32)

    def body(q_tab_ref, k_tab_ref, ka_ref, qa_ref, vt_ref, o_ref, lse_ref, m_s, l_s, acc):
        qi, ki = q_tab_ref[pl.program_id(1)], k_tab_ref[pl.program_id(1)]

        @pl.when(ki == 0)
        def _():
            m_s[...] = jnp.full_like(m_s, NEG)
            l_s[...] = jnp.zeros_like(l_s)
            acc[...] = jnp.zeros_like(acc)

        def step(diagonal):
            scores = []
            for g in range(FLASH_G):
                s = _dot_nn(ka_ref[g], qa_ref[g])
                scores.append(_causal(s) if diagonal else s)
            for g in range(FLASH_G):
                s = scores[g]
                m_new = jnp.maximum(m_s[g], jnp.max(s, axis=0, keepdims=True))
                alpha = jnp.exp(m_s[g] - m_new)
                p = jnp.exp(s - m_new)
                l_s[g] = alpha * l_s[g] + jnp.sum(p, axis=0, keepdims=True)
                p_hi = p.astype(BF)
                p_lo = p - p_hi.astype(F32)
                v = vt_ref[g]
                acc[g] = alpha * acc[g] + (_dot_nn(v, p_hi) + _dot_nn(v, p_lo))
                m_s[g] = m_new

        @pl.when(ki < qi)
        def _():
            step(False)

        @pl.when(ki == qi)
        def _():
            step(True)
            o_ref[...] = acc[...] / l_s[...]
            lse_ref[...] = m_s[...] + jnp.log(l_s[...])

    G = FLASH_G
    grid_spec = pltpu.PrefetchScalarGridSpec(
        num_scalar_prefetch=2, grid=(H // G, len(pairs)),
        in_specs=[pl.BlockSpec((G, T, FOX_KA), lambda h, p, qt, kt: (h, kt[p], 0)),
                  pl.BlockSpec((G, FOX_KA, T), lambda h, p, qt, kt: (h, 0, qt[p])),
                  pl.BlockSpec((G, hd, T), lambda h, p, qt, kt: (h, 0, kt[p]))],
        out_specs=[pl.BlockSpec((G, hd, T), lambda h, p, qt, kt: (h, 0, qt[p])),
                   pl.BlockSpec((G, 1, T), lambda h, p, qt, kt: (h, 0, qt[p]))],
        scratch_shapes=[pltpu.VMEM((G, 1, T), F32), pltpu.VMEM((G, 1, T), F32), pltpu.VMEM((G, hd, T), F32)])
    return pl.pallas_call(
        body, name=tag + "_flash", grid_spec=grid_spec,
        out_shape=[jax.ShapeDtypeStruct((H, hd, S), F32), jax.ShapeDtypeStruct((H, 1, S), F32)],
        compiler_params=_cparams(2))(q_tab, k_tab, ka, qa, vt)


def flash_bwd(qa, ka, qs, kt, v, ot, do, dot_, lse, tag):
    H, S, hd = v.shape
    T = min(FLASH_T, S)
    nb = S // T

    pairs = [(qi, ki) for ki in range(nb) for qi in range(ki, nb)]
    q_tab = jnp.asarray([p[0] for p in pairs], jnp.int32)
    k_tab = jnp.asarray([p[1] for p in pairs], jnp.int32)

    def body(q_tab_ref, k_tab_ref, ka_ref, qa_ref, qs_ref, kt_ref, v_ref, ot_ref, do_ref, dot_ref, lse_ref,
             dqt_ref, dk_ref, dv_ref, dc_ref, dk_acc, dv_acc, dc_acc):
        qi, ki = q_tab_ref[pl.program_id(1)], k_tab_ref[pl.program_id(1)]

        @pl.when(pl.program_id(1) == 0)
        def _():
            dqt_ref[...] = jnp.zeros_like(dqt_ref)

        @pl.when(qi == ki)
        def _():
            dk_acc[...] = jnp.zeros_like(dk_acc)
            dv_acc[...] = jnp.zeros_like(dv_acc)
            dc_acc[...] = jnp.zeros_like(dc_acc)

        def step(diagonal):
            for g in range(FLASH_G):
                s = _dot_nn(ka_ref[g], qa_ref[g])
                if diagonal:
                    s = _causal(s)
                p = jnp.exp(s - lse_ref[g])
                dot_v = dot_ref[g]
                dp = _dot_nn(v_ref[g], dot_v)
                delta = jnp.sum(dot_v.astype(F32) * ot_ref[g], axis=0, keepdims=True)
                ds = p * (dp - delta)
                part = ds[:, 0:LANES]
                for j in range(1, T // LANES):
                    part = part + ds[:, j * LANES:(j + 1) * LANES]
                dc_acc[g] += part
                dsb = ds.astype(BF)
                dv_acc[g] += _dot_nn(p, do_ref[g])
                dk_acc[g] += _dot_nn(dsb, qs_ref[g])
                dqt_ref[g, qi] += _dot_nn(kt_ref[g], dsb)

        @pl.when(qi > ki)
        def _():
            step(False)

        @pl.when(qi == ki)
        def _():
            step(True)

        @pl.when(qi == nb - 1)
        def _():
            dk_ref[...] = dk_acc[...]
            dv_ref[...] = dv_acc[...]
            for g in range(FLASH_G):
                dc_ref[g] = -jnp.sum(dc_acc[g].T, axis=0, keepdims=True)

    qblk = lambda h, p, qt, kt_: (h, qt[p], 0)
    qblk_t = lambda h, p, qt, kt_: (h, 0, qt[p])
    kblk = lambda h, p, qt, kt_: (h, kt_[p], 0)
    kblk_t = lambda h, p, qt, kt_: (h, 0, kt_[p])
    G = FLASH_G
    grid_spec = pltpu.PrefetchScalarGridSpec(
        num_scalar_prefetch=2, grid=(H // G, len(pairs)),
        in_specs=[pl.BlockSpec((G, T, FOX_KA), kblk), pl.BlockSpec((G, FOX_KA, T), qblk_t),
                  pl.BlockSpec((G, T, hd), qblk), pl.BlockSpec((G, hd, T), kblk_t), pl.BlockSpec((G, T, hd), kblk),
                  pl.BlockSpec((G, hd, T), qblk_t), pl.BlockSpec((G, T, hd), qblk), pl.BlockSpec((G, hd, T), qblk_t),
                  pl.BlockSpec((G, 1, T), qblk_t)],
        out_specs=[pl.BlockSpec((G, nb, hd, T), lambda h, p, qt, kt_: (h, 0, 0, 0)), pl.BlockSpec((G, T, hd), kblk),
                   pl.BlockSpec((G, T, hd), kblk), pl.BlockSpec((G, 1, T), kblk_t)],
        scratch_shapes=[pltpu.VMEM((G, T, hd), F32), pltpu.VMEM((G, T, hd), F32), pltpu.VMEM((G, T, LANES), F32)])
    return pl.pallas_call(
        body, name=tag + "_dflash", grid_spec=grid_spec,
        out_shape=[jax.ShapeDtypeStruct((H, nb, hd, T), F32), jax.ShapeDtypeStruct((H, S, hd), F32),
                   jax.ShapeDtypeStruct((H, S, hd), F32), jax.ShapeDtypeStruct((H, 1, S), F32)],
        compiler_params=_cparams(2))(q_tab, k_tab, ka, qa, qs, kt, v, ot, do, dot_, lse)


def _to_heads(x2d):
    S = x2d.shape[0]
    return x2d.reshape(S, FOX_HEADS, FOX_HD).transpose(1, 0, 2)


def _from_heads(x3d):
    S = x3d.shape[1]
    return x3d.transpose(1, 0, 2).reshape(S, FOX_W)


def _gl_rows(P):
    return [Rows(P, D, PK_GL // D + k) for k in range(3)]


def merge_fwd(P, ya, yb, yc, b_gate, tag):
    def fn(step, i, n, g0, g1, g2, ya, yb, yc, b):
        out = 0.0
        for k, (gl, y) in enumerate(((g0, ya), (g1, yb), (g2, yc))):
            out = out + _sigmoid(gl.astype(F32) + b[:, k * D:(k + 1) * D]) * y.astype(F32)
        return (out,), ()
    return rowwise(fn, _gl_rows(P) + [ya, yb, yc], [b_gate], [(D, BF)], ts=512, name=tag + "_merge")[0][0]


def merge_bwd(dm, P, ya, yb, yc, b_gate, tag):
    def fn(step, i, n, dm, g0, g1, g2, ya, yb, yc, b):
        dm = dm.astype(F32)
        dys, dgls = [], []
        for k, (gl, y) in enumerate(((g0, ya), (g1, yb), (g2, yc))):
            g = _sigmoid(gl.astype(F32) + b[:, k * D:(k + 1) * D])
            dys.append(dm * g)
            dgls.append(dm * y.astype(F32) * (g * (1.0 - g)))
        dgl = jnp.concatenate(dgls, axis=1)
        return (dys[0], dys[1], dys[2], dgl), (_colsum(dgl),)
    (dya, dyb, dyc, dgl), (db,) = rowwise(
        fn, [dm] + _gl_rows(P) + [ya, yb, yc], [b_gate], [(D, BF), (D, BF), (D, BF), (3 * D, BF)], [((1, 3 * D), F32)],
        ts=256, name=tag + "_dmerge")
    return dya, dyb, dyc, dgl, db


def _xattn_probs(qh, kh):
    s = _dot_nt(qh, kh) * (X_HD ** -0.5)
    e = jnp.exp(s - jnp.max(s, axis=1, keepdims=True))
    return e / jnp.sum(e, axis=1, keepdims=True)


def xattn_fwd(q, kv, tag):
    def fn(step, i, n, q, k, v):
        os = []
        for h in range(X_HEADS):
            sl = slice(h * X_HD, (h + 1) * X_HD)
            os.append(_dot_nn(_xattn_probs(q[:, sl], k[:, sl]), v[:, sl]))
        return (jnp.concatenate(os, axis=1),), ()

    return _xattn_call(fn, [q], kv, [(D, BF)], [], tag + "_xattn")[0][0]


def _xattn_call(fn, rows, kv, outs, accs, name):
    S, ts, M = rows[0].shape[0], 512, kv.shape[0]
    ts = min(ts, S)
    nr, no, na = len(rows), len(outs), len(accs)

    def body(*refs):
        in_refs, out_refs, acc_refs = refs[:nr + 2], refs[nr + 2:nr + 2 + no], refs[nr + 2 + no:]
        step = pl.program_id(0)
        o, inc = fn(step, step, S // ts, *[r[...] for r in in_refs])
        for ref, val in zip(out_refs, o):
            ref[...] = val.astype(ref.dtype)
        if na:
            @pl.when(step == 0)
            def _():
                for ref in acc_refs:
                    ref[...] = jnp.zeros_like(ref)
            for ref, val in zip(acc_refs, inc):
                ref[...] += val

    row = pl.BlockSpec((ts, D), lambda s: (s, 0))
    res = pl.pallas_call(
        body, name=name, grid=(S // ts,),
        in_specs=[row] * nr + [pl.BlockSpec((M, D), lambda s: (0, 0)), pl.BlockSpec((M, D), lambda s: (0, 1))],
        out_specs=[row] * no + [pl.BlockSpec(tuple(s), lambda s_: (0, 0)) for s, _ in accs],
        out_shape=[jax.ShapeDtypeStruct((S, w), dt) for w, dt in outs] + [jax.ShapeDtypeStruct(tuple(s), dt) for s, dt in accs],
        compiler_params=_cparams(1))(*rows, kv, kv)
    return res[:no], res[no:]


def xattn_bwd(q, kv, do, tag):
    def fn(step, i, n, q, do, k, v):
        dqs, dks, dvs = [], [], []
        for h in range(X_HEADS):
            sl = slice(h * X_HD, (h + 1) * X_HD)
            p = _xattn_probs(q[:, sl], k[:, sl])
            dp = _dot_nt(do[:, sl], v[:, sl])
            ds = (p * (dp - jnp.sum(p * dp, axis=1, keepdims=True)) * (X_HD ** -0.5)).astype(BF)
            dqs.append(_dot_nn(ds, k[:, sl]))
            dks.append(_dot_tn(ds, q[:, sl]))
            dvs.append(_dot_tn(p, do[:, sl]))
        return (jnp.concatenate(dqs, axis=1),), (jnp.concatenate(dks + dvs, axis=1),)

    (dq,), (dkv,) = _xattn_call(fn, [q, do], kv, [(D, BF)], [((kv.shape[0], 2 * D), F32)], tag + "_dxattn")
    return dq, dkv


def layer_fwd(x, mem, w, tag):
    x1, s_ffn1 = ffn_fwd(x, w["g_ffn1"], w["ffn1_in"], w["ffn1_out"], tag + "_ffn1")
    u = rms_fwd(x1, w["g_mix"], name=tag + "_mixrms")
    P = matmul(u, w["wp"], "nn", name=tag + "_proj")
    fl = matmul(u, w["wfl"], "nn", out_dtype=F32, name=tag + "_projf")
    d, ya0 = pool_fwd(P, w["w_pool"], w["pool_scale"], tag)
    ya = matmul(ya0, w["up_a"], "nn", name=tag + "_upa")
    xc, hh, yb0 = lru_fwd(P, w["conv_w"], w["conv_b"], w["w_rg_a"], w["b_rg_a"], w["w_rg_x"], w["b_rg_x"], w["lam"], tag)
    yb = matmul(yb0, w["up_b"], "nn", name=tag + "_upb")
    c = cumlogf_fwd(fl, w["b_f"], tag)
    q, k, v = (_to_heads(P[:, o:o + FOX_W]) for o in (PK_Q, PK_K, PK_V))
    qa, ka, qs = fox_operands(q, k, c[:, :FOX_HEADS].T)
    ot, lse = flash_fwd(qa, ka, v.transpose(0, 2, 1), tag)
    o2 = ot.transpose(2, 0, 1).reshape(-1, FOX_W)
    yc = matmul(o2, w["up_c"], "nn", name=tag + "_upc")
    merged = merge_fwd(P, ya, yb, yc, w["b_gate"], tag)
    x2 = matmul(merged, w["w_o"], "nn", out_dtype=F32, res=x1, name=tag + "_wo")
    hq = rms_fwd(x2, w["g_cross"], name=tag + "_xrms")
    qx = matmul(hq, w["xq"], "nn", name=tag + "_xq")
    mn = rms_fwd(mem, w["g_mem"], name=tag + "_mrms")
    kv = matmul(mn, w["xkv"], "nn", name=tag + "_xkv")
    ox = xattn_fwd(qx, kv, tag)
    x3 = matmul(ox, w["xo"], "nn", out_dtype=F32, res=x2, name=tag + "_xo")
    x4, s_ffn2 = ffn_fwd(x3, w["g_ffn2"], w["ffn2_in"], w["ffn2_out"], tag + "_ffn2")
    saved = dict(ffn1=s_ffn1, ffn2=s_ffn2, x1=x1, u=u, P=P, fl=fl, d=d, ya0=ya0, ya=ya, xc=xc, hh=hh, yb0=yb0, yb=yb,
                 qa=qa, ka=ka, qs=qs, k=k, v=v, ot=ot, lse=lse, o2=o2, yc=yc, merged=merged, x2=x2, hq=hq, qx=qx,
                 mn=mn, kv=kv, ox=ox)
    return x4, saved


def layer_bwd(dx4, mem, w, s, tag):
    g = {}
    dx3, g["g_ffn2"], g["w_ffn2_in"], g["w_ffn2_out"] = ffn_bwd(dx4, s["ffn2"], w["g_ffn2"], w["ffn2_in"], w["ffn2_out"], tag + "_ffn2")
    dox = matmul(dx3, w["xo"], "nt", name=tag + "_dox")
    g["w_xo"] = matmul(s["ox"], dx3, "tn", name=tag + "_dwxo")
    dqx, dkv = xattn_bwd(s["qx"], s["kv"], dox, tag)
    g["w_xq"] = matmul(s["hq"], dqx, "tn", name=tag + "_dwxq")
    dhq = matmul(dqx, w["xq"], "nt", name=tag + "_dhq")
    dx2, g["g_cross"] = rms_bwd(s["x2"], w["g_cross"], dhq, dx3, name=tag + "_dxrms")
    g["w_xkv"] = matmul(s["mn"], dkv, "tn", name=tag + "_dwxkv")
    dmn = matmul(dkv, w["xkv"], "nt", name=tag + "_dmn")
    _, g["g_mem"] = rms_bwd(mem, w["g_mem"], dmn, None, name=tag + "_dmrms")
    P = s["P"]
    dmerged = matmul(dx2, w["w_o"], "nt", name=tag + "_dmerged")
    g["w_o"] = matmul(s["merged"], dx2, "tn", name=tag + "_dwo")
    dya, dyb, dyc, dgl, g["b_gate"] = merge_bwd(dmerged, P, s["ya"], s["yb"], s["yc"], w["b_gate"], tag)
    dya0 = matmul(dya, w["up_a"], "nt", name=tag + "_dya0")
    g["w_up_a"] = matmul(s["ya0"], dya, "tn", name=tag + "_dwupa")
    dxa, g["w_pool"], g["pool_scale"] = pool_bwd(dya0, s["d"], w["w_pool"], w["pool_scale"], tag)
    dyb0 = matmul(dyb, w["up_b"], "nt", name=tag + "_dyb0")
    g["w_up_b"] = matmul(s["yb0"], dyb, "tn", name=tag + "_dwupb")
    (dgb, dxb, g["w_rg_a"], g["b_rg_a"], g["w_rg_x"], g["b_rg_x"], g["lru_lambda"], g["conv_w"], g["conv_b"]) = lru_bwd(
        dyb0, P, s["hh"], s["xc"], w["conv_w"], w["w_rg_a"], w["b_rg_a"], w["w_rg_x"], w["b_rg_x"], w["lam"], tag)
    do2 = matmul(dyc, w["up_c"], "nt", name=tag + "_do2")
    g["w_up_c"] = matmul(s["o2"], dyc, "tn", name=tag + "_dwupc")
    do = _to_heads(do2)
    dqt, dk, dv, dck = flash_bwd(s["qa"], s["ka"], s["qs"], s["k"].transpose(0, 2, 1), s["v"], s["ot"], do,
                                 do.transpose(0, 2, 1), s["lse"], tag)
    dq = (dqt * (FOX_HD ** -0.5)).transpose(0, 1, 3, 2).reshape(dk.shape)
    dc = jnp.pad(dck[:, 0, :].T, ((0, 0), (0, LANES - FOX_HEADS)))
    dfl, g["b_f"] = cumlogf_bwd(dc, s["fl"], w["b_f"], tag)
    dP = jnp.concatenate([dxb, dgb, dgl, dxa] + [_from_heads(t).astype(BF) for t in (dq, dk, dv)], axis=1)
    du = matmul(dP, w["wp"], "nt", out_dtype=F32, name=tag + "_du")
    du = matmul(dfl, w["wfl"], "nt", out_dtype=F32, res=du, name=tag + "_duf")
    g["wp"] = matmul(s["u"], dP, "tn", name=tag + "_dwp")
    g["wfl"] = matmul(s["u"], dfl, "tn", out_dtype=F32, name=tag + "_dwfl")
    dx1, g["g_mix"] = rms_bwd(s["x1"], w["g_mix"], du, dx2, name=tag + "_dmixrms")
    dx0, g["g_ffn1"], g["w_ffn1_in"], g["w_ffn1_out"] = ffn_bwd(dx1, s["ffn1"], w["g_ffn1"], w["ffn1_in"], w["ffn1_out"], tag + "_ffn1")
    return dx0, g


def loss_head(x, target, g_final):
    def fn(step, i, n, x, t, g):
        r = _rstd(x)
        xhat = x * r
        e = xhat * g - t
        dy = e * (1.0 / D)
        gd = dy * g
        dx = r * (gd - xhat * jnp.mean(xhat * gd, axis=-1, keepdims=True))
        loss = 0.5 * jnp.sum(jnp.mean(e * e, axis=-1, keepdims=True), axis=0, keepdims=True)
        return (dx,), (jnp.broadcast_to(loss, (1, LANES)), _colsum(dy * xhat))

    (dx,), (loss, dg) = rowwise(fn, [x, target], [g_final], [(D, F32)], [((1, LANES), F32), ((1, D), F32)], ts=512, name="loss_head")
    return loss[0, 0], dx, dg


BIG = (("w_ffn1_in", (D, 2 * DFF), 1), ("w_ffn1_out", (DFF, D), 0), ("w_in", (D, 7176), 1), ("w_up_a", (POOL_W, D), 1),
       ("conv_w", (4, D), 1), ("w_up_b", (D, D), 0), ("w_up_c", (FOX_W, D), 1), ("w_o", (D, D), 0), ("w_xq", (D, D), 0),
       ("w_xkv", (D, 2 * D), 1), ("w_xo", (D, D), 0), ("w_ffn2_in", (D, 2 * DFF), 1), ("w_ffn2_out", (DFF, D), 0))
SMALL = (("g_ffn1", (D,)), ("g_mix", (D,)), ("b_f", (FOX_HEADS,)), ("b_gate", (3 * D,)), ("w_pool", (4, LANES, LANES)),
         ("pool_scale", (POOL_W,)), ("conv_b", (D,)), ("w_rg_a", (LRU_HEADS, LANES, LANES)), ("b_rg_a", (D,)),
         ("w_rg_x", (LRU_HEADS, LANES, LANES)), ("b_rg_x", (D,)), ("lru_lambda", (D,)), ("g_cross", (D,)), ("g_mem", (D,)),
         ("g_ffn2", (D,)))
ORDER = ("g_ffn1", "w_ffn1_in", "w_ffn1_out", "g_mix", "w_in", "b_f", "b_gate", "w_pool", "pool_scale", "w_up_a", "conv_w",
         "conv_b", "w_rg_a", "b_rg_a", "w_rg_x", "b_rg_x", "lru_lambda", "w_up_b", "w_up_c", "w_o", "g_cross", "g_mem", "w_xq",
         "w_xkv", "w_xo", "g_ffn2", "w_ffn2_in", "w_ffn2_out", "g_final")

IN_SPLIT = (("xa", 0, 512), ("xb", 512, 1024), ("gb", 1536, 1024), ("q", 2560, 512), ("k", 3072, 512), ("v", 3584, 512),
            ("fl", 4096, 8), ("gl", 4104, 3072))
PACK_ORDER = ("xb", "gb", "gl", "xa", "q", "k", "v")


def _shard_shape(shape, axis):
    s = list(shape)
    s[axis] //= N_CHIPS
    return (DEPTH, *s)


def _round_up(n, m):
    return -(-n // m) * m


SMALL_SEG_ROWS = 8


def _small_rows(n_elements):
    return _round_up(-(-n_elements // D), SMALL_SEG_ROWS)


def _pack_small(tree, names, row_multiple=SMALL_SEG_ROWS, fill=0.0):
    parts = []
    for n in names:
        flat = tree[n].astype(F32).reshape(-1)
        rows = _small_rows(flat.shape[0])
        parts.append(jnp.pad(flat, (0, rows * D - flat.shape[0]), constant_values=fill).reshape(rows, D))
    total = sum(p.shape[0] for p in parts)
    if total % row_multiple:
        parts.append(jnp.full((_round_up(total, row_multiple) - total, D), fill, F32))
    return jnp.concatenate(parts, axis=0)


def _unpack_small(buf, like, names):
    out, off = {}, 0
    for n in names:
        sh = like[n].shape
        rows = _small_rows(math.prod(sh))
        out[n] = buf[off:off + rows].reshape(-1)[:math.prod(sh)].reshape(sh)
        off += rows
    return out


def _from_chip_major(g, shape, axis):
    return jnp.moveaxis(g, 1, axis + 1).reshape(DEPTH, *shape)


def _to_chip_major(full, shape, axis):
    sh = list(shape)
    sh[axis:axis + 1] = [N_CHIPS, shape[axis] // N_CHIPS]
    return jnp.moveaxis(full.reshape(DEPTH, *sh), axis + 1, 1)


def _layer_weights(full, small, l):
    w_in = full["w_in"][l]
    cols = {n: w_in[:, o:o + s] for n, o, s in IN_SPLIT}
    w = dict(ffn1_in=full["w_ffn1_in"][l], ffn1_out=full["w_ffn1_out"][l], ffn2_in=full["w_ffn2_in"][l], ffn2_out=full["w_ffn2_out"][l],
             wp=jnp.concatenate([cols[n] for n in PACK_ORDER], axis=1),
             wfl=jnp.pad(cols["fl"], ((0, 0), (0, LANES - FOX_HEADS))),
             up_a=full["w_up_a"][l], up_b=full["w_up_b"][l], up_c=full["w_up_c"][l], w_o=full["w_o"][l], xq=full["w_xq"][l],
             xkv=full["w_xkv"][l], xo=full["w_xo"][l], conv_w=full["conv_w"][l])
    for n in ("g_ffn1", "g_mix", "b_gate", "pool_scale", "conv_b", "b_rg_a", "b_rg_x", "g_cross", "g_mem", "g_ffn2"):
        w[n] = small[n][l].reshape(1, -1)
    w["lam"] = small["lru_lambda"][l].reshape(1, -1)
    w["b_f"] = jnp.pad(small["b_f"][l], (0, LANES - FOX_HEADS)).reshape(1, LANES)
    for n in ("w_pool", "w_rg_a", "w_rg_x"):
        w[n] = small[n][l]
    return w


def _unpack_w_in_grad(gwp, gwfl):
    pk = {}
    off = 0
    sizes = {n: s for n, _, s in IN_SPLIT}
    for n in PACK_ORDER:
        pk[n] = gwp[:, off:off + sizes[n]]
        off += sizes[n]
    pk["fl"] = gwfl[:, :FOX_HEADS].astype(gwp.dtype)
    return jnp.concatenate([pk[n] for n, _, _ in IN_SPLIT], axis=1)


def _place():
    x, y, c = lax.axis_index("x"), lax.axis_index("y"), lax.axis_index("c")
    others = [(1 - x, y), (x, 1 - y), (1 - x, 1 - y)]
    return x, y, c, others


def _rcopy(src, dst, send_sems, recv_sems, k, to):
    return pltpu.make_async_remote_copy(src_ref=src, dst_ref=dst, send_sem=send_sems.at[k], recv_sem=recv_sems.at[k],
                                        device_id=to, device_id_type=MESH)


def _comm_call(body, name, ins, out_shape, n_sems):
    return pl.pallas_call(
        body, name=name, in_specs=[ANY] * len(ins), out_specs=[ANY] * len(out_shape), out_shape=out_shape,
        scratch_shapes=[pltpu.SemaphoreType.DMA((n_sems,)), pltpu.SemaphoreType.DMA((n_sems,)), pltpu.SemaphoreType.DMA((n_sems,))],
    )(*ins)


def allgather_same_core(bufs):
    nb = len(bufs)

    def body(*refs):
        ins, outs = refs[:nb], refs[nb:2 * nb]
        send_sems, recv_sems, local_sems = refs[2 * nb:]
        x, y, c, others = _place()
        me = 2 * x + y
        sends = []
        for b in range(nb):
            for j, (ox, oy) in enumerate(others):
                cp = _rcopy(ins[b], outs[b].at[me], send_sems, recv_sems, 3 * b + j, (ox, oy, c))
                cp.start()
                sends.append(cp)
        for b in range(nb):
            for j, (ox, oy) in enumerate(others):
                theirs = outs[b].at[2 * ox + oy]
                _rcopy(theirs, theirs, send_sems, recv_sems, 3 * b + j, (ox, oy, c)).wait_recv()
        for cp in sends:
            cp.wait_send()

    shapes = [jax.ShapeDtypeStruct((N_CHIPS, *b.shape), b.dtype) for b in bufs]
    outs = _comm_call(body, "allgather_same_core", list(bufs), shapes, 3 * nb)
    return [_with_own_block(o, b) for o, b in zip(outs, bufs)]


def _with_own_block(gathered, mine):
    me = 2 * lax.axis_index("x") + lax.axis_index("y")
    return lax.dynamic_update_index_in_dim(gathered, mine, me, axis=0)


def gather_weights(shards):
    nw = len(shards)

    def body(*refs):
        ins, outs = refs[:nw], refs[nw:2 * nw]
        send_sems, recv_sems, _ = refs[2 * nw:]
        x, y, c, _ = _place()
        fx, fy = 1 - c, c
        first = (x + fx - 2 * x * fx, y + fy - 2 * y * fy)
        second = (x + fy - 2 * x * fy, y + fx - 2 * y * fx)
        me, c1, c2, cd = 2 * x + y, 2 * first[0] + first[1], 2 * second[0] + second[1], 2 * (1 - x) + (1 - y)
        sibling = (x, y, 1 - c)
        started = []

        def send(w, k, src, dst, to):
            cp = _rcopy(src, dst, send_sems, recv_sems, 6 * w + k, to)
            cp.start()
            started.append(cp)

        def landed(w, k, block):
            _rcopy(block, block, send_sems, recv_sems, 6 * w + k, sibling).wait_recv()

        for w in range(nw):
            send(w, 0, ins[w].at[c], outs[w].at[c, me], (*first, c))
            send(w, 1, ins[w].at[c], outs[w].at[c, me], (*second, c))
        for w in range(nw):
            block = outs[w].at[c, c1]
            landed(w, 0, block)
            send(w, 2, block, block, (*second, c))
            send(w, 3, block, block, sibling)
        for w in range(nw):
            block = outs[w].at[c, c2]
            landed(w, 1, block)
            send(w, 4, block, block, sibling)
        for w in range(nw):
            block = outs[w].at[c, cd]
            landed(w, 2, block)
            send(w, 5, block, block, sibling)
        for w in range(nw):
            landed(w, 3, outs[w].at[1 - c, c2])
            landed(w, 4, outs[w].at[1 - c, c1])
            landed(w, 5, outs[w].at[1 - c, cd])
        for cp in started:
            cp.wait_send()

    shapes = [jax.ShapeDtypeStruct((DEPTH, N_CHIPS, *s.shape[1:]), s.dtype) for s in shards]
    outs = _comm_call(body, "gather_weights", list(shards), shapes, 6 * nw)
    me = 2 * lax.axis_index("x") + lax.axis_index("y")
    return [lax.dynamic_update_index_in_dim(o, s, me, axis=1) for o, s in zip(outs, shards)]


def sibling_layers(bufs):
    nb = len(bufs)

    def body(*refs):
        ins, outs = refs[:nb], refs[nb:2 * nb]
        send_sems, recv_sems, _ = refs[2 * nb:]
        x, y, c, _ = _place()
        work = []
        for b in range(nb):
            cp = _rcopy(ins[b].at[1 - c], outs[b], send_sems, recv_sems, b, (x, y, 1 - c))
            cp.start()
            work.append(cp)
        for cp in work:
            cp.wait()

    shapes = [jax.ShapeDtypeStruct(b.shape[1:], b.dtype) for b in bufs]
    return _comm_call(body, "sibling_layers", list(bufs), shapes, nb)


def chip_blocks(bufs):
    nb = len(bufs)

    def body(*refs):
        ins, outs = refs[:nb], refs[nb:2 * nb]
        send_sems, recv_sems, _ = refs[2 * nb:]
        x, y, c, others = _place()
        sends = []
        for b in range(nb):
            for j, (ox, oy) in enumerate(others):
                cp = _rcopy(ins[b].at[2 * ox + oy], outs[b].at[j], send_sems, recv_sems, 3 * b + j, (ox, oy, c))
                cp.start()
                sends.append(cp)
        for cp in sends:
            cp.wait()

    shapes = [jax.ShapeDtypeStruct((3, *b.shape[1:]), b.dtype) for b in bufs]
    return _comm_call(body, "chip_blocks", list(bufs), shapes, 3 * nb)


def sibling_other_layer(bufs):
    nb = len(bufs)

    def body(*refs):
        ins, outs = refs[:nb], refs[nb:2 * nb]
        send_sems, recv_sems, _ = refs[2 * nb:]
        x, y, c, _ = _place()
        work = []
        for b in range(nb):
            cp = _rcopy(ins[b], outs[b].at[c], send_sems, recv_sems, b, (x, y, 1 - c))
            cp.start()
            work.append(cp)
        for b, cp in enumerate(work):
            cp.wait_send()
            theirs = outs[b].at[1 - c]
            _rcopy(theirs, theirs, send_sems, recv_sems, b, (x, y, 1 - c)).wait_recv()

    shapes = [jax.ShapeDtypeStruct((DEPTH, *b.shape), b.dtype) for b in bufs]
    outs = _comm_call(body, "sibling_other_layer", list(bufs), shapes, nb)
    c = lax.axis_index("c")
    return [lax.dynamic_update_index_in_dim(o, b, c, axis=0) for o, b in zip(outs, bufs)]


ADD_BLOCK_BYTES = 2 * 2 ** 20


def add_arrays(parts, out_dtype, name):
    shape = parts[0].shape
    cols = shape[-1]
    rows = math.prod(shape[:-1])

    def fn(step, i, n, *vals):
        acc = vals[0].astype(F32)
        for v in vals[1:]:
            acc = acc + v.astype(F32)
        return (acc,), ()

    ts = rows
    for cand in (2048, 1024, 512, 256, 128, 64, 32, 16):
        if rows % cand == 0 and cand * cols * 4 <= ADD_BLOCK_BYTES:
            ts = cand
            break
    out = rowwise(fn, [p.reshape(rows, cols) for p in parts], outs=[(cols, out_dtype)], ts=ts, name=name)[0][0]
    return out.reshape(shape)


def reduce_gradients(bufs, names):
    c, me = lax.axis_index("c"), 2 * lax.axis_index("x") + lax.axis_index("y")
    theirs = sibling_layers(bufs)
    mine = [lax.dynamic_index_in_dim(b, c, axis=0, keepdims=False) for b in bufs]
    part = [add_arrays([m, t], m.dtype, "rs_add_sibling_" + n) for m, t, n in zip(mine, theirs, names)]
    recv = chip_blocks(part)
    own = [lax.dynamic_index_in_dim(p, me, axis=0, keepdims=False) for p in part]
    red = [add_arrays([o, r[0], r[1], r[2]], F32, "rs_add_chips_" + n) for o, r, n in zip(own, recv, names)]
    return sibling_other_layer(red)


_BC1 = 1.0 - ADAM_B1 ** ADAM_STEP
_BC2 = 1.0 - ADAM_B2 ** ADAM_STEP


def adamw(w, g, m, v, name):
    rows, cols = w.shape

    def fn(step, i, n, w, g, m, v):
        m2 = ADAM_B1 * m + (1.0 - ADAM_B1) * g
        v2 = ADAM_B2 * v + (1.0 - ADAM_B2) * (g * g)
        delta = -ADAM_LR * ((m2 / _BC1) / (jnp.sqrt(v2 / _BC2) + ADAM_EPS) + ADAM_WD * w)
        return (delta, m2, v2), ()

    ts = _pick(rows, (256, 128, 64, 32, 16, 8))
    return rowwise(fn, [w, g, m, v], outs=[(cols, F32)] * 3, ts=ts, name=name)[0]


def local_step(x, mem, target, full, small):
    ws = [_layer_weights(full, small, l) for l in range(DEPTH)]
    saved = []
    h = x
    for l in range(DEPTH):
        h, s = layer_fwd(h, mem, ws[l], "l")
        saved.append(s)
    loss, dh, dg_final = loss_head(h, target, small["g_final"].reshape(1, D))
    grads = [None] * DEPTH
    for l in reversed(range(DEPTH)):
        dh, grads[l] = layer_bwd(dh, mem, ws[l], saved[l], "l")
    out = {}
    for l in range(DEPTH):
        g = grads[l]
        g["w_in"] = _unpack_w_in_grad(g.pop("wp"), g.pop("wfl"))
        g["b_f"] = g["b_f"][:, :FOX_HEADS]
    for name, shape, _ in BIG:
        out[name] = jnp.stack([grads[l][name].reshape(shape) for l in range(DEPTH)])
    for name, shape in SMALL:
        out[name] = jnp.stack([grads[l][name].reshape(shape) for l in range(DEPTH)])
    out["g_final"] = dg_final.reshape(D)
    return loss, dh, out


def kernel(x, mem, g_ffn1, w_ffn1_in, w_ffn1_out, g_mix, w_in, b_f, b_gate, w_pool, pool_scale, w_up_a, conv_w, conv_b, w_rg_a, b_rg_a, w_rg_x, b_rg_x, lru_lambda, w_up_b, w_up_c, w_o, g_cross, g_mem, w_xq, w_xkv, w_xo, g_ffn2, w_ffn2_in, w_ffn2_out, g_final, loss_target, m_g_ffn1, m_w_ffn1_in, m_w_ffn1_out, m_g_mix, m_w_in, m_b_f, m_b_gate, m_w_pool, m_pool_scale, m_w_up_a, m_conv_w, m_conv_b, m_w_rg_a, m_b_rg_a, m_w_rg_x, m_b_rg_x, m_lru_lambda, m_w_up_b, m_w_up_c, m_w_o, m_g_cross, m_g_mem, m_w_xq, m_w_xkv, m_w_xo, m_g_ffn2, m_w_ffn2_in, m_w_ffn2_out, m_g_final, v_g_ffn1, v_w_ffn1_in, v_w_ffn1_out, v_g_mix, v_w_in, v_b_f, v_b_gate, v_w_pool, v_pool_scale, v_w_up_a, v_conv_w, v_conv_b, v_w_rg_a, v_b_rg_a, v_w_rg_x, v_b_rg_x, v_lru_lambda, v_w_up_b, v_w_up_c, v_w_o, v_g_cross, v_g_mem, v_w_xq, v_w_xkv, v_w_xo, v_g_ffn2, v_w_ffn2_in, v_w_ffn2_out, v_g_final):
    args = dict(locals())
    weights = {n: args[n] for n in ORDER}
    m_in = {n: args["m_" + n] for n in ORDER}
    v_in = {n: args["v_" + n] for n in ORDER}
    big_names = [n for n, _, _ in BIG]
    small_names = [n for n, _ in SMALL] + ["g_final"]

    shards = []
    for n in big_names:
        shards += list(_split3(weights[n])) if n == "conv_w" else [weights[n].astype(BF)]
    gathered = iter(gather_weights(shards))
    full = {}
    for n, shape, axis in BIG:
        g = next(gathered)
        if n == "conv_w":
            g = g.astype(F32) + next(gathered).astype(F32) + next(gathered).astype(F32)
        full[n] = _from_chip_major(g, shape, axis)

    small = {n: weights[n] for n in small_names}
    loss, grad_x, contrib = local_step(x[0], mem[0], loss_target[0], full, small)
    loss = lax.psum(loss, ("x", "y", "c"))

    bufs = [_to_chip_major(contrib[n].astype(BF), shape, axis) for n, shape, axis in BIG]
    small_buf = _pack_small(contrib, small_names, DEPTH * N_CHIPS * SMALL_SEG_ROWS).reshape(DEPTH, N_CHIPS, -1, D)
    red = reduce_gradients(bufs + [small_buf], big_names + ["small"])

    grads = dict(zip(big_names, red[:-1]))
    red_small = allgather_same_core([red[-1]])[0].transpose(1, 0, 2, 3).reshape(-1, D)
    grads.update(_unpack_small(red_small, weights, small_names))

    delta, new_m, new_v = {}, {}, {}
    for name in big_names:
        sh = weights[name].shape
        two_d = (-1, sh[-1])
        d, m2, v2 = adamw(weights[name].reshape(two_d), grads[name].reshape(two_d), m_in[name].reshape(two_d),
                          v_in[name].reshape(two_d), "adamw_" + name)
        delta[name], new_m[name], new_v[name] = d.reshape(sh), m2.reshape(sh), v2.reshape(sh)
    outs = adamw(_pack_small(weights, small_names), _pack_small(grads, small_names), _pack_small(m_in, small_names),
                 _pack_small(v_in, small_names, fill=1.0), "adamw_small")
    for t, res in zip((delta, new_m, new_v), outs):
        t.update(_unpack_small(res, weights, small_names))

    return (loss, grad_x[None], *[grads[n] for n in ORDER], *[delta[n] for n in ORDER],
            *[new_m[n] for n in ORDER], *[new_v[n] for n in ORDER])
```

```python
import math

import jax
import jax.numpy as jnp
from jax import lax
from jax.experimental import pallas as pl
from jax.experimental.pallas import tpu as pltpu

F32 = jnp.float32
BF = jnp.bfloat16

D = 1024
DFF = 2816
DEPTH = 2
POOL_W = 512
POOL_WINDOWS = (2, 4, 8, 16)
LRU_HEADS = 8
LRU_C = 8.0
FOX_HEADS = 8
FOX_HD = 64
FOX_W = 512
X_HEADS = 4
X_HD = 256
EPS = 1e-6
LANES = 128
N_CHIPS = 4

ADAM_LR, ADAM_B1, ADAM_B2, ADAM_EPS, ADAM_WD, ADAM_STEP = 0.001, 0.9, 0.999, 1e-08, 0.01, 10

VMEM_LIMIT_BYTES = 56 * 2 ** 20

PK_XB, PK_GB, PK_GL, PK_XA, PK_Q, PK_K, PK_V = 0, 1024, 2048, 5120, 5632, 6144, 6656
PK_W = 7168

MESH = pl.DeviceIdType.MESH
ANY = pl.BlockSpec(memory_space=pl.ANY)


def _cparams(ngrid):
    return pltpu.CompilerParams(dimension_semantics=("arbitrary",) * ngrid, vmem_limit_bytes=VMEM_LIMIT_BYTES)


def _pick(n, cands):
    for c in cands:
        if n % c == 0:
            return c
    return n


def _iota(shape, dim):
    return lax.broadcasted_iota(jnp.int32, shape, dim)


def _sigmoid(x):
    return 1.0 / (1.0 + jnp.exp(-x))


def _softplus(z):
    return jnp.maximum(z, 0.0) + jnp.log1p(jnp.exp(-jnp.abs(z)))


def _expm1(x):
    small = jnp.abs(x) < 0.25
    xs = jnp.where(small, x, 0.0)
    poly = xs * (1.0 + xs * (1 / 2 + xs * (1 / 6 + xs * (1 / 24 + xs * (1 / 120 + xs * (1 / 720 + xs * (1 / 5040 + xs * (1 / 40320))))))))
    return jnp.where(small, poly, jnp.exp(x) - 1.0)


_GELU_K = math.sqrt(2.0 / math.pi)


def _gelu_and_grad(x):
    inner = _GELU_K * (x + 0.044715 * x * x * x)
    t = jnp.tanh(inner)
    g = 0.5 * x * (1.0 + t)
    dg = 0.5 * (1.0 + t) + 0.5 * x * (1.0 - t * t) * _GELU_K * (1.0 + 3 * 0.044715 * x * x)
    return g, dg


def _dot(a, b, dims):
    return lax.dot_general(a.astype(BF), b.astype(BF), (dims, ((), ())), preferred_element_type=F32)


def _dot_nn(a, b):
    return _dot(a, b, ((1,), (0,)))


def _dot_nt(a, b):
    return _dot(a, b, ((1,), (1,)))


def _dot_tn(a, b):
    return _dot(a, b, ((0,), (0,)))


def _colsum(x):
    return jnp.sum(x, axis=0, keepdims=True)


_TM = (1024, 1408, 512, 256, 128)
_TN = (1408, 1024, 512, 384, 256, 128)
_TK = (2432, 1408, 1024, 512, 256, 128)


def matmul(a, b, mode, *, out_dtype=None, alpha=1.0, res=None, norm_g=None, name):
    out_dtype = BF if out_dtype is None else out_dtype
    if mode == "nn":
        (M, K), N = a.shape, b.shape[1]
    elif mode == "nt":
        (M, K), N = a.shape, b.shape[0]
    else:
        (K, M), N = a.shape, b.shape[1]
    tm, tn, tk = _pick(M, _TM), _pick(N, _TN), _pick(K, _TK)
    nk = K // tk
    dims = {"nn": ((1,), (0,)), "nt": ((1,), (1,)), "tn": ((0,), (0,))}[mode]
    a_spec = pl.BlockSpec((tk, tm), lambda i, j, k: (k, i)) if mode == "tn" else pl.BlockSpec((tm, tk), lambda i, j, k: (i, k))
    b_spec = pl.BlockSpec((tn, tk), lambda i, j, k: (j, k)) if mode == "nt" else pl.BlockSpec((tk, tn), lambda i, j, k: (k, j))
    o_spec = pl.BlockSpec((tm, tn), lambda i, j, k: (i, j))
    has_res, has_norm = res is not None, norm_g is not None
    assert not has_norm or tn == N, "the norm epilogue needs whole rows"

    def body(*refs):
        a_ref, b_ref = refs[0], refs[1]
        r_ref = refs[2] if has_res else None
        g_ref = refs[2 + has_res] if has_norm else None
        o_ref = refs[2 + has_res + has_norm]
        h_ref = refs[3 + has_res + has_norm] if has_norm else None
        acc = refs[-1]
        k = pl.program_id(2)

        @pl.when(k == 0)
        def _():
            acc[...] = jnp.zeros_like(acc)

        acc[...] += _dot(a_ref[...], b_ref[...], dims)

        @pl.when(k == nk - 1)
        def _():
            r = acc[...]
            if alpha != 1.0:
                r = r * alpha
            if has_res:
                r = r + r_ref[...].astype(F32)
            r = r.astype(o_ref.dtype)
            o_ref[...] = r
            if has_norm:
                rf = r.astype(F32)
                h_ref[...] = ((rf * _rstd(rf)) * g_ref[...]).astype(h_ref.dtype)

    ins = [a, b] + ([res] if has_res else []) + ([norm_g] if has_norm else [])
    specs = [a_spec, b_spec] + ([o_spec] if has_res else []) + ([pl.BlockSpec((1, N), lambda i, j, k: (0, 0))] if has_norm else [])
    out = pl.pallas_call(
        body, name=name, grid=(M // tm, N // tn, nk), in_specs=specs,
        out_specs=[o_spec, o_spec] if has_norm else o_spec,
        out_shape=([jax.ShapeDtypeStruct((M, N), out_dtype), jax.ShapeDtypeStruct((M, N), BF)] if has_norm
                   else jax.ShapeDtypeStruct((M, N), out_dtype)),
        scratch_shapes=[pltpu.VMEM((tm, tn), F32)], compiler_params=_cparams(3))(*ins)
    return tuple(out) if has_norm else out


def matmul_rms_bwd(a, b, x, g, dres, *, res=None, name):
    (M, K), N = a.shape, b.shape[0]
    tm, tk = _pick(M, _TM), _pick(K, _TK)
    nk = K // tk
    has_res, has_dres = res is not None, dres is not None

    def body(*refs):
        a_ref, b_ref, x_ref, g_ref = refs[:4]
        rest = list(refs[4:])
        r_ref = rest.pop(0) if has_res else None
        d_ref = rest.pop(0) if has_dres else None
        dx_ref, dg_ref, acc = rest
        i, k = pl.program_id(0), pl.program_id(1)

        @pl.when(k == 0)
        def _():
            acc[...] = jnp.zeros_like(acc)

        @pl.when(jnp.logical_and(i == 0, k == 0))
        def _():
            dg_ref[...] = jnp.zeros_like(dg_ref)

        acc[...] += _dot_nt(a_ref[...], b_ref[...])

        @pl.when(k == nk - 1)
        def _():
            dh = acc[...]
            if has_res:
                dh = dh + r_ref[...]
            xv = x_ref[...]
            r = _rstd(xv)
            xhat = xv * r
            gd = dh * g_ref[...]
            dx = r * (gd - xhat * jnp.mean(xhat * gd, axis=-1, keepdims=True))
            if has_dres:
                dx = dx + d_ref[...]
            dx_ref[...] = dx
            dg_ref[...] += _colsum(dh * xhat)

    row = pl.BlockSpec((tm, N), lambda i, k: (i, 0))
    vec = pl.BlockSpec((1, N), lambda i, k: (0, 0))
    ins = [a, b, x, g] + ([res] if has_res else []) + ([dres] if has_dres else [])
    specs = ([pl.BlockSpec((tm, tk), lambda i, k: (i, k)), pl.BlockSpec((N, tk), lambda i, k: (0, k)), row, vec]
             + [row] * (has_res + has_dres))
    dx, dg = pl.pallas_call(
        body, name=name, grid=(M // tm, nk), in_specs=specs, out_specs=[row, vec],
        out_shape=[jax.ShapeDtypeStruct((M, N), F32), jax.ShapeDtypeStruct((1, N), F32)],
        scratch_shapes=[pltpu.VMEM((tm, N), F32)], compiler_params=_cparams(2))(*ins)
    return dx, dg


class Rows:
    def __init__(self, arr, w=None, cb=0, halo=None, h=8):
        self.arr, self.w, self.cb, self.halo, self.h = arr, (arr.shape[1] if w is None else w), cb, halo, h


def rowwise(fn, rows, consts=(), outs=(), accs=(), scratch=(), *, ts, name, reverse=False):
    rows = [r if isinstance(r, Rows) else Rows(r) for r in rows]
    S = rows[0].arr.shape[0]
    n = S // ts
    tile = (lambda s: n - 1 - s) if reverse else (lambda s: s)

    def row_spec(r):
        if r.halo is None:
            return pl.BlockSpec((ts, r.w), lambda s: (tile(s), r.cb))
        per, last = ts // r.h, S // r.h - 1
        if r.halo == "prev":
            return pl.BlockSpec((r.h, r.w), lambda s: (jnp.maximum(tile(s) * per - 1, 0), r.cb))
        return pl.BlockSpec((r.h, r.w), lambda s: (jnp.minimum((tile(s) + 1) * per, last), r.cb))

    def whole(shape):
        nd = len(shape)
        return pl.BlockSpec(tuple(shape), lambda s: (0,) * nd)

    nr, nc, no, na = len(rows), len(consts), len(outs), len(accs)

    def body(*refs):
        in_refs, out_refs = refs[:nr + nc], refs[nr + nc:nr + nc + no]
        acc_refs, scr = refs[nr + nc + no:nr + nc + no + na], refs[nr + nc + no + na:]
        step = pl.program_id(0)
        o, inc = fn(step, tile(step), n, *[r[...] for r in in_refs], *scr)
        for ref, val in zip(out_refs, o):
            ref[...] = val.astype(ref.dtype)
        if na:
            @pl.when(step == 0)
            def _():
                for ref in acc_refs:
                    ref[...] = jnp.zeros_like(ref)
            for ref, val in zip(acc_refs, inc):
                ref[...] += val.astype(ref.dtype)

    out_shape = [jax.ShapeDtypeStruct((S, w), dt) for w, dt in outs] + [jax.ShapeDtypeStruct(tuple(s), dt) for s, dt in accs]
    out_specs = [pl.BlockSpec((ts, w), lambda s: (tile(s), 0)) for w, _ in outs] + [whole(s) for s, _ in accs]
    res = pl.pallas_call(
        body, name=name, grid=(n,),
        in_specs=[row_spec(r) for r in rows] + [whole(c.shape) for c in consts],
        out_specs=out_specs, out_shape=out_shape, scratch_shapes=list(scratch),
        compiler_params=_cparams(1))(*[r.arr for r in rows], *consts)
    return res[:no], res[no:]


def _rstd(x):
    return lax.rsqrt(jnp.mean(x * x, axis=-1, keepdims=True) + EPS)


def rms_fwd(x, g, *, name, ts=512):
    def fn(step, i, n, x, g):
        return ((x * _rstd(x)) * g,), ()
    return rowwise(fn, [x], [g], [(D, BF)], ts=min(ts, x.shape[0]), name=name)[0][0]


def rms_bwd(x, g, dh, dres, *, name, ts=512):
    has_res = dres is not None

    def fn(step, i, n, x, dh, *rest):
        g = rest[-1]
        r = _rstd(x)
        xhat = x * r
        dh = dh.astype(F32)
        gd = dh * g
        dx = r * (gd - xhat * jnp.mean(xhat * gd, axis=-1, keepdims=True))
        if has_res:
            dx = dx + rest[0]
        return (dx,), (_colsum(dh * xhat),)

    rows = [x, dh] + ([dres] if has_res else [])
    o, a = rowwise(fn, rows, [g], [(D, F32)], [((1, D), F32)], ts=min(ts, x.shape[0]), name=name)
    return o[0], a[0]


def ffn_fwd(x, h, w_in, w_out, next_g, tag):
    ab = matmul(h, w_in, "nn", name=tag + "_in")

    def act_fn(step, i, n, ab):
        a, b = ab[:, :DFF].astype(F32), ab[:, DFF:].astype(F32)
        return ((a * _sigmoid(a)) * b,), ()

    act = rowwise(act_fn, [ab], outs=[(DFF, BF)], ts=256, name=tag + "_act")[0][0]
    y = matmul(act, w_out, "nn", out_dtype=F32, alpha=0.5, res=x, norm_g=next_g, name=tag + "_out")
    y, h_next = y if next_g is not None else (y, None)
    return y, h_next, (x, h, ab, act)


def ffn_bwd(dy, saved, g, w_in, w_out, tag):
    x, h, ab, act = saved
    dact = matmul(dy, w_out, "nt", alpha=0.5, name=tag + "_dact")
    dw_out = matmul(act, dy, "tn", alpha=0.5, name=tag + "_dwout")

    def dab_fn(step, i, n, ab, dact):
        a, b, dact = ab[:, :DFF].astype(F32), ab[:, DFF:].astype(F32), dact.astype(F32)
        s = _sigmoid(a)
        da = dact * b * (s * (1.0 + a * (1.0 - s)))
        db = dact * (a * s)
        return (jnp.concatenate([da, db], axis=1),), ()

    dab = rowwise(dab_fn, [ab, dact], outs=[(2 * DFF, BF)], ts=256, name=tag + "_dab")[0][0]
    dw_in = matmul(h, dab, "tn", name=tag + "_dwin")
    dx, dg = matmul_rms_bwd(dab, w_in, x, g, dy, name=tag + "_dh")
    return dx, dg, dw_in, dw_out


POOL_TS = 256
POOL_HALO = 16


def pool_fwd(P, w_pool, pool_scale, tag):
    ts = POOL_TS

    def fn(step, i, n, xa, halo, w, scale):
        xa = xa.astype(F32)
        halo = halo.astype(F32) * jnp.where(i > 0, 1.0, 0.0)
        ext = jnp.concatenate([halo, xa], axis=0)
        pos = (i * ts + 1 + _iota((ts, LANES), 0)).astype(F32)
        ds, ys = [], []
        for gi, win in enumerate(POOL_WINDOWS):
            e = ext[:, gi * LANES:(gi + 1) * LANES]
            sh = 1
            while sh < win:
                e = e + pltpu.roll(e, sh, 0)
                sh *= 2
            mean = e[POOL_HALO:] / jnp.minimum(pos, float(win))
            d = mean - xa[:, gi * LANES:(gi + 1) * LANES]
            ds.append(d)
            ys.append(_dot_nn(d, w[gi]))
        d = jnp.concatenate(ds, axis=1)
        return (d, jnp.concatenate(ys, axis=1) * scale), ()

    xa = Rows(P, POOL_W, PK_XA // POOL_W)
    xa_prev = Rows(P, POOL_W, PK_XA // POOL_W, "prev", POOL_HALO)
    (d, ya0), _ = rowwise(fn, [xa, xa_prev], [w_pool, pool_scale], [(POOL_W, BF), (POOL_W, BF)], ts=ts, name=tag + "_pool")
    return d, ya0


def pool_bwd(dya0, d, w_pool, pool_scale, tag):
    ts = POOL_TS
    L = ts + POOL_HALO

    def fn(step, i, n, dya0, dya0_next, d, w, scale):
        d = d.astype(F32)
        dz = jnp.concatenate([dya0.astype(F32), dya0_next.astype(F32) * jnp.where(i < n - 1, 1.0, 0.0)], axis=0) * scale
        pos = (i * ts + 1 + _iota((L, LANES), 0)).astype(F32)
        dxa, dws, dsc = [], [], []
        for gi, win in enumerate(POOL_WINDOWS):
            sl = slice(gi * LANES, (gi + 1) * LANES)
            dzg = dz[:, sl]
            dd = _dot_nt(dzg, w[gi])
            e = dd / jnp.minimum(pos, float(win))
            sh = 1
            while sh < win:
                e = e + pltpu.roll(e, L - sh, 0)
                sh *= 2
            dxa.append(e[:ts] - dd[:ts])
            dws.append(_dot_tn(d[:, sl], dzg[:ts])[None])
            z = _dot_nn(d[:, sl], w[gi])
            dsc.append(_colsum(dya0[:, sl].astype(F32) * z))
        return (jnp.concatenate(dxa, axis=1),), (jnp.concatenate(dws, axis=0), jnp.concatenate(dsc, axis=1))

    (dxa,), (dw_pool, dscale) = rowwise(
        fn, [dya0, Rows(dya0, halo="next", h=POOL_HALO), d], [w_pool, pool_scale],
        [(POOL_W, BF)], [((4, LANES, LANES), F32), ((1, POOL_W), F32)], ts=ts, name=tag + "_dpool")
    return dxa, dw_pool, dscale


LRU_R = 512


def _scan_fwd(A, U):
    R = A.shape[0]
    row = _iota(A.shape, 0)
    d = 1
    while d < R:
        m = row >= d
        A_sh = jnp.where(m, pltpu.roll(A, d, 0), 1.0)
        U_sh = jnp.where(m, pltpu.roll(U, d, 0), 0.0)
        U = A * U_sh + U
        A = A * A_sh
        d *= 2
    return A, U


def _scan_bwd(B, X):
    R = B.shape[0]
    row = _iota(B.shape, 0)
    d = 1
    while d < R:
        m = row < R - d
        B_sh = jnp.where(m, pltpu.roll(B, R - d, 0), 1.0)
        X_sh = jnp.where(m, pltpu.roll(X, R - d, 0), 0.0)
        X = X + B * X_sh
        B = B * B_sh
        d *= 2
    return X


def _lru_gates(xc, wa, ba, wx, bx, lam):
    r = _sigmoid(_dot_nn(xc, wa) + ba)
    ig = _sigmoid(_dot_nn(xc, wx) + bx)
    sp = _softplus(-lam)
    log_a = -LRU_C * r * sp
    a = jnp.exp(log_a)
    mult = jnp.sqrt(-_expm1(2.0 * log_a))
    return r, ig, sp, a, mult


def _lru_specs(S, R, reverse):
    nch = S // R
    ch = (lambda j: nch - 1 - j) if reverse else (lambda j: j)
    per = R // 8

    def col(off):
        return pl.BlockSpec((R, LANES), lambda h, j: (ch(j), off + h))

    def prev(off):
        return pl.BlockSpec((8, LANES), lambda h, j: (jnp.maximum(ch(j) * per - 1, 0), off + h))

    vec = pl.BlockSpec((1, LANES), lambda h, j: (0, h))
    cw = pl.BlockSpec((4, LANES), lambda h, j: (0, h))
    wsq = pl.BlockSpec((None, LANES, LANES), lambda h, j: (h, 0, 0))
    return nch, ch, col, prev, vec, cw, wsq


def lru_fwd(P, conv_w, conv_b, w_a, b_a, w_x, b_x, lam, tag):
    S = P.shape[0]
    R = min(LRU_R, S)
    nch, ch, col, prev, vec, cw_spec, wsq = _lru_specs(S, R, False)

    def body(xb_ref, halo_ref, gb_ref, cw_ref, cb_ref, wa_ref, ba_ref, wx_ref, bx_ref, lam_ref, xc_ref, h_ref, yb_ref, carry):
        j = pl.program_id(1)
        xb = xb_ref[...].astype(F32)
        halo = halo_ref[...].astype(F32) * jnp.where(j > 0, 1.0, 0.0)
        ext = jnp.concatenate([halo, xb], axis=0)
        cw = cw_ref[...]
        xc = cb_ref[...]
        for k in range(4):
            e = ext if k == 3 else pltpu.roll(ext, 3 - k, 0)
            xc = xc + e[8:] * cw[k:k + 1]
        r, ig, sp, a, mult = _lru_gates(xc, wa_ref[...], ba_ref[...], wx_ref[...], bx_ref[...], lam_ref[...])
        u = mult * (ig * xc)
        cum_a, hloc = _scan_fwd(a, u)

        @pl.when(j == 0)
        def _():
            carry[...] = jnp.zeros_like(carry)

        hfull = hloc + cum_a * carry[0:1]
        carry[...] = jnp.broadcast_to(hfull[R - 1:R], carry.shape)
        gel, _ = _gelu_and_grad(gb_ref[...].astype(F32))
        xc_ref[...] = xc
        h_ref[...] = hfull
        yb_ref[...] = (hfull * gel).astype(yb_ref.dtype)

    out = pl.BlockSpec((R, LANES), lambda h, j: (j, h))
    return pl.pallas_call(
        body, name=tag + "_lru", grid=(LRU_HEADS, nch),
        in_specs=[col(PK_XB // LANES), prev(PK_XB // LANES), col(PK_GB // LANES), cw_spec, vec, wsq, vec, wsq, vec, vec],
        out_specs=[out, out, out],
        out_shape=[jax.ShapeDtypeStruct((S, D), F32), jax.ShapeDtypeStruct((S, D), F32), jax.ShapeDtypeStruct((S, D), BF)],
        scratch_shapes=[pltpu.VMEM((8, LANES), F32)], compiler_params=_cparams(2),
    )(P, P, P, conv_w, conv_b, w_a, b_a, w_x, b_x, lam)


def lru_bwd(dyb0, P, hh, xc, conv_w, w_a, b_a, w_x, b_x, lam, tag):
    S = P.shape[0]
    R = min(LRU_R, S)
    nch, ch, col, prev, vec, cw_spec, wsq = _lru_specs(S, R, True)

    def body(dyb_ref, gb_ref, h_ref, hprev_ref, xc_ref, xb_ref, xbprev_ref, cw_ref, wa_ref, ba_ref, wx_ref, bx_ref, lam_ref,
             dgb_ref, dxb_ref, dwa_ref, dba_ref, dwx_ref, dbx_ref, dlam_ref, dcw_ref, dcb_ref, gcarry, dxc_head):
        j = pl.program_id(1)
        jj = nch - 1 - j
        has_prev = jnp.where(jj > 0, 1.0, 0.0)
        row = _iota((R, LANES), 0)
        xc, hv, lam = xc_ref[...], h_ref[...], lam_ref[...]
        wa, wx = wa_ref[...], wx_ref[...]
        r, ig, sp, a, mult = _lru_gates(xc, wa, ba_ref[...], wx, bx_ref[...], lam)
        dyb = dyb_ref[...].astype(F32)
        gel, dgel = _gelu_and_grad(gb_ref[...].astype(F32))
        dgb_ref[...] = (dyb * hv * dgel).astype(dgb_ref.dtype)

        @pl.when(j == 0)
        def _():
            gcarry[...] = jnp.zeros_like(gcarry)
            dxc_head[...] = jnp.zeros_like(dxc_head)
            for ref in (dwa_ref, dba_ref, dwx_ref, dbx_ref, dlam_ref, dcw_ref, dcb_ref):
                ref[...] = jnp.zeros_like(ref)

        B = jnp.where(row < R - 1, pltpu.roll(a, R - 1, 0), 0.0)
        X = dyb * gel + jnp.where(row == R - 1, gcarry[0:1], 0.0)
        G = _scan_bwd(B, X)
        gcarry[...] = jnp.broadcast_to(a[0:1] * G[0:1], gcarry.shape)
        hprev = jnp.where(row == 0, hprev_ref[...][7:8] * has_prev, pltpu.roll(hv, 1, 0))
        da = G * hprev
        dmult = G * ig * xc
        dig = G * mult * xc
        dxc = G * mult * ig
        dlog_a = da * a - dmult * (a * a) / mult
        dzr = dlog_a * (-LRU_C * sp) * (r * (1.0 - r))
        dzi = dig * (ig * (1.0 - ig))
        dxc = dxc + _dot_nt(dzr, wa) + _dot_nt(dzi, wx)
        dwa_ref[...] += _dot_tn(xc, dzr)
        dwx_ref[...] += _dot_tn(xc, dzi)
        dba_ref[...] += _colsum(dzr)
        dbx_ref[...] += _colsum(dzi)
        dlam_ref[...] += _colsum(dlog_a * (-LRU_C * r)) * (-_sigmoid(-lam))
        dcb_ref[...] += _colsum(dxc)
        cw = cw_ref[...]
        ext = jnp.concatenate([dxc, dxc_head[...]], axis=0)
        dxb = dxc * cw[3:4]
        for k in range(3):
            dxb = dxb + pltpu.roll(ext, R + 8 - (3 - k), 0)[:R] * cw[k:k + 1]
        dxb_ref[...] = dxb.astype(dxb_ref.dtype)
        dxc_head[...] = dxc[0:8]
        extx = jnp.concatenate([xbprev_ref[...].astype(F32) * has_prev, xb_ref[...].astype(F32)], axis=0)
        incs = []
        for k in range(4):
            e = extx if k == 3 else pltpu.roll(extx, 3 - k, 0)
            incs.append(_colsum(dxc * e[8:]))
        dcw_ref[...] += jnp.concatenate(incs, axis=0)

    plain = pl.BlockSpec((R, LANES), lambda h, j: (ch(j), h))
    plain_prev = pl.BlockSpec((8, LANES), lambda h, j: (jnp.maximum(ch(j) * (R // 8) - 1, 0), h))
    return pl.pallas_call(
        body, name=tag + "_dlru", grid=(LRU_HEADS, nch),
        in_specs=[plain, col(PK_GB // LANES), plain, plain_prev, plain, col(PK_XB // LANES), prev(PK_XB // LANES),
                  cw_spec, wsq, vec, wsq, vec, vec],
        out_specs=[plain, plain, wsq, vec, wsq, vec, vec, cw_spec, vec],
        out_shape=[jax.ShapeDtypeStruct((S, D), BF), jax.ShapeDtypeStruct((S, D), BF),
                   jax.ShapeDtypeStruct((LRU_HEADS, LANES, LANES), F32), jax.ShapeDtypeStruct((1, D), F32),
                   jax.ShapeDtypeStruct((LRU_HEADS, LANES, LANES), F32), jax.ShapeDtypeStruct((1, D), F32),
                   jax.ShapeDtypeStruct((1, D), F32), jax.ShapeDtypeStruct((4, D), F32), jax.ShapeDtypeStruct((1, D), F32)],
        scratch_shapes=[pltpu.VMEM((8, LANES), F32), pltpu.VMEM((8, LANES), F32)], compiler_params=_cparams(2),
    )(dyb0, P, hh, hh, xc, P, P, conv_w, w_a, b_a, w_x, b_x, lam)


CUM_TS = 512
FLASH_T = 512
FLASH_G = 8
NEG = -1e30


def cumlogf_fwd(fl, b_f, tag):
    ts = min(CUM_TS, fl.shape[0])

    def fn(step, i, n, fl, bf, carry):
        x = -_softplus(-(fl + bf))
        row = _iota(x.shape, 0)
        d = 1
        while d < ts:
            x = x + jnp.where(row >= d, pltpu.roll(x, d, 0), 0.0)
            d *= 2

        @pl.when(step == 0)
        def _():
            carry[...] = jnp.zeros_like(carry)

        c = x + carry[0:1]
        carry[...] = jnp.broadcast_to(c[ts - 1:ts], carry.shape)
        return (c,), ()

    return rowwise(fn, [fl], [b_f], [(LANES, F32)], scratch=[pltpu.VMEM((8, LANES), F32)], ts=ts, name=tag + "_cum")[0][0]


def cumlogf_bwd(dc, fl, b_f, tag):
    ts = min(CUM_TS, fl.shape[0])

    def fn(step, i, n, dc, fl, bf, carry):
        row = _iota(dc.shape, 0)
        x = dc
        d = 1
        while d < ts:
            x = x + jnp.where(row < ts - d, pltpu.roll(x, ts - d, 0), 0.0)
            d *= 2

        @pl.when(step == 0)
        def _():
            carry[...] = jnp.zeros_like(carry)

        g = x + carry[0:1]
        carry[...] = jnp.broadcast_to(g[0:1], carry.shape)
        dfl = g * _sigmoid(-(fl + bf))
        return (dfl,), (_colsum(dfl),)

    (dfl,), (db,) = rowwise(fn, [dc, fl], [b_f], [(LANES, F32)], [((1, LANES), F32)], scratch=[pltpu.VMEM((8, LANES), F32)],
                            ts=ts, name=tag + "_dcum", reverse=True)
    return dfl, db


FOX_KA = 80


def _split3(c):
    c = c.astype(F32)
    hi = lax.reduce_precision(c, 8, 7)
    r = c - hi
    mid = lax.reduce_precision(r, 8, 7)
    return hi.astype(BF), mid.astype(BF), (r - mid).astype(BF)


def fox_operands(q, k, ch):
    H, S, hd = q.shape
    qs = q * (FOX_HD ** -0.5)
    pieces = [p.astype(F32) for p in _split3(ch)]
    pad = FOX_KA - hd

    def unit(i):
        return (jnp.arange(FOX_KA) == hd + i).astype(F32)

    qa = jnp.pad(qs.transpose(0, 2, 1).astype(F32), ((0, 0), (0, pad), (0, 0)))
    ka = jnp.pad(k.astype(F32), ((0, 0), (0, 0), (0, pad)))
    for i, p in enumerate(pieces):
        qa = qa + p[:, None, :] * unit(i)[None, :, None] + unit(3 + i)[None, :, None]
        ka = ka + unit(i)[None, None, :] - p[:, :, None] * unit(3 + i)[None, None, :]
    return qa.astype(BF), ka.astype(BF), qs


def _causal(s):
    return jnp.where(_iota(s.shape, 0) <= _iota(s.shape, 1), s, NEG)


def flash_fwd(qa, ka, vt, tag):
    H, hd, S = vt.shape
    T = min(FLASH_T, S)
    nb = S // T

    pairs = [(qi, ki) for qi in range(nb) for ki in range(qi + 1)]
    q_tab = jnp.asarray([p[0] for p in pairs], jnp.int32)
    k_tab = jnp.asarray([p[1] for p in pairs], jnp.int32)

    def body(q_tab_ref, k_tab_ref, ka_ref, qa_ref, vt_ref, o_ref, lse_ref, m_s, l_s, acc):
        qi, ki = q_tab_ref[pl.program_id(1)], k_tab_ref[pl.program_id(1)]

        @pl.when(ki == 0)
        def _():
            m_s[...] = jnp.full_like(m_s, NEG)
            l_s[...] = jnp.zeros_like(l_s)
            acc[...] = jnp.zeros_like(acc)

        def step(diagonal):
            scores = []
            for g in range(FLASH_G):
                s = _dot_nn(ka_ref[g], qa_ref[g])
                scores.append(_causal(s) if diagonal else s)
            for g in range(FLASH_G):
                s = scores[g]
                m_new = jnp.maximum(m_s[g], jnp.max(s, axis=0, keepdims=True))
                alpha = jnp.exp(m_s[g] - m_new)
                p = jnp.exp(s - m_new)
                l_s[g] = alpha * l_s[g] + jnp.sum(p, axis=0, keepdims=True)
                p_hi = p.astype(BF)
                p_lo = p - p_hi.astype(F32)
                v = vt_ref[g]
                acc[g] = alpha * acc[g] + (_dot_nn(v, p_hi) + _dot_nn(v, p_lo))
                m_s[g] = m_new

        @pl.when(ki < qi)
        def _():
            step(False)

        @pl.when(ki == qi)
        def _():
            step(True)
            o_ref[...] = acc[...] / l_s[...]
            lse_ref[...] = m_s[...] + jnp.log(l_s[...])

    G = FLASH_G
    grid_spec = pltpu.PrefetchScalarGridSpec(
        num_scalar_prefetch=2, grid=(H // G, len(pairs)),
        in_specs=[pl.BlockSpec((G, T, FOX_KA), lambda h, p, qt, kt: (h, kt[p], 0)),
                  pl.BlockSpec((G, FOX_KA, T), lambda h, p, qt, kt: (h, 0, qt[p])),
                  pl.BlockSpec((G, hd, T), lambda h, p, qt, kt: (h, 0, kt[p]))],
        out_specs=[pl.BlockSpec((G, hd, T), lambda h, p, qt, kt: (h, 0, qt[p])),
                   pl.BlockSpec((G, 1, T), lambda h, p, qt, kt: (h, 0, qt[p]))],
        scratch_shapes=[pltpu.VMEM((G, 1, T), F32), pltpu.VMEM((G, 1, T), F32), pltpu.VMEM((G, hd, T), F32)])
    return pl.pallas_call(
        body, name=tag + "_flash", grid_spec=grid_spec,
        out_shape=[jax.ShapeDtypeStruct((H, hd, S), F32), jax.ShapeDtypeStruct((H, 1, S), F32)],
        compiler_params=_cparams(2))(q_tab, k_tab, ka, qa, vt)


def flash_bwd(qa, ka, qs, kt, v, ot, do, dot_, lse, tag):
    H, S, hd = v.shape
    T = min(FLASH_T, S)
    nb = S // T

    pairs = [(qi, ki) for ki in range(nb) for qi in range(ki, nb)]
    q_tab = jnp.asarray([p[0] for p in pairs], jnp.int32)
    k_tab = jnp.asarray([p[1] for p in pairs], jnp.int32)

    def body(q_tab_ref, k_tab_ref, ka_ref, qa_ref, qs_ref, kt_ref, v_ref, ot_ref, do_ref, dot_ref, lse_ref,
             dqt_ref, dk_ref, dv_ref, dc_ref, dk_acc, dv_acc, dc_acc):
        qi, ki = q_tab_ref[pl.program_id(1)], k_tab_ref[pl.program_id(1)]

        @pl.when(pl.program_id(1) == 0)
        def _():
            dqt_ref[...] = jnp.zeros_like(dqt_ref)

        @pl.when(qi == ki)
        def _():
            dk_acc[...] = jnp.zeros_like(dk_acc)
            dv_acc[...] = jnp.zeros_like(dv_acc)
            dc_acc[...] = jnp.zeros_like(dc_acc)

        def step(diagonal):
            for g in range(FLASH_G):
                s = _dot_nn(ka_ref[g], qa_ref[g])
                if diagonal:
                    s = _causal(s)
                p = jnp.exp(s - lse_ref[g])
                dot_v = dot_ref[g]
                dp = _dot_nn(v_ref[g], dot_v)
                delta = jnp.sum(dot_v.astype(F32) * ot_ref[g], axis=0, keepdims=True)
                ds = p * (dp - delta)
                part = ds[:, 0:LANES]
                for j in range(1, T // LANES):
                    part = part + ds[:, j * LANES:(j + 1) * LANES]
                dc_acc[g] += part
                dsb = ds.astype(BF)
                dv_acc[g] += _dot_nn(p, do_ref[g])
                dk_acc[g] += _dot_nn(dsb, qs_ref[g])
                dqt_ref[g, qi] += _dot_nn(kt_ref[g], dsb)

        @pl.when(qi > ki)
        def _():
            step(False)

        @pl.when(qi == ki)
        def _():
            step(True)

        @pl.when(qi == nb - 1)
        def _():
            dk_ref[...] = dk_acc[...]
            dv_ref[...] = dv_acc[...]
            for g in range(FLASH_G):
                dc_ref[g] = -jnp.sum(dc_acc[g].T, axis=0, keepdims=True)

    qblk = lambda h, p, qt, kt_: (h, qt[p], 0)
    qblk_t = lambda h, p, qt, kt_: (h, 0, qt[p])
    kblk = lambda h, p, qt, kt_: (h, kt_[p], 0)
    kblk_t = lambda h, p, qt, kt_: (h, 0, kt_[p])
    G = FLASH_G
    grid_spec = pltpu.PrefetchScalarGridSpec(
        num_scalar_prefetch=2, grid=(H // G, len(pairs)),
        in_specs=[pl.BlockSpec((G, T, FOX_KA), kblk), pl.BlockSpec((G, FOX_KA, T), qblk_t),
                  pl.BlockSpec((G, T, hd), qblk), pl.BlockSpec((G, hd, T), kblk_t), pl.BlockSpec((G, T, hd), kblk),
                  pl.BlockSpec((G, hd, T), qblk_t), pl.BlockSpec((G, T, hd), qblk), pl.BlockSpec((G, hd, T), qblk_t),
                  pl.BlockSpec((G, 1, T), qblk_t)],
        out_specs=[pl.BlockSpec((G, nb, hd, T), lambda h, p, qt, kt_: (h, 0, 0, 0)), pl.BlockSpec((G, T, hd), kblk),
                   pl.BlockSpec((G, T, hd), kblk), pl.BlockSpec((G, 1, T), kblk_t)],
        scratch_shapes=[pltpu.VMEM((G, T, hd), F32), pltpu.VMEM((G, T, hd), F32), pltpu.VMEM((G, T, LANES), F32)])
    return pl.pallas_call(
        body, name=tag + "_dflash", grid_spec=grid_spec,
        out_shape=[jax.ShapeDtypeStruct((H, nb, hd, T), F32), jax.ShapeDtypeStruct((H, S, hd), F32),
                   jax.ShapeDtypeStruct((H, S, hd), F32), jax.ShapeDtypeStruct((H, 1, S), F32)],
        compiler_params=_cparams(2))(q_tab, k_tab, ka, qa, qs, kt, v, ot, do, dot_, lse)


def _to_heads(x2d):
    S = x2d.shape[0]
    return x2d.reshape(S, FOX_HEADS, FOX_HD).transpose(1, 0, 2)


def _from_heads(x3d):
    S = x3d.shape[1]
    return x3d.transpose(1, 0, 2).reshape(S, FOX_W)


def _gl_rows(P):
    return [Rows(P, D, PK_GL // D + k) for k in range(3)]


def merge_fwd(P, ya, yb, yc, b_gate, tag):
    def fn(step, i, n, g0, g1, g2, ya, yb, yc, b):
        out = 0.0
        for k, (gl, y) in enumerate(((g0, ya), (g1, yb), (g2, yc))):
            out = out + _sigmoid(gl.astype(F32) + b[:, k * D:(k + 1) * D]) * y.astype(F32)
        return (out,), ()
    return rowwise(fn, _gl_rows(P) + [ya, yb, yc], [b_gate], [(D, BF)], ts=512, name=tag + "_merge")[0][0]


def merge_bwd(dm, P, ya, yb, yc, b_gate, tag):
    def fn(step, i, n, dm, g0, g1, g2, ya, yb, yc, b):
        dm = dm.astype(F32)
        dys, dgls = [], []
        for k, (gl, y) in enumerate(((g0, ya), (g1, yb), (g2, yc))):
            g = _sigmoid(gl.astype(F32) + b[:, k * D:(k + 1) * D])
            dys.append(dm * g)
            dgls.append(dm * y.astype(F32) * (g * (1.0 - g)))
        dgl = jnp.concatenate(dgls, axis=1)
        return (dys[0], dys[1], dys[2], dgl), (_colsum(dgl),)
    (dya, dyb, dyc, dgl), (db,) = rowwise(
        fn, [dm] + _gl_rows(P) + [ya, yb, yc], [b_gate], [(D, BF), (D, BF), (D, BF), (3 * D, BF)], [((1, 3 * D), F32)],
        ts=256, name=tag + "_dmerge")
    return dya, dyb, dyc, dgl, db


def _xattn_probs(qh, kh):
    s = _dot_nt(qh, kh) * (X_HD ** -0.5)
    e = jnp.exp(s - jnp.max(s, axis=1, keepdims=True))
    return e / jnp.sum(e, axis=1, keepdims=True)


def xattn_fwd(q, kv, tag):
    def fn(step, i, n, q, k, v):
        os = []
        for h in range(X_HEADS):
            sl = slice(h * X_HD, (h + 1) * X_HD)
            os.append(_dot_nn(_xattn_probs(q[:, sl], k[:, sl]), v[:, sl]))
        return (jnp.concatenate(os, axis=1),), ()

    return _xattn_call(fn, [q], kv, [(D, BF)], [], tag + "_xattn")[0][0]


def _xattn_call(fn, rows, kv, outs, accs, name):
    S, ts, M = rows[0].shape[0], 512, kv.shape[0]
    ts = min(ts, S)
    nr, no, na = len(rows), len(outs), len(accs)

    def body(*refs):
        in_refs, out_refs, acc_refs = refs[:nr + 2], refs[nr + 2:nr + 2 + no], refs[nr + 2 + no:]
        step = pl.program_id(0)
        o, inc = fn(step, step, S // ts, *[r[...] for r in in_refs])
        for ref, val in zip(out_refs, o):
            ref[...] = val.astype(ref.dtype)
        if na:
            @pl.when(step == 0)
            def _():
                for ref in acc_refs:
                    ref[...] = jnp.zeros_like(ref)
            for ref, val in zip(acc_refs, inc):
                ref[...] += val

    row = pl.BlockSpec((ts, D), lambda s: (s, 0))
    res = pl.pallas_call(
        body, name=name, grid=(S // ts,),
        in_specs=[row] * nr + [pl.BlockSpec((M, D), lambda s: (0, 0)), pl.BlockSpec((M, D), lambda s: (0, 1))],
        out_specs=[row] * no + [pl.BlockSpec(tuple(s), lambda s_: (0, 0)) for s, _ in accs],
        out_shape=[jax.ShapeDtypeStruct((S, w), dt) for w, dt in outs] + [jax.ShapeDtypeStruct(tuple(s), dt) for s, dt in accs],
        compiler_params=_cparams(1))(*rows, kv, kv)
    return res[:no], res[no:]


def xattn_bwd(q, kv, do, tag):
    def fn(step, i, n, q, do, k, v):
        dqs, dks, dvs = [], [], []
        for h in range(X_HEADS):
            sl = slice(h * X_HD, (h + 1) * X_HD)
            p = _xattn_probs(q[:, sl], k[:, sl])
            dp = _dot_nt(do[:, sl], v[:, sl])
            ds = (p * (dp - jnp.sum(p * dp, axis=1, keepdims=True)) * (X_HD ** -0.5)).astype(BF)
            dqs.append(_dot_nn(ds, k[:, sl]))
            dks.append(_dot_tn(ds, q[:, sl]))
            dvs.append(_dot_tn(p, do[:, sl]))
        return (jnp.concatenate(dqs, axis=1),), (jnp.concatenate(dks + dvs, axis=1),)

    (dq,), (dkv,) = _xattn_call(fn, [q, do], kv, [(D, BF)], [((kv.shape[0], 2 * D), F32)], tag + "_dxattn")
    return dq, dkv


def layer_fwd(x, h_in, mem, w, next_g, tag):
    x1, u, s_ffn1 = ffn_fwd(x, h_in, w["ffn1_in"], w["ffn1_out"], w["g_mix"], tag + "_ffn1")
    P = matmul(u, w["wp"], "nn", name=tag + "_proj")
    fl = matmul(u, w["wfl"], "nn", out_dtype=F32, name=tag + "_projf")
    d, ya0 = pool_fwd(P, w["w_pool"], w["pool_scale"], tag)
    ya = matmul(ya0, w["up_a"], "nn", name=tag + "_upa")
    xc, hh, yb0 = lru_fwd(P, w["conv_w"], w["conv_b"], w["w_rg_a"], w["b_rg_a"], w["w_rg_x"], w["b_rg_x"], w["lam"], tag)
    yb = matmul(yb0, w["up_b"], "nn", name=tag + "_upb")
    c = cumlogf_fwd(fl, w["b_f"], tag)
    q, k, v = (_to_heads(P[:, o:o + FOX_W]) for o in (PK_Q, PK_K, PK_V))
    qa, ka, qs = fox_operands(q, k, c[:, :FOX_HEADS].T)
    ot, lse = flash_fwd(qa, ka, v.transpose(0, 2, 1), tag)
    o2 = ot.transpose(2, 0, 1).reshape(-1, FOX_W)
    yc = matmul(o2, w["up_c"], "nn", name=tag + "_upc")
    merged = merge_fwd(P, ya, yb, yc, w["b_gate"], tag)
    x2, hq = matmul(merged, w["w_o"], "nn", out_dtype=F32, res=x1, norm_g=w["g_cross"], name=tag + "_wo")
    qx = matmul(hq, w["xq"], "nn", name=tag + "_xq")
    mn = rms_fwd(mem, w["g_mem"], name=tag + "_mrms")
    kv = matmul(mn, w["xkv"], "nn", name=tag + "_xkv")
    ox = xattn_fwd(qx, kv, tag)
    x3, h2 = matmul(ox, w["xo"], "nn", out_dtype=F32, res=x2, norm_g=w["g_ffn2"], name=tag + "_xo")
    x4, h_out, s_ffn2 = ffn_fwd(x3, h2, w["ffn2_in"], w["ffn2_out"], next_g, tag + "_ffn2")
    saved = dict(ffn1=s_ffn1, ffn2=s_ffn2, x1=x1, u=u, P=P, fl=fl, d=d, ya0=ya0, ya=ya, xc=xc, hh=hh, yb0=yb0, yb=yb,
                 qa=qa, ka=ka, qs=qs, k=k, v=v, ot=ot, lse=lse, o2=o2, yc=yc, merged=merged, x2=x2, hq=hq, qx=qx,
                 mn=mn, kv=kv, ox=ox)
    return x4, h_out, saved


def layer_bwd(dx4, mem, w, s, tag):
    g = {}
    dx3, g["g_ffn2"], g["w_ffn2_in"], g["w_ffn2_out"] = ffn_bwd(dx4, s["ffn2"], w["g_ffn2"], w["ffn2_in"], w["ffn2_out"], tag + "_ffn2")
    dox = matmul(dx3, w["xo"], "nt", name=tag + "_dox")
    g["w_xo"] = matmul(s["ox"], dx3, "tn", name=tag + "_dwxo")
    dqx, dkv = xattn_bwd(s["qx"], s["kv"], dox, tag)
    g["w_xq"] = matmul(s["hq"], dqx, "tn", name=tag + "_dwxq")
    dx2, g["g_cross"] = matmul_rms_bwd(dqx, w["xq"], s["x2"], w["g_cross"], dx3, name=tag + "_dhq")
    g["w_xkv"] = matmul(s["mn"], dkv, "tn", name=tag + "_dwxkv")
    dmn = matmul(dkv, w["xkv"], "nt", name=tag + "_dmn")
    _, g["g_mem"] = rms_bwd(mem, w["g_mem"], dmn, None, name=tag + "_dmrms")
    P = s["P"]
    dmerged = matmul(dx2, w["w_o"], "nt", name=tag + "_dmerged")
    g["w_o"] = matmul(s["merged"], dx2, "tn", name=tag + "_dwo")
    dya, dyb, dyc, dgl, g["b_gate"] = merge_bwd(dmerged, P, s["ya"], s["yb"], s["yc"], w["b_gate"], tag)
    dya0 = matmul(dya, w["up_a"], "nt", name=tag + "_dya0")
    g["w_up_a"] = matmul(s["ya0"], dya, "tn", name=tag + "_dwupa")
    dxa, g["w_pool"], g["pool_scale"] = pool_bwd(dya0, s["d"], w["w_pool"], w["pool_scale"], tag)
    dyb0 = matmul(dyb, w["up_b"], "nt", name=tag + "_dyb0")
    g["w_up_b"] = matmul(s["yb0"], dyb, "tn", name=tag + "_dwupb")
    (dgb, dxb, g["w_rg_a"], g["b_rg_a"], g["w_rg_x"], g["b_rg_x"], g["lru_lambda"], g["conv_w"], g["conv_b"]) = lru_bwd(
        dyb0, P, s["hh"], s["xc"], w["conv_w"], w["w_rg_a"], w["b_rg_a"], w["w_rg_x"], w["b_rg_x"], w["lam"], tag)
    do2 = matmul(dyc, w["up_c"], "nt", name=tag + "_do2")
    g["w_up_c"] = matmul(s["o2"], dyc, "tn", name=tag + "_dwupc")
    do = _to_heads(do2)
    dqt, dk, dv, dck = flash_bwd(s["qa"], s["ka"], s["qs"], s["k"].transpose(0, 2, 1), s["v"], s["ot"], do,
                                 do.transpose(0, 2, 1), s["lse"], tag)
    dq = (dqt * (FOX_HD ** -0.5)).transpose(0, 1, 3, 2).reshape(dk.shape)
    dc = jnp.pad(dck[:, 0, :].T, ((0, 0), (0, LANES - FOX_HEADS)))
    dfl, g["b_f"] = cumlogf_bwd(dc, s["fl"], w["b_f"], tag)
    dP = jnp.concatenate([dxb, dgb, dgl, dxa] + [_from_heads(t).astype(BF) for t in (dq, dk, dv)], axis=1)
    du = matmul(dP, w["wp"], "nt", out_dtype=F32, name=tag + "_du")
    dx1, g["g_mix"] = matmul_rms_bwd(dfl, w["wfl"], s["x1"], w["g_mix"], dx2, res=du, name=tag + "_duf")
    g["wp"] = matmul(s["u"], dP, "tn", name=tag + "_dwp")
    g["wfl"] = matmul(s["u"], dfl, "tn", out_dtype=F32, name=tag + "_dwfl")
    dx0, g["g_ffn1"], g["w_ffn1_in"], g["w_ffn1_out"] = ffn_bwd(dx1, s["ffn1"], w["g_ffn1"], w["ffn1_in"], w["ffn1_out"], tag + "_ffn1")
    return dx0, g


def loss_head(x, target, g_final):
    def fn(step, i, n, x, t, g):
        r = _rstd(x)
        xhat = x * r
        e = xhat * g - t
        dy = e * (1.0 / D)
        gd = dy * g
        dx = r * (gd - xhat * jnp.mean(xhat * gd, axis=-1, keepdims=True))
        loss = 0.5 * jnp.sum(jnp.mean(e * e, axis=-1, keepdims=True), axis=0, keepdims=True)
        return (dx,), (jnp.broadcast_to(loss, (1, LANES)), _colsum(dy * xhat))

    (dx,), (loss, dg) = rowwise(fn, [x, target], [g_final], [(D, F32)], [((1, LANES), F32), ((1, D), F32)], ts=512, name="loss_head")
    return loss[0, 0], dx, dg


BIG = (("w_ffn1_in", (D, 2 * DFF), 1), ("w_ffn1_out", (DFF, D), 0), ("w_in", (D, 7176), 1), ("w_up_a", (POOL_W, D), 1),
       ("conv_w", (4, D), 1), ("w_up_b", (D, D), 0), ("w_up_c", (FOX_W, D), 1), ("w_o", (D, D), 0), ("w_xq", (D, D), 0),
       ("w_xkv", (D, 2 * D), 1), ("w_xo", (D, D), 0), ("w_ffn2_in", (D, 2 * DFF), 1), ("w_ffn2_out", (DFF, D), 0))
SMALL = (("g_ffn1", (D,)), ("g_mix", (D,)), ("b_f", (FOX_HEADS,)), ("b_gate", (3 * D,)), ("w_pool", (4, LANES, LANES)),
         ("pool_scale", (POOL_W,)), ("conv_b", (D,)), ("w_rg_a", (LRU_HEADS, LANES, LANES)), ("b_rg_a", (D,)),
         ("w_rg_x", (LRU_HEADS, LANES, LANES)), ("b_rg_x", (D,)), ("lru_lambda", (D,)), ("g_cross", (D,)), ("g_mem", (D,)),
         ("g_ffn2", (D,)))
ORDER = ("g_ffn1", "w_ffn1_in", "w_ffn1_out", "g_mix", "w_in", "b_f", "b_gate", "w_pool", "pool_scale", "w_up_a", "conv_w",
         "conv_b", "w_rg_a", "b_rg_a", "w_rg_x", "b_rg_x", "lru_lambda", "w_up_b", "w_up_c", "w_o", "g_cross", "g_mem", "w_xq",
         "w_xkv", "w_xo", "g_ffn2", "w_ffn2_in", "w_ffn2_out", "g_final")

IN_SPLIT = (("xa", 0, 512), ("xb", 512, 1024), ("gb", 1536, 1024), ("q", 2560, 512), ("k", 3072, 512), ("v", 3584, 512),
            ("fl", 4096, 8), ("gl", 4104, 3072))
PACK_ORDER = ("xb", "gb", "gl", "xa", "q", "k", "v")


def _shard_shape(shape, axis):
    s = list(shape)
    s[axis] //= N_CHIPS
    return (DEPTH, *s)


def _round_up(n, m):
    return -(-n // m) * m


SMALL_SEG_ROWS = 8


def _small_rows(n_elements):
    return _round_up(-(-n_elements // D), SMALL_SEG_ROWS)


def _pack_small(tree, names, row_multiple=SMALL_SEG_ROWS, fill=0.0):
    parts = []
    for n in names:
        flat = tree[n].astype(F32).reshape(-1)
        rows = _small_rows(flat.shape[0])
        parts.append(jnp.pad(flat, (0, rows * D - flat.shape[0]), constant_values=fill).reshape(rows, D))
    total = sum(p.shape[0] for p in parts)
    if total % row_multiple:
        parts.append(jnp.full((_round_up(total, row_multiple) - total, D), fill, F32))
    return jnp.concatenate(parts, axis=0)


def _unpack_small(buf, like, names):
    out, off = {}, 0
    for n in names:
        sh = like[n].shape
        rows = _small_rows(math.prod(sh))
        out[n] = buf[off:off + rows].reshape(-1)[:math.prod(sh)].reshape(sh)
        off += rows
    return out


def _from_chip_major(g, shape, axis):
    return jnp.moveaxis(g, 1, axis + 1).reshape(DEPTH, *shape)


def _to_chip_major(full, shape, axis):
    sh = list(shape)
    sh[axis:axis + 1] = [N_CHIPS, shape[axis] // N_CHIPS]
    return jnp.moveaxis(full.reshape(DEPTH, *sh), axis + 1, 1)


def _layer_weights(full, small, l):
    w_in = full["w_in"][l]
    cols = {n: w_in[:, o:o + s] for n, o, s in IN_SPLIT}
    w = dict(ffn1_in=full["w_ffn1_in"][l], ffn1_out=full["w_ffn1_out"][l], ffn2_in=full["w_ffn2_in"][l], ffn2_out=full["w_ffn2_out"][l],
             wp=jnp.concatenate([cols[n] for n in PACK_ORDER], axis=1),
             wfl=jnp.pad(cols["fl"], ((0, 0), (0, LANES - FOX_HEADS))),
             up_a=full["w_up_a"][l], up_b=full["w_up_b"][l], up_c=full["w_up_c"][l], w_o=full["w_o"][l], xq=full["w_xq"][l],
             xkv=full["w_xkv"][l], xo=full["w_xo"][l], conv_w=full["conv_w"][l])
    for n in ("g_ffn1", "g_mix", "b_gate", "pool_scale", "conv_b", "b_rg_a", "b_rg_x", "g_cross", "g_mem", "g_ffn2"):
        w[n] = small[n][l].reshape(1, -1)
    w["lam"] = small["lru_lambda"][l].reshape(1, -1)
    w["b_f"] = jnp.pad(small["b_f"][l], (0, LANES - FOX_HEADS)).reshape(1, LANES)
    for n in ("w_pool", "w_rg_a", "w_rg_x"):
        w[n] = small[n][l]
    return w


def _unpack_w_in_grad(gwp, gwfl):
    pk = {}
    off = 0
    sizes = {n: s for n, _, s in IN_SPLIT}
    for n in PACK_ORDER:
        pk[n] = gwp[:, off:off + sizes[n]]
        off += sizes[n]
    pk["fl"] = gwfl[:, :FOX_HEADS].astype(gwp.dtype)
    return jnp.concatenate([pk[n] for n, _, _ in IN_SPLIT], axis=1)


def _place():
    x, y, c = lax.axis_index("x"), lax.axis_index("y"), lax.axis_index("c")
    others = [(1 - x, y), (x, 1 - y), (1 - x, 1 - y)]
    return x, y, c, others


def _rcopy(src, dst, send_sems, recv_sems, k, to):
    return pltpu.make_async_remote_copy(src_ref=src, dst_ref=dst, send_sem=send_sems.at[k], recv_sem=recv_sems.at[k],
                                        device_id=to, device_id_type=MESH)


def _comm_call(body, name, ins, out_shape, n_sems):
    return pl.pallas_call(
        body, name=name, in_specs=[ANY] * len(ins), out_specs=[ANY] * len(out_shape), out_shape=out_shape,
        scratch_shapes=[pltpu.SemaphoreType.DMA((n_sems,)), pltpu.SemaphoreType.DMA((n_sems,)), pltpu.SemaphoreType.DMA((n_sems,))],
    )(*ins)


def allgather_same_core(bufs):
    nb = len(bufs)

    def body(*refs):
        ins, outs = refs[:nb], refs[nb:2 * nb]
        send_sems, recv_sems, local_sems = refs[2 * nb:]
        x, y, c, others = _place()
        me = 2 * x + y
        sends = []
        for b in range(nb):
            for j, (ox, oy) in enumerate(others):
                cp = _rcopy(ins[b], outs[b].at[me], send_sems, recv_sems, 3 * b + j, (ox, oy, c))
                cp.start()
                sends.append(cp)
        for b in range(nb):
            for j, (ox, oy) in enumerate(others):
                theirs = outs[b].at[2 * ox + oy]
                _rcopy(theirs, theirs, send_sems, recv_sems, 3 * b + j, (ox, oy, c)).wait_recv()
        for cp in sends:
            cp.wait_send()

    shapes = [jax.ShapeDtypeStruct((N_CHIPS, *b.shape), b.dtype) for b in bufs]
    outs = _comm_call(body, "allgather_same_core", list(bufs), shapes, 3 * nb)
    return [_with_own_block(o, b) for o, b in zip(outs, bufs)]


def _with_own_block(gathered, mine):
    me = 2 * lax.axis_index("x") + lax.axis_index("y")
    return lax.dynamic_update_index_in_dim(gathered, mine, me, axis=0)


def gather_weights(shards):
    nw = len(shards)

    def body(*refs):
        ins, outs = refs[:nw], refs[nw:2 * nw]
        send_sems, recv_sems, _ = refs[2 * nw:]
        x, y, c, _ = _place()
        fx, fy = 1 - c, c
        first = (x + fx - 2 * x * fx, y + fy - 2 * y * fy)
        second = (x + fy - 2 * x * fy, y + fx - 2 * y * fx)
        me, c1, c2, cd = 2 * x + y, 2 * first[0] + first[1], 2 * second[0] + second[1], 2 * (1 - x) + (1 - y)
        sibling = (x, y, 1 - c)
        started = []

        def send(w, k, src, dst, to):
            cp = _rcopy(src, dst, send_sems, recv_sems, 6 * w + k, to)
            cp.start()
            started.append(cp)

        def landed(w, k, block):
            _rcopy(block, block, send_sems, recv_sems, 6 * w + k, sibling).wait_recv()

        for w in range(nw):
            send(w, 0, ins[w].at[c], outs[w].at[c, me], (*first, c))
            send(w, 1, ins[w].at[c], outs[w].at[c, me], (*second, c))
        for w in range(nw):
            block = outs[w].at[c, c1]
            landed(w, 0, block)
            send(w, 2, block, block, (*second, c))
            send(w, 3, block, block, sibling)
        for w in range(nw):
            block = outs[w].at[c, c2]
            landed(w, 1, block)
            send(w, 4, block, block, sibling)
        for w in range(nw):
            block = outs[w].at[c, cd]
            landed(w, 2, block)
            send(w, 5, block, block, sibling)
        for w in range(nw):
            landed(w, 3, outs[w].at[1 - c, c2])
            landed(w, 4, outs[w].at[1 - c, c1])
            landed(w, 5, outs[w].at[1 - c, cd])
        for cp in started:
            cp.wait_send()

    shapes = [jax.ShapeDtypeStruct((DEPTH, N_CHIPS, *s.shape[1:]), s.dtype) for s in shards]
    outs = _comm_call(body, "gather_weights", list(shards), shapes, 6 * nw)
    me = 2 * lax.axis_index("x") + lax.axis_index("y")
    return [lax.dynamic_update_index_in_dim(o, s, me, axis=1) for o, s in zip(outs, shards)]


def sibling_layers(bufs):
    nb = len(bufs)

    def body(*refs):
        ins, outs = refs[:nb], refs[nb:2 * nb]
        send_sems, recv_sems, _ = refs[2 * nb:]
        x, y, c, _ = _place()
        work = []
        for b in range(nb):
            cp = _rcopy(ins[b].at[1 - c], outs[b], send_sems, recv_sems, b, (x, y, 1 - c))
            cp.start()
            work.append(cp)
        for cp in work:
            cp.wait()

    shapes = [jax.ShapeDtypeStruct(b.shape[1:], b.dtype) for b in bufs]
    return _comm_call(body, "sibling_layers", list(bufs), shapes, nb)


def chip_blocks(bufs):
    nb = len(bufs)

    def body(*refs):
        ins, outs = refs[:nb], refs[nb:2 * nb]
        send_sems, recv_sems, _ = refs[2 * nb:]
        x, y, c, others = _place()
        sends = []
        for b in range(nb):
            for j, (ox, oy) in enumerate(others):
                cp = _rcopy(ins[b].at[2 * ox + oy], outs[b].at[j], send_sems, recv_sems, 3 * b + j, (ox, oy, c))
                cp.start()
                sends.append(cp)
        for cp in sends:
            cp.wait()

    shapes = [jax.ShapeDtypeStruct((3, *b.shape[1:]), b.dtype) for b in bufs]
    return _comm_call(body, "chip_blocks", list(bufs), shapes, 3 * nb)


def sibling_other_layer(bufs):
    nb = len(bufs)

    def body(*refs):
        ins, outs = refs[:nb], refs[nb:2 * nb]
        send_sems, recv_sems, _ = refs[2 * nb:]
        x, y, c, _ = _place()
        work = []
        for b in range(nb):
            cp = _rcopy(ins[b], outs[b].at[c], send_sems, recv_sems, b, (x, y, 1 - c))
            cp.start()
            work.append(cp)
        for b, cp in enumerate(work):
            cp.wait_send()
            theirs = outs[b].at[1 - c]
            _rcopy(theirs, theirs, send_sems, recv_sems, b, (x, y, 1 - c)).wait_recv()

    shapes = [jax.ShapeDtypeStruct((DEPTH, *b.shape), b.dtype) for b in bufs]
    outs = _comm_call(body, "sibling_other_layer", list(bufs), shapes, nb)
    c = lax.axis_index("c")
    return [lax.dynamic_update_index_in_dim(o, b, c, axis=0) for o, b in zip(outs, bufs)]


ADD_BLOCK_BYTES = 2 * 2 ** 20


def add_arrays(parts, out_dtype, name):
    shape = parts[0].shape
    cols = shape[-1]
    rows = math.prod(shape[:-1])

    def fn(step, i, n, *vals):
        acc = vals[0].astype(F32)
        for v in vals[1:]:
            acc = acc + v.astype(F32)
        return (acc,), ()

    ts = rows
    for cand in (2048, 1024, 512, 256, 128, 64, 32, 16):
        if rows % cand == 0 and cand * cols * 4 <= ADD_BLOCK_BYTES:
            ts = cand
            break
    out = rowwise(fn, [p.reshape(rows, cols) for p in parts], outs=[(cols, out_dtype)], ts=ts, name=name)[0][0]
    return out.reshape(shape)


def reduce_gradients(bufs, names):
    c, me = lax.axis_index("c"), 2 * lax.axis_index("x") + lax.axis_index("y")
    theirs = sibling_layers(bufs)
    mine = [lax.dynamic_index_in_dim(b, c, axis=0, keepdims=False) for b in bufs]
    part = [add_arrays([m, t], m.dtype, "rs_add_sibling_" + n) for m, t, n in zip(mine, theirs, names)]
    recv = chip_blocks(part)
    own = [lax.dynamic_index_in_dim(p, me, axis=0, keepdims=False) for p in part]
    red = [add_arrays([o, r[0], r[1], r[2]], F32, "rs_add_chips_" + n) for o, r, n in zip(own, recv, names)]
    return sibling_other_layer(red)


_BC1 = 1.0 - ADAM_B1 ** ADAM_STEP
_BC2 = 1.0 - ADAM_B2 ** ADAM_STEP


def adamw(w, g, m, v, name):
    rows, cols = w.shape

    def fn(step, i, n, w, g, m, v):
        m2 = ADAM_B1 * m + (1.0 - ADAM_B1) * g
        v2 = ADAM_B2 * v + (1.0 - ADAM_B2) * (g * g)
        delta = -ADAM_LR * ((m2 / _BC1) / (jnp.sqrt(v2 / _BC2) + ADAM_EPS) + ADAM_WD * w)
        return (delta, m2, v2), ()

    ts = _pick(rows, (256, 128, 64, 32, 16, 8))
    return rowwise(fn, [w, g, m, v], outs=[(cols, F32)] * 3, ts=ts, name=name)[0]


def local_step(x, mem, target, full, small):
    ws = [_layer_weights(full, small, l) for l in range(DEPTH)]
    saved = []
    h, hn = x, rms_fwd(x, ws[0]["g_ffn1"], name="l_ffn1_rms")
    for l in range(DEPTH):
        h, hn, s = layer_fwd(h, hn, mem, ws[l], ws[l + 1]["g_ffn1"] if l + 1 < DEPTH else None, "l")
        saved.append(s)
    loss, dh, dg_final = loss_head(h, target, small["g_final"].reshape(1, D))
    grads = [None] * DEPTH
    for l in reversed(range(DEPTH)):
        dh, grads[l] = layer_bwd(dh, mem, ws[l], saved[l], "l")
    out = {}
    for l in range(DEPTH):
        g = grads[l]
        g["w_in"] = _unpack_w_in_grad(g.pop("wp"), g.pop("wfl"))
        g["b_f"] = g["b_f"][:, :FOX_HEADS]
    for name, shape, _ in BIG:
        out[name] = jnp.stack([grads[l][name].reshape(shape) for l in range(DEPTH)])
    for name, shape in SMALL:
        out[name] = jnp.stack([grads[l][name].reshape(shape) for l in range(DEPTH)])
    out["g_final"] = dg_final.reshape(D)
    return loss, dh, out


def kernel(x, mem, g_ffn1, w_ffn1_in, w_ffn1_out, g_mix, w_in, b_f, b_gate, w_pool, pool_scale, w_up_a, conv_w, conv_b, w_rg_a, b_rg_a, w_rg_x, b_rg_x, lru_lambda, w_up_b, w_up_c, w_o, g_cross, g_mem, w_xq, w_xkv, w_xo, g_ffn2, w_ffn2_in, w_ffn2_out, g_final, loss_target, m_g_ffn1, m_w_ffn1_in, m_w_ffn1_out, m_g_mix, m_w_in, m_b_f, m_b_gate, m_w_pool, m_pool_scale, m_w_up_a, m_conv_w, m_conv_b, m_w_rg_a, m_b_rg_a, m_w_rg_x, m_b_rg_x, m_lru_lambda, m_w_up_b, m_w_up_c, m_w_o, m_g_cross, m_g_mem, m_w_xq, m_w_xkv, m_w_xo, m_g_ffn2, m_w_ffn2_in, m_w_ffn2_out, m_g_final, v_g_ffn1, v_w_ffn1_in, v_w_ffn1_out, v_g_mix, v_w_in, v_b_f, v_b_gate, v_w_pool, v_pool_scale, v_w_up_a, v_conv_w, v_conv_b, v_w_rg_a, v_b_rg_a, v_w_rg_x, v_b_rg_x, v_lru_lambda, v_w_up_b, v_w_up_c, v_w_o, v_g_cross, v_g_mem, v_w_xq, v_w_xkv, v_w_xo, v_g_ffn2, v_w_ffn2_in, v_w_ffn2_out, v_g_final):
    args = dict(locals())
    weights = {n: args[n] for n in ORDER}
    m_in = {n: args["m_" + n] for n in ORDER}
    v_in = {n: args["v_" + n] for n in ORDER}
    big_names = [n for n, _, _ in BIG]
    small_names = [n for n, _ in SMALL] + ["g_final"]

    shards = []
    for n in big_names:
        shards += list(_split3(weights[n])) if n == "conv_w" else [weights[n].astype(BF)]
    gathered = iter(gather_weights(shards))
    full = {}
    for n, shape, axis in BIG:
        g = next(gathered)
        if n == "conv_w":
            g = g.astype(F32) + next(gathered).astype(F32) + next(gathered).astype(F32)
        full[n] = _from_chip_major(g, shape, axis)

    small = {n: weights[n] for n in small_names}
    loss, grad_x, contrib = local_step(x[0], mem[0], loss_target[0], full, small)
    loss = lax.psum(loss, ("x", "y", "c"))

    bufs = [_to_chip_major(contrib[n].astype(BF), shape, axis) for n, shape, axis in BIG]
    small_buf = _pack_small(contrib, small_names, DEPTH * N_CHIPS * SMALL_SEG_ROWS).reshape(DEPTH, N_CHIPS, -1, D)
    red = reduce_gradients(bufs + [small_buf], big_names + ["small"])

    grads = dict(zip(big_names, red[:-1]))
    red_small = allgather_same_core([red[-1]])[0].transpose(1, 0, 2, 3).reshape(-1, D)
    grads.update(_unpack_small(red_small, weights, small_names))

    delta, new_m, new_v = {}, {}, {}
    for name in big_names:
        sh = weights[name].shape
        two_d = (-1, sh[-1])
        d, m2, v2 = adamw(weights[name].reshape(two_d), grads[name].reshape(two_d), m_in[name].reshape(two_d),
                          v_in[name].reshape(two_d), "adamw_" + name)
        delta[name], new_m[name], new_v[name] = d.reshape(sh), m2.reshape(sh), v2.reshape(sh)
    outs = adamw(_pack_small(weights, small_names), _pack_small(grads, small_names), _pack_small(m_in, small_names),
                 _pack_small(v_in, small_names, fill=1.0), "adamw_small")
    for t, res in zip((delta, new_m, new_v), outs):
        t.update(_unpack_small(res, weights, small_names))

    return (loss, grad_x[None], *[grads[n] for n in ORDER], *[delta[n] for n in ORDER],
            *[new_m[n] for n in ORDER], *[new_v[n] for n in ORDER])
```

```python
import math

import jax
import jax.numpy as jnp
from jax import lax
from jax.experimental import pallas as pl
from jax.experimental.pallas import tpu as pltpu

F32 = jnp.float32
BF = jnp.bfloat16

D = 1024
DFF = 2816
DEPTH = 2
POOL_W = 512
POOL_WINDOWS = (2, 4, 8, 16)
LRU_HEADS = 8
LRU_C = 8.0
FOX_HEADS = 8
FOX_HD = 64
FOX_W = 512
X_HEADS = 4
X_HD = 256
EPS = 1e-6
LANES = 128
N_CHIPS = 4

ADAM_LR, ADAM_B1, ADAM_B2, ADAM_EPS, ADAM_WD, ADAM_STEP = 0.001, 0.9, 0.999, 1e-08, 0.01, 10

VMEM_LIMIT_BYTES = 56 * 2 ** 20

PK_XB, PK_GB, PK_GL, PK_XA, PK_Q, PK_K, PK_V = 0, 1024, 2048, 5120, 5632, 6144, 6656
PK_W = 7168

MESH = pl.DeviceIdType.MESH
ANY = pl.BlockSpec(memory_space=pl.ANY)


def _cparams(ngrid):
    return pltpu.CompilerParams(dimension_semantics=("arbitrary",) * ngrid, vmem_limit_bytes=VMEM_LIMIT_BYTES)


def _pick(n, cands):
    for c in cands:
        if n % c == 0:
            return c
    return n


def _iota(shape, dim):
    return lax.broadcasted_iota(jnp.int32, shape, dim)


def _sigmoid(x):
    return 1.0 / (1.0 + jnp.exp(-x))


def _softplus(z):
    return jnp.maximum(z, 0.0) + jnp.log1p(jnp.exp(-jnp.abs(z)))


def _expm1(x):
    small = jnp.abs(x) < 0.25
    xs = jnp.where(small, x, 0.0)
    poly = xs * (1.0 + xs * (1 / 2 + xs * (1 / 6 + xs * (1 / 24 + xs * (1 / 120 + xs * (1 / 720 + xs * (1 / 5040 + xs * (1 / 40320))))))))
    return jnp.where(small, poly, jnp.exp(x) - 1.0)


_GELU_K = math.sqrt(2.0 / math.pi)


def _gelu_and_grad(x):
    inner = _GELU_K * (x + 0.044715 * x * x * x)
    t = jnp.tanh(inner)
    g = 0.5 * x * (1.0 + t)
    dg = 0.5 * (1.0 + t) + 0.5 * x * (1.0 - t * t) * _GELU_K * (1.0 + 3 * 0.044715 * x * x)
    return g, dg


def _dot(a, b, dims):
    return lax.dot_general(a.astype(BF), b.astype(BF), (dims, ((), ())), preferred_element_type=F32)


def _dot_nn(a, b):
    return _dot(a, b, ((1,), (0,)))


def _dot_nt(a, b):
    return _dot(a, b, ((1,), (1,)))


def _dot_tn(a, b):
    return _dot(a, b, ((0,), (0,)))


def _colsum(x):
    return jnp.sum(x, axis=0, keepdims=True)


_TM = (1024, 1408, 512, 256, 128)
_TN = (1408, 1024, 512, 384, 256, 128)
_TK = (2432, 1408, 1024, 512, 256, 128)


def matmul(a, b, mode, *, out_dtype=None, alpha=1.0, res=None, norm_g=None, name):
    out_dtype = BF if out_dtype is None else out_dtype
    if mode == "nn":
        (M, K), N = a.shape, b.shape[1]
    elif mode == "nt":
        (M, K), N = a.shape, b.shape[0]
    else:
        (K, M), N = a.shape, b.shape[1]
    tm, tn, tk = _pick(M, _TM), _pick(N, _TN), _pick(K, _TK)
    nk = K // tk
    dims = {"nn": ((1,), (0,)), "nt": ((1,), (1,)), "tn": ((0,), (0,))}[mode]
    a_spec = pl.BlockSpec((tk, tm), lambda i, j, k: (k, i)) if mode == "tn" else pl.BlockSpec((tm, tk), lambda i, j, k: (i, k))
    b_spec = pl.BlockSpec((tn, tk), lambda i, j, k: (j, k)) if mode == "nt" else pl.BlockSpec((tk, tn), lambda i, j, k: (k, j))
    o_spec = pl.BlockSpec((tm, tn), lambda i, j, k: (i, j))
    has_res, has_norm = res is not None, norm_g is not None
    assert not has_norm or tn == N, "the norm epilogue needs whole rows"

    def body(*refs):
        a_ref, b_ref = refs[0], refs[1]
        r_ref = refs[2] if has_res else None
        g_ref = refs[2 + has_res] if has_norm else None
        o_ref = refs[2 + has_res + has_norm]
        h_ref = refs[3 + has_res + has_norm] if has_norm else None
        acc = refs[-1]
        k = pl.program_id(2)

        @pl.when(k == 0)
        def _():
            acc[...] = jnp.zeros_like(acc)

        acc[...] += _dot(a_ref[...], b_ref[...], dims)

        @pl.when(k == nk - 1)
        def _():
            r = acc[...]
            if alpha != 1.0:
                r = r * alpha
            if has_res:
                r = r + r_ref[...].astype(F32)
            r = r.astype(o_ref.dtype)
            o_ref[...] = r
            if has_norm:
                rf = r.astype(F32)
                h_ref[...] = ((rf * _rstd(rf)) * g_ref[...]).astype(h_ref.dtype)

    ins = [a, b] + ([res] if has_res else []) + ([norm_g] if has_norm else [])
    specs = [a_spec, b_spec] + ([o_spec] if has_res else []) + ([pl.BlockSpec((1, N), lambda i, j, k: (0, 0))] if has_norm else [])
    out = pl.pallas_call(
        body, name=name, grid=(M // tm, N // tn, nk), in_specs=specs,
        out_specs=[o_spec, o_spec] if has_norm else o_spec,
        out_shape=([jax.ShapeDtypeStruct((M, N), out_dtype), jax.ShapeDtypeStruct((M, N), BF)] if has_norm
                   else jax.ShapeDtypeStruct((M, N), out_dtype)),
        scratch_shapes=[pltpu.VMEM((tm, tn), F32)], compiler_params=_cparams(3))(*ins)
    return tuple(out) if has_norm else out


def matmul_rms_bwd(a, b, x, g, dres, *, res=None, name):
    (M, K), N = a.shape, b.shape[0]
    tm, tk = _pick(M, _TM), _pick(K, _TK)
    nk = K // tk
    has_res, has_dres = res is not None, dres is not None

    def body(*refs):
        a_ref, b_ref, x_ref, g_ref = refs[:4]
        rest = list(refs[4:])
        r_ref = rest.pop(0) if has_res else None
        d_ref = rest.pop(0) if has_dres else None
        dx_ref, dg_ref, acc = rest
        i, k = pl.program_id(0), pl.program_id(1)

        @pl.when(k == 0)
        def _():
            acc[...] = jnp.zeros_like(acc)

        @pl.when(jnp.logical_and(i == 0, k == 0))
        def _():
            dg_ref[...] = jnp.zeros_like(dg_ref)

        acc[...] += _dot_nt(a_ref[...], b_ref[...])

        @pl.when(k == nk - 1)
        def _():
            dh = acc[...]
            if has_res:
                dh = dh + r_ref[...]
            xv = x_ref[...]
            r = _rstd(xv)
            xhat = xv * r
            gd = dh * g_ref[...]
            dx = r * (gd - xhat * jnp.mean(xhat * gd, axis=-1, keepdims=True))
            if has_dres:
                dx = dx + d_ref[...]
            dx_ref[...] = dx
            dg_ref[...] += _colsum(dh * xhat)

    row = pl.BlockSpec((tm, N), lambda i, k: (i, 0))
    vec = pl.BlockSpec((1, N), lambda i, k: (0, 0))
    ins = [a, b, x, g] + ([res] if has_res else []) + ([dres] if has_dres else [])
    specs = ([pl.BlockSpec((tm, tk), lambda i, k: (i, k)), pl.BlockSpec((N, tk), lambda i, k: (0, k)), row, vec]
             + [row] * (has_res + has_dres))
    dx, dg = pl.pallas_call(
        body, name=name, grid=(M // tm, nk), in_specs=specs, out_specs=[row, vec],
        out_shape=[jax.ShapeDtypeStruct((M, N), F32), jax.ShapeDtypeStruct((1, N), F32)],
        scratch_shapes=[pltpu.VMEM((tm, N), F32)], compiler_params=_cparams(2))(*ins)
    return dx, dg


class Rows:
    def __init__(self, arr, w=None, cb=0, halo=None, h=8):
        self.arr, self.w, self.cb, self.halo, self.h = arr, (arr.shape[1] if w is None else w), cb, halo, h


def rowwise(fn, rows, consts=(), outs=(), accs=(), scratch=(), *, ts, name, reverse=False):
    rows = [r if isinstance(r, Rows) else Rows(r) for r in rows]
    S = rows[0].arr.shape[0]
    n = S // ts
    tile = (lambda s: n - 1 - s) if reverse else (lambda s: s)

    def row_spec(r):
        if r.halo is None:
            return pl.BlockSpec((ts, r.w), lambda s: (tile(s), r.cb))
        per, last = ts // r.h, S // r.h - 1
        if r.halo == "prev":
            return pl.BlockSpec((r.h, r.w), lambda s: (jnp.maximum(tile(s) * per - 1, 0), r.cb))
        return pl.BlockSpec((r.h, r.w), lambda s: (jnp.minimum((tile(s) + 1) * per, last), r.cb))

    def whole(shape):
        nd = len(shape)
        return pl.BlockSpec(tuple(shape), lambda s: (0,) * nd)

    nr, nc, no, na = len(rows), len(consts), len(outs), len(accs)

    def body(*refs):
        in_refs, out_refs = refs[:nr + nc], refs[nr + nc:nr + nc + no]
        acc_refs, scr = refs[nr + nc + no:nr + nc + no + na], refs[nr + nc + no + na:]
        step = pl.program_id(0)
        o, inc = fn(step, tile(step), n, *[r[...] for r in in_refs], *scr)
        for ref, val in zip(out_refs, o):
            ref[...] = val.astype(ref.dtype)
        if na:
            @pl.when(step == 0)
            def _():
                for ref in acc_refs:
                    ref[...] = jnp.zeros_like(ref)
            for ref, val in zip(acc_refs, inc):
                ref[...] += val.astype(ref.dtype)

    out_shape = [jax.ShapeDtypeStruct((S, w), dt) for w, dt in outs] + [jax.ShapeDtypeStruct(tuple(s), dt) for s, dt in accs]
    out_specs = [pl.BlockSpec((ts, w), lambda s: (tile(s), 0)) for w, _ in outs] + [whole(s) for s, _ in accs]
    res = pl.pallas_call(
        body, name=name, grid=(n,),
        in_specs=[row_spec(r) for r in rows] + [whole(c.shape) for c in consts],
        out_specs=out_specs, out_shape=out_shape, scratch_shapes=list(scratch),
        compiler_params=_cparams(1))(*[r.arr for r in rows], *consts)
    return res[:no], res[no:]


def _rstd(x):
    return lax.rsqrt(jnp.mean(x * x, axis=-1, keepdims=True) + EPS)


def rms_fwd(x, g, *, name, ts=512):
    def fn(step, i, n, x, g):
        return ((x * _rstd(x)) * g,), ()
    return rowwise(fn, [x], [g], [(D, BF)], ts=min(ts, x.shape[0]), name=name)[0][0]


def rms_bwd(x, g, dh, dres, *, name, ts=512):
    has_res = dres is not None

    def fn(step, i, n, x, dh, *rest):
        g = rest[-1]
        r = _rstd(x)
        xhat = x * r
        dh = dh.astype(F32)
        gd = dh * g
        dx = r * (gd - xhat * jnp.mean(xhat * gd, axis=-1, keepdims=True))
        if has_res:
            dx = dx + rest[0]
        return (dx,), (_colsum(dh * xhat),)

    rows = [x, dh] + ([dres] if has_res else [])
    o, a = rowwise(fn, rows, [g], [(D, F32)], [((1, D), F32)], ts=min(ts, x.shape[0]), name=name)
    return o[0], a[0]


def swiglu_in(h, w_in, name):
    M, K = h.shape
    tm, tn = _pick(M, _TM), _pick(DFF, _TN)
    nj = DFF // tn

    def body(h_ref, wa_ref, wb_ref, a_ref, b_ref, act_ref):
        hv = h_ref[...]
        a = _dot_nn(hv, wa_ref[...]).astype(a_ref.dtype)
        b = _dot_nn(hv, wb_ref[...]).astype(b_ref.dtype)
        a_ref[...] = a
        b_ref[...] = b
        af = a.astype(F32)
        act_ref[...] = ((af * _sigmoid(af)) * b.astype(F32)).astype(act_ref.dtype)

    out = pl.BlockSpec((tm, tn), lambda i, j: (i, j))
    return pl.pallas_call(
        body, name=name, grid=(M // tm, nj),
        in_specs=[pl.BlockSpec((tm, K), lambda i, j: (i, 0)), pl.BlockSpec((K, tn), lambda i, j: (0, j)),
                  pl.BlockSpec((K, tn), lambda i, j: (0, j + nj))],
        out_specs=[out, out, out], out_shape=[jax.ShapeDtypeStruct((M, DFF), BF)] * 3,
        compiler_params=_cparams(2))(h, w_in, w_in)


def ffn_fwd(x, h, w_in, w_out, next_g, tag):
    a, b, act = swiglu_in(h, w_in, tag + "_in")
    y = matmul(act, w_out, "nn", out_dtype=F32, alpha=0.5, res=x, norm_g=next_g, name=tag + "_out")
    y, h_next = y if next_g is not None else (y, None)
    return y, h_next, (x, h, a, b, act)


def ffn_bwd(dy, saved, g, w_in, w_out, tag):
    x, h, a, b, act = saved
    dact = matmul(dy, w_out, "nt", alpha=0.5, name=tag + "_dact")
    dw_out = matmul(act, dy, "tn", alpha=0.5, name=tag + "_dwout")

    def dab_fn(step, i, n, a, b, dact):
        a, b, dact = a.astype(F32), b.astype(F32), dact.astype(F32)
        s = _sigmoid(a)
        da = dact * b * (s * (1.0 + a * (1.0 - s)))
        db = dact * (a * s)
        return (jnp.concatenate([da, db], axis=1),), ()

    dab = rowwise(dab_fn, [a, b, dact], outs=[(2 * DFF, BF)], ts=256, name=tag + "_dab")[0][0]
    dw_in = matmul(h, dab, "tn", name=tag + "_dwin")
    dx, dg = matmul_rms_bwd(dab, w_in, x, g, dy, name=tag + "_dh")
    return dx, dg, dw_in, dw_out


POOL_TS = 256
POOL_HALO = 16


def pool_fwd(P, w_pool, pool_scale, tag):
    ts = POOL_TS

    def fn(step, i, n, xa, halo, w, scale):
        xa = xa.astype(F32)
        halo = halo.astype(F32) * jnp.where(i > 0, 1.0, 0.0)
        ext = jnp.concatenate([halo, xa], axis=0)
        pos = (i * ts + 1 + _iota((ts, LANES), 0)).astype(F32)
        ds, ys = [], []
        for gi, win in enumerate(POOL_WINDOWS):
            e = ext[:, gi * LANES:(gi + 1) * LANES]
            sh = 1
            while sh < win:
                e = e + pltpu.roll(e, sh, 0)
                sh *= 2
            mean = e[POOL_HALO:] / jnp.minimum(pos, float(win))
            d = mean - xa[:, gi * LANES:(gi + 1) * LANES]
            ds.append(d)
            ys.append(_dot_nn(d, w[gi]))
        d = jnp.concatenate(ds, axis=1)
        return (d, jnp.concatenate(ys, axis=1) * scale), ()

    xa = Rows(P, POOL_W, PK_XA // POOL_W)
    xa_prev = Rows(P, POOL_W, PK_XA // POOL_W, "prev", POOL_HALO)
    (d, ya0), _ = rowwise(fn, [xa, xa_prev], [w_pool, pool_scale], [(POOL_W, BF), (POOL_W, BF)], ts=ts, name=tag + "_pool")
    return d, ya0


def pool_bwd(dya0, d, w_pool, pool_scale, tag):
    ts = POOL_TS
    L = ts + POOL_HALO

    def fn(step, i, n, dya0, dya0_next, d, w, scale):
        d = d.astype(F32)
        dz = jnp.concatenate([dya0.astype(F32), dya0_next.astype(F32) * jnp.where(i < n - 1, 1.0, 0.0)], axis=0) * scale
        pos = (i * ts + 1 + _iota((L, LANES), 0)).astype(F32)
        dxa, dws, dsc = [], [], []
        for gi, win in enumerate(POOL_WINDOWS):
            sl = slice(gi * LANES, (gi + 1) * LANES)
            dzg = dz[:, sl]
            dd = _dot_nt(dzg, w[gi])
            e = dd / jnp.minimum(pos, float(win))
            sh = 1
            while sh < win:
                e = e + pltpu.roll(e, L - sh, 0)
                sh *= 2
            dxa.append(e[:ts] - dd[:ts])
            dws.append(_dot_tn(d[:, sl], dzg[:ts])[None])
            z = _dot_nn(d[:, sl], w[gi])
            dsc.append(_colsum(dya0[:, sl].astype(F32) * z))
        return (jnp.concatenate(dxa, axis=1),), (jnp.concatenate(dws, axis=0), jnp.concatenate(dsc, axis=1))

    (dxa,), (dw_pool, dscale) = rowwise(
        fn, [dya0, Rows(dya0, halo="next", h=POOL_HALO), d], [w_pool, pool_scale],
        [(POOL_W, BF)], [((4, LANES, LANES), F32), ((1, POOL_W), F32)], ts=ts, name=tag + "_dpool")
    return dxa, dw_pool, dscale


LRU_R = 512


def _scan_fwd(A, U):
    R = A.shape[0]
    row = _iota(A.shape, 0)
    d = 1
    while d < R:
        m = row >= d
        A_sh = jnp.where(m, pltpu.roll(A, d, 0), 1.0)
        U_sh = jnp.where(m, pltpu.roll(U, d, 0), 0.0)
        U = A * U_sh + U
        A = A * A_sh
        d *= 2
    return A, U


def _scan_bwd(B, X):
    R = B.shape[0]
    row = _iota(B.shape, 0)
    d = 1
    while d < R:
        m = row < R - d
        B_sh = jnp.where(m, pltpu.roll(B, R - d, 0), 1.0)
        X_sh = jnp.where(m, pltpu.roll(X, R - d, 0), 0.0)
        X = X + B * X_sh
        B = B * B_sh
        d *= 2
    return X


def _lru_gates(xc, wa, ba, wx, bx, lam):
    r = _sigmoid(_dot_nn(xc, wa) + ba)
    ig = _sigmoid(_dot_nn(xc, wx) + bx)
    sp = _softplus(-lam)
    log_a = -LRU_C * r * sp
    a = jnp.exp(log_a)
    mult = jnp.sqrt(-_expm1(2.0 * log_a))
    return r, ig, sp, a, mult


def _lru_specs(S, R, reverse):
    nch = S // R
    ch = (lambda j: nch - 1 - j) if reverse else (lambda j: j)
    per = R // 8

    def col(off):
        return pl.BlockSpec((R, LANES), lambda h, j: (ch(j), off + h))

    def prev(off):
        return pl.BlockSpec((8, LANES), lambda h, j: (jnp.maximum(ch(j) * per - 1, 0), off + h))

    vec = pl.BlockSpec((1, LANES), lambda h, j: (0, h))
    cw = pl.BlockSpec((4, LANES), lambda h, j: (0, h))
    wsq = pl.BlockSpec((None, LANES, LANES), lambda h, j: (h, 0, 0))
    return nch, ch, col, prev, vec, cw, wsq


def lru_fwd(P, conv_w, conv_b, w_a, b_a, w_x, b_x, lam, tag):
    S = P.shape[0]
    R = min(LRU_R, S)
    nch, ch, col, prev, vec, cw_spec, wsq = _lru_specs(S, R, False)

    def body(xb_ref, halo_ref, gb_ref, cw_ref, cb_ref, wa_ref, ba_ref, wx_ref, bx_ref, lam_ref, xc_ref, h_ref, yb_ref, carry):
        j = pl.program_id(1)
        xb = xb_ref[...].astype(F32)
        halo = halo_ref[...].astype(F32) * jnp.where(j > 0, 1.0, 0.0)
        ext = jnp.concatenate([halo, xb], axis=0)
        cw = cw_ref[...]
        xc = cb_ref[...]
        for k in range(4):
            e = ext if k == 3 else pltpu.roll(ext, 3 - k, 0)
            xc = xc + e[8:] * cw[k:k + 1]
        r, ig, sp, a, mult = _lru_gates(xc, wa_ref[...], ba_ref[...], wx_ref[...], bx_ref[...], lam_ref[...])
        u = mult * (ig * xc)
        cum_a, hloc = _scan_fwd(a, u)

        @pl.when(j == 0)
        def _():
            carry[...] = jnp.zeros_like(carry)

        hfull = hloc + cum_a * carry[0:1]
        carry[...] = jnp.broadcast_to(hfull[R - 1:R], carry.shape)
        gel, _ = _gelu_and_grad(gb_ref[...].astype(F32))
        xc_ref[...] = xc
        h_ref[...] = hfull
        yb_ref[...] = (hfull * gel).astype(yb_ref.dtype)

    out = pl.BlockSpec((R, LANES), lambda h, j: (j, h))
    return pl.pallas_call(
        body, name=tag + "_lru", grid=(LRU_HEADS, nch),
        in_specs=[col(PK_XB // LANES), prev(PK_XB // LANES), col(PK_GB // LANES), cw_spec, vec, wsq, vec, wsq, vec, vec],
        out_specs=[out, out, out],
        out_shape=[jax.ShapeDtypeStruct((S, D), F32), jax.ShapeDtypeStruct((S, D), F32), jax.ShapeDtypeStruct((S, D), BF)],
        scratch_shapes=[pltpu.VMEM((8, LANES), F32)], compiler_params=_cparams(2),
    )(P, P, P, conv_w, conv_b, w_a, b_a, w_x, b_x, lam)


def lru_bwd(dyb0, P, hh, xc, conv_w, w_a, b_a, w_x, b_x, lam, tag):
    S = P.shape[0]
    R = min(LRU_R, S)
    nch, ch, col, prev, vec, cw_spec, wsq = _lru_specs(S, R, True)

    def body(dyb_ref, gb_ref, h_ref, hprev_ref, xc_ref, xb_ref, xbprev_ref, cw_ref, wa_ref, ba_ref, wx_ref, bx_ref, lam_ref,
             dgb_ref, dxb_ref, dwa_ref, dba_ref, dwx_ref, dbx_ref, dlam_ref, dcw_ref, dcb_ref, gcarry, dxc_head):
        j = pl.program_id(1)
        jj = nch - 1 - j
        has_prev = jnp.where(jj > 0, 1.0, 0.0)
        row = _iota((R, LANES), 0)
        xc, hv, lam = xc_ref[...], h_ref[...], lam_ref[...]
        wa, wx = wa_ref[...], wx_ref[...]
        r, ig, sp, a, mult = _lru_gates(xc, wa, ba_ref[...], wx, bx_ref[...], lam)
        dyb = dyb_ref[...].astype(F32)
        gel, dgel = _gelu_and_grad(gb_ref[...].astype(F32))
        dgb_ref[...] = (dyb * hv * dgel).astype(dgb_ref.dtype)

        @pl.when(j == 0)
        def _():
            gcarry[...] = jnp.zeros_like(gcarry)
            dxc_head[...] = jnp.zeros_like(dxc_head)
            for ref in (dwa_ref, dba_ref, dwx_ref, dbx_ref, dlam_ref, dcw_ref, dcb_ref):
                ref[...] = jnp.zeros_like(ref)

        B = jnp.where(row < R - 1, pltpu.roll(a, R - 1, 0), 0.0)
        X = dyb * gel + jnp.where(row == R - 1, gcarry[0:1], 0.0)
        G = _scan_bwd(B, X)
        gcarry[...] = jnp.broadcast_to(a[0:1] * G[0:1], gcarry.shape)
        hprev = jnp.where(row == 0, hprev_ref[...][7:8] * has_prev, pltpu.roll(hv, 1, 0))
        da = G * hprev
        dmult = G * ig * xc
        dig = G * mult * xc
        dxc = G * mult * ig
        dlog_a = da * a - dmult * (a * a) / mult
        dzr = dlog_a * (-LRU_C * sp) * (r * (1.0 - r))
        dzi = dig * (ig * (1.0 - ig))
        dxc = dxc + _dot_nt(dzr, wa) + _dot_nt(dzi, wx)
        dwa_ref[...] += _dot_tn(xc, dzr)
        dwx_ref[...] += _dot_tn(xc, dzi)
        dba_ref[...] += _colsum(dzr)
        dbx_ref[...] += _colsum(dzi)
        dlam_ref[...] += _colsum(dlog_a * (-LRU_C * r)) * (-_sigmoid(-lam))
        dcb_ref[...] += _colsum(dxc)
        cw = cw_ref[...]
        ext = jnp.concatenate([dxc, dxc_head[...]], axis=0)
        dxb = dxc * cw[3:4]
        for k in range(3):
            dxb = dxb + pltpu.roll(ext, R + 8 - (3 - k), 0)[:R] * cw[k:k + 1]
        dxb_ref[...] = dxb.astype(dxb_ref.dtype)
        dxc_head[...] = dxc[0:8]
        extx = jnp.concatenate([xbprev_ref[...].astype(F32) * has_prev, xb_ref[...].astype(F32)], axis=0)
        incs = []
        for k in range(4):
            e = extx if k == 3 else pltpu.roll(extx, 3 - k, 0)
            incs.append(_colsum(dxc * e[8:]))
        dcw_ref[...] += jnp.concatenate(incs, axis=0)

    plain = pl.BlockSpec((R, LANES), lambda h, j: (ch(j), h))
    plain_prev = pl.BlockSpec((8, LANES), lambda h, j: (jnp.maximum(ch(j) * (R // 8) - 1, 0), h))
    return pl.pallas_call(
        body, name=tag + "_dlru", grid=(LRU_HEADS, nch),
        in_specs=[plain, col(PK_GB // LANES), plain, plain_prev, plain, col(PK_XB // LANES), prev(PK_XB // LANES),
                  cw_spec, wsq, vec, wsq, vec, vec],
        out_specs=[plain, plain, wsq, vec, wsq, vec, vec, cw_spec, vec],
        out_shape=[jax.ShapeDtypeStruct((S, D), BF), jax.ShapeDtypeStruct((S, D), BF),
                   jax.ShapeDtypeStruct((LRU_HEADS, LANES, LANES), F32), jax.ShapeDtypeStruct((1, D), F32),
                   jax.ShapeDtypeStruct((LRU_HEADS, LANES, LANES), F32), jax.ShapeDtypeStruct((1, D), F32),
                   jax.ShapeDtypeStruct((1, D), F32), jax.ShapeDtypeStruct((4, D), F32), jax.ShapeDtypeStruct((1, D), F32)],
        scratch_shapes=[pltpu.VMEM((8, LANES), F32), pltpu.VMEM((8, LANES), F32)], compiler_params=_cparams(2),
    )(dyb0, P, hh, hh, xc, P, P, conv_w, w_a, b_a, w_x, b_x, lam)


CUM_TS = 512
FLASH_T = 512
FLASH_G = 8
NEG = -1e30


def cumlogf_fwd(fl, b_f, tag):
    ts = min(CUM_TS, fl.shape[0])

    def fn(step, i, n, fl, bf, carry):
        x = -_softplus(-(fl + bf))
        row = _iota(x.shape, 0)
        d = 1
        while d < ts:
            x = x + jnp.where(row >= d, pltpu.roll(x, d, 0), 0.0)
            d *= 2

        @pl.when(step == 0)
        def _():
            carry[...] = jnp.zeros_like(carry)

        c = x + carry[0:1]
        carry[...] = jnp.broadcast_to(c[ts - 1:ts], carry.shape)
        return (c,), ()

    return rowwise(fn, [fl], [b_f], [(LANES, F32)], scratch=[pltpu.VMEM((8, LANES), F32)], ts=ts, name=tag + "_cum")[0][0]


def cumlogf_bwd(dc, fl, b_f, tag):
    ts = min(CUM_TS, fl.shape[0])

    def fn(step, i, n, dc, fl, bf, carry):
        row = _iota(dc.shape, 0)
        x = dc
        d = 1
        while d < ts:
            x = x + jnp.where(row < ts - d, pltpu.roll(x, ts - d, 0), 0.0)
            d *= 2

        @pl.when(step == 0)
        def _():
            carry[...] = jnp.zeros_like(carry)

        g = x + carry[0:1]
        carry[...] = jnp.broadcast_to(g[0:1], carry.shape)
        dfl = g * _sigmoid(-(fl + bf))
        return (dfl,), (_colsum(dfl),)

    (dfl,), (db,) = rowwise(fn, [dc, fl], [b_f], [(LANES, F32)], [((1, LANES), F32)], scratch=[pltpu.VMEM((8, LANES), F32)],
                            ts=ts, name=tag + "_dcum", reverse=True)
    return dfl, db


FOX_KA = 80


def _split3(c):
    c = c.astype(F32)
    hi = lax.reduce_precision(c, 8, 7)
    r = c - hi
    mid = lax.reduce_precision(r, 8, 7)
    return hi.astype(BF), mid.astype(BF), (r - mid).astype(BF)


def fox_operands(q, k, ch):
    H, S, hd = q.shape
    qs = q * (FOX_HD ** -0.5)
    pieces = [p.astype(F32) for p in _split3(ch)]
    pad = FOX_KA - hd

    def unit(i):
        return (jnp.arange(FOX_KA) == hd + i).astype(F32)

    qa = jnp.pad(qs.transpose(0, 2, 1).astype(F32), ((0, 0), (0, pad), (0, 0)))
    ka = jnp.pad(k.astype(F32), ((0, 0), (0, 0), (0, pad)))
    for i, p in enumerate(pieces):
        qa = qa + p[:, None, :] * unit(i)[None, :, None] + unit(3 + i)[None, :, None]
        ka = ka + unit(i)[None, None, :] - p[:, :, None] * unit(3 + i)[None, None, :]
    return qa.astype(BF), ka.astype(BF), qs


def _causal(s):
    return jnp.where(_iota(s.shape, 0) <= _iota(s.shape, 1), s, NEG)


def flash_fwd(qa, ka, vt, tag):
    H, hd, S = vt.shape
    T = min(FLASH_T, S)
    nb = S // T

    pairs = [(qi, ki) for qi in range(nb) for ki in range(qi + 1)]
    q_tab = jnp.asarray([p[0] for p in pairs], jnp.int32)
    k_tab = jnp.asarray([p[1] for p in pairs], jnp.int32)

    def body(q_tab_ref, k_tab_ref, ka_ref, qa_ref, vt_ref, o_ref, lse_ref, m_s, l_s, acc):
        qi, ki = q_tab_ref[pl.program_id(1)], k_tab_ref[pl.program_id(1)]

        @pl.when(ki == 0)
        def _():
            m_s[...] = jnp.full_like(m_s, NEG)
            l_s[...] = jnp.zeros_like(l_s)
            acc[...] = jnp.zeros_like(acc)

        def step(diagonal):
            scores = []
            for g in range(FLASH_G):
                s = _dot_nn(ka_ref[g], qa_ref[g])
                scores.append(_causal(s) if diagonal else s)
            for g in range(FLASH_G):
                s = scores[g]
                m_new = jnp.maximum(m_s[g], jnp.max(s, axis=0, keepdims=True))
                alpha = jnp.exp(m_s[g] - m_new)
                p = jnp.exp(s - m_new)
                l_s[g] = alpha * l_s[g] + jnp.sum(p, axis=0, keepdims=True)
                p_hi = p.astype(BF)
                p_lo = p - p_hi.astype(F32)
                v = vt_ref[g]
                acc[g] = alpha * acc[g] + (_dot_nn(v, p_hi) + _dot_nn(v, p_lo))
                m_s[g] = m_new

        @pl.when(ki < qi)
        def _():
            step(False)

        @pl.when(ki == qi)
        def _():
            step(True)
            o_ref[...] = acc[...] / l_s[...]
            lse_ref[...] = m_s[...] + jnp.log(l_s[...])

    G = FLASH_G
    grid_spec = pltpu.PrefetchScalarGridSpec(
        num_scalar_prefetch=2, grid=(H // G, len(pairs)),
        in_specs=[pl.BlockSpec((G, T, FOX_KA), lambda h, p, qt, kt: (h, kt[p], 0)),
                  pl.BlockSpec((G, FOX_KA, T), lambda h, p, qt, kt: (h, 0, qt[p])),
                  pl.BlockSpec((G, hd, T), lambda h, p, qt, kt: (h, 0, kt[p]))],
        out_specs=[pl.BlockSpec((G, hd, T), lambda h, p, qt, kt: (h, 0, qt[p])),
                   pl.BlockSpec((G, 1, T), lambda h, p, qt, kt: (h, 0, qt[p]))],
        scratch_shapes=[pltpu.VMEM((G, 1, T), F32), pltpu.VMEM((G, 1, T), F32), pltpu.VMEM((G, hd, T), F32)])
    return pl.pallas_call(
        body, name=tag + "_flash", grid_spec=grid_spec,
        out_shape=[jax.ShapeDtypeStruct((H, hd, S), F32), jax.ShapeDtypeStruct((H, 1, S), F32)],
        compiler_params=_cparams(2))(q_tab, k_tab, ka, qa, vt)


def flash_bwd(qa, ka, qs, kt, v, ot, do, dot_, lse, tag):
    H, S, hd = v.shape
    T = min(FLASH_T, S)
    nb = S // T

    pairs = [(qi, ki) for ki in range(nb) for qi in range(ki, nb)]
    q_tab = jnp.asarray([p[0] for p in pairs], jnp.int32)
    k_tab = jnp.asarray([p[1] for p in pairs], jnp.int32)

    def body(q_tab_ref, k_tab_ref, ka_ref, qa_ref, qs_ref, kt_ref, v_ref, ot_ref, do_ref, dot_ref, lse_ref,
             dqt_ref, dk_ref, dv_ref, dc_ref, dk_acc, dv_acc, dc_acc):
        qi, ki = q_tab_ref[pl.program_id(1)], k_tab_ref[pl.program_id(1)]

        @pl.when(pl.program_id(1) == 0)
        def _():
            dqt_ref[...] = jnp.zeros_like(dqt_ref)

        @pl.when(qi == ki)
        def _():
            dk_acc[...] = jnp.zeros_like(dk_acc)
            dv_acc[...] = jnp.zeros_like(dv_acc)
            dc_acc[...] = jnp.zeros_like(dc_acc)

        def step(diagonal):
            for g in range(FLASH_G):
                s = _dot_nn(ka_ref[g], qa_ref[g])
                if diagonal:
                    s = _causal(s)
                p = jnp.exp(s - lse_ref[g])
                dot_v = dot_ref[g]
                dp = _dot_nn(v_ref[g], dot_v)
                delta = jnp.sum(dot_v.astype(F32) * ot_ref[g], axis=0, keepdims=True)
                ds = p * (dp - delta)
                part = ds[:, 0:LANES]
                for j in range(1, T // LANES):
                    part = part + ds[:, j * LANES:(j + 1) * LANES]
                dc_acc[g] += part
                dsb = ds.astype(BF)
                dv_acc[g] += _dot_nn(p, do_ref[g])
                dk_acc[g] += _dot_nn(dsb, qs_ref[g])
                dqt_ref[g, qi] += _dot_nn(kt_ref[g], dsb)

        @pl.when(qi > ki)
        def _():
            step(False)

        @pl.when(qi == ki)
        def _():
            step(True)

        @pl.when(qi == nb - 1)
        def _():
            dk_ref[...] = dk_acc[...]
            dv_ref[...] = dv_acc[...]
            for g in range(FLASH_G):
                dc_ref[g] = -jnp.sum(dc_acc[g].T, axis=0, keepdims=True)

    qblk = lambda h, p, qt, kt_: (h, qt[p], 0)
    qblk_t = lambda h, p, qt, kt_: (h, 0, qt[p])
    kblk = lambda h, p, qt, kt_: (h, kt_[p], 0)
    kblk_t = lambda h, p, qt, kt_: (h, 0, kt_[p])
    G = FLASH_G
    grid_spec = pltpu.PrefetchScalarGridSpec(
        num_scalar_prefetch=2, grid=(H // G, len(pairs)),
        in_specs=[pl.BlockSpec((G, T, FOX_KA), kblk), pl.BlockSpec((G, FOX_KA, T), qblk_t),
                  pl.BlockSpec((G, T, hd), qblk), pl.BlockSpec((G, hd, T), kblk_t), pl.BlockSpec((G, T, hd), kblk),
                  pl.BlockSpec((G, hd, T), qblk_t), pl.BlockSpec((G, T, hd), qblk), pl.BlockSpec((G, hd, T), qblk_t),
                  pl.BlockSpec((G, 1, T), qblk_t)],
        out_specs=[pl.BlockSpec((G, nb, hd, T), lambda h, p, qt, kt_: (h, 0, 0, 0)), pl.BlockSpec((G, T, hd), kblk),
                   pl.BlockSpec((G, T, hd), kblk), pl.BlockSpec((G, 1, T), kblk_t)],
        scratch_shapes=[pltpu.VMEM((G, T, hd), F32), pltpu.VMEM((G, T, hd), F32), pltpu.VMEM((G, T, LANES), F32)])
    return pl.pallas_call(
        body, name=tag + "_dflash", grid_spec=grid_spec,
        out_shape=[jax.ShapeDtypeStruct((H, nb, hd, T), F32), jax.ShapeDtypeStruct((H, S, hd), F32),
                   jax.ShapeDtypeStruct((H, S, hd), F32), jax.ShapeDtypeStruct((H, 1, S), F32)],
        compiler_params=_cparams(2))(q_tab, k_tab, ka, qa, qs, kt, v, ot, do, dot_, lse)


def _to_heads(x2d):
    S = x2d.shape[0]
    return x2d.reshape(S, FOX_HEADS, FOX_HD).transpose(1, 0, 2)


def _from_heads(x3d):
    S = x3d.shape[1]
    return x3d.transpose(1, 0, 2).reshape(S, FOX_W)


def _gl_rows(P):
    return [Rows(P, D, PK_GL // D + k) for k in range(3)]


def merge_fwd(P, ya, yb, yc, b_gate, tag):
    def fn(step, i, n, g0, g1, g2, ya, yb, yc, b):
        out = 0.0
        for k, (gl, y) in enumerate(((g0, ya), (g1, yb), (g2, yc))):
            out = out + _sigmoid(gl.astype(F32) + b[:, k * D:(k + 1) * D]) * y.astype(F32)
        return (out,), ()
    return rowwise(fn, _gl_rows(P) + [ya, yb, yc], [b_gate], [(D, BF)], ts=512, name=tag + "_merge")[0][0]


def merge_bwd(dm, P, ya, yb, yc, b_gate, tag):
    def fn(step, i, n, dm, g0, g1, g2, ya, yb, yc, b):
        dm = dm.astype(F32)
        dys, dgls = [], []
        for k, (gl, y) in enumerate(((g0, ya), (g1, yb), (g2, yc))):
            g = _sigmoid(gl.astype(F32) + b[:, k * D:(k + 1) * D])
            dys.append(dm * g)
            dgls.append(dm * y.astype(F32) * (g * (1.0 - g)))
        dgl = jnp.concatenate(dgls, axis=1)
        return (dys[0], dys[1], dys[2], dgl), (_colsum(dgl),)
    (dya, dyb, dyc, dgl), (db,) = rowwise(
        fn, [dm] + _gl_rows(P) + [ya, yb, yc], [b_gate], [(D, BF), (D, BF), (D, BF), (3 * D, BF)], [((1, 3 * D), F32)],
        ts=256, name=tag + "_dmerge")
    return dya, dyb, dyc, dgl, db


def _xattn_probs(qh, kh):
    s = _dot_nt(qh, kh) * (X_HD ** -0.5)
    e = jnp.exp(s - jnp.max(s, axis=1, keepdims=True))
    return e / jnp.sum(e, axis=1, keepdims=True)


def xattn_fwd(q, kv, tag):
    def fn(step, i, n, q, k, v):
        os = []
        for h in range(X_HEADS):
            sl = slice(h * X_HD, (h + 1) * X_HD)
            os.append(_dot_nn(_xattn_probs(q[:, sl], k[:, sl]), v[:, sl]))
        return (jnp.concatenate(os, axis=1),), ()

    return _xattn_call(fn, [q], kv, [(D, BF)], [], tag + "_xattn")[0][0]


def _xattn_call(fn, rows, kv, outs, accs, name):
    S, ts, M = rows[0].shape[0], 512, kv.shape[0]
    ts = min(ts, S)
    nr, no, na = len(rows), len(outs), len(accs)

    def body(*refs):
        in_refs, out_refs, acc_refs = refs[:nr + 2], refs[nr + 2:nr + 2 + no], refs[nr + 2 + no:]
        step = pl.program_id(0)
        o, inc = fn(step, step, S // ts, *[r[...] for r in in_refs])
        for ref, val in zip(out_refs, o):
            ref[...] = val.astype(ref.dtype)
        if na:
            @pl.when(step == 0)
            def _():
                for ref in acc_refs:
                    ref[...] = jnp.zeros_like(ref)
            for ref, val in zip(acc_refs, inc):
                ref[...] += val

    row = pl.BlockSpec((ts, D), lambda s: (s, 0))
    res = pl.pallas_call(
        body, name=name, grid=(S // ts,),
        in_specs=[row] * nr + [pl.BlockSpec((M, D), lambda s: (0, 0)), pl.BlockSpec((M, D), lambda s: (0, 1))],
        out_specs=[row] * no + [pl.BlockSpec(tuple(s), lambda s_: (0, 0)) for s, _ in accs],
        out_shape=[jax.ShapeDtypeStruct((S, w), dt) for w, dt in outs] + [jax.ShapeDtypeStruct(tuple(s), dt) for s, dt in accs],
        compiler_params=_cparams(1))(*rows, kv, kv)
    return res[:no], res[no:]


def xattn_bwd(q, kv, do, tag):
    def fn(step, i, n, q, do, k, v):
        dqs, dks, dvs = [], [], []
        for h in range(X_HEADS):
            sl = slice(h * X_HD, (h + 1) * X_HD)
            p = _xattn_probs(q[:, sl], k[:, sl])
            dp = _dot_nt(do[:, sl], v[:, sl])
            ds = (p * (dp - jnp.sum(p * dp, axis=1, keepdims=True)) * (X_HD ** -0.5)).astype(BF)
            dqs.append(_dot_nn(ds, k[:, sl]))
            dks.append(_dot_tn(ds, q[:, sl]))
            dvs.append(_dot_tn(p, do[:, sl]))
        return (jnp.concatenate(dqs, axis=1),), (jnp.concatenate(dks + dvs, axis=1),)

    (dq,), (dkv,) = _xattn_call(fn, [q, do], kv, [(D, BF)], [((kv.shape[0], 2 * D), F32)], tag + "_dxattn")
    return dq, dkv


def layer_fwd(x, h_in, mem, w, next_g, tag):
    x1, u, s_ffn1 = ffn_fwd(x, h_in, w["ffn1_in"], w["ffn1_out"], w["g_mix"], tag + "_ffn1")
    P = matmul(u, w["wp"], "nn", name=tag + "_proj")
    fl = matmul(u, w["wfl"], "nn", out_dtype=F32, name=tag + "_projf")
    d, ya0 = pool_fwd(P, w["w_pool"], w["pool_scale"], tag)
    ya = matmul(ya0, w["up_a"], "nn", name=tag + "_upa")
    xc, hh, yb0 = lru_fwd(P, w["conv_w"], w["conv_b"], w["w_rg_a"], w["b_rg_a"], w["w_rg_x"], w["b_rg_x"], w["lam"], tag)
    yb = matmul(yb0, w["up_b"], "nn", name=tag + "_upb")
    c = cumlogf_fwd(fl, w["b_f"], tag)
    q, k, v = (_to_heads(P[:, o:o + FOX_W]) for o in (PK_Q, PK_K, PK_V))
    qa, ka, qs = fox_operands(q, k, c[:, :FOX_HEADS].T)
    ot, lse = flash_fwd(qa, ka, v.transpose(0, 2, 1), tag)
    o2 = ot.transpose(2, 0, 1).reshape(-1, FOX_W)
    yc = matmul(o2, w["up_c"], "nn", name=tag + "_upc")
    merged = merge_fwd(P, ya, yb, yc, w["b_gate"], tag)
    x2, hq = matmul(merged, w["w_o"], "nn", out_dtype=F32, res=x1, norm_g=w["g_cross"], name=tag + "_wo")
    qx = matmul(hq, w["xq"], "nn", name=tag + "_xq")
    mn = rms_fwd(mem, w["g_mem"], name=tag + "_mrms")
    kv = matmul(mn, w["xkv"], "nn", name=tag + "_xkv")
    ox = xattn_fwd(qx, kv, tag)
    x3, h2 = matmul(ox, w["xo"], "nn", out_dtype=F32, res=x2, norm_g=w["g_ffn2"], name=tag + "_xo")
    x4, h_out, s_ffn2 = ffn_fwd(x3, h2, w["ffn2_in"], w["ffn2_out"], next_g, tag + "_ffn2")
    saved = dict(ffn1=s_ffn1, ffn2=s_ffn2, x1=x1, u=u, P=P, fl=fl, d=d, ya0=ya0, ya=ya, xc=xc, hh=hh, yb0=yb0, yb=yb,
                 qa=qa, ka=ka, qs=qs, k=k, v=v, ot=ot, lse=lse, o2=o2, yc=yc, merged=merged, x2=x2, hq=hq, qx=qx,
                 mn=mn, kv=kv, ox=ox)
    return x4, h_out, saved


def layer_bwd(dx4, mem, w, s, tag):
    g = {}
    dx3, g["g_ffn2"], g["w_ffn2_in"], g["w_ffn2_out"] = ffn_bwd(dx4, s["ffn2"], w["g_ffn2"], w["ffn2_in"], w["ffn2_out"], tag + "_ffn2")
    dox = matmul(dx3, w["xo"], "nt", name=tag + "_dox")
    g["w_xo"] = matmul(s["ox"], dx3, "tn", name=tag + "_dwxo")
    dqx, dkv = xattn_bwd(s["qx"], s["kv"], dox, tag)
    g["w_xq"] = matmul(s["hq"], dqx, "tn", name=tag + "_dwxq")
    dx2, g["g_cross"] = matmul_rms_bwd(dqx, w["xq"], s["x2"], w["g_cross"], dx3, name=tag + "_dhq")
    g["w_xkv"] = matmul(s["mn"], dkv, "tn", name=tag + "_dwxkv")
    dmn = matmul(dkv, w["xkv"], "nt", name=tag + "_dmn")
    _, g["g_mem"] = rms_bwd(mem, w["g_mem"], dmn, None, name=tag + "_dmrms")
    P = s["P"]
    dmerged = matmul(dx2, w["w_o"], "nt", name=tag + "_dmerged")
    g["w_o"] = matmul(s["merged"], dx2, "tn", name=tag + "_dwo")
    dya, dyb, dyc, dgl, g["b_gate"] = merge_bwd(dmerged, P, s["ya"], s["yb"], s["yc"], w["b_gate"], tag)
    dya0 = matmul(dya, w["up_a"], "nt", name=tag + "_dya0")
    g["w_up_a"] = matmul(s["ya0"], dya, "tn", name=tag + "_dwupa")
    dxa, g["w_pool"], g["pool_scale"] = pool_bwd(dya0, s["d"], w["w_pool"], w["pool_scale"], tag)
    dyb0 = matmul(dyb, w["up_b"], "nt", name=tag + "_dyb0")
    g["w_up_b"] = matmul(s["yb0"], dyb, "tn", name=tag + "_dwupb")
    (dgb, dxb, g["w_rg_a"], g["b_rg_a"], g["w_rg_x"], g["b_rg_x"], g["lru_lambda"], g["conv_w"], g["conv_b"]) = lru_bwd(
        dyb0, P, s["hh"], s["xc"], w["conv_w"], w["w_rg_a"], w["b_rg_a"], w["w_rg_x"], w["b_rg_x"], w["lam"], tag)
    do2 = matmul(dyc, w["up_c"], "nt", name=tag + "_do2")
    g["w_up_c"] = matmul(s["o2"], dyc, "tn", name=tag + "_dwupc")
    do = _to_heads(do2)
    dqt, dk, dv, dck = flash_bwd(s["qa"], s["ka"], s["qs"], s["k"].transpose(0, 2, 1), s["v"], s["ot"], do,
                                 do.transpose(0, 2, 1), s["lse"], tag)
    dq = (dqt * (FOX_HD ** -0.5)).transpose(0, 1, 3, 2).reshape(dk.shape)
    dc = jnp.pad(dck[:, 0, :].T, ((0, 0), (0, LANES - FOX_HEADS)))
    dfl, g["b_f"] = cumlogf_bwd(dc, s["fl"], w["b_f"], tag)
    dP = jnp.concatenate([dxb, dgb, dgl, dxa] + [_from_heads(t).astype(BF) for t in (dq, dk, dv)], axis=1)
    du = matmul(dP, w["wp"], "nt", out_dtype=F32, name=tag + "_du")
    dx1, g["g_mix"] = matmul_rms_bwd(dfl, w["wfl"], s["x1"], w["g_mix"], dx2, res=du, name=tag + "_duf")
    g["wp"] = matmul(s["u"], dP, "tn", name=tag + "_dwp")
    g["wfl"] = matmul(s["u"], dfl, "tn", out_dtype=F32, name=tag + "_dwfl")
    dx0, g["g_ffn1"], g["w_ffn1_in"], g["w_ffn1_out"] = ffn_bwd(dx1, s["ffn1"], w["g_ffn1"], w["ffn1_in"], w["ffn1_out"], tag + "_ffn1")
    return dx0, g


def loss_head(x, target, g_final):
    def fn(step, i, n, x, t, g):
        r = _rstd(x)
        xhat = x * r
        e = xhat * g - t
        dy = e * (1.0 / D)
        gd = dy * g
        dx = r * (gd - xhat * jnp.mean(xhat * gd, axis=-1, keepdims=True))
        loss = 0.5 * jnp.sum(jnp.mean(e * e, axis=-1, keepdims=True), axis=0, keepdims=True)
        return (dx,), (jnp.broadcast_to(loss, (1, LANES)), _colsum(dy * xhat))

    (dx,), (loss, dg) = rowwise(fn, [x, target], [g_final], [(D, F32)], [((1, LANES), F32), ((1, D), F32)], ts=512, name="loss_head")
    return loss[0, 0], dx, dg


BIG = (("w_ffn1_in", (D, 2 * DFF), 1), ("w_ffn1_out", (DFF, D), 0), ("w_in", (D, 7176), 1), ("w_up_a", (POOL_W, D), 1),
       ("conv_w", (4, D), 1), ("w_up_b", (D, D), 0), ("w_up_c", (FOX_W, D), 1), ("w_o", (D, D), 0), ("w_xq", (D, D), 0),
       ("w_xkv", (D, 2 * D), 1), ("w_xo", (D, D), 0), ("w_ffn2_in", (D, 2 * DFF), 1), ("w_ffn2_out", (DFF, D), 0))
SMALL = (("g_ffn1", (D,)), ("g_mix", (D,)), ("b_f", (FOX_HEADS,)), ("b_gate", (3 * D,)), ("w_pool", (4, LANES, LANES)),
         ("pool_scale", (POOL_W,)), ("conv_b", (D,)), ("w_rg_a", (LRU_HEADS, LANES, LANES)), ("b_rg_a", (D,)),
         ("w_rg_x", (LRU_HEADS, LANES, LANES)), ("b_rg_x", (D,)), ("lru_lambda", (D,)), ("g_cross", (D,)), ("g_mem", (D,)),
         ("g_ffn2", (D,)))
ORDER = ("g_ffn1", "w_ffn1_in", "w_ffn1_out", "g_mix", "w_in", "b_f", "b_gate", "w_pool", "pool_scale", "w_up_a", "conv_w",
         "conv_b", "w_rg_a", "b_rg_a", "w_rg_x", "b_rg_x", "lru_lambda", "w_up_b", "w_up_c", "w_o", "g_cross", "g_mem", "w_xq",
         "w_xkv", "w_xo", "g_ffn2", "w_ffn2_in", "w_ffn2_out", "g_final")

IN_SPLIT = (("xa", 0, 512), ("xb", 512, 1024), ("gb", 1536, 1024), ("q", 2560, 512), ("k", 3072, 512), ("v", 3584, 512),
            ("fl", 4096, 8), ("gl", 4104, 3072))
PACK_ORDER = ("xb", "gb", "gl", "xa", "q", "k", "v")


def _shard_shape(shape, axis):
    s = list(shape)
    s[axis] //= N_CHIPS
    return (DEPTH, *s)


def _round_up(n, m):
    return -(-n // m) * m


SMALL_SEG_ROWS = 8


def _small_rows(n_elements):
    return _round_up(-(-n_elements // D), SMALL_SEG_ROWS)


def _pack_small(tree, names, row_multiple=SMALL_SEG_ROWS, fill=0.0):
    parts = []
    for n in names:
        flat = tree[n].astype(F32).reshape(-1)
        rows = _small_rows(flat.shape[0])
        parts.append(jnp.pad(flat, (0, rows * D - flat.shape[0]), constant_values=fill).reshape(rows, D))
    total = sum(p.shape[0] for p in parts)
    if total % row_multiple:
        parts.append(jnp.full((_round_up(total, row_multiple) - total, D), fill, F32))
    return jnp.concatenate(parts, axis=0)


def _unpack_small(buf, like, names):
    out, off = {}, 0
    for n in names:
        sh = like[n].shape
        rows = _small_rows(math.prod(sh))
        out[n] = buf[off:off + rows].reshape(-1)[:math.prod(sh)].reshape(sh)
        off += rows
    return out


def _from_chip_major(g, shape, axis):
    return jnp.moveaxis(g, 1, axis + 1).reshape(DEPTH, *shape)


def _to_chip_major(full, shape, axis):
    sh = list(shape)
    sh[axis:axis + 1] = [N_CHIPS, shape[axis] // N_CHIPS]
    return jnp.moveaxis(full.reshape(DEPTH, *sh), axis + 1, 1)


def _layer_weights(full, small, l):
    w_in = full["w_in"][l]
    cols = {n: w_in[:, o:o + s] for n, o, s in IN_SPLIT}
    w = dict(ffn1_in=full["w_ffn1_in"][l], ffn1_out=full["w_ffn1_out"][l], ffn2_in=full["w_ffn2_in"][l], ffn2_out=full["w_ffn2_out"][l],
             wp=jnp.concatenate([cols[n] for n in PACK_ORDER], axis=1),
             wfl=jnp.pad(cols["fl"], ((0, 0), (0, LANES - FOX_HEADS))),
             up_a=full["w_up_a"][l], up_b=full["w_up_b"][l], up_c=full["w_up_c"][l], w_o=full["w_o"][l], xq=full["w_xq"][l],
             xkv=full["w_xkv"][l], xo=full["w_xo"][l], conv_w=full["conv_w"][l])
    for n in ("g_ffn1", "g_mix", "b_gate", "pool_scale", "conv_b", "b_rg_a", "b_rg_x", "g_cross", "g_mem", "g_ffn2"):
        w[n] = small[n][l].reshape(1, -1)
    w["lam"] = small["lru_lambda"][l].reshape(1, -1)
    w["b_f"] = jnp.pad(small["b_f"][l], (0, LANES - FOX_HEADS)).reshape(1, LANES)
    for n in ("w_pool", "w_rg_a", "w_rg_x"):
        w[n] = small[n][l]
    return w


def _unpack_w_in_grad(gwp, gwfl):
    pk = {}
    off = 0
    sizes = {n: s for n, _, s in IN_SPLIT}
    for n in PACK_ORDER:
        pk[n] = gwp[:, off:off + sizes[n]]
        off += sizes[n]
    pk["fl"] = gwfl[:, :FOX_HEADS].astype(gwp.dtype)
    return jnp.concatenate([pk[n] for n, _, _ in IN_SPLIT], axis=1)


def _place():
    x, y, c = lax.axis_index("x"), lax.axis_index("y"), lax.axis_index("c")
    others = [(1 - x, y), (x, 1 - y), (1 - x, 1 - y)]
    return x, y, c, others


def _rcopy(src, dst, send_sems, recv_sems, k, to):
    return pltpu.make_async_remote_copy(src_ref=src, dst_ref=dst, send_sem=send_sems.at[k], recv_sem=recv_sems.at[k],
                                        device_id=to, device_id_type=MESH)


def _comm_call(body, name, ins, out_shape, n_sems):
    return pl.pallas_call(
        body, name=name, in_specs=[ANY] * len(ins), out_specs=[ANY] * len(out_shape), out_shape=out_shape,
        scratch_shapes=[pltpu.SemaphoreType.DMA((n_sems,)), pltpu.SemaphoreType.DMA((n_sems,)), pltpu.SemaphoreType.DMA((n_sems,))],
    )(*ins)


def allgather_same_core(bufs):
    nb = len(bufs)

    def body(*refs):
        ins, outs = refs[:nb], refs[nb:2 * nb]
        send_sems, recv_sems, local_sems = refs[2 * nb:]
        x, y, c, others = _place()
        me = 2 * x + y
        sends = []
        for b in range(nb):
            for j, (ox, oy) in enumerate(others):
                cp = _rcopy(ins[b], outs[b].at[me], send_sems, recv_sems, 3 * b + j, (ox, oy, c))
                cp.start()
                sends.append(cp)
        for b in range(nb):
            for j, (ox, oy) in enumerate(others):
                theirs = outs[b].at[2 * ox + oy]
                _rcopy(theirs, theirs, send_sems, recv_sems, 3 * b + j, (ox, oy, c)).wait_recv()
        for cp in sends:
            cp.wait_send()

    shapes = [jax.ShapeDtypeStruct((N_CHIPS, *b.shape), b.dtype) for b in bufs]
    outs = _comm_call(body, "allgather_same_core", list(bufs), shapes, 3 * nb)
    return [_with_own_block(o, b) for o, b in zip(outs, bufs)]


def _with_own_block(gathered, mine):
    me = 2 * lax.axis_index("x") + lax.axis_index("y")
    return lax.dynamic_update_index_in_dim(gathered, mine, me, axis=0)


def gather_weights(shards):
    nw = len(shards)

    def body(*refs):
        ins, outs = refs[:nw], refs[nw:2 * nw]
        send_sems, recv_sems, _ = refs[2 * nw:]
        x, y, c, _ = _place()
        fx, fy = 1 - c, c
        first = (x + fx - 2 * x * fx, y + fy - 2 * y * fy)
        second = (x + fy - 2 * x * fy, y + fx - 2 * y * fx)
        me, c1, c2, cd = 2 * x + y, 2 * first[0] + first[1], 2 * second[0] + second[1], 2 * (1 - x) + (1 - y)
        sibling = (x, y, 1 - c)
        started = []

        def send(w, k, src, dst, to):
            cp = _rcopy(src, dst, send_sems, recv_sems, 6 * w + k, to)
            cp.start()
            started.append(cp)

        def landed(w, k, block):
            _rcopy(block, block, send_sems, recv_sems, 6 * w + k, sibling).wait_recv()

        for w in range(nw):
            send(w, 0, ins[w].at[c], outs[w].at[c, me], (*first, c))
            send(w, 1, ins[w].at[c], outs[w].at[c, me], (*second, c))
        for w in range(nw):
            block = outs[w].at[c, c1]
            landed(w, 0, block)
            send(w, 2, block, block, (*second, c))
            send(w, 3, block, block, sibling)
        for w in range(nw):
            block = outs[w].at[c, c2]
            landed(w, 1, block)
            send(w, 4, block, block, sibling)
        for w in range(nw):
            block = outs[w].at[c, cd]
            landed(w, 2, block)
            send(w, 5, block, block, sibling)
        for w in range(nw):
            landed(w, 3, outs[w].at[1 - c, c2])
            landed(w, 4, outs[w].at[1 - c, c1])
            landed(w, 5, outs[w].at[1 - c, cd])
        for cp in started:
            cp.wait_send()

    shapes = [jax.ShapeDtypeStruct((DEPTH, N_CHIPS, *s.shape[1:]), s.dtype) for s in shards]
    outs = _comm_call(body, "gather_weights", list(shards), shapes, 6 * nw)
    me = 2 * lax.axis_index("x") + lax.axis_index("y")
    return [lax.dynamic_update_index_in_dim(o, s, me, axis=1) for o, s in zip(outs, shards)]


def sibling_layers(bufs):
    nb = len(bufs)

    def body(*refs):
        ins, outs = refs[:nb], refs[nb:2 * nb]
        send_sems, recv_sems, _ = refs[2 * nb:]
        x, y, c, _ = _place()
        work = []
        for b in range(nb):
            cp = _rcopy(ins[b].at[1 - c], outs[b], send_sems, recv_sems, b, (x, y, 1 - c))
            cp.start()
            work.append(cp)
        for cp in work:
            cp.wait()

    shapes = [jax.ShapeDtypeStruct(b.shape[1:], b.dtype) for b in bufs]
    return _comm_call(body, "sibling_layers", list(bufs), shapes, nb)


def chip_blocks(bufs):
    nb = len(bufs)

    def body(*refs):
        ins, outs = refs[:nb], refs[nb:2 * nb]
        send_sems, recv_sems, _ = refs[2 * nb:]
        x, y, c, others = _place()
        sends = []
        for b in range(nb):
            for j, (ox, oy) in enumerate(others):
                cp = _rcopy(ins[b].at[2 * ox + oy], outs[b].at[j], send_sems, recv_sems, 3 * b + j, (ox, oy, c))
                cp.start()
                sends.append(cp)
        for cp in sends:
            cp.wait()

    shapes = [jax.ShapeDtypeStruct((3, *b.shape[1:]), b.dtype) for b in bufs]
    return _comm_call(body, "chip_blocks", list(bufs), shapes, 3 * nb)


def sibling_other_layer(bufs):
    nb = len(bufs)

    def body(*refs):
        ins, outs = refs[:nb], refs[nb:2 * nb]
        send_sems, recv_sems, _ = refs[2 * nb:]
        x, y, c, _ = _place()
        work = []
        for b in range(nb):
            cp = _rcopy(ins[b], outs[b].at[c], send_sems, recv_sems, b, (x, y, 1 - c))
            cp.start()
            work.append(cp)
        for b, cp in enumerate(work):
            cp.wait_send()
            theirs = outs[b].at[1 - c]
            _rcopy(theirs, theirs, send_sems, recv_sems, b, (x, y, 1 - c)).wait_recv()

    shapes = [jax.ShapeDtypeStruct((DEPTH, *b.shape), b.dtype) for b in bufs]
    outs = _comm_call(body, "sibling_other_layer", list(bufs), shapes, nb)
    c = lax.axis_index("c")
    return [lax.dynamic_update_index_in_dim(o, b, c, axis=0) for o, b in zip(outs, bufs)]


ADD_BLOCK_BYTES = 2 * 2 ** 20


def add_arrays(parts, out_dtype, name):
    shape = parts[0].shape
    cols = shape[-1]
    rows = math.prod(shape[:-1])

    def fn(step, i, n, *vals):
        acc = vals[0].astype(F32)
        for v in vals[1:]:
            acc = acc + v.astype(F32)
        return (acc,), ()

    ts = rows
    for cand in (2048, 1024, 512, 256, 128, 64, 32, 16):
        if rows % cand == 0 and cand * cols * 4 <= ADD_BLOCK_BYTES:
            ts = cand
            break
    out = rowwise(fn, [p.reshape(rows, cols) for p in parts], outs=[(cols, out_dtype)], ts=ts, name=name)[0][0]
    return out.reshape(shape)


def reduce_gradients(bufs, names):
    c, me = lax.axis_index("c"), 2 * lax.axis_index("x") + lax.axis_index("y")
    theirs = sibling_layers(bufs)
    mine = [lax.dynamic_index_in_dim(b, c, axis=0, keepdims=False) for b in bufs]
    part = [add_arrays([m, t], m.dtype, "rs_add_sibling_" + n) for m, t, n in zip(mine, theirs, names)]
    recv = chip_blocks(part)
    own = [lax.dynamic_index_in_dim(p, me, axis=0, keepdims=False) for p in part]
    red = [add_arrays([o, r[0], r[1], r[2]], F32, "rs_add_chips_" + n) for o, r, n in zip(own, recv, names)]
    return sibling_other_layer(red)


_BC1 = 1.0 - ADAM_B1 ** ADAM_STEP
_BC2 = 1.0 - ADAM_B2 ** ADAM_STEP


def adamw(w, g, m, v, name):
    rows, cols = w.shape

    def fn(step, i, n, w, g, m, v):
        m2 = ADAM_B1 * m + (1.0 - ADAM_B1) * g
        v2 = ADAM_B2 * v + (1.0 - ADAM_B2) * (g * g)
        delta = -ADAM_LR * ((m2 / _BC1) / (jnp.sqrt(v2 / _BC2) + ADAM_EPS) + ADAM_WD * w)
        return (delta, m2, v2), ()

    ts = _pick(rows, (256, 128, 64, 32, 16, 8))
    return rowwise(fn, [w, g, m, v], outs=[(cols, F32)] * 3, ts=ts, name=name)[0]


def local_step(x, mem, target, full, small):
    ws = [_layer_weights(full, small, l) for l in range(DEPTH)]
    saved = []
    h, hn = x, rms_fwd(x, ws[0]["g_ffn1"], name="l_ffn1_rms")
    for l in range(DEPTH):
        h, hn, s = layer_fwd(h, hn, mem, ws[l], ws[l + 1]["g_ffn1"] if l + 1 < DEPTH else None, "l")
        saved.append(s)
    loss, dh, dg_final = loss_head(h, target, small["g_final"].reshape(1, D))
    grads = [None] * DEPTH
    for l in reversed(range(DEPTH)):
        dh, grads[l] = layer_bwd(dh, mem, ws[l], saved[l], "l")
    out = {}
    for l in range(DEPTH):
        g = grads[l]
        g["w_in"] = _unpack_w_in_grad(g.pop("wp"), g.pop("wfl"))
        g["b_f"] = g["b_f"][:, :FOX_HEADS]
    for name, shape, _ in BIG:
        out[name] = jnp.stack([grads[l][name].reshape(shape) for l in range(DEPTH)])
    for name, shape in SMALL:
        out[name] = jnp.stack([grads[l][name].reshape(shape) for l in range(DEPTH)])
    out["g_final"] = dg_final.reshape(D)
    return loss, dh, out


def kernel(x, mem, g_ffn1, w_ffn1_in, w_ffn1_out, g_mix, w_in, b_f, b_gate, w_pool, pool_scale, w_up_a, conv_w, conv_b, w_rg_a, b_rg_a, w_rg_x, b_rg_x, lru_lambda, w_up_b, w_up_c, w_o, g_cross, g_mem, w_xq, w_xkv, w_xo, g_ffn2, w_ffn2_in, w_ffn2_out, g_final, loss_target, m_g_ffn1, m_w_ffn1_in, m_w_ffn1_out, m_g_mix, m_w_in, m_b_f, m_b_gate, m_w_pool, m_pool_scale, m_w_up_a, m_conv_w, m_conv_b, m_w_rg_a, m_b_rg_a, m_w_rg_x, m_b_rg_x, m_lru_lambda, m_w_up_b, m_w_up_c, m_w_o, m_g_cross, m_g_mem, m_w_xq, m_w_xkv, m_w_xo, m_g_ffn2, m_w_ffn2_in, m_w_ffn2_out, m_g_final, v_g_ffn1, v_w_ffn1_in, v_w_ffn1_out, v_g_mix, v_w_in, v_b_f, v_b_gate, v_w_pool, v_pool_scale, v_w_up_a, v_conv_w, v_conv_b, v_w_rg_a, v_b_rg_a, v_w_rg_x, v_b_rg_x, v_lru_lambda, v_w_up_b, v_w_up_c, v_w_o, v_g_cross, v_g_mem, v_w_xq, v_w_xkv, v_w_xo, v_g_ffn2, v_w_ffn2_in, v_w_ffn2_out, v_g_final):
    args = dict(locals())
    weights = {n: args[n] for n in ORDER}
    m_in = {n: args["m_" + n] for n in ORDER}
    v_in = {n: args["v_" + n] for n in ORDER}
    big_names = [n for n, _, _ in BIG]
    small_names = [n for n, _ in SMALL] + ["g_final"]

    shards = []
    for n in big_names:
        shards += list(_split3(weights[n])) if n == "conv_w" else [weights[n].astype(BF)]
    gathered = iter(gather_weights(shards))
    full = {}
    for n, shape, axis in BIG:
        g = next(gathered)
        if n == "conv_w":
            g = g.astype(F32) + next(gathered).astype(F32) + next(gathered).astype(F32)
        full[n] = _from_chip_major(g, shape, axis)

    small = {n: weights[n] for n in small_names}
    loss, grad_x, contrib = local_step(x[0], mem[0], loss_target[0], full, small)
    loss = lax.psum(loss, ("x", "y", "c"))

    bufs = [_to_chip_major(contrib[n].astype(BF), shape, axis) for n, shape, axis in BIG]
    small_buf = _pack_small(contrib, small_names, DEPTH * N_CHIPS * SMALL_SEG_ROWS).reshape(DEPTH, N_CHIPS, -1, D)
    red = reduce_gradients(bufs + [small_buf], big_names + ["small"])

    grads = dict(zip(big_names, red[:-1]))
    red_small = allgather_same_core([red[-1]])[0].transpose(1, 0, 2, 3).reshape(-1, D)
    grads.update(_unpack_small(red_small, weights, small_names))

    delta, new_m, new_v = {}, {}, {}
    for name in big_names:
        sh = weights[name].shape
        two_d = (-1, sh[-1])
        d, m2, v2 = adamw(weights[name].reshape(two_d), grads[name].reshape(two_d), m_in[name].reshape(two_d),
                          v_in[name].reshape(two_d), "adamw_" + name)
        delta[name], new_m[name], new_v[name] = d.reshape(sh), m2.reshape(sh), v2.reshape(sh)
    outs = adamw(_pack_small(weights, small_names), _pack_small(grads, small_names), _pack_small(m_in, small_names),
                 _pack_small(v_in, small_names, fill=1.0), "adamw_small")
    for t, res in zip((delta, new_m, new_v), outs):
        t.update(_unpack_small(res, weights, small_names))

    return (loss, grad_x[None], *[grads[n] for n in ORDER], *[delta[n] for n in ORDER],
            *[new_m[n] for n in ORDER], *[new_v[n] for n in ORDER])
```

```python
import math

import jax
import jax.numpy as jnp
from jax import lax
from jax.experimental import pallas as pl
from jax.experimental.pallas import tpu as pltpu

F32 = jnp.float32
BF = jnp.bfloat16

D = 1024
DFF = 2816
DEPTH = 2
POOL_W = 512
POOL_WINDOWS = (2, 4, 8, 16)
LRU_HEADS = 8
LRU_C = 8.0
FOX_HEADS = 8
FOX_HD = 64
FOX_W = 512
X_HEADS = 4
X_HD = 256
EPS = 1e-6
LANES = 128
N_CHIPS = 4

ADAM_LR, ADAM_B1, ADAM_B2, ADAM_EPS, ADAM_WD, ADAM_STEP = 0.001, 0.9, 0.999, 1e-08, 0.01, 10

VMEM_LIMIT_BYTES = 56 * 2 ** 20

PK_XB, PK_GB, PK_GL, PK_XA, PK_Q, PK_K, PK_V = 0, 1024, 2048, 5120, 5632, 6144, 6656
PK_W = 7168

MESH = pl.DeviceIdType.MESH
ANY = pl.BlockSpec(memory_space=pl.ANY)


def _cparams(ngrid):
    return pltpu.CompilerParams(dimension_semantics=("arbitrary",) * ngrid, vmem_limit_bytes=VMEM_LIMIT_BYTES)


def _pick(n, cands):
    for c in cands:
        if n % c == 0:
            return c
    return n


def _iota(shape, dim):
    return lax.broadcasted_iota(jnp.int32, shape, dim)


def _sigmoid(x):
    return 1.0 / (1.0 + jnp.exp(-x))


def _softplus(z):
    return jnp.maximum(z, 0.0) + jnp.log1p(jnp.exp(-jnp.abs(z)))


def _expm1(x):
    small = jnp.abs(x) < 0.25
    xs = jnp.where(small, x, 0.0)
    poly = xs * (1.0 + xs * (1 / 2 + xs * (1 / 6 + xs * (1 / 24 + xs * (1 / 120 + xs * (1 / 720 + xs * (1 / 5040 + xs * (1 / 40320))))))))
    return jnp.where(small, poly, jnp.exp(x) - 1.0)


_GELU_K = math.sqrt(2.0 / math.pi)


def _gelu_and_grad(x):
    inner = _GELU_K * (x + 0.044715 * x * x * x)
    t = jnp.tanh(inner)
    g = 0.5 * x * (1.0 + t)
    dg = 0.5 * (1.0 + t) + 0.5 * x * (1.0 - t * t) * _GELU_K * (1.0 + 3 * 0.044715 * x * x)
    return g, dg


def _dot(a, b, dims):
    return lax.dot_general(a.astype(BF), b.astype(BF), (dims, ((), ())), preferred_element_type=F32)


def _dot_nn(a, b):
    return _dot(a, b, ((1,), (0,)))


def _dot_nt(a, b):
    return _dot(a, b, ((1,), (1,)))


def _dot_tn(a, b):
    return _dot(a, b, ((0,), (0,)))


def _colsum(x):
    return jnp.sum(x, axis=0, keepdims=True)


_TM = (1024, 1408, 512, 256, 128)
_TN = (1408, 1024, 512, 384, 256, 128)
_TK = (2432, 1408, 1024, 512, 256, 128)


def matmul(a, b, mode, *, out_dtype=None, alpha=1.0, res=None, norm_g=None, name):
    out_dtype = BF if out_dtype is None else out_dtype
    if mode == "nn":
        (M, K), N = a.shape, b.shape[1]
    elif mode == "nt":
        (M, K), N = a.shape, b.shape[0]
    else:
        (K, M), N = a.shape, b.shape[1]
    tm, tn, tk = _pick(M, _TM), _pick(N, _TN), _pick(K, _TK)
    nk = K // tk
    dims = {"nn": ((1,), (0,)), "nt": ((1,), (1,)), "tn": ((0,), (0,))}[mode]
    a_spec = pl.BlockSpec((tk, tm), lambda i, j, k: (k, i)) if mode == "tn" else pl.BlockSpec((tm, tk), lambda i, j, k: (i, k))
    b_spec = pl.BlockSpec((tn, tk), lambda i, j, k: (j, k)) if mode == "nt" else pl.BlockSpec((tk, tn), lambda i, j, k: (k, j))
    o_spec = pl.BlockSpec((tm, tn), lambda i, j, k: (i, j))
    has_res, has_norm = res is not None, norm_g is not None
    assert not has_norm or tn == N, "the norm epilogue needs whole rows"

    def body(*refs):
        a_ref, b_ref = refs[0], refs[1]
        r_ref = refs[2] if has_res else None
        g_ref = refs[2 + has_res] if has_norm else None
        o_ref = refs[2 + has_res + has_norm]
        h_ref = refs[3 + has_res + has_norm] if has_norm else None
        acc = refs[-1]
        k = pl.program_id(2)

        @pl.when(k == 0)
        def _():
            acc[...] = jnp.zeros_like(acc)

        acc[...] += _dot(a_ref[...], b_ref[...], dims)

        @pl.when(k == nk - 1)
        def _():
            r = acc[...]
            if alpha != 1.0:
                r = r * alpha
            if has_res:
                r = r + r_ref[...].astype(F32)
            r = r.astype(o_ref.dtype)
            o_ref[...] = r
            if has_norm:
                rf = r.astype(F32)
                h_ref[...] = ((rf * _rstd(rf)) * g_ref[...]).astype(h_ref.dtype)

    ins = [a, b] + ([res] if has_res else []) + ([norm_g] if has_norm else [])
    specs = [a_spec, b_spec] + ([o_spec] if has_res else []) + ([pl.BlockSpec((1, N), lambda i, j, k: (0, 0))] if has_norm else [])
    out = pl.pallas_call(
        body, name=name, grid=(M // tm, N // tn, nk), in_specs=specs,
        out_specs=[o_spec, o_spec] if has_norm else o_spec,
        out_shape=([jax.ShapeDtypeStruct((M, N), out_dtype), jax.ShapeDtypeStruct((M, N), BF)] if has_norm
                   else jax.ShapeDtypeStruct((M, N), out_dtype)),
        scratch_shapes=[pltpu.VMEM((tm, tn), F32)], compiler_params=_cparams(3))(*ins)
    return tuple(out) if has_norm else out


def matmul_rms_bwd(a, b, x, g, dres, *, a2=None, res=None, name):
    (M, K), N = a.shape, b.shape[0]
    tm, tk = _pick(M, _TM), _pick(K, _TK)
    nk1 = K // tk
    nk = nk1 * (2 if a2 is not None else 1)
    has_a2, has_res, has_dres = a2 is not None, res is not None, dres is not None

    def body(*refs):
        a_ref, b_ref, x_ref, g_ref = refs[:4]
        rest = list(refs[4:])
        a2_ref = rest.pop(0) if has_a2 else None
        r_ref = rest.pop(0) if has_res else None
        d_ref = rest.pop(0) if has_dres else None
        dx_ref, dg_ref, acc = rest
        i, k = pl.program_id(0), pl.program_id(1)

        @pl.when(k == 0)
        def _():
            acc[...] = jnp.zeros_like(acc)

        @pl.when(jnp.logical_and(i == 0, k == 0))
        def _():
            dg_ref[...] = jnp.zeros_like(dg_ref)

        if has_a2:
            @pl.when(k < nk1)
            def _():
                acc[...] += _dot_nt(a_ref[...], b_ref[...])

            @pl.when(k >= nk1)
            def _():
                acc[...] += _dot_nt(a2_ref[...], b_ref[...])
        else:
            acc[...] += _dot_nt(a_ref[...], b_ref[...])

        @pl.when(k == nk - 1)
        def _():
            dh = acc[...]
            if has_res:
                dh = dh + r_ref[...]
            xv = x_ref[...]
            r = _rstd(xv)
            xhat = xv * r
            gd = dh * g_ref[...]
            dx = r * (gd - xhat * jnp.mean(xhat * gd, axis=-1, keepdims=True))
            if has_dres:
                dx = dx + d_ref[...]
            dx_ref[...] = dx
            dg_ref[...] += _colsum(dh * xhat)

    row = pl.BlockSpec((tm, N), lambda i, k: (i, 0))
    vec = pl.BlockSpec((1, N), lambda i, k: (0, 0))
    ins = [a, b, x, g] + ([a2] if has_a2 else []) + ([res] if has_res else []) + ([dres] if has_dres else [])
    specs = ([pl.BlockSpec((tm, tk), lambda i, k: (i, jnp.minimum(k, nk1 - 1))), pl.BlockSpec((N, tk), lambda i, k: (0, k)), row, vec]
             + ([pl.BlockSpec((tm, tk), lambda i, k: (i, jnp.maximum(k - nk1, 0)))] if has_a2 else [])
             + [row] * (has_res + has_dres))
    dx, dg = pl.pallas_call(
        body, name=name, grid=(M // tm, nk), in_specs=specs, out_specs=[row, vec],
        out_shape=[jax.ShapeDtypeStruct((M, N), F32), jax.ShapeDtypeStruct((1, N), F32)],
        scratch_shapes=[pltpu.VMEM((tm, N), F32)], compiler_params=_cparams(2))(*ins)
    return dx, dg


class Rows:
    def __init__(self, arr, w=None, cb=0, halo=None, h=8):
        self.arr, self.w, self.cb, self.halo, self.h = arr, (arr.shape[1] if w is None else w), cb, halo, h


def rowwise(fn, rows, consts=(), outs=(), accs=(), scratch=(), *, ts, name, reverse=False):
    rows = [r if isinstance(r, Rows) else Rows(r) for r in rows]
    S = rows[0].arr.shape[0]
    n = S // ts
    tile = (lambda s: n - 1 - s) if reverse else (lambda s: s)

    def row_spec(r):
        if r.halo is None:
            return pl.BlockSpec((ts, r.w), lambda s: (tile(s), r.cb))
        per, last = ts // r.h, S // r.h - 1
        if r.halo == "prev":
            return pl.BlockSpec((r.h, r.w), lambda s: (jnp.maximum(tile(s) * per - 1, 0), r.cb))
        return pl.BlockSpec((r.h, r.w), lambda s: (jnp.minimum((tile(s) + 1) * per, last), r.cb))

    def whole(shape):
        nd = len(shape)
        return pl.BlockSpec(tuple(shape), lambda s: (0,) * nd)

    nr, nc, no, na = len(rows), len(consts), len(outs), len(accs)

    def body(*refs):
        in_refs, out_refs = refs[:nr + nc], refs[nr + nc:nr + nc + no]
        acc_refs, scr = refs[nr + nc + no:nr + nc + no + na], refs[nr + nc + no + na:]
        step = pl.program_id(0)
        o, inc = fn(step, tile(step), n, *[r[...] for r in in_refs], *scr)
        for ref, val in zip(out_refs, o):
            ref[...] = val.astype(ref.dtype)
        if na:
            @pl.when(step == 0)
            def _():
                for ref in acc_refs:
                    ref[...] = jnp.zeros_like(ref)
            for ref, val in zip(acc_refs, inc):
                ref[...] += val.astype(ref.dtype)

    out_shape = [jax.ShapeDtypeStruct((S, w), dt) for w, dt in outs] + [jax.ShapeDtypeStruct(tuple(s), dt) for s, dt in accs]
    out_specs = [pl.BlockSpec((ts, w), lambda s: (tile(s), 0)) for w, _ in outs] + [whole(s) for s, _ in accs]
    res = pl.pallas_call(
        body, name=name, grid=(n,),
        in_specs=[row_spec(r) for r in rows] + [whole(c.shape) for c in consts],
        out_specs=out_specs, out_shape=out_shape, scratch_shapes=list(scratch),
        compiler_params=_cparams(1))(*[r.arr for r in rows], *consts)
    return res[:no], res[no:]


def _rstd(x):
    return lax.rsqrt(jnp.mean(x * x, axis=-1, keepdims=True) + EPS)


def rms_fwd(x, g, *, name, ts=512):
    def fn(step, i, n, x, g):
        return ((x * _rstd(x)) * g,), ()
    return rowwise(fn, [x], [g], [(D, BF)], ts=min(ts, x.shape[0]), name=name)[0][0]


def rms_bwd(x, g, dh, dres, *, name, ts=512):
    has_res = dres is not None

    def fn(step, i, n, x, dh, *rest):
        g = rest[-1]
        r = _rstd(x)
        xhat = x * r
        dh = dh.astype(F32)
        gd = dh * g
        dx = r * (gd - xhat * jnp.mean(xhat * gd, axis=-1, keepdims=True))
        if has_res:
            dx = dx + rest[0]
        return (dx,), (_colsum(dh * xhat),)

    rows = [x, dh] + ([dres] if has_res else [])
    o, a = rowwise(fn, rows, [g], [(D, F32)], [((1, D), F32)], ts=min(ts, x.shape[0]), name=name)
    return o[0], a[0]


def swiglu_in(h, w_in, name):
    M, K = h.shape
    tm, tn = _pick(M, _TM), _pick(DFF, _TN)
    nj = DFF // tn

    def body(h_ref, wa_ref, wb_ref, a_ref, b_ref, act_ref):
        hv = h_ref[...]
        a = _dot_nn(hv, wa_ref[...]).astype(a_ref.dtype)
        b = _dot_nn(hv, wb_ref[...]).astype(b_ref.dtype)
        a_ref[...] = a
        b_ref[...] = b
        af = a.astype(F32)
        act_ref[...] = ((af * _sigmoid(af)) * b.astype(F32)).astype(act_ref.dtype)

    out = pl.BlockSpec((tm, tn), lambda i, j: (i, j))
    return pl.pallas_call(
        body, name=name, grid=(M // tm, nj),
        in_specs=[pl.BlockSpec((tm, K), lambda i, j: (i, 0)), pl.BlockSpec((K, tn), lambda i, j: (0, j)),
                  pl.BlockSpec((K, tn), lambda i, j: (0, j + nj))],
        out_specs=[out, out, out], out_shape=[jax.ShapeDtypeStruct((M, DFF), BF)] * 3,
        compiler_params=_cparams(2))(h, w_in, w_in)


def ffn_fwd(x, h, w_in, w_out, next_g, tag):
    a, b, act = swiglu_in(h, w_in, tag + "_in")
    y = matmul(act, w_out, "nn", out_dtype=F32, alpha=0.5, res=x, norm_g=next_g, name=tag + "_out")
    y, h_next = y if next_g is not None else (y, None)
    return y, h_next, (x, h, a, b, act)


def swiglu_bwd(dy, w_out, a, b, name):
    M, K = dy.shape
    tm, tn = _pick(M, _TM), _pick(DFF, _TN)

    def body(dy_ref, w_ref, a_ref, b_ref, da_ref, db_ref):
        dact = 0.5 * _dot_nt(dy_ref[...], w_ref[...])
        av, bv = a_ref[...].astype(F32), b_ref[...].astype(F32)
        s = _sigmoid(av)
        da_ref[...] = (dact * bv * (s * (1.0 + av * (1.0 - s)))).astype(da_ref.dtype)
        db_ref[...] = (dact * (av * s)).astype(db_ref.dtype)

    tile = pl.BlockSpec((tm, tn), lambda i, j: (i, j))
    return pl.pallas_call(
        body, name=name, grid=(M // tm, DFF // tn),
        in_specs=[pl.BlockSpec((tm, K), lambda i, j: (i, 0)), pl.BlockSpec((tn, K), lambda i, j: (j, 0)), tile, tile],
        out_specs=[tile, tile], out_shape=[jax.ShapeDtypeStruct((M, DFF), BF)] * 2,
        compiler_params=_cparams(2))(dy, w_out, a, b)


def ffn_bwd(dy, saved, g, w_in, w_out, tag):
    x, h, a, b, act = saved
    da, db = swiglu_bwd(dy, w_out, a, b, tag + "_dact")
    dw_out = matmul(act, dy, "tn", alpha=0.5, name=tag + "_dwout")
    dw_in = jnp.concatenate([matmul(h, da, "tn", name=tag + "_dwin_a"), matmul(h, db, "tn", name=tag + "_dwin_b")], axis=1)
    dx, dg = matmul_rms_bwd(da, w_in, x, g, dy, a2=db, name=tag + "_dh")
    return dx, dg, dw_in, dw_out


POOL_TS = 256
POOL_HALO = 16


def pool_fwd(P, w_pool, pool_scale, tag):
    ts = POOL_TS

    def fn(step, i, n, xa, halo, w, scale):
        xa = xa.astype(F32)
        halo = halo.astype(F32) * jnp.where(i > 0, 1.0, 0.0)
        ext = jnp.concatenate([halo, xa], axis=0)
        pos = (i * ts + 1 + _iota((ts, LANES), 0)).astype(F32)
        ds, ys = [], []
        for gi, win in enumerate(POOL_WINDOWS):
            e = ext[:, gi * LANES:(gi + 1) * LANES]
            sh = 1
            while sh < win:
                e = e + pltpu.roll(e, sh, 0)
                sh *= 2
            mean = e[POOL_HALO:] / jnp.minimum(pos, float(win))
            d = mean - xa[:, gi * LANES:(gi + 1) * LANES]
            ds.append(d)
            ys.append(_dot_nn(d, w[gi]))
        d = jnp.concatenate(ds, axis=1)
        return (d, jnp.concatenate(ys, axis=1) * scale), ()

    xa = Rows(P, POOL_W, PK_XA // POOL_W)
    xa_prev = Rows(P, POOL_W, PK_XA // POOL_W, "prev", POOL_HALO)
    (d, ya0), _ = rowwise(fn, [xa, xa_prev], [w_pool, pool_scale], [(POOL_W, BF), (POOL_W, BF)], ts=ts, name=tag + "_pool")
    return d, ya0


def pool_bwd(dya0, d, w_pool, pool_scale, tag):
    ts = POOL_TS
    L = ts + POOL_HALO

    def fn(step, i, n, dya0, dya0_next, d, w, scale):
        d = d.astype(F32)
        dz = jnp.concatenate([dya0.astype(F32), dya0_next.astype(F32) * jnp.where(i < n - 1, 1.0, 0.0)], axis=0) * scale
        pos = (i * ts + 1 + _iota((L, LANES), 0)).astype(F32)
        dxa, dws, dsc = [], [], []
        for gi, win in enumerate(POOL_WINDOWS):
            sl = slice(gi * LANES, (gi + 1) * LANES)
            dzg = dz[:, sl]
            dd = _dot_nt(dzg, w[gi])
            e = dd / jnp.minimum(pos, float(win))
            sh = 1
            while sh < win:
                e = e + pltpu.roll(e, L - sh, 0)
                sh *= 2
            dxa.append(e[:ts] - dd[:ts])
            dws.append(_dot_tn(d[:, sl], dzg[:ts])[None])
            z = _dot_nn(d[:, sl], w[gi])
            dsc.append(_colsum(dya0[:, sl].astype(F32) * z))
        return (jnp.concatenate(dxa, axis=1),), (jnp.concatenate(dws, axis=0), jnp.concatenate(dsc, axis=1))

    (dxa,), (dw_pool, dscale) = rowwise(
        fn, [dya0, Rows(dya0, halo="next", h=POOL_HALO), d], [w_pool, pool_scale],
        [(POOL_W, BF)], [((4, LANES, LANES), F32), ((1, POOL_W), F32)], ts=ts, name=tag + "_dpool")
    return dxa, dw_pool, dscale


LRU_R = 512


def _scan_fwd(A, U):
    R = A.shape[0]
    row = _iota(A.shape, 0)
    d = 1
    while d < R:
        m = row >= d
        A_sh = jnp.where(m, pltpu.roll(A, d, 0), 1.0)
        U_sh = jnp.where(m, pltpu.roll(U, d, 0), 0.0)
        U = A * U_sh + U
        A = A * A_sh
        d *= 2
    return A, U


def _scan_bwd(B, X):
    R = B.shape[0]
    row = _iota(B.shape, 0)
    d = 1
    while d < R:
        m = row < R - d
        B_sh = jnp.where(m, pltpu.roll(B, R - d, 0), 1.0)
        X_sh = jnp.where(m, pltpu.roll(X, R - d, 0), 0.0)
        X = X + B * X_sh
        B = B * B_sh
        d *= 2
    return X


def _lru_gates(xc, wa, ba, wx, bx, lam):
    r = _sigmoid(_dot_nn(xc, wa) + ba)
    ig = _sigmoid(_dot_nn(xc, wx) + bx)
    sp = _softplus(-lam)
    log_a = -LRU_C * r * sp
    a = jnp.exp(log_a)
    mult = jnp.sqrt(-_expm1(2.0 * log_a))
    return r, ig, sp, a, mult


def _lru_specs(S, R, reverse):
    nch = S // R
    ch = (lambda j: nch - 1 - j) if reverse else (lambda j: j)
    per = R // 8

    def col(off):
        return pl.BlockSpec((R, LANES), lambda h, j: (ch(j), off + h))

    def prev(off):
        return pl.BlockSpec((8, LANES), lambda h, j: (jnp.maximum(ch(j) * per - 1, 0), off + h))

    vec = pl.BlockSpec((1, LANES), lambda h, j: (0, h))
    cw = pl.BlockSpec((4, LANES), lambda h, j: (0, h))
    wsq = pl.BlockSpec((None, LANES, LANES), lambda h, j: (h, 0, 0))
    return nch, ch, col, prev, vec, cw, wsq


def lru_fwd(P, conv_w, conv_b, w_a, b_a, w_x, b_x, lam, tag):
    S = P.shape[0]
    R = min(LRU_R, S)
    nch, ch, col, prev, vec, cw_spec, wsq = _lru_specs(S, R, False)

    def body(xb_ref, halo_ref, gb_ref, cw_ref, cb_ref, wa_ref, ba_ref, wx_ref, bx_ref, lam_ref, xc_ref, h_ref, yb_ref, carry):
        j = pl.program_id(1)
        xb = xb_ref[...].astype(F32)
        halo = halo_ref[...].astype(F32) * jnp.where(j > 0, 1.0, 0.0)
        ext = jnp.concatenate([halo, xb], axis=0)
        cw = cw_ref[...]
        xc = cb_ref[...]
        for k in range(4):
            e = ext if k == 3 else pltpu.roll(ext, 3 - k, 0)
            xc = xc + e[8:] * cw[k:k + 1]
        r, ig, sp, a, mult = _lru_gates(xc, wa_ref[...], ba_ref[...], wx_ref[...], bx_ref[...], lam_ref[...])
        u = mult * (ig * xc)
        cum_a, hloc = _scan_fwd(a, u)

        @pl.when(j == 0)
        def _():
            carry[...] = jnp.zeros_like(carry)

        hfull = hloc + cum_a * carry[0:1]
        carry[...] = jnp.broadcast_to(hfull[R - 1:R], carry.shape)
        gel, _ = _gelu_and_grad(gb_ref[...].astype(F32))
        xc_ref[...] = xc
        h_ref[...] = hfull
        yb_ref[...] = (hfull * gel).astype(yb_ref.dtype)

    out = pl.BlockSpec((R, LANES), lambda h, j: (j, h))
    return pl.pallas_call(
        body, name=tag + "_lru", grid=(LRU_HEADS, nch),
        in_specs=[col(PK_XB // LANES), prev(PK_XB // LANES), col(PK_GB // LANES), cw_spec, vec, wsq, vec, wsq, vec, vec],
        out_specs=[out, out, out],
        out_shape=[jax.ShapeDtypeStruct((S, D), F32), jax.ShapeDtypeStruct((S, D), F32), jax.ShapeDtypeStruct((S, D), BF)],
        scratch_shapes=[pltpu.VMEM((8, LANES), F32)], compiler_params=_cparams(2),
    )(P, P, P, conv_w, conv_b, w_a, b_a, w_x, b_x, lam)


def lru_bwd(dyb0, P, hh, xc, conv_w, w_a, b_a, w_x, b_x, lam, tag):
    S = P.shape[0]
    R = min(LRU_R, S)
    nch, ch, col, prev, vec, cw_spec, wsq = _lru_specs(S, R, True)

    def body(dyb_ref, gb_ref, h_ref, hprev_ref, xc_ref, xb_ref, xbprev_ref, cw_ref, wa_ref, ba_ref, wx_ref, bx_ref, lam_ref,
             dgb_ref, dxb_ref, dwa_ref, dba_ref, dwx_ref, dbx_ref, dlam_ref, dcw_ref, dcb_ref, gcarry, dxc_head):
        j = pl.program_id(1)
        jj = nch - 1 - j
        has_prev = jnp.where(jj > 0, 1.0, 0.0)
        row = _iota((R, LANES), 0)
        xc, hv, lam = xc_ref[...], h_ref[...], lam_ref[...]
        wa, wx = wa_ref[...], wx_ref[...]
        r, ig, sp, a, mult = _lru_gates(xc, wa, ba_ref[...], wx, bx_ref[...], lam)
        dyb = dyb_ref[...].astype(F32)
        gel, dgel = _gelu_and_grad(gb_ref[...].astype(F32))
        dgb_ref[...] = (dyb * hv * dgel).astype(dgb_ref.dtype)

        @pl.when(j == 0)
        def _():
            gcarry[...] = jnp.zeros_like(gcarry)
            dxc_head[...] = jnp.zeros_like(dxc_head)
            for ref in (dwa_ref, dba_ref, dwx_ref, dbx_ref, dlam_ref, dcw_ref, dcb_ref):
                ref[...] = jnp.zeros_like(ref)

        B = jnp.where(row < R - 1, pltpu.roll(a, R - 1, 0), 0.0)
        X = dyb * gel + jnp.where(row == R - 1, gcarry[0:1], 0.0)
        G = _scan_bwd(B, X)
        gcarry[...] = jnp.broadcast_to(a[0:1] * G[0:1], gcarry.shape)
        hprev = jnp.where(row == 0, hprev_ref[...][7:8] * has_prev, pltpu.roll(hv, 1, 0))
        da = G * hprev
        dmult = G * ig * xc
        dig = G * mult * xc
        dxc = G * mult * ig
        dlog_a = da * a - dmult * (a * a) / mult
        dzr = dlog_a * (-LRU_C * sp) * (r * (1.0 - r))
        dzi = dig * (ig * (1.0 - ig))
        dxc = dxc + _dot_nt(dzr, wa) + _dot_nt(dzi, wx)
        dwa_ref[...] += _dot_tn(xc, dzr)
        dwx_ref[...] += _dot_tn(xc, dzi)
        dba_ref[...] += _colsum(dzr)
        dbx_ref[...] += _colsum(dzi)
        dlam_ref[...] += _colsum(dlog_a * (-LRU_C * r)) * (-_sigmoid(-lam))
        dcb_ref[...] += _colsum(dxc)
        cw = cw_ref[...]
        ext = jnp.concatenate([dxc, dxc_head[...]], axis=0)
        dxb = dxc * cw[3:4]
        for k in range(3):
            dxb = dxb + pltpu.roll(ext, R + 8 - (3 - k), 0)[:R] * cw[k:k + 1]
        dxb_ref[...] = dxb.astype(dxb_ref.dtype)
        dxc_head[...] = dxc[0:8]
        extx = jnp.concatenate([xbprev_ref[...].astype(F32) * has_prev, xb_ref[...].astype(F32)], axis=0)
        incs = []
        for k in range(4):
            e = extx if k == 3 else pltpu.roll(extx, 3 - k, 0)
            incs.append(_colsum(dxc * e[8:]))
        dcw_ref[...] += jnp.concatenate(incs, axis=0)

    plain = pl.BlockSpec((R, LANES), lambda h, j: (ch(j), h))
    plain_prev = pl.BlockSpec((8, LANES), lambda h, j: (jnp.maximum(ch(j) * (R // 8) - 1, 0), h))
    return pl.pallas_call(
        body, name=tag + "_dlru", grid=(LRU_HEADS, nch),
        in_specs=[plain, col(PK_GB // LANES), plain, plain_prev, plain, col(PK_XB // LANES), prev(PK_XB // LANES),
                  cw_spec, wsq, vec, wsq, vec, vec],
        out_specs=[plain, plain, wsq, vec, wsq, vec, vec, cw_spec, vec],
        out_shape=[jax.ShapeDtypeStruct((S, D), BF), jax.ShapeDtypeStruct((S, D), BF),
                   jax.ShapeDtypeStruct((LRU_HEADS, LANES, LANES), F32), jax.ShapeDtypeStruct((1, D), F32),
                   jax.ShapeDtypeStruct((LRU_HEADS, LANES, LANES), F32), jax.ShapeDtypeStruct((1, D), F32),
                   jax.ShapeDtypeStruct((1, D), F32), jax.ShapeDtypeStruct((4, D), F32), jax.ShapeDtypeStruct((1, D), F32)],
        scratch_shapes=[pltpu.VMEM((8, LANES), F32), pltpu.VMEM((8, LANES), F32)], compiler_params=_cparams(2),
    )(dyb0, P, hh, hh, xc, P, P, conv_w, w_a, b_a, w_x, b_x, lam)


CUM_TS = 512
FLASH_T = 512
FLASH_G = 8
NEG = -1e30


def cumlogf_fwd(fl, b_f, tag):
    ts = min(CUM_TS, fl.shape[0])

    def fn(step, i, n, fl, bf, carry):
        x = -_softplus(-(fl + bf))
        row = _iota(x.shape, 0)
        d = 1
        while d < ts:
            x = x + jnp.where(row >= d, pltpu.roll(x, d, 0), 0.0)
            d *= 2

        @pl.when(step == 0)
        def _():
            carry[...] = jnp.zeros_like(carry)

        c = x + carry[0:1]
        carry[...] = jnp.broadcast_to(c[ts - 1:ts], carry.shape)
        return (c,), ()

    return rowwise(fn, [fl], [b_f], [(LANES, F32)], scratch=[pltpu.VMEM((8, LANES), F32)], ts=ts, name=tag + "_cum")[0][0]


def cumlogf_bwd(dc, fl, b_f, tag):
    ts = min(CUM_TS, fl.shape[0])

    def fn(step, i, n, dc, fl, bf, carry):
        row = _iota(dc.shape, 0)
        x = dc
        d = 1
        while d < ts:
            x = x + jnp.where(row < ts - d, pltpu.roll(x, ts - d, 0), 0.0)
            d *= 2

        @pl.when(step == 0)
        def _():
            carry[...] = jnp.zeros_like(carry)

        g = x + carry[0:1]
        carry[...] = jnp.broadcast_to(g[0:1], carry.shape)
        dfl = g * _sigmoid(-(fl + bf))
        return (dfl,), (_colsum(dfl),)

    (dfl,), (db,) = rowwise(fn, [dc, fl], [b_f], [(LANES, F32)], [((1, LANES), F32)], scratch=[pltpu.VMEM((8, LANES), F32)],
                            ts=ts, name=tag + "_dcum", reverse=True)
    return dfl, db


FOX_KA = 80


def _split3(c):
    c = c.astype(F32)
    hi = lax.reduce_precision(c, 8, 7)
    r = c - hi
    mid = lax.reduce_precision(r, 8, 7)
    return hi.astype(BF), mid.astype(BF), (r - mid).astype(BF)


def fox_operands(q, k, ch):
    H, S, hd = q.shape
    qs = q * (FOX_HD ** -0.5)
    pieces = [p.astype(F32) for p in _split3(ch)]
    pad = FOX_KA - hd

    def unit(i):
        return (jnp.arange(FOX_KA) == hd + i).astype(F32)

    qa = jnp.pad(qs.transpose(0, 2, 1).astype(F32), ((0, 0), (0, pad), (0, 0)))
    ka = jnp.pad(k.astype(F32), ((0, 0), (0, 0), (0, pad)))
    for i, p in enumerate(pieces):
        qa = qa + p[:, None, :] * unit(i)[None, :, None] + unit(3 + i)[None, :, None]
        ka = ka + unit(i)[None, None, :] - p[:, :, None] * unit(3 + i)[None, None, :]
    return qa.astype(BF), ka.astype(BF), qs


def _causal(s):
    return jnp.where(_iota(s.shape, 0) <= _iota(s.shape, 1), s, NEG)


def flash_fwd(qa, ka, vt, tag):
    H, hd, S = vt.shape
    T = min(FLASH_T, S)
    nb = S // T

    pairs = [(qi, ki) for qi in range(nb) for ki in range(qi + 1)]
    q_tab = jnp.asarray([p[0] for p in pairs], jnp.int32)
    k_tab = jnp.asarray([p[1] for p in pairs], jnp.int32)

    def body(q_tab_ref, k_tab_ref, ka_ref, qa_ref, vt_ref, o_ref, lse_ref, m_s, l_s, acc):
        qi, ki = q_tab_ref[pl.program_id(1)], k_tab_ref[pl.program_id(1)]

        @pl.when(ki == 0)
        def _():
            m_s[...] = jnp.full_like(m_s, NEG)
            l_s[...] = jnp.zeros_like(l_s)
            acc[...] = jnp.zeros_like(acc)

        def step(diagonal):
            scores = []
            for g in range(FLASH_G):
                s = _dot_nn(ka_ref[g], qa_ref[g])
                scores.append(_causal(s) if diagonal else s)
            for g in range(FLASH_G):
                s = scores[g]
                m_new = jnp.maximum(m_s[g], jnp.max(s, axis=0, keepdims=True))
                alpha = jnp.exp(m_s[g] - m_new)
                p = jnp.exp(s - m_new)
                l_s[g] = alpha * l_s[g] + jnp.sum(p, axis=0, keepdims=True)
                p_hi = p.astype(BF)
                p_lo = p - p_hi.astype(F32)
                v = vt_ref[g]
                acc[g] = alpha * acc[g] + (_dot_nn(v, p_hi) + _dot_nn(v, p_lo))
                m_s[g] = m_new

        @pl.when(ki < qi)
        def _():
            step(False)

        @pl.when(ki == qi)
        def _():
            step(True)
            o_ref[...] = acc[...] / l_s[...]
            lse_ref[...] = m_s[...] + jnp.log(l_s[...])

    G = FLASH_G
    grid_spec = pltpu.PrefetchScalarGridSpec(
        num_scalar_prefetch=2, grid=(H // G, len(pairs)),
        in_specs=[pl.BlockSpec((G, T, FOX_KA), lambda h, p, qt, kt: (h, kt[p], 0)),
                  pl.BlockSpec((G, FOX_KA, T), lambda h, p, qt, kt: (h, 0, qt[p])),
                  pl.BlockSpec((G, hd, T), lambda h, p, qt, kt: (h, 0, kt[p]))],
        out_specs=[pl.BlockSpec((G, hd, T), lambda h, p, qt, kt: (h, 0, qt[p])),
                   pl.BlockSpec((G, 1, T), lambda h, p, qt, kt: (h, 0, qt[p]))],
        scratch_shapes=[pltpu.VMEM((G, 1, T), F32), pltpu.VMEM((G, 1, T), F32), pltpu.VMEM((G, hd, T), F32)])
    return pl.pallas_call(
        body, name=tag + "_flash", grid_spec=grid_spec,
        out_shape=[jax.ShapeDtypeStruct((H, hd, S), F32), jax.ShapeDtypeStruct((H, 1, S), F32)],
        compiler_params=_cparams(2))(q_tab, k_tab, ka, qa, vt)


def flash_bwd(qa, ka, qs, kt, v, ot, do, dot_, lse, tag):
    H, S, hd = v.shape
    T = min(FLASH_T, S)
    nb = S // T

    pairs = [(qi, ki) for ki in range(nb) for qi in range(ki, nb)]
    q_tab = jnp.asarray([p[0] for p in pairs], jnp.int32)
    k_tab = jnp.asarray([p[1] for p in pairs], jnp.int32)

    def body(q_tab_ref, k_tab_ref, ka_ref, qa_ref, qs_ref, kt_ref, v_ref, ot_ref, do_ref, dot_ref, lse_ref,
             dqt_ref, dk_ref, dv_ref, dc_ref, dk_acc, dv_acc, dc_acc):
        qi, ki = q_tab_ref[pl.program_id(1)], k_tab_ref[pl.program_id(1)]

        @pl.when(pl.program_id(1) == 0)
        def _():
            dqt_ref[...] = jnp.zeros_like(dqt_ref)

        @pl.when(qi == ki)
        def _():
            dk_acc[...] = jnp.zeros_like(dk_acc)
            dv_acc[...] = jnp.zeros_like(dv_acc)
            dc_acc[...] = jnp.zeros_like(dc_acc)

        def step(diagonal):
            for g in range(FLASH_G):
                s = _dot_nn(ka_ref[g], qa_ref[g])
                if diagonal:
                    s = _causal(s)
                p = jnp.exp(s - lse_ref[g])
                dot_v = dot_ref[g]
                dp = _dot_nn(v_ref[g], dot_v)
                delta = jnp.sum(dot_v.astype(F32) * ot_ref[g], axis=0, keepdims=True)
                ds = p * (dp - delta)
                part = ds[:, 0:LANES]
                for j in range(1, T // LANES):
                    part = part + ds[:, j * LANES:(j + 1) * LANES]
                dc_acc[g] += part
                dsb = ds.astype(BF)
                dv_acc[g] += _dot_nn(p, do_ref[g])
                dk_acc[g] += _dot_nn(dsb, qs_ref[g])
                dqt_ref[g, qi] += _dot_nn(kt_ref[g], dsb)

        @pl.when(qi > ki)
        def _():
            step(False)

        @pl.when(qi == ki)
        def _():
            step(True)

        @pl.when(qi == nb - 1)
        def _():
            dk_ref[...] = dk_acc[...]
            dv_ref[...] = dv_acc[...]
            for g in range(FLASH_G):
                dc_ref[g] = -jnp.sum(dc_acc[g].T, axis=0, keepdims=True)

    qblk = lambda h, p, qt, kt_: (h, qt[p], 0)
    qblk_t = lambda h, p, qt, kt_: (h, 0, qt[p])
    kblk = lambda h, p, qt, kt_: (h, kt_[p], 0)
    kblk_t = lambda h, p, qt, kt_: (h, 0, kt_[p])
    G = FLASH_G
    grid_spec = pltpu.PrefetchScalarGridSpec(
        num_scalar_prefetch=2, grid=(H // G, len(pairs)),
        in_specs=[pl.BlockSpec((G, T, FOX_KA), kblk), pl.BlockSpec((G, FOX_KA, T), qblk_t),
                  pl.BlockSpec((G, T, hd), qblk), pl.BlockSpec((G, hd, T), kblk_t), pl.BlockSpec((G, T, hd), kblk),
                  pl.BlockSpec((G, hd, T), qblk_t), pl.BlockSpec((G, T, hd), qblk), pl.BlockSpec((G, hd, T), qblk_t),
                  pl.BlockSpec((G, 1, T), qblk_t)],
        out_specs=[pl.BlockSpec((G, nb, hd, T), lambda h, p, qt, kt_: (h, 0, 0, 0)), pl.BlockSpec((G, T, hd), kblk),
                   pl.BlockSpec((G, T, hd), kblk), pl.BlockSpec((G, 1, T), kblk_t)],
        scratch_shapes=[pltpu.VMEM((G, T, hd), F32), pltpu.VMEM((G, T, hd), F32), pltpu.VMEM((G, T, LANES), F32)])
    return pl.pallas_call(
        body, name=tag + "_dflash", grid_spec=grid_spec,
        out_shape=[jax.ShapeDtypeStruct((H, nb, hd, T), F32), jax.ShapeDtypeStruct((H, S, hd), F32),
                   jax.ShapeDtypeStruct((H, S, hd), F32), jax.ShapeDtypeStruct((H, 1, S), F32)],
        compiler_params=_cparams(2))(q_tab, k_tab, ka, qa, qs, kt, v, ot, do, dot_, lse)


def _to_heads(x2d):
    S = x2d.shape[0]
    return x2d.reshape(S, FOX_HEADS, FOX_HD).transpose(1, 0, 2)


def _from_heads(x3d):
    S = x3d.shape[1]
    return x3d.transpose(1, 0, 2).reshape(S, FOX_W)


def _gl_rows(P):
    return [Rows(P, D, PK_GL // D + k) for k in range(3)]


def merge_fwd(P, ya, yb, yc, b_gate, tag):
    def fn(step, i, n, g0, g1, g2, ya, yb, yc, b):
        out = 0.0
        for k, (gl, y) in enumerate(((g0, ya), (g1, yb), (g2, yc))):
            out = out + _sigmoid(gl.astype(F32) + b[:, k * D:(k + 1) * D]) * y.astype(F32)
        return (out,), ()
    return rowwise(fn, _gl_rows(P) + [ya, yb, yc], [b_gate], [(D, BF)], ts=512, name=tag + "_merge")[0][0]


def merge_bwd(dm, P, ya, yb, yc, b_gate, tag):
    def fn(step, i, n, dm, g0, g1, g2, ya, yb, yc, b):
        dm = dm.astype(F32)
        dys, dgls = [], []
        for k, (gl, y) in enumerate(((g0, ya), (g1, yb), (g2, yc))):
            g = _sigmoid(gl.astype(F32) + b[:, k * D:(k + 1) * D])
            dys.append(dm * g)
            dgls.append(dm * y.astype(F32) * (g * (1.0 - g)))
        dgl = jnp.concatenate(dgls, axis=1)
        return (dys[0], dys[1], dys[2], dgl), (_colsum(dgl),)
    (dya, dyb, dyc, dgl), (db,) = rowwise(
        fn, [dm] + _gl_rows(P) + [ya, yb, yc], [b_gate], [(D, BF), (D, BF), (D, BF), (3 * D, BF)], [((1, 3 * D), F32)],
        ts=256, name=tag + "_dmerge")
    return dya, dyb, dyc, dgl, db


def _xattn_probs(qh, kh):
    s = _dot_nt(qh, kh) * (X_HD ** -0.5)
    e = jnp.exp(s - jnp.max(s, axis=1, keepdims=True))
    return e / jnp.sum(e, axis=1, keepdims=True)


def xattn_fwd(q, kv, tag):
    def fn(step, i, n, q, k, v):
        os = []
        for h in range(X_HEADS):
            sl = slice(h * X_HD, (h + 1) * X_HD)
            os.append(_dot_nn(_xattn_probs(q[:, sl], k[:, sl]), v[:, sl]))
        return (jnp.concatenate(os, axis=1),), ()

    return _xattn_call(fn, [q], kv, [(D, BF)], [], tag + "_xattn")[0][0]


def _xattn_call(fn, rows, kv, outs, accs, name):
    S, ts, M = rows[0].shape[0], 512, kv.shape[0]
    ts = min(ts, S)
    nr, no, na = len(rows), len(outs), len(accs)

    def body(*refs):
        in_refs, out_refs, acc_refs = refs[:nr + 2], refs[nr + 2:nr + 2 + no], refs[nr + 2 + no:]
        step = pl.program_id(0)
        o, inc = fn(step, step, S // ts, *[r[...] for r in in_refs])
        for ref, val in zip(out_refs, o):
            ref[...] = val.astype(ref.dtype)
        if na:
            @pl.when(step == 0)
            def _():
                for ref in acc_refs:
                    ref[...] = jnp.zeros_like(ref)
            for ref, val in zip(acc_refs, inc):
                ref[...] += val

    row = pl.BlockSpec((ts, D), lambda s: (s, 0))
    res = pl.pallas_call(
        body, name=name, grid=(S // ts,),
        in_specs=[row] * nr + [pl.BlockSpec((M, D), lambda s: (0, 0)), pl.BlockSpec((M, D), lambda s: (0, 1))],
        out_specs=[row] * no + [pl.BlockSpec(tuple(s), lambda s_: (0, 0)) for s, _ in accs],
        out_shape=[jax.ShapeDtypeStruct((S, w), dt) for w, dt in outs] + [jax.ShapeDtypeStruct(tuple(s), dt) for s, dt in accs],
        compiler_params=_cparams(1))(*rows, kv, kv)
    return res[:no], res[no:]


def xattn_bwd(q, kv, do, tag):
    def fn(step, i, n, q, do, k, v):
        dqs, dks, dvs = [], [], []
        for h in range(X_HEADS):
            sl = slice(h * X_HD, (h + 1) * X_HD)
            p = _xattn_probs(q[:, sl], k[:, sl])
            dp = _dot_nt(do[:, sl], v[:, sl])
            ds = (p * (dp - jnp.sum(p * dp, axis=1, keepdims=True)) * (X_HD ** -0.5)).astype(BF)
            dqs.append(_dot_nn(ds, k[:, sl]))
            dks.append(_dot_tn(ds, q[:, sl]))
            dvs.append(_dot_tn(p, do[:, sl]))
        return (jnp.concatenate(dqs, axis=1),), (jnp.concatenate(dks + dvs, axis=1),)

    (dq,), (dkv,) = _xattn_call(fn, [q, do], kv, [(D, BF)], [((kv.shape[0], 2 * D), F32)], tag + "_dxattn")
    return dq, dkv


def layer_fwd(x, h_in, mem, w, next_g, tag):
    x1, u, s_ffn1 = ffn_fwd(x, h_in, w["ffn1_in"], w["ffn1_out"], w["g_mix"], tag + "_ffn1")
    P = matmul(u, w["wp"], "nn", name=tag + "_proj")
    fl = matmul(u, w["wfl"], "nn", out_dtype=F32, name=tag + "_projf")
    d, ya0 = pool_fwd(P, w["w_pool"], w["pool_scale"], tag)
    ya = matmul(ya0, w["up_a"], "nn", name=tag + "_upa")
    xc, hh, yb0 = lru_fwd(P, w["conv_w"], w["conv_b"], w["w_rg_a"], w["b_rg_a"], w["w_rg_x"], w["b_rg_x"], w["lam"], tag)
    yb = matmul(yb0, w["up_b"], "nn", name=tag + "_upb")
    c = cumlogf_fwd(fl, w["b_f"], tag)
    q, k, v = (_to_heads(P[:, o:o + FOX_W]) for o in (PK_Q, PK_K, PK_V))
    qa, ka, qs = fox_operands(q, k, c[:, :FOX_HEADS].T)
    ot, lse = flash_fwd(qa, ka, v.transpose(0, 2, 1), tag)
    o2 = ot.transpose(2, 0, 1).reshape(-1, FOX_W)
    yc = matmul(o2, w["up_c"], "nn", name=tag + "_upc")
    merged = merge_fwd(P, ya, yb, yc, w["b_gate"], tag)
    x2, hq = matmul(merged, w["w_o"], "nn", out_dtype=F32, res=x1, norm_g=w["g_cross"], name=tag + "_wo")
    qx = matmul(hq, w["xq"], "nn", name=tag + "_xq")
    mn = rms_fwd(mem, w["g_mem"], name=tag + "_mrms")
    kv = matmul(mn, w["xkv"], "nn", name=tag + "_xkv")
    ox = xattn_fwd(qx, kv, tag)
    x3, h2 = matmul(ox, w["xo"], "nn", out_dtype=F32, res=x2, norm_g=w["g_ffn2"], name=tag + "_xo")
    x4, h_out, s_ffn2 = ffn_fwd(x3, h2, w["ffn2_in"], w["ffn2_out"], next_g, tag + "_ffn2")
    saved = dict(ffn1=s_ffn1, ffn2=s_ffn2, x1=x1, u=u, P=P, fl=fl, d=d, ya0=ya0, ya=ya, xc=xc, hh=hh, yb0=yb0, yb=yb,
                 qa=qa, ka=ka, qs=qs, k=k, v=v, ot=ot, lse=lse, o2=o2, yc=yc, merged=merged, x2=x2, hq=hq, qx=qx,
                 mn=mn, kv=kv, ox=ox)
    return x4, h_out, saved


def layer_bwd(dx4, mem, w, s, tag):
    g = {}
    dx3, g["g_ffn2"], g["w_ffn2_in"], g["w_ffn2_out"] = ffn_bwd(dx4, s["ffn2"], w["g_ffn2"], w["ffn2_in"], w["ffn2_out"], tag + "_ffn2")
    dox = matmul(dx3, w["xo"], "nt", name=tag + "_dox")
    g["w_xo"] = matmul(s["ox"], dx3, "tn", name=tag + "_dwxo")
    dqx, dkv = xattn_bwd(s["qx"], s["kv"], dox, tag)
    g["w_xq"] = matmul(s["hq"], dqx, "tn", name=tag + "_dwxq")
    dx2, g["g_cross"] = matmul_rms_bwd(dqx, w["xq"], s["x2"], w["g_cross"], dx3, name=tag + "_dhq")
    g["w_xkv"] = matmul(s["mn"], dkv, "tn", name=tag + "_dwxkv")
    dmn = matmul(dkv, w["xkv"], "nt", name=tag + "_dmn")
    _, g["g_mem"] = rms_bwd(mem, w["g_mem"], dmn, None, name=tag + "_dmrms")
    P = s["P"]
    dmerged = matmul(dx2, w["w_o"], "nt", name=tag + "_dmerged")
    g["w_o"] = matmul(s["merged"], dx2, "tn", name=tag + "_dwo")
    dya, dyb, dyc, dgl, g["b_gate"] = merge_bwd(dmerged, P, s["ya"], s["yb"], s["yc"], w["b_gate"], tag)
    dya0 = matmul(dya, w["up_a"], "nt", name=tag + "_dya0")
    g["w_up_a"] = matmul(s["ya0"], dya, "tn", name=tag + "_dwupa")
    dxa, g["w_pool"], g["pool_scale"] = pool_bwd(dya0, s["d"], w["w_pool"], w["pool_scale"], tag)
    dyb0 = matmul(dyb, w["up_b"], "nt", name=tag + "_dyb0")
    g["w_up_b"] = matmul(s["yb0"], dyb, "tn", name=tag + "_dwupb")
    (dgb, dxb, g["w_rg_a"], g["b_rg_a"], g["w_rg_x"], g["b_rg_x"], g["lru_lambda"], g["conv_w"], g["conv_b"]) = lru_bwd(
        dyb0, P, s["hh"], s["xc"], w["conv_w"], w["w_rg_a"], w["b_rg_a"], w["w_rg_x"], w["b_rg_x"], w["lam"], tag)
    do2 = matmul(dyc, w["up_c"], "nt", name=tag + "_do2")
    g["w_up_c"] = matmul(s["o2"], dyc, "tn", name=tag + "_dwupc")
    do = _to_heads(do2)
    dqt, dk, dv, dck = flash_bwd(s["qa"], s["ka"], s["qs"], s["k"].transpose(0, 2, 1), s["v"], s["ot"], do,
                                 do.transpose(0, 2, 1), s["lse"], tag)
    dq = (dqt * (FOX_HD ** -0.5)).transpose(0, 1, 3, 2).reshape(dk.shape)
    dc = jnp.pad(dck[:, 0, :].T, ((0, 0), (0, LANES - FOX_HEADS)))
    dfl, g["b_f"] = cumlogf_bwd(dc, s["fl"], w["b_f"], tag)
    dP = jnp.concatenate([dxb, dgb, dgl, dxa] + [_from_heads(t).astype(BF) for t in (dq, dk, dv)], axis=1)
    du = matmul(dP, w["wp"], "nt", out_dtype=F32, name=tag + "_du")
    dx1, g["g_mix"] = matmul_rms_bwd(dfl, w["wfl"], s["x1"], w["g_mix"], dx2, res=du, name=tag + "_duf")
    g["wp"] = matmul(s["u"], dP, "tn", name=tag + "_dwp")
    g["wfl"] = matmul(s["u"], dfl, "tn", out_dtype=F32, name=tag + "_dwfl")
    dx0, g["g_ffn1"], g["w_ffn1_in"], g["w_ffn1_out"] = ffn_bwd(dx1, s["ffn1"], w["g_ffn1"], w["ffn1_in"], w["ffn1_out"], tag + "_ffn1")
    return dx0, g


def loss_head(x, target, g_final):
    def fn(step, i, n, x, t, g):
        r = _rstd(x)
        xhat = x * r
        e = xhat * g - t
        dy = e * (1.0 / D)
        gd = dy * g
        dx = r * (gd - xhat * jnp.mean(xhat * gd, axis=-1, keepdims=True))
        loss = 0.5 * jnp.sum(jnp.mean(e * e, axis=-1, keepdims=True), axis=0, keepdims=True)
        return (dx,), (jnp.broadcast_to(loss, (1, LANES)), _colsum(dy * xhat))

    (dx,), (loss, dg) = rowwise(fn, [x, target], [g_final], [(D, F32)], [((1, LANES), F32), ((1, D), F32)], ts=512, name="loss_head")
    return loss[0, 0], dx, dg


BIG = (("w_ffn1_in", (D, 2 * DFF), 1), ("w_ffn1_out", (DFF, D), 0), ("w_in", (D, 7176), 1), ("w_up_a", (POOL_W, D), 1),
       ("conv_w", (4, D), 1), ("w_up_b", (D, D), 0), ("w_up_c", (FOX_W, D), 1), ("w_o", (D, D), 0), ("w_xq", (D, D), 0),
       ("w_xkv", (D, 2 * D), 1), ("w_xo", (D, D), 0), ("w_ffn2_in", (D, 2 * DFF), 1), ("w_ffn2_out", (DFF, D), 0))
SMALL = (("g_ffn1", (D,)), ("g_mix", (D,)), ("b_f", (FOX_HEADS,)), ("b_gate", (3 * D,)), ("w_pool", (4, LANES, LANES)),
         ("pool_scale", (POOL_W,)), ("conv_b", (D,)), ("w_rg_a", (LRU_HEADS, LANES, LANES)), ("b_rg_a", (D,)),
         ("w_rg_x", (LRU_HEADS, LANES, LANES)), ("b_rg_x", (D,)), ("lru_lambda", (D,)), ("g_cross", (D,)), ("g_mem", (D,)),
         ("g_ffn2", (D,)))
ORDER = ("g_ffn1", "w_ffn1_in", "w_ffn1_out", "g_mix", "w_in", "b_f", "b_gate", "w_pool", "pool_scale", "w_up_a", "conv_w",
         "conv_b", "w_rg_a", "b_rg_a", "w_rg_x", "b_rg_x", "lru_lambda", "w_up_b", "w_up_c", "w_o", "g_cross", "g_mem", "w_xq",
         "w_xkv", "w_xo", "g_ffn2", "w_ffn2_in", "w_ffn2_out", "g_final")

IN_SPLIT = (("xa", 0, 512), ("xb", 512, 1024), ("gb", 1536, 1024), ("q", 2560, 512), ("k", 3072, 512), ("v", 3584, 512),
            ("fl", 4096, 8), ("gl", 4104, 3072))
PACK_ORDER = ("xb", "gb", "gl", "xa", "q", "k", "v")


def _shard_shape(shape, axis):
    s = list(shape)
    s[axis] //= N_CHIPS
    return (DEPTH, *s)


def _round_up(n, m):
    return -(-n // m) * m


SMALL_SEG_ROWS = 8


def _small_rows(n_elements):
    return _round_up(-(-n_elements // D), SMALL_SEG_ROWS)


def _pack_small(tree, names, row_multiple=SMALL_SEG_ROWS, fill=0.0):
    parts = []
    for n in names:
        flat = tree[n].astype(F32).reshape(-1)
        rows = _small_rows(flat.shape[0])
        parts.append(jnp.pad(flat, (0, rows * D - flat.shape[0]), constant_values=fill).reshape(rows, D))
    total = sum(p.shape[0] for p in parts)
    if total % row_multiple:
        parts.append(jnp.full((_round_up(total, row_multiple) - total, D), fill, F32))
    return jnp.concatenate(parts, axis=0)


def _unpack_small(buf, like, names):
    out, off = {}, 0
    for n in names:
        sh = like[n].shape
        rows = _small_rows(math.prod(sh))
        out[n] = buf[off:off + rows].reshape(-1)[:math.prod(sh)].reshape(sh)
        off += rows
    return out


def _from_chip_major(g, shape, axis):
    return jnp.moveaxis(g, 1, axis + 1).reshape(DEPTH, *shape)


def _to_chip_major(full, shape, axis):
    sh = list(shape)
    sh[axis:axis + 1] = [N_CHIPS, shape[axis] // N_CHIPS]
    return jnp.moveaxis(full.reshape(DEPTH, *sh), axis + 1, 1)


def _layer_weights(full, small, l):
    w_in = full["w_in"][l]
    cols = {n: w_in[:, o:o + s] for n, o, s in IN_SPLIT}
    w = dict(ffn1_in=full["w_ffn1_in"][l], ffn1_out=full["w_ffn1_out"][l], ffn2_in=full["w_ffn2_in"][l], ffn2_out=full["w_ffn2_out"][l],
             wp=jnp.concatenate([cols[n] for n in PACK_ORDER], axis=1),
             wfl=jnp.pad(cols["fl"], ((0, 0), (0, LANES - FOX_HEADS))),
             up_a=full["w_up_a"][l], up_b=full["w_up_b"][l], up_c=full["w_up_c"][l], w_o=full["w_o"][l], xq=full["w_xq"][l],
             xkv=full["w_xkv"][l], xo=full["w_xo"][l], conv_w=full["conv_w"][l])
    for n in ("g_ffn1", "g_mix", "b_gate", "pool_scale", "conv_b", "b_rg_a", "b_rg_x", "g_cross", "g_mem", "g_ffn2"):
        w[n] = small[n][l].reshape(1, -1)
    w["lam"] = small["lru_lambda"][l].reshape(1, -1)
    w["b_f"] = jnp.pad(small["b_f"][l], (0, LANES - FOX_HEADS)).reshape(1, LANES)
    for n in ("w_pool", "w_rg_a", "w_rg_x"):
        w[n] = small[n][l]
    return w


def _unpack_w_in_grad(gwp, gwfl):
    pk = {}
    off = 0
    sizes = {n: s for n, _, s in IN_SPLIT}
    for n in PACK_ORDER:
        pk[n] = gwp[:, off:off + sizes[n]]
        off += sizes[n]
    pk["fl"] = gwfl[:, :FOX_HEADS].astype(gwp.dtype)
    return jnp.concatenate([pk[n] for n, _, _ in IN_SPLIT], axis=1)


def _place():
    x, y, c = lax.axis_index("x"), lax.axis_index("y"), lax.axis_index("c")
    others = [(1 - x, y), (x, 1 - y), (1 - x, 1 - y)]
    return x, y, c, others


def _rcopy(src, dst, send_sems, recv_sems, k, to):
    return pltpu.make_async_remote_copy(src_ref=src, dst_ref=dst, send_sem=send_sems.at[k], recv_sem=recv_sems.at[k],
                                        device_id=to, device_id_type=MESH)


def _comm_call(body, name, ins, out_shape, n_sems):
    return pl.pallas_call(
        body, name=name, in_specs=[ANY] * len(ins), out_specs=[ANY] * len(out_shape), out_shape=out_shape,
        scratch_shapes=[pltpu.SemaphoreType.DMA((n_sems,)), pltpu.SemaphoreType.DMA((n_sems,)), pltpu.SemaphoreType.DMA((n_sems,))],
    )(*ins)


def allgather_same_core(bufs):
    nb = len(bufs)

    def body(*refs):
        ins, outs = refs[:nb], refs[nb:2 * nb]
        send_sems, recv_sems, local_sems = refs[2 * nb:]
        x, y, c, others = _place()
        me = 2 * x + y
        sends = []
        for b in range(nb):
            for j, (ox, oy) in enumerate(others):
                cp = _rcopy(ins[b], outs[b].at[me], send_sems, recv_sems, 3 * b + j, (ox, oy, c))
                cp.start()
                sends.append(cp)
        for b in range(nb):
            for j, (ox, oy) in enumerate(others):
                theirs = outs[b].at[2 * ox + oy]
                _rcopy(theirs, theirs, send_sems, recv_sems, 3 * b + j, (ox, oy, c)).wait_recv()
        for cp in sends:
            cp.wait_send()

    shapes = [jax.ShapeDtypeStruct((N_CHIPS, *b.shape), b.dtype) for b in bufs]
    outs = _comm_call(body, "allgather_same_core", list(bufs), shapes, 3 * nb)
    return [_with_own_block(o, b) for o, b in zip(outs, bufs)]


def _with_own_block(gathered, mine):
    me = 2 * lax.axis_index("x") + lax.axis_index("y")
    return lax.dynamic_update_index_in_dim(gathered, mine, me, axis=0)


def gather_weights(shards):
    nw = len(shards)

    def body(*refs):
        ins, outs = refs[:nw], refs[nw:2 * nw]
        send_sems, recv_sems, _ = refs[2 * nw:]
        x, y, c, _ = _place()
        fx, fy = 1 - c, c
        first = (x + fx - 2 * x * fx, y + fy - 2 * y * fy)
        second = (x + fy - 2 * x * fy, y + fx - 2 * y * fx)
        me, c1, c2, cd = 2 * x + y, 2 * first[0] + first[1], 2 * second[0] + second[1], 2 * (1 - x) + (1 - y)
        sibling = (x, y, 1 - c)
        started = []

        def send(w, k, src, dst, to):
            cp = _rcopy(src, dst, send_sems, recv_sems, 6 * w + k, to)
            cp.start()
            started.append(cp)

        def landed(w, k, block):
            _rcopy(block, block, send_sems, recv_sems, 6 * w + k, sibling).wait_recv()

        for w in range(nw):
            send(w, 0, ins[w].at[c], outs[w].at[c, me], (*first, c))
            send(w, 1, ins[w].at[c], outs[w].at[c, me], (*second, c))
        for w in range(nw):
            block = outs[w].at[c, c1]
            landed(w, 0, block)
            send(w, 2, block, block, (*second, c))
            send(w, 3, block, block, sibling)
        for w in range(nw):
            block = outs[w].at[c, c2]
            landed(w, 1, block)
            send(w, 4, block, block, sibling)
        for w in range(nw):
            block = outs[w].at[c, cd]
            landed(w, 2, block)
            send(w, 5, block, block, sibling)
        for w in range(nw):
            landed(w, 3, outs[w].at[1 - c, c2])
            landed(w, 4, outs[w].at[1 - c, c1])
            landed(w, 5, outs[w].at[1 - c, cd])
        for cp in started:
            cp.wait_send()

    shapes = [jax.ShapeDtypeStruct((DEPTH, N_CHIPS, *s.shape[1:]), s.dtype) for s in shards]
    outs = _comm_call(body, "gather_weights", list(shards), shapes, 6 * nw)
    me = 2 * lax.axis_index("x") + lax.axis_index("y")
    return [lax.dynamic_update_index_in_dim(o, s, me, axis=1) for o, s in zip(outs, shards)]


def sibling_layers(bufs):
    nb = len(bufs)

    def body(*refs):
        ins, outs = refs[:nb], refs[nb:2 * nb]
        send_sems, recv_sems, _ = refs[2 * nb:]
        x, y, c, _ = _place()
        work = []
        for b in range(nb):
            cp = _rcopy(ins[b].at[1 - c], outs[b], send_sems, recv_sems, b, (x, y, 1 - c))
            cp.start()
            work.append(cp)
        for cp in work:
            cp.wait()

    shapes = [jax.ShapeDtypeStruct(b.shape[1:], b.dtype) for b in bufs]
    return _comm_call(body, "sibling_layers", list(bufs), shapes, nb)


def chip_blocks(bufs):
    nb = len(bufs)

    def body(*refs):
        ins, outs = refs[:nb], refs[nb:2 * nb]
        send_sems, recv_sems, _ = refs[2 * nb:]
        x, y, c, others = _place()
        sends = []
        for b in range(nb):
            for j, (ox, oy) in enumerate(others):
                cp = _rcopy(ins[b].at[2 * ox + oy], outs[b].at[j], send_sems, recv_sems, 3 * b + j, (ox, oy, c))
                cp.start()
                sends.append(cp)
        for cp in sends:
            cp.wait()

    shapes = [jax.ShapeDtypeStruct((3, *b.shape[1:]), b.dtype) for b in bufs]
    return _comm_call(body, "chip_blocks", list(bufs), shapes, 3 * nb)


def sibling_other_layer(bufs):
    nb = len(bufs)

    def body(*refs):
        ins, outs = refs[:nb], refs[nb:2 * nb]
        send_sems, recv_sems, _ = refs[2 * nb:]
        x, y, c, _ = _place()
        work = []
        for b in range(nb):
            cp = _rcopy(ins[b], outs[b].at[c], send_sems, recv_sems, b, (x, y, 1 - c))
            cp.start()
            work.append(cp)
        for b, cp in enumerate(work):
            cp.wait_send()
            theirs = outs[b].at[1 - c]
            _rcopy(theirs, theirs, send_sems, recv_sems, b, (x, y, 1 - c)).wait_recv()

    shapes = [jax.ShapeDtypeStruct((DEPTH, *b.shape), b.dtype) for b in bufs]
    outs = _comm_call(body, "sibling_other_layer", list(bufs), shapes, nb)
    c = lax.axis_index("c")
    return [lax.dynamic_update_index_in_dim(o, b, c, axis=0) for o, b in zip(outs, bufs)]


ADD_BLOCK_BYTES = 2 * 2 ** 20


def add_arrays(parts, out_dtype, name):
    shape = parts[0].shape
    cols = shape[-1]
    rows = math.prod(shape[:-1])

    def fn(step, i, n, *vals):
        acc = vals[0].astype(F32)
        for v in vals[1:]:
            acc = acc + v.astype(F32)
        return (acc,), ()

    ts = rows
    for cand in (2048, 1024, 512, 256, 128, 64, 32, 16):
        if rows % cand == 0 and cand * cols * 4 <= ADD_BLOCK_BYTES:
            ts = cand
            break
    out = rowwise(fn, [p.reshape(rows, cols) for p in parts], outs=[(cols, out_dtype)], ts=ts, name=name)[0][0]
    return out.reshape(shape)


def reduce_gradients(bufs, names):
    c, me = lax.axis_index("c"), 2 * lax.axis_index("x") + lax.axis_index("y")
    theirs = sibling_layers(bufs)
    mine = [lax.dynamic_index_in_dim(b, c, axis=0, keepdims=False) for b in bufs]
    part = [add_arrays([m, t], m.dtype, "rs_add_sibling_" + n) for m, t, n in zip(mine, theirs, names)]
    recv = chip_blocks(part)
    own = [lax.dynamic_index_in_dim(p, me, axis=0, keepdims=False) for p in part]
    red = [add_arrays([o, r[0], r[1], r[2]], F32, "rs_add_chips_" + n) for o, r, n in zip(own, recv, names)]
    return sibling_other_layer(red)


_BC1 = 1.0 - ADAM_B1 ** ADAM_STEP
_BC2 = 1.0 - ADAM_B2 ** ADAM_STEP


def adamw(w, g, m, v, name):
    rows, cols = w.shape

    def fn(step, i, n, w, g, m, v):
        m2 = ADAM_B1 * m + (1.0 - ADAM_B1) * g
        v2 = ADAM_B2 * v + (1.0 - ADAM_B2) * (g * g)
        delta = -ADAM_LR * ((m2 / _BC1) / (jnp.sqrt(v2 / _BC2) + ADAM_EPS) + ADAM_WD * w)
        return (delta, m2, v2), ()

    ts = _pick(rows, (256, 128, 64, 32, 16, 8))
    return rowwise(fn, [w, g, m, v], outs=[(cols, F32)] * 3, ts=ts, name=name)[0]


def local_step(x, mem, target, full, small):
    ws = [_layer_weights(full, small, l) for l in range(DEPTH)]
    saved = []
    h, hn = x, rms_fwd(x, ws[0]["g_ffn1"], name="l_ffn1_rms")
    for l in range(DEPTH):
        h, hn, s = layer_fwd(h, hn, mem, ws[l], ws[l + 1]["g_ffn1"] if l + 1 < DEPTH else None, "l")
        saved.append(s)
    loss, dh, dg_final = loss_head(h, target, small["g_final"].reshape(1, D))
    grads = [None] * DEPTH
    for l in reversed(range(DEPTH)):
        dh, grads[l] = layer_bwd(dh, mem, ws[l], saved[l], "l")
    out = {}
    for l in range(DEPTH):
        g = grads[l]
        g["w_in"] = _unpack_w_in_grad(g.pop("wp"), g.pop("wfl"))
        g["b_f"] = g["b_f"][:, :FOX_HEADS]
    for name, shape, _ in BIG:
        out[name] = jnp.stack([grads[l][name].reshape(shape) for l in range(DEPTH)])
    for name, shape in SMALL:
        out[name] = jnp.stack([grads[l][name].reshape(shape) for l in range(DEPTH)])
    out["g_final"] = dg_final.reshape(D)
    return loss, dh, out


def kernel(x, mem, g_ffn1, w_ffn1_in, w_ffn1_out, g_mix, w_in, b_f, b_gate, w_pool, pool_scale, w_up_a, conv_w, conv_b, w_rg_a, b_rg_a, w_rg_x, b_rg_x, lru_lambda, w_up_b, w_up_c, w_o, g_cross, g_mem, w_xq, w_xkv, w_xo, g_ffn2, w_ffn2_in, w_ffn2_out, g_final, loss_target, m_g_ffn1, m_w_ffn1_in, m_w_ffn1_out, m_g_mix, m_w_in, m_b_f, m_b_gate, m_w_pool, m_pool_scale, m_w_up_a, m_conv_w, m_conv_b, m_w_rg_a, m_b_rg_a, m_w_rg_x, m_b_rg_x, m_lru_lambda, m_w_up_b, m_w_up_c, m_w_o, m_g_cross, m_g_mem, m_w_xq, m_w_xkv, m_w_xo, m_g_ffn2, m_w_ffn2_in, m_w_ffn2_out, m_g_final, v_g_ffn1, v_w_ffn1_in, v_w_ffn1_out, v_g_mix, v_w_in, v_b_f, v_b_gate, v_w_pool, v_pool_scale, v_w_up_a, v_conv_w, v_conv_b, v_w_rg_a, v_b_rg_a, v_w_rg_x, v_b_rg_x, v_lru_lambda, v_w_up_b, v_w_up_c, v_w_o, v_g_cross, v_g_mem, v_w_xq, v_w_xkv, v_w_xo, v_g_ffn2, v_w_ffn2_in, v_w_ffn2_out, v_g_final):
    args = dict(locals())
    weights = {n: args[n] for n in ORDER}
    m_in = {n: args["m_" + n] for n in ORDER}
    v_in = {n: args["v_" + n] for n in ORDER}
    big_names = [n for n, _, _ in BIG]
    small_names = [n for n, _ in SMALL] + ["g_final"]

    shards = []
    for n in big_names:
        shards += list(_split3(weights[n])) if n == "conv_w" else [weights[n].astype(BF)]
    gathered = iter(gather_weights(shards))
    full = {}
    for n, shape, axis in BIG:
        g = next(gathered)
        if n == "conv_w":
            g = g.astype(F32) + next(gathered).astype(F32) + next(gathered).astype(F32)
        full[n] = _from_chip_major(g, shape, axis)

    small = {n: weights[n] for n in small_names}
    loss, grad_x, contrib = local_step(x[0], mem[0], loss_target[0], full, small)
    loss = lax.psum(loss, ("x", "y", "c"))

    bufs = [_to_chip_major(contrib[n].astype(BF), shape, axis) for n, shape, axis in BIG]
    small_buf = _pack_small(contrib, small_names, DEPTH * N_CHIPS * SMALL_SEG_ROWS).reshape(DEPTH, N_CHIPS, -1, D)
    red = reduce_gradients(bufs + [small_buf], big_names + ["small"])

    grads = dict(zip(big_names, red[:-1]))
    red_small = allgather_same_core([red[-1]])[0].transpose(1, 0, 2, 3).reshape(-1, D)
    grads.update(_unpack_small(red_small, weights, small_names))

    delta, new_m, new_v = {}, {}, {}
    for name in big_names:
        sh = weights[name].shape
        two_d = (-1, sh[-1])
        d, m2, v2 = adamw(weights[name].reshape(two_d), grads[name].reshape(two_d), m_in[name].reshape(two_d),
                          v_in[name].reshape(two_d), "adamw_" + name)
        delta[name], new_m[name], new_v[name] = d.reshape(sh), m2.reshape(sh), v2.reshape(sh)
    outs = adamw(_pack_small(weights, small_names), _pack_small(grads, small_names), _pack_small(m_in, small_names),
                 _pack_small(v_in, small_names, fill=1.0), "adamw_small")
    for t, res in zip((delta, new_m, new_v), outs):
        t.update(_unpack_small(res, weights, small_names))

    return (loss, grad_x[None], *[grads[n] for n in ORDER], *[delta[n] for n in ORDER],
            *[new_m[n] for n in ORDER], *[new_v[n] for n in ORDER])
```

```python
import math

import jax
import jax.numpy as jnp
from jax import lax
from jax.experimental import pallas as pl
from jax.experimental.pallas import tpu as pltpu

F32 = jnp.float32
BF = jnp.bfloat16

D = 1024
DFF = 2816
DEPTH = 2
POOL_W = 512
POOL_WINDOWS = (2, 4, 8, 16)
LRU_HEADS = 8
LRU_C = 8.0
FOX_HEADS = 8
FOX_HD = 64
FOX_W = 512
X_HEADS = 4
X_HD = 256
EPS = 1e-6
LANES = 128
N_CHIPS = 4

ADAM_LR, ADAM_B1, ADAM_B2, ADAM_EPS, ADAM_WD, ADAM_STEP = 0.001, 0.9, 0.999, 1e-08, 0.01, 10

VMEM_LIMIT_BYTES = 56 * 2 ** 20

PK_XB, PK_GB, PK_GL, PK_XA, PK_Q, PK_K, PK_V = 0, 1024, 2048, 5120, 5632, 6144, 6656
PK_W = 7168

MESH = pl.DeviceIdType.MESH
ANY = pl.BlockSpec(memory_space=pl.ANY)


def _cparams(ngrid):
    return pltpu.CompilerParams(dimension_semantics=("arbitrary",) * ngrid, vmem_limit_bytes=VMEM_LIMIT_BYTES)


def _pick(n, cands):
    for c in cands:
        if n % c == 0:
            return c
    return n


def _iota(shape, dim):
    return lax.broadcasted_iota(jnp.int32, shape, dim)


def _sigmoid(x):
    return 1.0 / (1.0 + jnp.exp(-x))


def _softplus(z):
    return jnp.maximum(z, 0.0) + jnp.log1p(jnp.exp(-jnp.abs(z)))


def _expm1(x):
    small = jnp.abs(x) < 0.25
    xs = jnp.where(small, x, 0.0)
    poly = xs * (1.0 + xs * (1 / 2 + xs * (1 / 6 + xs * (1 / 24 + xs * (1 / 120 + xs * (1 / 720 + xs * (1 / 5040 + xs * (1 / 40320))))))))
    return jnp.where(small, poly, jnp.exp(x) - 1.0)


_GELU_K = math.sqrt(2.0 / math.pi)


def _gelu_and_grad(x):
    inner = _GELU_K * (x + 0.044715 * x * x * x)
    t = jnp.tanh(inner)
    g = 0.5 * x * (1.0 + t)
    dg = 0.5 * (1.0 + t) + 0.5 * x * (1.0 - t * t) * _GELU_K * (1.0 + 3 * 0.044715 * x * x)
    return g, dg


def _dot(a, b, dims):
    return lax.dot_general(a.astype(BF), b.astype(BF), (dims, ((), ())), preferred_element_type=F32)


def _dot_nn(a, b):
    return _dot(a, b, ((1,), (0,)))


def _dot_nt(a, b):
    return _dot(a, b, ((1,), (1,)))


def _dot_tn(a, b):
    return _dot(a, b, ((0,), (0,)))


def _colsum(x):
    return jnp.sum(x, axis=0, keepdims=True)


_TM = (1024, 1408, 512, 256, 128)
_TN = (1408, 1024, 512, 384, 256, 128)
_TK = (2432, 1408, 1024, 512, 256, 128)


def matmul(a, b, mode, *, out_dtype=None, alpha=1.0, res=None, norm_g=None, name):
    out_dtype = BF if out_dtype is None else out_dtype
    if mode == "nn":
        (M, K), N = a.shape, b.shape[1]
    elif mode == "nt":
        (M, K), N = a.shape, b.shape[0]
    else:
        (K, M), N = a.shape, b.shape[1]
    tm, tn, tk = _pick(M, _TM), _pick(N, _TN), _pick(K, _TK)
    nk = K // tk
    dims = {"nn": ((1,), (0,)), "nt": ((1,), (1,)), "tn": ((0,), (0,))}[mode]
    a_spec = pl.BlockSpec((tk, tm), lambda i, j, k: (k, i)) if mode == "tn" else pl.BlockSpec((tm, tk), lambda i, j, k: (i, k))
    b_spec = pl.BlockSpec((tn, tk), lambda i, j, k: (j, k)) if mode == "nt" else pl.BlockSpec((tk, tn), lambda i, j, k: (k, j))
    o_spec = pl.BlockSpec((tm, tn), lambda i, j, k: (i, j))
    has_res, has_norm = res is not None, norm_g is not None
    assert not has_norm or tn == N, "the norm epilogue needs whole rows"

    def body(*refs):
        a_ref, b_ref = refs[0], refs[1]
        r_ref = refs[2] if has_res else None
        g_ref = refs[2 + has_res] if has_norm else None
        o_ref = refs[2 + has_res + has_norm]
        h_ref = refs[3 + has_res + has_norm] if has_norm else None
        acc = refs[-1]
        k = pl.program_id(2)

        @pl.when(k == 0)
        def _():
            acc[...] = jnp.zeros_like(acc)

        acc[...] += _dot(a_ref[...], b_ref[...], dims)

        @pl.when(k == nk - 1)
        def _():
            r = acc[...]
            if alpha != 1.0:
                r = r * alpha
            if has_res:
                r = r + r_ref[...].astype(F32)
            r = r.astype(o_ref.dtype)
            o_ref[...] = r
            if has_norm:
                rf = r.astype(F32)
                h_ref[...] = ((rf * _rstd(rf)) * g_ref[...]).astype(h_ref.dtype)

    ins = [a, b] + ([res] if has_res else []) + ([norm_g] if has_norm else [])
    specs = [a_spec, b_spec] + ([o_spec] if has_res else []) + ([pl.BlockSpec((1, N), lambda i, j, k: (0, 0))] if has_norm else [])
    out = pl.pallas_call(
        body, name=name, grid=(M // tm, N // tn, nk), in_specs=specs,
        out_specs=[o_spec, o_spec] if has_norm else o_spec,
        out_shape=([jax.ShapeDtypeStruct((M, N), out_dtype), jax.ShapeDtypeStruct((M, N), BF)] if has_norm
                   else jax.ShapeDtypeStruct((M, N), out_dtype)),
        scratch_shapes=[pltpu.VMEM((tm, tn), F32)], compiler_params=_cparams(3))(*ins)
    return tuple(out) if has_norm else out


def matmul_rms_bwd(a, b, x, g, dres, *, a2=None, res=None, name):
    (M, K), N = a.shape, b.shape[0]
    tm, tk = _pick(M, _TM), _pick(K, _TK)
    nk1 = K // tk
    nk = nk1 * (2 if a2 is not None else 1)
    has_a2, has_res, has_dres = a2 is not None, res is not None, dres is not None

    def body(*refs):
        a_ref, b_ref, x_ref, g_ref = refs[:4]
        rest = list(refs[4:])
        a2_ref = rest.pop(0) if has_a2 else None
        r_ref = rest.pop(0) if has_res else None
        d_ref = rest.pop(0) if has_dres else None
        dx_ref, dg_ref, acc = rest
        i, k = pl.program_id(0), pl.program_id(1)

        @pl.when(k == 0)
        def _():
            acc[...] = jnp.zeros_like(acc)

        @pl.when(jnp.logical_and(i == 0, k == 0))
        def _():
            dg_ref[...] = jnp.zeros_like(dg_ref)

        if has_a2:
            @pl.when(k < nk1)
            def _():
                acc[...] += _dot_nt(a_ref[...], b_ref[...])

            @pl.when(k >= nk1)
            def _():
                acc[...] += _dot_nt(a2_ref[...], b_ref[...])
        else:
            acc[...] += _dot_nt(a_ref[...], b_ref[...])

        @pl.when(k == nk - 1)
        def _():
            dh = acc[...]
            if has_res:
                dh = dh + r_ref[...]
            xv = x_ref[...]
            r = _rstd(xv)
            xhat = xv * r
            gd = dh * g_ref[...]
            dx = r * (gd - xhat * jnp.mean(xhat * gd, axis=-1, keepdims=True))
            if has_dres:
                dx = dx + d_ref[...]
            dx_ref[...] = dx
            dg_ref[...] += _colsum(dh * xhat)

    row = pl.BlockSpec((tm, N), lambda i, k: (i, 0))
    vec = pl.BlockSpec((1, N), lambda i, k: (0, 0))
    ins = [a, b, x, g] + ([a2] if has_a2 else []) + ([res] if has_res else []) + ([dres] if has_dres else [])
    specs = ([pl.BlockSpec((tm, tk), lambda i, k: (i, jnp.minimum(k, nk1 - 1))), pl.BlockSpec((N, tk), lambda i, k: (0, k)), row, vec]
             + ([pl.BlockSpec((tm, tk), lambda i, k: (i, jnp.maximum(k - nk1, 0)))] if has_a2 else [])
             + [row] * (has_res + has_dres))
    dx, dg = pl.pallas_call(
        body, name=name, grid=(M // tm, nk), in_specs=specs, out_specs=[row, vec],
        out_shape=[jax.ShapeDtypeStruct((M, N), F32), jax.ShapeDtypeStruct((1, N), F32)],
        scratch_shapes=[pltpu.VMEM((tm, N), F32)], compiler_params=_cparams(2))(*ins)
    return dx, dg


class Rows:
    def __init__(self, arr, w=None, cb=0, halo=None, h=8):
        self.arr, self.w, self.cb, self.halo, self.h = arr, (arr.shape[1] if w is None else w), cb, halo, h


def rowwise(fn, rows, consts=(), outs=(), accs=(), scratch=(), *, ts, name, reverse=False):
    rows = [r if isinstance(r, Rows) else Rows(r) for r in rows]
    S = rows[0].arr.shape[0]
    n = S // ts
    tile = (lambda s: n - 1 - s) if reverse else (lambda s: s)

    def row_spec(r):
        if r.halo is None:
            return pl.BlockSpec((ts, r.w), lambda s: (tile(s), r.cb))
        per, last = ts // r.h, S // r.h - 1
        if r.halo == "prev":
            return pl.BlockSpec((r.h, r.w), lambda s: (jnp.maximum(tile(s) * per - 1, 0), r.cb))
        return pl.BlockSpec((r.h, r.w), lambda s: (jnp.minimum((tile(s) + 1) * per, last), r.cb))

    def whole(shape):
        nd = len(shape)
        return pl.BlockSpec(tuple(shape), lambda s: (0,) * nd)

    nr, nc, no, na = len(rows), len(consts), len(outs), len(accs)

    def body(*refs):
        in_refs, out_refs = refs[:nr + nc], refs[nr + nc:nr + nc + no]
        acc_refs, scr = refs[nr + nc + no:nr + nc + no + na], refs[nr + nc + no + na:]
        step = pl.program_id(0)
        o, inc = fn(step, tile(step), n, *[r[...] for r in in_refs], *scr)
        for ref, val in zip(out_refs, o):
            ref[...] = val.astype(ref.dtype)
        if na:
            @pl.when(step == 0)
            def _():
                for ref in acc_refs:
                    ref[...] = jnp.zeros_like(ref)
            for ref, val in zip(acc_refs, inc):
                ref[...] += val.astype(ref.dtype)

    out_shape = [jax.ShapeDtypeStruct((S, w), dt) for w, dt in outs] + [jax.ShapeDtypeStruct(tuple(s), dt) for s, dt in accs]
    out_specs = [pl.BlockSpec((ts, w), lambda s: (tile(s), 0)) for w, _ in outs] + [whole(s) for s, _ in accs]
    res = pl.pallas_call(
        body, name=name, grid=(n,),
        in_specs=[row_spec(r) for r in rows] + [whole(c.shape) for c in consts],
        out_specs=out_specs, out_shape=out_shape, scratch_shapes=list(scratch),
        compiler_params=_cparams(1))(*[r.arr for r in rows], *consts)
    return res[:no], res[no:]


def _rstd(x):
    return lax.rsqrt(jnp.mean(x * x, axis=-1, keepdims=True) + EPS)


def rms_fwd(x, g, *, name, ts=512):
    def fn(step, i, n, x, g):
        return ((x * _rstd(x)) * g,), ()
    return rowwise(fn, [x], [g], [(D, BF)], ts=min(ts, x.shape[0]), name=name)[0][0]


def rms_bwd(x, g, dh, dres, *, name, ts=512):
    has_res = dres is not None

    def fn(step, i, n, x, dh, *rest):
        g = rest[-1]
        r = _rstd(x)
        xhat = x * r
        dh = dh.astype(F32)
        gd = dh * g
        dx = r * (gd - xhat * jnp.mean(xhat * gd, axis=-1, keepdims=True))
        if has_res:
            dx = dx + rest[0]
        return (dx,), (_colsum(dh * xhat),)

    rows = [x, dh] + ([dres] if has_res else [])
    o, a = rowwise(fn, rows, [g], [(D, F32)], [((1, D), F32)], ts=min(ts, x.shape[0]), name=name)
    return o[0], a[0]


def swiglu_in(h, w_in, name):
    M, K = h.shape
    tm, tn = _pick(M, _TM), _pick(DFF, _TN)
    nj = DFF // tn

    def body(h_ref, wa_ref, wb_ref, a_ref, b_ref, act_ref):
        hv = h_ref[...]
        a = _dot_nn(hv, wa_ref[...]).astype(a_ref.dtype)
        b = _dot_nn(hv, wb_ref[...]).astype(b_ref.dtype)
        a_ref[...] = a
        b_ref[...] = b
        af = a.astype(F32)
        act_ref[...] = ((af * _sigmoid(af)) * b.astype(F32)).astype(act_ref.dtype)

    out = pl.BlockSpec((tm, tn), lambda i, j: (i, j))
    return pl.pallas_call(
        body, name=name, grid=(M // tm, nj),
        in_specs=[pl.BlockSpec((tm, K), lambda i, j: (i, 0)), pl.BlockSpec((K, tn), lambda i, j: (0, j)),
                  pl.BlockSpec((K, tn), lambda i, j: (0, j + nj))],
        out_specs=[out, out, out], out_shape=[jax.ShapeDtypeStruct((M, DFF), BF)] * 3,
        compiler_params=_cparams(2))(h, w_in, w_in)


def ffn_fwd(x, h, w_in, w_out, next_g, tag):
    a, b, act = swiglu_in(h, w_in, tag + "_in")
    y = matmul(act, w_out, "nn", out_dtype=F32, alpha=0.5, res=x, norm_g=next_g, name=tag + "_out")
    y, h_next = y if next_g is not None else (y, None)
    return y, h_next, (x, h, a, b, act)


def swiglu_bwd(dy, w_out, a, b, name):
    M, K = dy.shape
    tm, tn = _pick(M, _TM), _pick(DFF, _TN)

    def body(dy_ref, w_ref, a_ref, b_ref, da_ref, db_ref):
        dact = 0.5 * _dot_nt(dy_ref[...], w_ref[...])
        av, bv = a_ref[...].astype(F32), b_ref[...].astype(F32)
        s = _sigmoid(av)
        da_ref[...] = (dact * bv * (s * (1.0 + av * (1.0 - s)))).astype(da_ref.dtype)
        db_ref[...] = (dact * (av * s)).astype(db_ref.dtype)

    tile = pl.BlockSpec((tm, tn), lambda i, j: (i, j))
    return pl.pallas_call(
        body, name=name, grid=(M // tm, DFF // tn),
        in_specs=[pl.BlockSpec((tm, K), lambda i, j: (i, 0)), pl.BlockSpec((tn, K), lambda i, j: (j, 0)), tile, tile],
        out_specs=[tile, tile], out_shape=[jax.ShapeDtypeStruct((M, DFF), BF)] * 2,
        compiler_params=_cparams(2))(dy, w_out, a, b)


def ffn_bwd(dy, saved, g, w_in, w_out, tag):
    x, h, a, b, act = saved
    da, db = swiglu_bwd(dy, w_out, a, b, tag + "_dact")
    dw_out = matmul(act, dy, "tn", alpha=0.5, name=tag + "_dwout")
    dw_in = jnp.concatenate([matmul(h, da, "tn", name=tag + "_dwin_a"), matmul(h, db, "tn", name=tag + "_dwin_b")], axis=1)
    dx, dg = matmul_rms_bwd(da, w_in, x, g, dy, a2=db, name=tag + "_dh")
    return dx, dg, dw_in, dw_out


POOL_TS = 256
POOL_HALO = 16


def pool_fwd(P, w_pool, pool_scale, tag):
    ts = POOL_TS

    def fn(step, i, n, xa, halo, w, scale):
        xa = xa.astype(F32)
        halo = halo.astype(F32) * jnp.where(i > 0, 1.0, 0.0)
        ext = jnp.concatenate([halo, xa], axis=0)
        pos = (i * ts + 1 + _iota((ts, LANES), 0)).astype(F32)
        ds, ys = [], []
        for gi, win in enumerate(POOL_WINDOWS):
            e = ext[:, gi * LANES:(gi + 1) * LANES]
            sh = 1
            while sh < win:
                e = e + pltpu.roll(e, sh, 0)
                sh *= 2
            mean = e[POOL_HALO:] / jnp.minimum(pos, float(win))
            d = mean - xa[:, gi * LANES:(gi + 1) * LANES]
            ds.append(d)
            ys.append(_dot_nn(d, w[gi]))
        d = jnp.concatenate(ds, axis=1)
        return (d, jnp.concatenate(ys, axis=1) * scale), ()

    xa = Rows(P, POOL_W, PK_XA // POOL_W)
    xa_prev = Rows(P, POOL_W, PK_XA // POOL_W, "prev", POOL_HALO)
    (d, ya0), _ = rowwise(fn, [xa, xa_prev], [w_pool, pool_scale], [(POOL_W, BF), (POOL_W, BF)], ts=ts, name=tag + "_pool")
    return d, ya0


def pool_bwd(dya0, d, w_pool, pool_scale, tag):
    ts = POOL_TS
    L = ts + POOL_HALO

    def fn(step, i, n, dya0, dya0_next, d, w, scale):
        d = d.astype(F32)
        dz = jnp.concatenate([dya0.astype(F32), dya0_next.astype(F32) * jnp.where(i < n - 1, 1.0, 0.0)], axis=0) * scale
        pos = (i * ts + 1 + _iota((L, LANES), 0)).astype(F32)
        dxa, dws, dsc = [], [], []
        for gi, win in enumerate(POOL_WINDOWS):
            sl = slice(gi * LANES, (gi + 1) * LANES)
            dzg = dz[:, sl]
            dd = _dot_nt(dzg, w[gi])
            e = dd / jnp.minimum(pos, float(win))
            sh = 1
            while sh < win:
                e = e + pltpu.roll(e, L - sh, 0)
                sh *= 2
            dxa.append(e[:ts] - dd[:ts])
            dws.append(_dot_tn(d[:, sl], dzg[:ts])[None])
            z = _dot_nn(d[:, sl], w[gi])
            dsc.append(_colsum(dya0[:, sl].astype(F32) * z))
        return (jnp.concatenate(dxa, axis=1),), (jnp.concatenate(dws, axis=0), jnp.concatenate(dsc, axis=1))

    (dxa,), (dw_pool, dscale) = rowwise(
        fn, [dya0, Rows(dya0, halo="next", h=POOL_HALO), d], [w_pool, pool_scale],
        [(POOL_W, BF)], [((4, LANES, LANES), F32), ((1, POOL_W), F32)], ts=ts, name=tag + "_dpool")
    return dxa, dw_pool, dscale


LRU_R = 512


def _scan_fwd(A, U):
    R = A.shape[0]
    row = _iota(A.shape, 0)
    d = 1
    while d < R:
        m = row >= d
        A_sh = jnp.where(m, pltpu.roll(A, d, 0), 1.0)
        U_sh = jnp.where(m, pltpu.roll(U, d, 0), 0.0)
        U = A * U_sh + U
        A = A * A_sh
        d *= 2
    return A, U


def _scan_bwd(B, X):
    R = B.shape[0]
    row = _iota(B.shape, 0)
    d = 1
    while d < R:
        m = row < R - d
        B_sh = jnp.where(m, pltpu.roll(B, R - d, 0), 1.0)
        X_sh = jnp.where(m, pltpu.roll(X, R - d, 0), 0.0)
        X = X + B * X_sh
        B = B * B_sh
        d *= 2
    return X


def _lru_gates(xc, wa, ba, wx, bx, lam):
    r = _sigmoid(_dot_nn(xc, wa) + ba)
    ig = _sigmoid(_dot_nn(xc, wx) + bx)
    sp = _softplus(-lam)
    log_a = -LRU_C * r * sp
    a = jnp.exp(log_a)
    mult = jnp.sqrt(-_expm1(2.0 * log_a))
    return r, ig, sp, a, mult


def _lru_specs(S, R, reverse):
    nch = S // R
    ch = (lambda j: nch - 1 - j) if reverse else (lambda j: j)
    per = R // 8

    def col(off):
        return pl.BlockSpec((R, LANES), lambda h, j: (ch(j), off + h))

    def prev(off):
        return pl.BlockSpec((8, LANES), lambda h, j: (jnp.maximum(ch(j) * per - 1, 0), off + h))

    vec = pl.BlockSpec((1, LANES), lambda h, j: (0, h))
    cw = pl.BlockSpec((4, LANES), lambda h, j: (0, h))
    wsq = pl.BlockSpec((None, LANES, LANES), lambda h, j: (h, 0, 0))
    return nch, ch, col, prev, vec, cw, wsq


def lru_fwd(P, conv_w, conv_b, w_a, b_a, w_x, b_x, lam, tag):
    S = P.shape[0]
    R = min(LRU_R, S)
    nch, ch, col, prev, vec, cw_spec, wsq = _lru_specs(S, R, False)

    def body(xb_ref, halo_ref, gb_ref, cw_ref, cb_ref, wa_ref, ba_ref, wx_ref, bx_ref, lam_ref, xc_ref, h_ref, yb_ref, carry):
        j = pl.program_id(1)
        xb = xb_ref[...].astype(F32)
        halo = halo_ref[...].astype(F32) * jnp.where(j > 0, 1.0, 0.0)
        ext = jnp.concatenate([halo, xb], axis=0)
        cw = cw_ref[...]
        xc = cb_ref[...]
        for k in range(4):
            e = ext if k == 3 else pltpu.roll(ext, 3 - k, 0)
            xc = xc + e[8:] * cw[k:k + 1]
        r, ig, sp, a, mult = _lru_gates(xc, wa_ref[...], ba_ref[...], wx_ref[...], bx_ref[...], lam_ref[...])
        u = mult * (ig * xc)
        cum_a, hloc = _scan_fwd(a, u)

        @pl.when(j == 0)
        def _():
            carry[...] = jnp.zeros_like(carry)

        hfull = hloc + cum_a * carry[0:1]
        carry[...] = jnp.broadcast_to(hfull[R - 1:R], carry.shape)
        gel, _ = _gelu_and_grad(gb_ref[...].astype(F32))
        xc_ref[...] = xc
        h_ref[...] = hfull
        yb_ref[...] = (hfull * gel).astype(yb_ref.dtype)

    out = pl.BlockSpec((R, LANES), lambda h, j: (j, h))
    return pl.pallas_call(
        body, name=tag + "_lru", grid=(LRU_HEADS, nch),
        in_specs=[col(PK_XB // LANES), prev(PK_XB // LANES), col(PK_GB // LANES), cw_spec, vec, wsq, vec, wsq, vec, vec],
        out_specs=[out, out, out],
        out_shape=[jax.ShapeDtypeStruct((S, D), F32), jax.ShapeDtypeStruct((S, D), F32), jax.ShapeDtypeStruct((S, D), BF)],
        scratch_shapes=[pltpu.VMEM((8, LANES), F32)], compiler_params=_cparams(2),
    )(P, P, P, conv_w, conv_b, w_a, b_a, w_x, b_x, lam)


def lru_bwd(dyb0, P, hh, xc, conv_w, w_a, b_a, w_x, b_x, lam, tag):
    S = P.shape[0]
    R = min(LRU_R, S)
    nch, ch, col, prev, vec, cw_spec, wsq = _lru_specs(S, R, True)

    def body(dyb_ref, gb_ref, h_ref, hprev_ref, xc_ref, xb_ref, xbprev_ref, cw_ref, wa_ref, ba_ref, wx_ref, bx_ref, lam_ref,
             dgb_ref, dxb_ref, dwa_ref, dba_ref, dwx_ref, dbx_ref, dlam_ref, dcw_ref, dcb_ref, gcarry, dxc_head):
        j = pl.program_id(1)
        jj = nch - 1 - j
        has_prev = jnp.where(jj > 0, 1.0, 0.0)
        row = _iota((R, LANES), 0)
        xc, hv, lam = xc_ref[...], h_ref[...], lam_ref[...]
        wa, wx = wa_ref[...], wx_ref[...]
        r, ig, sp, a, mult = _lru_gates(xc, wa, ba_ref[...], wx, bx_ref[...], lam)
        dyb = dyb_ref[...].astype(F32)
        gel, dgel = _gelu_and_grad(gb_ref[...].astype(F32))
        dgb_ref[...] = (dyb * hv * dgel).astype(dgb_ref.dtype)

        @pl.when(j == 0)
        def _():
            gcarry[...] = jnp.zeros_like(gcarry)
            dxc_head[...] = jnp.zeros_like(dxc_head)
            for ref in (dwa_ref, dba_ref, dwx_ref, dbx_ref, dlam_ref, dcw_ref, dcb_ref):
                ref[...] = jnp.zeros_like(ref)

        B = jnp.where(row < R - 1, pltpu.roll(a, R - 1, 0), 0.0)
        X = dyb * gel + jnp.where(row == R - 1, gcarry[0:1], 0.0)
        G = _scan_bwd(B, X)
        gcarry[...] = jnp.broadcast_to(a[0:1] * G[0:1], gcarry.shape)
        hprev = jnp.where(row == 0, hprev_ref[...][7:8] * has_prev, pltpu.roll(hv, 1, 0))
        da = G * hprev
        dmult = G * ig * xc
        dig = G * mult * xc
        dxc = G * mult * ig
        dlog_a = da * a - dmult * (a * a) / mult
        dzr = dlog_a * (-LRU_C * sp) * (r * (1.0 - r))
        dzi = dig * (ig * (1.0 - ig))
        dxc = dxc + _dot_nt(dzr, wa) + _dot_nt(dzi, wx)
        dwa_ref[...] += _dot_tn(xc, dzr)
        dwx_ref[...] += _dot_tn(xc, dzi)
        dba_ref[...] += _colsum(dzr)
        dbx_ref[...] += _colsum(dzi)
        dlam_ref[...] += _colsum(dlog_a * (-LRU_C * r)) * (-_sigmoid(-lam))
        dcb_ref[...] += _colsum(dxc)
        cw = cw_ref[...]
        ext = jnp.concatenate([dxc, dxc_head[...]], axis=0)
        dxb = dxc * cw[3:4]
        for k in range(3):
            dxb = dxb + pltpu.roll(ext, R + 8 - (3 - k), 0)[:R] * cw[k:k + 1]
        dxb_ref[...] = dxb.astype(dxb_ref.dtype)
        dxc_head[...] = dxc[0:8]
        extx = jnp.concatenate([xbprev_ref[...].astype(F32) * has_prev, xb_ref[...].astype(F32)], axis=0)
        incs = []
        for k in range(4):
            e = extx if k == 3 else pltpu.roll(extx, 3 - k, 0)
            incs.append(_colsum(dxc * e[8:]))
        dcw_ref[...] += jnp.concatenate(incs, axis=0)

    plain = pl.BlockSpec((R, LANES), lambda h, j: (ch(j), h))
    plain_prev = pl.BlockSpec((8, LANES), lambda h, j: (jnp.maximum(ch(j) * (R // 8) - 1, 0), h))
    return pl.pallas_call(
        body, name=tag + "_dlru", grid=(LRU_HEADS, nch),
        in_specs=[plain, col(PK_GB // LANES), plain, plain_prev, plain, col(PK_XB // LANES), prev(PK_XB // LANES),
                  cw_spec, wsq, vec, wsq, vec, vec],
        out_specs=[plain, plain, wsq, vec, wsq, vec, vec, cw_spec, vec],
        out_shape=[jax.ShapeDtypeStruct((S, D), BF), jax.ShapeDtypeStruct((S, D), BF),
                   jax.ShapeDtypeStruct((LRU_HEADS, LANES, LANES), F32), jax.ShapeDtypeStruct((1, D), F32),
                   jax.ShapeDtypeStruct((LRU_HEADS, LANES, LANES), F32), jax.ShapeDtypeStruct((1, D), F32),
                   jax.ShapeDtypeStruct((1, D), F32), jax.ShapeDtypeStruct((4, D), F32), jax.ShapeDtypeStruct((1, D), F32)],
        scratch_shapes=[pltpu.VMEM((8, LANES), F32), pltpu.VMEM((8, LANES), F32)], compiler_params=_cparams(2),
    )(dyb0, P, hh, hh, xc, P, P, conv_w, w_a, b_a, w_x, b_x, lam)


CUM_TS = 512
FLASH_T = 512
FLASH_G = 8
NEG = -1e30


def cumlogf_fwd(fl, b_f, tag):
    ts = min(CUM_TS, fl.shape[0])

    def fn(step, i, n, fl, bf, carry):
        x = -_softplus(-(fl + bf))
        row = _iota(x.shape, 0)
        d = 1
        while d < ts:
            x = x + jnp.where(row >= d, pltpu.roll(x, d, 0), 0.0)
            d *= 2

        @pl.when(step == 0)
        def _():
            carry[...] = jnp.zeros_like(carry)

        c = x + carry[0:1]
        carry[...] = jnp.broadcast_to(c[ts - 1:ts], carry.shape)
        return (c,), ()

    return rowwise(fn, [fl], [b_f], [(LANES, F32)], scratch=[pltpu.VMEM((8, LANES), F32)], ts=ts, name=tag + "_cum")[0][0]


def cumlogf_bwd(dc, fl, b_f, tag):
    ts = min(CUM_TS, fl.shape[0])

    def fn(step, i, n, dc, fl, bf, carry):
        row = _iota(dc.shape, 0)
        x = dc
        d = 1
        while d < ts:
            x = x + jnp.where(row < ts - d, pltpu.roll(x, ts - d, 0), 0.0)
            d *= 2

        @pl.when(step == 0)
        def _():
            carry[...] = jnp.zeros_like(carry)

        g = x + carry[0:1]
        carry[...] = jnp.broadcast_to(g[0:1], carry.shape)
        dfl = g * _sigmoid(-(fl + bf))
        return (dfl,), (_colsum(dfl),)

    (dfl,), (db,) = rowwise(fn, [dc, fl], [b_f], [(LANES, F32)], [((1, LANES), F32)], scratch=[pltpu.VMEM((8, LANES), F32)],
                            ts=ts, name=tag + "_dcum", reverse=True)
    return dfl, db


FOX_KA = 80


def _split3(c):
    c = c.astype(F32)
    hi = lax.reduce_precision(c, 8, 7)
    r = c - hi
    mid = lax.reduce_precision(r, 8, 7)
    return hi.astype(BF), mid.astype(BF), (r - mid).astype(BF)


def fox_operands(q, k, ch):
    H, S, hd = q.shape
    qs = q * (FOX_HD ** -0.5)
    pieces = [p.astype(F32) for p in _split3(ch)]
    pad = FOX_KA - hd

    def unit(i):
        return (jnp.arange(FOX_KA) == hd + i).astype(F32)

    qa = jnp.pad(qs.transpose(0, 2, 1).astype(F32), ((0, 0), (0, pad), (0, 0)))
    ka = jnp.pad(k.astype(F32), ((0, 0), (0, 0), (0, pad)))
    for i, p in enumerate(pieces):
        qa = qa + p[:, None, :] * unit(i)[None, :, None] + unit(3 + i)[None, :, None]
        ka = ka + unit(i)[None, None, :] - p[:, :, None] * unit(3 + i)[None, None, :]
    return qa.astype(BF), ka.astype(BF), qs


def _causal(s):
    return jnp.where(_iota(s.shape, 0) <= _iota(s.shape, 1), s, NEG)


def flash_fwd(qa, ka, vt, tag):
    H, hd, S = vt.shape
    T = min(FLASH_T, S)
    nb = S // T

    pairs = [(qi, ki) for qi in range(nb) for ki in range(qi + 1)]
    q_tab = jnp.asarray([p[0] for p in pairs], jnp.int32)
    k_tab = jnp.asarray([p[1] for p in pairs], jnp.int32)

    def body(q_tab_ref, k_tab_ref, ka_ref, qa_ref, vt_ref, o_ref, lse_ref, m_s, l_s, acc):
        qi, ki = q_tab_ref[pl.program_id(1)], k_tab_ref[pl.program_id(1)]

        @pl.when(ki == 0)
        def _():
            m_s[...] = jnp.full_like(m_s, NEG)
            l_s[...] = jnp.zeros_like(l_s)
            acc[...] = jnp.zeros_like(acc)

        def step(diagonal):
            scores = []
            for g in range(FLASH_G):
                s = _dot_nn(ka_ref[g], qa_ref[g])
                scores.append(_causal(s) if diagonal else s)
            for g in range(FLASH_G):
                s = scores[g]
                m_new = jnp.maximum(m_s[g], jnp.max(s, axis=0, keepdims=True))
                alpha = jnp.exp(m_s[g] - m_new)
                p = jnp.exp(s - m_new)
                l_s[g] = alpha * l_s[g] + jnp.sum(p, axis=0, keepdims=True)
                p_hi = p.astype(BF)
                p_lo = p - p_hi.astype(F32)
                v = vt_ref[g]
                acc[g] = alpha * acc[g] + (_dot_nn(v, p_hi) + _dot_nn(v, p_lo))
                m_s[g] = m_new

        @pl.when(ki < qi)
        def _():
            step(False)

        @pl.when(ki == qi)
        def _():
            step(True)
            o_ref[...] = acc[...] / l_s[...]
            lse_ref[...] = m_s[...] + jnp.log(l_s[...])

    G = FLASH_G
    grid_spec = pltpu.PrefetchScalarGridSpec(
        num_scalar_prefetch=2, grid=(H // G, len(pairs)),
        in_specs=[pl.BlockSpec((G, T, FOX_KA), lambda h, p, qt, kt: (h, kt[p], 0)),
                  pl.BlockSpec((G, FOX_KA, T), lambda h, p, qt, kt: (h, 0, qt[p])),
                  pl.BlockSpec((G, hd, T), lambda h, p, qt, kt: (h, 0, kt[p]))],
        out_specs=[pl.BlockSpec((G, hd, T), lambda h, p, qt, kt: (h, 0, qt[p])),
                   pl.BlockSpec((G, 1, T), lambda h, p, qt, kt: (h, 0, qt[p]))],
        scratch_shapes=[pltpu.VMEM((G, 1, T), F32), pltpu.VMEM((G, 1, T), F32), pltpu.VMEM((G, hd, T), F32)])
    return pl.pallas_call(
        body, name=tag + "_flash", grid_spec=grid_spec,
        out_shape=[jax.ShapeDtypeStruct((H, hd, S), F32), jax.ShapeDtypeStruct((H, 1, S), F32)],
        compiler_params=_cparams(2))(q_tab, k_tab, ka, qa, vt)


def flash_bwd(qa, ka, qs, kt, v, ot, do, dot_, lse, tag):
    H, S, hd = v.shape
    T = min(FLASH_T, S)
    nb = S // T

    pairs = [(qi, ki) for ki in range(nb) for qi in range(ki, nb)]
    q_tab = jnp.asarray([p[0] for p in pairs], jnp.int32)
    k_tab = jnp.asarray([p[1] for p in pairs], jnp.int32)

    def body(q_tab_ref, k_tab_ref, ka_ref, qa_ref, qs_ref, kt_ref, v_ref, ot_ref, do_ref, dot_ref, lse_ref,
             dqt_ref, dk_ref, dv_ref, dc_ref, dk_acc, dv_acc, dc_acc):
        qi, ki = q_tab_ref[pl.program_id(1)], k_tab_ref[pl.program_id(1)]

        @pl.when(pl.program_id(1) == 0)
        def _():
            dqt_ref[...] = jnp.zeros_like(dqt_ref)

        @pl.when(qi == ki)
        def _():
            dk_acc[...] = jnp.zeros_like(dk_acc)
            dv_acc[...] = jnp.zeros_like(dv_acc)
            dc_acc[...] = jnp.zeros_like(dc_acc)

        def step(diagonal):
            for g in range(FLASH_G):
                s = _dot_nn(ka_ref[g], qa_ref[g])
                if diagonal:
                    s = _causal(s)
                p = jnp.exp(s - lse_ref[g])
                dot_v = dot_ref[g]
                dp = _dot_nn(v_ref[g], dot_v)
                delta = jnp.sum(dot_v.astype(F32) * ot_ref[g], axis=0, keepdims=True)
                ds = p * (dp - delta)
                part = ds[:, 0:LANES]
                for j in range(1, T // LANES):
                    part = part + ds[:, j * LANES:(j + 1) * LANES]
                dc_acc[g] += part
                dsb = ds.astype(BF)
                dv_acc[g] += _dot_nn(p, do_ref[g])
                dk_acc[g] += _dot_nn(dsb, qs_ref[g])
                dqt_ref[g, qi] += _dot_nn(kt_ref[g], dsb)

        @pl.when(qi > ki)
        def _():
            step(False)

        @pl.when(qi == ki)
        def _():
            step(True)

        @pl.when(qi == nb - 1)
        def _():
            dk_ref[...] = dk_acc[...]
            dv_ref[...] = dv_acc[...]
            for g in range(FLASH_G):
                dc_ref[g] = -jnp.sum(dc_acc[g].T, axis=0, keepdims=True)

    qblk = lambda h, p, qt, kt_: (h, qt[p], 0)
    qblk_t = lambda h, p, qt, kt_: (h, 0, qt[p])
    kblk = lambda h, p, qt, kt_: (h, kt_[p], 0)
    kblk_t = lambda h, p, qt, kt_: (h, 0, kt_[p])
    G = FLASH_G
    grid_spec = pltpu.PrefetchScalarGridSpec(
        num_scalar_prefetch=2, grid=(H // G, len(pairs)),
        in_specs=[pl.BlockSpec((G, T, FOX_KA), kblk), pl.BlockSpec((G, FOX_KA, T), qblk_t),
                  pl.BlockSpec((G, T, hd), qblk), pl.BlockSpec((G, hd, T), kblk_t), pl.BlockSpec((G, T, hd), kblk),
                  pl.BlockSpec((G, hd, T), qblk_t), pl.BlockSpec((G, T, hd), qblk), pl.BlockSpec((G, hd, T), qblk_t),
                  pl.BlockSpec((G, 1, T), qblk_t)],
        out_specs=[pl.BlockSpec((G, nb, hd, T), lambda h, p, qt, kt_: (h, 0, 0, 0)), pl.BlockSpec((G, T, hd), kblk),
                   pl.BlockSpec((G, T, hd), kblk), pl.BlockSpec((G, 1, T), kblk_t)],
        scratch_shapes=[pltpu.VMEM((G, T, hd), F32), pltpu.VMEM((G, T, hd), F32), pltpu.VMEM((G, T, LANES), F32)])
    return pl.pallas_call(
        body, name=tag + "_dflash", grid_spec=grid_spec,
        out_shape=[jax.ShapeDtypeStruct((H, nb, hd, T), F32), jax.ShapeDtypeStruct((H, S, hd), F32),
                   jax.ShapeDtypeStruct((H, S, hd), F32), jax.ShapeDtypeStruct((H, 1, S), F32)],
        compiler_params=_cparams(2))(q_tab, k_tab, ka, qa, qs, kt, v, ot, do, dot_, lse)


def _to_heads(x2d):
    S = x2d.shape[0]
    return x2d.reshape(S, FOX_HEADS, FOX_HD).transpose(1, 0, 2)


def _from_heads(x3d):
    S = x3d.shape[1]
    return x3d.transpose(1, 0, 2).reshape(S, FOX_W)


def _gl_rows(P):
    return [Rows(P, D, PK_GL // D + k) for k in range(3)]


def merge_fwd(P, ya, yb, yc, b_gate, tag):
    def fn(step, i, n, g0, g1, g2, ya, yb, yc, b):
        out = 0.0
        for k, (gl, y) in enumerate(((g0, ya), (g1, yb), (g2, yc))):
            out = out + _sigmoid(gl.astype(F32) + b[:, k * D:(k + 1) * D]) * y.astype(F32)
        return (out,), ()
    return rowwise(fn, _gl_rows(P) + [ya, yb, yc], [b_gate], [(D, BF)], ts=512, name=tag + "_merge")[0][0]


def merge_bwd(dm, P, ya, yb, yc, b_gate, tag):
    def fn(step, i, n, dm, g0, g1, g2, ya, yb, yc, b):
        dm = dm.astype(F32)
        dys, dgls = [], []
        for k, (gl, y) in enumerate(((g0, ya), (g1, yb), (g2, yc))):
            g = _sigmoid(gl.astype(F32) + b[:, k * D:(k + 1) * D])
            dys.append(dm * g)
            dgls.append(dm * y.astype(F32) * (g * (1.0 - g)))
        dgl = jnp.concatenate(dgls, axis=1)
        return (dys[0], dys[1], dys[2], dgl), (_colsum(dgl),)
    (dya, dyb, dyc, dgl), (db,) = rowwise(
        fn, [dm] + _gl_rows(P) + [ya, yb, yc], [b_gate], [(D, BF), (D, BF), (D, BF), (3 * D, BF)], [((1, 3 * D), F32)],
        ts=256, name=tag + "_dmerge")
    return dya, dyb, dyc, dgl, db


def _xattn_softmax(qk):
    s = qk * (X_HD ** -0.5)
    e = jnp.exp(s - jnp.max(s, axis=1, keepdims=True))
    return e / jnp.sum(e, axis=1, keepdims=True)


def xattn_fwd(q, kv, tag):
    def fn(step, i, n, q, k, v):
        heads = [slice(h * X_HD, (h + 1) * X_HD) for h in range(X_HEADS)]
        scores = [_dot_nt(q[:, sl], k[:, sl]) for sl in heads]
        os = [_dot_nn(_xattn_softmax(s), v[:, sl]) for s, sl in zip(scores, heads)]
        return (jnp.concatenate(os, axis=1),), ()

    return _xattn_call(fn, [q], kv, [(D, BF)], [], tag + "_xattn")[0][0]


def _xattn_call(fn, rows, kv, outs, accs, name):
    S, ts, M = rows[0].shape[0], 512, kv.shape[0]
    ts = min(ts, S)
    nr, no, na = len(rows), len(outs), len(accs)

    def body(*refs):
        in_refs, out_refs, acc_refs = refs[:nr + 2], refs[nr + 2:nr + 2 + no], refs[nr + 2 + no:]
        step = pl.program_id(0)
        o, inc = fn(step, step, S // ts, *[r[...] for r in in_refs])
        for ref, val in zip(out_refs, o):
            ref[...] = val.astype(ref.dtype)
        if na:
            @pl.when(step == 0)
            def _():
                for ref in acc_refs:
                    ref[...] = jnp.zeros_like(ref)
            for ref, val in zip(acc_refs, inc):
                ref[...] += val

    row = pl.BlockSpec((ts, D), lambda s: (s, 0))
    res = pl.pallas_call(
        body, name=name, grid=(S // ts,),
        in_specs=[row] * nr + [pl.BlockSpec((M, D), lambda s: (0, 0)), pl.BlockSpec((M, D), lambda s: (0, 1))],
        out_specs=[row] * no + [pl.BlockSpec(tuple(s), lambda s_: (0, 0)) for s, _ in accs],
        out_shape=[jax.ShapeDtypeStruct((S, w), dt) for w, dt in outs] + [jax.ShapeDtypeStruct(tuple(s), dt) for s, dt in accs],
        compiler_params=_cparams(1))(*rows, kv, kv)
    return res[:no], res[no:]


def xattn_bwd(q, kv, do, tag):
    def fn(step, i, n, q, do, k, v):
        dqs, dks, dvs = [], [], []
        heads = [slice(h * X_HD, (h + 1) * X_HD) for h in range(X_HEADS)]
        scores = [_dot_nt(q[:, sl], k[:, sl]) for sl in heads]
        dps = [_dot_nt(do[:, sl], v[:, sl]) for sl in heads]
        for sl, qk, dp in zip(heads, scores, dps):
            p = _xattn_softmax(qk)
            ds = (p * (dp - jnp.sum(p * dp, axis=1, keepdims=True)) * (X_HD ** -0.5)).astype(BF)
            dqs.append(_dot_nn(ds, k[:, sl]))
            dks.append(_dot_tn(ds, q[:, sl]))
            dvs.append(_dot_tn(p, do[:, sl]))
        return (jnp.concatenate(dqs, axis=1),), (jnp.concatenate(dks + dvs, axis=1),)

    (dq,), (dkv,) = _xattn_call(fn, [q, do], kv, [(D, BF)], [((kv.shape[0], 2 * D), F32)], tag + "_dxattn")
    return dq, dkv


def layer_fwd(x, h_in, mem, w, next_g, tag):
    x1, u, s_ffn1 = ffn_fwd(x, h_in, w["ffn1_in"], w["ffn1_out"], w["g_mix"], tag + "_ffn1")
    P = matmul(u, w["wp"], "nn", name=tag + "_proj")
    fl = matmul(u, w["wfl"], "nn", out_dtype=F32, name=tag + "_projf")
    d, ya0 = pool_fwd(P, w["w_pool"], w["pool_scale"], tag)
    ya = matmul(ya0, w["up_a"], "nn", name=tag + "_upa")
    xc, hh, yb0 = lru_fwd(P, w["conv_w"], w["conv_b"], w["w_rg_a"], w["b_rg_a"], w["w_rg_x"], w["b_rg_x"], w["lam"], tag)
    yb = matmul(yb0, w["up_b"], "nn", name=tag + "_upb")
    c = cumlogf_fwd(fl, w["b_f"], tag)
    q, k, v = (_to_heads(P[:, o:o + FOX_W]) for o in (PK_Q, PK_K, PK_V))
    qa, ka, qs = fox_operands(q, k, c[:, :FOX_HEADS].T)
    ot, lse = flash_fwd(qa, ka, v.transpose(0, 2, 1), tag)
    o2 = ot.transpose(2, 0, 1).reshape(-1, FOX_W)
    yc = matmul(o2, w["up_c"], "nn", name=tag + "_upc")
    merged = merge_fwd(P, ya, yb, yc, w["b_gate"], tag)
    x2, hq = matmul(merged, w["w_o"], "nn", out_dtype=F32, res=x1, norm_g=w["g_cross"], name=tag + "_wo")
    qx = matmul(hq, w["xq"], "nn", name=tag + "_xq")
    mn = rms_fwd(mem, w["g_mem"], name=tag + "_mrms")
    kv = matmul(mn, w["xkv"], "nn", name=tag + "_xkv")
    ox = xattn_fwd(qx, kv, tag)
    x3, h2 = matmul(ox, w["xo"], "nn", out_dtype=F32, res=x2, norm_g=w["g_ffn2"], name=tag + "_xo")
    x4, h_out, s_ffn2 = ffn_fwd(x3, h2, w["ffn2_in"], w["ffn2_out"], next_g, tag + "_ffn2")
    saved = dict(ffn1=s_ffn1, ffn2=s_ffn2, x1=x1, u=u, P=P, fl=fl, d=d, ya0=ya0, ya=ya, xc=xc, hh=hh, yb0=yb0, yb=yb,
                 qa=qa, ka=ka, qs=qs, k=k, v=v, ot=ot, lse=lse, o2=o2, yc=yc, merged=merged, x2=x2, hq=hq, qx=qx,
                 mn=mn, kv=kv, ox=ox)
    return x4, h_out, saved


def layer_bwd(dx4, mem, w, s, tag):
    g = {}
    dx3, g["g_ffn2"], g["w_ffn2_in"], g["w_ffn2_out"] = ffn_bwd(dx4, s["ffn2"], w["g_ffn2"], w["ffn2_in"], w["ffn2_out"], tag + "_ffn2")
    dox = matmul(dx3, w["xo"], "nt", name=tag + "_dox")
    g["w_xo"] = matmul(s["ox"], dx3, "tn", name=tag + "_dwxo")
    dqx, dkv = xattn_bwd(s["qx"], s["kv"], dox, tag)
    g["w_xq"] = matmul(s["hq"], dqx, "tn", name=tag + "_dwxq")
    dx2, g["g_cross"] = matmul_rms_bwd(dqx, w["xq"], s["x2"], w["g_cross"], dx3, name=tag + "_dhq")
    g["w_xkv"] = matmul(s["mn"], dkv, "tn", name=tag + "_dwxkv")
    dmn = matmul(dkv, w["xkv"], "nt", name=tag + "_dmn")
    _, g["g_mem"] = rms_bwd(mem, w["g_mem"], dmn, None, name=tag + "_dmrms")
    P = s["P"]
    dmerged = matmul(dx2, w["w_o"], "nt", name=tag + "_dmerged")
    g["w_o"] = matmul(s["merged"], dx2, "tn", name=tag + "_dwo")
    dya, dyb, dyc, dgl, g["b_gate"] = merge_bwd(dmerged, P, s["ya"], s["yb"], s["yc"], w["b_gate"], tag)
    dya0 = matmul(dya, w["up_a"], "nt", name=tag + "_dya0")
    g["w_up_a"] = matmul(s["ya0"], dya, "tn", name=tag + "_dwupa")
    dxa, g["w_pool"], g["pool_scale"] = pool_bwd(dya0, s["d"], w["w_pool"], w["pool_scale"], tag)
    dyb0 = matmul(dyb, w["up_b"], "nt", name=tag + "_dyb0")
    g["w_up_b"] = matmul(s["yb0"], dyb, "tn", name=tag + "_dwupb")
    (dgb, dxb, g["w_rg_a"], g["b_rg_a"], g["w_rg_x"], g["b_rg_x"], g["lru_lambda"], g["conv_w"], g["conv_b"]) = lru_bwd(
        dyb0, P, s["hh"], s["xc"], w["conv_w"], w["w_rg_a"], w["b_rg_a"], w["w_rg_x"], w["b_rg_x"], w["lam"], tag)
    do2 = matmul(dyc, w["up_c"], "nt", name=tag + "_do2")
    g["w_up_c"] = matmul(s["o2"], dyc, "tn", name=tag + "_dwupc")
    do = _to_heads(do2)
    dqt, dk, dv, dck = flash_bwd(s["qa"], s["ka"], s["qs"], s["k"].transpose(0, 2, 1), s["v"], s["ot"], do,
                                 do.transpose(0, 2, 1), s["lse"], tag)
    dq = (dqt * (FOX_HD ** -0.5)).transpose(0, 1, 3, 2).reshape(dk.shape)
    dc = jnp.pad(dck[:, 0, :].T, ((0, 0), (0, LANES - FOX_HEADS)))
    dfl, g["b_f"] = cumlogf_bwd(dc, s["fl"], w["b_f"], tag)
    dP = jnp.concatenate([dxb, dgb, dgl, dxa] + [_from_heads(t).astype(BF) for t in (dq, dk, dv)], axis=1)
    du = matmul(dP, w["wp"], "nt", out_dtype=F32, name=tag + "_du")
    dx1, g["g_mix"] = matmul_rms_bwd(dfl, w["wfl"], s["x1"], w["g_mix"], dx2, res=du, name=tag + "_duf")
    g["wp"] = matmul(s["u"], dP, "tn", name=tag + "_dwp")
    g["wfl"] = matmul(s["u"], dfl, "tn", out_dtype=F32, name=tag + "_dwfl")
    dx0, g["g_ffn1"], g["w_ffn1_in"], g["w_ffn1_out"] = ffn_bwd(dx1, s["ffn1"], w["g_ffn1"], w["ffn1_in"], w["ffn1_out"], tag + "_ffn1")
    return dx0, g


def loss_head(x, target, g_final):
    def fn(step, i, n, x, t, g):
        r = _rstd(x)
        xhat = x * r
        e = xhat * g - t
        dy = e * (1.0 / D)
        gd = dy * g
        dx = r * (gd - xhat * jnp.mean(xhat * gd, axis=-1, keepdims=True))
        loss = 0.5 * jnp.sum(jnp.mean(e * e, axis=-1, keepdims=True), axis=0, keepdims=True)
        return (dx,), (jnp.broadcast_to(loss, (1, LANES)), _colsum(dy * xhat))

    (dx,), (loss, dg) = rowwise(fn, [x, target], [g_final], [(D, F32)], [((1, LANES), F32), ((1, D), F32)], ts=512, name="loss_head")
    return loss[0, 0], dx, dg


BIG = (("w_ffn1_in", (D, 2 * DFF), 1), ("w_ffn1_out", (DFF, D), 0), ("w_in", (D, 7176), 1), ("w_up_a", (POOL_W, D), 1),
       ("conv_w", (4, D), 1), ("w_up_b", (D, D), 0), ("w_up_c", (FOX_W, D), 1), ("w_o", (D, D), 0), ("w_xq", (D, D), 0),
       ("w_xkv", (D, 2 * D), 1), ("w_xo", (D, D), 0), ("w_ffn2_in", (D, 2 * DFF), 1), ("w_ffn2_out", (DFF, D), 0))
SMALL = (("g_ffn1", (D,)), ("g_mix", (D,)), ("b_f", (FOX_HEADS,)), ("b_gate", (3 * D,)), ("w_pool", (4, LANES, LANES)),
         ("pool_scale", (POOL_W,)), ("conv_b", (D,)), ("w_rg_a", (LRU_HEADS, LANES, LANES)), ("b_rg_a", (D,)),
         ("w_rg_x", (LRU_HEADS, LANES, LANES)), ("b_rg_x", (D,)), ("lru_lambda", (D,)), ("g_cross", (D,)), ("g_mem", (D,)),
         ("g_ffn2", (D,)))
ORDER = ("g_ffn1", "w_ffn1_in", "w_ffn1_out", "g_mix", "w_in", "b_f", "b_gate", "w_pool", "pool_scale", "w_up_a", "conv_w",
         "conv_b", "w_rg_a", "b_rg_a", "w_rg_x", "b_rg_x", "lru_lambda", "w_up_b", "w_up_c", "w_o", "g_cross", "g_mem", "w_xq",
         "w_xkv", "w_xo", "g_ffn2", "w_ffn2_in", "w_ffn2_out", "g_final")

IN_SPLIT = (("xa", 0, 512), ("xb", 512, 1024), ("gb", 1536, 1024), ("q", 2560, 512), ("k", 3072, 512), ("v", 3584, 512),
            ("fl", 4096, 8), ("gl", 4104, 3072))
PACK_ORDER = ("xb", "gb", "gl", "xa", "q", "k", "v")


def _shard_shape(shape, axis):
    s = list(shape)
    s[axis] //= N_CHIPS
    return (DEPTH, *s)


def _round_up(n, m):
    return -(-n // m) * m


SMALL_SEG_ROWS = 8


def _small_rows(n_elements):
    return _round_up(-(-n_elements // D), SMALL_SEG_ROWS)


def _pack_small(tree, names, row_multiple=SMALL_SEG_ROWS, fill=0.0):
    parts = []
    for n in names:
        flat = tree[n].astype(F32).reshape(-1)
        rows = _small_rows(flat.shape[0])
        parts.append(jnp.pad(flat, (0, rows * D - flat.shape[0]), constant_values=fill).reshape(rows, D))
    total = sum(p.shape[0] for p in parts)
    if total % row_multiple:
        parts.append(jnp.full((_round_up(total, row_multiple) - total, D), fill, F32))
    return jnp.concatenate(parts, axis=0)


def _unpack_small(buf, like, names):
    out, off = {}, 0
    for n in names:
        sh = like[n].shape
        rows = _small_rows(math.prod(sh))
        out[n] = buf[off:off + rows].reshape(-1)[:math.prod(sh)].reshape(sh)
        off += rows
    return out


def _from_chip_major(g, shape, axis):
    return jnp.moveaxis(g, 1, axis + 1).reshape(DEPTH, *shape)


def _to_chip_major(full, shape, axis):
    sh = list(shape)
    sh[axis:axis + 1] = [N_CHIPS, shape[axis] // N_CHIPS]
    return jnp.moveaxis(full.reshape(DEPTH, *sh), axis + 1, 1)


def _layer_weights(full, small, l):
    w_in = full["w_in"][l]
    cols = {n: w_in[:, o:o + s] for n, o, s in IN_SPLIT}
    w = dict(ffn1_in=full["w_ffn1_in"][l], ffn1_out=full["w_ffn1_out"][l], ffn2_in=full["w_ffn2_in"][l], ffn2_out=full["w_ffn2_out"][l],
             wp=jnp.concatenate([cols[n] for n in PACK_ORDER], axis=1),
             wfl=jnp.pad(cols["fl"], ((0, 0), (0, LANES - FOX_HEADS))),
             up_a=full["w_up_a"][l], up_b=full["w_up_b"][l], up_c=full["w_up_c"][l], w_o=full["w_o"][l], xq=full["w_xq"][l],
             xkv=full["w_xkv"][l], xo=full["w_xo"][l], conv_w=full["conv_w"][l])
    for n in ("g_ffn1", "g_mix", "b_gate", "pool_scale", "conv_b", "b_rg_a", "b_rg_x", "g_cross", "g_mem", "g_ffn2"):
        w[n] = small[n][l].reshape(1, -1)
    w["lam"] = small["lru_lambda"][l].reshape(1, -1)
    w["b_f"] = jnp.pad(small["b_f"][l], (0, LANES - FOX_HEADS)).reshape(1, LANES)
    for n in ("w_pool", "w_rg_a", "w_rg_x"):
        w[n] = small[n][l]
    return w


def _unpack_w_in_grad(gwp, gwfl):
    pk = {}
    off = 0
    sizes = {n: s for n, _, s in IN_SPLIT}
    for n in PACK_ORDER:
        pk[n] = gwp[:, off:off + sizes[n]]
        off += sizes[n]
    pk["fl"] = gwfl[:, :FOX_HEADS].astype(gwp.dtype)
    return jnp.concatenate([pk[n] for n, _, _ in IN_SPLIT], axis=1)


def _place():
    x, y, c = lax.axis_index("x"), lax.axis_index("y"), lax.axis_index("c")
    others = [(1 - x, y), (x, 1 - y), (1 - x, 1 - y)]
    return x, y, c, others


def _rcopy(src, dst, send_sems, recv_sems, k, to):
    return pltpu.make_async_remote_copy(src_ref=src, dst_ref=dst, send_sem=send_sems.at[k], recv_sem=recv_sems.at[k],
                                        device_id=to, device_id_type=MESH)


def _comm_call(body, name, ins, out_shape, n_sems):
    return pl.pallas_call(
        body, name=name, in_specs=[ANY] * len(ins), out_specs=[ANY] * len(out_shape), out_shape=out_shape,
        scratch_shapes=[pltpu.SemaphoreType.DMA((n_sems,)), pltpu.SemaphoreType.DMA((n_sems,)), pltpu.SemaphoreType.DMA((n_sems,))],
    )(*ins)


def allgather_same_core(bufs):
    nb = len(bufs)

    def body(*refs):
        ins, outs = refs[:nb], refs[nb:2 * nb]
        send_sems, recv_sems, local_sems = refs[2 * nb:]
        x, y, c, others = _place()
        me = 2 * x + y
        sends = []
        for b in range(nb):
            for j, (ox, oy) in enumerate(others):
                cp = _rcopy(ins[b], outs[b].at[me], send_sems, recv_sems, 3 * b + j, (ox, oy, c))
                cp.start()
                sends.append(cp)
        for b in range(nb):
            for j, (ox, oy) in enumerate(others):
                theirs = outs[b].at[2 * ox + oy]
                _rcopy(theirs, theirs, send_sems, recv_sems, 3 * b + j, (ox, oy, c)).wait_recv()
        for cp in sends:
            cp.wait_send()

    shapes = [jax.ShapeDtypeStruct((N_CHIPS, *b.shape), b.dtype) for b in bufs]
    outs = _comm_call(body, "allgather_same_core", list(bufs), shapes, 3 * nb)
    return [_with_own_block(o, b) for o, b in zip(outs, bufs)]


def _with_own_block(gathered, mine):
    me = 2 * lax.axis_index("x") + lax.axis_index("y")
    return lax.dynamic_update_index_in_dim(gathered, mine, me, axis=0)


def gather_weights(shards):
    nw = len(shards)

    def body(*refs):
        ins, outs = refs[:nw], refs[nw:2 * nw]
        send_sems, recv_sems, _ = refs[2 * nw:]
        x, y, c, _ = _place()
        fx, fy = 1 - c, c
        first = (x + fx - 2 * x * fx, y + fy - 2 * y * fy)
        second = (x + fy - 2 * x * fy, y + fx - 2 * y * fx)
        me, c1, c2, cd = 2 * x + y, 2 * first[0] + first[1], 2 * second[0] + second[1], 2 * (1 - x) + (1 - y)
        sibling = (x, y, 1 - c)
        started = []

        def send(w, k, src, dst, to):
            cp = _rcopy(src, dst, send_sems, recv_sems, 6 * w + k, to)
            cp.start()
            started.append(cp)

        def landed(w, k, block):
            _rcopy(block, block, send_sems, recv_sems, 6 * w + k, sibling).wait_recv()

        for w in range(nw):
            send(w, 0, ins[w].at[c], outs[w].at[c, me], (*first, c))
            send(w, 1, ins[w].at[c], outs[w].at[c, me], (*second, c))
        for w in range(nw):
            block = outs[w].at[c, c1]
            landed(w, 0, block)
            send(w, 2, block, block, (*second, c))
            send(w, 3, block, block, sibling)
        for w in range(nw):
            block = outs[w].at[c, c2]
            landed(w, 1, block)
            send(w, 4, block, block, sibling)
        for w in range(nw):
            block = outs[w].at[c, cd]
            landed(w, 2, block)
            send(w, 5, block, block, sibling)
        for w in range(nw):
            landed(w, 3, outs[w].at[1 - c, c2])
            landed(w, 4, outs[w].at[1 - c, c1])
            landed(w, 5, outs[w].at[1 - c, cd])
        for cp in started:
            cp.wait_send()

    shapes = [jax.ShapeDtypeStruct((DEPTH, N_CHIPS, *s.shape[1:]), s.dtype) for s in shards]
    outs = _comm_call(body, "gather_weights", list(shards), shapes, 6 * nw)
    me = 2 * lax.axis_index("x") + lax.axis_index("y")
    return [lax.dynamic_update_index_in_dim(o, s, me, axis=1) for o, s in zip(outs, shards)]


def sibling_layers(bufs):
    nb = len(bufs)

    def body(*refs):
        ins, outs = refs[:nb], refs[nb:2 * nb]
        send_sems, recv_sems, _ = refs[2 * nb:]
        x, y, c, _ = _place()
        work = []
        for b in range(nb):
            cp = _rcopy(ins[b].at[1 - c], outs[b], send_sems, recv_sems, b, (x, y, 1 - c))
            cp.start()
            work.append(cp)
        for cp in work:
            cp.wait()

    shapes = [jax.ShapeDtypeStruct(b.shape[1:], b.dtype) for b in bufs]
    return _comm_call(body, "sibling_layers", list(bufs), shapes, nb)


def chip_blocks(bufs):
    nb = len(bufs)

    def body(*refs):
        ins, outs = refs[:nb], refs[nb:2 * nb]
        send_sems, recv_sems, _ = refs[2 * nb:]
        x, y, c, others = _place()
        sends = []
        for b in range(nb):
            for j, (ox, oy) in enumerate(others):
                cp = _rcopy(ins[b].at[2 * ox + oy], outs[b].at[j], send_sems, recv_sems, 3 * b + j, (ox, oy, c))
                cp.start()
                sends.append(cp)
        for cp in sends:
            cp.wait()

    shapes = [jax.ShapeDtypeStruct((3, *b.shape[1:]), b.dtype) for b in bufs]
    return _comm_call(body, "chip_blocks", list(bufs), shapes, 3 * nb)


def sibling_other_layer(bufs):
    nb = len(bufs)

    def body(*refs):
        ins, outs = refs[:nb], refs[nb:2 * nb]
        send_sems, recv_sems, _ = refs[2 * nb:]
        x, y, c, _ = _place()
        work = []
        for b in range(nb):
            cp = _rcopy(ins[b], outs[b].at[c], send_sems, recv_sems, b, (x, y, 1 - c))
            cp.start()
            work.append(cp)
        for b, cp in enumerate(work):
            cp.wait_send()
            theirs = outs[b].at[1 - c]
            _rcopy(theirs, theirs, send_sems, recv_sems, b, (x, y, 1 - c)).wait_recv()

    shapes = [jax.ShapeDtypeStruct((DEPTH, *b.shape), b.dtype) for b in bufs]
    outs = _comm_call(body, "sibling_other_layer", list(bufs), shapes, nb)
    c = lax.axis_index("c")
    return [lax.dynamic_update_index_in_dim(o, b, c, axis=0) for o, b in zip(outs, bufs)]


ADD_BLOCK_BYTES = 2 * 2 ** 20


def add_arrays(parts, out_dtype, name):
    shape = parts[0].shape
    cols = shape[-1]
    rows = math.prod(shape[:-1])

    def fn(step, i, n, *vals):
        acc = vals[0].astype(F32)
        for v in vals[1:]:
            acc = acc + v.astype(F32)
        return (acc,), ()

    ts = rows
    for cand in (2048, 1024, 512, 256, 128, 64, 32, 16):
        if rows % cand == 0 and cand * cols * 4 <= ADD_BLOCK_BYTES:
            ts = cand
            break
    out = rowwise(fn, [p.reshape(rows, cols) for p in parts], outs=[(cols, out_dtype)], ts=ts, name=name)[0][0]
    return out.reshape(shape)


def reduce_gradients(bufs, names):
    c, me = lax.axis_index("c"), 2 * lax.axis_index("x") + lax.axis_index("y")
    theirs = sibling_layers(bufs)
    mine = [lax.dynamic_index_in_dim(b, c, axis=0, keepdims=False) for b in bufs]
    part = [add_arrays([m, t], m.dtype, "rs_add_sibling_" + n) for m, t, n in zip(mine, theirs, names)]
    recv = chip_blocks(part)
    own = [lax.dynamic_index_in_dim(p, me, axis=0, keepdims=False) for p in part]
    red = [add_arrays([o, r[0], r[1], r[2]], F32, "rs_add_chips_" + n) for o, r, n in zip(own, recv, names)]
    return sibling_other_layer(red)


_BC1 = 1.0 - ADAM_B1 ** ADAM_STEP
_BC2 = 1.0 - ADAM_B2 ** ADAM_STEP


def adamw(w, g, m, v, name):
    rows, cols = w.shape

    def fn(step, i, n, w, g, m, v):
        m2 = ADAM_B1 * m + (1.0 - ADAM_B1) * g
        v2 = ADAM_B2 * v + (1.0 - ADAM_B2) * (g * g)
        delta = -ADAM_LR * ((m2 / _BC1) / (jnp.sqrt(v2 / _BC2) + ADAM_EPS) + ADAM_WD * w)
        return (delta, m2, v2), ()

    ts = _pick(rows, (256, 128, 64, 32, 16, 8))
    return rowwise(fn, [w, g, m, v], outs=[(cols, F32)] * 3, ts=ts, name=name)[0]


def local_step(x, mem, target, full, small):
    ws = [_layer_weights(full, small, l) for l in range(DEPTH)]
    saved = []
    h, hn = x, rms_fwd(x, ws[0]["g_ffn1"], name="l_ffn1_rms")
    for l in range(DEPTH):
        h, hn, s = layer_fwd(h, hn, mem, ws[l], ws[l + 1]["g_ffn1"] if l + 1 < DEPTH else None, "l")
        saved.append(s)
    loss, dh, dg_final = loss_head(h, target, small["g_final"].reshape(1, D))
    grads = [None] * DEPTH
    for l in reversed(range(DEPTH)):
        dh, grads[l] = layer_bwd(dh, mem, ws[l], saved[l], "l")
    out = {}
    for l in range(DEPTH):
        g = grads[l]
        g["w_in"] = _unpack_w_in_grad(g.pop("wp"), g.pop("wfl"))
        g["b_f"] = g["b_f"][:, :FOX_HEADS]
    for name, shape, _ in BIG:
        out[name] = jnp.stack([grads[l][name].reshape(shape) for l in range(DEPTH)])
    for name, shape in SMALL:
        out[name] = jnp.stack([grads[l][name].reshape(shape) for l in range(DEPTH)])
    out["g_final"] = dg_final.reshape(D)
    return loss, dh, out


def kernel(x, mem, g_ffn1, w_ffn1_in, w_ffn1_out, g_mix, w_in, b_f, b_gate, w_pool, pool_scale, w_up_a, conv_w, conv_b, w_rg_a, b_rg_a, w_rg_x, b_rg_x, lru_lambda, w_up_b, w_up_c, w_o, g_cross, g_mem, w_xq, w_xkv, w_xo, g_ffn2, w_ffn2_in, w_ffn2_out, g_final, loss_target, m_g_ffn1, m_w_ffn1_in, m_w_ffn1_out, m_g_mix, m_w_in, m_b_f, m_b_gate, m_w_pool, m_pool_scale, m_w_up_a, m_conv_w, m_conv_b, m_w_rg_a, m_b_rg_a, m_w_rg_x, m_b_rg_x, m_lru_lambda, m_w_up_b, m_w_up_c, m_w_o, m_g_cross, m_g_mem, m_w_xq, m_w_xkv, m_w_xo, m_g_ffn2, m_w_ffn2_in, m_w_ffn2_out, m_g_final, v_g_ffn1, v_w_ffn1_in, v_w_ffn1_out, v_g_mix, v_w_in, v_b_f, v_b_gate, v_w_pool, v_pool_scale, v_w_up_a, v_conv_w, v_conv_b, v_w_rg_a, v_b_rg_a, v_w_rg_x, v_b_rg_x, v_lru_lambda, v_w_up_b, v_w_up_c, v_w_o, v_g_cross, v_g_mem, v_w_xq, v_w_xkv, v_w_xo, v_g_ffn2, v_w_ffn2_in, v_w_ffn2_out, v_g_final):
    args = dict(locals())
    weights = {n: args[n] for n in ORDER}
    m_in = {n: args["m_" + n] for n in ORDER}
    v_in = {n: args["v_" + n] for n in ORDER}
    big_names = [n for n, _, _ in BIG]
    small_names = [n for n, _ in SMALL] + ["g_final"]

    shards = []
    for n in big_names:
        shards += list(_split3(weights[n])) if n == "conv_w" else [weights[n].astype(BF)]
    gathered = iter(gather_weights(shards))
    full = {}
    for n, shape, axis in BIG:
        g = next(gathered)
        if n == "conv_w":
            g = g.astype(F32) + next(gathered).astype(F32) + next(gathered).astype(F32)
        full[n] = _from_chip_major(g, shape, axis)

    small = {n: weights[n] for n in small_names}
    loss, grad_x, contrib = local_step(x[0], mem[0], loss_target[0], full, small)
    loss = lax.psum(loss, ("x", "y", "c"))

    bufs = [_to_chip_major(contrib[n].astype(BF), shape, axis) for n, shape, axis in BIG]
    small_buf = _pack_small(contrib, small_names, DEPTH * N_CHIPS * SMALL_SEG_ROWS).reshape(DEPTH, N_CHIPS, -1, D)
    red = reduce_gradients(bufs + [small_buf], big_names + ["small"])

    grads = dict(zip(big_names, red[:-1]))
    red_small = allgather_same_core([red[-1]])[0].transpose(1, 0, 2, 3).reshape(-1, D)
    grads.update(_unpack_small(red_small, weights, small_names))

    delta, new_m, new_v = {}, {}, {}
    for name in big_names:
        sh = weights[name].shape
        two_d = (-1, sh[-1])
        d, m2, v2 = adamw(weights[name].reshape(two_d), grads[name].reshape(two_d), m_in[name].reshape(two_d),
                          v_in[name].reshape(two_d), "adamw_" + name)
        delta[name], new_m[name], new_v[name] = d.reshape(sh), m2.reshape(sh), v2.reshape(sh)
    outs = adamw(_pack_small(weights, small_names), _pack_small(grads, small_names), _pack_small(m_in, small_names),
                 _pack_small(v_in, small_names, fill=1.0), "adamw_small")
    for t, res in zip((delta, new_m, new_v), outs):
        t.update(_unpack_small(res, weights, small_names))

    return (loss, grad_x[None], *[grads[n] for n in ORDER], *[delta[n] for n in ORDER],
            *[new_m[n] for n in ORDER], *[new_v[n] for n in ORDER])
```

```python
import math

import jax
import jax.numpy as jnp
from jax import lax
from jax.experimental import pallas as pl
from jax.experimental.pallas import tpu as pltpu

F32 = jnp.float32
BF = jnp.bfloat16

D = 1024
DFF = 2816
DEPTH = 2
POOL_W = 512
POOL_WINDOWS = (2, 4, 8, 16)
LRU_HEADS = 8
LRU_C = 8.0
FOX_HEADS = 8
FOX_HD = 64
FOX_W = 512
X_HEADS = 4
X_HD = 256
EPS = 1e-6
LANES = 128
N_CHIPS = 4

ADAM_LR, ADAM_B1, ADAM_B2, ADAM_EPS, ADAM_WD, ADAM_STEP = 0.001, 0.9, 0.999, 1e-08, 0.01, 10

VMEM_LIMIT_BYTES = 56 * 2 ** 20

PK_XB, PK_GB, PK_GL, PK_XA, PK_Q, PK_K, PK_V = 0, 1024, 2048, 5120, 5632, 6144, 6656
PK_W = 7168

MESH = pl.DeviceIdType.MESH
ANY = pl.BlockSpec(memory_space=pl.ANY)


def _cparams(ngrid):
    return pltpu.CompilerParams(dimension_semantics=("arbitrary",) * ngrid, vmem_limit_bytes=VMEM_LIMIT_BYTES)


def _pick(n, cands):
    for c in cands:
        if n % c == 0:
            return c
    return n


def _iota(shape, dim):
    return lax.broadcasted_iota(jnp.int32, shape, dim)


def _sigmoid(x):
    return 1.0 / (1.0 + jnp.exp(-x))


def _softplus(z):
    return jnp.maximum(z, 0.0) + jnp.log1p(jnp.exp(-jnp.abs(z)))


def _expm1(x):
    small = jnp.abs(x) < 0.25
    xs = jnp.where(small, x, 0.0)
    poly = xs * (1.0 + xs * (1 / 2 + xs * (1 / 6 + xs * (1 / 24 + xs * (1 / 120 + xs * (1 / 720 + xs * (1 / 5040 + xs * (1 / 40320))))))))
    return jnp.where(small, poly, jnp.exp(x) - 1.0)


_GELU_K = math.sqrt(2.0 / math.pi)


def _gelu_and_grad(x):
    inner = _GELU_K * (x + 0.044715 * x * x * x)
    t = jnp.tanh(inner)
    g = 0.5 * x * (1.0 + t)
    dg = 0.5 * (1.0 + t) + 0.5 * x * (1.0 - t * t) * _GELU_K * (1.0 + 3 * 0.044715 * x * x)
    return g, dg


def _dot(a, b, dims):
    return lax.dot_general(a.astype(BF), b.astype(BF), (dims, ((), ())), preferred_element_type=F32)


def _dot_nn(a, b):
    return _dot(a, b, ((1,), (0,)))


def _dot_nt(a, b):
    return _dot(a, b, ((1,), (1,)))


def _dot_tn(a, b):
    return _dot(a, b, ((0,), (0,)))


def _colsum(x):
    return jnp.sum(x, axis=0, keepdims=True)


_TM = (1024, 1408, 512, 256, 128)
_TN = (1408, 1024, 512, 384, 256, 128)
_TK = (2432, 1408, 1024, 512, 256, 128)


def matmul(a, b, mode, *, out_dtype=None, alpha=1.0, res=None, norm_g=None, name):
    out_dtype = BF if out_dtype is None else out_dtype
    if mode == "nn":
        (M, K), N = a.shape, b.shape[1]
    elif mode == "nt":
        (M, K), N = a.shape, b.shape[0]
    else:
        (K, M), N = a.shape, b.shape[1]
    tm, tn, tk = _pick(M, _TM), _pick(N, _TN), _pick(K, _TK)
    nk = K // tk
    dims = {"nn": ((1,), (0,)), "nt": ((1,), (1,)), "tn": ((0,), (0,))}[mode]
    a_spec = pl.BlockSpec((tk, tm), lambda i, j, k: (k, i)) if mode == "tn" else pl.BlockSpec((tm, tk), lambda i, j, k: (i, k))
    b_spec = pl.BlockSpec((tn, tk), lambda i, j, k: (j, k)) if mode == "nt" else pl.BlockSpec((tk, tn), lambda i, j, k: (k, j))
    o_spec = pl.BlockSpec((tm, tn), lambda i, j, k: (i, j))
    has_res, has_norm = res is not None, norm_g is not None
    assert not has_norm or tn == N, "the norm epilogue needs whole rows"

    def body(*refs):
        a_ref, b_ref = refs[0], refs[1]
        r_ref = refs[2] if has_res else None
        g_ref = refs[2 + has_res] if has_norm else None
        o_ref = refs[2 + has_res + has_norm]
        h_ref = refs[3 + has_res + has_norm] if has_norm else None
        acc = refs[-1]
        k = pl.program_id(2)

        @pl.when(k == 0)
        def _():
            acc[...] = jnp.zeros_like(acc)

        acc[...] += _dot(a_ref[...], b_ref[...], dims)

        @pl.when(k == nk - 1)
        def _():
            r = acc[...]
            if alpha != 1.0:
                r = r * alpha
            if has_res:
                r = r + r_ref[...].astype(F32)
            r = r.astype(o_ref.dtype)
            o_ref[...] = r
            if has_norm:
                rf = r.astype(F32)
                h_ref[...] = ((rf * _rstd(rf)) * g_ref[...]).astype(h_ref.dtype)

    ins = [a, b] + ([res] if has_res else []) + ([norm_g] if has_norm else [])
    specs = [a_spec, b_spec] + ([o_spec] if has_res else []) + ([pl.BlockSpec((1, N), lambda i, j, k: (0, 0))] if has_norm else [])
    out = pl.pallas_call(
        body, name=name, grid=(M // tm, N // tn, nk), in_specs=specs,
        out_specs=[o_spec, o_spec] if has_norm else o_spec,
        out_shape=([jax.ShapeDtypeStruct((M, N), out_dtype), jax.ShapeDtypeStruct((M, N), BF)] if has_norm
                   else jax.ShapeDtypeStruct((M, N), out_dtype)),
        scratch_shapes=[pltpu.VMEM((tm, tn), F32)], compiler_params=_cparams(3))(*ins)
    return tuple(out) if has_norm else out


def matmul_rms_bwd(a, b, x, g, dres, *, a2=None, res=None, name):
    (M, K), N = a.shape, b.shape[0]
    tm, tk = _pick(M, _TM), _pick(K, _TK)
    nk1 = K // tk
    nk = nk1 * (2 if a2 is not None else 1)
    has_a2, has_res, has_dres = a2 is not None, res is not None, dres is not None

    def body(*refs):
        a_ref, b_ref, x_ref, g_ref = refs[:4]
        rest = list(refs[4:])
        a2_ref = rest.pop(0) if has_a2 else None
        r_ref = rest.pop(0) if has_res else None
        d_ref = rest.pop(0) if has_dres else None
        dx_ref, dg_ref, acc = rest
        i, k = pl.program_id(0), pl.program_id(1)

        @pl.when(k == 0)
        def _():
            acc[...] = jnp.zeros_like(acc)

        @pl.when(jnp.logical_and(i == 0, k == 0))
        def _():
            dg_ref[...] = jnp.zeros_like(dg_ref)

        if has_a2:
            @pl.when(k < nk1)
            def _():
                acc[...] += _dot_nt(a_ref[...], b_ref[...])

            @pl.when(k >= nk1)
            def _():
                acc[...] += _dot_nt(a2_ref[...], b_ref[...])
        else:
            acc[...] += _dot_nt(a_ref[...], b_ref[...])

        @pl.when(k == nk - 1)
        def _():
            dh = acc[...]
            if has_res:
                dh = dh + r_ref[...]
            xv = x_ref[...]
            r = _rstd(xv)
            xhat = xv * r
            gd = dh * g_ref[...]
            dx = r * (gd - xhat * jnp.mean(xhat * gd, axis=-1, keepdims=True))
            if has_dres:
                dx = dx + d_ref[...]
            dx_ref[...] = dx
            dg_ref[...] += _colsum(dh * xhat)

    row = pl.BlockSpec((tm, N), lambda i, k: (i, 0))
    vec = pl.BlockSpec((1, N), lambda i, k: (0, 0))
    ins = [a, b, x, g] + ([a2] if has_a2 else []) + ([res] if has_res else []) + ([dres] if has_dres else [])
    specs = ([pl.BlockSpec((tm, tk), lambda i, k: (i, jnp.minimum(k, nk1 - 1))), pl.BlockSpec((N, tk), lambda i, k: (0, k)), row, vec]
             + ([pl.BlockSpec((tm, tk), lambda i, k: (i, jnp.maximum(k - nk1, 0)))] if has_a2 else [])
             + [row] * (has_res + has_dres))
    dx, dg = pl.pallas_call(
        body, name=name, grid=(M // tm, nk), in_specs=specs, out_specs=[row, vec],
        out_shape=[jax.ShapeDtypeStruct((M, N), F32), jax.ShapeDtypeStruct((1, N), F32)],
        scratch_shapes=[pltpu.VMEM((tm, N), F32)], compiler_params=_cparams(2))(*ins)
    return dx, dg


class Rows:
    def __init__(self, arr, w=None, cb=0, halo=None, h=8):
        self.arr, self.w, self.cb, self.halo, self.h = arr, (arr.shape[1] if w is None else w), cb, halo, h


def rowwise(fn, rows, consts=(), outs=(), accs=(), scratch=(), *, ts, name, reverse=False):
    rows = [r if isinstance(r, Rows) else Rows(r) for r in rows]
    S = rows[0].arr.shape[0]
    n = S // ts
    tile = (lambda s: n - 1 - s) if reverse else (lambda s: s)

    def row_spec(r):
        if r.halo is None:
            return pl.BlockSpec((ts, r.w), lambda s: (tile(s), r.cb))
        per, last = ts // r.h, S // r.h - 1
        if r.halo == "prev":
            return pl.BlockSpec((r.h, r.w), lambda s: (jnp.maximum(tile(s) * per - 1, 0), r.cb))
        return pl.BlockSpec((r.h, r.w), lambda s: (jnp.minimum((tile(s) + 1) * per, last), r.cb))

    def whole(shape):
        nd = len(shape)
        return pl.BlockSpec(tuple(shape), lambda s: (0,) * nd)

    nr, nc, no, na = len(rows), len(consts), len(outs), len(accs)

    def body(*refs):
        in_refs, out_refs = refs[:nr + nc], refs[nr + nc:nr + nc + no]
        acc_refs, scr = refs[nr + nc + no:nr + nc + no + na], refs[nr + nc + no + na:]
        step = pl.program_id(0)
        o, inc = fn(step, tile(step), n, *[r[...] for r in in_refs], *scr)
        for ref, val in zip(out_refs, o):
            ref[...] = val.astype(ref.dtype)
        if na:
            @pl.when(step == 0)
            def _():
                for ref in acc_refs:
                    ref[...] = jnp.zeros_like(ref)
            for ref, val in zip(acc_refs, inc):
                ref[...] += val.astype(ref.dtype)

    out_shape = [jax.ShapeDtypeStruct((S, w), dt) for w, dt in outs] + [jax.ShapeDtypeStruct(tuple(s), dt) for s, dt in accs]
    out_specs = [pl.BlockSpec((ts, w), lambda s: (tile(s), 0)) for w, _ in outs] + [whole(s) for s, _ in accs]
    res = pl.pallas_call(
        body, name=name, grid=(n,),
        in_specs=[row_spec(r) for r in rows] + [whole(c.shape) for c in consts],
        out_specs=out_specs, out_shape=out_shape, scratch_shapes=list(scratch),
        compiler_params=_cparams(1))(*[r.arr for r in rows], *consts)
    return res[:no], res[no:]


def _rstd(x):
    return lax.rsqrt(jnp.mean(x * x, axis=-1, keepdims=True) + EPS)


def rms_fwd(x, g, *, name, ts=512):
    def fn(step, i, n, x, g):
        return ((x * _rstd(x)) * g,), ()
    return rowwise(fn, [x], [g], [(D, BF)], ts=min(ts, x.shape[0]), name=name)[0][0]


def rms_bwd(x, g, dh, dres, *, name, ts=512):
    has_res = dres is not None

    def fn(step, i, n, x, dh, *rest):
        g = rest[-1]
        r = _rstd(x)
        xhat = x * r
        dh = dh.astype(F32)
        gd = dh * g
        dx = r * (gd - xhat * jnp.mean(xhat * gd, axis=-1, keepdims=True))
        if has_res:
            dx = dx + rest[0]
        return (dx,), (_colsum(dh * xhat),)

    rows = [x, dh] + ([dres] if has_res else [])
    o, a = rowwise(fn, rows, [g], [(D, F32)], [((1, D), F32)], ts=min(ts, x.shape[0]), name=name)
    return o[0], a[0]


def swiglu_in(h, w_in, name):
    M, K = h.shape
    tm, tn = _pick(M, _TM), _pick(DFF, _TN)
    nj = DFF // tn

    def body(h_ref, wa_ref, wb_ref, a_ref, b_ref, act_ref):
        hv = h_ref[...]
        a = _dot_nn(hv, wa_ref[...]).astype(a_ref.dtype)
        b = _dot_nn(hv, wb_ref[...]).astype(b_ref.dtype)
        a_ref[...] = a
        b_ref[...] = b
        af = a.astype(F32)
        act_ref[...] = ((af * _sigmoid(af)) * b.astype(F32)).astype(act_ref.dtype)

    out = pl.BlockSpec((tm, tn), lambda i, j: (i, j))
    return pl.pallas_call(
        body, name=name, grid=(M // tm, nj),
        in_specs=[pl.BlockSpec((tm, K), lambda i, j: (i, 0)), pl.BlockSpec((K, tn), lambda i, j: (0, j)),
                  pl.BlockSpec((K, tn), lambda i, j: (0, j + nj))],
        out_specs=[out, out, out], out_shape=[jax.ShapeDtypeStruct((M, DFF), BF)] * 3,
        compiler_params=_cparams(2))(h, w_in, w_in)


def ffn_fwd(x, h, w_in, w_out, next_g, tag):
    a, b, act = swiglu_in(h, w_in, tag + "_in")
    y = matmul(act, w_out, "nn", out_dtype=F32, alpha=0.5, res=x, norm_g=next_g, name=tag + "_out")
    y, h_next = y if next_g is not None else (y, None)
    return y, h_next, (x, h, a, b, act)


def swiglu_bwd(dy, w_out, a, b, name):
    M, K = dy.shape
    tm, tn = _pick(M, _TM), _pick(DFF, _TN)

    def body(dy_ref, w_ref, a_ref, b_ref, da_ref, db_ref):
        dact = 0.5 * _dot_nt(dy_ref[...], w_ref[...])
        av, bv = a_ref[...].astype(F32), b_ref[...].astype(F32)
        s = _sigmoid(av)
        da_ref[...] = (dact * bv * (s * (1.0 + av * (1.0 - s)))).astype(da_ref.dtype)
        db_ref[...] = (dact * (av * s)).astype(db_ref.dtype)

    tile = pl.BlockSpec((tm, tn), lambda i, j: (i, j))
    return pl.pallas_call(
        body, name=name, grid=(M // tm, DFF // tn),
        in_specs=[pl.BlockSpec((tm, K), lambda i, j: (i, 0)), pl.BlockSpec((tn, K), lambda i, j: (j, 0)), tile, tile],
        out_specs=[tile, tile], out_shape=[jax.ShapeDtypeStruct((M, DFF), BF)] * 2,
        compiler_params=_cparams(2))(dy, w_out, a, b)


def ffn_bwd(dy, saved, g, w_in, w_out, tag):
    x, h, a, b, act = saved
    da, db = swiglu_bwd(dy, w_out, a, b, tag + "_dact")
    dw_out = matmul(act, dy, "tn", alpha=0.5, name=tag + "_dwout")
    dw_in = jnp.concatenate([matmul(h, da, "tn", name=tag + "_dwin_a"), matmul(h, db, "tn", name=tag + "_dwin_b")], axis=1)
    dx, dg = matmul_rms_bwd(da, w_in, x, g, dy, a2=db, name=tag + "_dh")
    return dx, dg, dw_in, dw_out


POOL_TS = 256
POOL_HALO = 16


def pool_fwd(P, w_pool, pool_scale, tag):
    ts = POOL_TS

    def fn(step, i, n, xa, halo, w, scale):
        xa = xa.astype(F32)
        halo = halo.astype(F32) * jnp.where(i > 0, 1.0, 0.0)
        ext = jnp.concatenate([halo, xa], axis=0)
        pos = (i * ts + 1 + _iota((ts, LANES), 0)).astype(F32)
        ds, ys = [], []
        for gi, win in enumerate(POOL_WINDOWS):
            e = ext[:, gi * LANES:(gi + 1) * LANES]
            sh = 1
            while sh < win:
                e = e + pltpu.roll(e, sh, 0)
                sh *= 2
            mean = e[POOL_HALO:] / jnp.minimum(pos, float(win))
            d = mean - xa[:, gi * LANES:(gi + 1) * LANES]
            ds.append(d)
            ys.append(_dot_nn(d, w[gi]))
        d = jnp.concatenate(ds, axis=1)
        return (d, jnp.concatenate(ys, axis=1) * scale), ()

    xa = Rows(P, POOL_W, PK_XA // POOL_W)
    xa_prev = Rows(P, POOL_W, PK_XA // POOL_W, "prev", POOL_HALO)
    (d, ya0), _ = rowwise(fn, [xa, xa_prev], [w_pool, pool_scale], [(POOL_W, BF), (POOL_W, BF)], ts=ts, name=tag + "_pool")
    return d, ya0


def pool_bwd(dya0, d, w_pool, pool_scale, tag):
    ts = POOL_TS
    L = ts + POOL_HALO

    def fn(step, i, n, dya0, dya0_next, d, w, scale):
        d = d.astype(F32)
        dz = jnp.concatenate([dya0.astype(F32), dya0_next.astype(F32) * jnp.where(i < n - 1, 1.0, 0.0)], axis=0) * scale
        pos = (i * ts + 1 + _iota((L, LANES), 0)).astype(F32)
        dxa, dws, dsc = [], [], []
        for gi, win in enumerate(POOL_WINDOWS):
            sl = slice(gi * LANES, (gi + 1) * LANES)
            dzg = dz[:, sl]
            dd = _dot_nt(dzg, w[gi])
            e = dd / jnp.minimum(pos, float(win))
            sh = 1
            while sh < win:
                e = e + pltpu.roll(e, L - sh, 0)
                sh *= 2
            dxa.append(e[:ts] - dd[:ts])
            dws.append(_dot_tn(d[:, sl], dzg[:ts])[None])
            z = _dot_nn(d[:, sl], w[gi])
            dsc.append(_colsum(dya0[:, sl].astype(F32) * z))
        return (jnp.concatenate(dxa, axis=1),), (jnp.concatenate(dws, axis=0), jnp.concatenate(dsc, axis=1))

    (dxa,), (dw_pool, dscale) = rowwise(
        fn, [dya0, Rows(dya0, halo="next", h=POOL_HALO), d], [w_pool, pool_scale],
        [(POOL_W, BF)], [((4, LANES, LANES), F32), ((1, POOL_W), F32)], ts=ts, name=tag + "_dpool")
    return dxa, dw_pool, dscale


LRU_R = 512


def _scan_fwd(A, U):
    R = A.shape[0]
    row = _iota(A.shape, 0)
    d = 1
    while d < R:
        m = row >= d
        A_sh = jnp.where(m, pltpu.roll(A, d, 0), 1.0)
        U_sh = jnp.where(m, pltpu.roll(U, d, 0), 0.0)
        U = A * U_sh + U
        A = A * A_sh
        d *= 2
    return A, U


def _scan_bwd(B, X):
    R = B.shape[0]
    row = _iota(B.shape, 0)
    d = 1
    while d < R:
        m = row < R - d
        B_sh = jnp.where(m, pltpu.roll(B, R - d, 0), 1.0)
        X_sh = jnp.where(m, pltpu.roll(X, R - d, 0), 0.0)
        X = X + B * X_sh
        B = B * B_sh
        d *= 2
    return X


def _lru_gates(xc, wa, ba, wx, bx, lam):
    r = _sigmoid(_dot_nn(xc, wa) + ba)
    ig = _sigmoid(_dot_nn(xc, wx) + bx)
    sp = _softplus(-lam)
    log_a = -LRU_C * r * sp
    a = jnp.exp(log_a)
    mult = jnp.sqrt(-_expm1(2.0 * log_a))
    return r, ig, sp, a, mult


def _lru_specs(S, R, reverse):
    nch = S // R
    ch = (lambda j: nch - 1 - j) if reverse else (lambda j: j)
    per = R // 8

    def col(off):
        return pl.BlockSpec((R, LANES), lambda h, j: (ch(j), off + h))

    def prev(off):
        return pl.BlockSpec((8, LANES), lambda h, j: (jnp.maximum(ch(j) * per - 1, 0), off + h))

    vec = pl.BlockSpec((1, LANES), lambda h, j: (0, h))
    cw = pl.BlockSpec((4, LANES), lambda h, j: (0, h))
    wsq = pl.BlockSpec((None, LANES, LANES), lambda h, j: (h, 0, 0))
    return nch, ch, col, prev, vec, cw, wsq


def lru_fwd(P, conv_w, conv_b, w_a, b_a, w_x, b_x, lam, tag):
    S = P.shape[0]
    R = min(LRU_R, S)
    nch, ch, col, prev, vec, cw_spec, wsq = _lru_specs(S, R, False)

    def body(xb_ref, halo_ref, gb_ref, cw_ref, cb_ref, wa_ref, ba_ref, wx_ref, bx_ref, lam_ref, xc_ref, h_ref, yb_ref, carry):
        j = pl.program_id(1)
        xb = xb_ref[...].astype(F32)
        halo = halo_ref[...].astype(F32) * jnp.where(j > 0, 1.0, 0.0)
        ext = jnp.concatenate([halo, xb], axis=0)
        cw = cw_ref[...]
        xc = cb_ref[...]
        for k in range(4):
            e = ext if k == 3 else pltpu.roll(ext, 3 - k, 0)
            xc = xc + e[8:] * cw[k:k + 1]
        r, ig, sp, a, mult = _lru_gates(xc, wa_ref[...], ba_ref[...], wx_ref[...], bx_ref[...], lam_ref[...])
        u = mult * (ig * xc)
        cum_a, hloc = _scan_fwd(a, u)

        @pl.when(j == 0)
        def _():
            carry[...] = jnp.zeros_like(carry)

        hfull = hloc + cum_a * carry[0:1]
        carry[...] = jnp.broadcast_to(hfull[R - 1:R], carry.shape)
        gel, _ = _gelu_and_grad(gb_ref[...].astype(F32))
        xc_ref[...] = xc
        h_ref[...] = hfull
        yb_ref[...] = (hfull * gel).astype(yb_ref.dtype)

    out = pl.BlockSpec((R, LANES), lambda h, j: (j, h))
    return pl.pallas_call(
        body, name=tag + "_lru", grid=(LRU_HEADS, nch),
        in_specs=[col(PK_XB // LANES), prev(PK_XB // LANES), col(PK_GB // LANES), cw_spec, vec, wsq, vec, wsq, vec, vec],
        out_specs=[out, out, out],
        out_shape=[jax.ShapeDtypeStruct((S, D), F32), jax.ShapeDtypeStruct((S, D), F32), jax.ShapeDtypeStruct((S, D), BF)],
        scratch_shapes=[pltpu.VMEM((8, LANES), F32)], compiler_params=_cparams(2),
    )(P, P, P, conv_w, conv_b, w_a, b_a, w_x, b_x, lam)


def lru_bwd(dyb0, P, hh, xc, conv_w, w_a, b_a, w_x, b_x, lam, tag):
    S = P.shape[0]
    R = min(LRU_R, S)
    nch, ch, col, prev, vec, cw_spec, wsq = _lru_specs(S, R, True)

    def body(dyb_ref, gb_ref, h_ref, hprev_ref, xc_ref, xb_ref, xbprev_ref, cw_ref, wa_ref, ba_ref, wx_ref, bx_ref, lam_ref,
             dgb_ref, dxb_ref, dwa_ref, dba_ref, dwx_ref, dbx_ref, dlam_ref, dcw_ref, dcb_ref, gcarry, dxc_head):
        j = pl.program_id(1)
        jj = nch - 1 - j
        has_prev = jnp.where(jj > 0, 1.0, 0.0)
        row = _iota((R, LANES), 0)
        xc, hv, lam = xc_ref[...], h_ref[...], lam_ref[...]
        wa, wx = wa_ref[...], wx_ref[...]
        r, ig, sp, a, mult = _lru_gates(xc, wa, ba_ref[...], wx, bx_ref[...], lam)
        dyb = dyb_ref[...].astype(F32)
        gel, dgel = _gelu_and_grad(gb_ref[...].astype(F32))
        dgb_ref[...] = (dyb * hv * dgel).astype(dgb_ref.dtype)

        @pl.when(j == 0)
        def _():
            gcarry[...] = jnp.zeros_like(gcarry)
            dxc_head[...] = jnp.zeros_like(dxc_head)
            for ref in (dwa_ref, dba_ref, dwx_ref, dbx_ref, dlam_ref, dcw_ref, dcb_ref):
                ref[...] = jnp.zeros_like(ref)

        B = jnp.where(row < R - 1, pltpu.roll(a, R - 1, 0), 0.0)
        X = dyb * gel + jnp.where(row == R - 1, gcarry[0:1], 0.0)
        G = _scan_bwd(B, X)
        gcarry[...] = jnp.broadcast_to(a[0:1] * G[0:1], gcarry.shape)
        hprev = jnp.where(row == 0, hprev_ref[...][7:8] * has_prev, pltpu.roll(hv, 1, 0))
        da = G * hprev
        dmult = G * ig * xc
        dig = G * mult * xc
        dxc = G * mult * ig
        dlog_a = da * a - dmult * (a * a) / mult
        dzr = dlog_a * (-LRU_C * sp) * (r * (1.0 - r))
        dzi = dig * (ig * (1.0 - ig))
        dxc = dxc + _dot_nt(dzr, wa) + _dot_nt(dzi, wx)
        dwa_ref[...] += _dot_tn(xc, dzr)
        dwx_ref[...] += _dot_tn(xc, dzi)
        dba_ref[...] += _colsum(dzr)
        dbx_ref[...] += _colsum(dzi)
        dlam_ref[...] += _colsum(dlog_a * (-LRU_C * r)) * (-_sigmoid(-lam))
        dcb_ref[...] += _colsum(dxc)
        cw = cw_ref[...]
        ext = jnp.concatenate([dxc, dxc_head[...]], axis=0)
        dxb = dxc * cw[3:4]
        for k in range(3):
            dxb = dxb + pltpu.roll(ext, R + 8 - (3 - k), 0)[:R] * cw[k:k + 1]
        dxb_ref[...] = dxb.astype(dxb_ref.dtype)
        dxc_head[...] = dxc[0:8]
        extx = jnp.concatenate([xbprev_ref[...].astype(F32) * has_prev, xb_ref[...].astype(F32)], axis=0)
        incs = []
        for k in range(4):
            e = extx if k == 3 else pltpu.roll(extx, 3 - k, 0)
            incs.append(_colsum(dxc * e[8:]))
        dcw_ref[...] += jnp.concatenate(incs, axis=0)

    plain = pl.BlockSpec((R, LANES), lambda h, j: (ch(j), h))
    plain_prev = pl.BlockSpec((8, LANES), lambda h, j: (jnp.maximum(ch(j) * (R // 8) - 1, 0), h))
    return pl.pallas_call(
        body, name=tag + "_dlru", grid=(LRU_HEADS, nch),
        in_specs=[plain, col(PK_GB // LANES), plain, plain_prev, plain, col(PK_XB // LANES), prev(PK_XB // LANES),
                  cw_spec, wsq, vec, wsq, vec, vec],
        out_specs=[plain, plain, wsq, vec, wsq, vec, vec, cw_spec, vec],
        out_shape=[jax.ShapeDtypeStruct((S, D), BF), jax.ShapeDtypeStruct((S, D), BF),
                   jax.ShapeDtypeStruct((LRU_HEADS, LANES, LANES), F32), jax.ShapeDtypeStruct((1, D), F32),
                   jax.ShapeDtypeStruct((LRU_HEADS, LANES, LANES), F32), jax.ShapeDtypeStruct((1, D), F32),
                   jax.ShapeDtypeStruct((1, D), F32), jax.ShapeDtypeStruct((4, D), F32), jax.ShapeDtypeStruct((1, D), F32)],
        scratch_shapes=[pltpu.VMEM((8, LANES), F32), pltpu.VMEM((8, LANES), F32)], compiler_params=_cparams(2),
    )(dyb0, P, hh, hh, xc, P, P, conv_w, w_a, b_a, w_x, b_x, lam)


CUM_TS = 512
FLASH_T = 512
FLASH_G = 8
NEG = -1e30


def cumlogf_fwd(fl, b_f, tag):
    ts = min(CUM_TS, fl.shape[0])

    def fn(step, i, n, fl, bf, carry):
        x = -_softplus(-(fl + bf))
        row = _iota(x.shape, 0)
        d = 1
        while d < ts:
            x = x + jnp.where(row >= d, pltpu.roll(x, d, 0), 0.0)
            d *= 2

        @pl.when(step == 0)
        def _():
            carry[...] = jnp.zeros_like(carry)

        c = x + carry[0:1]
        carry[...] = jnp.broadcast_to(c[ts - 1:ts], carry.shape)
        return (c,), ()

    return rowwise(fn, [fl], [b_f], [(LANES, F32)], scratch=[pltpu.VMEM((8, LANES), F32)], ts=ts, name=tag + "_cum")[0][0]


def cumlogf_bwd(dc, fl, b_f, tag):
    ts = min(CUM_TS, fl.shape[0])

    def fn(step, i, n, dc, fl, bf, carry):
        row = _iota(dc.shape, 0)
        x = dc
        d = 1
        while d < ts:
            x = x + jnp.where(row < ts - d, pltpu.roll(x, ts - d, 0), 0.0)
            d *= 2

        @pl.when(step == 0)
        def _():
            carry[...] = jnp.zeros_like(carry)

        g = x + carry[0:1]
        carry[...] = jnp.broadcast_to(g[0:1], carry.shape)
        dfl = g * _sigmoid(-(fl + bf))
        return (dfl,), (_colsum(dfl),)

    (dfl,), (db,) = rowwise(fn, [dc, fl], [b_f], [(LANES, F32)], [((1, LANES), F32)], scratch=[pltpu.VMEM((8, LANES), F32)],
                            ts=ts, name=tag + "_dcum", reverse=True)
    return dfl, db


FOX_KA = 80


def _split3(c):
    c = c.astype(F32)
    hi = lax.reduce_precision(c, 8, 7)
    r = c - hi
    mid = lax.reduce_precision(r, 8, 7)
    return hi.astype(BF), mid.astype(BF), (r - mid).astype(BF)


def fox_operands(q, k, ch):
    H, S, hd = q.shape
    qs = q * (FOX_HD ** -0.5)
    pieces = [p.astype(F32) for p in _split3(ch)]
    pad = FOX_KA - hd

    def unit(i):
        return (jnp.arange(FOX_KA) == hd + i).astype(F32)

    qa = jnp.pad(qs.transpose(0, 2, 1).astype(F32), ((0, 0), (0, pad), (0, 0)))
    ka = jnp.pad(k.astype(F32), ((0, 0), (0, 0), (0, pad)))
    for i, p in enumerate(pieces):
        qa = qa + p[:, None, :] * unit(i)[None, :, None] + unit(3 + i)[None, :, None]
        ka = ka + unit(i)[None, None, :] - p[:, :, None] * unit(3 + i)[None, None, :]
    return qa.astype(BF), ka.astype(BF), qs


def _causal(s):
    return jnp.where(_iota(s.shape, 0) <= _iota(s.shape, 1), s, NEG)


def flash_fwd(qa, ka, vt, tag):
    H, hd, S = vt.shape
    T = min(FLASH_T, S)
    nb = S // T

    pairs = [(qi, ki) for qi in range(nb) for ki in range(qi + 1)]
    q_tab = jnp.asarray([p[0] for p in pairs], jnp.int32)
    k_tab = jnp.asarray([p[1] for p in pairs], jnp.int32)

    def body(q_tab_ref, k_tab_ref, ka_ref, qa_ref, vt_ref, o_ref, lse_ref, m_s, l_s, acc):
        qi, ki = q_tab_ref[pl.program_id(1)], k_tab_ref[pl.program_id(1)]

        @pl.when(ki == 0)
        def _():
            m_s[...] = jnp.full_like(m_s, NEG)
            l_s[...] = jnp.zeros_like(l_s)
            acc[...] = jnp.zeros_like(acc)

        def step(diagonal):
            scores = []
            for g in range(FLASH_G):
                s = _dot_nn(ka_ref[g], qa_ref[g])
                scores.append(_causal(s) if diagonal else s)
            for g in range(FLASH_G):
                v = vt_ref[g]
                for half in range(2):
                    cols = slice(half * (T // 2), (half + 1) * (T // 2))
                    s = scores[g][:, cols]
                    m_old = m_s[g, :, cols]
                    m_new = jnp.maximum(m_old, jnp.max(s, axis=0, keepdims=True))
                    alpha = jnp.exp(m_old - m_new)
                    p = jnp.exp(s - m_new)
                    l_s[g, :, cols] = alpha * l_s[g, :, cols] + jnp.sum(p, axis=0, keepdims=True)
                    p_hi = p.astype(BF)
                    p_lo = p - p_hi.astype(F32)
                    acc[g, :, cols] = alpha * acc[g, :, cols] + (_dot_nn(v, p_hi) + _dot_nn(v, p_lo))
                    m_s[g, :, cols] = m_new

        @pl.when(ki < qi)
        def _():
            step(False)

        @pl.when(ki == qi)
        def _():
            step(True)
            o_ref[...] = acc[...] / l_s[...]
            lse_ref[...] = m_s[...] + jnp.log(l_s[...])

    G = FLASH_G
    grid_spec = pltpu.PrefetchScalarGridSpec(
        num_scalar_prefetch=2, grid=(H // G, len(pairs)),
        in_specs=[pl.BlockSpec((G, T, FOX_KA), lambda h, p, qt, kt: (h, kt[p], 0)),
                  pl.BlockSpec((G, FOX_KA, T), lambda h, p, qt, kt: (h, 0, qt[p])),
                  pl.BlockSpec((G, hd, T), lambda h, p, qt, kt: (h, 0, kt[p]))],
        out_specs=[pl.BlockSpec((G, hd, T), lambda h, p, qt, kt: (h, 0, qt[p])),
                   pl.BlockSpec((G, 1, T), lambda h, p, qt, kt: (h, 0, qt[p]))],
        scratch_shapes=[pltpu.VMEM((G, 1, T), F32), pltpu.VMEM((G, 1, T), F32), pltpu.VMEM((G, hd, T), F32)])
    return pl.pallas_call(
        body, name=tag + "_flash", grid_spec=grid_spec,
        out_shape=[jax.ShapeDtypeStruct((H, hd, S), F32), jax.ShapeDtypeStruct((H, 1, S), F32)],
        compiler_params=_cparams(2))(q_tab, k_tab, ka, qa, vt)


def flash_bwd(qa, ka, qs, kt, v, ot, do, dot_, lse, tag):
    H, S, hd = v.shape
    T = min(FLASH_T, S)
    nb = S // T

    pairs = [(qi, ki) for ki in range(nb) for qi in range(ki, nb)]
    q_tab = jnp.asarray([p[0] for p in pairs], jnp.int32)
    k_tab = jnp.asarray([p[1] for p in pairs], jnp.int32)

    def body(q_tab_ref, k_tab_ref, ka_ref, qa_ref, qs_ref, kt_ref, v_ref, ot_ref, do_ref, dot_ref, lse_ref,
             dqt_ref, dk_ref, dv_ref, dc_ref, dk_acc, dv_acc, dc_acc):
        qi, ki = q_tab_ref[pl.program_id(1)], k_tab_ref[pl.program_id(1)]

        @pl.when(pl.program_id(1) == 0)
        def _():
            dqt_ref[...] = jnp.zeros_like(dqt_ref)

        @pl.when(qi == ki)
        def _():
            dk_acc[...] = jnp.zeros_like(dk_acc)
            dv_acc[...] = jnp.zeros_like(dv_acc)
            dc_acc[...] = jnp.zeros_like(dc_acc)

        def step(diagonal):
            for g in range(FLASH_G):
                s = _dot_nn(ka_ref[g], qa_ref[g])
                if diagonal:
                    s = _causal(s)
                p = jnp.exp(s - lse_ref[g])
                dot_v = dot_ref[g]
                dp = _dot_nn(v_ref[g], dot_v)
                delta = jnp.sum(dot_v.astype(F32) * ot_ref[g], axis=0, keepdims=True)
                ds = p * (dp - delta)
                part = ds[:, 0:LANES]
                for j in range(1, T // LANES):
                    part = part + ds[:, j * LANES:(j + 1) * LANES]
                dc_acc[g] += part
                dsb = ds.astype(BF)
                dv_acc[g] += _dot_nn(p, do_ref[g])
                dk_acc[g] += _dot_nn(dsb, qs_ref[g])
                dqt_ref[g, qi] += _dot_nn(kt_ref[g], dsb)

        @pl.when(qi > ki)
        def _():
            step(False)

        @pl.when(qi == ki)
        def _():
            step(True)

        @pl.when(qi == nb - 1)
        def _():
            dk_ref[...] = dk_acc[...]
            dv_ref[...] = dv_acc[...]
            for g in range(FLASH_G):
                dc_ref[g] = -jnp.sum(dc_acc[g].T, axis=0, keepdims=True)

    qblk = lambda h, p, qt, kt_: (h, qt[p], 0)
    qblk_t = lambda h, p, qt, kt_: (h, 0, qt[p])
    kblk = lambda h, p, qt, kt_: (h, kt_[p], 0)
    kblk_t = lambda h, p, qt, kt_: (h, 0, kt_[p])
    G = FLASH_G
    grid_spec = pltpu.PrefetchScalarGridSpec(
        num_scalar_prefetch=2, grid=(H // G, len(pairs)),
        in_specs=[pl.BlockSpec((G, T, FOX_KA), kblk), pl.BlockSpec((G, FOX_KA, T), qblk_t),
                  pl.BlockSpec((G, T, hd), qblk), pl.BlockSpec((G, hd, T), kblk_t), pl.BlockSpec((G, T, hd), kblk),
                  pl.BlockSpec((G, hd, T), qblk_t), pl.BlockSpec((G, T, hd), qblk), pl.BlockSpec((G, hd, T), qblk_t),
                  pl.BlockSpec((G, 1, T), qblk_t)],
        out_specs=[pl.BlockSpec((G, nb, hd, T), lambda h, p, qt, kt_: (h, 0, 0, 0)), pl.BlockSpec((G, T, hd), kblk),
                   pl.BlockSpec((G, T, hd), kblk), pl.BlockSpec((G, 1, T), kblk_t)],
        scratch_shapes=[pltpu.VMEM((G, T, hd), F32), pltpu.VMEM((G, T, hd), F32), pltpu.VMEM((G, T, LANES), F32)])
    return pl.pallas_call(
        body, name=tag + "_dflash", grid_spec=grid_spec,
        out_shape=[jax.ShapeDtypeStruct((H, nb, hd, T), F32), jax.ShapeDtypeStruct((H, S, hd), F32),
                   jax.ShapeDtypeStruct((H, S, hd), F32), jax.ShapeDtypeStruct((H, 1, S), F32)],
        compiler_params=_cparams(2))(q_tab, k_tab, ka, qa, qs, kt, v, ot, do, dot_, lse)


def _to_heads(x2d):
    S = x2d.shape[0]
    return x2d.reshape(S, FOX_HEADS, FOX_HD).transpose(1, 0, 2)


def _from_heads(x3d):
    S = x3d.shape[1]
    return x3d.transpose(1, 0, 2).reshape(S, FOX_W)


def _gl_rows(P):
    return [Rows(P, D, PK_GL // D + k) for k in range(3)]


def merge_fwd(P, ya, yb, yc, b_gate, tag):
    def fn(step, i, n, g0, g1, g2, ya, yb, yc, b):
        out = 0.0
        for k, (gl, y) in enumerate(((g0, ya), (g1, yb), (g2, yc))):
            out = out + _sigmoid(gl.astype(F32) + b[:, k * D:(k + 1) * D]) * y.astype(F32)
        return (out,), ()
    return rowwise(fn, _gl_rows(P) + [ya, yb, yc], [b_gate], [(D, BF)], ts=512, name=tag + "_merge")[0][0]


def merge_bwd(dm, P, ya, yb, yc, b_gate, tag):
    def fn(step, i, n, dm, g0, g1, g2, ya, yb, yc, b):
        dm = dm.astype(F32)
        dys, dgls = [], []
        for k, (gl, y) in enumerate(((g0, ya), (g1, yb), (g2, yc))):
            g = _sigmoid(gl.astype(F32) + b[:, k * D:(k + 1) * D])
            dys.append(dm * g)
            dgls.append(dm * y.astype(F32) * (g * (1.0 - g)))
        dgl = jnp.concatenate(dgls, axis=1)
        return (dys[0], dys[1], dys[2], dgl), (_colsum(dgl),)
    (dya, dyb, dyc, dgl), (db,) = rowwise(
        fn, [dm] + _gl_rows(P) + [ya, yb, yc], [b_gate], [(D, BF), (D, BF), (D, BF), (3 * D, BF)], [((1, 3 * D), F32)],
        ts=256, name=tag + "_dmerge")
    return dya, dyb, dyc, dgl, db


def _xattn_softmax(qk):
    s = qk * (X_HD ** -0.5)
    e = jnp.exp(s - jnp.max(s, axis=1, keepdims=True))
    return e / jnp.sum(e, axis=1, keepdims=True)


def xattn_fwd(q, kv, tag):
    def fn(step, i, n, q, k, v):
        heads = [slice(h * X_HD, (h + 1) * X_HD) for h in range(X_HEADS)]
        scores = [_dot_nt(q[:, sl], k[:, sl]) for sl in heads]
        os = [_dot_nn(_xattn_softmax(s), v[:, sl]) for s, sl in zip(scores, heads)]
        return (jnp.concatenate(os, axis=1),), ()

    return _xattn_call(fn, [q], kv, [(D, BF)], [], tag + "_xattn")[0][0]


def _xattn_call(fn, rows, kv, outs, accs, name):
    S, ts, M = rows[0].shape[0], 512, kv.shape[0]
    ts = min(ts, S)
    nr, no, na = len(rows), len(outs), len(accs)

    def body(*refs):
        in_refs, out_refs, acc_refs = refs[:nr + 2], refs[nr + 2:nr + 2 + no], refs[nr + 2 + no:]
        step = pl.program_id(0)
        o, inc = fn(step, step, S // ts, *[r[...] for r in in_refs])
        for ref, val in zip(out_refs, o):
            ref[...] = val.astype(ref.dtype)
        if na:
            @pl.when(step == 0)
            def _():
                for ref in acc_refs:
                    ref[...] = jnp.zeros_like(ref)
            for ref, val in zip(acc_refs, inc):
                ref[...] += val

    row = pl.BlockSpec((ts, D), lambda s: (s, 0))
    res = pl.pallas_call(
        body, name=name, grid=(S // ts,),
        in_specs=[row] * nr + [pl.BlockSpec((M, D), lambda s: (0, 0)), pl.BlockSpec((M, D), lambda s: (0, 1))],
        out_specs=[row] * no + [pl.BlockSpec(tuple(s), lambda s_: (0, 0)) for s, _ in accs],
        out_shape=[jax.ShapeDtypeStruct((S, w), dt) for w, dt in outs] + [jax.ShapeDtypeStruct(tuple(s), dt) for s, dt in accs],
        compiler_params=_cparams(1))(*rows, kv, kv)
    return res[:no], res[no:]


def xattn_bwd(q, kv, do, tag):
    def fn(step, i, n, q, do, k, v):
        dqs, dks, dvs = [], [], []
        heads = [slice(h * X_HD, (h + 1) * X_HD) for h in range(X_HEADS)]
        scores = [_dot_nt(q[:, sl], k[:, sl]) for sl in heads]
        dps = [_dot_nt(do[:, sl], v[:, sl]) for sl in heads]
        for sl, qk, dp in zip(heads, scores, dps):
            p = _xattn_softmax(qk)
            ds = (p * (dp - jnp.sum(p * dp, axis=1, keepdims=True)) * (X_HD ** -0.5)).astype(BF)
            dqs.append(_dot_nn(ds, k[:, sl]))
            dks.append(_dot_tn(ds, q[:, sl]))
            dvs.append(_dot_tn(p, do[:, sl]))
        return (jnp.concatenate(dqs, axis=1),), (jnp.concatenate(dks + dvs, axis=1),)

    (dq,), (dkv,) = _xattn_call(fn, [q, do], kv, [(D, BF)], [((kv.shape[0], 2 * D), F32)], tag + "_dxattn")
    return dq, dkv


def layer_fwd(x, h_in, mem, w, next_g, tag):
    x1, u, s_ffn1 = ffn_fwd(x, h_in, w["ffn1_in"], w["ffn1_out"], w["g_mix"], tag + "_ffn1")
    P = matmul(u, w["wp"], "nn", name=tag + "_proj")
    fl = matmul(u, w["wfl"], "nn", out_dtype=F32, name=tag + "_projf")
    d, ya0 = pool_fwd(P, w["w_pool"], w["pool_scale"], tag)
    ya = matmul(ya0, w["up_a"], "nn", name=tag + "_upa")
    xc, hh, yb0 = lru_fwd(P, w["conv_w"], w["conv_b"], w["w_rg_a"], w["b_rg_a"], w["w_rg_x"], w["b_rg_x"], w["lam"], tag)
    yb = matmul(yb0, w["up_b"], "nn", name=tag + "_upb")
    c = cumlogf_fwd(fl, w["b_f"], tag)
    q, k, v = (_to_heads(P[:, o:o + FOX_W]) for o in (PK_Q, PK_K, PK_V))
    qa, ka, qs = fox_operands(q, k, c[:, :FOX_HEADS].T)
    ot, lse = flash_fwd(qa, ka, v.transpose(0, 2, 1), tag)
    o2 = ot.transpose(2, 0, 1).reshape(-1, FOX_W)
    yc = matmul(o2, w["up_c"], "nn", name=tag + "_upc")
    merged = merge_fwd(P, ya, yb, yc, w["b_gate"], tag)
    x2, hq = matmul(merged, w["w_o"], "nn", out_dtype=F32, res=x1, norm_g=w["g_cross"], name=tag + "_wo")
    qx = matmul(hq, w["xq"], "nn", name=tag + "_xq")
    mn = rms_fwd(mem, w["g_mem"], name=tag + "_mrms")
    kv = matmul(mn, w["xkv"], "nn", name=tag + "_xkv")
    ox = xattn_fwd(qx, kv, tag)
    x3, h2 = matmul(ox, w["xo"], "nn", out_dtype=F32, res=x2, norm_g=w["g_ffn2"], name=tag + "_xo")
    x4, h_out, s_ffn2 = ffn_fwd(x3, h2, w["ffn2_in"], w["ffn2_out"], next_g, tag + "_ffn2")
    saved = dict(ffn1=s_ffn1, ffn2=s_ffn2, x1=x1, u=u, P=P, fl=fl, d=d, ya0=ya0, ya=ya, xc=xc, hh=hh, yb0=yb0, yb=yb,
                 qa=qa, ka=ka, qs=qs, k=k, v=v, ot=ot, lse=lse, o2=o2, yc=yc, merged=merged, x2=x2, hq=hq, qx=qx,
                 mn=mn, kv=kv, ox=ox)
    return x4, h_out, saved


def layer_bwd(dx4, mem, w, s, tag):
    g = {}
    dx3, g["g_ffn2"], g["w_ffn2_in"], g["w_ffn2_out"] = ffn_bwd(dx4, s["ffn2"], w["g_ffn2"], w["ffn2_in"], w["ffn2_out"], tag + "_ffn2")
    dox = matmul(dx3, w["xo"], "nt", name=tag + "_dox")
    g["w_xo"] = matmul(s["ox"], dx3, "tn", name=tag + "_dwxo")
    dqx, dkv = xattn_bwd(s["qx"], s["kv"], dox, tag)
    g["w_xq"] = matmul(s["hq"], dqx, "tn", name=tag + "_dwxq")
    dx2, g["g_cross"] = matmul_rms_bwd(dqx, w["xq"], s["x2"], w["g_cross"], dx3, name=tag + "_dhq")
    g["w_xkv"] = matmul(s["mn"], dkv, "tn", name=tag + "_dwxkv")
    dmn = matmul(dkv, w["xkv"], "nt", name=tag + "_dmn")
    _, g["g_mem"] = rms_bwd(mem, w["g_mem"], dmn, None, name=tag + "_dmrms")
    P = s["P"]
    dmerged = matmul(dx2, w["w_o"], "nt", name=tag + "_dmerged")
    g["w_o"] = matmul(s["merged"], dx2, "tn", name=tag + "_dwo")
    dya, dyb, dyc, dgl, g["b_gate"] = merge_bwd(dmerged, P, s["ya"], s["yb"], s["yc"], w["b_gate"], tag)
    dya0 = matmul(dya, w["up_a"], "nt", name=tag + "_dya0")
    g["w_up_a"] = matmul(s["ya0"], dya, "tn", name=tag + "_dwupa")
    dxa, g["w_pool"], g["pool_scale"] = pool_bwd(dya0, s["d"], w["w_pool"], w["pool_scale"], tag)
    dyb0 = matmul(dyb, w["up_b"], "nt", name=tag + "_dyb0")
    g["w_up_b"] = matmul(s["yb0"], dyb, "tn", name=tag + "_dwupb")
    (dgb, dxb, g["w_rg_a"], g["b_rg_a"], g["w_rg_x"], g["b_rg_x"], g["lru_lambda"], g["conv_w"], g["conv_b"]) = lru_bwd(
        dyb0, P, s["hh"], s["xc"], w["conv_w"], w["w_rg_a"], w["b_rg_a"], w["w_rg_x"], w["b_rg_x"], w["lam"], tag)
    do2 = matmul(dyc, w["up_c"], "nt", name=tag + "_do2")
    g["w_up_c"] = matmul(s["o2"], dyc, "tn", name=tag + "_dwupc")
    do = _to_heads(do2)
    dqt, dk, dv, dck = flash_bwd(s["qa"], s["ka"], s["qs"], s["k"].transpose(0, 2, 1), s["v"], s["ot"], do,
                                 do.transpose(0, 2, 1), s["lse"], tag)
    dq = (dqt * (FOX_HD ** -0.5)).transpose(0, 1, 3, 2).reshape(dk.shape)
    dc = jnp.pad(dck[:, 0, :].T, ((0, 0), (0, LANES - FOX_HEADS)))
    dfl, g["b_f"] = cumlogf_bwd(dc, s["fl"], w["b_f"], tag)
    dP = jnp.concatenate([dxb, dgb, dgl, dxa] + [_from_heads(t).astype(BF) for t in (dq, dk, dv)], axis=1)
    du = matmul(dP, w["wp"], "nt", out_dtype=F32, name=tag + "_du")
    dx1, g["g_mix"] = matmul_rms_bwd(dfl, w["wfl"], s["x1"], w["g_mix"], dx2, res=du, name=tag + "_duf")
    g["wp"] = matmul(s["u"], dP, "tn", name=tag + "_dwp")
    g["wfl"] = matmul(s["u"], dfl, "tn", out_dtype=F32, name=tag + "_dwfl")
    dx0, g["g_ffn1"], g["w_ffn1_in"], g["w_ffn1_out"] = ffn_bwd(dx1, s["ffn1"], w["g_ffn1"], w["ffn1_in"], w["ffn1_out"], tag + "_ffn1")
    return dx0, g


def loss_head(x, target, g_final):
    def fn(step, i, n, x, t, g):
        r = _rstd(x)
        xhat = x * r
        e = xhat * g - t
        dy = e * (1.0 / D)
        gd = dy * g
        dx = r * (gd - xhat * jnp.mean(xhat * gd, axis=-1, keepdims=True))
        loss = 0.5 * jnp.sum(jnp.mean(e * e, axis=-1, keepdims=True), axis=0, keepdims=True)
        return (dx,), (jnp.broadcast_to(loss, (1, LANES)), _colsum(dy * xhat))

    (dx,), (loss, dg) = rowwise(fn, [x, target], [g_final], [(D, F32)], [((1, LANES), F32), ((1, D), F32)], ts=512, name="loss_head")
    return loss[0, 0], dx, dg


BIG = (("w_ffn1_in", (D, 2 * DFF), 1), ("w_ffn1_out", (DFF, D), 0), ("w_in", (D, 7176), 1), ("w_up_a", (POOL_W, D), 1),
       ("conv_w", (4, D), 1), ("w_up_b", (D, D), 0), ("w_up_c", (FOX_W, D), 1), ("w_o", (D, D), 0), ("w_xq", (D, D), 0),
       ("w_xkv", (D, 2 * D), 1), ("w_xo", (D, D), 0), ("w_ffn2_in", (D, 2 * DFF), 1), ("w_ffn2_out", (DFF, D), 0))
SMALL = (("g_ffn1", (D,)), ("g_mix", (D,)), ("b_f", (FOX_HEADS,)), ("b_gate", (3 * D,)), ("w_pool", (4, LANES, LANES)),
         ("pool_scale", (POOL_W,)), ("conv_b", (D,)), ("w_rg_a", (LRU_HEADS, LANES, LANES)), ("b_rg_a", (D,)),
         ("w_rg_x", (LRU_HEADS, LANES, LANES)), ("b_rg_x", (D,)), ("lru_lambda", (D,)), ("g_cross", (D,)), ("g_mem", (D,)),
         ("g_ffn2", (D,)))
ORDER = ("g_ffn1", "w_ffn1_in", "w_ffn1_out", "g_mix", "w_in", "b_f", "b_gate", "w_pool", "pool_scale", "w_up_a", "conv_w",
         "conv_b", "w_rg_a", "b_rg_a", "w_rg_x", "b_rg_x", "lru_lambda", "w_up_b", "w_up_c", "w_o", "g_cross", "g_mem", "w_xq",
         "w_xkv", "w_xo", "g_ffn2", "w_ffn2_in", "w_ffn2_out", "g_final")

IN_SPLIT = (("xa", 0, 512), ("xb", 512, 1024), ("gb", 1536, 1024), ("q", 2560, 512), ("k", 3072, 512), ("v", 3584, 512),
            ("fl", 4096, 8), ("gl", 4104, 3072))
PACK_ORDER = ("xb", "gb", "gl", "xa", "q", "k", "v")


def _shard_shape(shape, axis):
    s = list(shape)
    s[axis] //= N_CHIPS
    return (DEPTH, *s)


def _round_up(n, m):
    return -(-n // m) * m


SMALL_SEG_ROWS = 8


def _small_rows(n_elements):
    return _round_up(-(-n_elements // D), SMALL_SEG_ROWS)


def _pack_small(tree, names, row_multiple=SMALL_SEG_ROWS, fill=0.0):
    parts = []
    for n in names:
        flat = tree[n].astype(F32).reshape(-1)
        rows = _small_rows(flat.shape[0])
        parts.append(jnp.pad(flat, (0, rows * D - flat.shape[0]), constant_values=fill).reshape(rows, D))
    total = sum(p.shape[0] for p in parts)
    if total % row_multiple:
        parts.append(jnp.full((_round_up(total, row_multiple) - total, D), fill, F32))
    return jnp.concatenate(parts, axis=0)


def _unpack_small(buf, like, names):
    out, off = {}, 0
    for n in names:
        sh = like[n].shape
        rows = _small_rows(math.prod(sh))
        out[n] = buf[off:off + rows].reshape(-1)[:math.prod(sh)].reshape(sh)
        off += rows
    return out


def _from_chip_major(g, shape, axis):
    return jnp.moveaxis(g, 1, axis + 1).reshape(DEPTH, *shape)


def _to_chip_major(full, shape, axis):
    sh = list(shape)
    sh[axis:axis + 1] = [N_CHIPS, shape[axis] // N_CHIPS]
    return jnp.moveaxis(full.reshape(DEPTH, *sh), axis + 1, 1)


def _layer_weights(full, small, l):
    w_in = full["w_in"][l]
    cols = {n: w_in[:, o:o + s] for n, o, s in IN_SPLIT}
    w = dict(ffn1_in=full["w_ffn1_in"][l], ffn1_out=full["w_ffn1_out"][l], ffn2_in=full["w_ffn2_in"][l], ffn2_out=full["w_ffn2_out"][l],
             wp=jnp.concatenate([cols[n] for n in PACK_ORDER], axis=1),
             wfl=jnp.pad(cols["fl"], ((0, 0), (0, LANES - FOX_HEADS))),
             up_a=full["w_up_a"][l], up_b=full["w_up_b"][l], up_c=full["w_up_c"][l], w_o=full["w_o"][l], xq=full["w_xq"][l],
             xkv=full["w_xkv"][l], xo=full["w_xo"][l], conv_w=full["conv_w"][l])
    for n in ("g_ffn1", "g_mix", "b_gate", "pool_scale", "conv_b", "b_rg_a", "b_rg_x", "g_cross", "g_mem", "g_ffn2"):
        w[n] = small[n][l].reshape(1, -1)
    w["lam"] = small["lru_lambda"][l].reshape(1, -1)
    w["b_f"] = jnp.pad(small["b_f"][l], (0, LANES - FOX_HEADS)).reshape(1, LANES)
    for n in ("w_pool", "w_rg_a", "w_rg_x"):
        w[n] = small[n][l]
    return w


def _unpack_w_in_grad(gwp, gwfl):
    pk = {}
    off = 0
    sizes = {n: s for n, _, s in IN_SPLIT}
    for n in PACK_ORDER:
        pk[n] = gwp[:, off:off + sizes[n]]
        off += sizes[n]
    pk["fl"] = gwfl[:, :FOX_HEADS].astype(gwp.dtype)
    return jnp.concatenate([pk[n] for n, _, _ in IN_SPLIT], axis=1)


def _place():
    x, y, c = lax.axis_index("x"), lax.axis_index("y"), lax.axis_index("c")
    others = [(1 - x, y), (x, 1 - y), (1 - x, 1 - y)]
    return x, y, c, others


def _rcopy(src, dst, send_sems, recv_sems, k, to):
    return pltpu.make_async_remote_copy(src_ref=src, dst_ref=dst, send_sem=send_sems.at[k], recv_sem=recv_sems.at[k],
                                        device_id=to, device_id_type=MESH)


def _comm_call(body, name, ins, out_shape, n_sems):
    return pl.pallas_call(
        body, name=name, in_specs=[ANY] * len(ins), out_specs=[ANY] * len(out_shape), out_shape=out_shape,
        scratch_shapes=[pltpu.SemaphoreType.DMA((n_sems,)), pltpu.SemaphoreType.DMA((n_sems,)), pltpu.SemaphoreType.DMA((n_sems,))],
    )(*ins)


def allgather_same_core(bufs):
    nb = len(bufs)

    def body(*refs):
        ins, outs = refs[:nb], refs[nb:2 * nb]
        send_sems, recv_sems, local_sems = refs[2 * nb:]
        x, y, c, others = _place()
        me = 2 * x + y
        sends = []
        for b in range(nb):
            for j, (ox, oy) in enumerate(others):
                cp = _rcopy(ins[b], outs[b].at[me], send_sems, recv_sems, 3 * b + j, (ox, oy, c))
                cp.start()
                sends.append(cp)
        for b in range(nb):
            for j, (ox, oy) in enumerate(others):
                theirs = outs[b].at[2 * ox + oy]
                _rcopy(theirs, theirs, send_sems, recv_sems, 3 * b + j, (ox, oy, c)).wait_recv()
        for cp in sends:
            cp.wait_send()

    shapes = [jax.ShapeDtypeStruct((N_CHIPS, *b.shape), b.dtype) for b in bufs]
    outs = _comm_call(body, "allgather_same_core", list(bufs), shapes, 3 * nb)
    return [_with_own_block(o, b) for o, b in zip(outs, bufs)]


def _with_own_block(gathered, mine):
    me = 2 * lax.axis_index("x") + lax.axis_index("y")
    return lax.dynamic_update_index_in_dim(gathered, mine, me, axis=0)


def gather_weights(shards):
    nw = len(shards)

    def body(*refs):
        ins, outs = refs[:nw], refs[nw:2 * nw]
        send_sems, recv_sems, _ = refs[2 * nw:]
        x, y, c, _ = _place()
        fx, fy = 1 - c, c
        first = (x + fx - 2 * x * fx, y + fy - 2 * y * fy)
        second = (x + fy - 2 * x * fy, y + fx - 2 * y * fx)
        me, c1, c2, cd = 2 * x + y, 2 * first[0] + first[1], 2 * second[0] + second[1], 2 * (1 - x) + (1 - y)
        sibling = (x, y, 1 - c)
        started = []

        def send(w, k, src, dst, to):
            cp = _rcopy(src, dst, send_sems, recv_sems, 6 * w + k, to)
            cp.start()
            started.append(cp)

        def landed(w, k, block):
            _rcopy(block, block, send_sems, recv_sems, 6 * w + k, sibling).wait_recv()

        for w in range(nw):
            send(w, 0, ins[w].at[c], outs[w].at[c, me], (*first, c))
            send(w, 1, ins[w].at[c], outs[w].at[c, me], (*second, c))
        for w in range(nw):
            block = outs[w].at[c, c1]
            landed(w, 0, block)
            send(w, 2, block, block, (*second, c))
            send(w, 3, block, block, sibling)
        for w in range(nw):
            block = outs[w].at[c, c2]
            landed(w, 1, block)
            send(w, 4, block, block, sibling)
        for w in range(nw):
            block = outs[w].at[c, cd]
            landed(w, 2, block)
            send(w, 5, block, block, sibling)
        for w in range(nw):
            landed(w, 3, outs[w].at[1 - c, c2])
            landed(w, 4, outs[w].at[1 - c, c1])
            landed(w, 5, outs[w].at[1 - c, cd])
        for cp in started:
            cp.wait_send()

    shapes = [jax.ShapeDtypeStruct((DEPTH, N_CHIPS, *s.shape[1:]), s.dtype) for s in shards]
    outs = _comm_call(body, "gather_weights", list(shards), shapes, 6 * nw)
    me = 2 * lax.axis_index("x") + lax.axis_index("y")
    return [lax.dynamic_update_index_in_dim(o, s, me, axis=1) for o, s in zip(outs, shards)]


def sibling_layers(bufs):
    nb = len(bufs)

    def body(*refs):
        ins, outs = refs[:nb], refs[nb:2 * nb]
        send_sems, recv_sems, _ = refs[2 * nb:]
        x, y, c, _ = _place()
        work = []
        for b in range(nb):
            cp = _rcopy(ins[b].at[1 - c], outs[b], send_sems, recv_sems, b, (x, y, 1 - c))
            cp.start()
            work.append(cp)
        for cp in work:
            cp.wait()

    shapes = [jax.ShapeDtypeStruct(b.shape[1:], b.dtype) for b in bufs]
    return _comm_call(body, "sibling_layers", list(bufs), shapes, nb)


def chip_blocks(bufs):
    nb = len(bufs)

    def body(*refs):
        ins, outs = refs[:nb], refs[nb:2 * nb]
        send_sems, recv_sems, _ = refs[2 * nb:]
        x, y, c, others = _place()
        sends = []
        for b in range(nb):
            for j, (ox, oy) in enumerate(others):
                cp = _rcopy(ins[b].at[2 * ox + oy], outs[b].at[j], send_sems, recv_sems, 3 * b + j, (ox, oy, c))
                cp.start()
                sends.append(cp)
        for cp in sends:
            cp.wait()

    shapes = [jax.ShapeDtypeStruct((3, *b.shape[1:]), b.dtype) for b in bufs]
    return _comm_call(body, "chip_blocks", list(bufs), shapes, 3 * nb)


def sibling_other_layer(bufs):
    nb = len(bufs)

    def body(*refs):
        ins, outs = refs[:nb], refs[nb:2 * nb]
        send_sems, recv_sems, _ = refs[2 * nb:]
        x, y, c, _ = _place()
        work = []
        for b in range(nb):
            cp = _rcopy(ins[b], outs[b].at[c], send_sems, recv_sems, b, (x, y, 1 - c))
            cp.start()
            work.append(cp)
        for b, cp in enumerate(work):
            cp.wait_send()
            theirs = outs[b].at[1 - c]
            _rcopy(theirs, theirs, send_sems, recv_sems, b, (x, y, 1 - c)).wait_recv()

    shapes = [jax.ShapeDtypeStruct((DEPTH, *b.shape), b.dtype) for b in bufs]
    outs = _comm_call(body, "sibling_other_layer", list(bufs), shapes, nb)
    c = lax.axis_index("c")
    return [lax.dynamic_update_index_in_dim(o, b, c, axis=0) for o, b in zip(outs, bufs)]


ADD_BLOCK_BYTES = 2 * 2 ** 20


def add_arrays(parts, out_dtype, name):
    shape = parts[0].shape
    cols = shape[-1]
    rows = math.prod(shape[:-1])

    def fn(step, i, n, *vals):
        acc = vals[0].astype(F32)
        for v in vals[1:]:
            acc = acc + v.astype(F32)
        return (acc,), ()

    ts = rows
    for cand in (2048, 1024, 512, 256, 128, 64, 32, 16):
        if rows % cand == 0 and cand * cols * 4 <= ADD_BLOCK_BYTES:
            ts = cand
            break
    out = rowwise(fn, [p.reshape(rows, cols) for p in parts], outs=[(cols, out_dtype)], ts=ts, name=name)[0][0]
    return out.reshape(shape)


def reduce_gradients(bufs, names):
    c, me = lax.axis_index("c"), 2 * lax.axis_index("x") + lax.axis_index("y")
    theirs = sibling_layers(bufs)
    mine = [lax.dynamic_index_in_dim(b, c, axis=0, keepdims=False) for b in bufs]
    part = [add_arrays([m, t], m.dtype, "rs_add_sibling_" + n) for m, t, n in zip(mine, theirs, names)]
    recv = chip_blocks(part)
    own = [lax.dynamic_index_in_dim(p, me, axis=0, keepdims=False) for p in part]
    red = [add_arrays([o, r[0], r[1], r[2]], F32, "rs_add_chips_" + n) for o, r, n in zip(own, recv, names)]
    return sibling_other_layer(red)


_BC1 = 1.0 - ADAM_B1 ** ADAM_STEP
_BC2 = 1.0 - ADAM_B2 ** ADAM_STEP


def adamw(w, g, m, v, name):
    rows, cols = w.shape

    def fn(step, i, n, w, g, m, v):
        m2 = ADAM_B1 * m + (1.0 - ADAM_B1) * g
        v2 = ADAM_B2 * v + (1.0 - ADAM_B2) * (g * g)
        delta = -ADAM_LR * ((m2 / _BC1) / (jnp.sqrt(v2 / _BC2) + ADAM_EPS) + ADAM_WD * w)
        return (delta, m2, v2), ()

    ts = _pick(rows, (256, 128, 64, 32, 16, 8))
    return rowwise(fn, [w, g, m, v], outs=[(cols, F32)] * 3, ts=ts, name=name)[0]


def local_step(x, mem, target, full, small):
    ws = [_layer_weights(full, small, l) for l in range(DEPTH)]
    saved = []
    h, hn = x, rms_fwd(x, ws[0]["g_ffn1"], name="l_ffn1_rms")
    for l in range(DEPTH):
        h, hn, s = layer_fwd(h, hn, mem, ws[l], ws[l + 1]["g_ffn1"] if l + 1 < DEPTH else None, "l")
        saved.append(s)
    loss, dh, dg_final = loss_head(h, target, small["g_final"].reshape(1, D))
    grads = [None] * DEPTH
    for l in reversed(range(DEPTH)):
        dh, grads[l] = layer_bwd(dh, mem, ws[l], saved[l], "l")
    out = {}
    for l in range(DEPTH):
        g = grads[l]
        g["w_in"] = _unpack_w_in_grad(g.pop("wp"), g.pop("wfl"))
        g["b_f"] = g["b_f"][:, :FOX_HEADS]
    for name, shape, _ in BIG:
        out[name] = jnp.stack([grads[l][name].reshape(shape) for l in range(DEPTH)])
    for name, shape in SMALL:
        out[name] = jnp.stack([grads[l][name].reshape(shape) for l in range(DEPTH)])
    out["g_final"] = dg_final.reshape(D)
    return loss, dh, out


def kernel(x, mem, g_ffn1, w_ffn1_in, w_ffn1_out, g_mix, w_in, b_f, b_gate, w_pool, pool_scale, w_up_a, conv_w, conv_b, w_rg_a, b_rg_a, w_rg_x, b_rg_x, lru_lambda, w_up_b, w_up_c, w_o, g_cross, g_mem, w_xq, w_xkv, w_xo, g_ffn2, w_ffn2_in, w_ffn2_out, g_final, loss_target, m_g_ffn1, m_w_ffn1_in, m_w_ffn1_out, m_g_mix, m_w_in, m_b_f, m_b_gate, m_w_pool, m_pool_scale, m_w_up_a, m_conv_w, m_conv_b, m_w_rg_a, m_b_rg_a, m_w_rg_x, m_b_rg_x, m_lru_lambda, m_w_up_b, m_w_up_c, m_w_o, m_g_cross, m_g_mem, m_w_xq, m_w_xkv, m_w_xo, m_g_ffn2, m_w_ffn2_in, m_w_ffn2_out, m_g_final, v_g_ffn1, v_w_ffn1_in, v_w_ffn1_out, v_g_mix, v_w_in, v_b_f, v_b_gate, v_w_pool, v_pool_scale, v_w_up_a, v_conv_w, v_conv_b, v_w_rg_a, v_b_rg_a, v_w_rg_x, v_b_rg_x, v_lru_lambda, v_w_up_b, v_w_up_c, v_w_o, v_g_cross, v_g_mem, v_w_xq, v_w_xkv, v_w_xo, v_g_ffn2, v_w_ffn2_in, v_w_ffn2_out, v_g_final):
    args = dict(locals())
    weights = {n: args[n] for n in ORDER}
    m_in = {n: args["m_" + n] for n in ORDER}
    v_in = {n: args["v_" + n] for n in ORDER}
    big_names = [n for n, _, _ in BIG]
    small_names = [n for n, _ in SMALL] + ["g_final"]

    shards = []
    for n in big_names:
        shards += list(_split3(weights[n])) if n == "conv_w" else [weights[n].astype(BF)]
    gathered = iter(gather_weights(shards))
    full = {}
    for n, shape, axis in BIG:
        g = next(gathered)
        if n == "conv_w":
            g = g.astype(F32) + next(gathered).astype(F32) + next(gathered).astype(F32)
        full[n] = _from_chip_major(g, shape, axis)

    small = {n: weights[n] for n in small_names}
    loss, grad_x, contrib = local_step(x[0], mem[0], loss_target[0], full, small)
    loss = lax.psum(loss, ("x", "y", "c"))

    bufs = [_to_chip_major(contrib[n].astype(BF), shape, axis) for n, shape, axis in BIG]
    small_buf = _pack_small(contrib, small_names, DEPTH * N_CHIPS * SMALL_SEG_ROWS).reshape(DEPTH, N_CHIPS, -1, D)
    red = reduce_gradients(bufs + [small_buf], big_names + ["small"])

    grads = dict(zip(big_names, red[:-1]))
    red_small = allgather_same_core([red[-1]])[0].transpose(1, 0, 2, 3).reshape(-1, D)
    grads.update(_unpack_small(red_small, weights, small_names))

    delta, new_m, new_v = {}, {}, {}
    for name in big_names:
        sh = weights[name].shape
        two_d = (-1, sh[-1])
        d, m2, v2 = adamw(weights[name].reshape(two_d), grads[name].reshape(two_d), m_in[name].reshape(two_d),
                          v_in[name].reshape(two_d), "adamw_" + name)
        delta[name], new_m[name], new_v[name] = d.reshape(sh), m2.reshape(sh), v2.reshape(sh)
    outs = adamw(_pack_small(weights, small_names), _pack_small(grads, small_names), _pack_small(m_in, small_names),
                 _pack_small(v_in, small_names, fill=1.0), "adamw_small")
    for t, res in zip((delta, new_m, new_v), outs):
        t.update(_unpack_small(res, weights, small_names))

    return (loss, grad_x[None], *[grads[n] for n in ORDER], *[delta[n] for n in ORDER],
            *[new_m[n] for n in ORDER], *[new_v[n] for n in ORDER])
```
